```python
import jax, jax.numpy as jnp
from jax import lax
import numpy as np


D_MODEL = 1024
BATCH = 16
SEQ = 4096
DEPTH = 4

CTX_LEN = 256
GRID_W = 64
HEAD_DIM = 128
D_LRU = D_MODEL
LRU_BLOCK_W = 64
LRU_BLOCKS = D_LRU // LRU_BLOCK_W
LRU_C = 8.0
CONV_W = 4
CONV_LEFT = 2
N_HEADS_W = D_MODEL // HEAD_DIM
N_KV_W = N_HEADS_W // 4
WINDOW = 128
N_HEADS_G = D_MODEL // HEAD_DIM
N_KV_G = N_HEADS_G // 4
Q_BLOCK = 128
ROPE_THETA = 10000.0
EPS = 1e-6
NEG_INF = -1e30
N_BRANCH = 3
BRANCH_W = D_MODEL

QW_W, KVW_W = N_HEADS_W * HEAD_DIM, N_KV_W * HEAD_DIM
QW_G, KVW_G = N_HEADS_G * HEAD_DIM, N_KV_G * HEAD_DIM
IN_WIDTHS = (D_LRU, D_LRU,
             QW_W, KVW_W, KVW_W, QW_W,
             QW_G, KVW_G, KVW_G, QW_G,
             D_MODEL, D_MODEL, D_MODEL)
IN_WIDTH = sum(IN_WIDTHS)
SPLIT_POINTS = tuple(int(v) for v in np.cumsum(IN_WIDTHS)[:-1])

kernel_name = "hybrid_rglru_window_axial_prefix_block"


def rmsnorm(x, g):
    xf = x.astype(jnp.float32)
    y = xf * lax.rsqrt(jnp.mean(xf * xf, axis=-1, keepdims=True) + EPS)
    return y.astype(x.dtype) * g


def axial_rope_tables(S):
    rows = S // GRID_W
    row_id = jnp.repeat(jnp.arange(rows), GRID_W)
    col_id = jnp.tile(jnp.arange(GRID_W), rows)
    P = HEAD_DIM // 4
    inv = ROPE_THETA ** (-jnp.arange(P, dtype=jnp.float32) / P)
    ang = jnp.stack([row_id[:, None] * inv, col_id[:, None] * inv], axis=1)
    return jnp.cos(ang), jnp.sin(ang)


def rope_2d(x, cos, sin):
    B, T, H, hd = x.shape
    xs = x.reshape(B, T, H, 2, 2, hd // 4)
    x0, x1 = xs[..., 0, :], xs[..., 1, :]
    c, s = cos[None, :, None], sin[None, :, None]
    return jnp.stack([x0 * c - x1 * s, x1 * c + x0 * s], axis=-2).reshape(B, T, H, hd).astype(x.dtype)


def dwconv(u, w, b):
    T = u.shape[1]
    up = jnp.pad(u, ((0, 0), (CONV_LEFT, CONV_W - 1 - CONV_LEFT), (0, 0)))
    return sum(up[:, k:k + T] * w[k] for k in range(CONV_W)) + b


def lru_coeffs(u, wa, ba, wx, bx, lam):
    ub = u.reshape(*u.shape[:-1], LRU_BLOCKS, LRU_BLOCK_W)
    r = jax.nn.sigmoid(jnp.einsum('btnd,nde->btne', ub, wa).reshape(u.shape) + ba)
    i = jax.nn.sigmoid(jnp.einsum('btnd,nde->btne', ub, wx).reshape(u.shape) + bx)
    log_a = -LRU_C * r.astype(jnp.float32) * jax.nn.softplus(-lam.astype(jnp.float32))
    a = jnp.exp(log_a)
    b = jnp.sqrt(-jnp.expm1(2.0 * log_a)) * (i * u).astype(jnp.float32)
    return a, b


def _combine(left, right):
    a1, b1 = left
    a2, b2 = right
    return a1 * a2, a2 * b1 + b2


def linear_scan(a, b, reverse):
    return lax.associative_scan(_combine, (a, b), reverse=reverse, axis=1)[1]


def rglru_branch(u, uc, conv_w, conv_b, wa, ba, wx, bx, lam):
    u = dwconv(u, conv_w, conv_b)
    uc = dwconv(uc, conv_w, conv_b)
    y, yc = 0.0, 0.0
    for d, rev in enumerate((False, True)):
        ac, bc = lru_coeffs(uc, wa[d], ba[d], wx[d], bx[d], lam[d])
        hc = linear_scan(ac, bc, rev)
        a, b = lru_coeffs(u, wa[d], ba[d], wx[d], bx[d], lam[d])
        if rev:
            b = b.at[:, -1].add(a[:, -1] * hc[:, 0])
        else:
            b = b.at[:, 0].add(a[:, 0] * hc[:, -1])
        y = y + linear_scan(a, b, rev)
        yc = yc + hc
    return y.astype(u.dtype), yc.astype(u.dtype)


def attend(q, kv_sets, sink):
    scale = q.shape[-1] ** -0.5
    logits = []
    for k, v, mask in kv_sets:
        s = jnp.einsum('bqkgd,bjkd->bkgqj', q, k).astype(jnp.float32) * scale
        if mask is not None:
            s = jnp.where(mask, s, NEG_INF)
        logits.append(s)
    if sink is not None:
        B, Q, KVH, G, _ = q.shape
        logits.append(jnp.broadcast_to(sink.astype(jnp.float32).reshape(1, KVH, G, 1, 1), (B, KVH, G, Q, 1)))
    p = jax.nn.softmax(jnp.concatenate(logits, axis=-1), axis=-1)
    out, off = 0.0, 0
    for k, v, _ in kv_sets:
        J = k.shape[1]
        out = out + jnp.einsum('bkgqj,bjkd->bqkgd', p[..., off:off + J].astype(v.dtype), v)
        off += J
    return out


def latent_attention(q, k, v, kc, vc, sink, window):
    B, S, H, hd = q.shape
    KVH = k.shape[2]
    nblk = S // Q_BLOCK
    qb = q.reshape(B, nblk, Q_BLOCK, KVH, H // KVH, hd).swapaxes(0, 1)
    if window is None:
        out = lax.map(lambda qn: attend(qn, [(k, v, None), (kc, vc, None)], sink), qb)
    else:
        span = Q_BLOCK + 2 * window
        pad = ((0, 0), (window, window), (0, 0), (0, 0))
        kp, vp = jnp.pad(k, pad), jnp.pad(v, pad)

        def body(args):
            n, qn = args
            start = n * Q_BLOCK
            kn = lax.dynamic_slice_in_dim(kp, start, span, axis=1)
            vn = lax.dynamic_slice_in_dim(vp, start, span, axis=1)
            qpos = start + jnp.arange(Q_BLOCK)
            kpos = start - window + jnp.arange(span)
            mask = (jnp.abs(kpos[None, :] - qpos[:, None]) <= window) & (kpos >= 0)[None, :] & (kpos < S)[None, :]
            return attend(qn, [(kn, vn, mask), (kc, vc, None)], sink)

        out = lax.map(body, (jnp.arange(nblk), qb))
    return out.swapaxes(0, 1).reshape(B, S, H * hd)


def context_attention(q, k, v, sink):
    B, L, H, hd = q.shape
    KVH = k.shape[2]
    o = attend(q.reshape(B, L, KVH, H // KVH, hd), [(k, v, None)], sink)
    return o.reshape(B, L, H * hd)


def heads(t, n):
    return t.reshape(*t.shape[:-1], n, HEAD_DIM)


def merge_branches(ys, gate_paths, merge_logits, w_branch, w_out):
    out = 0.0
    for n in range(N_BRANCH):
        out = out + jax.nn.sigmoid(merge_logits[n]) * ((ys[n] * jax.nn.silu(gate_paths[n])) @ w_branch[n])
    return out @ w_out


def _fwd_setup_inputs(seed: int = 0) -> dict:
    key = jax.random.key(seed)
    ks = jax.random.split(key, 24)
    nrm = lambda k, shape, s: jax.random.normal(k, shape, jnp.float32) * s
    x = nrm(ks[0], (BATCH, SEQ, D_MODEL), 1.0)
    c = nrm(ks[1], (BATCH, D_MODEL), 1.0)
    ctx = nrm(ks[2], (BATCH, CTX_LEN, D_MODEL), 1.0)
    c_ctx = nrm(ks[3], (D_MODEL,), 1.0)
    norm_g = 1.0 + nrm(ks[4], (DEPTH, D_MODEL), 0.02)
    w_mod = nrm(ks[5], (DEPTH, D_MODEL, 3 * D_MODEL), 0.5 * D_MODEL ** -0.5)
    b_mod = nrm(ks[6], (DEPTH, 3 * D_MODEL), 0.02)
    w_in = nrm(ks[7], (DEPTH, D_MODEL, IN_WIDTH), D_MODEL ** -0.5)
    conv_w = nrm(ks[8], (DEPTH, CONV_W, D_LRU), CONV_W ** -0.5)
    conv_b = nrm(ks[9], (DEPTH, D_LRU), 0.02)
    lru_wa = nrm(ks[10], (DEPTH, 2, LRU_BLOCKS, LRU_BLOCK_W, LRU_BLOCK_W), LRU_BLOCK_W ** -0.5)
    lru_ba = nrm(ks[11], (DEPTH, 2, D_LRU), 0.1)
    lru_wx = nrm(ks[12], (DEPTH, 2, LRU_BLOCKS, LRU_BLOCK_W, LRU_BLOCK_W), LRU_BLOCK_W ** -0.5)
    lru_bx = nrm(ks[13], (DEPTH, 2, D_LRU), 0.1)
    u = jax.random.uniform(ks[14], (DEPTH, 2, D_LRU), jnp.float32, 0.9, 0.999)
    a = u ** (1.0 / LRU_C)
    lru_lambda = jnp.log(a) - jnp.log1p(-a)
    attn_sink = nrm(ks[15], (DEPTH, N_HEADS_W), 0.5)
    q_norm_g = 1.0 + nrm(ks[16], (DEPTH, HEAD_DIM), 0.02)
    k_norm_g = 1.0 + nrm(ks[17], (DEPTH, HEAD_DIM), 0.02)
    w_branch = nrm(ks[18], (DEPTH, N_BRANCH, BRANCH_W, D_MODEL), BRANCH_W ** -0.5)
    w_out = nrm(ks[19], (DEPTH, D_MODEL, D_MODEL), D_MODEL ** -0.5)
    final_g = 1.0 + nrm(ks[20], (D_MODEL,), 0.02)
    return {"x": x, "c": c, "ctx": ctx, "c_ctx": c_ctx, "norm_g": norm_g, "w_mod": w_mod, "b_mod": b_mod,
            "w_in": w_in, "conv_w": conv_w, "conv_b": conv_b, "lru_wa": lru_wa, "lru_ba": lru_ba,
            "lru_wx": lru_wx, "lru_bx": lru_bx, "lru_lambda": lru_lambda, "attn_sink": attn_sink,
            "q_norm_g": q_norm_g, "k_norm_g": k_norm_g, "w_branch": w_branch, "w_out": w_out,
            "final_g": final_g}


def _fwd_reference(x, c, ctx, c_ctx, norm_g, w_mod, b_mod, w_in, conv_w, conv_b, lru_wa, lru_ba, lru_wx, lru_bx,
              lru_lambda, attn_sink, q_norm_g, k_norm_g, w_branch, w_out, final_g):
    S = x.shape[1]
    cos, sin = axial_rope_tables(S)
    sc, scc = jax.nn.silu(c), jax.nn.silu(c_ctx)
    xc = ctx
    for l in range(DEPTH):
        last = l == DEPTH - 1
        shift, scale, gate = jnp.split(sc @ w_mod[l] + b_mod[l], 3, axis=-1)
        shift_c, scale_c, gate_c = jnp.split(scc @ w_mod[l] + b_mod[l], 3, axis=-1)
        h = rmsnorm(x, norm_g[l]) * (1.0 + scale[:, None]) + shift[:, None]
        hc = rmsnorm(xc, norm_g[l]) * (1.0 + scale_c) + shift_c
        w_parts = jnp.split(w_in[l], SPLIT_POINTS, axis=1)
        uA, gA, qB, kB, vB, gB, qC, kC, vC, gC, mA, mB, mC = [h @ w for w in w_parts]
        uAc, gAc, qBc, kBc, vBc, gBc, qCc, kCc, vCc, gCc, mAc, mBc, mCc = [hc @ w for w in w_parts]
        yA, yAc = rglru_branch(uA, uAc, conv_w[l], conv_b[l], lru_wa[l], lru_ba[l], lru_wx[l], lru_bx[l], lru_lambda[l])
        kBc_h, vBc_h = heads(kBc, N_KV_W), heads(vBc, N_KV_W)
        yB = latent_attention(rope_2d(heads(qB, N_HEADS_W), cos, sin), rope_2d(heads(kB, N_KV_W), cos, sin),
                              heads(vB, N_KV_W), kBc_h, vBc_h, attn_sink[l], WINDOW)
        kCc_h, vCc_h = rmsnorm(heads(kCc, N_KV_G), k_norm_g[l]), heads(vCc, N_KV_G)
        yC = latent_attention(rope_2d(rmsnorm(heads(qC, N_HEADS_G), q_norm_g[l]), cos, sin),
                              rope_2d(rmsnorm(heads(kC, N_KV_G), k_norm_g[l]), cos, sin),
                              heads(vC, N_KV_G), kCc_h, vCc_h, None, None)
        y = merge_branches((yA, yB, yC), (gA, gB, gC), (mA, mB, mC), w_branch[l], w_out[l])
        if not last:
            yBc = context_attention(heads(qBc, N_HEADS_W), kBc_h, vBc_h, attn_sink[l])
            yCc = context_attention(rmsnorm(heads(qCc, N_HEADS_G), q_norm_g[l]), kCc_h, vCc_h, None)
            yc = merge_branches((yAc, yBc, yCc), (gAc, gBc, gCc), (mAc, mBc, mCc), w_branch[l], w_out[l])
            xc = xc + gate_c * yc
        x = x + gate[:, None] * y
    return rmsnorm(x, final_g)


import jax as _jax
import jax.numpy as _jnp

TWIN_FORMAT = 'train_step'
FWD_PARAMS = ['x', 'c', 'ctx', 'c_ctx', 'norm_g', 'w_mod', 'b_mod', 'w_in', 'conv_w', 'conv_b', 'lru_wa', 'lru_ba', 'lru_wx', 'lru_bx', 'lru_lambda', 'attn_sink', 'q_norm_g', 'k_norm_g', 'w_branch', 'w_out', 'final_g']
TWIN_WEIGHTS = ['c_ctx', 'norm_g', 'w_mod', 'b_mod', 'w_in', 'conv_w', 'conv_b', 'lru_wa', 'lru_ba', 'lru_wx', 'lru_bx', 'lru_lambda', 'attn_sink', 'q_norm_g', 'k_norm_g', 'w_branch', 'w_out', 'final_g']
TWIN_DIFF_INPUT = 'x'
TWIN_INPUTS = ['x', 'c', 'ctx', 'c_ctx', 'norm_g', 'w_mod', 'b_mod', 'w_in', 'conv_w', 'conv_b', 'lru_wa', 'lru_ba', 'lru_wx', 'lru_bx', 'lru_lambda', 'attn_sink', 'q_norm_g', 'k_norm_g', 'w_branch', 'w_out', 'final_g', 'loss_target', 'm_c_ctx', 'm_norm_g', 'm_w_mod', 'm_b_mod', 'm_w_in', 'm_conv_w', 'm_conv_b', 'm_lru_wa', 'm_lru_ba', 'm_lru_wx', 'm_lru_bx', 'm_lru_lambda', 'm_attn_sink', 'm_q_norm_g', 'm_k_norm_g', 'm_w_branch', 'm_w_out', 'm_final_g', 'v_c_ctx', 'v_norm_g', 'v_w_mod', 'v_b_mod', 'v_w_in', 'v_conv_w', 'v_conv_b', 'v_lru_wa', 'v_lru_ba', 'v_lru_wx', 'v_lru_bx', 'v_lru_lambda', 'v_attn_sink', 'v_q_norm_g', 'v_k_norm_g', 'v_w_branch', 'v_w_out', 'v_final_g']
TWIN_OUTPUTS = ['loss', 'grad_x', 'grad_c_ctx', 'grad_norm_g', 'grad_w_mod', 'grad_b_mod', 'grad_w_in', 'grad_conv_w', 'grad_conv_b', 'grad_lru_wa', 'grad_lru_ba', 'grad_lru_wx', 'grad_lru_bx', 'grad_lru_lambda', 'grad_attn_sink', 'grad_q_norm_g', 'grad_k_norm_g', 'grad_w_branch', 'grad_w_out', 'grad_final_g', 'delta_c_ctx', 'delta_norm_g', 'delta_w_mod', 'delta_b_mod', 'delta_w_in', 'delta_conv_w', 'delta_conv_b', 'delta_lru_wa', 'delta_lru_ba', 'delta_lru_wx', 'delta_lru_bx', 'delta_lru_lambda', 'delta_attn_sink', 'delta_q_norm_g', 'delta_k_norm_g', 'delta_w_branch', 'delta_w_out', 'delta_final_g', 'new_m_c_ctx', 'new_m_norm_g', 'new_m_w_mod', 'new_m_b_mod', 'new_m_w_in', 'new_m_conv_w', 'new_m_conv_b', 'new_m_lru_wa', 'new_m_lru_ba', 'new_m_lru_wx', 'new_m_lru_bx', 'new_m_lru_lambda', 'new_m_attn_sink', 'new_m_q_norm_g', 'new_m_k_norm_g', 'new_m_w_branch', 'new_m_w_out', 'new_m_final_g', 'new_v_c_ctx', 'new_v_norm_g', 'new_v_w_mod', 'new_v_b_mod', 'new_v_w_in', 'new_v_conv_w', 'new_v_conv_b', 'new_v_lru_wa', 'new_v_lru_ba', 'new_v_lru_wx', 'new_v_lru_bx', 'new_v_lru_lambda', 'new_v_attn_sink', 'new_v_q_norm_g', 'new_v_k_norm_g', 'new_v_w_branch', 'new_v_w_out', 'new_v_final_g']
TWIN_LEAF_KINDS = {'loss': 'loss', 'grad_x': 'grad_x', 'grad_c_ctx': 'grad_w', 'grad_norm_g': 'grad_w', 'grad_w_mod': 'grad_w', 'grad_b_mod': 'grad_w', 'grad_w_in': 'grad_w', 'grad_conv_w': 'grad_w', 'grad_conv_b': 'grad_w', 'grad_lru_wa': 'grad_w', 'grad_lru_ba': 'grad_w', 'grad_lru_wx': 'grad_w', 'grad_lru_bx': 'grad_w', 'grad_lru_lambda': 'grad_w', 'grad_attn_sink': 'grad_w', 'grad_q_norm_g': 'grad_w', 'grad_k_norm_g': 'grad_w', 'grad_w_branch': 'grad_w', 'grad_w_out': 'grad_w', 'grad_final_g': 'grad_w', 'delta_c_ctx': 'delta_w', 'delta_norm_g': 'delta_w', 'delta_w_mod': 'delta_w', 'delta_b_mod': 'delta_w', 'delta_w_in': 'delta_w', 'delta_conv_w': 'delta_w', 'delta_conv_b': 'delta_w', 'delta_lru_wa': 'delta_w', 'delta_lru_ba': 'delta_w', 'delta_lru_wx': 'delta_w', 'delta_lru_bx': 'delta_w', 'delta_lru_lambda': 'delta_w', 'delta_attn_sink': 'delta_w', 'delta_q_norm_g': 'delta_w', 'delta_k_norm_g': 'delta_w', 'delta_w_branch': 'delta_w', 'delta_w_out': 'delta_w', 'delta_final_g': 'delta_w', 'new_m_c_ctx': 'new_m', 'new_m_norm_g': 'new_m', 'new_m_w_mod': 'new_m', 'new_m_b_mod': 'new_m', 'new_m_w_in': 'new_m', 'new_m_conv_w': 'new_m', 'new_m_conv_b': 'new_m', 'new_m_lru_wa': 'new_m', 'new_m_lru_ba': 'new_m', 'new_m_lru_wx': 'new_m', 'new_m_lru_bx': 'new_m', 'new_m_lru_lambda': 'new_m', 'new_m_attn_sink': 'new_m', 'new_m_q_norm_g': 'new_m', 'new_m_k_norm_g': 'new_m', 'new_m_w_branch': 'new_m', 'new_m_w_out': 'new_m', 'new_m_final_g': 'new_m', 'new_v_c_ctx': 'new_v', 'new_v_norm_g': 'new_v', 'new_v_w_mod': 'new_v', 'new_v_b_mod': 'new_v', 'new_v_w_in': 'new_v', 'new_v_conv_w': 'new_v', 'new_v_conv_b': 'new_v', 'new_v_lru_wa': 'new_v', 'new_v_lru_ba': 'new_v', 'new_v_lru_wx': 'new_v', 'new_v_lru_bx': 'new_v', 'new_v_lru_lambda': 'new_v', 'new_v_attn_sink': 'new_v', 'new_v_q_norm_g': 'new_v', 'new_v_k_norm_g': 'new_v', 'new_v_w_branch': 'new_v', 'new_v_w_out': 'new_v', 'new_v_final_g': 'new_v'}


def _forward(args):
    return _fwd_reference(*[args[k] for k in FWD_PARAMS])


def _output_shape():
    out = _jax.eval_shape(lambda: _forward(_fwd_setup_inputs(0)))
    return out.shape, out.dtype

N_MICROBATCH = 1
ADAM_LR = 0.001
ADAM_B1 = 0.9
ADAM_B2 = 0.999
ADAM_EPS = 1e-08
ADAM_WD = 0.01
ADAM_STEP = 10
PER_EXAMPLE_BATCH_AXIS = {'x': 0, 'c': 0, 'ctx': 0, 'loss_target': 0}
SHARED_INPUTS = []
_WEIGHT_DTYPES = {'c_ctx': _jnp.float32, 'norm_g': _jnp.float32, 'w_mod': _jnp.float32, 'b_mod': _jnp.float32, 'w_in': _jnp.float32, 'conv_w': _jnp.float32, 'conv_b': _jnp.float32, 'lru_wa': _jnp.float32, 'lru_ba': _jnp.float32, 'lru_wx': _jnp.float32, 'lru_bx': _jnp.float32, 'lru_lambda': _jnp.float32, 'attn_sink': _jnp.float32, 'q_norm_g': _jnp.float32, 'k_norm_g': _jnp.float32, 'w_branch': _jnp.float32, 'w_out': _jnp.float32, 'final_g': _jnp.float32}
MOMENT_SCALE = {'c_ctx': 6.825165e-02, 'norm_g': 1.665402e-01, 'w_mod': 2.635093e-01, 'b_mod': 3.824804e-01, 'w_in': 6.801801e-02, 'conv_w': 1.492863e-01, 'conv_b': 3.508683e-01, 'lru_wa': 8.439905e-03, 'lru_ba': 1.401220e-02, 'lru_wx': 1.729382e-02, 'lru_bx': 3.246294e-02, 'lru_lambda': 3.748087e-02, 'attn_sink': 1.391693e-04, 'q_norm_g': 6.403298e-03, 'k_norm_g': 6.268890e-03, 'w_branch': 8.809324e-02, 'w_out': 1.537765e-01, 'final_g': 6.470542e+01}


def _to_microbatches(a, axis):
    t = _jnp.moveaxis(a, axis, 0)
    t = t.reshape((N_MICROBATCH, t.shape[0] // N_MICROBATCH) + t.shape[1:])
    return _jnp.moveaxis(t, 1, axis + 1)


def setup_inputs(seed: int = 0) -> dict:
    inp = _fwd_setup_inputs(seed)
    key = _jax.random.fold_in(_jax.random.key(seed), 7919)
    shape, _ = _output_shape()
    out = dict(inp)
    out["loss_target"] = _jax.random.normal(_jax.random.fold_in(key, 0), shape, _jnp.float32)
    for i, name in enumerate(TWIN_WEIGHTS):
        w = inp[name].astype(_jnp.float32)
        if MOMENT_SCALE is None:
            s = _jnp.sqrt(_jnp.mean(_jnp.square(w)) + 1e-30)
        else:
            s = MOMENT_SCALE[name]
        km, kv = _jax.random.split(_jax.random.fold_in(key, i + 1))
        out[name] = w
        out["m_" + name] = s * _jax.random.normal(km, w.shape, _jnp.float32)
        out["v_" + name] = (s * s) * _jax.random.uniform(kv, w.shape, _jnp.float32, 0.5, 1.5)
    if N_MICROBATCH > 1:
        for name, axis in PER_EXAMPLE_BATCH_AXIS.items():
            out[name] = _to_microbatches(out[name], axis)
    return {'x': out['x'], 'c': out['c'], 'ctx': out['ctx'], 'c_ctx': out['c_ctx'], 'norm_g': out['norm_g'], 'w_mod': out['w_mod'], 'b_mod': out['b_mod'], 'w_in': out['w_in'], 'conv_w': out['conv_w'], 'conv_b': out['conv_b'], 'lru_wa': out['lru_wa'], 'lru_ba': out['lru_ba'], 'lru_wx': out['lru_wx'], 'lru_bx': out['lru_bx'], 'lru_lambda': out['lru_lambda'], 'attn_sink': out['attn_sink'], 'q_norm_g': out['q_norm_g'], 'k_norm_g': out['k_norm_g'], 'w_branch': out['w_branch'], 'w_out': out['w_out'], 'final_g': out['final_g'], 'loss_target': out['loss_target'], 'm_c_ctx': out['m_c_ctx'], 'm_norm_g': out['m_norm_g'], 'm_w_mod': out['m_w_mod'], 'm_b_mod': out['m_b_mod'], 'm_w_in': out['m_w_in'], 'm_conv_w': out['m_conv_w'], 'm_conv_b': out['m_conv_b'], 'm_lru_wa': out['m_lru_wa'], 'm_lru_ba': out['m_lru_ba'], 'm_lru_wx': out['m_lru_wx'], 'm_lru_bx': out['m_lru_bx'], 'm_lru_lambda': out['m_lru_lambda'], 'm_attn_sink': out['m_attn_sink'], 'm_q_norm_g': out['m_q_norm_g'], 'm_k_norm_g': out['m_k_norm_g'], 'm_w_branch': out['m_w_branch'], 'm_w_out': out['m_w_out'], 'm_final_g': out['m_final_g'], 'v_c_ctx': out['v_c_ctx'], 'v_norm_g': out['v_norm_g'], 'v_w_mod': out['v_w_mod'], 'v_b_mod': out['v_b_mod'], 'v_w_in': out['v_w_in'], 'v_conv_w': out['v_conv_w'], 'v_conv_b': out['v_conv_b'], 'v_lru_wa': out['v_lru_wa'], 'v_lru_ba': out['v_lru_ba'], 'v_lru_wx': out['v_lru_wx'], 'v_lru_bx': out['v_lru_bx'], 'v_lru_lambda': out['v_lru_lambda'], 'v_attn_sink': out['v_attn_sink'], 'v_q_norm_g': out['v_q_norm_g'], 'v_k_norm_g': out['v_k_norm_g'], 'v_w_branch': out['v_w_branch'], 'v_w_out': out['v_w_out'], 'v_final_g': out['v_final_g']}


def _loss(weights, diff, rest, loss_target):
    with _jax.named_scope("forward"):
        args = {**rest, TWIN_DIFF_INPUT: diff, **{k: w.astype(_WEIGHT_DTYPES[k]) for k, w in weights.items()}}
        y = _forward(args)
    with _jax.named_scope("loss_head"):
        err = _jnp.square(y.astype(_jnp.float32) - loss_target)
        return 0.5 * _jnp.sum(_jnp.mean(err, axis=-1)) if err.ndim else 0.5 * err


def _adamw(w, g, m, v):
    m = ADAM_B1 * m + (1.0 - ADAM_B1) * g
    v = ADAM_B2 * v + (1.0 - ADAM_B2) * _jnp.square(g)
    m_hat = m / (1.0 - ADAM_B1 ** ADAM_STEP)
    v_hat = v / (1.0 - ADAM_B2 ** ADAM_STEP)
    delta = -ADAM_LR * (m_hat / (_jnp.sqrt(v_hat) + ADAM_EPS) + ADAM_WD * w)
    return delta, m, v


def reference(x, c, ctx, c_ctx, norm_g, w_mod, b_mod, w_in, conv_w, conv_b, lru_wa, lru_ba, lru_wx, lru_bx, lru_lambda, attn_sink, q_norm_g, k_norm_g, w_branch, w_out, final_g, loss_target, m_c_ctx, m_norm_g, m_w_mod, m_b_mod, m_w_in, m_conv_w, m_conv_b, m_lru_wa, m_lru_ba, m_lru_wx, m_lru_bx, m_lru_lambda, m_attn_sink, m_q_norm_g, m_k_norm_g, m_w_branch, m_w_out, m_final_g, v_c_ctx, v_norm_g, v_w_mod, v_b_mod, v_w_in, v_conv_w, v_conv_b, v_lru_wa, v_lru_ba, v_lru_wx, v_lru_bx, v_lru_lambda, v_attn_sink, v_q_norm_g, v_k_norm_g, v_w_branch, v_w_out, v_final_g):
    given = dict(x=x, c=c, ctx=ctx, c_ctx=c_ctx, norm_g=norm_g, w_mod=w_mod, b_mod=b_mod, w_in=w_in, conv_w=conv_w, conv_b=conv_b, lru_wa=lru_wa, lru_ba=lru_ba, lru_wx=lru_wx, lru_bx=lru_bx, lru_lambda=lru_lambda, attn_sink=attn_sink, q_norm_g=q_norm_g, k_norm_g=k_norm_g, w_branch=w_branch, w_out=w_out, final_g=final_g, loss_target=loss_target, m_c_ctx=m_c_ctx, m_norm_g=m_norm_g, m_w_mod=m_w_mod, m_b_mod=m_b_mod, m_w_in=m_w_in, m_conv_w=m_conv_w, m_conv_b=m_conv_b, m_lru_wa=m_lru_wa, m_lru_ba=m_lru_ba, m_lru_wx=m_lru_wx, m_lru_bx=m_lru_bx, m_lru_lambda=m_lru_lambda, m_attn_sink=m_attn_sink, m_q_norm_g=m_q_norm_g, m_k_norm_g=m_k_norm_g, m_w_branch=m_w_branch, m_w_out=m_w_out, m_final_g=m_final_g, v_c_ctx=v_c_ctx, v_norm_g=v_norm_g, v_w_mod=v_w_mod, v_b_mod=v_b_mod, v_w_in=v_w_in, v_conv_w=v_conv_w, v_conv_b=v_conv_b, v_lru_wa=v_lru_wa, v_lru_ba=v_lru_ba, v_lru_wx=v_lru_wx, v_lru_bx=v_lru_bx, v_lru_lambda=v_lru_lambda, v_attn_sink=v_attn_sink, v_q_norm_g=v_q_norm_g, v_k_norm_g=v_k_norm_g, v_w_branch=v_w_branch, v_w_out=v_w_out, v_final_g=v_final_g)
    weights = {n: given[n] for n in TWIN_WEIGHTS}
    shared = {n: given[n] for n in SHARED_INPUTS}
    per_example = {n: given[n] for n in ['x', 'c', 'ctx']}
    grad_fn = _jax.value_and_grad(_loss, argnums=(0, 1))

    def one_microbatch(ex, loss_target):
        ex = dict(ex)
        diff = ex.pop(TWIN_DIFF_INPUT)
        return grad_fn(weights, diff, {**shared, **ex}, loss_target)

    if N_MICROBATCH == 1:
        loss, (grad_w, grad_x) = one_microbatch(per_example, given["loss_target"])
    else:
        def body(carry, xs):
            loss_sum, grad_sum = carry
            l_k, (gw_k, gx_k) = one_microbatch(xs[0], xs[1])
            with _jax.named_scope("update"):
                return (loss_sum + l_k, _jax.tree.map(_jnp.add, grad_sum, gw_k)), gx_k

        init = (_jnp.zeros((), _jnp.float32), _jax.tree.map(_jnp.zeros_like, weights))
        (loss, grad_w), grad_x = _jax.lax.scan(body, init, (per_example, given["loss_target"]))
    with _jax.named_scope("update"):
        delta_w, new_m, new_v = {}, {}, {}
        for n in TWIN_WEIGHTS:
            delta_w[n], new_m[n], new_v[n] = _adamw(weights[n], grad_w[n], given["m_" + n], given["v_" + n])
    return (loss, grad_x, *[grad_w[n] for n in TWIN_WEIGHTS], *[delta_w[n] for n in TWIN_WEIGHTS],
            *[new_m[n] for n in TWIN_WEIGHTS], *[new_v[n] for n in TWIN_WEIGHTS])
```

```python
import functools

import jax
import jax.numpy as jnp
import numpy as np
from jax import lax
from jax.experimental import pallas as pl
from jax.experimental.pallas import tpu as pltpu

F32 = jnp.float32
BF16 = jnp.bfloat16

HEAD_DIM = 128
GROUP = 4
LRU_BLOCK_W = 64
LRU_C = 8.0
WINDOW = 128
GRID_W = 64
ROPE_THETA = 10000.0
EPS = 1e-6
NEG_INF = -1e30
ADAM_LR, ADAM_B1, ADAM_B2, ADAM_EPS, ADAM_WD, ADAM_STEP = 0.001, 0.9, 0.999, 1e-08, 0.01, 10

ROW_BLOCK = 256
LRU_LANES = 128
LRU_CHUNK = 128
ATT_BQ = 256
ATT_BK = 256
WIN_BQ = 128
WIN_SPAN = WIN_BQ + 2 * WINDOW
MOD_ROWS = 16
VMEM_LIMIT = 56 * 1024 * 1024

MESH = pl.DeviceIdType.MESH


def _cparams(sem=None, vmem=None):
    kw = {}
    if sem is not None:
        kw["dimension_semantics"] = sem
    if vmem is not None:
        kw["vmem_limit_bytes"] = vmem
    return pltpu.CompilerParams(**kw)


def _sigmoid(v):
    return 1.0 / (1.0 + jnp.exp(-v))


def _silu(v):
    return v * _sigmoid(v)


def _dsilu(v):
    s = _sigmoid(v)
    return s * (1.0 + v * (1.0 - s))


def _expm1(x):
    poly = x * (1.0 + x * (0.5 + x * (1.0 / 6 + x * (1.0 / 24 + x * (1.0 / 120 + x * (1.0 / 720 + x * (1.0 / 5040)))))))
    return jnp.where(jnp.abs(x) < 0.25, poly, jnp.exp(x) - 1.0)


def _log1p(y):
    u = 1.0 + y
    d = u - 1.0
    return jnp.where(d == 0.0, y, jnp.log(u) * (y / jnp.where(d == 0.0, 1.0, d)))


def _softplus(x):
    return jnp.maximum(x, 0.0) + _log1p(jnp.exp(-jnp.abs(x)))


def _dot(a, b):
    return jnp.dot(a, b, preferred_element_type=F32)


def _dot_nt(a, b):
    return lax.dot_general(a, b, (((1,), (1,)), ((), ())), preferred_element_type=F32)


def _dot_tn(a, b):
    return lax.dot_general(a, b, (((0,), (0,)), ((), ())), preferred_element_type=F32)


def _colsum(v):
    return jnp.sum(v, axis=0, keepdims=True)


def _layout(d_model):
    kvw = (d_model // HEAD_DIM // GROUP) * HEAD_DIM
    names = ["gA", "gB", "gC", "mA", "mB", "mC", "uA", "qB", "qC", "kB", "vB", "kC", "vC"]
    widths = [d_model] * 9 + [kvw] * 4
    off, o = {}, 0
    for n, w in zip(names, widths):
        off[n] = o
        o += w
    return off, o


def _orig_segments(d_model):
    kvw = (d_model // HEAD_DIM // GROUP) * HEAD_DIM
    names = ["uA", "gA", "qB", "kB", "vB", "gB", "qC", "kC", "vC", "gC", "mA", "mB", "mC"]
    widths = [d_model, d_model, d_model, kvw, kvw, d_model, d_model, kvw, kvw, d_model, d_model, d_model, d_model]
    out, o = [], 0
    for n, w in zip(names, widths):
        out.append((n, o, w))
        o += w
    return out


def _matmul(a, b, *, ta=False, tb=False, out_dtype=F32, bm, bn, bk, name):
    (kdim, m) = a.shape if ta else a.shape[::-1]
    (n, kdim2) = b.shape if tb else b.shape[::-1]
    assert kdim == kdim2 and m % bm == 0 and n % bn == 0 and kdim % bk == 0, (a.shape, b.shape, bm, bn, bk)
    nk = kdim // bk
    dims = (((0 if ta else 1,), (1 if tb else 0,)), ((), ()))

    def body(a_ref, b_ref, o_ref, *scratch):
        r = lax.dot_general(a_ref[...].astype(BF16), b_ref[...].astype(BF16), dims, preferred_element_type=F32)
        if nk == 1:
            o_ref[...] = r.astype(out_dtype)
        else:
            acc = scratch[0]
            k = pl.program_id(2)

            @pl.when(k == 0)
            def _():
                acc[...] = r

            @pl.when(k > 0)
            def _():
                acc[...] += r

            @pl.when(k == nk - 1)
            def _():
                o_ref[...] = acc[...].astype(out_dtype)

    a_spec = pl.BlockSpec((bk, bm), lambda i, j, k: (k, i)) if ta else pl.BlockSpec((bm, bk), lambda i, j, k: (i, k))
    b_spec = pl.BlockSpec((bn, bk), lambda i, j, k: (j, k)) if tb else pl.BlockSpec((bk, bn), lambda i, j, k: (k, j))
    return pl.pallas_call(
        body, name=name, grid=(m // bm, n // bn, nk),
        in_specs=[a_spec, b_spec], out_specs=pl.BlockSpec((bm, bn), lambda i, j, k: (i, j)),
        out_shape=jax.ShapeDtypeStruct((m, n), out_dtype),
        scratch_shapes=[pltpu.VMEM((bm, bn), F32)] if nk > 1 else [],
        compiler_params=_cparams(("parallel", "parallel", "arbitrary"), VMEM_LIMIT),
    )(a, b)


def _mod_fwd(c16, w_mod, b_mod):
    d3 = w_mod.shape[1]

    def body(c_ref, w_ref, b_ref, o_ref):
        o_ref[...] = _dot(_silu(c_ref[...]).astype(BF16), w_ref[...]) + b_ref[...]

    return pl.pallas_call(body, name="mod_fwd", out_shape=jax.ShapeDtypeStruct((MOD_ROWS, d3), F32),
                          compiler_params=_cparams(None, VMEM_LIMIT))(c16, w_mod, b_mod)


def _mod_bwd(c16, dmod16, w_mod):
    d, d3 = w_mod.shape

    def body(c_ref, g_ref, w_ref, dw_ref, db_ref, dc_ref):
        c = c_ref[...]
        g = g_ref[...]
        gb = g.astype(BF16)
        dw_ref[...] = _dot_tn(_silu(c).astype(BF16), gb)
        db_ref[...] = _colsum(g)
        dc_ref[...] = _dot_nt(gb, w_ref[...]) * _dsilu(c)

    return pl.pallas_call(
        body, name="mod_bwd",
        out_shape=(jax.ShapeDtypeStruct((d, d3), F32), jax.ShapeDtypeStruct((1, d3), F32),
                   jax.ShapeDtypeStruct((MOD_ROWS, d), F32)),
        compiler_params=_cparams(None, VMEM_LIMIT))(c16, dmod16, w_mod)


def _row_kind(t, lb):
    return jnp.where(t >= lb, 1, 0)


def _norm_mod_fwd(x3, g, modsel, ctx_len):
    b, t, d = x3.shape
    bt = ROW_BLOCK
    lb = ctx_len // bt

    def body(x_ref, g_ref, m_ref, h_ref):
        x = x_ref[0]
        rstd = lax.rsqrt(jnp.mean(x * x, axis=-1, keepdims=True) + EPS)
        y = x * rstd * g_ref[...]
        h_ref[0] = (y * (1.0 + m_ref[0, 0, 1:2, :]) + m_ref[0, 0, 0:1, :]).astype(BF16)

    return pl.pallas_call(
        body, name="norm_mod_fwd", grid=(b, t // bt),
        in_specs=[pl.BlockSpec((1, bt, d), lambda i, j: (i, j, 0)),
                  pl.BlockSpec((1, d), lambda i, j: (0, 0)),
                  pl.BlockSpec((1, 1, 8, d), lambda i, j: (i, _row_kind(j, lb), 0, 0))],
        out_specs=pl.BlockSpec((1, bt, d), lambda i, j: (i, j, 0)),
        out_shape=jax.ShapeDtypeStruct((b, t, d), BF16),
        compiler_params=_cparams(("parallel", "arbitrary")),
    )(x3, g, modsel)


def _norm_mod_bwd(dh3, x3, g, modsel, dres3, ctx_len):
    b, t, d = x3.shape
    bt = ROW_BLOCK
    lb = ctx_len // bt

    def body(dh_ref, x_ref, g_ref, m_ref, dres_ref, dx_ref, acc_ref):
        j = pl.program_id(1)
        x = x_ref[0]
        dh = dh_ref[0]
        g_row = g_ref[...]
        rstd = lax.rsqrt(jnp.mean(x * x, axis=-1, keepdims=True) + EPS)
        xhat = x * rstd
        dhpre = dh * (1.0 + m_ref[0, 0, 1:2, :])
        dxhat = dhpre * g_row
        dx = rstd * (dxhat - xhat * jnp.mean(dxhat * xhat, axis=-1, keepdims=True))
        dx_ref[0] = dx + dres_ref[0]

        @pl.when((j == 0) | (j == lb))
        def _():
            acc_ref[...] = jnp.zeros_like(acc_ref)

        acc_ref[0, 0, 0:1, :] += _colsum(dh)
        acc_ref[0, 0, 1:2, :] += _colsum(dh * (xhat * g_row))
        acc_ref[0, 0, 2:3, :] += _colsum(dhpre * xhat)

    blk = pl.BlockSpec((1, bt, d), lambda i, j: (i, j, 0))
    return pl.pallas_call(
        body, name="norm_mod_bwd", grid=(b, t // bt),
        in_specs=[blk, blk, pl.BlockSpec((1, d), lambda i, j: (0, 0)),
                  pl.BlockSpec((1, 1, 8, d), lambda i, j: (i, _row_kind(j, lb), 0, 0)), blk],
        out_specs=(blk, pl.BlockSpec((1, 1, 8, d), lambda i, j: (i, _row_kind(j, lb), 0, 0))),
        out_shape=(jax.ShapeDtypeStruct((b, t, d), F32), jax.ShapeDtypeStruct((b, 2, 8, d), F32)),
        compiler_params=_cparams(("parallel", "arbitrary")),
    )(dh3, x3, g, modsel, dres3)


def _shifted_rows(ref, c, off, ctx_len, total):
    ct = LRU_CHUNK
    r0 = pl.multiple_of(c * ct, ct)
    x0 = ref[pl.ds(r0, ct), :]
    row = lax.broadcasted_iota(jnp.int32, x0.shape, 0)
    if off < 0:
        k = -off
        has = jnp.logical_and(r0 != 0, r0 != ctx_len)
        p0 = pl.multiple_of(jnp.maximum(r0 - 8, 0), 8)
        edge = jnp.where(has, ref[pl.ds(p0, 8), :], 0.0)
        out = pltpu.roll(x0, k, 0)
        for j in range(k):
            out = jnp.where(row == j, edge[8 - k + j:8 - k + j + 1, :], out)
    else:
        k = off
        has = jnp.logical_and(r0 + ct != ctx_len, r0 + ct != total)
        n0 = pl.multiple_of(jnp.minimum(r0 + ct, total - 8), 8)
        edge = jnp.where(has, ref[pl.ds(n0, 8), :], 0.0)
        out = pltpu.roll(x0, ct - k, 0)
        for j in range(k):
            out = jnp.where(row == ct - k + j, edge[j:j + 1, :], out)
    return out


def _chunk_scan(a, b, reverse):
    n = a.shape[0]
    row = lax.broadcasted_iota(jnp.int32, a.shape, 0)
    s = 1
    while s < n:
        if reverse:
            a_s, b_s, ok = pltpu.roll(a, n - s, 0), pltpu.roll(b, n - s, 0), row < n - s
        else:
            a_s, b_s, ok = pltpu.roll(a, s, 0), pltpu.roll(b, s, 0), row >= s
        b = jnp.where(ok, a * b_s + b, b)
        a = jnp.where(ok, a * a_s, a)
        s *= 2
    return a, b


def _lru_order(d, s, n_ctx, n_all):
    if d == 0:
        return s
    return jnp.where(s < n_ctx, n_ctx - 1 - s, n_all - 1 - (s - n_ctx))


def _lru_gates(u, wa, ba, wx, bx, sp):
    ub = u.astype(BF16)
    r = _sigmoid(_dot(ub, wa) + ba)
    i = _sigmoid(_dot(ub, wx) + bx)
    log_a = (-LRU_C * sp) * r
    a = jnp.exp(log_a)
    sf = jnp.sqrt(-_expm1(2.0 * log_a))
    return ub, r, i, a, sf


def _lru_specs(t, n_lane_blocks_offset):
    ln = LRU_LANES
    return [
        pl.BlockSpec((4, ln), lambda i, j: (0, j)),
        pl.BlockSpec((1, ln), lambda i, j: (0, j)),
        pl.BlockSpec((2, 1, ln, ln), lambda i, j: (0, j, 0, 0)),
        pl.BlockSpec((2, ln), lambda i, j: (0, j)),
        pl.BlockSpec((2, 1, ln, ln), lambda i, j: (0, j, 0, 0)),
        pl.BlockSpec((2, ln), lambda i, j: (0, j)),
        pl.BlockSpec((2, ln), lambda i, j: (0, j)),
    ]


def _lru_conv(ua_ref, cw_ref, cb_ref, u_s, ctx_len, total):
    ct = LRU_CHUNK

    def conv(c, _):
        r0 = pl.multiple_of(c * ct, ct)
        u = (cw_ref[0:1, :] * _shifted_rows(ua_ref, c, -2, ctx_len, total)
             + cw_ref[1:2, :] * _shifted_rows(ua_ref, c, -1, ctx_len, total)
             + cw_ref[2:3, :] * ua_ref[pl.ds(r0, ct), :]
             + cw_ref[3:4, :] * _shifted_rows(ua_ref, c, 1, ctx_len, total) + cb_ref[...])
        u_s[pl.ds(r0, ct), :] = u
        return 0

    lax.fori_loop(0, total // ct, conv, 0)


def _lru_fwd(proj3, col0, conv_w, conv_b, wa_bd, ba, wx_bd, bx, lam, ctx_len):
    b, t, _ = proj3.shape
    d = conv_w.shape[1]
    ln, ct = LRU_LANES, LRU_CHUNK
    n_all, n_ctx = t // ct, ctx_len // ct
    cb0 = col0 // ln

    def body(ua_ref, cw_ref, cb_ref, wa_ref, ba_ref, wx_ref, bx_ref, lam_ref, y_ref, u_s):
        ua = ua_ref.at[0]
        _lru_conv(ua, cw_ref, cb_ref, u_s, ctx_len, t)
        for dr in (0, 1):
            sp = _softplus(-lam_ref[dr:dr + 1, :])
            wa, wx = wa_ref[dr, 0], wx_ref[dr, 0]
            ba_row, bx_row = ba_ref[dr:dr + 1, :], bx_ref[dr:dr + 1, :]

            def step(s, carry, dr=dr, sp=sp, wa=wa, wx=wx, ba_row=ba_row, bx_row=bx_row):
                c = _lru_order(dr, s, n_ctx, n_all)
                r0 = pl.multiple_of(c * ct, ct)
                u = u_s[pl.ds(r0, ct), :]
                _, _, i, a, sf = _lru_gates(u, wa, ba_row, wx, bx_row, sp)
                aa, h0 = _chunk_scan(a, sf * (i * u), reverse=(dr == 1))
                h = h0 + aa * carry
                if dr == 0:
                    y_ref[0, pl.ds(r0, ct), :] = h
                    return h[ct - 1:ct, :]
                y_ref[0, pl.ds(r0, ct), :] += h
                return h[0:1, :]

            lax.fori_loop(0, n_all, step, jnp.zeros((1, ln), F32))

    return pl.pallas_call(
        body, name="lru_fwd", grid=(b, d // ln),
        in_specs=[pl.BlockSpec((1, t, ln), lambda i, j: (i, 0, cb0 + j))] + _lru_specs(t, cb0),
        out_specs=pl.BlockSpec((1, t, ln), lambda i, j: (i, 0, j)),
        out_shape=jax.ShapeDtypeStruct((b, t, d), F32),
        scratch_shapes=[pltpu.VMEM((t, ln), F32)],
        compiler_params=_cparams(("parallel", "parallel"), VMEM_LIMIT),
    )(proj3, conv_w, conv_b, wa_bd, ba, wx_bd, bx, lam)


def _lru_bwd(dproj3, proj3, col0, dy3, conv_w, conv_b, wa_bd, ba, wx_bd, bx, lam, ctx_len):
    b, t, _ = proj3.shape
    d = conv_w.shape[1]
    ln, ct = LRU_LANES, LRU_CHUNK
    n_all, n_ctx = t // ct, ctx_len // ct
    cb0 = col0 // ln

    def body(dproj_hbm, ua_ref, dy_ref, cw_ref, cb_ref, wa_ref, ba_ref, wx_ref, bx_ref, lam_ref,
             dua_ref, vec_ref, dwa_ref, dwx_ref, u_s, h_s, du_s):
        del dproj_hbm
        ua = ua_ref.at[0]
        _lru_conv(ua, cw_ref, cb_ref, u_s, ctx_len, t)
        du_s[...] = jnp.zeros_like(du_s)
        vec_ref[...] = jnp.zeros_like(vec_ref)
        for dr in (0, 1):
            sp = _softplus(-lam_ref[dr:dr + 1, :])
            wa, wx = wa_ref[dr, 0], wx_ref[dr, 0]
            ba_row, bx_row = ba_ref[dr:dr + 1, :], bx_ref[dr:dr + 1, :]

            def fwd(s, carry, dr=dr, sp=sp, wa=wa, wx=wx, ba_row=ba_row, bx_row=bx_row):
                c = _lru_order(dr, s, n_ctx, n_all)
                r0 = pl.multiple_of(c * ct, ct)
                u = u_s[pl.ds(r0, ct), :]
                _, _, i, a, sf = _lru_gates(u, wa, ba_row, wx, bx_row, sp)
                aa, h0 = _chunk_scan(a, sf * (i * u), reverse=(dr == 1))
                h = h0 + aa * carry
                h_s[pl.ds(r0, ct), :] = h
                return h[ct - 1:ct, :] if dr == 0 else h[0:1, :]

            lax.fori_loop(0, n_all, fwd, jnp.zeros((1, ln), F32))

            def bwd(sr, carry, dr=dr, sp=sp, wa=wa, wx=wx, ba_row=ba_row, bx_row=bx_row):
                gc, dwa_acc, dwx_acc, vacc = carry
                c = _lru_order(dr, n_all - 1 - sr, n_ctx, n_all)
                r0 = pl.multiple_of(c * ct, ct)
                u = u_s[pl.ds(r0, ct), :]
                h = h_s[pl.ds(r0, ct), :]
                dy = dy_ref[0, pl.ds(r0, ct), :]
                ub, r, i, a, sf = _lru_gates(u, wa, ba_row, wx, bx_row, sp)
                row = lax.broadcasted_iota(jnp.int32, a.shape, 0)
                if dr == 0:
                    alpha = jnp.where(row == ct - 1, 1.0, pltpu.roll(a, ct - 1, 0))
                    aa, g0 = _chunk_scan(alpha, dy, reverse=True)
                    g = g0 + aa * gc
                    gc_new = a[0:1, :] * g[0:1, :]
                    p0 = pl.multiple_of(jnp.maximum(r0 - 8, 0), 8)
                    edge = jnp.where(r0 != 0, h_s[pl.ds(p0, 8), :], 0.0)[7:8, :]
                    h_prev = jnp.where(row == 0, edge, pltpu.roll(h, 1, 0))
                else:
                    alpha = jnp.where(row == 0, 1.0, pltpu.roll(a, 1, 0))
                    aa, g0 = _chunk_scan(alpha, dy, reverse=False)
                    g = g0 + aa * gc
                    gc_new = a[ct - 1:ct, :] * g[ct - 1:ct, :]
                    r_end = r0 + ct
                    n0 = pl.multiple_of(jnp.where(r_end == t, 0, jnp.minimum(r_end, t - 8)), 8)
                    edge = jnp.where(r_end != ctx_len, h_s[pl.ds(n0, 8), :], 0.0)[0:1, :]
                    h_prev = jnp.where(row == ct - 1, edge, pltpu.roll(h, ct - 1, 0))
                da = g * h_prev
                iu = i * u
                diu = g * sf
                dlog_a = da * a - (g * iu) * (a * a) / sf
                dpre_r = (dlog_a * (-LRU_C * sp)) * (r * (1.0 - r))
                dpre_i = (diu * u) * (i * (1.0 - i))
                dpr_b, dpi_b = dpre_r.astype(BF16), dpre_i.astype(BF16)
                du = diu * i + _dot_nt(dpr_b, wa) + _dot_nt(dpi_b, wx)
                du_s[pl.ds(r0, ct), :] += du
                dwa_acc = dwa_acc + _dot_tn(ub, dpr_b)
                dwx_acc = dwx_acc + _dot_tn(ub, dpi_b)
                vacc = (vacc[0] + _colsum(dpre_r), vacc[1] + _colsum(dpre_i), vacc[2] + _colsum(dlog_a * (-LRU_C * r)))
                return gc_new, dwa_acc, dwx_acc, vacc

            zrow = jnp.zeros((1, ln), F32)
            zmat = jnp.zeros((ln, ln), F32)
            _, dwa_acc, dwx_acc, vacc = lax.fori_loop(0, n_all, bwd, (zrow, zmat, zmat, (zrow, zrow, zrow)))
            dwa_ref[0, dr, 0] = dwa_acc
            dwx_ref[0, dr, 0] = dwx_acc
            vec_ref[0, 5 + dr:6 + dr, :] = vacc[0]
            vec_ref[0, 7 + dr:8 + dr, :] = vacc[1]
            vec_ref[0, 9 + dr:10 + dr, :] = vacc[2]

        def conv_bwd(c, acc):
            r0 = pl.multiple_of(c * ct, ct)
            du = du_s[pl.ds(r0, ct), :]
            dua = (cw_ref[0:1, :] * _shifted_rows(du_s, c, 2, ctx_len, t)
                   + cw_ref[1:2, :] * _shifted_rows(du_s, c, 1, ctx_len, t)
                   + cw_ref[2:3, :] * du
                   + cw_ref[3:4, :] * _shifted_rows(du_s, c, -1, ctx_len, t))
            dua_ref[0, pl.ds(r0, ct), :] = dua.astype(BF16)
            return (acc[0] + _colsum(du * _shifted_rows(ua, c, -2, ctx_len, t)),
                    acc[1] + _colsum(du * _shifted_rows(ua, c, -1, ctx_len, t)),
                    acc[2] + _colsum(du * ua[pl.ds(r0, ct), :]),
                    acc[3] + _colsum(du * _shifted_rows(ua, c, 1, ctx_len, t)),
                    acc[4] + _colsum(du))

        zrow = jnp.zeros((1, ln), F32)
        acc = lax.fori_loop(0, n_all, conv_bwd, (zrow,) * 5)
        for k in range(5):
            vec_ref[0, k:k + 1, :] = acc[k]

    ng = d // ln
    return pl.pallas_call(
        body, name="lru_bwd", grid=(b, ng),
        in_specs=[pl.BlockSpec(memory_space=pl.ANY),
                  pl.BlockSpec((1, t, ln), lambda i, j: (i, 0, cb0 + j)),
                  pl.BlockSpec((1, t, ln), lambda i, j: (i, 0, j))] + _lru_specs(t, cb0),
        out_specs=(pl.BlockSpec((1, t, ln), lambda i, j: (i, 0, cb0 + j)),
                   pl.BlockSpec((1, 16, ln), lambda i, j: (i, 0, j)),
                   pl.BlockSpec((1, 2, 1, ln, ln), lambda i, j: (i, 0, j, 0, 0)),
                   pl.BlockSpec((1, 2, 1, ln, ln), lambda i, j: (i, 0, j, 0, 0))),
        out_shape=(jax.ShapeDtypeStruct(dproj3.shape, dproj3.dtype),
                   jax.ShapeDtypeStruct((b, 16, d), F32),
                   jax.ShapeDtypeStruct((b, 2, ng, ln, ln), F32),
                   jax.ShapeDtypeStruct((b, 2, ng, ln, ln), F32)),
        scratch_shapes=[pltpu.VMEM((t, ln), F32)] * 3,
        input_output_aliases={0: 0},
        compiler_params=_cparams(("parallel", "parallel"), VMEM_LIMIT),
    )(dproj3, proj3, dy3, conv_w, conv_b, wa_bd, ba, wx_bd, bx, lam)


def _rope_tables(ctx_len, seq):
    p = HEAD_DIM // 4
    inv = ROPE_THETA ** (-jnp.arange(p, dtype=F32) / p)
    tok = jnp.arange(seq)
    ang_r = (tok // GRID_W)[:, None] * inv
    ang_c = (tok % GRID_W)[:, None] * inv
    cos = jnp.concatenate([jnp.cos(ang_r)] * 2 + [jnp.cos(ang_c)] * 2, axis=1)
    sin = jnp.concatenate([-jnp.sin(ang_r), jnp.sin(ang_r), -jnp.sin(ang_c), jnp.sin(ang_c)], axis=1)
    cos = jnp.concatenate([jnp.ones((ctx_len, HEAD_DIM), F32), cos], axis=0)
    sin = jnp.concatenate([jnp.zeros((ctx_len, HEAD_DIM), F32), sin], axis=0)
    return cos, sin


def _swap_halves(v):
    lane = lax.broadcasted_iota(jnp.int32, v.shape, 1)
    return jnp.where((lane & 63) < 32, pltpu.roll(v, 96, 1), pltpu.roll(v, 32, 1))


def _head_rstd(v):
    return lax.rsqrt(jnp.mean(v * v, axis=-1, keepdims=True) + EPS)


QKV_BLOCK = GROUP * HEAD_DIM


def _prep_fwd(proj3, qcol, kvcol, d, cos, sin, gq, gk, use_norm):
    b, t, _ = proj3.shape
    bt, wb = ROW_BLOCK, QKV_BLOCK
    nqb = d // wb
    assert qcol % wb == 0 and kvcol % wb == 0 and d // HEAD_DIM // GROUP == 2
    qb0, kvb = qcol // wb, kvcol // wb

    def body(p_ref, cos_ref, sin_ref, gq_ref, gk_ref, o_ref):
        s = pl.program_id(2)
        c, sn = cos_ref[...], sin_ref[...]

        def rope(v):
            return v * c + _swap_halves(v) * sn

        @pl.when(s < nqb)
        def _():
            for hh in range(GROUP):
                v = p_ref[0, :, hh * HEAD_DIM:(hh + 1) * HEAD_DIM]
                if use_norm:
                    v = v * _head_rstd(v) * gq_ref[...]
                o_ref[0, :, hh * HEAD_DIM:(hh + 1) * HEAD_DIM] = rope(v).astype(BF16)

        @pl.when(s == nqb)
        def _():
            for hh in range(2):
                v = p_ref[0, :, hh * HEAD_DIM:(hh + 1) * HEAD_DIM]
                if use_norm:
                    v = v * _head_rstd(v) * gk_ref[...]
                o_ref[0, :, hh * HEAD_DIM:(hh + 1) * HEAD_DIM] = rope(v).astype(BF16)
            o_ref[0, :, 2 * HEAD_DIM:] = p_ref[0, :, 2 * HEAD_DIM:].astype(BF16)

    return pl.pallas_call(
        body, name="prep_fwd_norm" if use_norm else "prep_fwd", grid=(b, t // bt, nqb + 1),
        in_specs=[pl.BlockSpec((1, bt, wb), lambda i, j, s: (i, j, jnp.where(s < nqb, qb0 + s, kvb))),
                  pl.BlockSpec((bt, HEAD_DIM), lambda i, j, s: (j, 0)),
                  pl.BlockSpec((bt, HEAD_DIM), lambda i, j, s: (j, 0)),
                  pl.BlockSpec((1, HEAD_DIM), lambda i, j, s: (0, 0)),
                  pl.BlockSpec((1, HEAD_DIM), lambda i, j, s: (0, 0))],
        out_specs=pl.BlockSpec((1, bt, wb), lambda i, j, s: (i, j, s)),
        out_shape=jax.ShapeDtypeStruct((b, t, d + wb), BF16),
        compiler_params=_cparams(("parallel", "parallel", "arbitrary")),
    )(proj3, cos, sin, gq, gk)


def _prep_bwd(dproj3, dq3, dkt, dvt, proj3, qcol, kvcol, d, cos, sin, gq, gk, use_norm):
    b, t, _ = proj3.shape
    bt, wb = ROW_BLOCK, QKV_BLOCK
    nqb = d // wb
    qb0, kvb = qcol // wb, kvcol // wb
    kvh = dkt.shape[1]

    def body(dproj_hbm, dq_ref, dkt_ref, dvt_ref, p_ref, cos_ref, sin_ref, gq_ref, gk_ref, o_ref, gacc_ref):
        del dproj_hbm
        j, s = pl.program_id(1), pl.program_id(2)
        c, sn = cos_ref[...], sin_ref[...]

        @pl.when((j == 0) & (s == 0))
        def _():
            gacc_ref[...] = jnp.zeros_like(gacc_ref)

        def unrope(dv):
            return dv * c + _swap_halves(dv * sn)

        def head_bwd(dyv, xv, g_ref, acc_row):
            dyv = unrope(dyv)
            if not use_norm:
                return dyv
            rstd = _head_rstd(xv)
            xhat = xv * rstd
            gacc_ref[0, acc_row:acc_row + 1, :] += _colsum(dyv * xhat)
            dxhat = dyv * g_ref[...]
            return rstd * (dxhat - xhat * jnp.mean(dxhat * xhat, axis=-1, keepdims=True))

        @pl.when(s < nqb)
        def _():
            for hh in range(GROUP):
                sl = slice(hh * HEAD_DIM, (hh + 1) * HEAD_DIM)
                o_ref[0, :, sl] = head_bwd(dq_ref[0, :, sl], p_ref[0, :, sl], gq_ref, 0).astype(BF16)

        @pl.when(s == nqb)
        def _():
            for hh in range(kvh):
                sl = slice(hh * HEAD_DIM, (hh + 1) * HEAD_DIM)
                o_ref[0, :, sl] = head_bwd(dkt_ref[0, hh].T, p_ref[0, :, sl], gk_ref, 1).astype(BF16)
                sv = slice((kvh + hh) * HEAD_DIM, (kvh + hh + 1) * HEAD_DIM)
                o_ref[0, :, sv] = dvt_ref[0, hh].T.astype(BF16)

    col = lambda i, j, s: (i, j, jnp.where(s < nqb, qb0 + s, kvb))
    return pl.pallas_call(
        body, name="prep_bwd_norm" if use_norm else "prep_bwd", grid=(b, t // bt, nqb + 1),
        in_specs=[pl.BlockSpec(memory_space=pl.ANY),
                  pl.BlockSpec((1, bt, wb), lambda i, j, s: (i, j, jnp.minimum(s, nqb - 1))),
                  pl.BlockSpec((1, kvh, HEAD_DIM, bt), lambda i, j, s: (i, 0, 0, j)),
                  pl.BlockSpec((1, kvh, HEAD_DIM, bt), lambda i, j, s: (i, 0, 0, j)),
                  pl.BlockSpec((1, bt, wb), col),
                  pl.BlockSpec((bt, HEAD_DIM), lambda i, j, s: (j, 0)),
                  pl.BlockSpec((bt, HEAD_DIM), lambda i, j, s: (j, 0)),
                  pl.BlockSpec((1, HEAD_DIM), lambda i, j, s: (0, 0)),
                  pl.BlockSpec((1, HEAD_DIM), lambda i, j, s: (0, 0))],
        out_specs=(pl.BlockSpec((1, bt, wb), col), pl.BlockSpec((1, 8, HEAD_DIM), lambda i, j, s: (i, 0, 0))),
        out_shape=(jax.ShapeDtypeStruct(dproj3.shape, dproj3.dtype), jax.ShapeDtypeStruct((b, 8, HEAD_DIM), F32)),
        input_output_aliases={0: 0},
        compiler_params=_cparams(("parallel", "arbitrary", "arbitrary")),
    )(dproj3, dq3, dkt, dvt, proj3, cos, sin, gq, gk)


def _stack_heads(ref, dtype=None):
    parts = [ref[0, :, g * HEAD_DIM:(g + 1) * HEAD_DIM] for g in range(GROUP)]
    v = jnp.concatenate(parts, axis=0)
    return v if dtype is None else v.astype(dtype)


def _unstack_heads(ref, v, bq):
    for g in range(GROUP):
        ref[0, :, g * HEAD_DIM:(g + 1) * HEAD_DIM] = v[g * bq:(g + 1) * bq, :]


def _attn_specs(t, d, bq):
    kvh = d // HEAD_DIM // GROUP
    kc0 = d // HEAD_DIM
    q_spec = pl.BlockSpec((1, bq, QKV_BLOCK), lambda i, h, j: (i, j, h))
    k_spec = pl.BlockSpec((1, t, HEAD_DIM), lambda i, h, j: (i, 0, kc0 + h))
    v_spec = pl.BlockSpec((1, t, HEAD_DIM), lambda i, h, j: (i, 0, kc0 + kvh + h))
    lse_spec = pl.BlockSpec((1, 1, GROUP * bq, HEAD_DIM), lambda i, h, j: (i, h, j, 0))
    kt_spec = pl.BlockSpec((1, 1, HEAD_DIM, t), lambda i, h, j: (i, h, 0, 0))
    return kvh, q_spec, k_spec, v_spec, lse_spec, kt_spec


SCALE = HEAD_DIM ** -0.5


def _attn_dense_fwd(qkv, d, ctx_len):
    b, t, _ = qkv.shape
    bq, bk = ATT_BQ, ATT_BK
    rows = GROUP * bq
    lq = ctx_len // bq
    kvh, q_spec, k_spec, v_spec, lse_spec, _ = _attn_specs(t, d, bq)

    def body(q_ref, k_ref, v_ref, o_ref, lse_ref, m_s, l_s, acc_s):
        i = pl.program_id(2)
        q4 = _stack_heads(q_ref)
        m_s[...] = jnp.full_like(m_s, NEG_INF)
        l_s[...] = jnp.zeros_like(l_s)
        acc_s[...] = jnp.zeros_like(acc_s)

        def step(j, _):
            k0 = pl.multiple_of(j * bk, bk)
            s = _dot_nt(q4, k_ref[0, pl.ds(k0, bk), :]) * SCALE
            m_prev = m_s[...]
            m_new = jnp.maximum(m_prev, jnp.max(s, axis=1, keepdims=True))
            alpha = jnp.exp(m_prev - m_new)
            p = jnp.exp(s - m_new)
            l_s[...] = alpha * l_s[...] + jnp.sum(p, axis=1, keepdims=True)
            acc_s[...] = alpha * acc_s[...] + _dot(p.astype(BF16), v_ref[0, pl.ds(k0, bk), :])
            m_s[...] = m_new
            return 0

        lax.fori_loop(0, jnp.where(i < lq, ctx_len // bk, t // bk), step, 0)
        l = l_s[...]
        _unstack_heads(o_ref, acc_s[...] / l, bq)
        lse_ref[0, 0] = jnp.broadcast_to(m_s[...] + jnp.log(l), (rows, HEAD_DIM))

    return pl.pallas_call(
        body, name="attn_dense_fwd", grid=(b, kvh, t // bq),
        in_specs=[q_spec, k_spec, v_spec], out_specs=(q_spec, lse_spec),
        out_shape=(jax.ShapeDtypeStruct((b, t, d), F32), jax.ShapeDtypeStruct((b, kvh, GROUP * t, HEAD_DIM), F32)),
        scratch_shapes=[pltpu.VMEM((rows, 1), F32), pltpu.VMEM((rows, 1), F32), pltpu.VMEM((rows, HEAD_DIM), F32)],
        compiler_params=_cparams(("parallel", "parallel", "arbitrary"), VMEM_LIMIT),
    )(qkv, qkv, qkv)


def _attn_dense_bwd(qkv, o3, do3, lse, d, ctx_len):
    b, t, _ = qkv.shape
    bq, bk = ATT_BQ, ATT_BK
    rows = GROUP * bq
    lq = ctx_len // bq
    kvh, q_spec, k_spec, v_spec, lse_spec, kt_spec = _attn_specs(t, d, bq)

    def body(q_ref, k_ref, v_ref, o_ref, do_ref, lse_ref, dq_ref, dkt_ref, dvt_ref, dq_s, qt_s, dot_s):
        i = pl.program_id(2)

        @pl.when(i == 0)
        def _():
            dkt_ref[...] = jnp.zeros_like(dkt_ref)
            dvt_ref[...] = jnp.zeros_like(dvt_ref)

        q4 = _stack_heads(q_ref)
        do4 = _stack_heads(do_ref)
        dd = jnp.sum(do4 * _stack_heads(o_ref), axis=1, keepdims=True)
        lse_col = lse_ref[0, 0][:, 0:1]
        do4b = do4.astype(BF16)
        qt_s[...] = q4.astype(F32).T.astype(BF16)
        dot_s[...] = do4.T.astype(BF16)
        dq_s[...] = jnp.zeros_like(dq_s)

        def step(j, _):
            k0 = pl.multiple_of(j * bk, bk)
            k = k_ref[0, pl.ds(k0, bk), :]
            p = jnp.exp(_dot_nt(q4, k) * SCALE - lse_col)
            dp = _dot_nt(do4b, v_ref[0, pl.ds(k0, bk), :])
            ds = (p * (dp - dd) * SCALE).astype(BF16)
            dq_s[...] += _dot(ds, k)
            dkt_ref[0, 0, :, pl.ds(k0, bk)] += _dot(qt_s[...], ds)
            dvt_ref[0, 0, :, pl.ds(k0, bk)] += _dot(dot_s[...], p.astype(BF16))
            return 0

        lax.fori_loop(0, jnp.where(i < lq, ctx_len // bk, t // bk), step, 0)
        _unstack_heads(dq_ref, dq_s[...], bq)

    return pl.pallas_call(
        body, name="attn_dense_bwd", grid=(b, kvh, t // bq),
        in_specs=[q_spec, k_spec, v_spec, q_spec, q_spec, lse_spec], out_specs=(q_spec, kt_spec, kt_spec),
        out_shape=(jax.ShapeDtypeStruct((b, t, d), F32), jax.ShapeDtypeStruct((b, kvh, HEAD_DIM, t), F32),
                   jax.ShapeDtypeStruct((b, kvh, HEAD_DIM, t), F32)),
        scratch_shapes=[pltpu.VMEM((rows, HEAD_DIM), F32), pltpu.VMEM((HEAD_DIM, rows), BF16),
                        pltpu.VMEM((HEAD_DIM, rows), BF16)],
        compiler_params=_cparams(("parallel", "parallel", "arbitrary"), VMEM_LIMIT),
    )(qkv, qkv, qkv, o3, do3, lse)


def _sink_column(sink_ref, h, bq):
    rowi = lax.broadcasted_iota(jnp.int32, (GROUP * bq, 1), 0)
    col = jnp.zeros((GROUP * bq, 1), F32)
    for g in range(GROUP):
        col = jnp.where((rowi >= g * bq) & (rowi < (g + 1) * bq), sink_ref[h * GROUP + g], col)
    return col


def _band(i, lq, ctx_len, t, bq):
    n = i - lq
    start = pl.multiple_of(jnp.clip(ctx_len + (n - 1) * bq, ctx_len, t - WIN_SPAN), bq)
    shape = (GROUP * bq, WIN_SPAN)
    kpos = start - ctx_len + lax.broadcasted_iota(jnp.int32, shape, 1)
    qpos = n * bq + (lax.broadcasted_iota(jnp.int32, shape, 0) & (bq - 1))
    return start, jnp.abs(kpos - qpos) <= WINDOW


def _attn_win_fwd(qkv, sink, d, ctx_len):
    b, t, _ = qkv.shape
    bq = WIN_BQ
    rows = GROUP * bq
    lq = ctx_len // bq
    kvh, q_spec, k_spec, v_spec, lse_spec, _ = _attn_specs(t, d, bq)

    def body(sink_ref, q_ref, k_ref, v_ref, o_ref, lse_ref):
        h, i = pl.program_id(1), pl.program_id(2)
        q4 = _stack_heads(q_ref)
        sink_col = _sink_column(sink_ref, h, bq)
        sc = _dot_nt(q4, k_ref[0, 0:ctx_len, :]) * SCALE
        mc = jnp.maximum(jnp.max(sc, axis=1, keepdims=True), sink_col)

        def finish(m, l, acc):
            _unstack_heads(o_ref, acc / l, bq)
            lse_ref[0, 0] = jnp.broadcast_to(m + jnp.log(l), (rows, HEAD_DIM))

        @pl.when(i < lq)
        def _():
            pc = jnp.exp(sc - mc)
            l = jnp.sum(pc, axis=1, keepdims=True) + jnp.exp(sink_col - mc)
            finish(mc, l, _dot(pc.astype(BF16), v_ref[0, 0:ctx_len, :]))

        @pl.when(i >= lq)
        def _():
            start, ok = _band(i, lq, ctx_len, t, bq)
            sb = jnp.where(ok, _dot_nt(q4, k_ref[0, pl.ds(start, WIN_SPAN), :]) * SCALE, NEG_INF)
            m = jnp.maximum(mc, jnp.max(sb, axis=1, keepdims=True))
            pc, pb = jnp.exp(sc - m), jnp.exp(sb - m)
            l = jnp.sum(pc, axis=1, keepdims=True) + jnp.sum(pb, axis=1, keepdims=True) + jnp.exp(sink_col - m)
            acc = _dot(pc.astype(BF16), v_ref[0, 0:ctx_len, :]) + _dot(pb.astype(BF16), v_ref[0, pl.ds(start, WIN_SPAN), :])
            finish(m, l, acc)

    return pl.pallas_call(
        body, name="attn_win_fwd", grid=(b, kvh, t // bq),
        in_specs=[pl.BlockSpec(memory_space=pltpu.SMEM), q_spec, k_spec, v_spec], out_specs=(q_spec, lse_spec),
        out_shape=(jax.ShapeDtypeStruct((b, t, d), F32), jax.ShapeDtypeStruct((b, kvh, GROUP * t, HEAD_DIM), F32)),
        compiler_params=_cparams(("parallel", "parallel", "arbitrary"), VMEM_LIMIT),
    )(sink, qkv, qkv, qkv)


def _attn_win_bwd(qkv, sink, o3, do3, lse, d, ctx_len):
    b, t, _ = qkv.shape
    bq = WIN_BQ
    rows = GROUP * bq
    lq = ctx_len // bq
    kvh, q_spec, k_spec, v_spec, lse_spec, kt_spec = _attn_specs(t, d, bq)

    def body(sink_ref, q_ref, k_ref, v_ref, o_ref, do_ref, lse_ref, dq_ref, dkt_ref, dvt_ref, dsk_ref):
        h, i = pl.program_id(1), pl.program_id(2)

        @pl.when(i == 0)
        def _():
            dkt_ref[...] = jnp.zeros_like(dkt_ref)
            dvt_ref[...] = jnp.zeros_like(dvt_ref)
            dsk_ref[...] = jnp.zeros_like(dsk_ref)

        q4 = _stack_heads(q_ref)
        do4 = _stack_heads(do_ref)
        dd = jnp.sum(do4 * _stack_heads(o_ref), axis=1, keepdims=True)
        lse_col = lse_ref[0, 0][:, 0:1]
        do4b = do4.astype(BF16)
        qt = q4.astype(F32).T.astype(BF16)
        dot = do4.T.astype(BF16)

        def part(k, v):
            return _dot_nt(q4, k) * SCALE, _dot_nt(do4b, v)

        def grads(p, dp, k):
            ds = (p * (dp - dd) * SCALE).astype(BF16)
            return _dot(ds, k), _dot(qt, ds), _dot(dot, p.astype(BF16))

        kc = k_ref[0, 0:ctx_len, :]
        sc, dpc = part(kc, v_ref[0, 0:ctx_len, :])
        dq_c, dk_c, dv_c = grads(jnp.exp(sc - lse_col), dpc, kc)
        dkt_ref[0, 0, :, 0:ctx_len] += dk_c
        dvt_ref[0, 0, :, 0:ctx_len] += dv_c
        _unstack_heads(dq_ref, dq_c, bq)

        @pl.when(i >= lq)
        def _():
            start, ok = _band(i, lq, ctx_len, t, bq)
            kb = k_ref[0, pl.ds(start, WIN_SPAN), :]
            sb, dpb = part(kb, v_ref[0, pl.ds(start, WIN_SPAN), :])
            pb = jnp.where(ok, jnp.exp(sb - lse_col), 0.0)
            dq_b, dk_b, dv_b = grads(pb, dpb, kb)
            dkt_ref[0, 0, :, pl.ds(start, WIN_SPAN)] += dk_b
            dvt_ref[0, 0, :, pl.ds(start, WIN_SPAN)] += dv_b
            for g in range(GROUP):
                dq_ref[0, :, g * HEAD_DIM:(g + 1) * HEAD_DIM] += dq_b[g * bq:(g + 1) * bq, :]

        ps = jnp.exp(_sink_column(sink_ref, h, bq) - lse_col) * dd
        for g in range(GROUP):
            val = jnp.sum(ps[g * bq:(g + 1) * bq, :], axis=0, keepdims=True)
            dsk_ref[0, 0, g:g + 1, :] -= jnp.broadcast_to(val, (1, HEAD_DIM))

    return pl.pallas_call(
        body, name="attn_win_bwd", grid=(b, kvh, t // bq),
        in_specs=[pl.BlockSpec(memory_space=pltpu.SMEM), q_spec, k_spec, v_spec, q_spec, q_spec, lse_spec],
        out_specs=(q_spec, kt_spec, kt_spec, pl.BlockSpec((1, 1, 8, HEAD_DIM), lambda i, h, j: (i, h, 0, 0))),
        out_shape=(jax.ShapeDtypeStruct((b, t, d), F32), jax.ShapeDtypeStruct((b, kvh, HEAD_DIM, t), F32),
                   jax.ShapeDtypeStruct((b, kvh, HEAD_DIM, t), F32), jax.ShapeDtypeStruct((b, kvh, 8, HEAD_DIM), F32)),
        compiler_params=_cparams(("parallel", "parallel", "arbitrary"), VMEM_LIMIT),
    )(sink, qkv, qkv, qkv, o3, do3, lse)


MERGE_BWD_ROWS = 128


def _resident(shape):
    return pl.BlockSpec(shape, lambda *_: (0,) * len(shape), pipeline_mode=pl.Buffered(1))


def _merge_fwd(x3, ya, yb, yc, proj3, w_br, w_out, modsel, ctx_len):
    b, t, d = x3.shape
    bt = ROW_BLOCK
    lb = ctx_len // bt

    def body(x_ref, ya_ref, yb_ref, yc_ref, gm_ref, wbr_ref, wo_ref, m_ref, xn_ref, out_ref):
        mix = jnp.zeros((bt, d), F32)
        for n, y_ref in enumerate((ya_ref, yb_ref, yc_ref)):
            z = (y_ref[0] * _silu(gm_ref[0, :, n * d:(n + 1) * d])).astype(BF16)
            mix = mix + _sigmoid(gm_ref[0, :, (3 + n) * d:(4 + n) * d]) * _dot(z, wbr_ref[n])
        o = _dot(mix.astype(BF16), wo_ref[...])
        out_ref[0] = o
        xn_ref[0] = x_ref[0] + m_ref[0, 0, 2:3, :] * o

    blk = pl.BlockSpec((1, bt, d), lambda i, j: (i, j, 0))
    return pl.pallas_call(
        body, name="merge_fwd", grid=(b, t // bt),
        in_specs=[blk, blk, blk, blk, pl.BlockSpec((1, bt, 6 * d), lambda i, j: (i, j, 0)),
                  _resident((3, d, d)), _resident((d, d)),
                  pl.BlockSpec((1, 1, 8, d), lambda i, j: (i, _row_kind(j, lb), 0, 0))],
        out_specs=(blk, blk),
        out_shape=(jax.ShapeDtypeStruct((b, t, d), F32), jax.ShapeDtypeStruct((b, t, d), F32)),
        compiler_params=_cparams(("parallel", "arbitrary"), VMEM_LIMIT),
    )(x3, ya, yb, yc, proj3, w_br, w_out, modsel)


def _merge_bwd(dxn3, out3, ya, yb, yc, proj3, w_br, w_out, modsel, ctx_len):
    b, t, d = dxn3.shape
    n_cols = proj3.shape[2]
    bt = MERGE_BWD_ROWS
    lb = ctx_len // bt

    def body(dxn_ref, out_ref, ya_ref, yb_ref, yc_ref, gm_ref, wbr_ref, wo_ref, m_ref,
             dgm_ref, dya_ref, dyb_ref, dyc_ref, z_ref, dt_ref, mix_ref, dout_ref, gacc_ref):
        j = pl.program_id(1)
        dxn = dxn_ref[0]
        doutb = (m_ref[0, 0, 2:3, :] * dxn).astype(BF16)
        dout_ref[0] = doutb

        @pl.when((j == 0) | (j == lb))
        def _():
            gacc_ref[...] = jnp.zeros_like(gacc_ref)

        gacc_ref[0, 0, 0:1, :] += _colsum(dxn * out_ref[0])
        dmix = _dot_nt(doutb, wo_ref[...])
        mix = jnp.zeros((bt, d), F32)
        for n, (y_ref, dy_ref) in enumerate(((ya_ref, dya_ref), (yb_ref, dyb_ref), (yc_ref, dyc_ref))):
            g = gm_ref[0, :, n * d:(n + 1) * d]
            y = y_ref[0]
            sig_g = _sigmoid(g)
            silu_g = g * sig_g
            z = (y * silu_g).astype(BF16)
            z_ref[n, 0] = z
            tn = _dot(z, wbr_ref[n])
            s = _sigmoid(gm_ref[0, :, (3 + n) * d:(4 + n) * d])
            mix = mix + s * tn
            dgm_ref[0, :, (3 + n) * d:(4 + n) * d] = (dmix * tn * (s * (1.0 - s))).astype(BF16)
            dtb = (dmix * s).astype(BF16)
            dt_ref[n, 0] = dtb
            dz = _dot_nt(dtb, wbr_ref[n])
            dy_ref[0] = dz * silu_g
            dgm_ref[0, :, n * d:(n + 1) * d] = (dz * y * (sig_g * (1.0 + g * (1.0 - sig_g)))).astype(BF16)
        mix_ref[0] = mix.astype(BF16)

    blk = pl.BlockSpec((1, bt, d), lambda i, j: (i, j, 0))
    blk4 = pl.BlockSpec((3, 1, bt, d), lambda i, j: (0, i, j, 0))
    wide = pl.BlockSpec((1, bt, 6 * d), lambda i, j: (i, j, 0))
    return pl.pallas_call(
        body, name="merge_bwd", grid=(b, t // bt),
        in_specs=[blk, blk, blk, blk, blk, wide, _resident((3, d, d)), _resident((d, d)),
                  pl.BlockSpec((1, 1, 8, d), lambda i, j: (i, _row_kind(j, lb), 0, 0))],
        out_specs=(wide, blk, blk, blk, blk4, blk4, blk, blk,
                   pl.BlockSpec((1, 1, 8, d), lambda i, j: (i, _row_kind(j, lb), 0, 0))),
        out_shape=(jax.ShapeDtypeStruct((b, t, n_cols), BF16),
                   jax.ShapeDtypeStruct((b, t, d), F32), jax.ShapeDtypeStruct((b, t, d), F32),
                   jax.ShapeDtypeStruct((b, t, d), F32),
                   jax.ShapeDtypeStruct((3, b, t, d), BF16), jax.ShapeDtypeStruct((3, b, t, d), BF16),
                   jax.ShapeDtypeStruct((b, t, d), BF16), jax.ShapeDtypeStruct((b, t, d), BF16),
                   jax.ShapeDtypeStruct((b, 2, 8, d), F32)),
        compiler_params=_cparams(("parallel", "arbitrary"), VMEM_LIMIT),
    )(dxn3, out3, ya, yb, yc, proj3, w_br, w_out, modsel)


def _final(x3, g, target, ctx_len):
    b, t, d = x3.shape
    bt = ROW_BLOCK
    lb = ctx_len // bt

    def body(x_ref, g_ref, t_ref, dx_ref, loss_ref, dg_ref):
        j = pl.program_id(1)

        @pl.when(j == 0)
        def _():
            loss_ref[...] = jnp.zeros_like(loss_ref)
            dg_ref[...] = jnp.zeros_like(dg_ref)

        @pl.when(j < lb)
        def _():
            dx_ref[...] = jnp.zeros_like(dx_ref)

        @pl.when(j >= lb)
        def _():
            x = x_ref[0]
            g_row = g_ref[...]
            rstd = lax.rsqrt(jnp.mean(x * x, axis=-1, keepdims=True) + EPS)
            xhat = x * rstd
            err = xhat * g_row - t_ref[0]
            loss_ref[...] += (0.5 / d) * jnp.sum(err * err)
            dy = err * (1.0 / d)
            dg_ref[0, 0:1, :] += _colsum(dy * xhat)
            dxhat = dy * g_row
            dx_ref[0] = rstd * (dxhat - xhat * jnp.mean(dxhat * xhat, axis=-1, keepdims=True))

    blk = pl.BlockSpec((1, bt, d), lambda i, j: (i, j, 0))
    return pl.pallas_call(
        body, name="final_loss", grid=(b, t // bt),
        in_specs=[blk, pl.BlockSpec((1, d), lambda i, j: (0, 0)),
                  pl.BlockSpec((1, bt, d), lambda i, j: (i, jnp.maximum(j - lb, 0), 0))],
        out_specs=(blk, pl.BlockSpec((1, 8, HEAD_DIM), lambda i, j: (i, 0, 0)), pl.BlockSpec((1, 8, d), lambda i, j: (i, 0, 0))),
        out_shape=(jax.ShapeDtypeStruct((b, t, d), F32), jax.ShapeDtypeStruct((b, 8, HEAD_DIM), F32),
                   jax.ShapeDtypeStruct((b, 8, d), F32)),
        compiler_params=_cparams(("parallel", "arbitrary")),
    )(x3, g, target)


def _pick(n, options):
    for o in options:
        if n % o == 0:
            return o
    raise ValueError((n, options))


def _block_diag(w):
    per = LRU_LANES // LRU_BLOCK_W
    nd, nb, bw, _ = w.shape
    wr = w.reshape(nd, nb // per, per, bw, bw)
    eye = jnp.eye(per, dtype=w.dtype)
    bd = wr[:, :, :, :, None, :] * eye[None, None, :, None, :, None]
    return bd.reshape(nd, nb // per, per * bw, per * bw).astype(BF16)


def _block_diag_grad(g):
    per = LRU_LANES // LRU_BLOCK_W
    nd, ng, _, _ = g.shape
    gr = g.reshape(nd, ng, per, LRU_BLOCK_W, per, LRU_BLOCK_W)
    diag = jnp.stack([gr[:, :, k, :, k, :] for k in range(per)], axis=2)
    return diag.reshape(nd, ng * per, LRU_BLOCK_W, LRU_BLOCK_W)


def _mod_select(mod16, b, d):
    m3 = mod16.reshape(MOD_ROWS, 3, d)
    lat = m3[:b]
    ctx = jnp.broadcast_to(m3[b][None], (b, 3, d))
    sel = jnp.stack([ctx, lat], axis=1)
    return jnp.pad(sel, ((0, 0), (0, 0), (0, 5), (0, 0)))


def _layer_fwd(x3, c16, p, cos, sin, ctx_len):
    b, t, d = x3.shape
    off, n_cols = _layout(d)
    mod16 = _mod_fwd(c16, p["w_mod"], p["b_mod"])
    modsel = _mod_select(mod16, b, d)
    h = _norm_mod_fwd(x3, p["norm_g"], modsel, ctx_len)
    proj = _matmul(h.reshape(b * t, d), p["w_in"], bm=_pick(b * t, (512, 256, 128)), bn=1024, bk=d, name="proj_fwd")
    proj3 = proj.reshape(b, t, n_cols)
    ya = _lru_fwd(proj3, off["uA"], p["conv_w"], p["conv_b"], p["wa_bd"], p["ba"], p["wx_bd"], p["bx"], p["lam"], ctx_len)
    qkv_b = _prep_fwd(proj3, off["qB"], off["kB"], d, cos, sin, p["gq"], p["gk"], use_norm=False)
    yb, lse_b = _attn_win_fwd(qkv_b, p["sink"], d, ctx_len)
    qkv_c = _prep_fwd(proj3, off["qC"], off["kC"], d, cos, sin, p["gq"], p["gk"], use_norm=True)
    yc, lse_c = _attn_dense_fwd(qkv_c, d, ctx_len)
    x_new, out3 = _merge_fwd(x3, ya, yb, yc, proj3, p["w_br"], p["w_out"], modsel, ctx_len)
    return x_new, (x3, modsel, h, proj3, ya, yb, yc, qkv_b, lse_b, qkv_c, lse_c, out3)


def _layer_bwd(dxn3, saved, c16, p, cos, sin, ctx_len):
    x3, modsel, h, proj3, ya, yb, yc, qkv_b, lse_b, qkv_c, lse_c, out3 = saved
    b, t, d = x3.shape
    off, n_cols = _layout(d)
    rows = b * t
    bk = _pick(rows, (512, 256, 128))
    dproj3, dya, dyb, dyc, z4, dt4, mixb, doutb, gacc = _merge_bwd(dxn3, out3, ya, yb, yc, proj3, p["w_br"], p["w_out"],
                                                                   modsel, ctx_len)
    dw_br = jnp.stack([_matmul(z4[n].reshape(rows, d), dt4[n].reshape(rows, d), ta=True, bm=d, bn=d, bk=bk,
                               name="dw_branch") for n in range(3)])
    dw_out = _matmul(mixb.reshape(rows, d), doutb.reshape(rows, d), ta=True, bm=d, bn=d, bk=bk, name="dw_out")
    dproj3, vec, dwa, dwx = _lru_bwd(dproj3, proj3, off["uA"], dya, p["conv_w"], p["conv_b"], p["wa_bd"], p["ba"],
                                     p["wx_bd"], p["bx"], p["lam"], ctx_len)
    dq_b, dkt_b, dvt_b, dsk = _attn_win_bwd(qkv_b, p["sink"], yb, dyb, lse_b, d, ctx_len)
    dproj3, _ = _prep_bwd(dproj3, dq_b, dkt_b, dvt_b, proj3, off["qB"], off["kB"], d, cos, sin, p["gq"], p["gk"], False)
    dq_c, dkt_c, dvt_c = _attn_dense_bwd(qkv_c, yc, dyc, lse_c, d, ctx_len)
    dproj3, gqk = _prep_bwd(dproj3, dq_c, dkt_c, dvt_c, proj3, off["qC"], off["kC"], d, cos, sin, p["gq"], p["gk"], True)
    dproj2 = dproj3.reshape(rows, n_cols)
    dw_in = _matmul(h.reshape(rows, d), dproj2, ta=True, bm=d, bn=1024, bk=bk, name="dw_in")
    dh = _matmul(dproj2, p["w_in"], tb=True, bm=_pick(rows, (512, 256, 128)), bn=d, bk=1024, name="dh")
    dx3, nacc = _norm_mod_bwd(dh.reshape(b, t, d), x3, p["norm_g"], modsel, dxn3, ctx_len)
    per = jnp.stack([nacc[:, :, 0], nacc[:, :, 1], gacc[:, :, 0]], axis=2)
    dmod = jnp.concatenate([per[:, 1].reshape(b, 3 * d), jnp.sum(per[:, 0], axis=0).reshape(1, 3 * d)], axis=0)
    dmod16 = jnp.pad(dmod, ((0, MOD_ROWS - b - 1), (0, 0)))
    dw_mod, db_mod, dc16 = _mod_bwd(c16, dmod16, p["w_mod"])
    vsum = jnp.sum(vec, axis=0)
    grads = {
        "norm_g": jnp.sum(nacc[:, :, 2], axis=(0, 1)),
        "w_mod": dw_mod, "b_mod": db_mod[0], "w_in": dw_in,
        "conv_w": vsum[0:4], "conv_b": vsum[4],
        "lru_wa": _block_diag_grad(jnp.sum(dwa, axis=0)), "lru_ba": vsum[5:7],
        "lru_wx": _block_diag_grad(jnp.sum(dwx, axis=0)), "lru_bx": vsum[7:9],
        "lru_lambda": vsum[9:11] * (-jax.nn.sigmoid(-p["lam"])),
        "attn_sink": jnp.sum(dsk[:, :, 0:GROUP, 0], axis=0).reshape(-1),
        "q_norm_g": jnp.sum(gqk[:, 0], axis=0), "k_norm_g": jnp.sum(gqk[:, 1], axis=0),
        "w_branch": dw_br, "w_out": dw_out,
    }
    return dx3, dc16, grads


def _reorder_in_cols(w, d, inverse=False):
    off_new, _ = _layout(d)
    segs = _orig_segments(d)
    if inverse:
        return jnp.concatenate([w[..., off_new[n]:off_new[n] + wd] for n, _, wd in segs], axis=-1)
    by_name = {n: (o, wd) for n, o, wd in segs}
    order = sorted(off_new, key=off_new.get)
    return jnp.concatenate([w[..., by_name[n][0]:by_name[n][0] + by_name[n][1]] for n in order], axis=-1)


def _layer_params(li, w):
    d = w["norm_g"].shape[1]
    return {
        "norm_g": w["norm_g"][li][None], "w_mod": w["w_mod"][li], "b_mod": w["b_mod"][li][None],
        "w_in": _reorder_in_cols(w["w_in"][li], d),
        "conv_w": w["conv_w"][li], "conv_b": w["conv_b"][li][None],
        "wa_bd": _block_diag(w["lru_wa"][li]), "ba": w["lru_ba"][li],
        "wx_bd": _block_diag(w["lru_wx"][li]), "bx": w["lru_bx"][li], "lam": w["lru_lambda"][li],
        "sink": w["attn_sink"][li], "gq": w["q_norm_g"][li][None], "gk": w["k_norm_g"][li][None],
        "w_br": w["w_branch"][li], "w_out": w["w_out"][li],
    }


def _local_step(x, c, ctx, target, c_ctx, final_g, layers):
    b, s, d = x.shape
    ctx_len = ctx.shape[1]
    cos, sin = _rope_tables(ctx_len, s)
    x3 = jnp.concatenate([ctx, x], axis=1)
    c16 = jnp.concatenate([c, c_ctx[None], jnp.zeros((MOD_ROWS - b - 1, d), F32)], axis=0)
    saved = []
    for p in layers:
        x3, sv = _layer_fwd(x3, c16, p, cos, sin, ctx_len)
        saved.append(sv)
    dx3, loss_acc, dgf = _final(x3, final_g[None], target, ctx_len)
    grads = [None] * len(layers)
    dc_ctx = jnp.zeros((d,), F32)
    for li in reversed(range(len(layers))):
        dx3, dc16, grads[li] = _layer_bwd(dx3, saved[li], c16, layers[li], cos, sin, ctx_len)
        dc_ctx = dc_ctx + dc16[b]
    return jnp.sum(loss_acc[:, 0, 0]), dx3[:, ctx_len:], dc_ctx, jnp.sum(dgf[:, 0], axis=0), grads


N_CHIPS = 4
ANY = pl.BlockSpec(memory_space=pl.ANY)


def _place():
    x, y, c = lax.axis_index("x"), lax.axis_index("y"), lax.axis_index("c")
    return x, y, c, [(1 - x, y), (x, 1 - y), (1 - x, 1 - y)]


def _axis_part(ref, axis, start, size):
    idx = [slice(None)] * len(ref.shape)
    idx[axis] = pl.ds(start, size)
    return ref.at[tuple(idx)]


def _remote(src, dst, send, recv, dev):
    return pltpu.make_async_remote_copy(src_ref=src, dst_ref=dst, send_sem=send, recv_sem=recv, device_id=dev,
                                        device_id_type=MESH)


def _gather_chips(shards, axes, name):
    n = len(shards)

    def whole(s, ax):
        shape = list(s.shape)
        shape[ax] *= N_CHIPS
        return jax.ShapeDtypeStruct(tuple(shape), s.dtype)

    def body(*refs):
        ins, outs = refs[:n], refs[n:2 * n]
        send, recv, loc = refs[2 * n:]
        x, y, c, chips = _place()
        me = 2 * x + y
        local = []
        for i in range(n):
            sz = shards[i].shape[axes[i]]
            mine = _axis_part(outs[i], axes[i], me * sz, sz)
            local.append(pltpu.make_async_copy(ins[i], mine, loc.at[i]))
            local[-1].start()
            for k, (px, py) in enumerate(chips):
                _remote(ins[i], mine, send.at[i, k], recv.at[i, k], (px, py, c)).start()
        for i in range(n):
            sz = shards[i].shape[axes[i]]
            for k, (px, py) in enumerate(chips):
                theirs = _axis_part(outs[i], axes[i], (2 * px + py) * sz, sz)
                cp = _remote(ins[i], theirs, send.at[i, k], recv.at[i, k], (px, py, c))
                cp.wait_recv()
                cp.wait_send()
            local[i].wait()

    return pl.pallas_call(
        body, name=name, in_specs=[ANY] * n, out_specs=tuple([ANY] * n),
        out_shape=tuple(whole(s, ax) for s, ax in zip(shards, axes)),
        scratch_shapes=[pltpu.SemaphoreType.DMA((n, 3)), pltpu.SemaphoreType.DMA((n, 3)), pltpu.SemaphoreType.DMA((n,))],
    )(*shards)


def _split_cores(gs, name):
    n = len(gs)

    def half(g):
        return jax.ShapeDtypeStruct((g.shape[0] // 2,) + g.shape[1:], g.dtype)

    def body(*refs):
        ins, kept, got = refs[:n], refs[n:2 * n], refs[2 * n:3 * n]
        send, recv, loc = refs[3 * n:]
        x, y, c, _ = _place()
        sib = (x, y, 1 - c)
        cps = []
        for i in range(n):
            hl = gs[i].shape[0] // 2
            lc = pltpu.make_async_copy(ins[i].at[pl.ds(c * hl, hl)], kept[i], loc.at[i])
            rc = _remote(ins[i].at[pl.ds((1 - c) * hl, hl)], got[i], send.at[i], recv.at[i], sib)
            lc.start()
            rc.start()
            cps.append((lc, rc))
        for lc, rc in cps:
            rc.wait()
            lc.wait()

    outs = pl.pallas_call(
        body, name=name, in_specs=[ANY] * n, out_specs=tuple([ANY] * (2 * n)),
        out_shape=tuple(half(g) for g in gs) * 2,
        scratch_shapes=[pltpu.SemaphoreType.DMA((n,)), pltpu.SemaphoreType.DMA((n,)), pltpu.SemaphoreType.DMA((n,))],
    )(*gs)
    return outs[:n], outs[n:]


def _scatter_chips(ps, axes, name):
    n = len(ps)

    def block(p, ax):
        shape = list(p.shape)
        shape[ax] //= N_CHIPS
        return tuple(shape)

    def body(*refs):
        ins, own, got = refs[:n], refs[n:2 * n], refs[2 * n:3 * n]
        send, recv, loc = refs[3 * n:]
        x, y, c, chips = _place()
        me = 2 * x + y
        local = []
        for i in range(n):
            sz = ps[i].shape[axes[i]] // N_CHIPS
            local.append(pltpu.make_async_copy(_axis_part(ins[i], axes[i], me * sz, sz), own[i], loc.at[i]))
            local[-1].start()
            for k, (px, py) in enumerate(chips):
                _remote(_axis_part(ins[i], axes[i], (2 * px + py) * sz, sz), got[i].at[k], send.at[i, k], recv.at[i, k],
                        (px, py, c)).start()
        for i in range(n):
            sz = ps[i].shape[axes[i]] // N_CHIPS
            for k, (px, py) in enumerate(chips):
                cp = _remote(_axis_part(ins[i], axes[i], (2 * px + py) * sz, sz), got[i].at[k], send.at[i, k],
                             recv.at[i, k], (px, py, c))
                cp.wait_recv()
                cp.wait_send()
            local[i].wait()

    outs = pl.pallas_call(
        body, name=name, in_specs=[ANY] * n, out_specs=tuple([ANY] * (2 * n)),
        out_shape=tuple(jax.ShapeDtypeStruct(block(p, ax), p.dtype) for p, ax in zip(ps, axes))
        + tuple(jax.ShapeDtypeStruct((3,) + block(p, ax), p.dtype) for p, ax in zip(ps, axes)),
        scratch_shapes=[pltpu.SemaphoreType.DMA((n, 3)), pltpu.SemaphoreType.DMA((n, 3)), pltpu.SemaphoreType.DMA((n,))],
    )(*ps)
    return outs[:n], outs[n:]


def _join_cores(hs, name):
    n = len(hs)

    def body(*refs):
        ins, outs = refs[:n], refs[n:2 * n]
        send, recv, loc = refs[2 * n:]
        x, y, c, _ = _place()
        sib = (x, y, 1 - c)
        cps = []
        for i in range(n):
            hl = hs[i].shape[0]
            lc = pltpu.make_async_copy(ins[i], outs[i].at[pl.ds(c * hl, hl)], loc.at[i])
            lc.start()
            _remote(ins[i], outs[i].at[pl.ds(c * hl, hl)], send.at[i], recv.at[i], sib).start()
            cps.append(lc)
        for i in range(n):
            hl = hs[i].shape[0]
            cp = _remote(ins[i], outs[i].at[pl.ds((1 - c) * hl, hl)], send.at[i], recv.at[i], sib)
            cp.wait_recv()
            cp.wait_send()
            cps[i].wait()

    return pl.pallas_call(
        body, name=name, in_specs=[ANY] * n, out_specs=tuple([ANY] * n),
        out_shape=tuple(jax.ShapeDtypeStruct((2 * h.shape[0],) + h.shape[1:], h.dtype) for h in hs),
        scratch_shapes=[pltpu.SemaphoreType.DMA((n,)), pltpu.SemaphoreType.DMA((n,)), pltpu.SemaphoreType.DMA((n,))],
    )(*hs)


def _all_reduce_small(buf):
    r = buf.shape[0]

    def body(in_ref, out_ref, sib_buf, chip_sum, got, send, recv):
        x, y, c, chips = _place()
        cp = _remote(in_ref, sib_buf, send.at[0], recv.at[0], (x, y, 1 - c))
        cp.start()
        cp.wait()
        chip_sum[...] = in_ref[...] + sib_buf[...]
        cps = [_remote(chip_sum, got.at[k], send.at[1 + k], recv.at[1 + k], (px, py, c)) for k, (px, py) in enumerate(chips)]
        for cp in cps:
            cp.start()
        for cp in cps:
            cp.wait()
        out_ref[...] = (chip_sum[...] + got[0]) + (got[1] + got[2])

    return pl.pallas_call(
        body, name="all_reduce_small", out_shape=jax.ShapeDtypeStruct(buf.shape, F32),
        in_specs=[pl.BlockSpec(memory_space=pltpu.VMEM)], out_specs=pl.BlockSpec(memory_space=pltpu.VMEM),
        scratch_shapes=[pltpu.VMEM((r, 128), F32), pltpu.VMEM((r, 128), F32), pltpu.VMEM((3, r, 128), F32),
                        pltpu.SemaphoreType.DMA((4,)), pltpu.SemaphoreType.DMA((4,))],
        compiler_params=_cparams(None, VMEM_LIMIT),
    )(buf)


ELEMENTWISE_BLOCK_BYTES = 1 << 20


def _view2d(a):
    cols = a.shape[-1] if a.ndim > 1 else 128
    return a.reshape(-1, cols)


def _row_block(rows, cols):
    want = max(8, ELEMENTWISE_BLOCK_BYTES // (4 * cols))
    br = rows
    while br > want and br % 2 == 0 and (br // 2) % 8 == 0:
        br //= 2
    return br


def _sum_in_order(terms, name):
    shape = terms[0].shape
    v = [_view2d(t) for t in terms]
    rows, cols = v[0].shape
    br = _row_block(rows, cols)

    def body(*refs):
        acc = refs[0][...]
        for rf in refs[1:len(terms)]:
            acc = acc + rf[...]
        refs[len(terms)][...] = acc

    blk = pl.BlockSpec((br, cols), lambda i: (i, 0))
    out = pl.pallas_call(body, name=name, grid=(rows // br,), in_specs=[blk] * len(terms), out_specs=blk,
                         out_shape=jax.ShapeDtypeStruct((rows, cols), F32),
                         compiler_params=_cparams(("parallel",)))(*v)
    return out.reshape(shape)


def _adamw(w, g, m, v, name):
    shape = w.shape
    ops = [_view2d(a) for a in (w, g, m, v)]
    rows, cols = ops[0].shape
    br = _row_block(rows, cols)
    c1 = 1.0 - ADAM_B1 ** ADAM_STEP
    c2 = 1.0 - ADAM_B2 ** ADAM_STEP

    def body(w_ref, g_ref, m_ref, v_ref, d_ref, nm_ref, nv_ref):
        g_ = g_ref[...]
        nm = ADAM_B1 * m_ref[...] + (1.0 - ADAM_B1) * g_
        nv = ADAM_B2 * v_ref[...] + (1.0 - ADAM_B2) * (g_ * g_)
        d_ref[...] = -ADAM_LR * ((nm / c1) / (jnp.sqrt(nv / c2) + ADAM_EPS) + ADAM_WD * w_ref[...])
        nm_ref[...] = nm
        nv_ref[...] = nv

    blk = pl.BlockSpec((br, cols), lambda i: (i, 0))
    outs = pl.pallas_call(body, name=name, grid=(rows // br,), in_specs=[blk] * 4, out_specs=(blk, blk, blk),
                          out_shape=tuple(jax.ShapeDtypeStruct((rows, cols), F32) for _ in range(3)),
                          compiler_params=_cparams(("parallel",)))(*ops)
    return tuple(o.reshape(shape) for o in outs)


def _pack(arrays):
    flat = jnp.concatenate([a.reshape(-1) for a in arrays])
    pad = (-flat.shape[0]) % (8 * 128)
    return jnp.pad(flat, (0, pad)).reshape(-1, 128)


def _unpack(buf, shapes):
    flat = buf.reshape(-1)
    out, o = [], 0
    for s in shapes:
        n = int(np.prod(s))
        out.append(flat[o:o + n].reshape(s))
        o += n
    return out


WEIGHTS = ["c_ctx", "norm_g", "w_mod", "b_mod", "w_in", "conv_w", "conv_b", "lru_wa", "lru_ba", "lru_wx", "lru_bx",
           "lru_lambda", "attn_sink", "q_norm_g", "k_norm_g", "w_branch", "w_out", "final_g"]
BIG = {"w_mod": 2, "w_in": 2, "w_branch": 2, "w_out": 1}
SMALL_SHARDED = ["conv_w", "lru_ba", "lru_bx", "lru_lambda"]
REPLICATED = [n for n in WEIGHTS if n not in BIG and n not in SMALL_SHARDED]


def kernel(x, c, ctx, c_ctx, norm_g, w_mod, b_mod, w_in, conv_w, conv_b, lru_wa, lru_ba, lru_wx, lru_bx, lru_lambda, attn_sink, q_norm_g, k_norm_g, w_branch, w_out, final_g, loss_target, m_c_ctx, m_norm_g, m_w_mod, m_b_mod, m_w_in, m_conv_w, m_conv_b, m_lru_wa, m_lru_ba, m_lru_wx, m_lru_bx, m_lru_lambda, m_attn_sink, m_q_norm_g, m_k_norm_g, m_w_branch, m_w_out, m_final_g, v_c_ctx, v_norm_g, v_w_mod, v_b_mod, v_w_in, v_conv_w, v_conv_b, v_lru_wa, v_lru_ba, v_lru_wx, v_lru_bx, v_lru_lambda, v_attn_sink, v_q_norm_g, v_k_norm_g, v_w_branch, v_w_out, v_final_g):
    args = dict(locals())
    w = {n: args[n] for n in WEIGHTS}
    mom = {n: args["m_" + n] for n in WEIGHTS}
    var = {n: args["v_" + n] for n in WEIGHTS}
    depth, d = norm_g.shape
    chip = 2 * lax.axis_index("x") + lax.axis_index("y")

    big_names = list(BIG)
    small_shard = jnp.concatenate([w[n] for n in SMALL_SHARDED], axis=1)
    gathered = _gather_chips([w[n].astype(BF16) for n in big_names] + [small_shard],
                             [BIG[n] for n in big_names] + [2], "gather_weights")
    whole = dict(w)
    whole.update(dict(zip(big_names, gathered[:-1])))
    o = 0
    for n in SMALL_SHARDED:
        rows = w[n].shape[1]
        whole[n] = gathered[-1][:, o:o + rows]
        o += rows
    layers = [_layer_params(li, whole) for li in range(depth)]

    loss_local, grad_x, g_c_ctx, g_final, lgrads = _local_step(x, c, ctx, loss_target, c_ctx, final_g, layers)
    loss = lax.psum(loss_local, ("x", "y", "c"))
    full = {n: jnp.stack([lg[n] for lg in lgrads]) for n in lgrads[0]}
    full["w_in"] = _reorder_in_cols(full["w_in"], d, inverse=True)
    full["c_ctx"], full["final_g"] = g_c_ctx, g_final

    kept, got = _split_cores([full[n] for n in big_names], "grad_split_cores")
    chip_part = [_sum_in_order([a, b_], "grad_chip_sum") for a, b_ in zip(kept, got)]
    own, recv = _scatter_chips(chip_part, [BIG[n] for n in big_names], "grad_scatter_chips")
    half = [_sum_in_order([a, r[0], r[1], r[2]], "grad_total") for a, r in zip(own, recv)]
    grad = dict(zip(big_names, _join_cores(half, "grad_join_cores")))

    small_names = REPLICATED + SMALL_SHARDED
    reduced = _unpack(_all_reduce_small(_pack([full[n] for n in small_names])), [full[n].shape for n in small_names])
    for n, g in zip(small_names, reduced):
        if n in SMALL_SHARDED:
            sz = w[n].shape[-1]
            g = lax.dynamic_slice_in_dim(g, chip * sz, sz, axis=g.ndim - 1)
        grad[n] = g

    delta, new_m, new_v = {}, {}, {}
    for n in big_names:
        delta[n], new_m[n], new_v[n] = _adamw(w[n], grad[n], mom[n], var[n], "adamw_" + n)
    shapes = [w[n].shape for n in small_names]
    packed = _adamw(_pack([w[n] for n in small_names]), _pack([grad[n] for n in small_names]),
                    _pack([mom[n] for n in small_names]), _pack([var[n] for n in small_names]), "adamw_small")
    for res, p in zip((delta, new_m, new_v), packed):
        res.update(dict(zip(small_names, _unpack(p, shapes))))

    return (loss, grad_x, *[grad[n] for n in WEIGHTS], *[delta[n] for n in WEIGHTS],
            *[new_m[n] for n in WEIGHTS], *[new_v[n] for n in WEIGHTS])
```

```python
import functools

import jax
import jax.numpy as jnp
import numpy as np
from jax import lax
from jax.experimental import pallas as pl
from jax.experimental.pallas import tpu as pltpu

F32 = jnp.float32
BF16 = jnp.bfloat16

HEAD_DIM = 128
GROUP = 4
LRU_BLOCK_W = 64
LRU_C = 8.0
WINDOW = 128
GRID_W = 64
ROPE_THETA = 10000.0
EPS = 1e-6
NEG_INF = -1e30
ADAM_LR, ADAM_B1, ADAM_B2, ADAM_EPS, ADAM_WD, ADAM_STEP = 0.001, 0.9, 0.999, 1e-08, 0.01, 10

ROW_BLOCK = 256
LRU_LANES = 128
LRU_CHUNK = 128
DENSE_FWD_BQ = 256
LOG2E = 1.4426950408889634
ATT_BQ = 256
ATT_BK = 256
WIN_BQ = 128
WIN_SPAN = WIN_BQ + 2 * WINDOW
MOD_ROWS = 16
VMEM_LIMIT = 56 * 1024 * 1024

MESH = pl.DeviceIdType.MESH


def _cparams(sem=None, vmem=None):
    kw = {}
    if sem is not None:
        kw["dimension_semantics"] = sem
    if vmem is not None:
        kw["vmem_limit_bytes"] = vmem
    return pltpu.CompilerParams(**kw)


def _sigmoid(v):
    return 1.0 / (1.0 + jnp.exp(-v))


def _silu(v):
    return v * _sigmoid(v)


def _dsilu(v):
    s = _sigmoid(v)
    return s * (1.0 + v * (1.0 - s))


def _expm1(x):
    poly = x * (1.0 + x * (0.5 + x * (1.0 / 6 + x * (1.0 / 24 + x * (1.0 / 120 + x * (1.0 / 720 + x * (1.0 / 5040)))))))
    return jnp.where(jnp.abs(x) < 0.25, poly, jnp.exp(x) - 1.0)


def _log1p(y):
    u = 1.0 + y
    d = u - 1.0
    return jnp.where(d == 0.0, y, jnp.log(u) * (y / jnp.where(d == 0.0, 1.0, d)))


def _softplus(x):
    return jnp.maximum(x, 0.0) + _log1p(jnp.exp(-jnp.abs(x)))


def _dot(a, b):
    return jnp.dot(a, b, preferred_element_type=F32)


def _dot_nt(a, b):
    return lax.dot_general(a, b, (((1,), (1,)), ((), ())), preferred_element_type=F32)


def _dot_tn(a, b):
    return lax.dot_general(a, b, (((0,), (0,)), ((), ())), preferred_element_type=F32)


def _colsum(v):
    return jnp.sum(v, axis=0, keepdims=True)


def _layout(d_model):
    kvw = (d_model // HEAD_DIM // GROUP) * HEAD_DIM
    names = ["gA", "gB", "gC", "mA", "mB", "mC", "uA", "qB", "qC", "kB", "vB", "kC", "vC"]
    widths = [d_model] * 9 + [kvw] * 4
    off, o = {}, 0
    for n, w in zip(names, widths):
        off[n] = o
        o += w
    return off, o


def _orig_segments(d_model):
    kvw = (d_model // HEAD_DIM // GROUP) * HEAD_DIM
    names = ["uA", "gA", "qB", "kB", "vB", "gB", "qC", "kC", "vC", "gC", "mA", "mB", "mC"]
    widths = [d_model, d_model, d_model, kvw, kvw, d_model, d_model, kvw, kvw, d_model, d_model, d_model, d_model]
    out, o = [], 0
    for n, w in zip(names, widths):
        out.append((n, o, w))
        o += w
    return out


def _matmul(a, b, *, ta=False, tb=False, out_dtype=F32, bm, bn, bk, name):
    (kdim, m) = a.shape if ta else a.shape[::-1]
    (n, kdim2) = b.shape if tb else b.shape[::-1]
    assert kdim == kdim2 and m % bm == 0 and n % bn == 0 and kdim % bk == 0, (a.shape, b.shape, bm, bn, bk)
    nk = kdim // bk
    dims = (((0 if ta else 1,), (1 if tb else 0,)), ((), ()))

    def body(a_ref, b_ref, o_ref, *scratch):
        r = lax.dot_general(a_ref[...].astype(BF16), b_ref[...].astype(BF16), dims, preferred_element_type=F32)
        if nk == 1:
            o_ref[...] = r.astype(out_dtype)
        else:
            acc = scratch[0]
            k = pl.program_id(2)

            @pl.when(k == 0)
            def _():
                acc[...] = r

            @pl.when(k > 0)
            def _():
                acc[...] += r

            @pl.when(k == nk - 1)
            def _():
                o_ref[...] = acc[...].astype(out_dtype)

    a_spec = pl.BlockSpec((bk, bm), lambda i, j, k: (k, i)) if ta else pl.BlockSpec((bm, bk), lambda i, j, k: (i, k))
    b_spec = pl.BlockSpec((bn, bk), lambda i, j, k: (j, k)) if tb else pl.BlockSpec((bk, bn), lambda i, j, k: (k, j))
    return pl.pallas_call(
        body, name=name, grid=(m // bm, n // bn, nk),
        in_specs=[a_spec, b_spec], out_specs=pl.BlockSpec((bm, bn), lambda i, j, k: (i, j)),
        out_shape=jax.ShapeDtypeStruct((m, n), out_dtype),
        scratch_shapes=[pltpu.VMEM((bm, bn), F32)] if nk > 1 else [],
        compiler_params=_cparams(("parallel", "parallel", "arbitrary"), VMEM_LIMIT),
    )(a, b)


def _mod_fwd(c16, w_mod, b_mod):
    d3 = w_mod.shape[1]

    def body(c_ref, w_ref, b_ref, o_ref):
        o_ref[...] = _dot(_silu(c_ref[...]).astype(BF16), w_ref[...]) + b_ref[...]

    return pl.pallas_call(body, name="mod_fwd", out_shape=jax.ShapeDtypeStruct((MOD_ROWS, d3), F32),
                          compiler_params=_cparams(None, VMEM_LIMIT))(c16, w_mod, b_mod)


def _mod_bwd(c16, dmod16, w_mod):
    d, d3 = w_mod.shape

    def body(c_ref, g_ref, w_ref, dw_ref, db_ref, dc_ref):
        c = c_ref[...]
        g = g_ref[...]
        gb = g.astype(BF16)
        dw_ref[...] = _dot_tn(_silu(c).astype(BF16), gb)
        db_ref[...] = _colsum(g)
        dc_ref[...] = _dot_nt(gb, w_ref[...]) * _dsilu(c)

    return pl.pallas_call(
        body, name="mod_bwd",
        out_shape=(jax.ShapeDtypeStruct((d, d3), F32), jax.ShapeDtypeStruct((1, d3), F32),
                   jax.ShapeDtypeStruct((MOD_ROWS, d), F32)),
        compiler_params=_cparams(None, VMEM_LIMIT))(c16, dmod16, w_mod)


def _row_kind(t, lb):
    return jnp.where(t >= lb, 1, 0)


def _norm_mod_fwd(x3, g, modsel, ctx_len):
    b, t, d = x3.shape
    bt = ROW_BLOCK
    lb = ctx_len // bt

    def body(x_ref, g_ref, m_ref, h_ref):
        x = x_ref[0]
        rstd = lax.rsqrt(jnp.mean(x * x, axis=-1, keepdims=True) + EPS)
        y = x * rstd * g_ref[...]
        h_ref[0] = (y * (1.0 + m_ref[0, 0, 1:2, :]) + m_ref[0, 0, 0:1, :]).astype(BF16)

    return pl.pallas_call(
        body, name="norm_mod_fwd", grid=(b, t // bt),
        in_specs=[pl.BlockSpec((1, bt, d), lambda i, j: (i, j, 0)),
                  pl.BlockSpec((1, d), lambda i, j: (0, 0)),
                  pl.BlockSpec((1, 1, 8, d), lambda i, j: (i, _row_kind(j, lb), 0, 0))],
        out_specs=pl.BlockSpec((1, bt, d), lambda i, j: (i, j, 0)),
        out_shape=jax.ShapeDtypeStruct((b, t, d), BF16),
        compiler_params=_cparams(("parallel", "arbitrary")),
    )(x3, g, modsel)


def _norm_mod_bwd(dh3, x3, g, modsel, dres3, ctx_len):
    b, t, d = x3.shape
    bt = ROW_BLOCK
    lb = ctx_len // bt

    def body(dh_ref, x_ref, g_ref, m_ref, dres_ref, dx_ref, acc_ref):
        j = pl.program_id(1)
        x = x_ref[0]
        dh = dh_ref[0]
        g_row = g_ref[...]
        rstd = lax.rsqrt(jnp.mean(x * x, axis=-1, keepdims=True) + EPS)
        xhat = x * rstd
        dhpre = dh * (1.0 + m_ref[0, 0, 1:2, :])
        dxhat = dhpre * g_row
        dx = rstd * (dxhat - xhat * jnp.mean(dxhat * xhat, axis=-1, keepdims=True))
        dx_ref[0] = dx + dres_ref[0]

        @pl.when((j == 0) | (j == lb))
        def _():
            acc_ref[...] = jnp.zeros_like(acc_ref)

        acc_ref[0, 0, 0:1, :] += _colsum(dh)
        acc_ref[0, 0, 1:2, :] += _colsum(dh * (xhat * g_row))
        acc_ref[0, 0, 2:3, :] += _colsum(dhpre * xhat)

    blk = pl.BlockSpec((1, bt, d), lambda i, j: (i, j, 0))
    return pl.pallas_call(
        body, name="norm_mod_bwd", grid=(b, t // bt),
        in_specs=[blk, blk, pl.BlockSpec((1, d), lambda i, j: (0, 0)),
                  pl.BlockSpec((1, 1, 8, d), lambda i, j: (i, _row_kind(j, lb), 0, 0)), blk],
        out_specs=(blk, pl.BlockSpec((1, 1, 8, d), lambda i, j: (i, _row_kind(j, lb), 0, 0))),
        out_shape=(jax.ShapeDtypeStruct((b, t, d), F32), jax.ShapeDtypeStruct((b, 2, 8, d), F32)),
        compiler_params=_cparams(("parallel", "arbitrary")),
    )(dh3, x3, g, modsel, dres3)


def _shifted_rows(ref, c, off, ctx_len, total):
    ct = LRU_CHUNK
    r0 = pl.multiple_of(c * ct, ct)
    x0 = ref[pl.ds(r0, ct), :]
    row = lax.broadcasted_iota(jnp.int32, x0.shape, 0)
    if off < 0:
        k = -off
        has = jnp.logical_and(r0 != 0, r0 != ctx_len)
        p0 = pl.multiple_of(jnp.maximum(r0 - 8, 0), 8)
        edge = jnp.where(has, ref[pl.ds(p0, 8), :], 0.0)
        out = pltpu.roll(x0, k, 0)
        for j in range(k):
            out = jnp.where(row == j, edge[8 - k + j:8 - k + j + 1, :], out)
    else:
        k = off
        has = jnp.logical_and(r0 + ct != ctx_len, r0 + ct != total)
        n0 = pl.multiple_of(jnp.minimum(r0 + ct, total - 8), 8)
        edge = jnp.where(has, ref[pl.ds(n0, 8), :], 0.0)
        out = pltpu.roll(x0, ct - k, 0)
        for j in range(k):
            out = jnp.where(row == ct - k + j, edge[j:j + 1, :], out)
    return out


def _chunk_scan(a, b, reverse):
    n = a.shape[0]
    row = lax.broadcasted_iota(jnp.int32, a.shape, 0)
    s = 1
    while s < n:
        if reverse:
            a_s, b_s, ok = pltpu.roll(a, n - s, 0), pltpu.roll(b, n - s, 0), row < n - s
        else:
            a_s, b_s, ok = pltpu.roll(a, s, 0), pltpu.roll(b, s, 0), row >= s
        b = jnp.where(ok, a * b_s + b, b)
        a = jnp.where(ok, a * a_s, a)
        s *= 2
    return a, b


def _lru_order(d, s, n_ctx, n_all):
    if d == 0:
        return s
    return jnp.where(s < n_ctx, n_ctx - 1 - s, n_all - 1 - (s - n_ctx))


def _lru_gates(u, wa, ba, wx, bx, sp):
    ub = u.astype(BF16)
    r = _sigmoid(_dot(ub, wa) + ba)
    i = _sigmoid(_dot(ub, wx) + bx)
    log_a = (-LRU_C * sp) * r
    a = jnp.exp(log_a)
    sf = jnp.sqrt(-_expm1(2.0 * log_a))
    return ub, r, i, a, sf


def _lru_specs(t, n_lane_blocks_offset):
    ln = LRU_LANES
    return [
        pl.BlockSpec((4, ln), lambda i, j: (0, j)),
        pl.BlockSpec((1, ln), lambda i, j: (0, j)),
        pl.BlockSpec((2, 1, ln, ln), lambda i, j: (0, j, 0, 0)),
        pl.BlockSpec((2, ln), lambda i, j: (0, j)),
        pl.BlockSpec((2, 1, ln, ln), lambda i, j: (0, j, 0, 0)),
        pl.BlockSpec((2, ln), lambda i, j: (0, j)),
        pl.BlockSpec((2, ln), lambda i, j: (0, j)),
    ]


def _lru_conv(ua_ref, cw_ref, cb_ref, u_s, ctx_len, total):
    ct = LRU_CHUNK

    def conv(c, _):
        r0 = pl.multiple_of(c * ct, ct)
        u = (cw_ref[0:1, :] * _shifted_rows(ua_ref, c, -2, ctx_len, total)
             + cw_ref[1:2, :] * _shifted_rows(ua_ref, c, -1, ctx_len, total)
             + cw_ref[2:3, :] * ua_ref[pl.ds(r0, ct), :]
             + cw_ref[3:4, :] * _shifted_rows(ua_ref, c, 1, ctx_len, total) + cb_ref[...])
        u_s[pl.ds(r0, ct), :] = u
        return 0

    lax.fori_loop(0, total // ct, conv, 0)


def _lru_fwd(proj3, col0, conv_w, conv_b, wa_bd, ba, wx_bd, bx, lam, ctx_len):
    b, t, _ = proj3.shape
    d = conv_w.shape[1]
    ln, ct = LRU_LANES, LRU_CHUNK
    n_all, n_ctx = t // ct, ctx_len // ct
    cb0 = col0 // ln

    def body(ua_ref, cw_ref, cb_ref, wa_ref, ba_ref, wx_ref, bx_ref, lam_ref, y_ref, u_s):
        ua = ua_ref.at[0]
        _lru_conv(ua, cw_ref, cb_ref, u_s, ctx_len, t)
        for dr in (0, 1):
            sp = _softplus(-lam_ref[dr:dr + 1, :])
            wa, wx = wa_ref[dr, 0], wx_ref[dr, 0]
            ba_row, bx_row = ba_ref[dr:dr + 1, :], bx_ref[dr:dr + 1, :]

            def step(s, carry, dr=dr, sp=sp, wa=wa, wx=wx, ba_row=ba_row, bx_row=bx_row):
                c = _lru_order(dr, s, n_ctx, n_all)
                r0 = pl.multiple_of(c * ct, ct)
                u = u_s[pl.ds(r0, ct), :]
                _, _, i, a, sf = _lru_gates(u, wa, ba_row, wx, bx_row, sp)
                aa, h0 = _chunk_scan(a, sf * (i * u), reverse=(dr == 1))
                h = h0 + aa * carry
                if dr == 0:
                    y_ref[0, pl.ds(r0, ct), :] = h
                    return h[ct - 1:ct, :]
                y_ref[0, pl.ds(r0, ct), :] += h
                return h[0:1, :]

            lax.fori_loop(0, n_all, step, jnp.zeros((1, ln), F32))

    return pl.pallas_call(
        body, name="lru_fwd", grid=(b, d // ln),
        in_specs=[pl.BlockSpec((1, t, ln), lambda i, j: (i, 0, cb0 + j))] + _lru_specs(t, cb0),
        out_specs=pl.BlockSpec((1, t, ln), lambda i, j: (i, 0, j)),
        out_shape=jax.ShapeDtypeStruct((b, t, d), F32),
        scratch_shapes=[pltpu.VMEM((t, ln), F32)],
        compiler_params=_cparams(("parallel", "parallel"), VMEM_LIMIT),
    )(proj3, conv_w, conv_b, wa_bd, ba, wx_bd, bx, lam)


def _lru_bwd(dproj3, proj3, col0, dy3, conv_w, conv_b, wa_bd, ba, wx_bd, bx, lam, ctx_len):
    b, t, _ = proj3.shape
    d = conv_w.shape[1]
    ln, ct = LRU_LANES, LRU_CHUNK
    n_all, n_ctx = t // ct, ctx_len // ct
    cb0 = col0 // ln

    def body(dproj_hbm, ua_ref, dy_ref, cw_ref, cb_ref, wa_ref, ba_ref, wx_ref, bx_ref, lam_ref,
             dua_ref, vec_ref, dwa_ref, dwx_ref, u_s, h_s, du_s):
        del dproj_hbm
        ua = ua_ref.at[0]
        _lru_conv(ua, cw_ref, cb_ref, u_s, ctx_len, t)
        du_s[...] = jnp.zeros_like(du_s)
        vec_ref[...] = jnp.zeros_like(vec_ref)
        for dr in (0, 1):
            sp = _softplus(-lam_ref[dr:dr + 1, :])
            wa, wx = wa_ref[dr, 0], wx_ref[dr, 0]
            ba_row, bx_row = ba_ref[dr:dr + 1, :], bx_ref[dr:dr + 1, :]

            def fwd(s, carry, dr=dr, sp=sp, wa=wa, wx=wx, ba_row=ba_row, bx_row=bx_row):
                c = _lru_order(dr, s, n_ctx, n_all)
                r0 = pl.multiple_of(c * ct, ct)
                u = u_s[pl.ds(r0, ct), :]
                _, _, i, a, sf = _lru_gates(u, wa, ba_row, wx, bx_row, sp)
                aa, h0 = _chunk_scan(a, sf * (i * u), reverse=(dr == 1))
                h = h0 + aa * carry
                h_s[pl.ds(r0, ct), :] = h
                return h[ct - 1:ct, :] if dr == 0 else h[0:1, :]

            lax.fori_loop(0, n_all, fwd, jnp.zeros((1, ln), F32))

            def bwd(sr, carry, dr=dr, sp=sp, wa=wa, wx=wx, ba_row=ba_row, bx_row=bx_row):
                gc, dwa_acc, dwx_acc, vacc = carry
                c = _lru_order(dr, n_all - 1 - sr, n_ctx, n_all)
                r0 = pl.multiple_of(c * ct, ct)
                u = u_s[pl.ds(r0, ct), :]
                h = h_s[pl.ds(r0, ct), :]
                dy = dy_ref[0, pl.ds(r0, ct), :]
                ub, r, i, a, sf = _lru_gates(u, wa, ba_row, wx, bx_row, sp)
                row = lax.broadcasted_iota(jnp.int32, a.shape, 0)
                if dr == 0:
                    alpha = jnp.where(row == ct - 1, 1.0, pltpu.roll(a, ct - 1, 0))
                    aa, g0 = _chunk_scan(alpha, dy, reverse=True)
                    g = g0 + aa * gc
                    gc_new = a[0:1, :] * g[0:1, :]
                    p0 = pl.multiple_of(jnp.maximum(r0 - 8, 0), 8)
                    edge = jnp.where(r0 != 0, h_s[pl.ds(p0, 8), :], 0.0)[7:8, :]
                    h_prev = jnp.where(row == 0, edge, pltpu.roll(h, 1, 0))
                else:
                    alpha = jnp.where(row == 0, 1.0, pltpu.roll(a, 1, 0))
                    aa, g0 = _chunk_scan(alpha, dy, reverse=False)
                    g = g0 + aa * gc
                    gc_new = a[ct - 1:ct, :] * g[ct - 1:ct, :]
                    r_end = r0 + ct
                    n0 = pl.multiple_of(jnp.where(r_end == t, 0, jnp.minimum(r_end, t - 8)), 8)
                    edge = jnp.where(r_end != ctx_len, h_s[pl.ds(n0, 8), :], 0.0)[0:1, :]
                    h_prev = jnp.where(row == ct - 1, edge, pltpu.roll(h, ct - 1, 0))
                da = g * h_prev
                iu = i * u
                diu = g * sf
                dlog_a = da * a - (g * iu) * (a * a) / sf
                dpre_r = (dlog_a * (-LRU_C * sp)) * (r * (1.0 - r))
                dpre_i = (diu * u) * (i * (1.0 - i))
                dpr_b, dpi_b = dpre_r.astype(BF16), dpre_i.astype(BF16)
                du = diu * i + _dot_nt(dpr_b, wa) + _dot_nt(dpi_b, wx)
                du_s[pl.ds(r0, ct), :] += du
                dwa_acc = dwa_acc + _dot_tn(ub, dpr_b)
                dwx_acc = dwx_acc + _dot_tn(ub, dpi_b)
                vacc = (vacc[0] + _colsum(dpre_r), vacc[1] + _colsum(dpre_i), vacc[2] + _colsum(dlog_a * (-LRU_C * r)))
                return gc_new, dwa_acc, dwx_acc, vacc

            zrow = jnp.zeros((1, ln), F32)
            zmat = jnp.zeros((ln, ln), F32)
            _, dwa_acc, dwx_acc, vacc = lax.fori_loop(0, n_all, bwd, (zrow, zmat, zmat, (zrow, zrow, zrow)))
            dwa_ref[0, dr, 0] = dwa_acc
            dwx_ref[0, dr, 0] = dwx_acc
            vec_ref[0, 5 + dr:6 + dr, :] = vacc[0]
            vec_ref[0, 7 + dr:8 + dr, :] = vacc[1]
            vec_ref[0, 9 + dr:10 + dr, :] = vacc[2]

        def conv_bwd(c, acc):
            r0 = pl.multiple_of(c * ct, ct)
            du = du_s[pl.ds(r0, ct), :]
            dua = (cw_ref[0:1, :] * _shifted_rows(du_s, c, 2, ctx_len, t)
                   + cw_ref[1:2, :] * _shifted_rows(du_s, c, 1, ctx_len, t)
                   + cw_ref[2:3, :] * du
                   + cw_ref[3:4, :] * _shifted_rows(du_s, c, -1, ctx_len, t))
            dua_ref[0, pl.ds(r0, ct), :] = dua.astype(BF16)
            return (acc[0] + _colsum(du * _shifted_rows(ua, c, -2, ctx_len, t)),
                    acc[1] + _colsum(du * _shifted_rows(ua, c, -1, ctx_len, t)),
                    acc[2] + _colsum(du * ua[pl.ds(r0, ct), :]),
                    acc[3] + _colsum(du * _shifted_rows(ua, c, 1, ctx_len, t)),
                    acc[4] + _colsum(du))

        zrow = jnp.zeros((1, ln), F32)
        acc = lax.fori_loop(0, n_all, conv_bwd, (zrow,) * 5)
        for k in range(5):
            vec_ref[0, k:k + 1, :] = acc[k]

    ng = d // ln
    return pl.pallas_call(
        body, name="lru_bwd", grid=(b, ng),
        in_specs=[pl.BlockSpec(memory_space=pl.ANY),
                  pl.BlockSpec((1, t, ln), lambda i, j: (i, 0, cb0 + j)),
                  pl.BlockSpec((1, t, ln), lambda i, j: (i, 0, j))] + _lru_specs(t, cb0),
        out_specs=(pl.BlockSpec((1, t, ln), lambda i, j: (i, 0, cb0 + j)),
                   pl.BlockSpec((1, 16, ln), lambda i, j: (i, 0, j)),
                   pl.BlockSpec((1, 2, 1, ln, ln), lambda i, j: (i, 0, j, 0, 0)),
                   pl.BlockSpec((1, 2, 1, ln, ln), lambda i, j: (i, 0, j, 0, 0))),
        out_shape=(jax.ShapeDtypeStruct(dproj3.shape, dproj3.dtype),
                   jax.ShapeDtypeStruct((b, 16, d), F32),
                   jax.ShapeDtypeStruct((b, 2, ng, ln, ln), F32),
                   jax.ShapeDtypeStruct((b, 2, ng, ln, ln), F32)),
        scratch_shapes=[pltpu.VMEM((t, ln), F32)] * 3,
        input_output_aliases={0: 0},
        compiler_params=_cparams(("parallel", "parallel"), VMEM_LIMIT),
    )(dproj3, proj3, dy3, conv_w, conv_b, wa_bd, ba, wx_bd, bx, lam)


def _rope_tables(ctx_len, seq):
    p = HEAD_DIM // 4
    inv = ROPE_THETA ** (-jnp.arange(p, dtype=F32) / p)
    tok = jnp.arange(seq)
    ang_r = (tok // GRID_W)[:, None] * inv
    ang_c = (tok % GRID_W)[:, None] * inv
    cos = jnp.concatenate([jnp.cos(ang_r)] * 2 + [jnp.cos(ang_c)] * 2, axis=1)
    sin = jnp.concatenate([-jnp.sin(ang_r), jnp.sin(ang_r), -jnp.sin(ang_c), jnp.sin(ang_c)], axis=1)
    cos = jnp.concatenate([jnp.ones((ctx_len, HEAD_DIM), F32), cos], axis=0)
    sin = jnp.concatenate([jnp.zeros((ctx_len, HEAD_DIM), F32), sin], axis=0)
    return cos, sin


def _swap_halves(v):
    lane = lax.broadcasted_iota(jnp.int32, v.shape, 1)
    return jnp.where((lane & 63) < 32, pltpu.roll(v, 96, 1), pltpu.roll(v, 32, 1))


def _head_rstd(v):
    return lax.rsqrt(jnp.mean(v * v, axis=-1, keepdims=True) + EPS)


QKV_BLOCK = GROUP * HEAD_DIM


def _prep_fwd(proj3, qcol, kvcol, d, cos, sin, gq, gk, use_norm):
    b, t, _ = proj3.shape
    bt, wb = ROW_BLOCK, QKV_BLOCK
    nqb = d // wb
    assert qcol % wb == 0 and kvcol % wb == 0 and d // HEAD_DIM // GROUP == 2
    qb0, kvb = qcol // wb, kvcol // wb

    def body(p_ref, cos_ref, sin_ref, gq_ref, gk_ref, o_ref):
        s = pl.program_id(2)
        c, sn = cos_ref[...], sin_ref[...]

        def rope(v):
            return v * c + _swap_halves(v) * sn

        @pl.when(s < nqb)
        def _():
            for hh in range(GROUP):
                v = p_ref[0, :, hh * HEAD_DIM:(hh + 1) * HEAD_DIM]
                if use_norm:
                    v = v * _head_rstd(v) * gq_ref[...]
                o_ref[0, :, hh * HEAD_DIM:(hh + 1) * HEAD_DIM] = rope(v).astype(BF16)

        @pl.when(s == nqb)
        def _():
            for hh in range(2):
                v = p_ref[0, :, hh * HEAD_DIM:(hh + 1) * HEAD_DIM]
                if use_norm:
                    v = v * _head_rstd(v) * gk_ref[...]
                o_ref[0, :, hh * HEAD_DIM:(hh + 1) * HEAD_DIM] = rope(v).astype(BF16)
            o_ref[0, :, 2 * HEAD_DIM:] = p_ref[0, :, 2 * HEAD_DIM:].astype(BF16)

    return pl.pallas_call(
        body, name="prep_fwd_norm" if use_norm else "prep_fwd", grid=(b, t // bt, nqb + 1),
        in_specs=[pl.BlockSpec((1, bt, wb), lambda i, j, s: (i, j, jnp.where(s < nqb, qb0 + s, kvb))),
                  pl.BlockSpec((bt, HEAD_DIM), lambda i, j, s: (j, 0)),
                  pl.BlockSpec((bt, HEAD_DIM), lambda i, j, s: (j, 0)),
                  pl.BlockSpec((1, HEAD_DIM), lambda i, j, s: (0, 0)),
                  pl.BlockSpec((1, HEAD_DIM), lambda i, j, s: (0, 0))],
        out_specs=pl.BlockSpec((1, bt, wb), lambda i, j, s: (i, j, s)),
        out_shape=jax.ShapeDtypeStruct((b, t, d + wb), BF16),
        compiler_params=_cparams(("parallel", "parallel", "arbitrary")),
    )(proj3, cos, sin, gq, gk)


def _prep_bwd(dproj3, dq3, dkt, dvt, proj3, qcol, kvcol, d, cos, sin, gq, gk, use_norm):
    b, t, _ = proj3.shape
    bt, wb = ROW_BLOCK, QKV_BLOCK
    nqb = d // wb
    qb0, kvb = qcol // wb, kvcol // wb
    kvh = dkt.shape[1]

    def body(dproj_hbm, dq_ref, dkt_ref, dvt_ref, p_ref, cos_ref, sin_ref, gq_ref, gk_ref, o_ref, gacc_ref):
        del dproj_hbm
        j, s = pl.program_id(1), pl.program_id(2)
        c, sn = cos_ref[...], sin_ref[...]

        @pl.when((j == 0) & (s == 0))
        def _():
            gacc_ref[...] = jnp.zeros_like(gacc_ref)

        def unrope(dv):
            return dv * c + _swap_halves(dv * sn)

        def head_bwd(dyv, xv, g_ref, acc_row):
            dyv = unrope(dyv)
            if not use_norm:
                return dyv
            rstd = _head_rstd(xv)
            xhat = xv * rstd
            gacc_ref[0, acc_row:acc_row + 1, :] += _colsum(dyv * xhat)
            dxhat = dyv * g_ref[...]
            return rstd * (dxhat - xhat * jnp.mean(dxhat * xhat, axis=-1, keepdims=True))

        @pl.when(s < nqb)
        def _():
            for hh in range(GROUP):
                sl = slice(hh * HEAD_DIM, (hh + 1) * HEAD_DIM)
                o_ref[0, :, sl] = head_bwd(dq_ref[0, :, sl], p_ref[0, :, sl], gq_ref, 0).astype(BF16)

        @pl.when(s == nqb)
        def _():
            for hh in range(kvh):
                sl = slice(hh * HEAD_DIM, (hh + 1) * HEAD_DIM)
                o_ref[0, :, sl] = head_bwd(dkt_ref[0, hh].T, p_ref[0, :, sl], gk_ref, 1).astype(BF16)
                sv = slice((kvh + hh) * HEAD_DIM, (kvh + hh + 1) * HEAD_DIM)
                o_ref[0, :, sv] = dvt_ref[0, hh].T.astype(BF16)

    col = lambda i, j, s: (i, j, jnp.where(s < nqb, qb0 + s, kvb))
    return pl.pallas_call(
        body, name="prep_bwd_norm" if use_norm else "prep_bwd", grid=(b, t // bt, nqb + 1),
        in_specs=[pl.BlockSpec(memory_space=pl.ANY),
                  pl.BlockSpec((1, bt, wb), lambda i, j, s: (i, j, jnp.minimum(s, nqb - 1))),
                  pl.BlockSpec((1, kvh, HEAD_DIM, bt), lambda i, j, s: (i, 0, 0, j)),
                  pl.BlockSpec((1, kvh, HEAD_DIM, bt), lambda i, j, s: (i, 0, 0, j)),
                  pl.BlockSpec((1, bt, wb), col),
                  pl.BlockSpec((bt, HEAD_DIM), lambda i, j, s: (j, 0)),
                  pl.BlockSpec((bt, HEAD_DIM), lambda i, j, s: (j, 0)),
                  pl.BlockSpec((1, HEAD_DIM), lambda i, j, s: (0, 0)),
                  pl.BlockSpec((1, HEAD_DIM), lambda i, j, s: (0, 0))],
        out_specs=(pl.BlockSpec((1, bt, wb), col), pl.BlockSpec((1, 8, HEAD_DIM), lambda i, j, s: (i, 0, 0))),
        out_shape=(jax.ShapeDtypeStruct(dproj3.shape, dproj3.dtype), jax.ShapeDtypeStruct((b, 8, HEAD_DIM), F32)),
        input_output_aliases={0: 0},
        compiler_params=_cparams(("parallel", "arbitrary", "arbitrary")),
    )(dproj3, dq3, dkt, dvt, proj3, cos, sin, gq, gk)


def _stack_heads(ref, dtype=None):
    parts = [ref[0, :, g * HEAD_DIM:(g + 1) * HEAD_DIM] for g in range(GROUP)]
    v = jnp.concatenate(parts, axis=0)
    return v if dtype is None else v.astype(dtype)


def _unstack_heads(ref, v, bq):
    for g in range(GROUP):
        ref[0, :, g * HEAD_DIM:(g + 1) * HEAD_DIM] = v[g * bq:(g + 1) * bq, :]


def _attn_specs(t, d, bq):
    kvh = d // HEAD_DIM // GROUP
    kc0 = d // HEAD_DIM
    q_spec = pl.BlockSpec((1, bq, QKV_BLOCK), lambda i, h, j: (i, j, h))
    k_spec = pl.BlockSpec((1, t, HEAD_DIM), lambda i, h, j: (i, 0, kc0 + h))
    v_spec = pl.BlockSpec((1, t, HEAD_DIM), lambda i, h, j: (i, 0, kc0 + kvh + h))
    lse_spec = pl.BlockSpec((1, GROUP, bq, HEAD_DIM), lambda i, h, j: (i, h, j, 0))
    kt_spec = pl.BlockSpec((1, 1, HEAD_DIM, t), lambda i, h, j: (i, h, 0, 0))
    return kvh, q_spec, k_spec, v_spec, lse_spec, kt_spec


SCALE = HEAD_DIM ** -0.5


def _attn_dense_fwd(qkv, d, ctx_len):
    b, t, _ = qkv.shape
    bq = DENSE_FWD_BQ
    lq = ctx_len // bq
    kvh, q_spec, k_spec, v_spec, lse_spec, _ = _attn_specs(t, d, bq)

    def body(q_ref, k_ref, v_ref, o_ref, lse_ref):
        i = pl.program_id(2)

        def attend(k, v):
            for g in range(GROUP):
                sl = slice(g * HEAD_DIM, (g + 1) * HEAD_DIM)
                s = _dot_nt(q_ref[0, :, sl], k)
                m = jnp.max(s, axis=1, keepdims=True)
                p = jnp.exp2((s - m) * (SCALE * LOG2E))
                l = jnp.sum(p, axis=1, keepdims=True)
                o_ref[0, :, sl] = _dot(p.astype(BF16), v) / l
                lse_ref[0, g] = jnp.broadcast_to(m * SCALE + jnp.log(l), (bq, HEAD_DIM))

        @pl.when(i < lq)
        def _():
            attend(k_ref[0, 0:ctx_len, :], v_ref[0, 0:ctx_len, :])

        @pl.when(i >= lq)
        def _():
            attend(k_ref[0], v_ref[0])

    return pl.pallas_call(
        body, name="attn_dense_fwd", grid=(b, kvh, t // bq),
        in_specs=[q_spec, k_spec, v_spec], out_specs=(q_spec, lse_spec),
        out_shape=(jax.ShapeDtypeStruct((b, t, d), F32), jax.ShapeDtypeStruct((b, kvh * GROUP, t, HEAD_DIM), F32)),
        compiler_params=_cparams(("parallel", "parallel", "arbitrary"), VMEM_LIMIT),
    )(qkv, qkv, qkv)


def _attn_dense_bwd(qkv, o3, do3, lse, d, ctx_len):
    b, t, _ = qkv.shape
    bq, bk = ATT_BQ, ATT_BK
    rows = GROUP * bq
    lq = ctx_len // bq
    kvh, q_spec, k_spec, v_spec, lse_spec, kt_spec = _attn_specs(t, d, bq)

    def body(q_ref, k_ref, v_ref, o_ref, do_ref, lse_ref, dq_ref, dkt_ref, dvt_ref, dq_s, qt_s, dot_s):
        i = pl.program_id(2)

        @pl.when(i == 0)
        def _():
            dkt_ref[...] = jnp.zeros_like(dkt_ref)
            dvt_ref[...] = jnp.zeros_like(dvt_ref)

        q4 = _stack_heads(q_ref)
        do4 = _stack_heads(do_ref)
        dd = jnp.sum(do4 * _stack_heads(o_ref), axis=1, keepdims=True)
        lse_col = lse_ref[0].reshape(rows, HEAD_DIM)[:, 0:1]
        do4b = do4.astype(BF16)
        qt_s[...] = q4.astype(F32).T.astype(BF16)
        dot_s[...] = do4.T.astype(BF16)
        dq_s[...] = jnp.zeros_like(dq_s)

        def step(j, _):
            k0 = pl.multiple_of(j * bk, bk)
            k = k_ref[0, pl.ds(k0, bk), :]
            p = jnp.exp(_dot_nt(q4, k) * SCALE - lse_col)
            dp = _dot_nt(do4b, v_ref[0, pl.ds(k0, bk), :])
            ds = (p * (dp - dd) * SCALE).astype(BF16)
            dq_s[...] += _dot(ds, k)
            dkt_ref[0, 0, :, pl.ds(k0, bk)] += _dot(qt_s[...], ds)
            dvt_ref[0, 0, :, pl.ds(k0, bk)] += _dot(dot_s[...], p.astype(BF16))
            return 0

        lax.fori_loop(0, jnp.where(i < lq, ctx_len // bk, t // bk), step, 0)
        _unstack_heads(dq_ref, dq_s[...], bq)

    return pl.pallas_call(
        body, name="attn_dense_bwd", grid=(b, kvh, t // bq),
        in_specs=[q_spec, k_spec, v_spec, q_spec, q_spec, lse_spec], out_specs=(q_spec, kt_spec, kt_spec),
        out_shape=(jax.ShapeDtypeStruct((b, t, d), F32), jax.ShapeDtypeStruct((b, kvh, HEAD_DIM, t), F32),
                   jax.ShapeDtypeStruct((b, kvh, HEAD_DIM, t), F32)),
        scratch_shapes=[pltpu.VMEM((rows, HEAD_DIM), F32), pltpu.VMEM((HEAD_DIM, rows), BF16),
                        pltpu.VMEM((HEAD_DIM, rows), BF16)],
        compiler_params=_cparams(("parallel", "parallel", "arbitrary"), VMEM_LIMIT),
    )(qkv, qkv, qkv, o3, do3, lse)


def _sink_column(sink_ref, h, bq):
    rowi = lax.broadcasted_iota(jnp.int32, (GROUP * bq, 1), 0)
    col = jnp.zeros((GROUP * bq, 1), F32)
    for g in range(GROUP):
        col = jnp.where((rowi >= g * bq) & (rowi < (g + 1) * bq), sink_ref[h * GROUP + g], col)
    return col


def _band(i, lq, ctx_len, t, bq):
    n = i - lq
    start = pl.multiple_of(jnp.clip(ctx_len + (n - 1) * bq, ctx_len, t - WIN_SPAN), bq)
    shape = (GROUP * bq, WIN_SPAN)
    kpos = start - ctx_len + lax.broadcasted_iota(jnp.int32, shape, 1)
    qpos = n * bq + (lax.broadcasted_iota(jnp.int32, shape, 0) & (bq - 1))
    return start, jnp.abs(kpos - qpos) <= WINDOW


def _attn_win_fwd(qkv, sink, d, ctx_len):
    b, t, _ = qkv.shape
    bq = WIN_BQ
    rows = GROUP * bq
    lq = ctx_len // bq
    kvh, q_spec, k_spec, v_spec, lse_spec, _ = _attn_specs(t, d, bq)

    def body(sink_ref, q_ref, k_ref, v_ref, o_ref, lse_ref):
        h, i = pl.program_id(1), pl.program_id(2)
        q4 = _stack_heads(q_ref)
        sink_col = _sink_column(sink_ref, h, bq)
        sc = _dot_nt(q4, k_ref[0, 0:ctx_len, :]) * SCALE
        mc = jnp.maximum(jnp.max(sc, axis=1, keepdims=True), sink_col)

        def finish(m, l, acc):
            _unstack_heads(o_ref, acc / l, bq)
            lse_ref[0] = jnp.broadcast_to(m + jnp.log(l), (rows, HEAD_DIM)).reshape(GROUP, bq, HEAD_DIM)

        @pl.when(i < lq)
        def _():
            pc = jnp.exp(sc - mc)
            l = jnp.sum(pc, axis=1, keepdims=True) + jnp.exp(sink_col - mc)
            finish(mc, l, _dot(pc.astype(BF16), v_ref[0, 0:ctx_len, :]))

        @pl.when(i >= lq)
        def _():
            start, ok = _band(i, lq, ctx_len, t, bq)
            sb = jnp.where(ok, _dot_nt(q4, k_ref[0, pl.ds(start, WIN_SPAN), :]) * SCALE, NEG_INF)
            m = jnp.maximum(mc, jnp.max(sb, axis=1, keepdims=True))
            pc, pb = jnp.exp(sc - m), jnp.exp(sb - m)
            l = jnp.sum(pc, axis=1, keepdims=True) + jnp.sum(pb, axis=1, keepdims=True) + jnp.exp(sink_col - m)
            acc = _dot(pc.astype(BF16), v_ref[0, 0:ctx_len, :]) + _dot(pb.astype(BF16), v_ref[0, pl.ds(start, WIN_SPAN), :])
            finish(m, l, acc)

    return pl.pallas_call(
        body, name="attn_win_fwd", grid=(b, kvh, t // bq),
        in_specs=[pl.BlockSpec(memory_space=pltpu.SMEM), q_spec, k_spec, v_spec], out_specs=(q_spec, lse_spec),
        out_shape=(jax.ShapeDtypeStruct((b, t, d), F32), jax.ShapeDtypeStruct((b, kvh * GROUP, t, HEAD_DIM), F32)),
        compiler_params=_cparams(("parallel", "parallel", "arbitrary"), VMEM_LIMIT),
    )(sink, qkv, qkv, qkv)


def _attn_win_bwd(qkv, sink, o3, do3, lse, d, ctx_len):
    b, t, _ = qkv.shape
    bq = WIN_BQ
    rows = GROUP * bq
    lq = ctx_len // bq
    kvh, q_spec, k_spec, v_spec, lse_spec, kt_spec = _attn_specs(t, d, bq)

    def body(sink_ref, q_ref, k_ref, v_ref, o_ref, do_ref, lse_ref, dq_ref, dkt_ref, dvt_ref, dsk_ref):
        h, i = pl.program_id(1), pl.program_id(2)

        @pl.when(i == 0)
        def _():
            dkt_ref[...] = jnp.zeros_like(dkt_ref)
            dvt_ref[...] = jnp.zeros_like(dvt_ref)
            dsk_ref[...] = jnp.zeros_like(dsk_ref)

        q4 = _stack_heads(q_ref)
        do4 = _stack_heads(do_ref)
        dd = jnp.sum(do4 * _stack_heads(o_ref), axis=1, keepdims=True)
        lse_col = lse_ref[0].reshape(rows, HEAD_DIM)[:, 0:1]
        do4b = do4.astype(BF16)
        qt = q4.astype(F32).T.astype(BF16)
        dot = do4.T.astype(BF16)

        def part(k, v):
            return _dot_nt(q4, k) * SCALE, _dot_nt(do4b, v)

        def grads(p, dp, k):
            ds = (p * (dp - dd) * SCALE).astype(BF16)
            return _dot(ds, k), _dot(qt, ds), _dot(dot, p.astype(BF16))

        kc = k_ref[0, 0:ctx_len, :]
        sc, dpc = part(kc, v_ref[0, 0:ctx_len, :])
        dq_c, dk_c, dv_c = grads(jnp.exp(sc - lse_col), dpc, kc)
        dkt_ref[0, 0, :, 0:ctx_len] += dk_c
        dvt_ref[0, 0, :, 0:ctx_len] += dv_c
        _unstack_heads(dq_ref, dq_c, bq)

        @pl.when(i >= lq)
        def _():
            start, ok = _band(i, lq, ctx_len, t, bq)
            kb = k_ref[0, pl.ds(start, WIN_SPAN), :]
            sb, dpb = part(kb, v_ref[0, pl.ds(start, WIN_SPAN), :])
            pb = jnp.where(ok, jnp.exp(sb - lse_col), 0.0)
            dq_b, dk_b, dv_b = grads(pb, dpb, kb)
            dkt_ref[0, 0, :, pl.ds(start, WIN_SPAN)] += dk_b
            dvt_ref[0, 0, :, pl.ds(start, WIN_SPAN)] += dv_b
            for g in range(GROUP):
                dq_ref[0, :, g * HEAD_DIM:(g + 1) * HEAD_DIM] += dq_b[g * bq:(g + 1) * bq, :]

        ps = jnp.exp(_sink_column(sink_ref, h, bq) - lse_col) * dd
        for g in range(GROUP):
            val = jnp.sum(ps[g * bq:(g + 1) * bq, :], axis=0, keepdims=True)
            dsk_ref[0, 0, g:g + 1, :] -= jnp.broadcast_to(val, (1, HEAD_DIM))

    return pl.pallas_call(
        body, name="attn_win_bwd", grid=(b, kvh, t // bq),
        in_specs=[pl.BlockSpec(memory_space=pltpu.SMEM), q_spec, k_spec, v_spec, q_spec, q_spec, lse_spec],
        out_specs=(q_spec, kt_spec, kt_spec, pl.BlockSpec((1, 1, 8, HEAD_DIM), lambda i, h, j: (i, h, 0, 0))),
        out_shape=(jax.ShapeDtypeStruct((b, t, d), F32), jax.ShapeDtypeStruct((b, kvh, HEAD_DIM, t), F32),
                   jax.ShapeDtypeStruct((b, kvh, HEAD_DIM, t), F32), jax.ShapeDtypeStruct((b, kvh, 8, HEAD_DIM), F32)),
        compiler_params=_cparams(("parallel", "parallel", "arbitrary"), VMEM_LIMIT),
    )(sink, qkv, qkv, qkv, o3, do3, lse)


MERGE_BWD_ROWS = 128


def _resident(shape):
    return pl.BlockSpec(shape, lambda *_: (0,) * len(shape), pipeline_mode=pl.Buffered(1))


def _merge_fwd(x3, ya, yb, yc, proj3, w_br, w_out, modsel, ctx_len):
    b, t, d = x3.shape
    bt = ROW_BLOCK
    lb = ctx_len // bt

    def body(x_ref, ya_ref, yb_ref, yc_ref, gm_ref, wbr_ref, wo_ref, m_ref, xn_ref, out_ref):
        mix = jnp.zeros((bt, d), F32)
        for n, y_ref in enumerate((ya_ref, yb_ref, yc_ref)):
            z = (y_ref[0] * _silu(gm_ref[0, :, n * d:(n + 1) * d])).astype(BF16)
            mix = mix + _sigmoid(gm_ref[0, :, (3 + n) * d:(4 + n) * d]) * _dot(z, wbr_ref[n])
        o = _dot(mix.astype(BF16), wo_ref[...])
        out_ref[0] = o
        xn_ref[0] = x_ref[0] + m_ref[0, 0, 2:3, :] * o

    blk = pl.BlockSpec((1, bt, d), lambda i, j: (i, j, 0))
    return pl.pallas_call(
        body, name="merge_fwd", grid=(b, t // bt),
        in_specs=[blk, blk, blk, blk, pl.BlockSpec((1, bt, 6 * d), lambda i, j: (i, j, 0)),
                  _resident((3, d, d)), _resident((d, d)),
                  pl.BlockSpec((1, 1, 8, d), lambda i, j: (i, _row_kind(j, lb), 0, 0))],
        out_specs=(blk, blk),
        out_shape=(jax.ShapeDtypeStruct((b, t, d), F32), jax.ShapeDtypeStruct((b, t, d), F32)),
        compiler_params=_cparams(("parallel", "arbitrary"), VMEM_LIMIT),
    )(x3, ya, yb, yc, proj3, w_br, w_out, modsel)


def _merge_bwd(dxn3, out3, ya, yb, yc, proj3, w_br, w_out, modsel, ctx_len):
    b, t, d = dxn3.shape
    n_cols = proj3.shape[2]
    bt = MERGE_BWD_ROWS
    lb = ctx_len // bt

    def body(dxn_ref, out_ref, ya_ref, yb_ref, yc_ref, gm_ref, wbr_ref, wo_ref, m_ref,
             dgm_ref, dya_ref, dyb_ref, dyc_ref, z_ref, dt_ref, mix_ref, dout_ref, gacc_ref):
        j = pl.program_id(1)
        dxn = dxn_ref[0]
        doutb = (m_ref[0, 0, 2:3, :] * dxn).astype(BF16)
        dout_ref[0] = doutb

        @pl.when((j == 0) | (j == lb))
        def _():
            gacc_ref[...] = jnp.zeros_like(gacc_ref)

        gacc_ref[0, 0, 0:1, :] += _colsum(dxn * out_ref[0])
        dmix = _dot_nt(doutb, wo_ref[...])
        mix = jnp.zeros((bt, d), F32)
        for n, (y_ref, dy_ref) in enumerate(((ya_ref, dya_ref), (yb_ref, dyb_ref), (yc_ref, dyc_ref))):
            g = gm_ref[0, :, n * d:(n + 1) * d]
            y = y_ref[0]
            sig_g = _sigmoid(g)
            silu_g = g * sig_g
            z = (y * silu_g).astype(BF16)
            z_ref[n, 0] = z
            tn = _dot(z, wbr_ref[n])
            s = _sigmoid(gm_ref[0, :, (3 + n) * d:(4 + n) * d])
            mix = mix + s * tn
            dgm_ref[0, :, (3 + n) * d:(4 + n) * d] = (dmix * tn * (s * (1.0 - s))).astype(BF16)
            dtb = (dmix * s).astype(BF16)
            dt_ref[n, 0] = dtb
            dz = _dot_nt(dtb, wbr_ref[n])
            dy_ref[0] = dz * silu_g
            dgm_ref[0, :, n * d:(n + 1) * d] = (dz * y * (sig_g * (1.0 + g * (1.0 - sig_g)))).astype(BF16)
        mix_ref[0] = mix.astype(BF16)

    blk = pl.BlockSpec((1, bt, d), lambda i, j: (i, j, 0))
    blk4 = pl.BlockSpec((3, 1, bt, d), lambda i, j: (0, i, j, 0))
    wide = pl.BlockSpec((1, bt, 6 * d), lambda i, j: (i, j, 0))
    return pl.pallas_call(
        body, name="merge_bwd", grid=(b, t // bt),
        in_specs=[blk, blk, blk, blk, blk, wide, _resident((3, d, d)), _resident((d, d)),
                  pl.BlockSpec((1, 1, 8, d), lambda i, j: (i, _row_kind(j, lb), 0, 0))],
        out_specs=(wide, blk, blk, blk, blk4, blk4, blk, blk,
                   pl.BlockSpec((1, 1, 8, d), lambda i, j: (i, _row_kind(j, lb), 0, 0))),
        out_shape=(jax.ShapeDtypeStruct((b, t, n_cols), BF16),
                   jax.ShapeDtypeStruct((b, t, d), F32), jax.ShapeDtypeStruct((b, t, d), F32),
                   jax.ShapeDtypeStruct((b, t, d), F32),
                   jax.ShapeDtypeStruct((3, b, t, d), BF16), jax.ShapeDtypeStruct((3, b, t, d), BF16),
                   jax.ShapeDtypeStruct((b, t, d), BF16), jax.ShapeDtypeStruct((b, t, d), BF16),
                   jax.ShapeDtypeStruct((b, 2, 8, d), F32)),
        compiler_params=_cparams(("parallel", "arbitrary"), VMEM_LIMIT),
    )(dxn3, out3, ya, yb, yc, proj3, w_br, w_out, modsel)


def _final(x3, g, target, ctx_len):
    b, t, d = x3.shape
    bt = ROW_BLOCK
    lb = ctx_len // bt

    def body(x_ref, g_ref, t_ref, dx_ref, loss_ref, dg_ref):
        j = pl.program_id(1)

        @pl.when(j == 0)
        def _():
            loss_ref[...] = jnp.zeros_like(loss_ref)
            dg_ref[...] = jnp.zeros_like(dg_ref)

        @pl.when(j < lb)
        def _():
            dx_ref[...] = jnp.zeros_like(dx_ref)

        @pl.when(j >= lb)
        def _():
            x = x_ref[0]
            g_row = g_ref[...]
            rstd = lax.rsqrt(jnp.mean(x * x, axis=-1, keepdims=True) + EPS)
            xhat = x * rstd
            err = xhat * g_row - t_ref[0]
            loss_ref[...] += (0.5 / d) * jnp.sum(err * err)
            dy = err * (1.0 / d)
            dg_ref[0, 0:1, :] += _colsum(dy * xhat)
            dxhat = dy * g_row
            dx_ref[0] = rstd * (dxhat - xhat * jnp.mean(dxhat * xhat, axis=-1, keepdims=True))

    blk = pl.BlockSpec((1, bt, d), lambda i, j: (i, j, 0))
    return pl.pallas_call(
        body, name="final_loss", grid=(b, t // bt),
        in_specs=[blk, pl.BlockSpec((1, d), lambda i, j: (0, 0)),
                  pl.BlockSpec((1, bt, d), lambda i, j: (i, jnp.maximum(j - lb, 0), 0))],
        out_specs=(blk, pl.BlockSpec((1, 8, HEAD_DIM), lambda i, j: (i, 0, 0)), pl.BlockSpec((1, 8, d), lambda i, j: (i, 0, 0))),
        out_shape=(jax.ShapeDtypeStruct((b, t, d), F32), jax.ShapeDtypeStruct((b, 8, HEAD_DIM), F32),
                   jax.ShapeDtypeStruct((b, 8, d), F32)),
        compiler_params=_cparams(("parallel", "arbitrary")),
    )(x3, g, target)


def _pick(n, options):
    for o in options:
        if n % o == 0:
            return o
    raise ValueError((n, options))


def _block_diag(w):
    per = LRU_LANES // LRU_BLOCK_W
    nd, nb, bw, _ = w.shape
    wr = w.reshape(nd, nb // per, per, bw, bw)
    eye = jnp.eye(per, dtype=w.dtype)
    bd = wr[:, :, :, :, None, :] * eye[None, None, :, None, :, None]
    return bd.reshape(nd, nb // per, per * bw, per * bw).astype(BF16)


def _block_diag_grad(g):
    per = LRU_LANES // LRU_BLOCK_W
    nd, ng, _, _ = g.shape
    gr = g.reshape(nd, ng, per, LRU_BLOCK_W, per, LRU_BLOCK_W)
    diag = jnp.stack([gr[:, :, k, :, k, :] for k in range(per)], axis=2)
    return diag.reshape(nd, ng * per, LRU_BLOCK_W, LRU_BLOCK_W)


def _mod_select(mod16, b, d):
    m3 = mod16.reshape(MOD_ROWS, 3, d)
    lat = m3[:b]
    ctx = jnp.broadcast_to(m3[b][None], (b, 3, d))
    sel = jnp.stack([ctx, lat], axis=1)
    return jnp.pad(sel, ((0, 0), (0, 0), (0, 5), (0, 0)))


def _layer_fwd(x3, c16, p, cos, sin, ctx_len):
    b, t, d = x3.shape
    off, n_cols = _layout(d)
    mod16 = _mod_fwd(c16, p["w_mod"], p["b_mod"])
    modsel = _mod_select(mod16, b, d)
    h = _norm_mod_fwd(x3, p["norm_g"], modsel, ctx_len)
    proj = _matmul(h.reshape(b * t, d), p["w_in"], bm=_pick(b * t, (512, 256, 128)), bn=1024, bk=d, name="proj_fwd")
    proj3 = proj.reshape(b, t, n_cols)
    ya = _lru_fwd(proj3, off["uA"], p["conv_w"], p["conv_b"], p["wa_bd"], p["ba"], p["wx_bd"], p["bx"], p["lam"], ctx_len)
    qkv_b = _prep_fwd(proj3, off["qB"], off["kB"], d, cos, sin, p["gq"], p["gk"], use_norm=False)
    yb, lse_b = _attn_win_fwd(qkv_b, p["sink"], d, ctx_len)
    qkv_c = _prep_fwd(proj3, off["qC"], off["kC"], d, cos, sin, p["gq"], p["gk"], use_norm=True)
    yc, lse_c = _attn_dense_fwd(qkv_c, d, ctx_len)
    x_new, out3 = _merge_fwd(x3, ya, yb, yc, proj3, p["w_br"], p["w_out"], modsel, ctx_len)
    return x_new, (x3, modsel, h, proj3, ya, yb, yc, qkv_b, lse_b, qkv_c, lse_c, out3)


def _layer_bwd(dxn3, saved, c16, p, cos, sin, ctx_len):
    x3, modsel, h, proj3, ya, yb, yc, qkv_b, lse_b, qkv_c, lse_c, out3 = saved
    b, t, d = x3.shape
    off, n_cols = _layout(d)
    rows = b * t
    bk = _pick(rows, (512, 256, 128))
    dproj3, dya, dyb, dyc, z4, dt4, mixb, doutb, gacc = _merge_bwd(dxn3, out3, ya, yb, yc, proj3, p["w_br"], p["w_out"],
                                                                   modsel, ctx_len)
    dw_br = jnp.stack([_matmul(z4[n].reshape(rows, d), dt4[n].reshape(rows, d), ta=True, bm=d, bn=d, bk=bk,
                               name="dw_branch") for n in range(3)])
    dw_out = _matmul(mixb.reshape(rows, d), doutb.reshape(rows, d), ta=True, bm=d, bn=d, bk=bk, name="dw_out")
    dproj3, vec, dwa, dwx = _lru_bwd(dproj3, proj3, off["uA"], dya, p["conv_w"], p["conv_b"], p["wa_bd"], p["ba"],
                                     p["wx_bd"], p["bx"], p["lam"], ctx_len)
    dq_b, dkt_b, dvt_b, dsk = _attn_win_bwd(qkv_b, p["sink"], yb, dyb, lse_b, d, ctx_len)
    dproj3, _ = _prep_bwd(dproj3, dq_b, dkt_b, dvt_b, proj3, off["qB"], off["kB"], d, cos, sin, p["gq"], p["gk"], False)
    dq_c, dkt_c, dvt_c = _attn_dense_bwd(qkv_c, yc, dyc, lse_c, d, ctx_len)
    dproj3, gqk = _prep_bwd(dproj3, dq_c, dkt_c, dvt_c, proj3, off["qC"], off["kC"], d, cos, sin, p["gq"], p["gk"], True)
    dproj2 = dproj3.reshape(rows, n_cols)
    dw_in = _matmul(h.reshape(rows, d), dproj2, ta=True, bm=d, bn=1024, bk=bk, name="dw_in")
    dh = _matmul(dproj2, p["w_in"], tb=True, bm=_pick(rows, (512, 256, 128)), bn=d, bk=1024, name="dh")
    dx3, nacc = _norm_mod_bwd(dh.reshape(b, t, d), x3, p["norm_g"], modsel, dxn3, ctx_len)
    per = jnp.stack([nacc[:, :, 0], nacc[:, :, 1], gacc[:, :, 0]], axis=2)
    dmod = jnp.concatenate([per[:, 1].reshape(b, 3 * d), jnp.sum(per[:, 0], axis=0).reshape(1, 3 * d)], axis=0)
    dmod16 = jnp.pad(dmod, ((0, MOD_ROWS - b - 1), (0, 0)))
    dw_mod, db_mod, dc16 = _mod_bwd(c16, dmod16, p["w_mod"])
    vsum = jnp.sum(vec, axis=0)
    grads = {
        "norm_g": jnp.sum(nacc[:, :, 2], axis=(0, 1)),
        "w_mod": dw_mod, "b_mod": db_mod[0], "w_in": dw_in,
        "conv_w": vsum[0:4], "conv_b": vsum[4],
        "lru_wa": _block_diag_grad(jnp.sum(dwa, axis=0)), "lru_ba": vsum[5:7],
        "lru_wx": _block_diag_grad(jnp.sum(dwx, axis=0)), "lru_bx": vsum[7:9],
        "lru_lambda": vsum[9:11] * (-jax.nn.sigmoid(-p["lam"])),
        "attn_sink": jnp.sum(dsk[:, :, 0:GROUP, 0], axis=0).reshape(-1),
        "q_norm_g": jnp.sum(gqk[:, 0], axis=0), "k_norm_g": jnp.sum(gqk[:, 1], axis=0),
        "w_branch": dw_br, "w_out": dw_out,
    }
    return dx3, dc16, grads


def _reorder_in_cols(w, d, inverse=False):
    off_new, _ = _layout(d)
    segs = _orig_segments(d)
    if inverse:
        return jnp.concatenate([w[..., off_new[n]:off_new[n] + wd] for n, _, wd in segs], axis=-1)
    by_name = {n: (o, wd) for n, o, wd in segs}
    order = sorted(off_new, key=off_new.get)
    return jnp.concatenate([w[..., by_name[n][0]:by_name[n][0] + by_name[n][1]] for n in order], axis=-1)


def _layer_params(li, w):
    d = w["norm_g"].shape[1]
    return {
        "norm_g": w["norm_g"][li][None], "w_mod": w["w_mod"][li], "b_mod": w["b_mod"][li][None],
        "w_in": _reorder_in_cols(w["w_in"][li], d),
        "conv_w": w["conv_w"][li], "conv_b": w["conv_b"][li][None],
        "wa_bd": _block_diag(w["lru_wa"][li]), "ba": w["lru_ba"][li],
        "wx_bd": _block_diag(w["lru_wx"][li]), "bx": w["lru_bx"][li], "lam": w["lru_lambda"][li],
        "sink": w["attn_sink"][li], "gq": w["q_norm_g"][li][None], "gk": w["k_norm_g"][li][None],
        "w_br": w["w_branch"][li], "w_out": w["w_out"][li],
    }


def _local_step(x, c, ctx, target, c_ctx, final_g, layers):
    b, s, d = x.shape
    ctx_len = ctx.shape[1]
    cos, sin = _rope_tables(ctx_len, s)
    x3 = jnp.concatenate([ctx, x], axis=1)
    c16 = jnp.concatenate([c, c_ctx[None], jnp.zeros((MOD_ROWS - b - 1, d), F32)], axis=0)
    saved = []
    for p in layers:
        x3, sv = _layer_fwd(x3, c16, p, cos, sin, ctx_len)
        saved.append(sv)
    dx3, loss_acc, dgf = _final(x3, final_g[None], target, ctx_len)
    grads = [None] * len(layers)
    dc_ctx = jnp.zeros((d,), F32)
    for li in reversed(range(len(layers))):
        dx3, dc16, grads[li] = _layer_bwd(dx3, saved[li], c16, layers[li], cos, sin, ctx_len)
        dc_ctx = dc_ctx + dc16[b]
    return jnp.sum(loss_acc[:, 0, 0]), dx3[:, ctx_len:], dc_ctx, jnp.sum(dgf[:, 0], axis=0), grads


N_CHIPS = 4
ANY = pl.BlockSpec(memory_space=pl.ANY)


def _place():
    x, y, c = lax.axis_index("x"), lax.axis_index("y"), lax.axis_index("c")
    return x, y, c, [(1 - x, y), (x, 1 - y), (1 - x, 1 - y)]


def _axis_part(ref, axis, start, size):
    idx = [slice(None)] * len(ref.shape)
    idx[axis] = pl.ds(start, size)
    return ref.at[tuple(idx)]


def _remote(src, dst, send, recv, dev):
    return pltpu.make_async_remote_copy(src_ref=src, dst_ref=dst, send_sem=send, recv_sem=recv, device_id=dev,
                                        device_id_type=MESH)


COPY_PIECES = 8


def _pieces(src, dst):
    shape = src.shape
    for ax in range(len(shape) - 1):
        if shape[ax] % COPY_PIECES == 0 and shape[ax] // COPY_PIECES >= 8:
            sz = shape[ax] // COPY_PIECES
            return [(_axis_part(src, ax, j * sz, sz), _axis_part(dst, ax, j * sz, sz)) for j in range(COPY_PIECES)]
    return [(src, dst)]


def _gather_chips(shards, axes, name):
    n = len(shards)

    def whole(s, ax):
        shape = list(s.shape)
        shape[ax] *= N_CHIPS
        return jax.ShapeDtypeStruct(tuple(shape), s.dtype)

    def body(*refs):
        ins, outs = refs[:n], refs[n:2 * n]
        send, recv, fsend, frecv, loc = refs[2 * n:]
        x, y, c, chips = _place()
        me = 2 * x + y
        sib = (x, y, 1 - c)

        def block(i, chip_index, half):
            sz = shards[i].shape[axes[i]]
            hl = shards[i].shape[0] // 2
            return _axis_part(outs[i], axes[i], chip_index * sz, sz).at[pl.ds(half * hl, hl)]

        def my_half(i):
            hl = shards[i].shape[0] // 2
            return ins[i].at[pl.ds(c * hl, hl)]

        for i in range(n):
            sz = shards[i].shape[axes[i]]
            for s_, d_ in _pieces(ins[i], _axis_part(outs[i], axes[i], me * sz, sz)):
                pltpu.make_async_copy(s_, d_, loc.at[i]).start()
            for k, (px, py) in enumerate(chips):
                _remote(my_half(i), block(i, me, c), send.at[i, k], recv.at[i, k], (px, py, c)).start()
        for i in range(n):
            for k, (px, py) in enumerate(chips):
                landed = block(i, 2 * px + py, c)
                _remote(my_half(i), landed, send.at[i, k], recv.at[i, k], (px, py, c)).wait_recv()
                for s_, d_ in _pieces(landed, landed):
                    _remote(s_, d_, fsend.at[i, k], frecv.at[i, k], sib).start()
        for i in range(n):
            sz = shards[i].shape[axes[i]]
            for k, (px, py) in enumerate(chips):
                passed = _remote(block(i, 2 * px + py, c), block(i, 2 * px + py, 1 - c), fsend.at[i, k], frecv.at[i, k], sib)
                passed.wait_recv()
                passed.wait_send()
                _remote(my_half(i), block(i, me, c), send.at[i, k], recv.at[i, k], (px, py, c)).wait_send()
            pltpu.make_async_copy(ins[i], _axis_part(outs[i], axes[i], me * sz, sz), loc.at[i]).wait()

    sems = pltpu.SemaphoreType.DMA((n, 3))
    return pl.pallas_call(
        body, name=name, in_specs=[ANY] * n, out_specs=tuple([ANY] * n),
        out_shape=tuple(whole(s, ax) for s, ax in zip(shards, axes)),
        scratch_shapes=[sems, sems, sems, sems, pltpu.SemaphoreType.DMA((n,))],
    )(*shards)


def _split_cores(gs, name):
    n = len(gs)

    def body(*refs):
        ins, got = refs[:n], refs[n:2 * n]
        send, recv = refs[2 * n:]
        x, y, c, _ = _place()
        sib = (x, y, 1 - c)

        def theirs(i):
            hl = gs[i].shape[0] // 2
            return ins[i].at[pl.ds((1 - c) * hl, hl)]

        for i in range(n):
            for s_, d_ in _pieces(theirs(i), got[i]):
                _remote(s_, d_, send.at[i], recv.at[i], sib).start()
        for i in range(n):
            _remote(theirs(i), got[i], send.at[i], recv.at[i], sib).wait()

    return pl.pallas_call(
        body, name=name, in_specs=[ANY] * n, out_specs=tuple([ANY] * n),
        out_shape=tuple(jax.ShapeDtypeStruct((g.shape[0] // 2,) + g.shape[1:], g.dtype) for g in gs),
        scratch_shapes=[pltpu.SemaphoreType.DMA((n,)), pltpu.SemaphoreType.DMA((n,))],
    )(*gs)


def _scatter_chips(ps, pbs, axes, name):
    n = len(ps)

    def block(p, ax):
        shape = list(p.shape)
        shape[ax] //= N_CHIPS
        return tuple(shape)

    def body(*refs):
        ins, inb, own, got = refs[:n], refs[n:2 * n], refs[2 * n:3 * n], refs[3 * n:4 * n]
        send, recv, loc = refs[4 * n:]
        x, y, c, chips = _place()
        me = 2 * x + y

        def part(ref, i, chip_index):
            sz = ps[i].shape[axes[i]] // N_CHIPS
            return _axis_part(ref, axes[i], chip_index * sz, sz)

        for i in range(n):
            for s_, d_ in _pieces(part(ins[i], i, me), own[i]):
                pltpu.make_async_copy(s_, d_, loc.at[i]).start()
            for k, (px, py) in enumerate(chips):
                _remote(part(inb[i], i, 2 * px + py), got[i].at[k], send.at[i, k], recv.at[i, k], (px, py, c)).start()
        for i in range(n):
            for k, (px, py) in enumerate(chips):
                _remote(part(inb[i], i, 2 * px + py), got[i].at[k], send.at[i, k], recv.at[i, k], (px, py, c)).wait()
            pltpu.make_async_copy(part(ins[i], i, me), own[i], loc.at[i]).wait()

    outs = pl.pallas_call(
        body, name=name, in_specs=[ANY] * (2 * n), out_specs=tuple([ANY] * (2 * n)),
        out_shape=tuple(jax.ShapeDtypeStruct(block(p, ax), p.dtype) for p, ax in zip(ps, axes))
        + tuple(jax.ShapeDtypeStruct((3,) + block(p, ax), p.dtype) for p, ax in zip(pbs, axes)),
        scratch_shapes=[pltpu.SemaphoreType.DMA((n, 3)), pltpu.SemaphoreType.DMA((n, 3)), pltpu.SemaphoreType.DMA((n,))],
    )(*ps, *pbs)
    return outs[:n], outs[n:]


def _join_cores(hs, name):
    n = len(hs)

    def body(*refs):
        ins, outs = refs[:n], refs[n:2 * n]
        send, recv, loc = refs[2 * n:]
        x, y, c, _ = _place()
        sib = (x, y, 1 - c)

        def half(i, which):
            hl = hs[i].shape[0]
            return outs[i].at[pl.ds(which * hl, hl)]

        for i in range(n):
            for s_, d_ in _pieces(ins[i], half(i, c)):
                pltpu.make_async_copy(s_, d_, loc.at[i]).start()
                _remote(s_, d_, send.at[i], recv.at[i], sib).start()
        for i in range(n):
            cp = _remote(ins[i], half(i, 1 - c), send.at[i], recv.at[i], sib)
            cp.wait_recv()
            cp.wait_send()
            pltpu.make_async_copy(ins[i], half(i, c), loc.at[i]).wait()

    return pl.pallas_call(
        body, name=name, in_specs=[ANY] * n, out_specs=tuple([ANY] * n),
        out_shape=tuple(jax.ShapeDtypeStruct((2 * h.shape[0],) + h.shape[1:], h.dtype) for h in hs),
        scratch_shapes=[pltpu.SemaphoreType.DMA((n,)), pltpu.SemaphoreType.DMA((n,)), pltpu.SemaphoreType.DMA((n,))],
    )(*hs)


def _all_reduce_small(buf):
    r = buf.shape[0]

    def body(in_ref, out_ref, sib_buf, chip_sum, got, send, recv):
        x, y, c, chips = _place()
        cp = _remote(in_ref, sib_buf, send.at[0], recv.at[0], (x, y, 1 - c))
        cp.start()
        cp.wait()
        chip_sum[...] = in_ref[...] + sib_buf[...]
        cps = [_remote(chip_sum, got.at[k], send.at[1 + k], recv.at[1 + k], (px, py, c)) for k, (px, py) in enumerate(chips)]
        for cp in cps:
            cp.start()
        for cp in cps:
            cp.wait()
        out_ref[...] = (chip_sum[...] + got[0]) + (got[1] + got[2])

    return pl.pallas_call(
        body, name="all_reduce_small", out_shape=jax.ShapeDtypeStruct(buf.shape, F32),
        in_specs=[pl.BlockSpec(memory_space=pltpu.VMEM)], out_specs=pl.BlockSpec(memory_space=pltpu.VMEM),
        scratch_shapes=[pltpu.VMEM((r, 128), F32), pltpu.VMEM((r, 128), F32), pltpu.VMEM((3, r, 128), F32),
                        pltpu.SemaphoreType.DMA((4,)), pltpu.SemaphoreType.DMA((4,))],
        compiler_params=_cparams(None, VMEM_LIMIT),
    )(buf)


ELEMENTWISE_BLOCK_BYTES = 1 << 20


def _view2d(a):
    cols = a.shape[-1] if a.ndim > 1 else 128
    return a.reshape(-1, cols)


def _row_block(rows, cols):
    want = max(8, ELEMENTWISE_BLOCK_BYTES // (4 * cols))
    br = rows
    while br > want and br % 2 == 0 and (br // 2) % 16 == 0:
        br //= 2
    return br


def _sum_half(g, got, core, name):
    h = got.shape[0]
    gv = g.reshape(2 * h, -1, g.shape[-1])
    tv = got.reshape(h, -1, g.shape[-1])
    _, rows, cols = tv.shape
    br = _row_block(rows, cols)

    def body(core_ref, g_ref, t_ref, p_ref, pb_ref):
        del core_ref
        p = g_ref[...] + t_ref[...]
        p_ref[...] = p
        pb_ref[...] = p.astype(BF16)

    blk = pl.BlockSpec((1, br, cols), lambda l, i, core_ref: (l, i, 0))
    p, pb = pl.pallas_call(
        body, name=name,
        grid_spec=pltpu.PrefetchScalarGridSpec(
            num_scalar_prefetch=1, grid=(h, rows // br),
            in_specs=[pl.BlockSpec((1, br, cols), lambda l, i, core_ref: (core_ref[0] * h + l, i, 0)), blk],
            out_specs=(blk, blk)),
        out_shape=(jax.ShapeDtypeStruct(tv.shape, F32), jax.ShapeDtypeStruct(tv.shape, BF16)),
        compiler_params=_cparams(("parallel", "parallel")))(core, gv, tv)
    return p.reshape(got.shape), pb.reshape(got.shape)


def _sum_blocks(own, got3, name):
    ov = _view2d(own)
    rows, cols = ov.shape
    tv = got3.reshape(3, rows, cols)
    br = _row_block(rows, cols)

    def body(o_ref, a_ref, b_ref, c_ref, out_ref):
        out_ref[...] = ((o_ref[...] + a_ref[0].astype(F32)) + b_ref[0].astype(F32)) + c_ref[0].astype(F32)

    blk = pl.BlockSpec((br, cols), lambda i: (i, 0))
    out = pl.pallas_call(
        body, name=name, grid=(rows // br,),
        in_specs=[blk] + [pl.BlockSpec((1, br, cols), lambda i, k=k: (k, i, 0)) for k in range(3)], out_specs=blk,
        out_shape=jax.ShapeDtypeStruct((rows, cols), F32), compiler_params=_cparams(("parallel",)))(ov, tv, tv, tv)
    return out.reshape(own.shape)


def _adamw(w, g, m, v, name):
    shape = w.shape
    ops = [_view2d(a) for a in (w, g, m, v)]
    rows, cols = ops[0].shape
    br = _row_block(rows, cols)
    c1 = 1.0 - ADAM_B1 ** ADAM_STEP
    c2 = 1.0 - ADAM_B2 ** ADAM_STEP

    def body(w_ref, g_ref, m_ref, v_ref, d_ref, nm_ref, nv_ref):
        g_ = g_ref[...]
        nm = ADAM_B1 * m_ref[...] + (1.0 - ADAM_B1) * g_
        nv = ADAM_B2 * v_ref[...] + (1.0 - ADAM_B2) * (g_ * g_)
        d_ref[...] = -ADAM_LR * ((nm / c1) / (jnp.sqrt(nv / c2) + ADAM_EPS) + ADAM_WD * w_ref[...])
        nm_ref[...] = nm
        nv_ref[...] = nv

    blk = pl.BlockSpec((br, cols), lambda i: (i, 0))
    outs = pl.pallas_call(body, name=name, grid=(rows // br,), in_specs=[blk] * 4, out_specs=(blk, blk, blk),
                          out_shape=tuple(jax.ShapeDtypeStruct((rows, cols), F32) for _ in range(3)),
                          compiler_params=_cparams(("parallel",)))(*ops)
    return tuple(o.reshape(shape) for o in outs)


def _pack(arrays):
    flat = jnp.concatenate([a.reshape(-1) for a in arrays])
    pad = (-flat.shape[0]) % (8 * 128)
    return jnp.pad(flat, (0, pad)).reshape(-1, 128)


def _unpack(buf, shapes):
    flat = buf.reshape(-1)
    out, o = [], 0
    for s in shapes:
        n = int(np.prod(s))
        out.append(flat[o:o + n].reshape(s))
        o += n
    return out


WEIGHTS = ["c_ctx", "norm_g", "w_mod", "b_mod", "w_in", "conv_w", "conv_b", "lru_wa", "lru_ba", "lru_wx", "lru_bx",
           "lru_lambda", "attn_sink", "q_norm_g", "k_norm_g", "w_branch", "w_out", "final_g"]
BIG = {"w_mod": 2, "w_in": 2, "w_branch": 2, "w_out": 1}
SMALL_SHARDED = ["conv_w", "lru_ba", "lru_bx", "lru_lambda"]
REPLICATED = [n for n in WEIGHTS if n not in BIG and n not in SMALL_SHARDED]


def kernel(x, c, ctx, c_ctx, norm_g, w_mod, b_mod, w_in, conv_w, conv_b, lru_wa, lru_ba, lru_wx, lru_bx, lru_lambda, attn_sink, q_norm_g, k_norm_g, w_branch, w_out, final_g, loss_target, m_c_ctx, m_norm_g, m_w_mod, m_b_mod, m_w_in, m_conv_w, m_conv_b, m_lru_wa, m_lru_ba, m_lru_wx, m_lru_bx, m_lru_lambda, m_attn_sink, m_q_norm_g, m_k_norm_g, m_w_branch, m_w_out, m_final_g, v_c_ctx, v_norm_g, v_w_mod, v_b_mod, v_w_in, v_conv_w, v_conv_b, v_lru_wa, v_lru_ba, v_lru_wx, v_lru_bx, v_lru_lambda, v_attn_sink, v_q_norm_g, v_k_norm_g, v_w_branch, v_w_out, v_final_g):
    args = dict(locals())
    w = {n: args[n] for n in WEIGHTS}
    mom = {n: args["m_" + n] for n in WEIGHTS}
    var = {n: args["v_" + n] for n in WEIGHTS}
    depth, d = norm_g.shape
    chip = 2 * lax.axis_index("x") + lax.axis_index("y")

    big_names = list(BIG)
    small_shard = jnp.concatenate([w[n] for n in SMALL_SHARDED], axis=1)
    gathered = _gather_chips([w[n].astype(BF16) for n in big_names] + [small_shard],
                             [BIG[n] for n in big_names] + [2], "gather_weights")
    whole = dict(w)
    whole.update(dict(zip(big_names, gathered[:-1])))
    o = 0
    for n in SMALL_SHARDED:
        rows = w[n].shape[1]
        whole[n] = gathered[-1][:, o:o + rows]
        o += rows
    layers = [_layer_params(li, whole) for li in range(depth)]

    loss_local, grad_x, g_c_ctx, g_final, lgrads = _local_step(x, c, ctx, loss_target, c_ctx, final_g, layers)
    loss = lax.psum(loss_local, ("x", "y", "c"))
    full = {n: jnp.stack([lg[n] for lg in lgrads]) for n in lgrads[0]}
    full["w_in"] = _reorder_in_cols(full["w_in"], d, inverse=True)
    full["c_ctx"], full["final_g"] = g_c_ctx, g_final

    core = jnp.reshape(lax.axis_index("c"), (1,)).astype(jnp.int32)
    bigs = [full[n] for n in big_names]
    got = _split_cores(bigs, "grad_split_cores")
    parts = [_sum_half(g, t_, core, "grad_chip_sum") for g, t_ in zip(bigs, got)]
    own, recv = _scatter_chips([p_ for p_, _ in parts], [pb for _, pb in parts], [BIG[n] for n in big_names],
                               "grad_scatter_chips")
    half = [_sum_blocks(a, r, "grad_total") for a, r in zip(own, recv)]
    grad = dict(zip(big_names, _join_cores(half, "grad_join_cores")))

    small_names = REPLICATED + SMALL_SHARDED
    reduced = _unpack(_all_reduce_small(_pack([full[n] for n in small_names])), [full[n].shape for n in small_names])
    for n, g in zip(small_names, reduced):
        if n in SMALL_SHARDED:
            sz = w[n].shape[-1]
            g = lax.dynamic_slice_in_dim(g, chip * sz, sz, axis=g.ndim - 1)
        grad[n] = g

    delta, new_m, new_v = {}, {}, {}
    for n in big_names:
        delta[n], new_m[n], new_v[n] = _adamw(w[n], grad[n], mom[n], var[n], "adamw_" + n)
    shapes = [w[n].shape for n in small_names]
    packed = _adamw(_pack([w[n] for n in small_names]), _pack([grad[n] for n in small_names]),
                    _pack([mom[n] for n in small_names]), _pack([var[n] for n in small_names]), "adamw_small")
    for res, p in zip((delta, new_m, new_v), packed):
        res.update(dict(zip(small_names, _unpack(p, shapes))))

    return (loss, grad_x, *[grad[n] for n in WEIGHTS], *[delta[n] for n in WEIGHTS],
            *[new_m[n] for n in WEIGHTS], *[new_v[n] for n in WEIGHTS])
```

```python
import functools

import jax
import jax.numpy as jnp
import numpy as np
from jax import lax
from jax.experimental import pallas as pl
from jax.experimental.pallas import tpu as pltpu

F32 = jnp.float32
BF16 = jnp.bfloat16

HEAD_DIM = 128
GROUP = 4
LRU_BLOCK_W = 64
LRU_C = 8.0
WINDOW = 128
GRID_W = 64
ROPE_THETA = 10000.0
EPS = 1e-6
NEG_INF = -1e30
ADAM_LR, ADAM_B1, ADAM_B2, ADAM_EPS, ADAM_WD, ADAM_STEP = 0.001, 0.9, 0.999, 1e-08, 0.01, 10

ROW_BLOCK = 256
LRU_LANES = 128
LRU_CHUNK = 128
DENSE_FWD_BQ = 256
LOG2E = 1.4426950408889634
ATT_BQ = 256
WIN_BQ = 128
WIN_SPAN = WIN_BQ + 2 * WINDOW
MOD_ROWS = 16
VMEM_LIMIT = 56 * 1024 * 1024

MESH = pl.DeviceIdType.MESH


def _cparams(sem=None, vmem=None):
    kw = {}
    if sem is not None:
        kw["dimension_semantics"] = sem
    if vmem is not None:
        kw["vmem_limit_bytes"] = vmem
    return pltpu.CompilerParams(**kw)


def _sigmoid(v):
    return 1.0 / (1.0 + jnp.exp(-v))


def _silu(v):
    return v * _sigmoid(v)


def _dsilu(v):
    s = _sigmoid(v)
    return s * (1.0 + v * (1.0 - s))


def _expm1(x):
    poly = x * (1.0 + x * (0.5 + x * (1.0 / 6 + x * (1.0 / 24 + x * (1.0 / 120 + x * (1.0 / 720 + x * (1.0 / 5040)))))))
    return jnp.where(jnp.abs(x) < 0.25, poly, jnp.exp(x) - 1.0)


def _log1p(y):
    u = 1.0 + y
    d = u - 1.0
    return jnp.where(d == 0.0, y, jnp.log(u) * (y / jnp.where(d == 0.0, 1.0, d)))


def _softplus(x):
    return jnp.maximum(x, 0.0) + _log1p(jnp.exp(-jnp.abs(x)))


def _dot(a, b):
    return jnp.dot(a, b, preferred_element_type=F32)


def _dot_nt(a, b):
    return lax.dot_general(a, b, (((1,), (1,)), ((), ())), preferred_element_type=F32)


def _dot_tn(a, b):
    return lax.dot_general(a, b, (((0,), (0,)), ((), ())), preferred_element_type=F32)


def _colsum(v):
    return jnp.sum(v, axis=0, keepdims=True)


def _layout(d_model):
    kvw = (d_model // HEAD_DIM // GROUP) * HEAD_DIM
    names = ["gA", "gB", "gC", "mA", "mB", "mC", "uA", "qB", "qC", "kB", "vB", "kC", "vC"]
    widths = [d_model] * 9 + [kvw] * 4
    off, o = {}, 0
    for n, w in zip(names, widths):
        off[n] = o
        o += w
    return off, o


def _orig_segments(d_model):
    kvw = (d_model // HEAD_DIM // GROUP) * HEAD_DIM
    names = ["uA", "gA", "qB", "kB", "vB", "gB", "qC", "kC", "vC", "gC", "mA", "mB", "mC"]
    widths = [d_model, d_model, d_model, kvw, kvw, d_model, d_model, kvw, kvw, d_model, d_model, d_model, d_model]
    out, o = [], 0
    for n, w in zip(names, widths):
        out.append((n, o, w))
        o += w
    return out


def _matmul(a, b, *, ta=False, tb=False, out_dtype=F32, bm, bn, bk, name, n_outer=False):
    (kdim, m) = a.shape if ta else a.shape[::-1]
    (n, kdim2) = b.shape if tb else b.shape[::-1]
    assert kdim == kdim2 and m % bm == 0 and n % bn == 0 and kdim % bk == 0, (a.shape, b.shape, bm, bn, bk)
    nk = kdim // bk
    dims = (((0 if ta else 1,), (1 if tb else 0,)), ((), ()))

    def ij(f):
        return (lambda j, i, k: f(i, j, k)) if n_outer else f

    def body(a_ref, b_ref, o_ref, *scratch):
        r = lax.dot_general(a_ref[...].astype(BF16), b_ref[...].astype(BF16), dims, preferred_element_type=F32)
        if nk == 1:
            o_ref[...] = r.astype(out_dtype)
        else:
            acc = scratch[0]
            k = pl.program_id(2)

            @pl.when(k == 0)
            def _():
                acc[...] = r

            @pl.when(k > 0)
            def _():
                acc[...] += r

            @pl.when(k == nk - 1)
            def _():
                o_ref[...] = acc[...].astype(out_dtype)

    a_spec = pl.BlockSpec((bk, bm), ij(lambda i, j, k: (k, i))) if ta else pl.BlockSpec((bm, bk), ij(lambda i, j, k: (i, k)))
    b_spec = pl.BlockSpec((bn, bk), ij(lambda i, j, k: (j, k))) if tb else pl.BlockSpec((bk, bn), ij(lambda i, j, k: (k, j)))
    return pl.pallas_call(
        body, name=name, grid=(n // bn, m // bm, nk) if n_outer else (m // bm, n // bn, nk),
        in_specs=[a_spec, b_spec], out_specs=pl.BlockSpec((bm, bn), ij(lambda i, j, k: (i, j))),
        out_shape=jax.ShapeDtypeStruct((m, n), out_dtype),
        scratch_shapes=[pltpu.VMEM((bm, bn), F32)] if nk > 1 else [],
        compiler_params=_cparams(("parallel", "parallel", "arbitrary"), VMEM_LIMIT),
    )(a, b)


def _mod_fwd(c16, w_mod, b_mod):
    d3 = w_mod.shape[1]

    def body(c_ref, w_ref, b_ref, o_ref):
        o_ref[...] = _dot(_silu(c_ref[...]).astype(BF16), w_ref[...]) + b_ref[...]

    return pl.pallas_call(body, name="mod_fwd", out_shape=jax.ShapeDtypeStruct((MOD_ROWS, d3), F32),
                          compiler_params=_cparams(None, VMEM_LIMIT))(c16, w_mod, b_mod)


def _mod_bwd(c16, dmod16, w_mod):
    d, d3 = w_mod.shape

    def body(c_ref, g_ref, w_ref, dw_ref, db_ref, dc_ref):
        c = c_ref[...]
        g = g_ref[...]
        gb = g.astype(BF16)
        dw_ref[...] = _dot_tn(_silu(c).astype(BF16), gb)
        db_ref[...] = _colsum(g)
        dc_ref[...] = _dot_nt(gb, w_ref[...]) * _dsilu(c)

    return pl.pallas_call(
        body, name="mod_bwd",
        out_shape=(jax.ShapeDtypeStruct((d, d3), F32), jax.ShapeDtypeStruct((1, d3), F32),
                   jax.ShapeDtypeStruct((MOD_ROWS, d), F32)),
        compiler_params=_cparams(None, VMEM_LIMIT))(c16, dmod16, w_mod)


def _row_kind(t, lb):
    return jnp.where(t >= lb, 1, 0)


def _norm_mod_fwd(x3, g, modsel, ctx_len):
    b, t, d = x3.shape
    bt = ROW_BLOCK
    lb = ctx_len // bt

    def body(x_ref, g_ref, m_ref, h_ref):
        x = x_ref[0]
        rstd = lax.rsqrt(jnp.mean(x * x, axis=-1, keepdims=True) + EPS)
        y = x * rstd * g_ref[...]
        h_ref[0] = (y * (1.0 + m_ref[0, 0, 1:2, :]) + m_ref[0, 0, 0:1, :]).astype(BF16)

    return pl.pallas_call(
        body, name="norm_mod_fwd", grid=(b, t // bt),
        in_specs=[pl.BlockSpec((1, bt, d), lambda i, j: (i, j, 0)),
                  pl.BlockSpec((1, d), lambda i, j: (0, 0)),
                  pl.BlockSpec((1, 1, 8, d), lambda i, j: (i, _row_kind(j, lb), 0, 0))],
        out_specs=pl.BlockSpec((1, bt, d), lambda i, j: (i, j, 0)),
        out_shape=jax.ShapeDtypeStruct((b, t, d), BF16),
        compiler_params=_cparams(("parallel", "arbitrary")),
    )(x3, g, modsel)


def _norm_mod_bwd(dh3, x3, g, modsel, dres3, ctx_len):
    b, t, d = x3.shape
    bt = ROW_BLOCK
    lb = ctx_len // bt

    def body(dh_ref, x_ref, g_ref, m_ref, dres_ref, dx_ref, acc_ref):
        j = pl.program_id(1)
        x = x_ref[0]
        dh = dh_ref[0]
        g_row = g_ref[...]
        rstd = lax.rsqrt(jnp.mean(x * x, axis=-1, keepdims=True) + EPS)
        xhat = x * rstd
        dhpre = dh * (1.0 + m_ref[0, 0, 1:2, :])
        dxhat = dhpre * g_row
        dx = rstd * (dxhat - xhat * jnp.mean(dxhat * xhat, axis=-1, keepdims=True))
        dx_ref[0] = dx + dres_ref[0]

        @pl.when((j == 0) | (j == lb))
        def _():
            acc_ref[...] = jnp.zeros_like(acc_ref)

        acc_ref[0, 0, 0:1, :] += _colsum(dh)
        acc_ref[0, 0, 1:2, :] += _colsum(dh * (xhat * g_row))
        acc_ref[0, 0, 2:3, :] += _colsum(dhpre * xhat)

    blk = pl.BlockSpec((1, bt, d), lambda i, j: (i, j, 0))
    return pl.pallas_call(
        body, name="norm_mod_bwd", grid=(b, t // bt),
        in_specs=[blk, blk, pl.BlockSpec((1, d), lambda i, j: (0, 0)),
                  pl.BlockSpec((1, 1, 8, d), lambda i, j: (i, _row_kind(j, lb), 0, 0)), blk],
        out_specs=(blk, pl.BlockSpec((1, 1, 8, d), lambda i, j: (i, _row_kind(j, lb), 0, 0))),
        out_shape=(jax.ShapeDtypeStruct((b, t, d), F32), jax.ShapeDtypeStruct((b, 2, 8, d), F32)),
        compiler_params=_cparams(("parallel", "arbitrary")),
    )(dh3, x3, g, modsel, dres3)


def _shifted_rows(ref, c, off, ctx_len, total):
    ct = LRU_CHUNK
    r0 = pl.multiple_of(c * ct, ct)
    x0 = ref[pl.ds(r0, ct), :]
    row = lax.broadcasted_iota(jnp.int32, x0.shape, 0)
    if off < 0:
        k = -off
        has = jnp.logical_and(r0 != 0, r0 != ctx_len)
        p0 = pl.multiple_of(jnp.maximum(r0 - 8, 0), 8)
        edge = jnp.where(has, ref[pl.ds(p0, 8), :], 0.0)
        out = pltpu.roll(x0, k, 0)
        for j in range(k):
            out = jnp.where(row == j, edge[8 - k + j:8 - k + j + 1, :], out)
    else:
        k = off
        has = jnp.logical_and(r0 + ct != ctx_len, r0 + ct != total)
        n0 = pl.multiple_of(jnp.minimum(r0 + ct, total - 8), 8)
        edge = jnp.where(has, ref[pl.ds(n0, 8), :], 0.0)
        out = pltpu.roll(x0, ct - k, 0)
        for j in range(k):
            out = jnp.where(row == ct - k + j, edge[j:j + 1, :], out)
    return out


def _chunk_scan(a, b, reverse):
    n = a.shape[0]
    row = lax.broadcasted_iota(jnp.int32, a.shape, 0)
    s = 1
    while s < n:
        if reverse:
            a_s, b_s, ok = pltpu.roll(a, n - s, 0), pltpu.roll(b, n - s, 0), row < n - s
        else:
            a_s, b_s, ok = pltpu.roll(a, s, 0), pltpu.roll(b, s, 0), row >= s
        b = jnp.where(ok, a * b_s + b, b)
        a = jnp.where(ok, a * a_s, a)
        s *= 2
    return a, b


def _lru_order(d, s, n_ctx, n_all):
    if d == 0:
        return s
    return jnp.where(s < n_ctx, n_ctx - 1 - s, n_all - 1 - (s - n_ctx))


def _lru_gates(u, wa, ba, wx, bx, sp):
    ub = u.astype(BF16)
    r = _sigmoid(_dot(ub, wa) + ba)
    i = _sigmoid(_dot(ub, wx) + bx)
    log_a = (-LRU_C * sp) * r
    a = jnp.exp(log_a)
    sf = jnp.sqrt(-_expm1(2.0 * log_a))
    return ub, r, i, a, sf


def _lru_specs(t, n_lane_blocks_offset):
    ln = LRU_LANES
    return [
        pl.BlockSpec((4, ln), lambda i, j: (0, j)),
        pl.BlockSpec((1, ln), lambda i, j: (0, j)),
        pl.BlockSpec((2, 1, ln, ln), lambda i, j: (0, j, 0, 0)),
        pl.BlockSpec((2, ln), lambda i, j: (0, j)),
        pl.BlockSpec((2, 1, ln, ln), lambda i, j: (0, j, 0, 0)),
        pl.BlockSpec((2, ln), lambda i, j: (0, j)),
        pl.BlockSpec((2, ln), lambda i, j: (0, j)),
    ]


def _lru_conv(ua_ref, cw_ref, cb_ref, u_s, ctx_len, total):
    ct = LRU_CHUNK

    def conv(c, _):
        r0 = pl.multiple_of(c * ct, ct)
        u = (cw_ref[0:1, :] * _shifted_rows(ua_ref, c, -2, ctx_len, total)
             + cw_ref[1:2, :] * _shifted_rows(ua_ref, c, -1, ctx_len, total)
             + cw_ref[2:3, :] * ua_ref[pl.ds(r0, ct), :]
             + cw_ref[3:4, :] * _shifted_rows(ua_ref, c, 1, ctx_len, total) + cb_ref[...])
        u_s[pl.ds(r0, ct), :] = u
        return 0

    lax.fori_loop(0, total // ct, conv, 0)


def _lru_fwd(proj3, col0, conv_w, conv_b, wa_bd, ba, wx_bd, bx, lam, ctx_len):
    b, t, _ = proj3.shape
    d = conv_w.shape[1]
    ln, ct = LRU_LANES, LRU_CHUNK
    n_all, n_ctx = t // ct, ctx_len // ct
    cb0 = col0 // ln

    def body(ua_ref, cw_ref, cb_ref, wa_ref, ba_ref, wx_ref, bx_ref, lam_ref, y_ref, u_s):
        ua = ua_ref.at[0]
        _lru_conv(ua, cw_ref, cb_ref, u_s, ctx_len, t)
        for dr in (0, 1):
            sp = _softplus(-lam_ref[dr:dr + 1, :])
            wa, wx = wa_ref[dr, 0], wx_ref[dr, 0]
            ba_row, bx_row = ba_ref[dr:dr + 1, :], bx_ref[dr:dr + 1, :]

            def step(s, carry, dr=dr, sp=sp, wa=wa, wx=wx, ba_row=ba_row, bx_row=bx_row):
                c = _lru_order(dr, s, n_ctx, n_all)
                r0 = pl.multiple_of(c * ct, ct)
                u = u_s[pl.ds(r0, ct), :]
                _, _, i, a, sf = _lru_gates(u, wa, ba_row, wx, bx_row, sp)
                aa, h0 = _chunk_scan(a, sf * (i * u), reverse=(dr == 1))
                h = h0 + aa * carry
                if dr == 0:
                    y_ref[0, pl.ds(r0, ct), :] = h
                    return h[ct - 1:ct, :]
                y_ref[0, pl.ds(r0, ct), :] += h
                return h[0:1, :]

            lax.fori_loop(0, n_all, step, jnp.zeros((1, ln), F32))

    return pl.pallas_call(
        body, name="lru_fwd", grid=(b, d // ln),
        in_specs=[pl.BlockSpec((1, t, ln), lambda i, j: (i, 0, cb0 + j))] + _lru_specs(t, cb0),
        out_specs=pl.BlockSpec((1, t, ln), lambda i, j: (i, 0, j)),
        out_shape=jax.ShapeDtypeStruct((b, t, d), F32),
        scratch_shapes=[pltpu.VMEM((t, ln), F32)],
        compiler_params=_cparams(("parallel", "parallel"), VMEM_LIMIT),
    )(proj3, conv_w, conv_b, wa_bd, ba, wx_bd, bx, lam)


def _lru_bwd(dproj3, proj3, col0, dy3, conv_w, conv_b, wa_bd, ba, wx_bd, bx, lam, ctx_len):
    b, t, _ = proj3.shape
    d = conv_w.shape[1]
    ln, ct = LRU_LANES, LRU_CHUNK
    n_all, n_ctx = t // ct, ctx_len // ct
    cb0 = col0 // ln

    def body(dproj_hbm, ua_ref, dy_ref, cw_ref, cb_ref, wa_ref, ba_ref, wx_ref, bx_ref, lam_ref,
             dua_ref, vec_ref, dwa_ref, dwx_ref, u_s, h_s, du_s):
        del dproj_hbm
        ua = ua_ref.at[0]
        _lru_conv(ua, cw_ref, cb_ref, u_s, ctx_len, t)
        du_s[...] = jnp.zeros_like(du_s)
        vec_ref[...] = jnp.zeros_like(vec_ref)
        for dr in (0, 1):
            sp = _softplus(-lam_ref[dr:dr + 1, :])
            wa, wx = wa_ref[dr, 0], wx_ref[dr, 0]
            ba_row, bx_row = ba_ref[dr:dr + 1, :], bx_ref[dr:dr + 1, :]

            def fwd(s, carry, dr=dr, sp=sp, wa=wa, wx=wx, ba_row=ba_row, bx_row=bx_row):
                c = _lru_order(dr, s, n_ctx, n_all)
                r0 = pl.multiple_of(c * ct, ct)
                u = u_s[pl.ds(r0, ct), :]
                _, _, i, a, sf = _lru_gates(u, wa, ba_row, wx, bx_row, sp)
                aa, h0 = _chunk_scan(a, sf * (i * u), reverse=(dr == 1))
                h = h0 + aa * carry
                h_s[pl.ds(r0, ct), :] = h
                return h[ct - 1:ct, :] if dr == 0 else h[0:1, :]

            lax.fori_loop(0, n_all, fwd, jnp.zeros((1, ln), F32))

            def bwd(sr, carry, dr=dr, sp=sp, wa=wa, wx=wx, ba_row=ba_row, bx_row=bx_row):
                gc, dwa_acc, dwx_acc, vacc = carry
                c = _lru_order(dr, n_all - 1 - sr, n_ctx, n_all)
                r0 = pl.multiple_of(c * ct, ct)
                u = u_s[pl.ds(r0, ct), :]
                h = h_s[pl.ds(r0, ct), :]
                dy = dy_ref[0, pl.ds(r0, ct), :]
                ub, r, i, a, sf = _lru_gates(u, wa, ba_row, wx, bx_row, sp)
                row = lax.broadcasted_iota(jnp.int32, a.shape, 0)
                if dr == 0:
                    alpha = jnp.where(row == ct - 1, 1.0, pltpu.roll(a, ct - 1, 0))
                    aa, g0 = _chunk_scan(alpha, dy, reverse=True)
                    g = g0 + aa * gc
                    gc_new = a[0:1, :] * g[0:1, :]
                    p0 = pl.multiple_of(jnp.maximum(r0 - 8, 0), 8)
                    edge = jnp.where(r0 != 0, h_s[pl.ds(p0, 8), :], 0.0)[7:8, :]
                    h_prev = jnp.where(row == 0, edge, pltpu.roll(h, 1, 0))
                else:
                    alpha = jnp.where(row == 0, 1.0, pltpu.roll(a, 1, 0))
                    aa, g0 = _chunk_scan(alpha, dy, reverse=False)
                    g = g0 + aa * gc
                    gc_new = a[ct - 1:ct, :] * g[ct - 1:ct, :]
                    r_end = r0 + ct
                    n0 = pl.multiple_of(jnp.where(r_end == t, 0, jnp.minimum(r_end, t - 8)), 8)
                    edge = jnp.where(r_end != ctx_len, h_s[pl.ds(n0, 8), :], 0.0)[0:1, :]
                    h_prev = jnp.where(row == ct - 1, edge, pltpu.roll(h, ct - 1, 0))
                da = g * h_prev
                iu = i * u
                diu = g * sf
                dlog_a = da * a - (g * iu) * (a * a) / sf
                dpre_r = (dlog_a * (-LRU_C * sp)) * (r * (1.0 - r))
                dpre_i = (diu * u) * (i * (1.0 - i))
                dpr_b, dpi_b = dpre_r.astype(BF16), dpre_i.astype(BF16)
                du = diu * i + _dot_nt(dpr_b, wa) + _dot_nt(dpi_b, wx)
                du_s[pl.ds(r0, ct), :] += du
                dwa_acc = dwa_acc + _dot_tn(ub, dpr_b)
                dwx_acc = dwx_acc + _dot_tn(ub, dpi_b)
                vacc = (vacc[0] + _colsum(dpre_r), vacc[1] + _colsum(dpre_i), vacc[2] + _colsum(dlog_a * (-LRU_C * r)))
                return gc_new, dwa_acc, dwx_acc, vacc

            zrow = jnp.zeros((1, ln), F32)
            zmat = jnp.zeros((ln, ln), F32)
            _, dwa_acc, dwx_acc, vacc = lax.fori_loop(0, n_all, bwd, (zrow, zmat, zmat, (zrow, zrow, zrow)))
            dwa_ref[0, dr, 0] = dwa_acc
            dwx_ref[0, dr, 0] = dwx_acc
            vec_ref[0, 5 + dr:6 + dr, :] = vacc[0]
            vec_ref[0, 7 + dr:8 + dr, :] = vacc[1]
            vec_ref[0, 9 + dr:10 + dr, :] = vacc[2]

        def conv_bwd(c, acc):
            r0 = pl.multiple_of(c * ct, ct)
            du = du_s[pl.ds(r0, ct), :]
            dua = (cw_ref[0:1, :] * _shifted_rows(du_s, c, 2, ctx_len, t)
                   + cw_ref[1:2, :] * _shifted_rows(du_s, c, 1, ctx_len, t)
                   + cw_ref[2:3, :] * du
                   + cw_ref[3:4, :] * _shifted_rows(du_s, c, -1, ctx_len, t))
            dua_ref[0, pl.ds(r0, ct), :] = dua.astype(BF16)
            return (acc[0] + _colsum(du * _shifted_rows(ua, c, -2, ctx_len, t)),
                    acc[1] + _colsum(du * _shifted_rows(ua, c, -1, ctx_len, t)),
                    acc[2] + _colsum(du * ua[pl.ds(r0, ct), :]),
                    acc[3] + _colsum(du * _shifted_rows(ua, c, 1, ctx_len, t)),
                    acc[4] + _colsum(du))

        zrow = jnp.zeros((1, ln), F32)
        acc = lax.fori_loop(0, n_all, conv_bwd, (zrow,) * 5)
        for k in range(5):
            vec_ref[0, k:k + 1, :] = acc[k]

    ng = d // ln
    return pl.pallas_call(
        body, name="lru_bwd", grid=(b, ng),
        in_specs=[pl.BlockSpec(memory_space=pl.ANY),
                  pl.BlockSpec((1, t, ln), lambda i, j: (i, 0, cb0 + j)),
                  pl.BlockSpec((1, t, ln), lambda i, j: (i, 0, j))] + _lru_specs(t, cb0),
        out_specs=(pl.BlockSpec((1, t, ln), lambda i, j: (i, 0, cb0 + j)),
                   pl.BlockSpec((1, 16, ln), lambda i, j: (i, 0, j)),
                   pl.BlockSpec((1, 2, 1, ln, ln), lambda i, j: (i, 0, j, 0, 0)),
                   pl.BlockSpec((1, 2, 1, ln, ln), lambda i, j: (i, 0, j, 0, 0))),
        out_shape=(jax.ShapeDtypeStruct(dproj3.shape, dproj3.dtype),
                   jax.ShapeDtypeStruct((b, 16, d), F32),
                   jax.ShapeDtypeStruct((b, 2, ng, ln, ln), F32),
                   jax.ShapeDtypeStruct((b, 2, ng, ln, ln), F32)),
        scratch_shapes=[pltpu.VMEM((t, ln), F32)] * 3,
        input_output_aliases={0: 0},
        compiler_params=_cparams(("parallel", "parallel"), VMEM_LIMIT),
    )(dproj3, proj3, dy3, conv_w, conv_b, wa_bd, ba, wx_bd, bx, lam)


def _rope_tables(ctx_len, seq):
    p = HEAD_DIM // 4
    inv = ROPE_THETA ** (-jnp.arange(p, dtype=F32) / p)
    tok = jnp.arange(seq)
    ang_r = (tok // GRID_W)[:, None] * inv
    ang_c = (tok % GRID_W)[:, None] * inv
    cos = jnp.concatenate([jnp.cos(ang_r)] * 2 + [jnp.cos(ang_c)] * 2, axis=1)
    sin = jnp.concatenate([-jnp.sin(ang_r), jnp.sin(ang_r), -jnp.sin(ang_c), jnp.sin(ang_c)], axis=1)
    cos = jnp.concatenate([jnp.ones((ctx_len, HEAD_DIM), F32), cos], axis=0)
    sin = jnp.concatenate([jnp.zeros((ctx_len, HEAD_DIM), F32), sin], axis=0)
    return cos, sin


def _swap_halves(v):
    lane = lax.broadcasted_iota(jnp.int32, v.shape, 1)
    return jnp.where((lane & 63) < 32, pltpu.roll(v, 96, 1), pltpu.roll(v, 32, 1))


def _head_rstd(v):
    return lax.rsqrt(jnp.mean(v * v, axis=-1, keepdims=True) + EPS)


QKV_BLOCK = GROUP * HEAD_DIM


def _prep_fwd(proj3, qcol, kvcol, d, cos, sin, gq, gk, use_norm):
    b, t, _ = proj3.shape
    bt, wb = ROW_BLOCK, QKV_BLOCK
    nqb = d // wb
    assert qcol % wb == 0 and kvcol % wb == 0 and d // HEAD_DIM // GROUP == 2
    qb0, kvb = qcol // wb, kvcol // wb

    def body(p_ref, cos_ref, sin_ref, gq_ref, gk_ref, o_ref):
        s = pl.program_id(2)
        c, sn = cos_ref[...], sin_ref[...]

        def rope(v):
            return v * c + _swap_halves(v) * sn

        @pl.when(s < nqb)
        def _():
            for hh in range(GROUP):
                v = p_ref[0, :, hh * HEAD_DIM:(hh + 1) * HEAD_DIM]
                if use_norm:
                    v = v * _head_rstd(v) * gq_ref[...]
                o_ref[0, :, hh * HEAD_DIM:(hh + 1) * HEAD_DIM] = rope(v).astype(BF16)

        @pl.when(s == nqb)
        def _():
            for hh in range(2):
                v = p_ref[0, :, hh * HEAD_DIM:(hh + 1) * HEAD_DIM]
                if use_norm:
                    v = v * _head_rstd(v) * gk_ref[...]
                o_ref[0, :, hh * HEAD_DIM:(hh + 1) * HEAD_DIM] = rope(v).astype(BF16)
            o_ref[0, :, 2 * HEAD_DIM:] = p_ref[0, :, 2 * HEAD_DIM:].astype(BF16)

    return pl.pallas_call(
        body, name="prep_fwd_norm" if use_norm else "prep_fwd", grid=(b, t // bt, nqb + 1),
        in_specs=[pl.BlockSpec((1, bt, wb), lambda i, j, s: (i, j, jnp.where(s < nqb, qb0 + s, kvb))),
                  pl.BlockSpec((bt, HEAD_DIM), lambda i, j, s: (j, 0)),
                  pl.BlockSpec((bt, HEAD_DIM), lambda i, j, s: (j, 0)),
                  pl.BlockSpec((1, HEAD_DIM), lambda i, j, s: (0, 0)),
                  pl.BlockSpec((1, HEAD_DIM), lambda i, j, s: (0, 0))],
        out_specs=pl.BlockSpec((1, bt, wb), lambda i, j, s: (i, j, s)),
        out_shape=jax.ShapeDtypeStruct((b, t, d + wb), BF16),
        compiler_params=_cparams(("parallel", "parallel", "arbitrary")),
    )(proj3, cos, sin, gq, gk)


def _prep_bwd(dproj3, dq3, dkt, dvt, proj3, qcol, kvcol, d, cos, sin, gq, gk, use_norm):
    b, t, _ = proj3.shape
    bt, wb = ROW_BLOCK, QKV_BLOCK
    nqb = d // wb
    qb0, kvb = qcol // wb, kvcol // wb
    kvh = dkt.shape[1]

    def body(dproj_hbm, dq_ref, dkt_ref, dvt_ref, p_ref, cos_ref, sin_ref, gq_ref, gk_ref, o_ref, gacc_ref):
        del dproj_hbm
        j, s = pl.program_id(1), pl.program_id(2)
        c, sn = cos_ref[...], sin_ref[...]

        @pl.when((j == 0) & (s == 0))
        def _():
            gacc_ref[...] = jnp.zeros_like(gacc_ref)

        def unrope(dv):
            return dv * c + _swap_halves(dv * sn)

        def head_bwd(dyv, xv, g_ref, acc_row):
            dyv = unrope(dyv)
            if not use_norm:
                return dyv
            rstd = _head_rstd(xv)
            xhat = xv * rstd
            gacc_ref[0, acc_row:acc_row + 1, :] += _colsum(dyv * xhat)
            dxhat = dyv * g_ref[...]
            return rstd * (dxhat - xhat * jnp.mean(dxhat * xhat, axis=-1, keepdims=True))

        @pl.when(s < nqb)
        def _():
            for hh in range(GROUP):
                sl = slice(hh * HEAD_DIM, (hh + 1) * HEAD_DIM)
                o_ref[0, :, sl] = head_bwd(dq_ref[0, :, sl], p_ref[0, :, sl], gq_ref, 0).astype(BF16)

        @pl.when(s == nqb)
        def _():
            for hh in range(kvh):
                sl = slice(hh * HEAD_DIM, (hh + 1) * HEAD_DIM)
                o_ref[0, :, sl] = head_bwd(dkt_ref[0, hh].T, p_ref[0, :, sl], gk_ref, 1).astype(BF16)
                sv = slice((kvh + hh) * HEAD_DIM, (kvh + hh + 1) * HEAD_DIM)
                o_ref[0, :, sv] = dvt_ref[0, hh].T.astype(BF16)

    col = lambda i, j, s: (i, j, jnp.where(s < nqb, qb0 + s, kvb))
    return pl.pallas_call(
        body, name="prep_bwd_norm" if use_norm else "prep_bwd", grid=(b, t // bt, nqb + 1),
        in_specs=[pl.BlockSpec(memory_space=pl.ANY),
                  pl.BlockSpec((1, bt, wb), lambda i, j, s: (i, j, jnp.minimum(s, nqb - 1))),
                  pl.BlockSpec((1, kvh, HEAD_DIM, bt), lambda i, j, s: (i, 0, 0, j)),
                  pl.BlockSpec((1, kvh, HEAD_DIM, bt), lambda i, j, s: (i, 0, 0, j)),
                  pl.BlockSpec((1, bt, wb), col),
                  pl.BlockSpec((bt, HEAD_DIM), lambda i, j, s: (j, 0)),
                  pl.BlockSpec((bt, HEAD_DIM), lambda i, j, s: (j, 0)),
                  pl.BlockSpec((1, HEAD_DIM), lambda i, j, s: (0, 0)),
                  pl.BlockSpec((1, HEAD_DIM), lambda i, j, s: (0, 0))],
        out_specs=(pl.BlockSpec((1, bt, wb), col), pl.BlockSpec((1, 8, HEAD_DIM), lambda i, j, s: (i, 0, 0))),
        out_shape=(jax.ShapeDtypeStruct(dproj3.shape, dproj3.dtype), jax.ShapeDtypeStruct((b, 8, HEAD_DIM), F32)),
        input_output_aliases={0: 0},
        compiler_params=_cparams(("parallel", "arbitrary", "arbitrary")),
    )(dproj3, dq3, dkt, dvt, proj3, cos, sin, gq, gk)


def _stack_heads(ref, dtype=None):
    parts = [ref[0, :, g * HEAD_DIM:(g + 1) * HEAD_DIM] for g in range(GROUP)]
    v = jnp.concatenate(parts, axis=0)
    return v if dtype is None else v.astype(dtype)


def _unstack_heads(ref, v, bq):
    for g in range(GROUP):
        ref[0, :, g * HEAD_DIM:(g + 1) * HEAD_DIM] = v[g * bq:(g + 1) * bq, :]


def _attn_specs(t, d, bq):
    kvh = d // HEAD_DIM // GROUP
    kc0 = d // HEAD_DIM
    q_spec = pl.BlockSpec((1, bq, QKV_BLOCK), lambda i, h, j: (i, j, h))
    k_spec = pl.BlockSpec((1, t, HEAD_DIM), lambda i, h, j: (i, 0, kc0 + h))
    v_spec = pl.BlockSpec((1, t, HEAD_DIM), lambda i, h, j: (i, 0, kc0 + kvh + h))
    lse_spec = pl.BlockSpec((1, GROUP, bq, HEAD_DIM), lambda i, h, j: (i, h, j, 0))
    kt_spec = pl.BlockSpec((1, 1, HEAD_DIM, t), lambda i, h, j: (i, h, 0, 0))
    return kvh, q_spec, k_spec, v_spec, lse_spec, kt_spec


SCALE = HEAD_DIM ** -0.5


def _attn_dense_fwd(qkv, d, ctx_len):
    b, t, _ = qkv.shape
    bq = DENSE_FWD_BQ
    lq = ctx_len // bq
    kvh, q_spec, k_spec, v_spec, lse_spec, _ = _attn_specs(t, d, bq)

    def body(q_ref, k_ref, v_ref, o_ref, lse_ref):
        i = pl.program_id(2)

        def attend(k, v):
            for g in range(GROUP):
                sl = slice(g * HEAD_DIM, (g + 1) * HEAD_DIM)
                s = _dot_nt(q_ref[0, :, sl], k)
                m = jnp.max(s, axis=1, keepdims=True)
                p = jnp.exp2((s - m) * (SCALE * LOG2E))
                l = jnp.sum(p, axis=1, keepdims=True)
                o_ref[0, :, sl] = _dot(p.astype(BF16), v) / l
                lse_ref[0, g] = jnp.broadcast_to(m * SCALE + jnp.log(l), (bq, HEAD_DIM))

        @pl.when(i < lq)
        def _():
            attend(k_ref[0, 0:ctx_len, :], v_ref[0, 0:ctx_len, :])

        @pl.when(i >= lq)
        def _():
            attend(k_ref[0], v_ref[0])

    return pl.pallas_call(
        body, name="attn_dense_fwd", grid=(b, kvh, t // bq),
        in_specs=[q_spec, k_spec, v_spec], out_specs=(q_spec, lse_spec),
        out_shape=(jax.ShapeDtypeStruct((b, t, d), F32), jax.ShapeDtypeStruct((b, kvh * GROUP, t, HEAD_DIM), F32)),
        compiler_params=_cparams(("parallel", "parallel", "arbitrary"), VMEM_LIMIT),
    )(qkv, qkv, qkv)


def _attn_dense_bwd(qkv, o3, do3, lse, d, ctx_len):
    b, t, _ = qkv.shape
    bq = ATT_BQ
    lq = ctx_len // bq
    kvh, q_spec, k_spec, v_spec, lse_spec, kt_spec = _attn_specs(t, d, bq)

    def body(q_ref, k_ref, v_ref, o_ref, do_ref, lse_ref, dq_ref, dkt_ref, dvt_ref):
        i = pl.program_id(2)

        @pl.when(i == 0)
        def _():
            dkt_ref[...] = jnp.zeros_like(dkt_ref)
            dvt_ref[...] = jnp.zeros_like(dvt_ref)

        def run(k, v, width):
            dk_acc = dv_acc = None
            for g in range(GROUP):
                sl = slice(g * HEAD_DIM, (g + 1) * HEAD_DIM)
                q = q_ref[0, :, sl]
                do = do_ref[0, :, sl]
                dd = jnp.sum(do * o_ref[0, :, sl], axis=1, keepdims=True)
                dob = do.astype(BF16)
                p = jnp.exp2(_dot_nt(q, k) * (SCALE * LOG2E) - lse_ref[0, g][:, 0:1] * LOG2E)
                ds = (p * (_dot_nt(dob, v) - dd) * SCALE).astype(BF16)
                dq_ref[0, :, sl] = _dot(ds, k)
                dk_g = _dot(q.astype(F32).T.astype(BF16), ds)
                dv_g = _dot(do.T.astype(BF16), p.astype(BF16))
                dk_acc = dk_g if dk_acc is None else dk_acc + dk_g
                dv_acc = dv_g if dv_acc is None else dv_acc + dv_g
            dkt_ref[0, 0, :, 0:width] += dk_acc
            dvt_ref[0, 0, :, 0:width] += dv_acc

        @pl.when(i < lq)
        def _():
            run(k_ref[0, 0:ctx_len, :], v_ref[0, 0:ctx_len, :], ctx_len)

        @pl.when(i >= lq)
        def _():
            run(k_ref[0], v_ref[0], t)

    return pl.pallas_call(
        body, name="attn_dense_bwd", grid=(b, kvh, t // bq),
        in_specs=[q_spec, k_spec, v_spec, q_spec, q_spec, lse_spec], out_specs=(q_spec, kt_spec, kt_spec),
        out_shape=(jax.ShapeDtypeStruct((b, t, d), F32), jax.ShapeDtypeStruct((b, kvh, HEAD_DIM, t), F32),
                   jax.ShapeDtypeStruct((b, kvh, HEAD_DIM, t), F32)),
        compiler_params=_cparams(("parallel", "parallel", "arbitrary"), VMEM_LIMIT),
    )(qkv, qkv, qkv, o3, do3, lse)


def _sink_column(sink_ref, h, bq):
    rowi = lax.broadcasted_iota(jnp.int32, (GROUP * bq, 1), 0)
    col = jnp.zeros((GROUP * bq, 1), F32)
    for g in range(GROUP):
        col = jnp.where((rowi >= g * bq) & (rowi < (g + 1) * bq), sink_ref[h * GROUP + g], col)
    return col


def _band(i, lq, ctx_len, t, bq):
    n = i - lq
    start = pl.multiple_of(jnp.clip(ctx_len + (n - 1) * bq, ctx_len, t - WIN_SPAN), bq)
    shape = (GROUP * bq, WIN_SPAN)
    kpos = start - ctx_len + lax.broadcasted_iota(jnp.int32, shape, 1)
    qpos = n * bq + (lax.broadcasted_iota(jnp.int32, shape, 0) & (bq - 1))
    return start, jnp.abs(kpos - qpos) <= WINDOW


def _attn_win_fwd(qkv, sink, d, ctx_len):
    b, t, _ = qkv.shape
    bq = WIN_BQ
    rows = GROUP * bq
    lq = ctx_len // bq
    kvh, q_spec, k_spec, v_spec, lse_spec, _ = _attn_specs(t, d, bq)

    def body(sink_ref, q_ref, k_ref, v_ref, o_ref, lse_ref):
        h, i = pl.program_id(1), pl.program_id(2)
        q4 = _stack_heads(q_ref)
        sink_col = _sink_column(sink_ref, h, bq)
        sc = _dot_nt(q4, k_ref[0, 0:ctx_len, :]) * SCALE
        mc = jnp.maximum(jnp.max(sc, axis=1, keepdims=True), sink_col)

        def finish(m, l, acc):
            _unstack_heads(o_ref, acc / l, bq)
            lse_ref[0] = jnp.broadcast_to(m + jnp.log(l), (rows, HEAD_DIM)).reshape(GROUP, bq, HEAD_DIM)

        @pl.when(i < lq)
        def _():
            pc = jnp.exp(sc - mc)
            l = jnp.sum(pc, axis=1, keepdims=True) + jnp.exp(sink_col - mc)
            finish(mc, l, _dot(pc.astype(BF16), v_ref[0, 0:ctx_len, :]))

        @pl.when(i >= lq)
        def _():
            start, ok = _band(i, lq, ctx_len, t, bq)
            sb = jnp.where(ok, _dot_nt(q4, k_ref[0, pl.ds(start, WIN_SPAN), :]) * SCALE, NEG_INF)
            m = jnp.maximum(mc, jnp.max(sb, axis=1, keepdims=True))
            pc, pb = jnp.exp(sc - m), jnp.exp(sb - m)
            l = jnp.sum(pc, axis=1, keepdims=True) + jnp.sum(pb, axis=1, keepdims=True) + jnp.exp(sink_col - m)
            acc = _dot(pc.astype(BF16), v_ref[0, 0:ctx_len, :]) + _dot(pb.astype(BF16), v_ref[0, pl.ds(start, WIN_SPAN), :])
            finish(m, l, acc)

    return pl.pallas_call(
        body, name="attn_win_fwd", grid=(b, kvh, t // bq),
        in_specs=[pl.BlockSpec(memory_space=pltpu.SMEM), q_spec, k_spec, v_spec], out_specs=(q_spec, lse_spec),
        out_shape=(jax.ShapeDtypeStruct((b, t, d), F32), jax.ShapeDtypeStruct((b, kvh * GROUP, t, HEAD_DIM), F32)),
        compiler_params=_cparams(("parallel", "parallel", "arbitrary"), VMEM_LIMIT),
    )(sink, qkv, qkv, qkv)


def _attn_win_bwd(qkv, sink, o3, do3, lse, d, ctx_len):
    b, t, _ = qkv.shape
    bq = WIN_BQ
    rows = GROUP * bq
    lq = ctx_len // bq
    kvh, q_spec, k_spec, v_spec, lse_spec, kt_spec = _attn_specs(t, d, bq)

    def body(sink_ref, q_ref, k_ref, v_ref, o_ref, do_ref, lse_ref, dq_ref, dkt_ref, dvt_ref, dsk_ref):
        h, i = pl.program_id(1), pl.program_id(2)

        @pl.when(i == 0)
        def _():
            dkt_ref[...] = jnp.zeros_like(dkt_ref)
            dvt_ref[...] = jnp.zeros_like(dvt_ref)
            dsk_ref[...] = jnp.zeros_like(dsk_ref)

        q4 = _stack_heads(q_ref)
        do4 = _stack_heads(do_ref)
        dd = jnp.sum(do4 * _stack_heads(o_ref), axis=1, keepdims=True)
        lse_col = lse_ref[0].reshape(rows, HEAD_DIM)[:, 0:1]
        do4b = do4.astype(BF16)
        qt = q4.astype(F32).T.astype(BF16)
        dot = do4.T.astype(BF16)

        def part(k, v):
            return _dot_nt(q4, k) * SCALE, _dot_nt(do4b, v)

        def grads(p, dp, k):
            ds = (p * (dp - dd) * SCALE).astype(BF16)
            return _dot(ds, k), _dot(qt, ds), _dot(dot, p.astype(BF16))

        kc = k_ref[0, 0:ctx_len, :]
        sc, dpc = part(kc, v_ref[0, 0:ctx_len, :])
        dq_c, dk_c, dv_c = grads(jnp.exp(sc - lse_col), dpc, kc)
        dkt_ref[0, 0, :, 0:ctx_len] += dk_c
        dvt_ref[0, 0, :, 0:ctx_len] += dv_c
        _unstack_heads(dq_ref, dq_c, bq)

        @pl.when(i >= lq)
        def _():
            start, ok = _band(i, lq, ctx_len, t, bq)
            kb = k_ref[0, pl.ds(start, WIN_SPAN), :]
            sb, dpb = part(kb, v_ref[0, pl.ds(start, WIN_SPAN), :])
            pb = jnp.where(ok, jnp.exp(sb - lse_col), 0.0)
            dq_b, dk_b, dv_b = grads(pb, dpb, kb)
            dkt_ref[0, 0, :, pl.ds(start, WIN_SPAN)] += dk_b
            dvt_ref[0, 0, :, pl.ds(start, WIN_SPAN)] += dv_b
            for g in range(GROUP):
                dq_ref[0, :, g * HEAD_DIM:(g + 1) * HEAD_DIM] += dq_b[g * bq:(g + 1) * bq, :]

        ps = jnp.exp(_sink_column(sink_ref, h, bq) - lse_col) * dd
        for g in range(GROUP):
            val = jnp.sum(ps[g * bq:(g + 1) * bq, :], axis=0, keepdims=True)
            dsk_ref[0, 0, g:g + 1, :] -= jnp.broadcast_to(val, (1, HEAD_DIM))

    return pl.pallas_call(
        body, name="attn_win_bwd", grid=(b, kvh, t // bq),
        in_specs=[pl.BlockSpec(memory_space=pltpu.SMEM), q_spec, k_spec, v_spec, q_spec, q_spec, lse_spec],
        out_specs=(q_spec, kt_spec, kt_spec, pl.BlockSpec((1, 1, 8, HEAD_DIM), lambda i, h, j: (i, h, 0, 0))),
        out_shape=(jax.ShapeDtypeStruct((b, t, d), F32), jax.ShapeDtypeStruct((b, kvh, HEAD_DIM, t), F32),
                   jax.ShapeDtypeStruct((b, kvh, HEAD_DIM, t), F32), jax.ShapeDtypeStruct((b, kvh, 8, HEAD_DIM), F32)),
        compiler_params=_cparams(("parallel", "parallel", "arbitrary"), VMEM_LIMIT),
    )(sink, qkv, qkv, qkv, o3, do3, lse)


MERGE_BWD_ROWS = 128


def _resident(shape):
    return pl.BlockSpec(shape, lambda *_: (0,) * len(shape), pipeline_mode=pl.Buffered(1))


def _merge_fwd(x3, ya, yb, yc, proj3, w_br, w_out, modsel, ctx_len):
    b, t, d = x3.shape
    bt = ROW_BLOCK
    lb = ctx_len // bt

    def body(x_ref, ya_ref, yb_ref, yc_ref, gm_ref, wbr_ref, wo_ref, m_ref, xn_ref, out_ref):
        mix = jnp.zeros((bt, d), F32)
        for n, y_ref in enumerate((ya_ref, yb_ref, yc_ref)):
            z = (y_ref[0] * _silu(gm_ref[0, :, n * d:(n + 1) * d])).astype(BF16)
            mix = mix + _sigmoid(gm_ref[0, :, (3 + n) * d:(4 + n) * d]) * _dot(z, wbr_ref[n])
        o = _dot(mix.astype(BF16), wo_ref[...])
        out_ref[0] = o
        xn_ref[0] = x_ref[0] + m_ref[0, 0, 2:3, :] * o

    blk = pl.BlockSpec((1, bt, d), lambda i, j: (i, j, 0))
    return pl.pallas_call(
        body, name="merge_fwd", grid=(b, t // bt),
        in_specs=[blk, blk, blk, blk, pl.BlockSpec((1, bt, 6 * d), lambda i, j: (i, j, 0)),
                  _resident((3, d, d)), _resident((d, d)),
                  pl.BlockSpec((1, 1, 8, d), lambda i, j: (i, _row_kind(j, lb), 0, 0))],
        out_specs=(blk, blk),
        out_shape=(jax.ShapeDtypeStruct((b, t, d), F32), jax.ShapeDtypeStruct((b, t, d), F32)),
        compiler_params=_cparams(("parallel", "arbitrary"), VMEM_LIMIT),
    )(x3, ya, yb, yc, proj3, w_br, w_out, modsel)


def _merge_bwd(dxn3, out3, ya, yb, yc, proj3, w_br, w_out, modsel, ctx_len):
    b, t, d = dxn3.shape
    n_cols = proj3.shape[2]
    bt = MERGE_BWD_ROWS
    lb = ctx_len // bt

    def body(dxn_ref, out_ref, ya_ref, yb_ref, yc_ref, gm_ref, wbr_ref, wo_ref, m_ref,
             dgm_ref, dya_ref, dyb_ref, dyc_ref, z_ref, dt_ref, mix_ref, dout_ref, gacc_ref):
        j = pl.program_id(1)
        dxn = dxn_ref[0]
        doutb = (m_ref[0, 0, 2:3, :] * dxn).astype(BF16)
        dout_ref[0] = doutb

        @pl.when((j == 0) | (j == lb))
        def _():
            gacc_ref[...] = jnp.zeros_like(gacc_ref)

        gacc_ref[0, 0, 0:1, :] += _colsum(dxn * out_ref[0])
        dmix = _dot_nt(doutb, wo_ref[...])
        mix = jnp.zeros((bt, d), F32)
        for n, (y_ref, dy_ref) in enumerate(((ya_ref, dya_ref), (yb_ref, dyb_ref), (yc_ref, dyc_ref))):
            g = gm_ref[0, :, n * d:(n + 1) * d]
            y = y_ref[0]
            sig_g = _sigmoid(g)
            silu_g = g * sig_g
            z = (y * silu_g).astype(BF16)
            z_ref[n, 0] = z
            tn = _dot(z, wbr_ref[n])
            s = _sigmoid(gm_ref[0, :, (3 + n) * d:(4 + n) * d])
            mix = mix + s * tn
            dgm_ref[0, :, (3 + n) * d:(4 + n) * d] = (dmix * tn * (s * (1.0 - s))).astype(BF16)
            dtb = (dmix * s).astype(BF16)
            dt_ref[n, 0] = dtb
            dz = _dot_nt(dtb, wbr_ref[n])
            dy_ref[0] = dz * silu_g
            dgm_ref[0, :, n * d:(n + 1) * d] = (dz * y * (sig_g * (1.0 + g * (1.0 - sig_g)))).astype(BF16)
        mix_ref[0] = mix.astype(BF16)

    blk = pl.BlockSpec((1, bt, d), lambda i, j: (i, j, 0))
    blk4 = pl.BlockSpec((3, 1, bt, d), lambda i, j: (0, i, j, 0))
    wide = pl.BlockSpec((1, bt, 6 * d), lambda i, j: (i, j, 0))
    return pl.pallas_call(
        body, name="merge_bwd", grid=(b, t // bt),
        in_specs=[blk, blk, blk, blk, blk, wide, _resident((3, d, d)), _resident((d, d)),
                  pl.BlockSpec((1, 1, 8, d), lambda i, j: (i, _row_kind(j, lb), 0, 0))],
        out_specs=(wide, blk, blk, blk, blk4, blk4, blk, blk,
                   pl.BlockSpec((1, 1, 8, d), lambda i, j: (i, _row_kind(j, lb), 0, 0))),
        out_shape=(jax.ShapeDtypeStruct((b, t, n_cols), BF16),
                   jax.ShapeDtypeStruct((b, t, d), F32), jax.ShapeDtypeStruct((b, t, d), F32),
                   jax.ShapeDtypeStruct((b, t, d), F32),
                   jax.ShapeDtypeStruct((3, b, t, d), BF16), jax.ShapeDtypeStruct((3, b, t, d), BF16),
                   jax.ShapeDtypeStruct((b, t, d), BF16), jax.ShapeDtypeStruct((b, t, d), BF16),
                   jax.ShapeDtypeStruct((b, 2, 8, d), F32)),
        compiler_params=_cparams(("parallel", "arbitrary"), VMEM_LIMIT),
    )(dxn3, out3, ya, yb, yc, proj3, w_br, w_out, modsel)


def _final(x3, g, target, ctx_len):
    b, t, d = x3.shape
    bt = ROW_BLOCK
    lb = ctx_len // bt

    def body(x_ref, g_ref, t_ref, dx_ref, loss_ref, dg_ref):
        j = pl.program_id(1)

        @pl.when(j == 0)
        def _():
            loss_ref[...] = jnp.zeros_like(loss_ref)
            dg_ref[...] = jnp.zeros_like(dg_ref)

        @pl.when(j < lb)
        def _():
            dx_ref[...] = jnp.zeros_like(dx_ref)

        @pl.when(j >= lb)
        def _():
            x = x_ref[0]
            g_row = g_ref[...]
            rstd = lax.rsqrt(jnp.mean(x * x, axis=-1, keepdims=True) + EPS)
            xhat = x * rstd
            err = xhat * g_row - t_ref[0]
            loss_ref[...] += (0.5 / d) * jnp.sum(err * err)
            dy = err * (1.0 / d)
            dg_ref[0, 0:1, :] += _colsum(dy * xhat)
            dxhat = dy * g_row
            dx_ref[0] = rstd * (dxhat - xhat * jnp.mean(dxhat * xhat, axis=-1, keepdims=True))

    blk = pl.BlockSpec((1, bt, d), lambda i, j: (i, j, 0))
    return pl.pallas_call(
        body, name="final_loss", grid=(b, t // bt),
        in_specs=[blk, pl.BlockSpec((1, d), lambda i, j: (0, 0)),
                  pl.BlockSpec((1, bt, d), lambda i, j: (i, jnp.maximum(j - lb, 0), 0))],
        out_specs=(blk, pl.BlockSpec((1, 8, HEAD_DIM), lambda i, j: (i, 0, 0)), pl.BlockSpec((1, 8, d), lambda i, j: (i, 0, 0))),
        out_shape=(jax.ShapeDtypeStruct((b, t, d), F32), jax.ShapeDtypeStruct((b, 8, HEAD_DIM), F32),
                   jax.ShapeDtypeStruct((b, 8, d), F32)),
        compiler_params=_cparams(("parallel", "arbitrary")),
    )(x3, g, target)


TOKEN_BLOCKS = (1088, 512, 256, 128)


def _pick(n, options):
    for o in options:
        if n % o == 0:
            return o
    raise ValueError((n, options))


def _block_diag(w):
    per = LRU_LANES // LRU_BLOCK_W
    nd, nb, bw, _ = w.shape
    wr = w.reshape(nd, nb // per, per, bw, bw)
    eye = jnp.eye(per, dtype=w.dtype)
    bd = wr[:, :, :, :, None, :] * eye[None, None, :, None, :, None]
    return bd.reshape(nd, nb // per, per * bw, per * bw).astype(BF16)


def _block_diag_grad(g):
    per = LRU_LANES // LRU_BLOCK_W
    nd, ng, _, _ = g.shape
    gr = g.reshape(nd, ng, per, LRU_BLOCK_W, per, LRU_BLOCK_W)
    diag = jnp.stack([gr[:, :, k, :, k, :] for k in range(per)], axis=2)
    return diag.reshape(nd, ng * per, LRU_BLOCK_W, LRU_BLOCK_W)


def _mod_select(mod16, b, d):
    m3 = mod16.reshape(MOD_ROWS, 3, d)
    lat = m3[:b]
    ctx = jnp.broadcast_to(m3[b][None], (b, 3, d))
    sel = jnp.stack([ctx, lat], axis=1)
    return jnp.pad(sel, ((0, 0), (0, 0), (0, 5), (0, 0)))


def _layer_fwd(x3, c16, p, cos, sin, ctx_len):
    b, t, d = x3.shape
    off, n_cols = _layout(d)
    mod16 = _mod_fwd(c16, p["w_mod"], p["b_mod"])
    modsel = _mod_select(mod16, b, d)
    h = _norm_mod_fwd(x3, p["norm_g"], modsel, ctx_len)
    proj = _matmul(h.reshape(b * t, d), p["w_in"], bm=_pick(b * t, TOKEN_BLOCKS), bn=1024, bk=d, name="proj_fwd",
                   n_outer=True)
    proj3 = proj.reshape(b, t, n_cols)
    ya = _lru_fwd(proj3, off["uA"], p["conv_w"], p["conv_b"], p["wa_bd"], p["ba"], p["wx_bd"], p["bx"], p["lam"], ctx_len)
    qkv_b = _prep_fwd(proj3, off["qB"], off["kB"], d, cos, sin, p["gq"], p["gk"], use_norm=False)
    yb, lse_b = _attn_win_fwd(qkv_b, p["sink"], d, ctx_len)
    qkv_c = _prep_fwd(proj3, off["qC"], off["kC"], d, cos, sin, p["gq"], p["gk"], use_norm=True)
    yc, lse_c = _attn_dense_fwd(qkv_c, d, ctx_len)
    x_new, out3 = _merge_fwd(x3, ya, yb, yc, proj3, p["w_br"], p["w_out"], modsel, ctx_len)
    return x_new, (x3, modsel, h, proj3, ya, yb, yc, qkv_b, lse_b, qkv_c, lse_c, out3)


def _layer_bwd(dxn3, saved, c16, p, cos, sin, ctx_len):
    x3, modsel, h, proj3, ya, yb, yc, qkv_b, lse_b, qkv_c, lse_c, out3 = saved
    b, t, d = x3.shape
    off, n_cols = _layout(d)
    rows = b * t
    bk = _pick(rows, TOKEN_BLOCKS)
    dproj3, dya, dyb, dyc, z4, dt4, mixb, doutb, gacc = _merge_bwd(dxn3, out3, ya, yb, yc, proj3, p["w_br"], p["w_out"],
                                                                   modsel, ctx_len)
    dw_br = jnp.stack([_matmul(z4[n].reshape(rows, d), dt4[n].reshape(rows, d), ta=True, bm=d, bn=d, bk=bk,
                               name="dw_branch") for n in range(3)])
    dw_out = _matmul(mixb.reshape(rows, d), doutb.reshape(rows, d), ta=True, bm=d, bn=d, bk=bk, name="dw_out")
    dproj3, vec, dwa, dwx = _lru_bwd(dproj3, proj3, off["uA"], dya, p["conv_w"], p["conv_b"], p["wa_bd"], p["ba"],
                                     p["wx_bd"], p["bx"], p["lam"], ctx_len)
    dq_b, dkt_b, dvt_b, dsk = _attn_win_bwd(qkv_b, p["sink"], yb, dyb, lse_b, d, ctx_len)
    dproj3, _ = _prep_bwd(dproj3, dq_b, dkt_b, dvt_b, proj3, off["qB"], off["kB"], d, cos, sin, p["gq"], p["gk"], False)
    dq_c, dkt_c, dvt_c = _attn_dense_bwd(qkv_c, yc, dyc, lse_c, d, ctx_len)
    dproj3, gqk = _prep_bwd(dproj3, dq_c, dkt_c, dvt_c, proj3, off["qC"], off["kC"], d, cos, sin, p["gq"], p["gk"], True)
    dproj2 = dproj3.reshape(rows, n_cols)
    dw_in = _matmul(h.reshape(rows, d), dproj2, ta=True, bm=d, bn=1024, bk=bk, name="dw_in")
    dh = _matmul(dproj2, p["w_in"], tb=True, bm=_pick(rows, TOKEN_BLOCKS), bn=d, bk=1024, name="dh")
    dx3, nacc = _norm_mod_bwd(dh.reshape(b, t, d), x3, p["norm_g"], modsel, dxn3, ctx_len)
    per = jnp.stack([nacc[:, :, 0], nacc[:, :, 1], gacc[:, :, 0]], axis=2)
    dmod = jnp.concatenate([per[:, 1].reshape(b, 3 * d), jnp.sum(per[:, 0], axis=0).reshape(1, 3 * d)], axis=0)
    dmod16 = jnp.pad(dmod, ((0, MOD_ROWS - b - 1), (0, 0)))
    dw_mod, db_mod, dc16 = _mod_bwd(c16, dmod16, p["w_mod"])
    vsum = jnp.sum(vec, axis=0)
    grads = {
        "norm_g": jnp.sum(nacc[:, :, 2], axis=(0, 1)),
        "w_mod": dw_mod, "b_mod": db_mod[0], "w_in": dw_in,
        "conv_w": vsum[0:4], "conv_b": vsum[4],
        "lru_wa": _block_diag_grad(jnp.sum(dwa, axis=0)), "lru_ba": vsum[5:7],
        "lru_wx": _block_diag_grad(jnp.sum(dwx, axis=0)), "lru_bx": vsum[7:9],
        "lru_lambda": vsum[9:11] * (-jax.nn.sigmoid(-p["lam"])),
        "attn_sink": jnp.sum(dsk[:, :, 0:GROUP, 0], axis=0).reshape(-1),
        "q_norm_g": jnp.sum(gqk[:, 0], axis=0), "k_norm_g": jnp.sum(gqk[:, 1], axis=0),
        "w_branch": dw_br, "w_out": dw_out,
    }
    return dx3, dc16, grads


def _reorder_in_cols(w, d, inverse=False):
    off_new, _ = _layout(d)
    segs = _orig_segments(d)
    if inverse:
        return jnp.concatenate([w[..., off_new[n]:off_new[n] + wd] for n, _, wd in segs], axis=-1)
    by_name = {n: (o, wd) for n, o, wd in segs}
    order = sorted(off_new, key=off_new.get)
    return jnp.concatenate([w[..., by_name[n][0]:by_name[n][0] + by_name[n][1]] for n in order], axis=-1)


def _layer_params(li, w):
    d = w["norm_g"].shape[1]
    return {
        "norm_g": w["norm_g"][li][None], "w_mod": w["w_mod"][li], "b_mod": w["b_mod"][li][None],
        "w_in": _reorder_in_cols(w["w_in"][li], d),
        "conv_w": w["conv_w"][li], "conv_b": w["conv_b"][li][None],
        "wa_bd": _block_diag(w["lru_wa"][li]), "ba": w["lru_ba"][li],
        "wx_bd": _block_diag(w["lru_wx"][li]), "bx": w["lru_bx"][li], "lam": w["lru_lambda"][li],
        "sink": w["attn_sink"][li], "gq": w["q_norm_g"][li][None], "gk": w["k_norm_g"][li][None],
        "w_br": w["w_branch"][li], "w_out": w["w_out"][li],
    }


def _local_step(x, c, ctx, target, c_ctx, final_g, layers):
    b, s, d = x.shape
    ctx_len = ctx.shape[1]
    cos, sin = _rope_tables(ctx_len, s)
    x3 = jnp.concatenate([ctx, x], axis=1)
    c16 = jnp.concatenate([c, c_ctx[None], jnp.zeros((MOD_ROWS - b - 1, d), F32)], axis=0)
    saved = []
    for p in layers:
        x3, sv = _layer_fwd(x3, c16, p, cos, sin, ctx_len)
        saved.append(sv)
    dx3, loss_acc, dgf = _final(x3, final_g[None], target, ctx_len)
    grads = [None] * len(layers)
    dc_ctx = jnp.zeros((d,), F32)
    for li in reversed(range(len(layers))):
        dx3, dc16, grads[li] = _layer_bwd(dx3, saved[li], c16, layers[li], cos, sin, ctx_len)
        dc_ctx = dc_ctx + dc16[b]
    return jnp.sum(loss_acc[:, 0, 0]), dx3[:, ctx_len:], dc_ctx, jnp.sum(dgf[:, 0], axis=0), grads


N_CHIPS = 4
ANY = pl.BlockSpec(memory_space=pl.ANY)


def _place():
    x, y, c = lax.axis_index("x"), lax.axis_index("y"), lax.axis_index("c")
    return x, y, c, [(1 - x, y), (x, 1 - y), (1 - x, 1 - y)]


def _axis_part(ref, axis, start, size):
    idx = [slice(None)] * len(ref.shape)
    idx[axis] = pl.ds(start, size)
    return ref.at[tuple(idx)]


def _remote(src, dst, send, recv, dev):
    return pltpu.make_async_remote_copy(src_ref=src, dst_ref=dst, send_sem=send, recv_sem=recv, device_id=dev,
                                        device_id_type=MESH)


COPY_PIECES = 8


def _pieces(src, dst):
    shape = src.shape
    for ax in range(len(shape) - 1):
        if shape[ax] % COPY_PIECES == 0 and shape[ax] // COPY_PIECES >= 8:
            sz = shape[ax] // COPY_PIECES
            return [(_axis_part(src, ax, j * sz, sz), _axis_part(dst, ax, j * sz, sz)) for j in range(COPY_PIECES)]
    return [(src, dst)]


def _gather_chips(wholes, axes, name):
    n = len(wholes)

    def body(*refs):
        bufs = refs[n:2 * n]
        send, recv, fsend, frecv = refs[2 * n:]
        x, y, c, chips = _place()
        me = 2 * x + y
        sib = (x, y, 1 - c)

        def block(i, chip_index, half):
            sz = wholes[i].shape[axes[i]] // N_CHIPS
            hl = wholes[i].shape[0] // 2
            return _axis_part(bufs[i], axes[i], chip_index * sz, sz).at[pl.ds(half * hl, hl)]

        for i in range(n):
            for k, (px, py) in enumerate(chips):
                _remote(block(i, me, c), block(i, me, c), send.at[i, k], recv.at[i, k], (px, py, c)).start()
        for i in range(n):
            for k, (px, py) in enumerate(chips):
                landed = block(i, 2 * px + py, c)
                _remote(landed, landed, send.at[i, k], recv.at[i, k], (px, py, c)).wait_recv()
                for s_, d_ in _pieces(landed, landed):
                    _remote(s_, d_, fsend.at[i, k], frecv.at[i, k], sib).start()
        for i in range(n):
            for k, (px, py) in enumerate(chips):
                passed = _remote(block(i, 2 * px + py, c), block(i, 2 * px + py, 1 - c), fsend.at[i, k], frecv.at[i, k], sib)
                passed.wait_recv()
                passed.wait_send()
                _remote(block(i, me, c), block(i, me, c), send.at[i, k], recv.at[i, k], (px, py, c)).wait_send()

    sems = pltpu.SemaphoreType.DMA((n, 3))
    return pl.pallas_call(
        body, name=name, in_specs=[ANY] * n, out_specs=tuple([ANY] * n),
        out_shape=tuple(jax.ShapeDtypeStruct(a.shape, a.dtype) for a in wholes),
        input_output_aliases={i: i for i in range(n)},
        scratch_shapes=[sems, sems, sems, sems],
    )(*wholes)


def _own_block_placed(shard, axis, chip):
    shape = list(shard.shape)
    shape[axis] *= N_CHIPS
    return lax.dynamic_update_slice_in_dim(lax.empty(tuple(shape), shard.dtype), shard, chip * shard.shape[axis], axis)


def _split_cores(gs, name):
    n = len(gs)

    def body(*refs):
        ins, got = refs[:n], refs[n:2 * n]
        send, recv = refs[2 * n:]
        x, y, c, _ = _place()
        sib = (x, y, 1 - c)

        def theirs(i):
            hl = gs[i].shape[0] // 2
            return ins[i].at[pl.ds((1 - c) * hl, hl)]

        for i in range(n):
            for s_, d_ in _pieces(theirs(i), got[i]):
                _remote(s_, d_, send.at[i], recv.at[i], sib).start()
        for i in range(n):
            _remote(theirs(i), got[i], send.at[i], recv.at[i], sib).wait()

    return pl.pallas_call(
        body, name=name, in_specs=[ANY] * n, out_specs=tuple([ANY] * n),
        out_shape=tuple(jax.ShapeDtypeStruct((g.shape[0] // 2,) + g.shape[1:], g.dtype) for g in gs),
        scratch_shapes=[pltpu.SemaphoreType.DMA((n,)), pltpu.SemaphoreType.DMA((n,))],
    )(*gs)


def _scatter_chips(pbs, axes, name):
    n = len(pbs)

    def block(p, ax):
        shape = list(p.shape)
        shape[ax] //= N_CHIPS
        return tuple(shape)

    def body(*refs):
        inb, got = refs[:n], refs[n:2 * n]
        send, recv = refs[2 * n:]
        x, y, c, chips = _place()

        def part(i, chip_index):
            sz = pbs[i].shape[axes[i]] // N_CHIPS
            return _axis_part(inb[i], axes[i], chip_index * sz, sz)

        for i in range(n):
            for k, (px, py) in enumerate(chips):
                _remote(part(i, 2 * px + py), got[i].at[k], send.at[i, k], recv.at[i, k], (px, py, c)).start()
        for i in range(n):
            for k, (px, py) in enumerate(chips):
                _remote(part(i, 2 * px + py), got[i].at[k], send.at[i, k], recv.at[i, k], (px, py, c)).wait()

    return pl.pallas_call(
        body, name=name, in_specs=[ANY] * n, out_specs=tuple([ANY] * n),
        out_shape=tuple(jax.ShapeDtypeStruct((3,) + block(p, ax), p.dtype) for p, ax in zip(pbs, axes)),
        scratch_shapes=[pltpu.SemaphoreType.DMA((n, 3)), pltpu.SemaphoreType.DMA((n, 3))],
    )(*pbs)


def _join_cores(bufs, name):
    n = len(bufs)

    def body(*refs):
        outs = refs[n:2 * n]
        send, recv = refs[2 * n:]
        x, y, c, _ = _place()
        sib = (x, y, 1 - c)

        def half(i, which):
            hl = bufs[i].shape[0] // 2
            return outs[i].at[pl.ds(which * hl, hl)]

        for i in range(n):
            for s_, d_ in _pieces(half(i, c), half(i, c)):
                _remote(s_, d_, send.at[i], recv.at[i], sib).start()
        for i in range(n):
            cp = _remote(half(i, c), half(i, 1 - c), send.at[i], recv.at[i], sib)
            cp.wait_recv()
            cp.wait_send()

    return pl.pallas_call(
        body, name=name, in_specs=[ANY] * n, out_specs=tuple([ANY] * n),
        out_shape=tuple(jax.ShapeDtypeStruct(a.shape, a.dtype) for a in bufs),
        input_output_aliases={i: i for i in range(n)},
        scratch_shapes=[pltpu.SemaphoreType.DMA((n,)), pltpu.SemaphoreType.DMA((n,))],
    )(*bufs)


def _all_reduce_small(buf):
    r = buf.shape[0]

    def body(in_ref, out_ref, sib_buf, chip_sum, got, send, recv):
        x, y, c, chips = _place()
        cp = _remote(in_ref, sib_buf, send.at[0], recv.at[0], (x, y, 1 - c))
        cp.start()
        cp.wait()
        chip_sum[...] = in_ref[...] + sib_buf[...]
        cps = [_remote(chip_sum, got.at[k], send.at[1 + k], recv.at[1 + k], (px, py, c)) for k, (px, py) in enumerate(chips)]
        for cp in cps:
            cp.start()
        for cp in cps:
            cp.wait()
        out_ref[...] = (chip_sum[...] + got[0]) + (got[1] + got[2])

    return pl.pallas_call(
        body, name="all_reduce_small", out_shape=jax.ShapeDtypeStruct(buf.shape, F32),
        in_specs=[pl.BlockSpec(memory_space=pltpu.VMEM)], out_specs=pl.BlockSpec(memory_space=pltpu.VMEM),
        scratch_shapes=[pltpu.VMEM((r, 128), F32), pltpu.VMEM((r, 128), F32), pltpu.VMEM((3, r, 128), F32),
                        pltpu.SemaphoreType.DMA((4,)), pltpu.SemaphoreType.DMA((4,))],
        compiler_params=_cparams(None, VMEM_LIMIT),
    )(buf)


ELEMENTWISE_BLOCK_BYTES = 1 << 20


def _view2d(a):
    cols = a.shape[-1] if a.ndim > 1 else 128
    return a.reshape(-1, cols)


def _row_block(rows, cols):
    want = max(8, ELEMENTWISE_BLOCK_BYTES // (4 * cols))
    br = rows
    while br > want and br % 2 == 0 and (br // 2) % 16 == 0:
        br //= 2
    return br


def _core():
    return lax.axis_index("c")


def _chip():
    return 2 * lax.axis_index("x") + lax.axis_index("y")


def _sum_half(g, got, name):
    h = got.shape[0]
    gv = g.reshape(2 * h, -1, g.shape[-1])
    tv = got.reshape(h, -1, g.shape[-1])
    _, rows, cols = tv.shape
    br = _row_block(rows, cols)

    def body(g_ref, t_ref, p_ref, pb_ref):
        p = g_ref[...] + t_ref[...]
        p_ref[...] = p
        pb_ref[...] = p.astype(BF16)

    blk = pl.BlockSpec((1, br, cols), lambda l, i: (l, i, 0))
    p, pb = pl.pallas_call(
        body, name=name, grid=(h, rows // br),
        in_specs=[pl.BlockSpec((1, br, cols), lambda l, i: (_core() * h + l, i, 0)), blk], out_specs=(blk, blk),
        out_shape=(jax.ShapeDtypeStruct(tv.shape, F32), jax.ShapeDtypeStruct(tv.shape, BF16)),
        compiler_params=_cparams(("parallel", "parallel")))(gv, tv)
    return p.reshape(got.shape), pb.reshape(got.shape)


def _sum_blocks(p, got3, axis, name):
    h = p.shape[0]
    blk_shape = got3.shape[1:]
    cols_mode = axis == p.ndim - 1
    pv = p.reshape(-1, p.shape[-2], p.shape[-1])
    tv = got3.reshape(3, -1, blk_shape[-2], blk_shape[-1])
    la, rb, cb = tv.shape[1:]
    assert cols_mode or axis == p.ndim - 2
    br = _row_block(rb, cb)
    per = rb // br

    def body(p_ref, a_ref, b_ref, c_ref, out_ref):
        out_ref[...] = ((p_ref[...] + a_ref[0].astype(F32)) + b_ref[0].astype(F32)) + c_ref[0].astype(F32)

    if cols_mode:
        p_spec = pl.BlockSpec((1, br, cb), lambda l, i: (l, i, _chip()))
    else:
        p_spec = pl.BlockSpec((1, br, cb), lambda l, i: (l, _chip() * per + i, 0))
    out = pl.pallas_call(
        body, name=name, grid=(la, per),
        in_specs=[p_spec] + [pl.BlockSpec((1, 1, br, cb), lambda l, i, k=k: (k, l, i, 0)) for k in range(3)],
        out_specs=pl.BlockSpec((1, br, cb), lambda l, i: (_core() * la + l, i, 0)),
        out_shape=jax.ShapeDtypeStruct((2 * la, rb, cb), F32),
        compiler_params=_cparams(("parallel", "parallel")))(pv, tv, tv, tv)
    return out.reshape((2 * h,) + blk_shape[1:])


def _adamw(w, g, m, v, name):
    shape = w.shape
    ops = [_view2d(a) for a in (w, g, m, v)]
    rows, cols = ops[0].shape
    br = _row_block(rows, cols)
    c1 = 1.0 - ADAM_B1 ** ADAM_STEP
    c2 = 1.0 - ADAM_B2 ** ADAM_STEP

    def body(w_ref, g_ref, m_ref, v_ref, d_ref, nm_ref, nv_ref):
        g_ = g_ref[...]
        nm = ADAM_B1 * m_ref[...] + (1.0 - ADAM_B1) * g_
        nv = ADAM_B2 * v_ref[...] + (1.0 - ADAM_B2) * (g_ * g_)
        d_ref[...] = -ADAM_LR * ((nm / c1) / (jnp.sqrt(nv / c2) + ADAM_EPS) + ADAM_WD * w_ref[...])
        nm_ref[...] = nm
        nv_ref[...] = nv

    blk = pl.BlockSpec((br, cols), lambda i: (i, 0))
    outs = pl.pallas_call(body, name=name, grid=(rows // br,), in_specs=[blk] * 4, out_specs=(blk, blk, blk),
                          out_shape=tuple(jax.ShapeDtypeStruct((rows, cols), F32) for _ in range(3)),
                          compiler_params=_cparams(("parallel",)))(*ops)
    return tuple(o.reshape(shape) for o in outs)


def _pack(arrays):
    flat = jnp.concatenate([a.reshape(-1) for a in arrays])
    pad = (-flat.shape[0]) % (8 * 128)
    return jnp.pad(flat, (0, pad)).reshape(-1, 128)


def _unpack(buf, shapes):
    flat = buf.reshape(-1)
    out, o = [], 0
    for s in shapes:
        n = int(np.prod(s))
        out.append(flat[o:o + n].reshape(s))
        o += n
    return out


WEIGHTS = ["c_ctx", "norm_g", "w_mod", "b_mod", "w_in", "conv_w", "conv_b", "lru_wa", "lru_ba", "lru_wx", "lru_bx",
           "lru_lambda", "attn_sink", "q_norm_g", "k_norm_g", "w_branch", "w_out", "final_g"]
BIG = {"w_mod": 2, "w_in": 2, "w_branch": 2, "w_out": 1}
SMALL_SHARDED = ["conv_w", "lru_ba", "lru_bx", "lru_lambda"]
REPLICATED = [n for n in WEIGHTS if n not in BIG and n not in SMALL_SHARDED]


def kernel(x, c, ctx, c_ctx, norm_g, w_mod, b_mod, w_in, conv_w, conv_b, lru_wa, lru_ba, lru_wx, lru_bx, lru_lambda, attn_sink, q_norm_g, k_norm_g, w_branch, w_out, final_g, loss_target, m_c_ctx, m_norm_g, m_w_mod, m_b_mod, m_w_in, m_conv_w, m_conv_b, m_lru_wa, m_lru_ba, m_lru_wx, m_lru_bx, m_lru_lambda, m_attn_sink, m_q_norm_g, m_k_norm_g, m_w_branch, m_w_out, m_final_g, v_c_ctx, v_norm_g, v_w_mod, v_b_mod, v_w_in, v_conv_w, v_conv_b, v_lru_wa, v_lru_ba, v_lru_wx, v_lru_bx, v_lru_lambda, v_attn_sink, v_q_norm_g, v_k_norm_g, v_w_branch, v_w_out, v_final_g):
    args = dict(locals())
    w = {n: args[n] for n in WEIGHTS}
    mom = {n: args["m_" + n] for n in WEIGHTS}
    var = {n: args["v_" + n] for n in WEIGHTS}
    depth, d = norm_g.shape
    chip = 2 * lax.axis_index("x") + lax.axis_index("y")

    big_names = list(BIG)
    small_shard = jnp.concatenate([w[n] for n in SMALL_SHARDED], axis=1)
    gather_axes = [BIG[n] for n in big_names] + [2]
    placed = [_own_block_placed(a, ax, chip)
              for a, ax in zip([w[n].astype(BF16) for n in big_names] + [small_shard], gather_axes)]
    gathered = _gather_chips(placed, gather_axes, "gather_weights")
    whole = dict(w)
    whole.update(dict(zip(big_names, gathered[:-1])))
    o = 0
    for n in SMALL_SHARDED:
        rows = w[n].shape[1]
        whole[n] = gathered[-1][:, o:o + rows]
        o += rows
    layers = [_layer_params(li, whole) for li in range(depth)]

    loss_local, grad_x, g_c_ctx, g_final, lgrads = _local_step(x, c, ctx, loss_target, c_ctx, final_g, layers)
    loss = lax.psum(loss_local, ("x", "y", "c"))
    full = {n: jnp.stack([lg[n] for lg in lgrads]) for n in lgrads[0]}
    full["w_in"] = _reorder_in_cols(full["w_in"], d, inverse=True)
    full["c_ctx"], full["final_g"] = g_c_ctx, g_final

    bigs = [full[n] for n in big_names]
    got = _split_cores(bigs, "grad_split_cores")
    parts = [_sum_half(g, t_, "grad_chip_sum") for g, t_ in zip(bigs, got)]
    recv = _scatter_chips([pb for _, pb in parts], [BIG[n] for n in big_names], "grad_scatter_chips")
    totals = [_sum_blocks(p_, r, BIG[n], "grad_total") for (p_, _), r, n in zip(parts, recv, big_names)]
    grad = dict(zip(big_names, _join_cores(totals, "grad_join_cores")))

    small_names = REPLICATED + SMALL_SHARDED
    reduced = _unpack(_all_reduce_small(_pack([full[n] for n in small_names])), [full[n].shape for n in small_names])
    for n, g in zip(small_names, reduced):
        if n in SMALL_SHARDED:
            sz = w[n].shape[-1]
            g = lax.dynamic_slice_in_dim(g, chip * sz, sz, axis=g.ndim - 1)
        grad[n] = g

    delta, new_m, new_v = {}, {}, {}
    for n in big_names:
        delta[n], new_m[n], new_v[n] = _adamw(w[n], grad[n], mom[n], var[n], "adamw_" + n)
    shapes = [w[n].shape for n in small_names]
    packed = _adamw(_pack([w[n] for n in small_names]), _pack([grad[n] for n in small_names]),
                    _pack([mom[n] for n in small_names]), _pack([var[n] for n in small_names]), "adamw_small")
    for res, p in zip((delta, new_m, new_v), packed):
        res.update(dict(zip(small_names, _unpack(p, shapes))))

    return (loss, grad_x, *[grad[n] for n in WEIGHTS], *[delta[n] for n in WEIGHTS],
            *[new_m[n] for n in WEIGHTS], *[new_v[n] for n in WEIGHTS])
```

```python
import functools

import jax
import jax.numpy as jnp
import numpy as np
from jax import lax
from jax.experimental import pallas as pl
from jax.experimental.pallas import tpu as pltpu

F32 = jnp.float32
BF16 = jnp.bfloat16

HEAD_DIM = 128
GROUP = 4
LRU_BLOCK_W = 64
LRU_C = 8.0
WINDOW = 128
GRID_W = 64
ROPE_THETA = 10000.0
EPS = 1e-6
NEG_INF = -1e30
ADAM_LR, ADAM_B1, ADAM_B2, ADAM_EPS, ADAM_WD, ADAM_STEP = 0.001, 0.9, 0.999, 1e-08, 0.01, 10

ROW_BLOCK = 256
LRU_LANES = 128
LRU_CHUNK = 128
LRU_UNROLL = 2
DENSE_FWD_BQ = 256
LOG2E = 1.4426950408889634
ATT_BQ = 256
WIN_BQ = 256
WIN_SPAN = WIN_BQ + 2 * WINDOW
MOD_ROWS = 16
VMEM_LIMIT = 56 * 1024 * 1024

MESH = pl.DeviceIdType.MESH


def _cparams(sem=None, vmem=None):
    kw = {}
    if sem is not None:
        kw["dimension_semantics"] = sem
    if vmem is not None:
        kw["vmem_limit_bytes"] = vmem
    return pltpu.CompilerParams(**kw)


def _sigmoid(v):
    return 1.0 / (1.0 + jnp.exp(-v))


def _silu(v):
    return v * _sigmoid(v)


def _dsilu(v):
    s = _sigmoid(v)
    return s * (1.0 + v * (1.0 - s))


def _expm1(x):
    poly = x * (1.0 + x * (0.5 + x * (1.0 / 6 + x * (1.0 / 24 + x * (1.0 / 120 + x * (1.0 / 720 + x * (1.0 / 5040)))))))
    return jnp.where(jnp.abs(x) < 0.25, poly, jnp.exp(x) - 1.0)


def _log1p(y):
    u = 1.0 + y
    d = u - 1.0
    return jnp.where(d == 0.0, y, jnp.log(u) * (y / jnp.where(d == 0.0, 1.0, d)))


def _softplus(x):
    return jnp.maximum(x, 0.0) + _log1p(jnp.exp(-jnp.abs(x)))


def _dot(a, b):
    return jnp.dot(a, b, preferred_element_type=F32)


def _dot_nt(a, b):
    return lax.dot_general(a, b, (((1,), (1,)), ((), ())), preferred_element_type=F32)


def _dot_tn(a, b):
    return lax.dot_general(a, b, (((0,), (0,)), ((), ())), preferred_element_type=F32)


def _colsum(v):
    return jnp.sum(v, axis=0, keepdims=True)


def _layout(d_model):
    kvw = (d_model // HEAD_DIM // GROUP) * HEAD_DIM
    names = ["gA", "gB", "gC", "mA", "mB", "mC", "uA", "qB", "qC", "kB", "vB", "kC", "vC"]
    widths = [d_model] * 9 + [kvw] * 4
    off, o = {}, 0
    for n, w in zip(names, widths):
        off[n] = o
        o += w
    return off, o


def _orig_segments(d_model):
    kvw = (d_model // HEAD_DIM // GROUP) * HEAD_DIM
    names = ["uA", "gA", "qB", "kB", "vB", "gB", "qC", "kC", "vC", "gC", "mA", "mB", "mC"]
    widths = [d_model, d_model, d_model, kvw, kvw, d_model, d_model, kvw, kvw, d_model, d_model, d_model, d_model]
    out, o = [], 0
    for n, w in zip(names, widths):
        out.append((n, o, w))
        o += w
    return out


def _matmul(a, b, *, ta=False, tb=False, out_dtype=F32, bm, bn, bk, name, n_outer=False):
    (kdim, m) = a.shape if ta else a.shape[::-1]
    (n, kdim2) = b.shape if tb else b.shape[::-1]
    assert kdim == kdim2 and m % bm == 0 and n % bn == 0 and kdim % bk == 0, (a.shape, b.shape, bm, bn, bk)
    nk = kdim // bk
    dims = (((0 if ta else 1,), (1 if tb else 0,)), ((), ()))

    def ij(f):
        return (lambda j, i, k: f(i, j, k)) if n_outer else f

    def body(a_ref, b_ref, o_ref, *scratch):
        r = lax.dot_general(a_ref[...].astype(BF16), b_ref[...].astype(BF16), dims, preferred_element_type=F32)
        if nk == 1:
            o_ref[...] = r.astype(out_dtype)
        else:
            acc = scratch[0]
            k = pl.program_id(2)

            @pl.when(k == 0)
            def _():
                acc[...] = r

            @pl.when(k > 0)
            def _():
                acc[...] += r

            @pl.when(k == nk - 1)
            def _():
                o_ref[...] = acc[...].astype(out_dtype)

    a_spec = pl.BlockSpec((bk, bm), ij(lambda i, j, k: (k, i))) if ta else pl.BlockSpec((bm, bk), ij(lambda i, j, k: (i, k)))
    b_spec = pl.BlockSpec((bn, bk), ij(lambda i, j, k: (j, k))) if tb else pl.BlockSpec((bk, bn), ij(lambda i, j, k: (k, j)))
    return pl.pallas_call(
        body, name=name, grid=(n // bn, m // bm, nk) if n_outer else (m // bm, n // bn, nk),
        in_specs=[a_spec, b_spec], out_specs=pl.BlockSpec((bm, bn), ij(lambda i, j, k: (i, j))),
        out_shape=jax.ShapeDtypeStruct((m, n), out_dtype),
        scratch_shapes=[pltpu.VMEM((bm, bn), F32)] if nk > 1 else [],
        compiler_params=_cparams(("parallel", "parallel", "arbitrary"), VMEM_LIMIT),
    )(a, b)


def _mod_fwd(c16, w_mod, b_mod):
    d3 = w_mod.shape[1]

    def body(c_ref, w_ref, b_ref, o_ref):
        o_ref[...] = _dot(_silu(c_ref[...]).astype(BF16), w_ref[...]) + b_ref[...]

    return pl.pallas_call(body, name="mod_fwd", out_shape=jax.ShapeDtypeStruct((MOD_ROWS, d3), F32),
                          compiler_params=_cparams(None, VMEM_LIMIT))(c16, w_mod, b_mod)


def _mod_bwd(c16, dmod16, w_mod):
    d, d3 = w_mod.shape

    def body(c_ref, g_ref, w_ref, dw_ref, db_ref, dc_ref):
        c = c_ref[...]
        g = g_ref[...]
        gb = g.astype(BF16)
        dw_ref[...] = _dot_tn(_silu(c).astype(BF16), gb)
        db_ref[...] = _colsum(g)
        dc_ref[...] = _dot_nt(gb, w_ref[...]) * _dsilu(c)

    return pl.pallas_call(
        body, name="mod_bwd",
        out_shape=(jax.ShapeDtypeStruct((d, d3), F32), jax.ShapeDtypeStruct((1, d3), F32),
                   jax.ShapeDtypeStruct((MOD_ROWS, d), F32)),
        compiler_params=_cparams(None, VMEM_LIMIT))(c16, dmod16, w_mod)


def _row_kind(t, lb):
    return jnp.where(t >= lb, 1, 0)


def _norm_mod_fwd(x3, g, modsel, ctx_len):
    b, t, d = x3.shape
    bt = ROW_BLOCK
    lb = ctx_len // bt

    def body(x_ref, g_ref, m_ref, h_ref):
        x = x_ref[0]
        rstd = lax.rsqrt(jnp.mean(x * x, axis=-1, keepdims=True) + EPS)
        y = x * rstd * g_ref[...]
        h_ref[0] = (y * (1.0 + m_ref[0, 0, 1:2, :]) + m_ref[0, 0, 0:1, :]).astype(BF16)

    return pl.pallas_call(
        body, name="norm_mod_fwd", grid=(b, t // bt),
        in_specs=[pl.BlockSpec((1, bt, d), lambda i, j: (i, j, 0)),
                  pl.BlockSpec((1, d), lambda i, j: (0, 0)),
                  pl.BlockSpec((1, 1, 8, d), lambda i, j: (i, _row_kind(j, lb), 0, 0))],
        out_specs=pl.BlockSpec((1, bt, d), lambda i, j: (i, j, 0)),
        out_shape=jax.ShapeDtypeStruct((b, t, d), BF16),
        compiler_params=_cparams(("parallel", "arbitrary")),
    )(x3, g, modsel)


def _norm_mod_bwd(dh3, x3, g, modsel, dres3, ctx_len):
    b, t, d = x3.shape
    bt = ROW_BLOCK
    lb = ctx_len // bt

    def body(dh_ref, x_ref, g_ref, m_ref, dres_ref, dx_ref, acc_ref):
        j = pl.program_id(1)
        x = x_ref[0]
        dh = dh_ref[0]
        g_row = g_ref[...]
        rstd = lax.rsqrt(jnp.mean(x * x, axis=-1, keepdims=True) + EPS)
        xhat = x * rstd
        dhpre = dh * (1.0 + m_ref[0, 0, 1:2, :])
        dxhat = dhpre * g_row
        dx = rstd * (dxhat - xhat * jnp.mean(dxhat * xhat, axis=-1, keepdims=True))
        dx_ref[0] = dx + dres_ref[0]

        @pl.when((j == 0) | (j == lb))
        def _():
            acc_ref[...] = jnp.zeros_like(acc_ref)

        acc_ref[0, 0, 0:1, :] += _colsum(dh)
        acc_ref[0, 0, 1:2, :] += _colsum(dh * (xhat * g_row))
        acc_ref[0, 0, 2:3, :] += _colsum(dhpre * xhat)

    blk = pl.BlockSpec((1, bt, d), lambda i, j: (i, j, 0))
    return pl.pallas_call(
        body, name="norm_mod_bwd", grid=(b, t // bt),
        in_specs=[blk, blk, pl.BlockSpec((1, d), lambda i, j: (0, 0)),
                  pl.BlockSpec((1, 1, 8, d), lambda i, j: (i, _row_kind(j, lb), 0, 0)), blk],
        out_specs=(blk, pl.BlockSpec((1, 1, 8, d), lambda i, j: (i, _row_kind(j, lb), 0, 0))),
        out_shape=(jax.ShapeDtypeStruct((b, t, d), F32), jax.ShapeDtypeStruct((b, 2, 8, d), F32)),
        compiler_params=_cparams(("parallel", "arbitrary")),
    )(dh3, x3, g, modsel, dres3)


def _shifted_rows(ref, c, off, ctx_len, total):
    ct = LRU_CHUNK
    r0 = pl.multiple_of(c * ct, ct)
    x0 = ref[pl.ds(r0, ct), :]
    row = lax.broadcasted_iota(jnp.int32, x0.shape, 0)
    if off < 0:
        k = -off
        has = jnp.logical_and(r0 != 0, r0 != ctx_len)
        p0 = pl.multiple_of(jnp.maximum(r0 - 8, 0), 8)
        edge = jnp.where(has, ref[pl.ds(p0, 8), :], 0.0)
        out = pltpu.roll(x0, k, 0)
        for j in range(k):
            out = jnp.where(row == j, edge[8 - k + j:8 - k + j + 1, :], out)
    else:
        k = off
        has = jnp.logical_and(r0 + ct != ctx_len, r0 + ct != total)
        n0 = pl.multiple_of(jnp.minimum(r0 + ct, total - 8), 8)
        edge = jnp.where(has, ref[pl.ds(n0, 8), :], 0.0)
        out = pltpu.roll(x0, ct - k, 0)
        for j in range(k):
            out = jnp.where(row == ct - k + j, edge[j:j + 1, :], out)
    return out


def _chunk_scan(a, b, reverse):
    n = a.shape[0]
    row = lax.broadcasted_iota(jnp.int32, a.shape, 0)
    s = 1
    while s < n:
        if reverse:
            a_s, b_s, ok = pltpu.roll(a, n - s, 0), pltpu.roll(b, n - s, 0), row < n - s
        else:
            a_s, b_s, ok = pltpu.roll(a, s, 0), pltpu.roll(b, s, 0), row >= s
        b = jnp.where(ok, a * b_s + b, b)
        a = jnp.where(ok, a * a_s, a)
        s *= 2
    return a, b


def _loop_chunks(n, body, init):
    assert n % LRU_UNROLL == 0

    def group(s2, carry):
        for u in range(LRU_UNROLL):
            carry = body(LRU_UNROLL * s2 + u, carry)
        return carry

    return lax.fori_loop(0, n // LRU_UNROLL, group, init)


def _lru_order(d, s, n_ctx, n_all):
    if d == 0:
        return s
    return jnp.where(s < n_ctx, n_ctx - 1 - s, n_all - 1 - (s - n_ctx))


def _lru_gates(u, wa, ba, wx, bx, sp):
    ub = u.astype(BF16)
    r = _sigmoid(_dot(ub, wa) + ba)
    i = _sigmoid(_dot(ub, wx) + bx)
    log_a = (-LRU_C * sp) * r
    a = jnp.exp(log_a)
    sf = jnp.sqrt(-_expm1(2.0 * log_a))
    return ub, r, i, a, sf


def _lru_specs(t, n_lane_blocks_offset):
    ln = LRU_LANES
    return [
        pl.BlockSpec((4, ln), lambda i, j: (0, j)),
        pl.BlockSpec((1, ln), lambda i, j: (0, j)),
        pl.BlockSpec((2, 1, ln, ln), lambda i, j: (0, j, 0, 0)),
        pl.BlockSpec((2, ln), lambda i, j: (0, j)),
        pl.BlockSpec((2, 1, ln, ln), lambda i, j: (0, j, 0, 0)),
        pl.BlockSpec((2, ln), lambda i, j: (0, j)),
        pl.BlockSpec((2, ln), lambda i, j: (0, j)),
    ]


def _lru_conv(ua_ref, cw_ref, cb_ref, u_s, ctx_len, total):
    ct = LRU_CHUNK

    def conv(c, _):
        r0 = pl.multiple_of(c * ct, ct)
        u = (cw_ref[0:1, :] * _shifted_rows(ua_ref, c, -2, ctx_len, total)
             + cw_ref[1:2, :] * _shifted_rows(ua_ref, c, -1, ctx_len, total)
             + cw_ref[2:3, :] * ua_ref[pl.ds(r0, ct), :]
             + cw_ref[3:4, :] * _shifted_rows(ua_ref, c, 1, ctx_len, total) + cb_ref[...])
        u_s[pl.ds(r0, ct), :] = u
        return 0

    lax.fori_loop(0, total // ct, conv, 0)


def _lru_fwd(proj3, col0, conv_w, conv_b, wa_bd, ba, wx_bd, bx, lam, ctx_len):
    b, t, _ = proj3.shape
    d = conv_w.shape[1]
    ln, ct = LRU_LANES, LRU_CHUNK
    n_all, n_ctx = t // ct, ctx_len // ct
    cb0 = col0 // ln

    def body(ua_ref, cw_ref, cb_ref, wa_ref, ba_ref, wx_ref, bx_ref, lam_ref, y_ref, u_s):
        ua = ua_ref.at[0]
        _lru_conv(ua, cw_ref, cb_ref, u_s, ctx_len, t)
        for dr in (0, 1):
            sp = _softplus(-lam_ref[dr:dr + 1, :])
            wa, wx = wa_ref[dr, 0], wx_ref[dr, 0]
            ba_row, bx_row = ba_ref[dr:dr + 1, :], bx_ref[dr:dr + 1, :]

            def step(s, carry, dr=dr, sp=sp, wa=wa, wx=wx, ba_row=ba_row, bx_row=bx_row):
                c = _lru_order(dr, s, n_ctx, n_all)
                r0 = pl.multiple_of(c * ct, ct)
                u = u_s[pl.ds(r0, ct), :]
                _, _, i, a, sf = _lru_gates(u, wa, ba_row, wx, bx_row, sp)
                aa, h0 = _chunk_scan(a, sf * (i * u), reverse=(dr == 1))
                h = h0 + aa * carry
                if dr == 0:
                    y_ref[0, pl.ds(r0, ct), :] = h
                    return h[ct - 1:ct, :]
                y_ref[0, pl.ds(r0, ct), :] += h
                return h[0:1, :]

            _loop_chunks(n_all, step, jnp.zeros((1, ln), F32))

    return pl.pallas_call(
        body, name="lru_fwd", grid=(b, d // ln),
        in_specs=[pl.BlockSpec((1, t, ln), lambda i, j: (i, 0, cb0 + j))] + _lru_specs(t, cb0),
        out_specs=pl.BlockSpec((1, t, ln), lambda i, j: (i, 0, j)),
        out_shape=jax.ShapeDtypeStruct((b, t, d), F32),
        scratch_shapes=[pltpu.VMEM((t, ln), F32)],
        compiler_params=_cparams(("parallel", "parallel"), VMEM_LIMIT),
    )(proj3, conv_w, conv_b, wa_bd, ba, wx_bd, bx, lam)


def _lru_bwd(dproj3, proj3, col0, dy3, conv_w, conv_b, wa_bd, ba, wx_bd, bx, lam, ctx_len):
    b, t, _ = proj3.shape
    d = conv_w.shape[1]
    ln, ct = LRU_LANES, LRU_CHUNK
    n_all, n_ctx = t // ct, ctx_len // ct
    cb0 = col0 // ln

    def body(dproj_hbm, ua_ref, dy_ref, cw_ref, cb_ref, wa_ref, ba_ref, wx_ref, bx_ref, lam_ref,
             dua_ref, vec_ref, dwa_ref, dwx_ref, u_s, h_s, du_s, a_s, sf_s, i_s, r_s):
        del dproj_hbm
        ua = ua_ref.at[0]
        _lru_conv(ua, cw_ref, cb_ref, u_s, ctx_len, t)
        du_s[...] = jnp.zeros_like(du_s)
        vec_ref[...] = jnp.zeros_like(vec_ref)
        for dr in (0, 1):
            sp = _softplus(-lam_ref[dr:dr + 1, :])
            wa, wx = wa_ref[dr, 0], wx_ref[dr, 0]
            ba_row, bx_row = ba_ref[dr:dr + 1, :], bx_ref[dr:dr + 1, :]

            def fwd(s, carry, dr=dr, sp=sp, wa=wa, wx=wx, ba_row=ba_row, bx_row=bx_row):
                c = _lru_order(dr, s, n_ctx, n_all)
                r0 = pl.multiple_of(c * ct, ct)
                u = u_s[pl.ds(r0, ct), :]
                _, r, i, a, sf = _lru_gates(u, wa, ba_row, wx, bx_row, sp)
                aa, h0 = _chunk_scan(a, sf * (i * u), reverse=(dr == 1))
                h = h0 + aa * carry
                h_s[pl.ds(r0, ct), :] = h
                a_s[pl.ds(r0, ct), :] = a
                sf_s[pl.ds(r0, ct), :] = sf
                i_s[pl.ds(r0, ct), :] = i
                r_s[pl.ds(r0, ct), :] = r
                return h[ct - 1:ct, :] if dr == 0 else h[0:1, :]

            _loop_chunks(n_all, fwd, jnp.zeros((1, ln), F32))
            dwa_ref[0, dr, 0] = jnp.zeros((ln, ln), F32)
            dwx_ref[0, dr, 0] = jnp.zeros((ln, ln), F32)

            def bwd(sr, carry, dr=dr, sp=sp, wa=wa, wx=wx):
                gc, vacc = carry
                c = _lru_order(dr, n_all - 1 - sr, n_ctx, n_all)
                r0 = pl.multiple_of(c * ct, ct)
                u = u_s[pl.ds(r0, ct), :]
                h = h_s[pl.ds(r0, ct), :]
                dy = dy_ref[0, pl.ds(r0, ct), :]
                ub = u.astype(BF16)
                r, i, a, sf = r_s[pl.ds(r0, ct), :], i_s[pl.ds(r0, ct), :], a_s[pl.ds(r0, ct), :], sf_s[pl.ds(r0, ct), :]
                row = lax.broadcasted_iota(jnp.int32, a.shape, 0)
                if dr == 0:
                    alpha = jnp.where(row == ct - 1, 1.0, pltpu.roll(a, ct - 1, 0))
                    aa, g0 = _chunk_scan(alpha, dy, reverse=True)
                    g = g0 + aa * gc
                    gc_new = a[0:1, :] * g[0:1, :]
                    p0 = pl.multiple_of(jnp.maximum(r0 - 8, 0), 8)
                    edge = jnp.where(r0 != 0, h_s[pl.ds(p0, 8), :], 0.0)[7:8, :]
                    h_prev = jnp.where(row == 0, edge, pltpu.roll(h, 1, 0))
                else:
                    alpha = jnp.where(row == 0, 1.0, pltpu.roll(a, 1, 0))
                    aa, g0 = _chunk_scan(alpha, dy, reverse=False)
                    g = g0 + aa * gc
                    gc_new = a[ct - 1:ct, :] * g[ct - 1:ct, :]
                    r_end = r0 + ct
                    n0 = pl.multiple_of(jnp.where(r_end == t, 0, jnp.minimum(r_end, t - 8)), 8)
                    edge = jnp.where(r_end != ctx_len, h_s[pl.ds(n0, 8), :], 0.0)[0:1, :]
                    h_prev = jnp.where(row == ct - 1, edge, pltpu.roll(h, ct - 1, 0))
                da = g * h_prev
                iu = i * u
                diu = g * sf
                dlog_a = da * a - (g * iu) * (a * a) / sf
                dpre_r = (dlog_a * (-LRU_C * sp)) * (r * (1.0 - r))
                dpre_i = (diu * u) * (i * (1.0 - i))
                dpr_b, dpi_b = dpre_r.astype(BF16), dpre_i.astype(BF16)
                du = diu * i + _dot_nt(dpr_b, wa) + _dot_nt(dpi_b, wx)
                du_s[pl.ds(r0, ct), :] += du
                dwa_ref[0, dr, 0] += _dot_tn(ub, dpr_b)
                dwx_ref[0, dr, 0] += _dot_tn(ub, dpi_b)
                vacc = (vacc[0] + _colsum(dpre_r), vacc[1] + _colsum(dpre_i), vacc[2] + _colsum(dlog_a * (-LRU_C * r)))
                return gc_new, vacc

            zrow = jnp.zeros((1, ln), F32)
            _, vacc = _loop_chunks(n_all, bwd, (zrow, (zrow, zrow, zrow)))
            vec_ref[0, 5 + dr:6 + dr, :] = vacc[0]
            vec_ref[0, 7 + dr:8 + dr, :] = vacc[1]
            vec_ref[0, 9 + dr:10 + dr, :] = vacc[2]

        def conv_bwd(c, acc):
            r0 = pl.multiple_of(c * ct, ct)
            du = du_s[pl.ds(r0, ct), :]
            dua = (cw_ref[0:1, :] * _shifted_rows(du_s, c, 2, ctx_len, t)
                   + cw_ref[1:2, :] * _shifted_rows(du_s, c, 1, ctx_len, t)
                   + cw_ref[2:3, :] * du
                   + cw_ref[3:4, :] * _shifted_rows(du_s, c, -1, ctx_len, t))
            dua_ref[0, pl.ds(r0, ct), :] = dua.astype(BF16)
            return (acc[0] + _colsum(du * _shifted_rows(ua, c, -2, ctx_len, t)),
                    acc[1] + _colsum(du * _shifted_rows(ua, c, -1, ctx_len, t)),
                    acc[2] + _colsum(du * ua[pl.ds(r0, ct), :]),
                    acc[3] + _colsum(du * _shifted_rows(ua, c, 1, ctx_len, t)),
                    acc[4] + _colsum(du))

        zrow = jnp.zeros((1, ln), F32)
        acc = lax.fori_loop(0, n_all, conv_bwd, (zrow,) * 5)
        for k in range(5):
            vec_ref[0, k:k + 1, :] = acc[k]

    ng = d // ln
    return pl.pallas_call(
        body, name="lru_bwd", grid=(b, ng),
        in_specs=[pl.BlockSpec(memory_space=pl.ANY),
                  pl.BlockSpec((1, t, ln), lambda i, j: (i, 0, cb0 + j)),
                  pl.BlockSpec((1, t, ln), lambda i, j: (i, 0, j))] + _lru_specs(t, cb0),
        out_specs=(pl.BlockSpec((1, t, ln), lambda i, j: (i, 0, cb0 + j)),
                   pl.BlockSpec((1, 16, ln), lambda i, j: (i, 0, j)),
                   pl.BlockSpec((1, 2, 1, ln, ln), lambda i, j: (i, 0, j, 0, 0)),
                   pl.BlockSpec((1, 2, 1, ln, ln), lambda i, j: (i, 0, j, 0, 0))),
        out_shape=(jax.ShapeDtypeStruct(dproj3.shape, dproj3.dtype),
                   jax.ShapeDtypeStruct((b, 16, d), F32),
                   jax.ShapeDtypeStruct((b, 2, ng, ln, ln), F32),
                   jax.ShapeDtypeStruct((b, 2, ng, ln, ln), F32)),
        scratch_shapes=[pltpu.VMEM((t, ln), F32)] * 7,
        input_output_aliases={0: 0},
        compiler_params=_cparams(("parallel", "parallel"), VMEM_LIMIT),
    )(dproj3, proj3, dy3, conv_w, conv_b, wa_bd, ba, wx_bd, bx, lam)


def _rope_tables(ctx_len, seq):
    p = HEAD_DIM // 4
    inv = ROPE_THETA ** (-jnp.arange(p, dtype=F32) / p)
    tok = jnp.arange(seq)
    ang_r = (tok // GRID_W)[:, None] * inv
    ang_c = (tok % GRID_W)[:, None] * inv
    cos = jnp.concatenate([jnp.cos(ang_r)] * 2 + [jnp.cos(ang_c)] * 2, axis=1)
    sin = jnp.concatenate([-jnp.sin(ang_r), jnp.sin(ang_r), -jnp.sin(ang_c), jnp.sin(ang_c)], axis=1)
    cos = jnp.concatenate([jnp.ones((ctx_len, HEAD_DIM), F32), cos], axis=0)
    sin = jnp.concatenate([jnp.zeros((ctx_len, HEAD_DIM), F32), sin], axis=0)
    return cos, sin


def _swap_halves(v):
    lane = lax.broadcasted_iota(jnp.int32, v.shape, 1)
    return jnp.where((lane & 63) < 32, pltpu.roll(v, 96, 1), pltpu.roll(v, 32, 1))


def _head_rstd(v):
    return lax.rsqrt(jnp.mean(v * v, axis=-1, keepdims=True) + EPS)


QKV_BLOCK = GROUP * HEAD_DIM
PREP_ROWS = (2176, 256)


def _prep_fwd(proj3, qcol, kvcol, d, cos, sin, gq, gk, use_norm):
    b, t, _ = proj3.shape
    bt, wb = _pick(t, PREP_ROWS), QKV_BLOCK
    nqb = d // wb
    assert qcol % wb == 0 and kvcol % wb == 0 and d // HEAD_DIM // GROUP == 2
    qb0, kvb = qcol // wb, kvcol // wb

    def body(p_ref, cos_ref, sin_ref, gq_ref, gk_ref, o_ref):
        s = pl.program_id(2)
        c, sn = cos_ref[...], sin_ref[...]

        def rope(v):
            return v * c + _swap_halves(v) * sn

        @pl.when(s < nqb)
        def _():
            for hh in range(GROUP):
                v = p_ref[0, :, hh * HEAD_DIM:(hh + 1) * HEAD_DIM]
                if use_norm:
                    v = v * _head_rstd(v) * gq_ref[...]
                o_ref[0, :, hh * HEAD_DIM:(hh + 1) * HEAD_DIM] = rope(v).astype(BF16)

        @pl.when(s == nqb)
        def _():
            for hh in range(2):
                v = p_ref[0, :, hh * HEAD_DIM:(hh + 1) * HEAD_DIM]
                if use_norm:
                    v = v * _head_rstd(v) * gk_ref[...]
                o_ref[0, :, hh * HEAD_DIM:(hh + 1) * HEAD_DIM] = rope(v).astype(BF16)
            o_ref[0, :, 2 * HEAD_DIM:] = p_ref[0, :, 2 * HEAD_DIM:].astype(BF16)

    return pl.pallas_call(
        body, name="prep_fwd_norm" if use_norm else "prep_fwd", grid=(b, t // bt, nqb + 1),
        in_specs=[pl.BlockSpec((1, bt, wb), lambda i, j, s: (i, j, jnp.where(s < nqb, qb0 + s, kvb))),
                  pl.BlockSpec((bt, HEAD_DIM), lambda i, j, s: (j, 0)),
                  pl.BlockSpec((bt, HEAD_DIM), lambda i, j, s: (j, 0)),
                  pl.BlockSpec((1, HEAD_DIM), lambda i, j, s: (0, 0)),
                  pl.BlockSpec((1, HEAD_DIM), lambda i, j, s: (0, 0))],
        out_specs=pl.BlockSpec((1, bt, wb), lambda i, j, s: (i, j, s)),
        out_shape=jax.ShapeDtypeStruct((b, t, d + wb), BF16),
        compiler_params=_cparams(("parallel", "parallel", "arbitrary"), VMEM_LIMIT),
    )(proj3, cos, sin, gq, gk)


def _prep_bwd(dproj3, dq3, dkt, dvt, proj3, qcol, kvcol, d, cos, sin, gq, gk, use_norm):
    b, t, _ = proj3.shape
    bt, wb = _pick(t, PREP_ROWS), QKV_BLOCK
    nqb = d // wb
    qb0, kvb = qcol // wb, kvcol // wb
    kvh = dkt.shape[1]

    def body(dproj_hbm, dq_ref, dkt_ref, dvt_ref, p_ref, cos_ref, sin_ref, gq_ref, gk_ref, o_ref, gacc_ref):
        del dproj_hbm
        j, s = pl.program_id(1), pl.program_id(2)
        c, sn = cos_ref[...], sin_ref[...]

        @pl.when((j == 0) & (s == 0))
        def _():
            gacc_ref[...] = jnp.zeros_like(gacc_ref)

        def unrope(dv):
            return dv * c + _swap_halves(dv * sn)

        def head_bwd(dyv, xv, g_ref, acc_row):
            dyv = unrope(dyv)
            if not use_norm:
                return dyv
            rstd = _head_rstd(xv)
            xhat = xv * rstd
            gacc_ref[0, acc_row:acc_row + 1, :] += _colsum(dyv * xhat)
            dxhat = dyv * g_ref[...]
            return rstd * (dxhat - xhat * jnp.mean(dxhat * xhat, axis=-1, keepdims=True))

        @pl.when(s < nqb)
        def _():
            for hh in range(GROUP):
                sl = slice(hh * HEAD_DIM, (hh + 1) * HEAD_DIM)
                o_ref[0, :, sl] = head_bwd(dq_ref[0, :, sl], p_ref[0, :, sl], gq_ref, 0).astype(BF16)

        @pl.when(s == nqb)
        def _():
            for hh in range(kvh):
                sl = slice(hh * HEAD_DIM, (hh + 1) * HEAD_DIM)
                o_ref[0, :, sl] = head_bwd(dkt_ref[0, hh].T, p_ref[0, :, sl], gk_ref, 1).astype(BF16)
                sv = slice((kvh + hh) * HEAD_DIM, (kvh + hh + 1) * HEAD_DIM)
                o_ref[0, :, sv] = dvt_ref[0, hh].T.astype(BF16)

    col = lambda i, j, s: (i, j, jnp.where(s < nqb, qb0 + s, kvb))
    return pl.pallas_call(
        body, name="prep_bwd_norm" if use_norm else "prep_bwd", grid=(b, t // bt, nqb + 1),
        in_specs=[pl.BlockSpec(memory_space=pl.ANY),
                  pl.BlockSpec((1, bt, wb), lambda i, j, s: (i, j, jnp.minimum(s, nqb - 1))),
                  pl.BlockSpec((1, kvh, HEAD_DIM, bt), lambda i, j, s: (i, 0, 0, j)),
                  pl.BlockSpec((1, kvh, HEAD_DIM, bt), lambda i, j, s: (i, 0, 0, j)),
                  pl.BlockSpec((1, bt, wb), col),
                  pl.BlockSpec((bt, HEAD_DIM), lambda i, j, s: (j, 0)),
                  pl.BlockSpec((bt, HEAD_DIM), lambda i, j, s: (j, 0)),
                  pl.BlockSpec((1, HEAD_DIM), lambda i, j, s: (0, 0)),
                  pl.BlockSpec((1, HEAD_DIM), lambda i, j, s: (0, 0))],
        out_specs=(pl.BlockSpec((1, bt, wb), col), pl.BlockSpec((1, 8, HEAD_DIM), lambda i, j, s: (i, 0, 0))),
        out_shape=(jax.ShapeDtypeStruct(dproj3.shape, dproj3.dtype), jax.ShapeDtypeStruct((b, 8, HEAD_DIM), F32)),
        input_output_aliases={0: 0},
        compiler_params=_cparams(("parallel", "arbitrary", "arbitrary"), VMEM_LIMIT),
    )(dproj3, dq3, dkt, dvt, proj3, cos, sin, gq, gk)


def _stack_heads(ref, dtype=None):
    parts = [ref[0, :, g * HEAD_DIM:(g + 1) * HEAD_DIM] for g in range(GROUP)]
    v = jnp.concatenate(parts, axis=0)
    return v if dtype is None else v.astype(dtype)


def _unstack_heads(ref, v, bq):
    for g in range(GROUP):
        ref[0, :, g * HEAD_DIM:(g + 1) * HEAD_DIM] = v[g * bq:(g + 1) * bq, :]


def _attn_specs(t, d, bq):
    kvh = d // HEAD_DIM // GROUP
    kc0 = d // HEAD_DIM
    q_spec = pl.BlockSpec((1, bq, QKV_BLOCK), lambda i, h, j: (i, j, h))
    k_spec = pl.BlockSpec((1, t, HEAD_DIM), lambda i, h, j: (i, 0, kc0 + h))
    v_spec = pl.BlockSpec((1, t, HEAD_DIM), lambda i, h, j: (i, 0, kc0 + kvh + h))
    lse_spec = pl.BlockSpec((1, GROUP, bq, HEAD_DIM), lambda i, h, j: (i, h, j, 0))
    kt_spec = pl.BlockSpec((1, 1, HEAD_DIM, t), lambda i, h, j: (i, h, 0, 0))
    return kvh, q_spec, k_spec, v_spec, lse_spec, kt_spec


SCALE = HEAD_DIM ** -0.5


def _attn_dense_fwd(qkv, d, ctx_len):
    b, t, _ = qkv.shape
    bq = DENSE_FWD_BQ
    lq = ctx_len // bq
    kvh, q_spec, k_spec, v_spec, lse_spec, _ = _attn_specs(t, d, bq)

    def body(q_ref, k_ref, v_ref, o_ref, lse_ref):
        i = pl.program_id(2)

        def attend(k, v):
            for g in range(GROUP):
                sl = slice(g * HEAD_DIM, (g + 1) * HEAD_DIM)
                s = _dot_nt(q_ref[0, :, sl], k)
                m = jnp.max(s, axis=1, keepdims=True)
                p = jnp.exp2((s - m) * (SCALE * LOG2E))
                l = jnp.sum(p, axis=1, keepdims=True)
                o_ref[0, :, sl] = _dot(p.astype(BF16), v) / l
                lse_ref[0, g] = jnp.broadcast_to(m * SCALE + jnp.log(l), (bq, HEAD_DIM))

        @pl.when(i < lq)
        def _():
            attend(k_ref[0, 0:ctx_len, :], v_ref[0, 0:ctx_len, :])

        @pl.when(i >= lq)
        def _():
            attend(k_ref[0], v_ref[0])

    return pl.pallas_call(
        body, name="attn_dense_fwd", grid=(b, kvh, t // bq),
        in_specs=[q_spec, k_spec, v_spec], out_specs=(q_spec, lse_spec),
        out_shape=(jax.ShapeDtypeStruct((b, t, d), F32), jax.ShapeDtypeStruct((b, kvh * GROUP, t, HEAD_DIM), F32)),
        compiler_params=_cparams(("parallel", "parallel", "arbitrary"), VMEM_LIMIT),
    )(qkv, qkv, qkv)


def _attn_dense_bwd(qkv, o3, do3, lse, d, ctx_len):
    b, t, _ = qkv.shape
    bq = ATT_BQ
    lq = ctx_len // bq
    kvh, q_spec, k_spec, v_spec, lse_spec, kt_spec = _attn_specs(t, d, bq)

    def body(q_ref, k_ref, v_ref, o_ref, do_ref, lse_ref, dq_ref, dkt_ref, dvt_ref):
        i = pl.program_id(2)

        @pl.when(i == 0)
        def _():
            dkt_ref[...] = jnp.zeros_like(dkt_ref)
            dvt_ref[...] = jnp.zeros_like(dvt_ref)

        def run(k, v, width):
            dk_acc = dv_acc = None
            for g in range(GROUP):
                sl = slice(g * HEAD_DIM, (g + 1) * HEAD_DIM)
                q = q_ref[0, :, sl]
                do = do_ref[0, :, sl]
                dd = jnp.sum(do * o_ref[0, :, sl], axis=1, keepdims=True)
                dob = do.astype(BF16)
                p = jnp.exp2(_dot_nt(q, k) * (SCALE * LOG2E) - lse_ref[0, g][:, 0:1] * LOG2E)
                ds = (p * (_dot_nt(dob, v) - dd) * SCALE).astype(BF16)
                dq_ref[0, :, sl] = _dot(ds, k)
                dk_g = _dot(q.astype(F32).T.astype(BF16), ds)
                dv_g = _dot(do.T.astype(BF16), p.astype(BF16))
                dk_acc = dk_g if dk_acc is None else dk_acc + dk_g
                dv_acc = dv_g if dv_acc is None else dv_acc + dv_g
            dkt_ref[0, 0, :, 0:width] += dk_acc
            dvt_ref[0, 0, :, 0:width] += dv_acc

        @pl.when(i < lq)
        def _():
            run(k_ref[0, 0:ctx_len, :], v_ref[0, 0:ctx_len, :], ctx_len)

        @pl.when(i >= lq)
        def _():
            run(k_ref[0], v_ref[0], t)

    return pl.pallas_call(
        body, name="attn_dense_bwd", grid=(b, kvh, t // bq),
        in_specs=[q_spec, k_spec, v_spec, q_spec, q_spec, lse_spec], out_specs=(q_spec, kt_spec, kt_spec),
        out_shape=(jax.ShapeDtypeStruct((b, t, d), F32), jax.ShapeDtypeStruct((b, kvh, HEAD_DIM, t), F32),
                   jax.ShapeDtypeStruct((b, kvh, HEAD_DIM, t), F32)),
        compiler_params=_cparams(("parallel", "parallel", "arbitrary"), VMEM_LIMIT),
    )(qkv, qkv, qkv, o3, do3, lse)


def _sink_column(sink_ref, h, bq):
    rowi = lax.broadcasted_iota(jnp.int32, (GROUP * bq, 1), 0)
    col = jnp.zeros((GROUP * bq, 1), F32)
    for g in range(GROUP):
        col = jnp.where((rowi >= g * bq) & (rowi < (g + 1) * bq), sink_ref[h * GROUP + g], col)
    return col


def _band(i, lq, ctx_len, t, bq):
    n = i - lq
    start = pl.multiple_of(jnp.clip(ctx_len + n * bq - WINDOW, ctx_len, t - WIN_SPAN), WINDOW)
    shape = (GROUP * bq, WIN_SPAN)
    kpos = start - ctx_len + lax.broadcasted_iota(jnp.int32, shape, 1)
    qpos = n * bq + (lax.broadcasted_iota(jnp.int32, shape, 0) & (bq - 1))
    return start, jnp.abs(kpos - qpos) <= WINDOW


def _attn_win_fwd(qkv, sink, d, ctx_len):
    b, t, _ = qkv.shape
    bq = WIN_BQ
    rows = GROUP * bq
    lq = ctx_len // bq
    kvh, q_spec, k_spec, v_spec, lse_spec, _ = _attn_specs(t, d, bq)

    def body(sink_ref, q_ref, k_ref, v_ref, o_ref, lse_ref):
        h, i = pl.program_id(1), pl.program_id(2)
        q4 = _stack_heads(q_ref)
        sink_col = _sink_column(sink_ref, h, bq)
        sc = _dot_nt(q4, k_ref[0, 0:ctx_len, :]) * SCALE
        mc = jnp.maximum(jnp.max(sc, axis=1, keepdims=True), sink_col)

        def finish(m, l, acc):
            _unstack_heads(o_ref, acc / l, bq)
            lse_ref[0] = jnp.broadcast_to(m + jnp.log(l), (rows, HEAD_DIM)).reshape(GROUP, bq, HEAD_DIM)

        @pl.when(i < lq)
        def _():
            pc = jnp.exp(sc - mc)
            l = jnp.sum(pc, axis=1, keepdims=True) + jnp.exp(sink_col - mc)
            finish(mc, l, _dot(pc.astype(BF16), v_ref[0, 0:ctx_len, :]))

        @pl.when(i >= lq)
        def _():
            start, ok = _band(i, lq, ctx_len, t, bq)
            sb = jnp.where(ok, _dot_nt(q4, k_ref[0, pl.ds(start, WIN_SPAN), :]) * SCALE, NEG_INF)
            m = jnp.maximum(mc, jnp.max(sb, axis=1, keepdims=True))
            pc, pb = jnp.exp(sc - m), jnp.exp(sb - m)
            l = jnp.sum(pc, axis=1, keepdims=True) + jnp.sum(pb, axis=1, keepdims=True) + jnp.exp(sink_col - m)
            acc = _dot(pc.astype(BF16), v_ref[0, 0:ctx_len, :]) + _dot(pb.astype(BF16), v_ref[0, pl.ds(start, WIN_SPAN), :])
            finish(m, l, acc)

    return pl.pallas_call(
        body, name="attn_win_fwd", grid=(b, kvh, t // bq),
        in_specs=[pl.BlockSpec(memory_space=pltpu.SMEM), q_spec, k_spec, v_spec], out_specs=(q_spec, lse_spec),
        out_shape=(jax.ShapeDtypeStruct((b, t, d), F32), jax.ShapeDtypeStruct((b, kvh * GROUP, t, HEAD_DIM), F32)),
        compiler_params=_cparams(("parallel", "parallel", "arbitrary"), VMEM_LIMIT),
    )(sink, qkv, qkv, qkv)


def _attn_win_bwd(qkv, sink, o3, do3, lse, d, ctx_len):
    b, t, _ = qkv.shape
    bq = WIN_BQ
    rows = GROUP * bq
    lq = ctx_len // bq
    kvh, q_spec, k_spec, v_spec, lse_spec, kt_spec = _attn_specs(t, d, bq)

    def body(sink_ref, q_ref, k_ref, v_ref, o_ref, do_ref, lse_ref, dq_ref, dkt_ref, dvt_ref, dsk_ref):
        h, i = pl.program_id(1), pl.program_id(2)

        @pl.when(i == 0)
        def _():
            dkt_ref[...] = jnp.zeros_like(dkt_ref)
            dvt_ref[...] = jnp.zeros_like(dvt_ref)
            dsk_ref[...] = jnp.zeros_like(dsk_ref)

        q4 = _stack_heads(q_ref)
        do4 = _stack_heads(do_ref)
        dd = jnp.sum(do4 * _stack_heads(o_ref), axis=1, keepdims=True)
        lse_col = lse_ref[0].reshape(rows, HEAD_DIM)[:, 0:1]
        do4b = do4.astype(BF16)
        qt = q4.astype(F32).T.astype(BF16)
        dot = do4.T.astype(BF16)

        def part(k, v):
            return _dot_nt(q4, k) * SCALE, _dot_nt(do4b, v)

        def grads(p, dp, k):
            ds = (p * (dp - dd) * SCALE).astype(BF16)
            return _dot(ds, k), _dot(qt, ds), _dot(dot, p.astype(BF16))

        kc = k_ref[0, 0:ctx_len, :]
        sc, dpc = part(kc, v_ref[0, 0:ctx_len, :])
        dq_c, dk_c, dv_c = grads(jnp.exp(sc - lse_col), dpc, kc)
        dkt_ref[0, 0, :, 0:ctx_len] += dk_c
        dvt_ref[0, 0, :, 0:ctx_len] += dv_c
        _unstack_heads(dq_ref, dq_c, bq)

        @pl.when(i >= lq)
        def _():
            start, ok = _band(i, lq, ctx_len, t, bq)
            kb = k_ref[0, pl.ds(start, WIN_SPAN), :]
            sb, dpb = part(kb, v_ref[0, pl.ds(start, WIN_SPAN), :])
            pb = jnp.where(ok, jnp.exp(sb - lse_col), 0.0)
            dq_b, dk_b, dv_b = grads(pb, dpb, kb)
            dkt_ref[0, 0, :, pl.ds(start, WIN_SPAN)] += dk_b
            dvt_ref[0, 0, :, pl.ds(start, WIN_SPAN)] += dv_b
            for g in range(GROUP):
                dq_ref[0, :, g * HEAD_DIM:(g + 1) * HEAD_DIM] += dq_b[g * bq:(g + 1) * bq, :]

        ps = jnp.exp(_sink_column(sink_ref, h, bq) - lse_col) * dd
        for g in range(GROUP):
            val = jnp.sum(ps[g * bq:(g + 1) * bq, :], axis=0, keepdims=True)
            dsk_ref[0, 0, g:g + 1, :] -= jnp.broadcast_to(val, (1, HEAD_DIM))

    return pl.pallas_call(
        body, name="attn_win_bwd", grid=(b, kvh, t // bq),
        in_specs=[pl.BlockSpec(memory_space=pltpu.SMEM), q_spec, k_spec, v_spec, q_spec, q_spec, lse_spec],
        out_specs=(q_spec, kt_spec, kt_spec, pl.BlockSpec((1, 1, 8, HEAD_DIM), lambda i, h, j: (i, h, 0, 0))),
        out_shape=(jax.ShapeDtypeStruct((b, t, d), F32), jax.ShapeDtypeStruct((b, kvh, HEAD_DIM, t), F32),
                   jax.ShapeDtypeStruct((b, kvh, HEAD_DIM, t), F32), jax.ShapeDtypeStruct((b, kvh, 8, HEAD_DIM), F32)),
        compiler_params=_cparams(("parallel", "parallel", "arbitrary"), VMEM_LIMIT),
    )(sink, qkv, qkv, qkv, o3, do3, lse)


MERGE_BWD_ROWS = 128


def _resident(shape):
    return pl.BlockSpec(shape, lambda *_: (0,) * len(shape), pipeline_mode=pl.Buffered(1))


def _merge_fwd(x3, ya, yb, yc, proj3, w_br, w_out, modsel, ctx_len):
    b, t, d = x3.shape
    bt = ROW_BLOCK
    lb = ctx_len // bt

    def body(x_ref, ya_ref, yb_ref, yc_ref, gm_ref, wbr_ref, wo_ref, m_ref, xn_ref, out_ref):
        mix = jnp.zeros((bt, d), F32)
        for n, y_ref in enumerate((ya_ref, yb_ref, yc_ref)):
            z = (y_ref[0] * _silu(gm_ref[0, :, n * d:(n + 1) * d])).astype(BF16)
            mix = mix + _sigmoid(gm_ref[0, :, (3 + n) * d:(4 + n) * d]) * _dot(z, wbr_ref[n])
        o = _dot(mix.astype(BF16), wo_ref[...])
        out_ref[0] = o
        xn_ref[0] = x_ref[0] + m_ref[0, 0, 2:3, :] * o

    blk = pl.BlockSpec((1, bt, d), lambda i, j: (i, j, 0))
    return pl.pallas_call(
        body, name="merge_fwd", grid=(b, t // bt),
        in_specs=[blk, blk, blk, blk, pl.BlockSpec((1, bt, 6 * d), lambda i, j: (i, j, 0)),
                  _resident((3, d, d)), _resident((d, d)),
                  pl.BlockSpec((1, 1, 8, d), lambda i, j: (i, _row_kind(j, lb), 0, 0))],
        out_specs=(blk, blk),
        out_shape=(jax.ShapeDtypeStruct((b, t, d), F32), jax.ShapeDtypeStruct((b, t, d), F32)),
        compiler_params=_cparams(("parallel", "arbitrary"), VMEM_LIMIT),
    )(x3, ya, yb, yc, proj3, w_br, w_out, modsel)


def _merge_bwd(dxn3, out3, ya, yb, yc, proj3, w_br, w_out, modsel, ctx_len):
    b, t, d = dxn3.shape
    n_cols = proj3.shape[2]
    bt = MERGE_BWD_ROWS
    lb = ctx_len // bt

    def body(dxn_ref, out_ref, ya_ref, yb_ref, yc_ref, gm_ref, wbr_ref, wo_ref, m_ref,
             dgm_ref, dya_ref, dyb_ref, dyc_ref, z_ref, dt_ref, mix_ref, dout_ref, gacc_ref):
        j = pl.program_id(1)
        dxn = dxn_ref[0]
        doutb = (m_ref[0, 0, 2:3, :] * dxn).astype(BF16)
        dout_ref[0] = doutb

        @pl.when((j == 0) | (j == lb))
        def _():
            gacc_ref[...] = jnp.zeros_like(gacc_ref)

        gacc_ref[0, 0, 0:1, :] += _colsum(dxn * out_ref[0])
        dmix = _dot_nt(doutb, wo_ref[...])
        mix = jnp.zeros((bt, d), F32)
        for n, (y_ref, dy_ref) in enumerate(((ya_ref, dya_ref), (yb_ref, dyb_ref), (yc_ref, dyc_ref))):
            g = gm_ref[0, :, n * d:(n + 1) * d]
            y = y_ref[0]
            sig_g = _sigmoid(g)
            silu_g = g * sig_g
            z = (y * silu_g).astype(BF16)
            z_ref[n, 0] = z
            tn = _dot(z, wbr_ref[n])
            s = _sigmoid(gm_ref[0, :, (3 + n) * d:(4 + n) * d])
            mix = mix + s * tn
            dgm_ref[0, :, (3 + n) * d:(4 + n) * d] = (dmix * tn * (s * (1.0 - s))).astype(BF16)
            dtb = (dmix * s).astype(BF16)
            dt_ref[n, 0] = dtb
            dz = _dot_nt(dtb, wbr_ref[n])
            dy_ref[0] = dz * silu_g
            dgm_ref[0, :, n * d:(n + 1) * d] = (dz * y * (sig_g * (1.0 + g * (1.0 - sig_g)))).astype(BF16)
        mix_ref[0] = mix.astype(BF16)

    blk = pl.BlockSpec((1, bt, d), lambda i, j: (i, j, 0))
    blk4 = pl.BlockSpec((3, 1, bt, d), lambda i, j: (0, i, j, 0))
    wide = pl.BlockSpec((1, bt, 6 * d), lambda i, j: (i, j, 0))
    return pl.pallas_call(
        body, name="merge_bwd", grid=(b, t // bt),
        in_specs=[blk, blk, blk, blk, blk, wide, _resident((3, d, d)), _resident((d, d)),
                  pl.BlockSpec((1, 1, 8, d), lambda i, j: (i, _row_kind(j, lb), 0, 0))],
        out_specs=(wide, blk, blk, blk, blk4, blk4, blk, blk,
                   pl.BlockSpec((1, 1, 8, d), lambda i, j: (i, _row_kind(j, lb), 0, 0))),
        out_shape=(jax.ShapeDtypeStruct((b, t, n_cols), BF16),
                   jax.ShapeDtypeStruct((b, t, d), F32), jax.ShapeDtypeStruct((b, t, d), F32),
                   jax.ShapeDtypeStruct((b, t, d), F32),
                   jax.ShapeDtypeStruct((3, b, t, d), BF16), jax.ShapeDtypeStruct((3, b, t, d), BF16),
                   jax.ShapeDtypeStruct((b, t, d), BF16), jax.ShapeDtypeStruct((b, t, d), BF16),
                   jax.ShapeDtypeStruct((b, 2, 8, d), F32)),
        compiler_params=_cparams(("parallel", "arbitrary"), VMEM_LIMIT),
    )(dxn3, out3, ya, yb, yc, proj3, w_br, w_out, modsel)


def _final(x3, g, target, ctx_len):
    b, t, d = x3.shape
    bt = ROW_BLOCK
    lb = ctx_len // bt

    def body(x_ref, g_ref, t_ref, dx_ref, loss_ref, dg_ref):
        j = pl.program_id(1)

        @pl.when(j == 0)
        def _():
            loss_ref[...] = jnp.zeros_like(loss_ref)
            dg_ref[...] = jnp.zeros_like(dg_ref)

        @pl.when(j < lb)
        def _():
            dx_ref[...] = jnp.zeros_like(dx_ref)

        @pl.when(j >= lb)
        def _():
            x = x_ref[0]
            g_row = g_ref[...]
            rstd = lax.rsqrt(jnp.mean(x * x, axis=-1, keepdims=True) + EPS)
            xhat = x * rstd
            err = xhat * g_row - t_ref[0]
            loss_ref[...] += (0.5 / d) * jnp.sum(err * err)
            dy = err * (1.0 / d)
            dg_ref[0, 0:1, :] += _colsum(dy * xhat)
            dxhat = dy * g_row
            dx_ref[0] = rstd * (dxhat - xhat * jnp.mean(dxhat * xhat, axis=-1, keepdims=True))

    blk = pl.BlockSpec((1, bt, d), lambda i, j: (i, j, 0))
    return pl.pallas_call(
        body, name="final_loss", grid=(b, t // bt),
        in_specs=[blk, pl.BlockSpec((1, d), lambda i, j: (0, 0)),
                  pl.BlockSpec((1, bt, d), lambda i, j: (i, jnp.maximum(j - lb, 0), 0))],
        out_specs=(blk, pl.BlockSpec((1, 8, HEAD_DIM), lambda i, j: (i, 0, 0)), pl.BlockSpec((1, 8, d), lambda i, j: (i, 0, 0))),
        out_shape=(jax.ShapeDtypeStruct((b, t, d), F32), jax.ShapeDtypeStruct((b, 8, HEAD_DIM), F32),
                   jax.ShapeDtypeStruct((b, 8, d), F32)),
        compiler_params=_cparams(("parallel", "arbitrary")),
    )(x3, g, target)


TOKEN_BLOCKS = (1088, 512, 256, 128)


def _pick(n, options):
    for o in options:
        if n % o == 0:
            return o
    raise ValueError((n, options))


def _block_diag(w):
    per = LRU_LANES // LRU_BLOCK_W
    nd, nb, bw, _ = w.shape
    wr = w.reshape(nd, nb // per, per, bw, bw)
    eye = jnp.eye(per, dtype=w.dtype)
    bd = wr[:, :, :, :, None, :] * eye[None, None, :, None, :, None]
    return bd.reshape(nd, nb // per, per * bw, per * bw).astype(BF16)


def _block_diag_grad(g):
    per = LRU_LANES // LRU_BLOCK_W
    nd, ng, _, _ = g.shape
    gr = g.reshape(nd, ng, per, LRU_BLOCK_W, per, LRU_BLOCK_W)
    diag = jnp.stack([gr[:, :, k, :, k, :] for k in range(per)], axis=2)
    return diag.reshape(nd, ng * per, LRU_BLOCK_W, LRU_BLOCK_W)


def _mod_select(mod16, b, d):
    m3 = mod16.reshape(MOD_ROWS, 3, d)
    lat = m3[:b]
    ctx = jnp.broadcast_to(m3[b][None], (b, 3, d))
    sel = jnp.stack([ctx, lat], axis=1)
    return jnp.pad(sel, ((0, 0), (0, 0), (0, 5), (0, 0)))


def _layer_fwd(x3, c16, p, cos, sin, ctx_len):
    b, t, d = x3.shape
    off, n_cols = _layout(d)
    mod16 = _mod_fwd(c16, p["w_mod"], p["b_mod"])
    modsel = _mod_select(mod16, b, d)
    h = _norm_mod_fwd(x3, p["norm_g"], modsel, ctx_len)
    proj = _matmul(h.reshape(b * t, d), p["w_in"], bm=_pick(b * t, TOKEN_BLOCKS), bn=1024, bk=d, name="proj_fwd",
                   n_outer=True)
    proj3 = proj.reshape(b, t, n_cols)
    ya = _lru_fwd(proj3, off["uA"], p["conv_w"], p["conv_b"], p["wa_bd"], p["ba"], p["wx_bd"], p["bx"], p["lam"], ctx_len)
    qkv_b = _prep_fwd(proj3, off["qB"], off["kB"], d, cos, sin, p["gq"], p["gk"], use_norm=False)
    yb, lse_b = _attn_win_fwd(qkv_b, p["sink"], d, ctx_len)
    qkv_c = _prep_fwd(proj3, off["qC"], off["kC"], d, cos, sin, p["gq"], p["gk"], use_norm=True)
    yc, lse_c = _attn_dense_fwd(qkv_c, d, ctx_len)
    x_new, out3 = _merge_fwd(x3, ya, yb, yc, proj3, p["w_br"], p["w_out"], modsel, ctx_len)
    return x_new, (x3, modsel, h, proj3, ya, yb, yc, qkv_b, lse_b, qkv_c, lse_c, out3)


def _layer_bwd(dxn3, saved, c16, p, cos, sin, ctx_len):
    x3, modsel, h, proj3, ya, yb, yc, qkv_b, lse_b, qkv_c, lse_c, out3 = saved
    b, t, d = x3.shape
    off, n_cols = _layout(d)
    rows = b * t
    bk = _pick(rows, TOKEN_BLOCKS)
    dproj3, dya, dyb, dyc, z4, dt4, mixb, doutb, gacc = _merge_bwd(dxn3, out3, ya, yb, yc, proj3, p["w_br"], p["w_out"],
                                                                   modsel, ctx_len)
    dw_br = jnp.stack([_matmul(z4[n].reshape(rows, d), dt4[n].reshape(rows, d), ta=True, bm=d, bn=d, bk=bk,
                               name="dw_branch") for n in range(3)])
    dw_out = _matmul(mixb.reshape(rows, d), doutb.reshape(rows, d), ta=True, bm=d, bn=d, bk=bk, name="dw_out")
    dproj3, vec, dwa, dwx = _lru_bwd(dproj3, proj3, off["uA"], dya, p["conv_w"], p["conv_b"], p["wa_bd"], p["ba"],
                                     p["wx_bd"], p["bx"], p["lam"], ctx_len)
    dq_b, dkt_b, dvt_b, dsk = _attn_win_bwd(qkv_b, p["sink"], yb, dyb, lse_b, d, ctx_len)
    dproj3, _ = _prep_bwd(dproj3, dq_b, dkt_b, dvt_b, proj3, off["qB"], off["kB"], d, cos, sin, p["gq"], p["gk"], False)
    dq_c, dkt_c, dvt_c = _attn_dense_bwd(qkv_c, yc, dyc, lse_c, d, ctx_len)
    dproj3, gqk = _prep_bwd(dproj3, dq_c, dkt_c, dvt_c, proj3, off["qC"], off["kC"], d, cos, sin, p["gq"], p["gk"], True)
    dproj2 = dproj3.reshape(rows, n_cols)
    dw_in = _matmul(h.reshape(rows, d), dproj2, ta=True, bm=d, bn=1024, bk=bk, name="dw_in")
    dh = _matmul(dproj2, p["w_in"], tb=True, bm=_pick(rows, TOKEN_BLOCKS), bn=d, bk=1024, name="dh")
    dx3, nacc = _norm_mod_bwd(dh.reshape(b, t, d), x3, p["norm_g"], modsel, dxn3, ctx_len)
    per = jnp.stack([nacc[:, :, 0], nacc[:, :, 1], gacc[:, :, 0]], axis=2)
    dmod = jnp.concatenate([per[:, 1].reshape(b, 3 * d), jnp.sum(per[:, 0], axis=0).reshape(1, 3 * d)], axis=0)
    dmod16 = jnp.pad(dmod, ((0, MOD_ROWS - b - 1), (0, 0)))
    dw_mod, db_mod, dc16 = _mod_bwd(c16, dmod16, p["w_mod"])
    vsum = jnp.sum(vec, axis=0)
    grads = {
        "norm_g": jnp.sum(nacc[:, :, 2], axis=(0, 1)),
        "w_mod": dw_mod, "b_mod": db_mod[0], "w_in": dw_in,
        "conv_w": vsum[0:4], "conv_b": vsum[4],
        "lru_wa": _block_diag_grad(jnp.sum(dwa, axis=0)), "lru_ba": vsum[5:7],
        "lru_wx": _block_diag_grad(jnp.sum(dwx, axis=0)), "lru_bx": vsum[7:9],
        "lru_lambda": vsum[9:11] * (-jax.nn.sigmoid(-p["lam"])),
        "attn_sink": jnp.sum(dsk[:, :, 0:GROUP, 0], axis=0).reshape(-1),
        "q_norm_g": jnp.sum(gqk[:, 0], axis=0), "k_norm_g": jnp.sum(gqk[:, 1], axis=0),
        "w_branch": dw_br, "w_out": dw_out,
    }
    return dx3, dc16, grads


def _reorder_in_cols(w, d, inverse=False):
    off_new, _ = _layout(d)
    segs = _orig_segments(d)
    if inverse:
        return jnp.concatenate([w[..., off_new[n]:off_new[n] + wd] for n, _, wd in segs], axis=-1)
    by_name = {n: (o, wd) for n, o, wd in segs}
    order = sorted(off_new, key=off_new.get)
    return jnp.concatenate([w[..., by_name[n][0]:by_name[n][0] + by_name[n][1]] for n in order], axis=-1)


def _layer_params(li, w):
    d = w["norm_g"].shape[1]
    return {
        "norm_g": w["norm_g"][li][None], "w_mod": w["w_mod"][li], "b_mod": w["b_mod"][li][None],
        "w_in": _reorder_in_cols(w["w_in"][li], d),
        "conv_w": w["conv_w"][li], "conv_b": w["conv_b"][li][None],
        "wa_bd": _block_diag(w["lru_wa"][li]), "ba": w["lru_ba"][li],
        "wx_bd": _block_diag(w["lru_wx"][li]), "bx": w["lru_bx"][li], "lam": w["lru_lambda"][li],
        "sink": w["attn_sink"][li], "gq": w["q_norm_g"][li][None], "gk": w["k_norm_g"][li][None],
        "w_br": w["w_branch"][li], "w_out": w["w_out"][li],
    }


def _local_step(x, c, ctx, target, c_ctx, final_g, layers):
    b, s, d = x.shape
    ctx_len = ctx.shape[1]
    cos, sin = _rope_tables(ctx_len, s)
    x3 = jnp.concatenate([ctx, x], axis=1)
    c16 = jnp.concatenate([c, c_ctx[None], jnp.zeros((MOD_ROWS - b - 1, d), F32)], axis=0)
    saved = []
    for p in layers:
        x3, sv = _layer_fwd(x3, c16, p, cos, sin, ctx_len)
        saved.append(sv)
    dx3, loss_acc, dgf = _final(x3, final_g[None], target, ctx_len)
    grads = [None] * len(layers)
    dc_ctx = jnp.zeros((d,), F32)
    for li in reversed(range(len(layers))):
        dx3, dc16, grads[li] = _layer_bwd(dx3, saved[li], c16, layers[li], cos, sin, ctx_len)
        dc_ctx = dc_ctx + dc16[b]
    return jnp.sum(loss_acc[:, 0, 0]), dx3[:, ctx_len:], dc_ctx, jnp.sum(dgf[:, 0], axis=0), grads


N_CHIPS = 4
ANY = pl.BlockSpec(memory_space=pl.ANY)


def _place():
    x, y, c = lax.axis_index("x"), lax.axis_index("y"), lax.axis_index("c")
    return x, y, c, [(1 - x, y), (x, 1 - y), (1 - x, 1 - y)]


def _axis_part(ref, axis, start, size):
    idx = [slice(None)] * len(ref.shape)
    idx[axis] = pl.ds(start, size)
    return ref.at[tuple(idx)]


def _remote(src, dst, send, recv, dev):
    return pltpu.make_async_remote_copy(src_ref=src, dst_ref=dst, send_sem=send, recv_sem=recv, device_id=dev,
                                        device_id_type=MESH)


COPY_PIECES = 8


def _pieces(src, dst):
    shape = src.shape
    for ax in range(len(shape) - 1):
        if shape[ax] % COPY_PIECES == 0 and shape[ax] // COPY_PIECES >= 8:
            sz = shape[ax] // COPY_PIECES
            return [(_axis_part(src, ax, j * sz, sz), _axis_part(dst, ax, j * sz, sz)) for j in range(COPY_PIECES)]
    return [(src, dst)]


def _gather_chips(wholes, axes, name):
    n = len(wholes)

    def body(*refs):
        bufs = refs[n:2 * n]
        send, recv, fsend, frecv = refs[2 * n:]
        x, y, c, chips = _place()
        me = 2 * x + y
        sib = (x, y, 1 - c)

        def block(i, chip_index, half):
            sz = wholes[i].shape[axes[i]] // N_CHIPS
            hl = wholes[i].shape[0] // 2
            return _axis_part(bufs[i], axes[i], chip_index * sz, sz).at[pl.ds(half * hl, hl)]

        for i in range(n):
            for k, (px, py) in enumerate(chips):
                _remote(block(i, me, c), block(i, me, c), send.at[i, k], recv.at[i, k], (px, py, c)).start()
        for i in range(n):
            for k, (px, py) in enumerate(chips):
                landed = block(i, 2 * px + py, c)
                _remote(landed, landed, send.at[i, k], recv.at[i, k], (px, py, c)).wait_recv()
                for s_, d_ in _pieces(landed, landed):
                    _remote(s_, d_, fsend.at[i, k], frecv.at[i, k], sib).start()
        for i in range(n):
            for k, (px, py) in enumerate(chips):
                passed = _remote(block(i, 2 * px + py, c), block(i, 2 * px + py, 1 - c), fsend.at[i, k], frecv.at[i, k], sib)
                passed.wait_recv()
                passed.wait_send()
                _remote(block(i, me, c), block(i, me, c), send.at[i, k], recv.at[i, k], (px, py, c)).wait_send()

    sems = pltpu.SemaphoreType.DMA((n, 3))
    return pl.pallas_call(
        body, name=name, in_specs=[ANY] * n, out_specs=tuple([ANY] * n),
        out_shape=tuple(jax.ShapeDtypeStruct(a.shape, a.dtype) for a in wholes),
        input_output_aliases={i: i for i in range(n)},
        scratch_shapes=[sems, sems, sems, sems],
    )(*wholes)


def _own_block_placed(shard, axis, chip):
    shape = list(shard.shape)
    shape[axis] *= N_CHIPS
    return lax.dynamic_update_slice_in_dim(lax.empty(tuple(shape), shard.dtype), shard, chip * shard.shape[axis], axis)


def _split_cores(gs, name):
    n = len(gs)

    def body(*refs):
        ins, got = refs[:n], refs[n:2 * n]
        send, recv = refs[2 * n:]
        x, y, c, _ = _place()
        sib = (x, y, 1 - c)

        def theirs(i):
            hl = gs[i].shape[0] // 2
            return ins[i].at[pl.ds((1 - c) * hl, hl)]

        for i in range(n):
            for s_, d_ in _pieces(theirs(i), got[i]):
                _remote(s_, d_, send.at[i], recv.at[i], sib).start()
        for i in range(n):
            _remote(theirs(i), got[i], send.at[i], recv.at[i], sib).wait()

    return pl.pallas_call(
        body, name=name, in_specs=[ANY] * n, out_specs=tuple([ANY] * n),
        out_shape=tuple(jax.ShapeDtypeStruct((g.shape[0] // 2,) + g.shape[1:], g.dtype) for g in gs),
        scratch_shapes=[pltpu.SemaphoreType.DMA((n,)), pltpu.SemaphoreType.DMA((n,))],
    )(*gs)


def _scatter_chips(pbs, axes, name):
    n = len(pbs)

    def block(p, ax):
        shape = list(p.shape)
        shape[ax] //= N_CHIPS
        return tuple(shape)

    def body(*refs):
        inb, got = refs[:n], refs[n:2 * n]
        send, recv = refs[2 * n:]
        x, y, c, chips = _place()

        def part(i, chip_index):
            sz = pbs[i].shape[axes[i]] // N_CHIPS
            return _axis_part(inb[i], axes[i], chip_index * sz, sz)

        for i in range(n):
            for k, (px, py) in enumerate(chips):
                _remote(part(i, 2 * px + py), got[i].at[k], send.at[i, k], recv.at[i, k], (px, py, c)).start()
        for i in range(n):
            for k, (px, py) in enumerate(chips):
                _remote(part(i, 2 * px + py), got[i].at[k], send.at[i, k], recv.at[i, k], (px, py, c)).wait()

    return pl.pallas_call(
        body, name=name, in_specs=[ANY] * n, out_specs=tuple([ANY] * n),
        out_shape=tuple(jax.ShapeDtypeStruct((3,) + block(p, ax), p.dtype) for p, ax in zip(pbs, axes)),
        scratch_shapes=[pltpu.SemaphoreType.DMA((n, 3)), pltpu.SemaphoreType.DMA((n, 3))],
    )(*pbs)


def _join_cores(bufs, name):
    n = len(bufs)

    def body(*refs):
        outs = refs[n:2 * n]
        send, recv = refs[2 * n:]
        x, y, c, _ = _place()
        sib = (x, y, 1 - c)

        def half(i, which):
            hl = bufs[i].shape[0] // 2
            return outs[i].at[pl.ds(which * hl, hl)]

        for i in range(n):
            for s_, d_ in _pieces(half(i, c), half(i, c)):
                _remote(s_, d_, send.at[i], recv.at[i], sib).start()
        for i in range(n):
            cp = _remote(half(i, c), half(i, 1 - c), send.at[i], recv.at[i], sib)
            cp.wait_recv()
            cp.wait_send()

    return pl.pallas_call(
        body, name=name, in_specs=[ANY] * n, out_specs=tuple([ANY] * n),
        out_shape=tuple(jax.ShapeDtypeStruct(a.shape, a.dtype) for a in bufs),
        input_output_aliases={i: i for i in range(n)},
        scratch_shapes=[pltpu.SemaphoreType.DMA((n,)), pltpu.SemaphoreType.DMA((n,))],
    )(*bufs)


def _all_reduce_small(buf):
    r = buf.shape[0]

    def body(in_ref, out_ref, sib_buf, chip_sum, got, send, recv):
        x, y, c, chips = _place()
        cp = _remote(in_ref, sib_buf, send.at[0], recv.at[0], (x, y, 1 - c))
        cp.start()
        cp.wait()
        chip_sum[...] = in_ref[...] + sib_buf[...]
        cps = [_remote(chip_sum, got.at[k], send.at[1 + k], recv.at[1 + k], (px, py, c)) for k, (px, py) in enumerate(chips)]
        for cp in cps:
            cp.start()
        for cp in cps:
            cp.wait()
        out_ref[...] = (chip_sum[...] + got[0]) + (got[1] + got[2])

    return pl.pallas_call(
        body, name="all_reduce_small", out_shape=jax.ShapeDtypeStruct(buf.shape, F32),
        in_specs=[pl.BlockSpec(memory_space=pltpu.VMEM)], out_specs=pl.BlockSpec(memory_space=pltpu.VMEM),
        scratch_shapes=[pltpu.VMEM((r, 128), F32), pltpu.VMEM((r, 128), F32), pltpu.VMEM((3, r, 128), F32),
                        pltpu.SemaphoreType.DMA((4,)), pltpu.SemaphoreType.DMA((4,))],
        compiler_params=_cparams(None, VMEM_LIMIT),
    )(buf)


ELEMENTWISE_BLOCK_BYTES = 1 << 20


def _view2d(a):
    cols = a.shape[-1] if a.ndim > 1 else 128
    return a.reshape(-1, cols)


def _row_block(rows, cols):
    want = max(8, ELEMENTWISE_BLOCK_BYTES // (4 * cols))
    br = rows
    while br > want and br % 2 == 0 and (br // 2) % 16 == 0:
        br //= 2
    return br


def _core():
    return lax.axis_index("c")


def _chip():
    return 2 * lax.axis_index("x") + lax.axis_index("y")


def _sum_half(g, got, name):
    h = got.shape[0]
    gv = g.reshape(2 * h, -1, g.shape[-1])
    tv = got.reshape(h, -1, g.shape[-1])
    _, rows, cols = tv.shape
    br = _row_block(rows, cols)

    def body(g_ref, t_ref, p_ref, pb_ref):
        p = g_ref[...] + t_ref[...]
        p_ref[...] = p
        pb_ref[...] = p.astype(BF16)

    blk = pl.BlockSpec((1, br, cols), lambda l, i: (l, i, 0))
    p, pb = pl.pallas_call(
        body, name=name, grid=(h, rows // br),
        in_specs=[pl.BlockSpec((1, br, cols), lambda l, i: (_core() * h + l, i, 0)), blk], out_specs=(blk, blk),
        out_shape=(jax.ShapeDtypeStruct(tv.shape, F32), jax.ShapeDtypeStruct(tv.shape, BF16)),
        compiler_params=_cparams(("parallel", "parallel")))(gv, tv)
    return p.reshape(got.shape), pb.reshape(got.shape)


def _sum_blocks(p, got3, axis, name):
    h = p.shape[0]
    blk_shape = got3.shape[1:]
    cols_mode = axis == p.ndim - 1
    pv = p.reshape(-1, p.shape[-2], p.shape[-1])
    tv = got3.reshape(3, -1, blk_shape[-2], blk_shape[-1])
    la, rb, cb = tv.shape[1:]
    assert cols_mode or axis == p.ndim - 2
    br = _row_block(rb, cb)
    per = rb // br

    def body(p_ref, a_ref, b_ref, c_ref, out_ref):
        out_ref[...] = ((p_ref[...] + a_ref[0].astype(F32)) + b_ref[0].astype(F32)) + c_ref[0].astype(F32)

    if cols_mode:
        p_spec = pl.BlockSpec((1, br, cb), lambda l, i: (l, i, _chip()))
    else:
        p_spec = pl.BlockSpec((1, br, cb), lambda l, i: (l, _chip() * per + i, 0))
    out = pl.pallas_call(
        body, name=name, grid=(la, per),
        in_specs=[p_spec] + [pl.BlockSpec((1, 1, br, cb), lambda l, i, k=k: (k, l, i, 0)) for k in range(3)],
        out_specs=pl.BlockSpec((1, br, cb), lambda l, i: (_core() * la + l, i, 0)),
        out_shape=jax.ShapeDtypeStruct((2 * la, rb, cb), F32),
        compiler_params=_cparams(("parallel", "parallel")))(pv, tv, tv, tv)
    return out.reshape((2 * h,) + blk_shape[1:])


def _adamw(w, g, m, v, name):
    shape = w.shape
    ops = [_view2d(a) for a in (w, g, m, v)]
    rows, cols = ops[0].shape
    br = _row_block(rows, cols)
    c1 = 1.0 - ADAM_B1 ** ADAM_STEP
    c2 = 1.0 - ADAM_B2 ** ADAM_STEP

    def body(w_ref, g_ref, m_ref, v_ref, d_ref, nm_ref, nv_ref):
        g_ = g_ref[...]
        nm = ADAM_B1 * m_ref[...] + (1.0 - ADAM_B1) * g_
        nv = ADAM_B2 * v_ref[...] + (1.0 - ADAM_B2) * (g_ * g_)
        d_ref[...] = -ADAM_LR * ((nm / c1) / (jnp.sqrt(nv / c2) + ADAM_EPS) + ADAM_WD * w_ref[...])
        nm_ref[...] = nm
        nv_ref[...] = nv

    blk = pl.BlockSpec((br, cols), lambda i: (i, 0))
    outs = pl.pallas_call(body, name=name, grid=(rows // br,), in_specs=[blk] * 4, out_specs=(blk, blk, blk),
                          out_shape=tuple(jax.ShapeDtypeStruct((rows, cols), F32) for _ in range(3)),
                          compiler_params=_cparams(("parallel",)))(*ops)
    return tuple(o.reshape(shape) for o in outs)


def _pack(arrays):
    flat = jnp.concatenate([a.reshape(-1) for a in arrays])
    pad = (-flat.shape[0]) % (8 * 128)
    return jnp.pad(flat, (0, pad)).reshape(-1, 128)


def _unpack(buf, shapes):
    flat = buf.reshape(-1)
    out, o = [], 0
    for s in shapes:
        n = int(np.prod(s))
        out.append(flat[o:o + n].reshape(s))
        o += n
    return out


WEIGHTS = ["c_ctx", "norm_g", "w_mod", "b_mod", "w_in", "conv_w", "conv_b", "lru_wa", "lru_ba", "lru_wx", "lru_bx",
           "lru_lambda", "attn_sink", "q_norm_g", "k_norm_g", "w_branch", "w_out", "final_g"]
BIG = {"w_mod": 2, "w_in": 2, "w_branch": 2, "w_out": 1}
SMALL_SHARDED = ["conv_w", "lru_ba", "lru_bx", "lru_lambda"]
REPLICATED = [n for n in WEIGHTS if n not in BIG and n not in SMALL_SHARDED]


def kernel(x, c, ctx, c_ctx, norm_g, w_mod, b_mod, w_in, conv_w, conv_b, lru_wa, lru_ba, lru_wx, lru_bx, lru_lambda, attn_sink, q_norm_g, k_norm_g, w_branch, w_out, final_g, loss_target, m_c_ctx, m_norm_g, m_w_mod, m_b_mod, m_w_in, m_conv_w, m_conv_b, m_lru_wa, m_lru_ba, m_lru_wx, m_lru_bx, m_lru_lambda, m_attn_sink, m_q_norm_g, m_k_norm_g, m_w_branch, m_w_out, m_final_g, v_c_ctx, v_norm_g, v_w_mod, v_b_mod, v_w_in, v_conv_w, v_conv_b, v_lru_wa, v_lru_ba, v_lru_wx, v_lru_bx, v_lru_lambda, v_attn_sink, v_q_norm_g, v_k_norm_g, v_w_branch, v_w_out, v_final_g):
    args = dict(locals())
    w = {n: args[n] for n in WEIGHTS}
    mom = {n: args["m_" + n] for n in WEIGHTS}
    var = {n: args["v_" + n] for n in WEIGHTS}
    depth, d = norm_g.shape
    chip = 2 * lax.axis_index("x") + lax.axis_index("y")

    big_names = list(BIG)
    small_shard = jnp.concatenate([w[n] for n in SMALL_SHARDED], axis=1)
    gather_axes = [BIG[n] for n in big_names] + [2]
    placed = [_own_block_placed(a, ax, chip)
              for a, ax in zip([w[n].astype(BF16) for n in big_names] + [small_shard], gather_axes)]
    gathered = _gather_chips(placed, gather_axes, "gather_weights")
    whole = dict(w)
    whole.update(dict(zip(big_names, gathered[:-1])))
    o = 0
    for n in SMALL_SHARDED:
        rows = w[n].shape[1]
        whole[n] = gathered[-1][:, o:o + rows]
        o += rows
    layers = [_layer_params(li, whole) for li in range(depth)]

    loss_local, grad_x, g_c_ctx, g_final, lgrads = _local_step(x, c, ctx, loss_target, c_ctx, final_g, layers)
    loss = lax.psum(loss_local, ("x", "y", "c"))
    full = {n: jnp.stack([lg[n] for lg in lgrads]) for n in lgrads[0]}
    full["w_in"] = _reorder_in_cols(full["w_in"], d, inverse=True)
    full["c_ctx"], full["final_g"] = g_c_ctx, g_final

    bigs = [full[n] for n in big_names]
    got = _split_cores(bigs, "grad_split_cores")
    parts = [_sum_half(g, t_, "grad_chip_sum") for g, t_ in zip(bigs, got)]
    recv = _scatter_chips([pb for _, pb in parts], [BIG[n] for n in big_names], "grad_scatter_chips")
    totals = [_sum_blocks(p_, r, BIG[n], "grad_total") for (p_, _), r, n in zip(parts, recv, big_names)]
    grad = dict(zip(big_names, _join_cores(totals, "grad_join_cores")))

    small_names = REPLICATED + SMALL_SHARDED
    reduced = _unpack(_all_reduce_small(_pack([full[n] for n in small_names])), [full[n].shape for n in small_names])
    for n, g in zip(small_names, reduced):
        if n in SMALL_SHARDED:
            sz = w[n].shape[-1]
            g = lax.dynamic_slice_in_dim(g, chip * sz, sz, axis=g.ndim - 1)
        grad[n] = g

    delta, new_m, new_v = {}, {}, {}
    for n in big_names:
        delta[n], new_m[n], new_v[n] = _adamw(w[n], grad[n], mom[n], var[n], "adamw_" + n)
    shapes = [w[n].shape for n in small_names]
    packed = _adamw(_pack([w[n] for n in small_names]), _pack([grad[n] for n in small_names]),
                    _pack([mom[n] for n in small_names]), _pack([var[n] for n in small_names]), "adamw_small")
    for res, p in zip((delta, new_m, new_v), packed):
        res.update(dict(zip(small_names, _unpack(p, shapes))))

    return (loss, grad_x, *[grad[n] for n in WEIGHTS], *[delta[n] for n in WEIGHTS],
            *[new_m[n] for n in WEIGHTS], *[new_v[n] for n in WEIGHTS])
```

```python
import functools

import jax
import jax.numpy as jnp
import numpy as np
from jax import lax
from jax.experimental import pallas as pl
from jax.experimental.pallas import tpu as pltpu

F32 = jnp.float32
BF16 = jnp.bfloat16

HEAD_DIM = 128
GROUP = 4
LRU_BLOCK_W = 64
LRU_C = 8.0
WINDOW = 128
GRID_W = 64
ROPE_THETA = 10000.0
EPS = 1e-6
NEG_INF = -1e30
ADAM_LR, ADAM_B1, ADAM_B2, ADAM_EPS, ADAM_WD, ADAM_STEP = 0.001, 0.9, 0.999, 1e-08, 0.01, 10

ROW_BLOCK = 256
LRU_LANES = 128
LRU_CHUNK = 128
LRU_UNROLL = 2
DENSE_FWD_BQ = 256
LOG2E = 1.4426950408889634
ATT_BQ = 256
WIN_BQ = 256
WIN_SPAN = WIN_BQ + 2 * WINDOW
MOD_ROWS = 16
VMEM_LIMIT = 56 * 1024 * 1024

MESH = pl.DeviceIdType.MESH


def _cparams(sem=None, vmem=None):
    kw = {}
    if sem is not None:
        kw["dimension_semantics"] = sem
    if vmem is not None:
        kw["vmem_limit_bytes"] = vmem
    return pltpu.CompilerParams(**kw)


def _sigmoid(v):
    return 1.0 / (1.0 + jnp.exp(-v))


def _silu(v):
    return v * _sigmoid(v)


def _dsilu(v):
    s = _sigmoid(v)
    return s * (1.0 + v * (1.0 - s))


def _expm1(x):
    poly = x * (1.0 + x * (0.5 + x * (1.0 / 6 + x * (1.0 / 24 + x * (1.0 / 120 + x * (1.0 / 720 + x * (1.0 / 5040)))))))
    return jnp.where(jnp.abs(x) < 0.25, poly, jnp.exp(x) - 1.0)


def _log1p(y):
    u = 1.0 + y
    d = u - 1.0
    return jnp.where(d == 0.0, y, jnp.log(u) * (y / jnp.where(d == 0.0, 1.0, d)))


def _softplus(x):
    return jnp.maximum(x, 0.0) + _log1p(jnp.exp(-jnp.abs(x)))


def _dot(a, b):
    return jnp.dot(a, b, preferred_element_type=F32)


def _dot_nt(a, b):
    return lax.dot_general(a, b, (((1,), (1,)), ((), ())), preferred_element_type=F32)


def _dot_tn(a, b):
    return lax.dot_general(a, b, (((0,), (0,)), ((), ())), preferred_element_type=F32)


def _colsum(v):
    return jnp.sum(v, axis=0, keepdims=True)


def _layout(d_model):
    kvw = (d_model // HEAD_DIM // GROUP) * HEAD_DIM
    names = ["gA", "gB", "gC", "mA", "mB", "mC", "uA", "qB", "qC", "kB", "vB", "kC", "vC"]
    widths = [d_model] * 9 + [kvw] * 4
    off, o = {}, 0
    for n, w in zip(names, widths):
        off[n] = o
        o += w
    return off, o


def _orig_segments(d_model):
    kvw = (d_model // HEAD_DIM // GROUP) * HEAD_DIM
    names = ["uA", "gA", "qB", "kB", "vB", "gB", "qC", "kC", "vC", "gC", "mA", "mB", "mC"]
    widths = [d_model, d_model, d_model, kvw, kvw, d_model, d_model, kvw, kvw, d_model, d_model, d_model, d_model]
    out, o = [], 0
    for n, w in zip(names, widths):
        out.append((n, o, w))
        o += w
    return out


def _matmul(a, b, *, ta=False, tb=False, out_dtype=F32, bm, bn, bk, name, n_outer=False, lead=None):
    a_shape = a.shape if lead is None else a.shape[1:]
    b_shape = b.shape if lead is None else b.shape[1:]
    (kdim, m) = a_shape if ta else a_shape[::-1]
    (n, kdim2) = b_shape if tb else b_shape[::-1]
    assert kdim == kdim2 and m % bm == 0 and n % bn == 0 and kdim % bk == 0, (a.shape, b.shape, bm, bn, bk)
    nk = kdim // bk
    dims = (((0 if ta else 1,), (1 if tb else 0,)), ((), ()))

    def ij(f):
        return (lambda j, i, k: f(i, j, k)) if n_outer else f

    def body(a_ref, b_ref, o_ref, *scratch):
        r = lax.dot_general(a_ref[...].astype(BF16), b_ref[...].astype(BF16), dims, preferred_element_type=F32)
        if nk == 1:
            o_ref[...] = r.astype(out_dtype)
        else:
            acc = scratch[0]
            k = pl.program_id(2)

            @pl.when(k == 0)
            def _():
                acc[...] = r

            @pl.when(k > 0)
            def _():
                acc[...] += r

            @pl.when(k == nk - 1)
            def _():
                o_ref[...] = acc[...].astype(out_dtype)

    def spec(shape, f):
        f = ij(f)
        if lead is None:
            return pl.BlockSpec(shape, f)
        return pl.BlockSpec((None,) + shape, lambda *g: (lead,) + f(*g))

    a_spec = spec((bk, bm), lambda i, j, k: (k, i)) if ta else spec((bm, bk), lambda i, j, k: (i, k))
    b_spec = spec((bn, bk), lambda i, j, k: (j, k)) if tb else spec((bk, bn), lambda i, j, k: (k, j))
    return pl.pallas_call(
        body, name=name, grid=(n // bn, m // bm, nk) if n_outer else (m // bm, n // bn, nk),
        in_specs=[a_spec, b_spec], out_specs=pl.BlockSpec((bm, bn), ij(lambda i, j, k: (i, j))),
        out_shape=jax.ShapeDtypeStruct((m, n), out_dtype),
        scratch_shapes=[pltpu.VMEM((bm, bn), F32)] if nk > 1 else [],
        compiler_params=_cparams(("parallel", "parallel", "arbitrary"), VMEM_LIMIT),
    )(a, b)


def _mod_fwd(c16, w_mod, b_mod):
    d3 = w_mod.shape[1]

    def body(c_ref, w_ref, b_ref, o_ref):
        o_ref[...] = _dot(_silu(c_ref[...]).astype(BF16), w_ref[...]) + b_ref[...]

    return pl.pallas_call(body, name="mod_fwd", out_shape=jax.ShapeDtypeStruct((MOD_ROWS, d3), F32),
                          compiler_params=_cparams(None, VMEM_LIMIT))(c16, w_mod, b_mod)


def _mod_bwd(c16, dmod16, w_mod):
    d, d3 = w_mod.shape

    def body(c_ref, g_ref, w_ref, dw_ref, db_ref, dc_ref):
        c = c_ref[...]
        g = g_ref[...]
        gb = g.astype(BF16)
        dw_ref[...] = _dot_tn(_silu(c).astype(BF16), gb)
        db_ref[...] = _colsum(g)
        dc_ref[...] = _dot_nt(gb, w_ref[...]) * _dsilu(c)

    return pl.pallas_call(
        body, name="mod_bwd",
        out_shape=(jax.ShapeDtypeStruct((d, d3), F32), jax.ShapeDtypeStruct((1, d3), F32),
                   jax.ShapeDtypeStruct((MOD_ROWS, d), F32)),
        compiler_params=_cparams(None, VMEM_LIMIT))(c16, dmod16, w_mod)


def _row_kind(t, lb):
    return jnp.where(t >= lb, 1, 0)


def _norm_mod_fwd(x3, g, modsel, ctx_len):
    b, t, d = x3.shape
    bt = ROW_BLOCK
    lb = ctx_len // bt

    def body(x_ref, g_ref, m_ref, h_ref):
        x = x_ref[0]
        rstd = lax.rsqrt(jnp.mean(x * x, axis=-1, keepdims=True) + EPS)
        y = x * rstd * g_ref[...]
        h_ref[0] = (y * (1.0 + m_ref[0, 0, 1:2, :]) + m_ref[0, 0, 0:1, :]).astype(BF16)

    return pl.pallas_call(
        body, name="norm_mod_fwd", grid=(b, t // bt),
        in_specs=[pl.BlockSpec((1, bt, d), lambda i, j: (i, j, 0)),
                  pl.BlockSpec((1, d), lambda i, j: (0, 0)),
                  pl.BlockSpec((1, 1, 8, d), lambda i, j: (i, _row_kind(j, lb), 0, 0))],
        out_specs=pl.BlockSpec((1, bt, d), lambda i, j: (i, j, 0)),
        out_shape=jax.ShapeDtypeStruct((b, t, d), BF16),
        compiler_params=_cparams(("parallel", "arbitrary")),
    )(x3, g, modsel)


def _norm_mod_bwd(dh3, x3, g, modsel, dres3, ctx_len):
    b, t, d = x3.shape
    bt = ROW_BLOCK
    lb = ctx_len // bt

    def body(dh_ref, x_ref, g_ref, m_ref, dres_ref, dx_ref, acc_ref):
        j = pl.program_id(1)
        x = x_ref[0]
        dh = dh_ref[0]
        g_row = g_ref[...]
        rstd = lax.rsqrt(jnp.mean(x * x, axis=-1, keepdims=True) + EPS)
        xhat = x * rstd
        dhpre = dh * (1.0 + m_ref[0, 0, 1:2, :])
        dxhat = dhpre * g_row
        dx = rstd * (dxhat - xhat * jnp.mean(dxhat * xhat, axis=-1, keepdims=True))
        dx_ref[0] = dx + dres_ref[0]

        @pl.when((j == 0) | (j == lb))
        def _():
            acc_ref[...] = jnp.zeros_like(acc_ref)

        acc_ref[0, 0, 0:1, :] += _colsum(dh)
        acc_ref[0, 0, 1:2, :] += _colsum(dh * (xhat * g_row))
        acc_ref[0, 0, 2:3, :] += _colsum(dhpre * xhat)

    blk = pl.BlockSpec((1, bt, d), lambda i, j: (i, j, 0))
    return pl.pallas_call(
        body, name="norm_mod_bwd", grid=(b, t // bt),
        in_specs=[blk, blk, pl.BlockSpec((1, d), lambda i, j: (0, 0)),
                  pl.BlockSpec((1, 1, 8, d), lambda i, j: (i, _row_kind(j, lb), 0, 0)), blk],
        out_specs=(blk, pl.BlockSpec((1, 1, 8, d), lambda i, j: (i, _row_kind(j, lb), 0, 0))),
        out_shape=(jax.ShapeDtypeStruct((b, t, d), F32), jax.ShapeDtypeStruct((b, 2, 8, d), F32)),
        compiler_params=_cparams(("parallel", "arbitrary")),
    )(dh3, x3, g, modsel, dres3)


def _shifted_rows(ref, c, off, ctx_len, total):
    ct = LRU_CHUNK
    r0 = pl.multiple_of(c * ct, ct)
    x0 = ref[pl.ds(r0, ct), :]
    row = lax.broadcasted_iota(jnp.int32, x0.shape, 0)
    if off < 0:
        k = -off
        has = jnp.logical_and(r0 != 0, r0 != ctx_len)
        p0 = pl.multiple_of(jnp.maximum(r0 - 8, 0), 8)
        edge = jnp.where(has, ref[pl.ds(p0, 8), :], 0.0)
        out = pltpu.roll(x0, k, 0)
        for j in range(k):
            out = jnp.where(row == j, edge[8 - k + j:8 - k + j + 1, :], out)
    else:
        k = off
        has = jnp.logical_and(r0 + ct != ctx_len, r0 + ct != total)
        n0 = pl.multiple_of(jnp.minimum(r0 + ct, total - 8), 8)
        edge = jnp.where(has, ref[pl.ds(n0, 8), :], 0.0)
        out = pltpu.roll(x0, ct - k, 0)
        for j in range(k):
            out = jnp.where(row == ct - k + j, edge[j:j + 1, :], out)
    return out


def _chunk_scan(a, b, reverse):
    n = a.shape[0]
    row = lax.broadcasted_iota(jnp.int32, a.shape, 0)
    s = 1
    while s < n:
        if reverse:
            a_s, b_s, ok = pltpu.roll(a, n - s, 0), pltpu.roll(b, n - s, 0), row < n - s
        else:
            a_s, b_s, ok = pltpu.roll(a, s, 0), pltpu.roll(b, s, 0), row >= s
        b = jnp.where(ok, a * b_s + b, b)
        a = jnp.where(ok, a * a_s, a)
        s *= 2
    return a, b


def _loop_chunks(n, body, init):
    assert n % LRU_UNROLL == 0

    def group(s2, carry):
        for u in range(LRU_UNROLL):
            carry = body(LRU_UNROLL * s2 + u, carry)
        return carry

    return lax.fori_loop(0, n // LRU_UNROLL, group, init)


def _lru_order(d, s, n_ctx, n_all):
    if d == 0:
        return s
    return jnp.where(s < n_ctx, n_ctx - 1 - s, n_all - 1 - (s - n_ctx))


def _lru_gates(u, wa, ba, wx, bx, sp):
    ub = u.astype(BF16)
    r = _sigmoid(_dot(ub, wa) + ba)
    i = _sigmoid(_dot(ub, wx) + bx)
    log_a = (-LRU_C * sp) * r
    a = jnp.exp(log_a)
    sf = jnp.sqrt(-_expm1(2.0 * log_a))
    return ub, r, i, a, sf


def _lru_specs(t, n_lane_blocks_offset):
    ln = LRU_LANES
    return [
        pl.BlockSpec((4, ln), lambda i, j: (0, j)),
        pl.BlockSpec((1, ln), lambda i, j: (0, j)),
        pl.BlockSpec((2, 1, ln, ln), lambda i, j: (0, j, 0, 0)),
        pl.BlockSpec((2, ln), lambda i, j: (0, j)),
        pl.BlockSpec((2, 1, ln, ln), lambda i, j: (0, j, 0, 0)),
        pl.BlockSpec((2, ln), lambda i, j: (0, j)),
        pl.BlockSpec((2, ln), lambda i, j: (0, j)),
    ]


def _lru_conv(ua_ref, cw_ref, cb_ref, u_s, ctx_len, total):
    ct = LRU_CHUNK

    def conv(c, _):
        r0 = pl.multiple_of(c * ct, ct)
        u = (cw_ref[0:1, :] * _shifted_rows(ua_ref, c, -2, ctx_len, total)
             + cw_ref[1:2, :] * _shifted_rows(ua_ref, c, -1, ctx_len, total)
             + cw_ref[2:3, :] * ua_ref[pl.ds(r0, ct), :]
             + cw_ref[3:4, :] * _shifted_rows(ua_ref, c, 1, ctx_len, total) + cb_ref[...])
        u_s[pl.ds(r0, ct), :] = u
        return 0

    lax.fori_loop(0, total // ct, conv, 0)


def _lru_fwd(proj3, col0, conv_w, conv_b, wa_bd, ba, wx_bd, bx, lam, ctx_len):
    b, t, _ = proj3.shape
    d = conv_w.shape[1]
    ln, ct = LRU_LANES, LRU_CHUNK
    n_all, n_ctx = t // ct, ctx_len // ct
    cb0 = col0 // ln

    def body(ua_ref, cw_ref, cb_ref, wa_ref, ba_ref, wx_ref, bx_ref, lam_ref, y_ref, u_s):
        ua = ua_ref.at[0]
        _lru_conv(ua, cw_ref, cb_ref, u_s, ctx_len, t)
        for dr in (0, 1):
            sp = _softplus(-lam_ref[dr:dr + 1, :])
            wa, wx = wa_ref[dr, 0], wx_ref[dr, 0]
            ba_row, bx_row = ba_ref[dr:dr + 1, :], bx_ref[dr:dr + 1, :]

            def step(s, carry, dr=dr, sp=sp, wa=wa, wx=wx, ba_row=ba_row, bx_row=bx_row):
                c = _lru_order(dr, s, n_ctx, n_all)
                r0 = pl.multiple_of(c * ct, ct)
                u = u_s[pl.ds(r0, ct), :]
                _, _, i, a, sf = _lru_gates(u, wa, ba_row, wx, bx_row, sp)
                aa, h0 = _chunk_scan(a, sf * (i * u), reverse=(dr == 1))
                h = h0 + aa * carry
                if dr == 0:
                    y_ref[0, pl.ds(r0, ct), :] = h
                    return h[ct - 1:ct, :]
                y_ref[0, pl.ds(r0, ct), :] += h
                return h[0:1, :]

            _loop_chunks(n_all, step, jnp.zeros((1, ln), F32))

    return pl.pallas_call(
        body, name="lru_fwd", grid=(b, d // ln),
        in_specs=[pl.BlockSpec((1, t, ln), lambda i, j: (i, 0, cb0 + j))] + _lru_specs(t, cb0),
        out_specs=pl.BlockSpec((1, t, ln), lambda i, j: (i, 0, j)),
        out_shape=jax.ShapeDtypeStruct((b, t, d), F32),
        scratch_shapes=[pltpu.VMEM((t, ln), F32)],
        compiler_params=_cparams(("parallel", "parallel"), VMEM_LIMIT),
    )(proj3, conv_w, conv_b, wa_bd, ba, wx_bd, bx, lam)


def _lru_bwd(dproj3, proj3, col0, dy3, conv_w, conv_b, wa_bd, ba, wx_bd, bx, lam, ctx_len):
    b, t, _ = proj3.shape
    d = conv_w.shape[1]
    ln, ct = LRU_LANES, LRU_CHUNK
    n_all, n_ctx = t // ct, ctx_len // ct
    cb0 = col0 // ln

    def body(dproj_hbm, ua_ref, dy_ref, cw_ref, cb_ref, wa_ref, ba_ref, wx_ref, bx_ref, lam_ref,
             dua_ref, vec_ref, dwa_ref, dwx_ref, u_s, h_s, du_s, a_s, sf_s, i_s, r_s):
        del dproj_hbm
        ua = ua_ref.at[0]
        _lru_conv(ua, cw_ref, cb_ref, u_s, ctx_len, t)
        du_s[...] = jnp.zeros_like(du_s)
        vec_ref[...] = jnp.zeros_like(vec_ref)
        for dr in (0, 1):
            sp = _softplus(-lam_ref[dr:dr + 1, :])
            wa, wx = wa_ref[dr, 0], wx_ref[dr, 0]
            ba_row, bx_row = ba_ref[dr:dr + 1, :], bx_ref[dr:dr + 1, :]

            def fwd(s, carry, dr=dr, sp=sp, wa=wa, wx=wx, ba_row=ba_row, bx_row=bx_row):
                c = _lru_order(dr, s, n_ctx, n_all)
                r0 = pl.multiple_of(c * ct, ct)
                u = u_s[pl.ds(r0, ct), :]
                _, r, i, a, sf = _lru_gates(u, wa, ba_row, wx, bx_row, sp)
                aa, h0 = _chunk_scan(a, sf * (i * u), reverse=(dr == 1))
                h = h0 + aa * carry
                h_s[pl.ds(r0, ct), :] = h
                a_s[pl.ds(r0, ct), :] = a
                sf_s[pl.ds(r0, ct), :] = sf
                i_s[pl.ds(r0, ct), :] = i
                r_s[pl.ds(r0, ct), :] = r
                return h[ct - 1:ct, :] if dr == 0 else h[0:1, :]

            _loop_chunks(n_all, fwd, jnp.zeros((1, ln), F32))
            dwa_ref[0, dr, 0] = jnp.zeros((ln, ln), F32)
            dwx_ref[0, dr, 0] = jnp.zeros((ln, ln), F32)

            def bwd(sr, carry, dr=dr, sp=sp, wa=wa, wx=wx):
                gc, vacc = carry
                c = _lru_order(dr, n_all - 1 - sr, n_ctx, n_all)
                r0 = pl.multiple_of(c * ct, ct)
                u = u_s[pl.ds(r0, ct), :]
                h = h_s[pl.ds(r0, ct), :]
                dy = dy_ref[0, pl.ds(r0, ct), :]
                ub = u.astype(BF16)
                r, i, a, sf = r_s[pl.ds(r0, ct), :], i_s[pl.ds(r0, ct), :], a_s[pl.ds(r0, ct), :], sf_s[pl.ds(r0, ct), :]
                row = lax.broadcasted_iota(jnp.int32, a.shape, 0)
                if dr == 0:
                    alpha = jnp.where(row == ct - 1, 1.0, pltpu.roll(a, ct - 1, 0))
                    aa, g0 = _chunk_scan(alpha, dy, reverse=True)
                    g = g0 + aa * gc
                    gc_new = a[0:1, :] * g[0:1, :]
                    p0 = pl.multiple_of(jnp.maximum(r0 - 8, 0), 8)
                    edge = jnp.where(r0 != 0, h_s[pl.ds(p0, 8), :], 0.0)[7:8, :]
                    h_prev = jnp.where(row == 0, edge, pltpu.roll(h, 1, 0))
                else:
                    alpha = jnp.where(row == 0, 1.0, pltpu.roll(a, 1, 0))
                    aa, g0 = _chunk_scan(alpha, dy, reverse=False)
                    g = g0 + aa * gc
                    gc_new = a[ct - 1:ct, :] * g[ct - 1:ct, :]
                    r_end = r0 + ct
                    n0 = pl.multiple_of(jnp.where(r_end == t, 0, jnp.minimum(r_end, t - 8)), 8)
                    edge = jnp.where(r_end != ctx_len, h_s[pl.ds(n0, 8), :], 0.0)[0:1, :]
                    h_prev = jnp.where(row == ct - 1, edge, pltpu.roll(h, ct - 1, 0))
                da = g * h_prev
                iu = i * u
                diu = g * sf
                dlog_a = da * a - (g * iu) * (a * a) / sf
                dpre_r = (dlog_a * (-LRU_C * sp)) * (r * (1.0 - r))
                dpre_i = (diu * u) * (i * (1.0 - i))
                dpr_b, dpi_b = dpre_r.astype(BF16), dpre_i.astype(BF16)
                du = diu * i + _dot_nt(dpr_b, wa) + _dot_nt(dpi_b, wx)
                du_s[pl.ds(r0, ct), :] += du
                dwa_ref[0, dr, 0] += _dot_tn(ub, dpr_b)
                dwx_ref[0, dr, 0] += _dot_tn(ub, dpi_b)
                vacc = (vacc[0] + _colsum(dpre_r), vacc[1] + _colsum(dpre_i), vacc[2] + _colsum(dlog_a * (-LRU_C * r)))
                return gc_new, vacc

            zrow = jnp.zeros((1, ln), F32)
            _, vacc = _loop_chunks(n_all, bwd, (zrow, (zrow, zrow, zrow)))
            vec_ref[0, 5 + dr:6 + dr, :] = vacc[0]
            vec_ref[0, 7 + dr:8 + dr, :] = vacc[1]
            vec_ref[0, 9 + dr:10 + dr, :] = vacc[2]

        def conv_bwd(c, acc):
            r0 = pl.multiple_of(c * ct, ct)
            du = du_s[pl.ds(r0, ct), :]
            dua = (cw_ref[0:1, :] * _shifted_rows(du_s, c, 2, ctx_len, t)
                   + cw_ref[1:2, :] * _shifted_rows(du_s, c, 1, ctx_len, t)
                   + cw_ref[2:3, :] * du
                   + cw_ref[3:4, :] * _shifted_rows(du_s, c, -1, ctx_len, t))
            dua_ref[0, pl.ds(r0, ct), :] = dua.astype(BF16)
            return (acc[0] + _colsum(du * _shifted_rows(ua, c, -2, ctx_len, t)),
                    acc[1] + _colsum(du * _shifted_rows(ua, c, -1, ctx_len, t)),
                    acc[2] + _colsum(du * ua[pl.ds(r0, ct), :]),
                    acc[3] + _colsum(du * _shifted_rows(ua, c, 1, ctx_len, t)),
                    acc[4] + _colsum(du))

        zrow = jnp.zeros((1, ln), F32)
        acc = lax.fori_loop(0, n_all, conv_bwd, (zrow,) * 5)
        for k in range(5):
            vec_ref[0, k:k + 1, :] = acc[k]

    ng = d // ln
    return pl.pallas_call(
        body, name="lru_bwd", grid=(b, ng),
        in_specs=[pl.BlockSpec(memory_space=pl.ANY),
                  pl.BlockSpec((1, t, ln), lambda i, j: (i, 0, cb0 + j)),
                  pl.BlockSpec((1, t, ln), lambda i, j: (i, 0, j))] + _lru_specs(t, cb0),
        out_specs=(pl.BlockSpec((1, t, ln), lambda i, j: (i, 0, cb0 + j)),
                   pl.BlockSpec((1, 16, ln), lambda i, j: (i, 0, j)),
                   pl.BlockSpec((1, 2, 1, ln, ln), lambda i, j: (i, 0, j, 0, 0)),
                   pl.BlockSpec((1, 2, 1, ln, ln), lambda i, j: (i, 0, j, 0, 0))),
        out_shape=(jax.ShapeDtypeStruct(dproj3.shape, dproj3.dtype),
                   jax.ShapeDtypeStruct((b, 16, d), F32),
                   jax.ShapeDtypeStruct((b, 2, ng, ln, ln), F32),
                   jax.ShapeDtypeStruct((b, 2, ng, ln, ln), F32)),
        scratch_shapes=[pltpu.VMEM((t, ln), F32)] * 7,
        input_output_aliases={0: 0},
        compiler_params=_cparams(("parallel", "parallel"), VMEM_LIMIT),
    )(dproj3, proj3, dy3, conv_w, conv_b, wa_bd, ba, wx_bd, bx, lam)


def _rope_tables(ctx_len, seq):
    p = HEAD_DIM // 4
    inv = ROPE_THETA ** (-jnp.arange(p, dtype=F32) / p)
    tok = jnp.arange(seq)
    ang_r = (tok // GRID_W)[:, None] * inv
    ang_c = (tok % GRID_W)[:, None] * inv
    cos = jnp.concatenate([jnp.cos(ang_r)] * 2 + [jnp.cos(ang_c)] * 2, axis=1)
    sin = jnp.concatenate([-jnp.sin(ang_r), jnp.sin(ang_r), -jnp.sin(ang_c), jnp.sin(ang_c)], axis=1)
    cos = jnp.concatenate([jnp.ones((ctx_len, HEAD_DIM), F32), cos], axis=0)
    sin = jnp.concatenate([jnp.zeros((ctx_len, HEAD_DIM), F32), sin], axis=0)
    return cos, sin


def _swap_halves(v):
    lane = lax.broadcasted_iota(jnp.int32, v.shape, 1)
    return jnp.where((lane & 63) < 32, pltpu.roll(v, 96, 1), pltpu.roll(v, 32, 1))


def _head_rstd(v):
    return lax.rsqrt(jnp.mean(v * v, axis=-1, keepdims=True) + EPS)


QKV_BLOCK = GROUP * HEAD_DIM
PREP_ROWS = (2176, 256)


def _prep_fwd(proj3, qcol, kvcol, d, cos, sin, gq, gk, use_norm):
    b, t, _ = proj3.shape
    bt, wb = _pick(t, PREP_ROWS), QKV_BLOCK
    nqb = d // wb
    assert qcol % wb == 0 and kvcol % wb == 0 and d // HEAD_DIM // GROUP == 2
    qb0, kvb = qcol // wb, kvcol // wb

    def body(p_ref, cos_ref, sin_ref, gq_ref, gk_ref, o_ref):
        s = pl.program_id(2)
        c, sn = cos_ref[...], sin_ref[...]

        def rope(v):
            return v * c + _swap_halves(v) * sn

        @pl.when(s < nqb)
        def _():
            for hh in range(GROUP):
                v = p_ref[0, :, hh * HEAD_DIM:(hh + 1) * HEAD_DIM]
                if use_norm:
                    v = v * _head_rstd(v) * gq_ref[...]
                o_ref[0, :, hh * HEAD_DIM:(hh + 1) * HEAD_DIM] = rope(v).astype(BF16)

        @pl.when(s == nqb)
        def _():
            for hh in range(2):
                v = p_ref[0, :, hh * HEAD_DIM:(hh + 1) * HEAD_DIM]
                if use_norm:
                    v = v * _head_rstd(v) * gk_ref[...]
                o_ref[0, :, hh * HEAD_DIM:(hh + 1) * HEAD_DIM] = rope(v).astype(BF16)
            o_ref[0, :, 2 * HEAD_DIM:] = p_ref[0, :, 2 * HEAD_DIM:].astype(BF16)

    return pl.pallas_call(
        body, name="prep_fwd_norm" if use_norm else "prep_fwd", grid=(b, t // bt, nqb + 1),
        in_specs=[pl.BlockSpec((1, bt, wb), lambda i, j, s: (i, j, jnp.where(s < nqb, qb0 + s, kvb))),
                  pl.BlockSpec((bt, HEAD_DIM), lambda i, j, s: (j, 0)),
                  pl.BlockSpec((bt, HEAD_DIM), lambda i, j, s: (j, 0)),
                  pl.BlockSpec((1, HEAD_DIM), lambda i, j, s: (0, 0)),
                  pl.BlockSpec((1, HEAD_DIM), lambda i, j, s: (0, 0))],
        out_specs=pl.BlockSpec((1, bt, wb), lambda i, j, s: (i, j, s)),
        out_shape=jax.ShapeDtypeStruct((b, t, d + wb), BF16),
        compiler_params=_cparams(("parallel", "parallel", "arbitrary"), VMEM_LIMIT),
    )(proj3, cos, sin, gq, gk)


def _prep_bwd(dproj3, dq3, dkt, dvt, proj3, qcol, kvcol, d, cos, sin, gq, gk, use_norm):
    b, t, _ = proj3.shape
    bt, wb = _pick(t, PREP_ROWS), QKV_BLOCK
    nqb = d // wb
    qb0, kvb = qcol // wb, kvcol // wb
    kvh = dkt.shape[1]

    def body(dproj_hbm, dq_ref, dkt_ref, dvt_ref, p_ref, cos_ref, sin_ref, gq_ref, gk_ref, o_ref, gacc_ref):
        del dproj_hbm
        j, s = pl.program_id(1), pl.program_id(2)
        c, sn = cos_ref[...], sin_ref[...]

        @pl.when((j == 0) & (s == 0))
        def _():
            gacc_ref[...] = jnp.zeros_like(gacc_ref)

        def unrope(dv):
            return dv * c + _swap_halves(dv * sn)

        def head_bwd(dyv, xv, g_ref, acc_row):
            dyv = unrope(dyv)
            if not use_norm:
                return dyv
            rstd = _head_rstd(xv)
            xhat = xv * rstd
            gacc_ref[0, acc_row:acc_row + 1, :] += _colsum(dyv * xhat)
            dxhat = dyv * g_ref[...]
            return rstd * (dxhat - xhat * jnp.mean(dxhat * xhat, axis=-1, keepdims=True))

        @pl.when(s < nqb)
        def _():
            for hh in range(GROUP):
                sl = slice(hh * HEAD_DIM, (hh + 1) * HEAD_DIM)
                o_ref[0, :, sl] = head_bwd(dq_ref[0, :, sl], p_ref[0, :, sl], gq_ref, 0).astype(BF16)

        @pl.when(s == nqb)
        def _():
            for hh in range(kvh):
                sl = slice(hh * HEAD_DIM, (hh + 1) * HEAD_DIM)
                o_ref[0, :, sl] = head_bwd(dkt_ref[0, hh].T, p_ref[0, :, sl], gk_ref, 1).astype(BF16)
                sv = slice((kvh + hh) * HEAD_DIM, (kvh + hh + 1) * HEAD_DIM)
                o_ref[0, :, sv] = dvt_ref[0, hh].T.astype(BF16)

    col = lambda i, j, s: (i, j, jnp.where(s < nqb, qb0 + s, kvb))
    return pl.pallas_call(
        body, name="prep_bwd_norm" if use_norm else "prep_bwd", grid=(b, t // bt, nqb + 1),
        in_specs=[pl.BlockSpec(memory_space=pl.ANY),
                  pl.BlockSpec((1, bt, wb), lambda i, j, s: (i, j, jnp.minimum(s, nqb - 1))),
                  pl.BlockSpec((1, kvh, HEAD_DIM, bt), lambda i, j, s: (i, 0, 0, j)),
                  pl.BlockSpec((1, kvh, HEAD_DIM, bt), lambda i, j, s: (i, 0, 0, j)),
                  pl.BlockSpec((1, bt, wb), col),
                  pl.BlockSpec((bt, HEAD_DIM), lambda i, j, s: (j, 0)),
                  pl.BlockSpec((bt, HEAD_DIM), lambda i, j, s: (j, 0)),
                  pl.BlockSpec((1, HEAD_DIM), lambda i, j, s: (0, 0)),
                  pl.BlockSpec((1, HEAD_DIM), lambda i, j, s: (0, 0))],
        out_specs=(pl.BlockSpec((1, bt, wb), col), pl.BlockSpec((1, 8, HEAD_DIM), lambda i, j, s: (i, 0, 0))),
        out_shape=(jax.ShapeDtypeStruct(dproj3.shape, dproj3.dtype), jax.ShapeDtypeStruct((b, 8, HEAD_DIM), F32)),
        input_output_aliases={0: 0},
        compiler_params=_cparams(("parallel", "arbitrary", "arbitrary"), VMEM_LIMIT),
    )(dproj3, dq3, dkt, dvt, proj3, cos, sin, gq, gk)


def _stack_heads(ref, dtype=None):
    parts = [ref[0, :, g * HEAD_DIM:(g + 1) * HEAD_DIM] for g in range(GROUP)]
    v = jnp.concatenate(parts, axis=0)
    return v if dtype is None else v.astype(dtype)


def _unstack_heads(ref, v, bq):
    for g in range(GROUP):
        ref[0, :, g * HEAD_DIM:(g + 1) * HEAD_DIM] = v[g * bq:(g + 1) * bq, :]


def _attn_specs(t, d, bq):
    kvh = d // HEAD_DIM // GROUP
    kc0 = d // HEAD_DIM
    q_spec = pl.BlockSpec((1, bq, QKV_BLOCK), lambda i, h, j: (i, j, h))
    k_spec = pl.BlockSpec((1, t, HEAD_DIM), lambda i, h, j: (i, 0, kc0 + h))
    v_spec = pl.BlockSpec((1, t, HEAD_DIM), lambda i, h, j: (i, 0, kc0 + kvh + h))
    lse_spec = pl.BlockSpec((1, GROUP, bq, HEAD_DIM), lambda i, h, j: (i, h, j, 0))
    kt_spec = pl.BlockSpec((1, 1, HEAD_DIM, t), lambda i, h, j: (i, h, 0, 0))
    return kvh, q_spec, k_spec, v_spec, lse_spec, kt_spec


SCALE = HEAD_DIM ** -0.5


def _attn_dense_fwd(qkv, d, ctx_len):
    b, t, _ = qkv.shape
    bq = DENSE_FWD_BQ
    lq = ctx_len // bq
    kvh, q_spec, k_spec, v_spec, lse_spec, _ = _attn_specs(t, d, bq)

    def body(q_ref, k_ref, v_ref, o_ref, lse_ref):
        i = pl.program_id(2)

        def attend(k, v):
            for g in range(GROUP):
                sl = slice(g * HEAD_DIM, (g + 1) * HEAD_DIM)
                s = _dot_nt(q_ref[0, :, sl], k)
                m = jnp.max(s, axis=1, keepdims=True)
                p = jnp.exp2((s - m) * (SCALE * LOG2E))
                l = jnp.sum(p, axis=1, keepdims=True)
                o_ref[0, :, sl] = _dot(p.astype(BF16), v) / l
                lse_ref[0, g] = jnp.broadcast_to(m * SCALE + jnp.log(l), (bq, HEAD_DIM))

        @pl.when(i < lq)
        def _():
            attend(k_ref[0, 0:ctx_len, :], v_ref[0, 0:ctx_len, :])

        @pl.when(i >= lq)
        def _():
            attend(k_ref[0], v_ref[0])

    return pl.pallas_call(
        body, name="attn_dense_fwd", grid=(b, kvh, t // bq),
        in_specs=[q_spec, k_spec, v_spec], out_specs=(q_spec, lse_spec),
        out_shape=(jax.ShapeDtypeStruct((b, t, d), F32), jax.ShapeDtypeStruct((b, kvh * GROUP, t, HEAD_DIM), F32)),
        compiler_params=_cparams(("parallel", "parallel", "arbitrary"), VMEM_LIMIT),
    )(qkv, qkv, qkv)


def _attn_dense_bwd(qkv, o3, do3, lse, d, ctx_len):
    b, t, _ = qkv.shape
    bq = ATT_BQ
    lq = ctx_len // bq
    kvh, q_spec, k_spec, v_spec, lse_spec, kt_spec = _attn_specs(t, d, bq)

    def body(q_ref, k_ref, v_ref, o_ref, do_ref, lse_ref, dq_ref, dkt_ref, dvt_ref):
        i = pl.program_id(2)

        @pl.when(i == 0)
        def _():
            dkt_ref[...] = jnp.zeros_like(dkt_ref)
            dvt_ref[...] = jnp.zeros_like(dvt_ref)

        def run(k, v, width):
            dk_acc = dv_acc = None
            for g in range(GROUP):
                sl = slice(g * HEAD_DIM, (g + 1) * HEAD_DIM)
                q = q_ref[0, :, sl]
                do = do_ref[0, :, sl]
                dd = jnp.sum(do * o_ref[0, :, sl], axis=1, keepdims=True)
                dob = do.astype(BF16)
                p = jnp.exp2(_dot_nt(q, k) * (SCALE * LOG2E) - lse_ref[0, g][:, 0:1] * LOG2E)
                ds = (p * (_dot_nt(dob, v) - dd) * SCALE).astype(BF16)
                dq_ref[0, :, sl] = _dot(ds, k)
                dk_g = _dot(q.astype(F32).T.astype(BF16), ds)
                dv_g = _dot(do.T.astype(BF16), p.astype(BF16))
                dk_acc = dk_g if dk_acc is None else dk_acc + dk_g
                dv_acc = dv_g if dv_acc is None else dv_acc + dv_g
            dkt_ref[0, 0, :, 0:width] += dk_acc
            dvt_ref[0, 0, :, 0:width] += dv_acc

        @pl.when(i < lq)
        def _():
            run(k_ref[0, 0:ctx_len, :], v_ref[0, 0:ctx_len, :], ctx_len)

        @pl.when(i >= lq)
        def _():
            run(k_ref[0], v_ref[0], t)

    return pl.pallas_call(
        body, name="attn_dense_bwd", grid=(b, kvh, t // bq),
        in_specs=[q_spec, k_spec, v_spec, q_spec, q_spec, lse_spec], out_specs=(q_spec, kt_spec, kt_spec),
        out_shape=(jax.ShapeDtypeStruct((b, t, d), F32), jax.ShapeDtypeStruct((b, kvh, HEAD_DIM, t), F32),
                   jax.ShapeDtypeStruct((b, kvh, HEAD_DIM, t), F32)),
        compiler_params=_cparams(("parallel", "parallel", "arbitrary"), VMEM_LIMIT),
    )(qkv, qkv, qkv, o3, do3, lse)


def _sink_column(sink_ref, h, bq):
    rowi = lax.broadcasted_iota(jnp.int32, (GROUP * bq, 1), 0)
    col = jnp.zeros((GROUP * bq, 1), F32)
    for g in range(GROUP):
        col = jnp.where((rowi >= g * bq) & (rowi < (g + 1) * bq), sink_ref[h * GROUP + g], col)
    return col


def _band(i, lq, ctx_len, t, bq):
    n = i - lq
    start = pl.multiple_of(jnp.clip(ctx_len + n * bq - WINDOW, ctx_len, t - WIN_SPAN), WINDOW)
    shape = (GROUP * bq, WIN_SPAN)
    kpos = start - ctx_len + lax.broadcasted_iota(jnp.int32, shape, 1)
    qpos = n * bq + (lax.broadcasted_iota(jnp.int32, shape, 0) & (bq - 1))
    return start, jnp.abs(kpos - qpos) <= WINDOW


def _attn_win_fwd(qkv, sink, d, ctx_len):
    b, t, _ = qkv.shape
    bq = WIN_BQ
    rows = GROUP * bq
    lq = ctx_len // bq
    kvh, q_spec, k_spec, v_spec, lse_spec, _ = _attn_specs(t, d, bq)

    def body(sink_ref, q_ref, k_ref, v_ref, o_ref, lse_ref):
        h, i = pl.program_id(1), pl.program_id(2)
        q4 = _stack_heads(q_ref)
        sink_col = _sink_column(sink_ref, h, bq)
        sc = _dot_nt(q4, k_ref[0, 0:ctx_len, :]) * SCALE
        mc = jnp.maximum(jnp.max(sc, axis=1, keepdims=True), sink_col)

        def finish(m, l, acc):
            _unstack_heads(o_ref, acc / l, bq)
            lse_ref[0] = jnp.broadcast_to(m + jnp.log(l), (rows, HEAD_DIM)).reshape(GROUP, bq, HEAD_DIM)

        @pl.when(i < lq)
        def _():
            pc = jnp.exp(sc - mc)
            l = jnp.sum(pc, axis=1, keepdims=True) + jnp.exp(sink_col - mc)
            finish(mc, l, _dot(pc.astype(BF16), v_ref[0, 0:ctx_len, :]))

        @pl.when(i >= lq)
        def _():
            start, ok = _band(i, lq, ctx_len, t, bq)
            sb = jnp.where(ok, _dot_nt(q4, k_ref[0, pl.ds(start, WIN_SPAN), :]) * SCALE, NEG_INF)
            m = jnp.maximum(mc, jnp.max(sb, axis=1, keepdims=True))
            pc, pb = jnp.exp(sc - m), jnp.exp(sb - m)
            l = jnp.sum(pc, axis=1, keepdims=True) + jnp.sum(pb, axis=1, keepdims=True) + jnp.exp(sink_col - m)
            acc = _dot(pc.astype(BF16), v_ref[0, 0:ctx_len, :]) + _dot(pb.astype(BF16), v_ref[0, pl.ds(start, WIN_SPAN), :])
            finish(m, l, acc)

    return pl.pallas_call(
        body, name="attn_win_fwd", grid=(b, kvh, t // bq),
        in_specs=[pl.BlockSpec(memory_space=pltpu.SMEM), q_spec, k_spec, v_spec], out_specs=(q_spec, lse_spec),
        out_shape=(jax.ShapeDtypeStruct((b, t, d), F32), jax.ShapeDtypeStruct((b, kvh * GROUP, t, HEAD_DIM), F32)),
        compiler_params=_cparams(("parallel", "parallel", "arbitrary"), VMEM_LIMIT),
    )(sink, qkv, qkv, qkv)


def _attn_win_bwd(qkv, sink, o3, do3, lse, d, ctx_len):
    b, t, _ = qkv.shape
    bq = WIN_BQ
    rows = GROUP * bq
    lq = ctx_len // bq
    kvh, q_spec, k_spec, v_spec, lse_spec, kt_spec = _attn_specs(t, d, bq)

    def body(sink_ref, q_ref, k_ref, v_ref, o_ref, do_ref, lse_ref, dq_ref, dkt_ref, dvt_ref, dsk_ref):
        h, i = pl.program_id(1), pl.program_id(2)

        @pl.when(i == 0)
        def _():
            dkt_ref[...] = jnp.zeros_like(dkt_ref)
            dvt_ref[...] = jnp.zeros_like(dvt_ref)
            dsk_ref[...] = jnp.zeros_like(dsk_ref)

        q4 = _stack_heads(q_ref)
        do4 = _stack_heads(do_ref)
        dd = jnp.sum(do4 * _stack_heads(o_ref), axis=1, keepdims=True)
        lse_col = lse_ref[0].reshape(rows, HEAD_DIM)[:, 0:1]
        do4b = do4.astype(BF16)
        qt = q4.astype(F32).T.astype(BF16)
        dot = do4.T.astype(BF16)

        def part(k, v):
            return _dot_nt(q4, k) * SCALE, _dot_nt(do4b, v)

        def grads(p, dp, k):
            ds = (p * (dp - dd) * SCALE).astype(BF16)
            return _dot(ds, k), _dot(qt, ds), _dot(dot, p.astype(BF16))

        kc = k_ref[0, 0:ctx_len, :]
        sc, dpc = part(kc, v_ref[0, 0:ctx_len, :])
        dq_c, dk_c, dv_c = grads(jnp.exp(sc - lse_col), dpc, kc)
        dkt_ref[0, 0, :, 0:ctx_len] += dk_c
        dvt_ref[0, 0, :, 0:ctx_len] += dv_c
        _unstack_heads(dq_ref, dq_c, bq)

        @pl.when(i >= lq)
        def _():
            start, ok = _band(i, lq, ctx_len, t, bq)
            kb = k_ref[0, pl.ds(start, WIN_SPAN), :]
            sb, dpb = part(kb, v_ref[0, pl.ds(start, WIN_SPAN), :])
            pb = jnp.where(ok, jnp.exp(sb - lse_col), 0.0)
            dq_b, dk_b, dv_b = grads(pb, dpb, kb)
            dkt_ref[0, 0, :, pl.ds(start, WIN_SPAN)] += dk_b
            dvt_ref[0, 0, :, pl.ds(start, WIN_SPAN)] += dv_b
            for g in range(GROUP):
                dq_ref[0, :, g * HEAD_DIM:(g + 1) * HEAD_DIM] += dq_b[g * bq:(g + 1) * bq, :]

        ps = jnp.exp(_sink_column(sink_ref, h, bq) - lse_col) * dd
        for g in range(GROUP):
            val = jnp.sum(ps[g * bq:(g + 1) * bq, :], axis=0, keepdims=True)
            dsk_ref[0, 0, g:g + 1, :] -= jnp.broadcast_to(val, (1, HEAD_DIM))

    return pl.pallas_call(
        body, name="attn_win_bwd", grid=(b, kvh, t // bq),
        in_specs=[pl.BlockSpec(memory_space=pltpu.SMEM), q_spec, k_spec, v_spec, q_spec, q_spec, lse_spec],
        out_specs=(q_spec, kt_spec, kt_spec, pl.BlockSpec((1, 1, 8, HEAD_DIM), lambda i, h, j: (i, h, 0, 0))),
        out_shape=(jax.ShapeDtypeStruct((b, t, d), F32), jax.ShapeDtypeStruct((b, kvh, HEAD_DIM, t), F32),
                   jax.ShapeDtypeStruct((b, kvh, HEAD_DIM, t), F32), jax.ShapeDtypeStruct((b, kvh, 8, HEAD_DIM), F32)),
        compiler_params=_cparams(("parallel", "parallel", "arbitrary"), VMEM_LIMIT),
    )(sink, qkv, qkv, qkv, o3, do3, lse)


MERGE_BWD_ROWS = 256
MERGE_BWD_VMEM = 60 * 1024 * 1024


def _resident(shape):
    return pl.BlockSpec(shape, lambda *_: (0,) * len(shape), pipeline_mode=pl.Buffered(1))


def _merge_fwd(x3, ya, yb, yc, proj3, w_br, w_out, modsel, ctx_len):
    b, t, d = x3.shape
    bt = ROW_BLOCK
    lb = ctx_len // bt

    def body(x_ref, ya_ref, yb_ref, yc_ref, gm_ref, wbr_ref, wo_ref, m_ref, xn_ref, out_ref):
        mix = jnp.zeros((bt, d), F32)
        for n, y_ref in enumerate((ya_ref, yb_ref, yc_ref)):
            z = (y_ref[0] * _silu(gm_ref[0, :, n * d:(n + 1) * d])).astype(BF16)
            mix = mix + _sigmoid(gm_ref[0, :, (3 + n) * d:(4 + n) * d]) * _dot(z, wbr_ref[n])
        o = _dot(mix.astype(BF16), wo_ref[...])
        out_ref[0] = o
        xn_ref[0] = x_ref[0] + m_ref[0, 0, 2:3, :] * o

    blk = pl.BlockSpec((1, bt, d), lambda i, j: (i, j, 0))
    return pl.pallas_call(
        body, name="merge_fwd", grid=(b, t // bt),
        in_specs=[blk, blk, blk, blk, pl.BlockSpec((1, bt, 6 * d), lambda i, j: (i, j, 0)),
                  _resident((3, d, d)), _resident((d, d)),
                  pl.BlockSpec((1, 1, 8, d), lambda i, j: (i, _row_kind(j, lb), 0, 0))],
        out_specs=(blk, blk),
        out_shape=(jax.ShapeDtypeStruct((b, t, d), F32), jax.ShapeDtypeStruct((b, t, d), F32)),
        compiler_params=_cparams(("parallel", "arbitrary"), VMEM_LIMIT),
    )(x3, ya, yb, yc, proj3, w_br, w_out, modsel)


def _merge_bwd(dxn3, out3, ya, yb, yc, proj3, w_br, w_out, modsel, ctx_len):
    b, t, d = dxn3.shape
    n_cols = proj3.shape[2]
    bt = MERGE_BWD_ROWS
    lb = ctx_len // bt

    def body(dxn_ref, out_ref, ya_ref, yb_ref, yc_ref, gm_ref, wbr_ref, wo_ref, m_ref,
             dgm_ref, dya_ref, dyb_ref, dyc_ref, z_ref, dt_ref, mix_ref, dout_ref, gacc_ref):
        j = pl.program_id(1)
        dxn = dxn_ref[0]
        doutb = (m_ref[0, 0, 2:3, :] * dxn).astype(BF16)
        dout_ref[0] = doutb

        @pl.when((j == 0) | (j == lb))
        def _():
            gacc_ref[...] = jnp.zeros_like(gacc_ref)

        gacc_ref[0, 0, 0:1, :] += _colsum(dxn * out_ref[0])
        dmix = _dot_nt(doutb, wo_ref[...])
        mix = jnp.zeros((bt, d), F32)
        for n, (y_ref, dy_ref) in enumerate(((ya_ref, dya_ref), (yb_ref, dyb_ref), (yc_ref, dyc_ref))):
            g = gm_ref[0, :, n * d:(n + 1) * d]
            y = y_ref[0]
            sig_g = _sigmoid(g)
            silu_g = g * sig_g
            z = (y * silu_g).astype(BF16)
            z_ref[n, 0] = z
            tn = _dot(z, wbr_ref[n])
            s = _sigmoid(gm_ref[0, :, (3 + n) * d:(4 + n) * d])
            mix = mix + s * tn
            dgm_ref[0, :, (3 + n) * d:(4 + n) * d] = (dmix * tn * (s * (1.0 - s))).astype(BF16)
            dtb = (dmix * s).astype(BF16)
            dt_ref[n, 0] = dtb
            dz = _dot_nt(dtb, wbr_ref[n])
            dy_ref[0] = dz * silu_g
            dgm_ref[0, :, n * d:(n + 1) * d] = (dz * y * (sig_g * (1.0 + g * (1.0 - sig_g)))).astype(BF16)
        mix_ref[0] = mix.astype(BF16)

    blk = pl.BlockSpec((1, bt, d), lambda i, j: (i, j, 0))
    blk4 = pl.BlockSpec((3, 1, bt, d), lambda i, j: (0, i, j, 0))
    wide = pl.BlockSpec((1, bt, 6 * d), lambda i, j: (i, j, 0))
    return pl.pallas_call(
        body, name="merge_bwd", grid=(b, t // bt),
        in_specs=[blk, blk, blk, blk, blk, wide, _resident((3, d, d)), _resident((d, d)),
                  pl.BlockSpec((1, 1, 8, d), lambda i, j: (i, _row_kind(j, lb), 0, 0))],
        out_specs=(wide, blk, blk, blk, blk4, blk4, blk, blk,
                   pl.BlockSpec((1, 1, 8, d), lambda i, j: (i, _row_kind(j, lb), 0, 0))),
        out_shape=(jax.ShapeDtypeStruct((b, t, n_cols), BF16),
                   jax.ShapeDtypeStruct((b, t, d), F32), jax.ShapeDtypeStruct((b, t, d), F32),
                   jax.ShapeDtypeStruct((b, t, d), F32),
                   jax.ShapeDtypeStruct((3, b, t, d), BF16), jax.ShapeDtypeStruct((3, b, t, d), BF16),
                   jax.ShapeDtypeStruct((b, t, d), BF16), jax.ShapeDtypeStruct((b, t, d), BF16),
                   jax.ShapeDtypeStruct((b, 2, 8, d), F32)),
        compiler_params=_cparams(("parallel", "arbitrary"), MERGE_BWD_VMEM),
    )(dxn3, out3, ya, yb, yc, proj3, w_br, w_out, modsel)


def _final(x3, g, target, ctx_len):
    b, t, d = x3.shape
    bt = ROW_BLOCK
    lb = ctx_len // bt

    def body(x_ref, g_ref, t_ref, dx_ref, loss_ref, dg_ref):
        j = pl.program_id(1)

        @pl.when(j == 0)
        def _():
            loss_ref[...] = jnp.zeros_like(loss_ref)
            dg_ref[...] = jnp.zeros_like(dg_ref)

        @pl.when(j < lb)
        def _():
            dx_ref[...] = jnp.zeros_like(dx_ref)

        @pl.when(j >= lb)
        def _():
            x = x_ref[0]
            g_row = g_ref[...]
            rstd = lax.rsqrt(jnp.mean(x * x, axis=-1, keepdims=True) + EPS)
            xhat = x * rstd
            err = xhat * g_row - t_ref[0]
            loss_ref[...] += (0.5 / d) * jnp.sum(err * err)
            dy = err * (1.0 / d)
            dg_ref[0, 0:1, :] += _colsum(dy * xhat)
            dxhat = dy * g_row
            dx_ref[0] = rstd * (dxhat - xhat * jnp.mean(dxhat * xhat, axis=-1, keepdims=True))

    blk = pl.BlockSpec((1, bt, d), lambda i, j: (i, j, 0))
    return pl.pallas_call(
        body, name="final_loss", grid=(b, t // bt),
        in_specs=[blk, pl.BlockSpec((1, d), lambda i, j: (0, 0)),
                  pl.BlockSpec((1, bt, d), lambda i, j: (i, jnp.maximum(j - lb, 0), 0))],
        out_specs=(blk, pl.BlockSpec((1, 8, HEAD_DIM), lambda i, j: (i, 0, 0)), pl.BlockSpec((1, 8, d), lambda i, j: (i, 0, 0))),
        out_shape=(jax.ShapeDtypeStruct((b, t, d), F32), jax.ShapeDtypeStruct((b, 8, HEAD_DIM), F32),
                   jax.ShapeDtypeStruct((b, 8, d), F32)),
        compiler_params=_cparams(("parallel", "arbitrary")),
    )(x3, g, target)


TOKEN_BLOCKS = (1088, 512, 256, 128)


def _pick(n, options):
    for o in options:
        if n % o == 0:
            return o
    raise ValueError((n, options))


def _block_diag(w):
    per = LRU_LANES // LRU_BLOCK_W
    nd, nb, bw, _ = w.shape
    wr = w.reshape(nd, nb // per, per, bw, bw)
    eye = jnp.eye(per, dtype=w.dtype)
    bd = wr[:, :, :, :, None, :] * eye[None, None, :, None, :, None]
    return bd.reshape(nd, nb // per, per * bw, per * bw).astype(BF16)


def _block_diag_grad(g):
    per = LRU_LANES // LRU_BLOCK_W
    nd, ng, _, _ = g.shape
    gr = g.reshape(nd, ng, per, LRU_BLOCK_W, per, LRU_BLOCK_W)
    diag = jnp.stack([gr[:, :, k, :, k, :] for k in range(per)], axis=2)
    return diag.reshape(nd, ng * per, LRU_BLOCK_W, LRU_BLOCK_W)


def _mod_select(mod16, b, d):
    m3 = mod16.reshape(MOD_ROWS, 3, d)
    lat = m3[:b]
    ctx = jnp.broadcast_to(m3[b][None], (b, 3, d))
    sel = jnp.stack([ctx, lat], axis=1)
    return jnp.pad(sel, ((0, 0), (0, 0), (0, 5), (0, 0)))


def _layer_fwd(x3, c16, p, cos, sin, ctx_len):
    b, t, d = x3.shape
    off, n_cols = _layout(d)
    mod16 = _mod_fwd(c16, p["w_mod"], p["b_mod"])
    modsel = _mod_select(mod16, b, d)
    h = _norm_mod_fwd(x3, p["norm_g"], modsel, ctx_len)
    proj = _matmul(h.reshape(b * t, d), p["w_in"], bm=_pick(b * t, TOKEN_BLOCKS), bn=1024, bk=d, name="proj_fwd",
                   n_outer=True)
    proj3 = proj.reshape(b, t, n_cols)
    ya = _lru_fwd(proj3, off["uA"], p["conv_w"], p["conv_b"], p["wa_bd"], p["ba"], p["wx_bd"], p["bx"], p["lam"], ctx_len)
    qkv_b = _prep_fwd(proj3, off["qB"], off["kB"], d, cos, sin, p["gq"], p["gk"], use_norm=False)
    yb, lse_b = _attn_win_fwd(qkv_b, p["sink"], d, ctx_len)
    qkv_c = _prep_fwd(proj3, off["qC"], off["kC"], d, cos, sin, p["gq"], p["gk"], use_norm=True)
    yc, lse_c = _attn_dense_fwd(qkv_c, d, ctx_len)
    x_new, out3 = _merge_fwd(x3, ya, yb, yc, proj3, p["w_br"], p["w_out"], modsel, ctx_len)
    return x_new, (x3, modsel, h, proj3, ya, yb, yc, qkv_b, lse_b, qkv_c, lse_c, out3)


def _layer_bwd(dxn3, saved, c16, p, cos, sin, ctx_len):
    x3, modsel, h, proj3, ya, yb, yc, qkv_b, lse_b, qkv_c, lse_c, out3 = saved
    b, t, d = x3.shape
    off, n_cols = _layout(d)
    rows = b * t
    bk = _pick(rows, TOKEN_BLOCKS)
    dproj3, dya, dyb, dyc, z4, dt4, mixb, doutb, gacc = _merge_bwd(dxn3, out3, ya, yb, yc, proj3, p["w_br"], p["w_out"],
                                                                   modsel, ctx_len)
    dw_br = jnp.stack([_matmul(z4.reshape(3, rows, d), dt4.reshape(3, rows, d), ta=True, bm=d, bn=d, bk=bk,
                               name="dw_branch", lead=n) for n in range(3)])
    dw_out = _matmul(mixb.reshape(rows, d), doutb.reshape(rows, d), ta=True, bm=d, bn=d, bk=bk, name="dw_out")
    dproj3, vec, dwa, dwx = _lru_bwd(dproj3, proj3, off["uA"], dya, p["conv_w"], p["conv_b"], p["wa_bd"], p["ba"],
                                     p["wx_bd"], p["bx"], p["lam"], ctx_len)
    dq_b, dkt_b, dvt_b, dsk = _attn_win_bwd(qkv_b, p["sink"], yb, dyb, lse_b, d, ctx_len)
    dproj3, _ = _prep_bwd(dproj3, dq_b, dkt_b, dvt_b, proj3, off["qB"], off["kB"], d, cos, sin, p["gq"], p["gk"], False)
    dq_c, dkt_c, dvt_c = _attn_dense_bwd(qkv_c, yc, dyc, lse_c, d, ctx_len)
    dproj3, gqk = _prep_bwd(dproj3, dq_c, dkt_c, dvt_c, proj3, off["qC"], off["kC"], d, cos, sin, p["gq"], p["gk"], True)
    dproj2 = dproj3.reshape(rows, n_cols)
    dw_in = _matmul(h.reshape(rows, d), dproj2, ta=True, bm=d, bn=1024, bk=bk, name="dw_in")
    dh = _matmul(dproj2, p["w_in"], tb=True, bm=_pick(rows, TOKEN_BLOCKS), bn=d, bk=1024, name="dh")
    dx3, nacc = _norm_mod_bwd(dh.reshape(b, t, d), x3, p["norm_g"], modsel, dxn3, ctx_len)
    per = jnp.stack([nacc[:, :, 0], nacc[:, :, 1], gacc[:, :, 0]], axis=2)
    dmod = jnp.concatenate([per[:, 1].reshape(b, 3 * d), jnp.sum(per[:, 0], axis=0).reshape(1, 3 * d)], axis=0)
    dmod16 = jnp.pad(dmod, ((0, MOD_ROWS - b - 1), (0, 0)))
    dw_mod, db_mod, dc16 = _mod_bwd(c16, dmod16, p["w_mod"])
    vsum = jnp.sum(vec, axis=0)
    grads = {
        "norm_g": jnp.sum(nacc[:, :, 2], axis=(0, 1)),
        "w_mod": dw_mod, "b_mod": db_mod[0], "w_in": dw_in,
        "conv_w": vsum[0:4], "conv_b": vsum[4],
        "lru_wa": _block_diag_grad(jnp.sum(dwa, axis=0)), "lru_ba": vsum[5:7],
        "lru_wx": _block_diag_grad(jnp.sum(dwx, axis=0)), "lru_bx": vsum[7:9],
        "lru_lambda": vsum[9:11] * (-jax.nn.sigmoid(-p["lam"])),
        "attn_sink": jnp.sum(dsk[:, :, 0:GROUP, 0], axis=0).reshape(-1),
        "q_norm_g": jnp.sum(gqk[:, 0], axis=0), "k_norm_g": jnp.sum(gqk[:, 1], axis=0),
        "w_branch": dw_br, "w_out": dw_out,
    }
    return dx3, dc16, grads


def _reorder_in_cols(w, d, inverse=False):
    off_new, _ = _layout(d)
    segs = _orig_segments(d)
    if inverse:
        return jnp.concatenate([w[..., off_new[n]:off_new[n] + wd] for n, _, wd in segs], axis=-1)
    by_name = {n: (o, wd) for n, o, wd in segs}
    order = sorted(off_new, key=off_new.get)
    return jnp.concatenate([w[..., by_name[n][0]:by_name[n][0] + by_name[n][1]] for n in order], axis=-1)


def _layer_params(li, w):
    d = w["norm_g"].shape[1]
    return {
        "norm_g": w["norm_g"][li][None], "w_mod": w["w_mod"][li], "b_mod": w["b_mod"][li][None],
        "w_in": _reorder_in_cols(w["w_in"][li], d),
        "conv_w": w["conv_w"][li], "conv_b": w["conv_b"][li][None],
        "wa_bd": _block_diag(w["lru_wa"][li]), "ba": w["lru_ba"][li],
        "wx_bd": _block_diag(w["lru_wx"][li]), "bx": w["lru_bx"][li], "lam": w["lru_lambda"][li],
        "sink": w["attn_sink"][li], "gq": w["q_norm_g"][li][None], "gk": w["k_norm_g"][li][None],
        "w_br": w["w_branch"][li], "w_out": w["w_out"][li],
    }


def _local_step(x, c, ctx, target, c_ctx, final_g, layers):
    b, s, d = x.shape
    ctx_len = ctx.shape[1]
    cos, sin = _rope_tables(ctx_len, s)
    x3 = jnp.concatenate([ctx, x], axis=1)
    c16 = jnp.concatenate([c, c_ctx[None], jnp.zeros((MOD_ROWS - b - 1, d), F32)], axis=0)
    saved = []
    for p in layers:
        x3, sv = _layer_fwd(x3, c16, p, cos, sin, ctx_len)
        saved.append(sv)
    dx3, loss_acc, dgf = _final(x3, final_g[None], target, ctx_len)
    grads = [None] * len(layers)
    dc_ctx = jnp.zeros((d,), F32)
    for li in reversed(range(len(layers))):
        dx3, dc16, grads[li] = _layer_bwd(dx3, saved[li], c16, layers[li], cos, sin, ctx_len)
        dc_ctx = dc_ctx + dc16[b]
    return jnp.sum(loss_acc[:, 0, 0]), dx3[:, ctx_len:], dc_ctx, jnp.sum(dgf[:, 0], axis=0), grads


N_CHIPS = 4
ANY = pl.BlockSpec(memory_space=pl.ANY)


def _place():
    x, y, c = lax.axis_index("x"), lax.axis_index("y"), lax.axis_index("c")
    return x, y, c, [(1 - x, y), (x, 1 - y), (1 - x, 1 - y)]


def _axis_part(ref, axis, start, size):
    idx = [slice(None)] * len(ref.shape)
    idx[axis] = pl.ds(start, size)
    return ref.at[tuple(idx)]


def _remote(src, dst, send, recv, dev):
    return pltpu.make_async_remote_copy(src_ref=src, dst_ref=dst, send_sem=send, recv_sem=recv, device_id=dev,
                                        device_id_type=MESH)


COPY_PIECES = 8


def _pieces(src, dst):
    shape = src.shape
    for ax in range(len(shape) - 1):
        if shape[ax] % COPY_PIECES == 0 and shape[ax] // COPY_PIECES >= 8:
            sz = shape[ax] // COPY_PIECES
            return [(_axis_part(src, ax, j * sz, sz), _axis_part(dst, ax, j * sz, sz)) for j in range(COPY_PIECES)]
    return [(src, dst)]


def _gather_chips(wholes, axes, name):
    n = len(wholes)

    def body(*refs):
        bufs = refs[n:2 * n]
        send, recv, fsend, frecv = refs[2 * n:]
        x, y, c, chips = _place()
        me = 2 * x + y
        sib = (x, y, 1 - c)

        def block(i, chip_index, half):
            sz = wholes[i].shape[axes[i]] // N_CHIPS
            hl = wholes[i].shape[0] // 2
            return _axis_part(bufs[i], axes[i], chip_index * sz, sz).at[pl.ds(half * hl, hl)]

        for i in range(n):
            for k, (px, py) in enumerate(chips):
                _remote(block(i, me, c), block(i, me, c), send.at[i, k], recv.at[i, k], (px, py, c)).start()
        for i in range(n):
            for k, (px, py) in enumerate(chips):
                landed = block(i, 2 * px + py, c)
                _remote(landed, landed, send.at[i, k], recv.at[i, k], (px, py, c)).wait_recv()
                for s_, d_ in _pieces(landed, landed):
                    _remote(s_, d_, fsend.at[i, k], frecv.at[i, k], sib).start()
        for i in range(n):
            for k, (px, py) in enumerate(chips):
                passed = _remote(block(i, 2 * px + py, c), block(i, 2 * px + py, 1 - c), fsend.at[i, k], frecv.at[i, k], sib)
                passed.wait_recv()
                passed.wait_send()
                _remote(block(i, me, c), block(i, me, c), send.at[i, k], recv.at[i, k], (px, py, c)).wait_send()

    sems = pltpu.SemaphoreType.DMA((n, 3))
    return pl.pallas_call(
        body, name=name, in_specs=[ANY] * n, out_specs=tuple([ANY] * n),
        out_shape=tuple(jax.ShapeDtypeStruct(a.shape, a.dtype) for a in wholes),
        input_output_aliases={i: i for i in range(n)},
        scratch_shapes=[sems, sems, sems, sems],
    )(*wholes)


def _own_block_placed(shard, axis, chip):
    shape = list(shard.shape)
    shape[axis] *= N_CHIPS
    return lax.dynamic_update_slice_in_dim(lax.empty(tuple(shape), shard.dtype), shard, chip * shard.shape[axis], axis)


def _split_cores(gs, name):
    n = len(gs)

    def body(*refs):
        ins, got = refs[:n], refs[n:2 * n]
        send, recv = refs[2 * n:]
        x, y, c, _ = _place()
        sib = (x, y, 1 - c)

        def theirs(i):
            hl = gs[i].shape[0] // 2
            return ins[i].at[pl.ds((1 - c) * hl, hl)]

        for i in range(n):
            for s_, d_ in _pieces(theirs(i), got[i]):
                _remote(s_, d_, send.at[i], recv.at[i], sib).start()
        for i in range(n):
            _remote(theirs(i), got[i], send.at[i], recv.at[i], sib).wait()

    return pl.pallas_call(
        body, name=name, in_specs=[ANY] * n, out_specs=tuple([ANY] * n),
        out_shape=tuple(jax.ShapeDtypeStruct((g.shape[0] // 2,) + g.shape[1:], g.dtype) for g in gs),
        scratch_shapes=[pltpu.SemaphoreType.DMA((n,)), pltpu.SemaphoreType.DMA((n,))],
    )(*gs)


def _scatter_chips(pbs, axes, name):
    n = len(pbs)

    def block(p, ax):
        shape = list(p.shape)
        shape[ax] //= N_CHIPS
        return tuple(shape)

    def body(*refs):
        inb, got = refs[:n], refs[n:2 * n]
        send, recv = refs[2 * n:]
        x, y, c, chips = _place()

        def part(i, chip_index):
            sz = pbs[i].shape[axes[i]] // N_CHIPS
            return _axis_part(inb[i], axes[i], chip_index * sz, sz)

        for i in range(n):
            for k, (px, py) in enumerate(chips):
                _remote(part(i, 2 * px + py), got[i].at[k], send.at[i, k], recv.at[i, k], (px, py, c)).start()
        for i in range(n):
            for k, (px, py) in enumerate(chips):
                _remote(part(i, 2 * px + py), got[i].at[k], send.at[i, k], recv.at[i, k], (px, py, c)).wait()

    return pl.pallas_call(
        body, name=name, in_specs=[ANY] * n, out_specs=tuple([ANY] * n),
        out_shape=tuple(jax.ShapeDtypeStruct((3,) + block(p, ax), p.dtype) for p, ax in zip(pbs, axes)),
        scratch_shapes=[pltpu.SemaphoreType.DMA((n, 3)), pltpu.SemaphoreType.DMA((n, 3))],
    )(*pbs)


def _join_cores(bufs, name):
    n = len(bufs)

    def body(*refs):
        outs = refs[n:2 * n]
        send, recv = refs[2 * n:]
        x, y, c, _ = _place()
        sib = (x, y, 1 - c)

        def half(i, which):
            hl = bufs[i].shape[0] // 2
            return outs[i].at[pl.ds(which * hl, hl)]

        for i in range(n):
            for s_, d_ in _pieces(half(i, c), half(i, c)):
                _remote(s_, d_, send.at[i], recv.at[i], sib).start()
        for i in range(n):
            cp = _remote(half(i, c), half(i, 1 - c), send.at[i], recv.at[i], sib)
            cp.wait_recv()
            cp.wait_send()

    return pl.pallas_call(
        body, name=name, in_specs=[ANY] * n, out_specs=tuple([ANY] * n),
        out_shape=tuple(jax.ShapeDtypeStruct(a.shape, a.dtype) for a in bufs),
        input_output_aliases={i: i for i in range(n)},
        scratch_shapes=[pltpu.SemaphoreType.DMA((n,)), pltpu.SemaphoreType.DMA((n,))],
    )(*bufs)


def _all_reduce_small(buf):
    r = buf.shape[0]

    def body(in_ref, out_ref, sib_buf, chip_sum, got, send, recv):
        x, y, c, chips = _place()
        cp = _remote(in_ref, sib_buf, send.at[0], recv.at[0], (x, y, 1 - c))
        cp.start()
        cp.wait()
        chip_sum[...] = in_ref[...] + sib_buf[...]
        cps = [_remote(chip_sum, got.at[k], send.at[1 + k], recv.at[1 + k], (px, py, c)) for k, (px, py) in enumerate(chips)]
        for cp in cps:
            cp.start()
        for cp in cps:
            cp.wait()
        out_ref[...] = (chip_sum[...] + got[0]) + (got[1] + got[2])

    return pl.pallas_call(
        body, name="all_reduce_small", out_shape=jax.ShapeDtypeStruct(buf.shape, F32),
        in_specs=[pl.BlockSpec(memory_space=pltpu.VMEM)], out_specs=pl.BlockSpec(memory_space=pltpu.VMEM),
        scratch_shapes=[pltpu.VMEM((r, 128), F32), pltpu.VMEM((r, 128), F32), pltpu.VMEM((3, r, 128), F32),
                        pltpu.SemaphoreType.DMA((4,)), pltpu.SemaphoreType.DMA((4,))],
        compiler_params=_cparams(None, VMEM_LIMIT),
    )(buf)


ELEMENTWISE_BLOCK_BYTES = 1 << 20


def _view2d(a):
    cols = a.shape[-1] if a.ndim > 1 else 128
    return a.reshape(-1, cols)


def _row_block(rows, cols):
    want = max(8, ELEMENTWISE_BLOCK_BYTES // (4 * cols))
    br = rows
    while br > want and br % 2 == 0 and (br // 2) % 16 == 0:
        br //= 2
    return br


def _core():
    return lax.axis_index("c")


def _chip():
    return 2 * lax.axis_index("x") + lax.axis_index("y")


def _sum_half(g, got, name):
    h = got.shape[0]
    gv = g.reshape(2 * h, -1, g.shape[-1])
    tv = got.reshape(h, -1, g.shape[-1])
    _, rows, cols = tv.shape
    br = _row_block(rows, cols)

    def body(g_ref, t_ref, p_ref, pb_ref):
        p = g_ref[...] + t_ref[...]
        p_ref[...] = p
        pb_ref[...] = p.astype(BF16)

    blk = pl.BlockSpec((1, br, cols), lambda l, i: (l, i, 0))
    p, pb = pl.pallas_call(
        body, name=name, grid=(h, rows // br),
        in_specs=[pl.BlockSpec((1, br, cols), lambda l, i: (_core() * h + l, i, 0)), blk], out_specs=(blk, blk),
        out_shape=(jax.ShapeDtypeStruct(tv.shape, F32), jax.ShapeDtypeStruct(tv.shape, BF16)),
        compiler_params=_cparams(("parallel", "parallel")))(gv, tv)
    return p.reshape(got.shape), pb.reshape(got.shape)


def _sum_blocks(p, got3, axis, name):
    h = p.shape[0]
    blk_shape = got3.shape[1:]
    cols_mode = axis == p.ndim - 1
    pv = p.reshape(-1, p.shape[-2], p.shape[-1])
    tv = got3.reshape(3, -1, blk_shape[-2], blk_shape[-1])
    la, rb, cb = tv.shape[1:]
    assert cols_mode or axis == p.ndim - 2
    br = _row_block(rb, cb)
    per = rb // br

    def body(p_ref, a_ref, b_ref, c_ref, out_ref):
        out_ref[...] = ((p_ref[...] + a_ref[0].astype(F32)) + b_ref[0].astype(F32)) + c_ref[0].astype(F32)

    if cols_mode:
        p_spec = pl.BlockSpec((1, br, cb), lambda l, i: (l, i, _chip()))
    else:
        p_spec = pl.BlockSpec((1, br, cb), lambda l, i: (l, _chip() * per + i, 0))
    out = pl.pallas_call(
        body, name=name, grid=(la, per),
        in_specs=[p_spec] + [pl.BlockSpec((1, 1, br, cb), lambda l, i, k=k: (k, l, i, 0)) for k in range(3)],
        out_specs=pl.BlockSpec((1, br, cb), lambda l, i: (_core() * la + l, i, 0)),
        out_shape=jax.ShapeDtypeStruct((2 * la, rb, cb), F32),
        compiler_params=_cparams(("parallel", "parallel")))(pv, tv, tv, tv)
    return out.reshape((2 * h,) + blk_shape[1:])


def _adamw(w, g, m, v, name):
    shape = w.shape
    ops = [_view2d(a) for a in (w, g, m, v)]
    rows, cols = ops[0].shape
    br = _row_block(rows, cols)
    c1 = 1.0 - ADAM_B1 ** ADAM_STEP
    c2 = 1.0 - ADAM_B2 ** ADAM_STEP

    def body(w_ref, g_ref, m_ref, v_ref, d_ref, nm_ref, nv_ref):
        g_ = g_ref[...]
        nm = ADAM_B1 * m_ref[...] + (1.0 - ADAM_B1) * g_
        nv = ADAM_B2 * v_ref[...] + (1.0 - ADAM_B2) * (g_ * g_)
        d_ref[...] = -ADAM_LR * ((nm / c1) / (jnp.sqrt(nv / c2) + ADAM_EPS) + ADAM_WD * w_ref[...])
        nm_ref[...] = nm
        nv_ref[...] = nv

    blk = pl.BlockSpec((br, cols), lambda i: (i, 0))
    outs = pl.pallas_call(body, name=name, grid=(rows // br,), in_specs=[blk] * 4, out_specs=(blk, blk, blk),
                          out_shape=tuple(jax.ShapeDtypeStruct((rows, cols), F32) for _ in range(3)),
                          compiler_params=_cparams(("parallel",)))(*ops)
    return tuple(o.reshape(shape) for o in outs)


def _pack(arrays):
    flat = jnp.concatenate([a.reshape(-1) for a in arrays])
    pad = (-flat.shape[0]) % (8 * 128)
    return jnp.pad(flat, (0, pad)).reshape(-1, 128)


def _unpack(buf, shapes):
    flat = buf.reshape(-1)
    out, o = [], 0
    for s in shapes:
        n = int(np.prod(s))
        out.append(flat[o:o + n].reshape(s))
        o += n
    return out


WEIGHTS = ["c_ctx", "norm_g", "w_mod", "b_mod", "w_in", "conv_w", "conv_b", "lru_wa", "lru_ba", "lru_wx", "lru_bx",
           "lru_lambda", "attn_sink", "q_norm_g", "k_norm_g", "w_branch", "w_out", "final_g"]
BIG = {"w_mod": 2, "w_in": 2, "w_branch": 2, "w_out": 1}
SMALL_SHARDED = ["conv_w", "lru_ba", "lru_bx", "lru_lambda"]
REPLICATED = [n for n in WEIGHTS if n not in BIG and n not in SMALL_SHARDED]


def kernel(x, c, ctx, c_ctx, norm_g, w_mod, b_mod, w_in, conv_w, conv_b, lru_wa, lru_ba, lru_wx, lru_bx, lru_lambda, attn_sink, q_norm_g, k_norm_g, w_branch, w_out, final_g, loss_target, m_c_ctx, m_norm_g, m_w_mod, m_b_mod, m_w_in, m_conv_w, m_conv_b, m_lru_wa, m_lru_ba, m_lru_wx, m_lru_bx, m_lru_lambda, m_attn_sink, m_q_norm_g, m_k_norm_g, m_w_branch, m_w_out, m_final_g, v_c_ctx, v_norm_g, v_w_mod, v_b_mod, v_w_in, v_conv_w, v_conv_b, v_lru_wa, v_lru_ba, v_lru_wx, v_lru_bx, v_lru_lambda, v_attn_sink, v_q_norm_g, v_k_norm_g, v_w_branch, v_w_out, v_final_g):
    args = dict(locals())
    w = {n: args[n] for n in WEIGHTS}
    mom = {n: args["m_" + n] for n in WEIGHTS}
    var = {n: args["v_" + n] for n in WEIGHTS}
    depth, d = norm_g.shape
    chip = 2 * lax.axis_index("x") + lax.axis_index("y")

    big_names = list(BIG)
    small_shard = jnp.concatenate([w[n] for n in SMALL_SHARDED], axis=1)
    gather_axes = [BIG[n] for n in big_names] + [2]
    placed = [_own_block_placed(a, ax, chip)
              for a, ax in zip([w[n].astype(BF16) for n in big_names] + [small_shard], gather_axes)]
    gathered = _gather_chips(placed, gather_axes, "gather_weights")
    whole = dict(w)
    whole.update(dict(zip(big_names, gathered[:-1])))
    o = 0
    for n in SMALL_SHARDED:
        rows = w[n].shape[1]
        whole[n] = gathered[-1][:, o:o + rows]
        o += rows
    layers = [_layer_params(li, whole) for li in range(depth)]

    loss_local, grad_x, g_c_ctx, g_final, lgrads = _local_step(x, c, ctx, loss_target, c_ctx, final_g, layers)
    loss = lax.psum(loss_local, ("x", "y", "c"))
    full = {n: jnp.stack([lg[n] for lg in lgrads]) for n in lgrads[0]}
    full["w_in"] = _reorder_in_cols(full["w_in"], d, inverse=True)
    full["c_ctx"], full["final_g"] = g_c_ctx, g_final

    bigs = [full[n] for n in big_names]
    got = _split_cores(bigs, "grad_split_cores")
    parts = [_sum_half(g, t_, "grad_chip_sum") for g, t_ in zip(bigs, got)]
    recv = _scatter_chips([pb for _, pb in parts], [BIG[n] for n in big_names], "grad_scatter_chips")
    totals = [_sum_blocks(p_, r, BIG[n], "grad_total") for (p_, _), r, n in zip(parts, recv, big_names)]
    grad = dict(zip(big_names, _join_cores(totals, "grad_join_cores")))

    small_names = REPLICATED + SMALL_SHARDED
    reduced = _unpack(_all_reduce_small(_pack([full[n] for n in small_names])), [full[n].shape for n in small_names])
    for n, g in zip(small_names, reduced):
        if n in SMALL_SHARDED:
            sz = w[n].shape[-1]
            g = lax.dynamic_slice_in_dim(g, chip * sz, sz, axis=g.ndim - 1)
        grad[n] = g

    delta, new_m, new_v = {}, {}, {}
    for n in big_names:
        delta[n], new_m[n], new_v[n] = _adamw(w[n], grad[n], mom[n], var[n], "adamw_" + n)
    shapes = [w[n].shape for n in small_names]
    packed = _adamw(_pack([w[n] for n in small_names]), _pack([grad[n] for n in small_names]),
                    _pack([mom[n] for n in small_names]), _pack([var[n] for n in small_names]), "adamw_small")
    for res, p in zip((delta, new_m, new_v), packed):
        res.update(dict(zip(small_names, _unpack(p, shapes))))

    return (loss, grad_x, *[grad[n] for n in WEIGHTS], *[delta[n] for n in WEIGHTS],
            *[new_m[n] for n in WEIGHTS], *[new_v[n] for n in WEIGHTS])
```

```python
import functools

import jax
import jax.numpy as jnp
import numpy as np
from jax import lax
from jax.experimental import pallas as pl
from jax.experimental.pallas import tpu as pltpu

F32 = jnp.float32
BF16 = jnp.bfloat16

HEAD_DIM = 128
GROUP = 4
LRU_BLOCK_W = 64
LRU_C = 8.0
WINDOW = 128
GRID_W = 64
ROPE_THETA = 10000.0
EPS = 1e-6
NEG_INF = -1e30
ADAM_LR, ADAM_B1, ADAM_B2, ADAM_EPS, ADAM_WD, ADAM_STEP = 0.001, 0.9, 0.999, 1e-08, 0.01, 10

ROW_BLOCK = 256
LRU_LANES = 128
LRU_CHUNK = 128
LRU_UNROLL = 2
DENSE_FWD_BQ = 256
LOG2E = 1.4426950408889634
ATT_BQ = 256
WIN_BQ = 256
WIN_SPAN = WIN_BQ + 2 * WINDOW
MOD_ROWS = 16
VMEM_LIMIT = 56 * 1024 * 1024

MESH = pl.DeviceIdType.MESH


def _cparams(sem=None, vmem=None):
    kw = {}
    if sem is not None:
        kw["dimension_semantics"] = sem
    if vmem is not None:
        kw["vmem_limit_bytes"] = vmem
    return pltpu.CompilerParams(**kw)


def _sigmoid(v):
    return 1.0 / (1.0 + jnp.exp(-v))


def _silu(v):
    return v * _sigmoid(v)


def _dsilu(v):
    s = _sigmoid(v)
    return s * (1.0 + v * (1.0 - s))


def _one_minus_square(log_a, a):
    z2 = log_a * log_a
    series = (-2.0 * a * log_a) * (1.0 + z2 * (1.0 / 6 + z2 * (1.0 / 120 + z2 * (1.0 / 5040))))
    return jnp.where(z2 < 0.25, series, 1.0 - a * a)


def _log1p(y):
    u = 1.0 + y
    d = u - 1.0
    return jnp.where(d == 0.0, y, jnp.log(u) * (y / jnp.where(d == 0.0, 1.0, d)))


def _softplus(x):
    return jnp.maximum(x, 0.0) + _log1p(jnp.exp(-jnp.abs(x)))


def _dot(a, b):
    return jnp.dot(a, b, preferred_element_type=F32)


def _dot_nt(a, b):
    return lax.dot_general(a, b, (((1,), (1,)), ((), ())), preferred_element_type=F32)


def _dot_tn(a, b):
    return lax.dot_general(a, b, (((0,), (0,)), ((), ())), preferred_element_type=F32)


def _colsum(v):
    return jnp.sum(v, axis=0, keepdims=True)


def _layout(d_model):
    kvw = (d_model // HEAD_DIM // GROUP) * HEAD_DIM
    names = ["gA", "gB", "gC", "mA", "mB", "mC", "uA", "qB", "qC", "kB", "vB", "kC", "vC"]
    widths = [d_model] * 9 + [kvw] * 4
    off, o = {}, 0
    for n, w in zip(names, widths):
        off[n] = o
        o += w
    return off, o


def _orig_segments(d_model):
    kvw = (d_model // HEAD_DIM // GROUP) * HEAD_DIM
    names = ["uA", "gA", "qB", "kB", "vB", "gB", "qC", "kC", "vC", "gC", "mA", "mB", "mC"]
    widths = [d_model, d_model, d_model, kvw, kvw, d_model, d_model, kvw, kvw, d_model, d_model, d_model, d_model]
    out, o = [], 0
    for n, w in zip(names, widths):
        out.append((n, o, w))
        o += w
    return out


def _matmul(a, b, *, ta=False, tb=False, out_dtype=F32, bm, bn, bk, name, n_outer=False, a_lead=None, b_lead=None):
    a_shape = a.shape if a_lead is None else a.shape[1:]
    b_shape = b.shape if b_lead is None else b.shape[1:]
    (kdim, m) = a_shape if ta else a_shape[::-1]
    (n, kdim2) = b_shape if tb else b_shape[::-1]
    assert kdim == kdim2 and m % bm == 0 and n % bn == 0 and kdim % bk == 0, (a.shape, b.shape, bm, bn, bk)
    nk = kdim // bk
    dims = (((0 if ta else 1,), (1 if tb else 0,)), ((), ()))

    def ij(f):
        return (lambda j, i, k: f(i, j, k)) if n_outer else f

    def body(a_ref, b_ref, o_ref, *scratch):
        r = lax.dot_general(a_ref[...].astype(BF16), b_ref[...].astype(BF16), dims, preferred_element_type=F32)
        if nk == 1:
            o_ref[...] = r.astype(out_dtype)
        else:
            acc = scratch[0]
            k = pl.program_id(2)

            @pl.when(k == 0)
            def _():
                acc[...] = r

            @pl.when(k > 0)
            def _():
                acc[...] += r

            @pl.when(k == nk - 1)
            def _():
                o_ref[...] = acc[...].astype(out_dtype)

    def spec(shape, f, lead):
        f = ij(f)
        if lead is None:
            return pl.BlockSpec(shape, f)
        return pl.BlockSpec((None,) + shape, lambda *g: (lead,) + f(*g))

    a_spec = spec((bk, bm), lambda i, j, k: (k, i), a_lead) if ta else spec((bm, bk), lambda i, j, k: (i, k), a_lead)
    b_spec = spec((bn, bk), lambda i, j, k: (j, k), b_lead) if tb else spec((bk, bn), lambda i, j, k: (k, j), b_lead)
    return pl.pallas_call(
        body, name=name, grid=(n // bn, m // bm, nk) if n_outer else (m // bm, n // bn, nk),
        in_specs=[a_spec, b_spec], out_specs=pl.BlockSpec((bm, bn), ij(lambda i, j, k: (i, j))),
        out_shape=jax.ShapeDtypeStruct((m, n), out_dtype),
        scratch_shapes=[pltpu.VMEM((bm, bn), F32)] if nk > 1 else [],
        compiler_params=_cparams(("parallel", "parallel", "arbitrary"), VMEM_LIMIT),
    )(a, b)


def _mod_fwd(c16, w_mod, b_mod):
    d3 = w_mod.shape[1]

    def body(c_ref, w_ref, b_ref, o_ref):
        o_ref[...] = _dot(_silu(c_ref[...]).astype(BF16), w_ref[...]) + b_ref[...]

    return pl.pallas_call(body, name="mod_fwd", out_shape=jax.ShapeDtypeStruct((MOD_ROWS, d3), F32),
                          compiler_params=_cparams(None, VMEM_LIMIT))(c16, w_mod, b_mod)


def _mod_bwd(c16, dmod16, w_mod):
    d, d3 = w_mod.shape

    def body(c_ref, g_ref, w_ref, dw_ref, db_ref, dc_ref):
        c = c_ref[...]
        g = g_ref[...]
        gb = g.astype(BF16)
        dw_ref[...] = _dot_tn(_silu(c).astype(BF16), gb)
        db_ref[...] = _colsum(g)
        dc_ref[...] = _dot_nt(gb, w_ref[...]) * _dsilu(c)

    return pl.pallas_call(
        body, name="mod_bwd",
        out_shape=(jax.ShapeDtypeStruct((d, d3), F32), jax.ShapeDtypeStruct((1, d3), F32),
                   jax.ShapeDtypeStruct((MOD_ROWS, d), F32)),
        compiler_params=_cparams(None, VMEM_LIMIT))(c16, dmod16, w_mod)


def _row_kind(t, lb):
    return jnp.where(t >= lb, 1, 0)


def _norm_mod_fwd(x3, g, modsel, ctx_len):
    b, t, d = x3.shape
    bt = ROW_BLOCK
    lb = ctx_len // bt

    def body(x_ref, g_ref, m_ref, h_ref):
        x = x_ref[0]
        rstd = lax.rsqrt(jnp.mean(x * x, axis=-1, keepdims=True) + EPS)
        y = x * rstd * g_ref[...]
        h_ref[0] = (y * (1.0 + m_ref[0, 0, 1:2, :]) + m_ref[0, 0, 0:1, :]).astype(BF16)

    return pl.pallas_call(
        body, name="norm_mod_fwd", grid=(b, t // bt),
        in_specs=[pl.BlockSpec((1, bt, d), lambda i, j: (i, j, 0)),
                  pl.BlockSpec((1, d), lambda i, j: (0, 0)),
                  pl.BlockSpec((1, 1, 8, d), lambda i, j: (i, _row_kind(j, lb), 0, 0))],
        out_specs=pl.BlockSpec((1, bt, d), lambda i, j: (i, j, 0)),
        out_shape=jax.ShapeDtypeStruct((b, t, d), BF16),
        compiler_params=_cparams(("parallel", "arbitrary")),
    )(x3, g, modsel)


def _norm_mod_bwd(dh3, x3, g, modsel, dres3, ctx_len):
    b, t, d = x3.shape
    bt = ROW_BLOCK
    lb = ctx_len // bt

    def body(dh_ref, x_ref, g_ref, m_ref, dres_ref, dx_ref, acc_ref):
        j = pl.program_id(1)
        x = x_ref[0]
        dh = dh_ref[0]
        g_row = g_ref[...]
        rstd = lax.rsqrt(jnp.mean(x * x, axis=-1, keepdims=True) + EPS)
        xhat = x * rstd
        dhpre = dh * (1.0 + m_ref[0, 0, 1:2, :])
        dxhat = dhpre * g_row
        dx = rstd * (dxhat - xhat * jnp.mean(dxhat * xhat, axis=-1, keepdims=True))
        dx_ref[0] = dx + dres_ref[0]

        @pl.when((j == 0) | (j == lb))
        def _():
            acc_ref[...] = jnp.zeros_like(acc_ref)

        acc_ref[0, 0, 0:1, :] += _colsum(dh)
        acc_ref[0, 0, 1:2, :] += _colsum(dh * (xhat * g_row))
        acc_ref[0, 0, 2:3, :] += _colsum(dhpre * xhat)

    blk = pl.BlockSpec((1, bt, d), lambda i, j: (i, j, 0))
    return pl.pallas_call(
        body, name="norm_mod_bwd", grid=(b, t // bt),
        in_specs=[blk, blk, pl.BlockSpec((1, d), lambda i, j: (0, 0)),
                  pl.BlockSpec((1, 1, 8, d), lambda i, j: (i, _row_kind(j, lb), 0, 0)), blk],
        out_specs=(blk, pl.BlockSpec((1, 1, 8, d), lambda i, j: (i, _row_kind(j, lb), 0, 0))),
        out_shape=(jax.ShapeDtypeStruct((b, t, d), F32), jax.ShapeDtypeStruct((b, 2, 8, d), F32)),
        compiler_params=_cparams(("parallel", "arbitrary")),
    )(dh3, x3, g, modsel, dres3)


def _shifted_rows(ref, c, off, ctx_len, total):
    ct = LRU_CHUNK
    r0 = pl.multiple_of(c * ct, ct)
    x0 = ref[pl.ds(r0, ct), :]
    row = lax.broadcasted_iota(jnp.int32, x0.shape, 0)
    if off < 0:
        k = -off
        has = jnp.logical_and(r0 != 0, r0 != ctx_len)
        p0 = pl.multiple_of(jnp.maximum(r0 - 8, 0), 8)
        edge = jnp.where(has, ref[pl.ds(p0, 8), :], 0.0)
        out = pltpu.roll(x0, k, 0)
        for j in range(k):
            out = jnp.where(row == j, edge[8 - k + j:8 - k + j + 1, :], out)
    else:
        k = off
        has = jnp.logical_and(r0 + ct != ctx_len, r0 + ct != total)
        n0 = pl.multiple_of(jnp.minimum(r0 + ct, total - 8), 8)
        edge = jnp.where(has, ref[pl.ds(n0, 8), :], 0.0)
        out = pltpu.roll(x0, ct - k, 0)
        for j in range(k):
            out = jnp.where(row == ct - k + j, edge[j:j + 1, :], out)
    return out


def _chunk_scan(a, b, reverse):
    n = a.shape[0]
    row = lax.broadcasted_iota(jnp.int32, a.shape, 0)
    s = 1
    while s < n:
        if reverse:
            a_s, b_s, ok = pltpu.roll(a, n - s, 0), pltpu.roll(b, n - s, 0), row < n - s
        else:
            a_s, b_s, ok = pltpu.roll(a, s, 0), pltpu.roll(b, s, 0), row >= s
        b = jnp.where(ok, a * b_s + b, b)
        a = jnp.where(ok, a * a_s, a)
        s *= 2
    return a, b


def _loop_chunks(n, body, init):
    assert n % LRU_UNROLL == 0

    def group(s2, carry):
        for u in range(LRU_UNROLL):
            carry = body(LRU_UNROLL * s2 + u, carry)
        return carry

    return lax.fori_loop(0, n // LRU_UNROLL, group, init)


def _lru_order(d, s, n_ctx, n_all):
    if d == 0:
        return s
    return jnp.where(s < n_ctx, n_ctx - 1 - s, n_all - 1 - (s - n_ctx))


def _lru_gates(u, wa, ba, wx, bx, sp):
    ub = u.astype(BF16)
    r = _sigmoid(_dot(ub, wa) + ba)
    i = _sigmoid(_dot(ub, wx) + bx)
    log_a = (-LRU_C * sp) * r
    a = jnp.exp(log_a)
    sf = jnp.sqrt(_one_minus_square(log_a, a))
    return ub, r, i, a, sf


def _lru_specs(t, n_lane_blocks_offset):
    ln = LRU_LANES
    return [
        pl.BlockSpec((4, ln), lambda i, j: (0, j)),
        pl.BlockSpec((1, ln), lambda i, j: (0, j)),
        pl.BlockSpec((2, 1, ln, ln), lambda i, j: (0, j, 0, 0)),
        pl.BlockSpec((2, ln), lambda i, j: (0, j)),
        pl.BlockSpec((2, 1, ln, ln), lambda i, j: (0, j, 0, 0)),
        pl.BlockSpec((2, ln), lambda i, j: (0, j)),
        pl.BlockSpec((2, ln), lambda i, j: (0, j)),
    ]


def _lru_conv(ua_ref, cw_ref, cb_ref, u_s, ctx_len, total):
    ct = LRU_CHUNK

    def conv(c, _):
        r0 = pl.multiple_of(c * ct, ct)
        u = (cw_ref[0:1, :] * _shifted_rows(ua_ref, c, -2, ctx_len, total)
             + cw_ref[1:2, :] * _shifted_rows(ua_ref, c, -1, ctx_len, total)
             + cw_ref[2:3, :] * ua_ref[pl.ds(r0, ct), :]
             + cw_ref[3:4, :] * _shifted_rows(ua_ref, c, 1, ctx_len, total) + cb_ref[...])
        u_s[pl.ds(r0, ct), :] = u
        return 0

    lax.fori_loop(0, total // ct, conv, 0)


def _lru_fwd(proj3, col0, conv_w, conv_b, wa_bd, ba, wx_bd, bx, lam, ctx_len):
    b, t, _ = proj3.shape
    d = conv_w.shape[1]
    ln, ct = LRU_LANES, LRU_CHUNK
    n_all, n_ctx = t // ct, ctx_len // ct
    cb0 = col0 // ln

    def body(ua_ref, cw_ref, cb_ref, wa_ref, ba_ref, wx_ref, bx_ref, lam_ref, y_ref, u_s):
        ua = ua_ref.at[0]
        _lru_conv(ua, cw_ref, cb_ref, u_s, ctx_len, t)
        for dr in (0, 1):
            sp = _softplus(-lam_ref[dr:dr + 1, :])
            wa, wx = wa_ref[dr, 0], wx_ref[dr, 0]
            ba_row, bx_row = ba_ref[dr:dr + 1, :], bx_ref[dr:dr + 1, :]

            def step(s, carry, dr=dr, sp=sp, wa=wa, wx=wx, ba_row=ba_row, bx_row=bx_row):
                c = _lru_order(dr, s, n_ctx, n_all)
                r0 = pl.multiple_of(c * ct, ct)
                u = u_s[pl.ds(r0, ct), :]
                _, _, i, a, sf = _lru_gates(u, wa, ba_row, wx, bx_row, sp)
                aa, h0 = _chunk_scan(a, sf * (i * u), reverse=(dr == 1))
                h = h0 + aa * carry
                if dr == 0:
                    y_ref[0, pl.ds(r0, ct), :] = h
                    return h[ct - 1:ct, :]
                y_ref[0, pl.ds(r0, ct), :] += h
                return h[0:1, :]

            _loop_chunks(n_all, step, jnp.zeros((1, ln), F32))

    return pl.pallas_call(
        body, name="lru_fwd", grid=(b, d // ln),
        in_specs=[pl.BlockSpec((1, t, ln), lambda i, j: (i, 0, cb0 + j))] + _lru_specs(t, cb0),
        out_specs=pl.BlockSpec((1, t, ln), lambda i, j: (i, 0, j)),
        out_shape=jax.ShapeDtypeStruct((b, t, d), F32),
        scratch_shapes=[pltpu.VMEM((t, ln), F32)],
        compiler_params=_cparams(("parallel", "parallel"), VMEM_LIMIT),
    )(proj3, conv_w, conv_b, wa_bd, ba, wx_bd, bx, lam)


def _lru_bwd(dproj3, proj3, col0, dy3, conv_w, conv_b, wa_bd, ba, wx_bd, bx, lam, ctx_len):
    b, t, _ = proj3.shape
    d = conv_w.shape[1]
    ln, ct = LRU_LANES, LRU_CHUNK
    n_all, n_ctx = t // ct, ctx_len // ct
    cb0 = col0 // ln

    def body(dproj_hbm, ua_ref, dy_ref, cw_ref, cb_ref, wa_ref, ba_ref, wx_ref, bx_ref, lam_ref,
             dua_ref, vec_ref, dwa_ref, dwx_ref, u_s, h_s, du_s, a_s, sf_s, i_s, r_s):
        del dproj_hbm
        ua = ua_ref.at[0]
        _lru_conv(ua, cw_ref, cb_ref, u_s, ctx_len, t)
        du_s[...] = jnp.zeros_like(du_s)
        vec_ref[...] = jnp.zeros_like(vec_ref)
        for dr in (0, 1):
            sp = _softplus(-lam_ref[dr:dr + 1, :])
            wa, wx = wa_ref[dr, 0], wx_ref[dr, 0]
            ba_row, bx_row = ba_ref[dr:dr + 1, :], bx_ref[dr:dr + 1, :]

            def fwd(s, carry, dr=dr, sp=sp, wa=wa, wx=wx, ba_row=ba_row, bx_row=bx_row):
                c = _lru_order(dr, s, n_ctx, n_all)
                r0 = pl.multiple_of(c * ct, ct)
                u = u_s[pl.ds(r0, ct), :]
                _, r, i, a, sf = _lru_gates(u, wa, ba_row, wx, bx_row, sp)
                aa, h0 = _chunk_scan(a, sf * (i * u), reverse=(dr == 1))
                h = h0 + aa * carry
                h_s[pl.ds(r0, ct), :] = h
                a_s[pl.ds(r0, ct), :] = a
                sf_s[pl.ds(r0, ct), :] = sf
                i_s[pl.ds(r0, ct), :] = i
                r_s[pl.ds(r0, ct), :] = r
                return h[ct - 1:ct, :] if dr == 0 else h[0:1, :]

            _loop_chunks(n_all, fwd, jnp.zeros((1, ln), F32))
            dwa_ref[0, dr, 0] = jnp.zeros((ln, ln), F32)
            dwx_ref[0, dr, 0] = jnp.zeros((ln, ln), F32)

            def bwd(sr, carry, dr=dr, sp=sp, wa=wa, wx=wx):
                gc, vacc = carry
                c = _lru_order(dr, n_all - 1 - sr, n_ctx, n_all)
                r0 = pl.multiple_of(c * ct, ct)
                u = u_s[pl.ds(r0, ct), :]
                h = h_s[pl.ds(r0, ct), :]
                dy = dy_ref[0, pl.ds(r0, ct), :]
                ub = u.astype(BF16)
                r, i, a, sf = r_s[pl.ds(r0, ct), :], i_s[pl.ds(r0, ct), :], a_s[pl.ds(r0, ct), :], sf_s[pl.ds(r0, ct), :]
                row = lax.broadcasted_iota(jnp.int32, a.shape, 0)
                if dr == 0:
                    alpha = jnp.where(row == ct - 1, 1.0, pltpu.roll(a, ct - 1, 0))
                    aa, g0 = _chunk_scan(alpha, dy, reverse=True)
                    g = g0 + aa * gc
                    gc_new = a[0:1, :] * g[0:1, :]
                    p0 = pl.multiple_of(jnp.maximum(r0 - 8, 0), 8)
                    edge = jnp.where(r0 != 0, h_s[pl.ds(p0, 8), :], 0.0)[7:8, :]
                    h_prev = jnp.where(row == 0, edge, pltpu.roll(h, 1, 0))
                else:
                    alpha = jnp.where(row == 0, 1.0, pltpu.roll(a, 1, 0))
                    aa, g0 = _chunk_scan(alpha, dy, reverse=False)
                    g = g0 + aa * gc
                    gc_new = a[ct - 1:ct, :] * g[ct - 1:ct, :]
                    r_end = r0 + ct
                    n0 = pl.multiple_of(jnp.where(r_end == t, 0, jnp.minimum(r_end, t - 8)), 8)
                    edge = jnp.where(r_end != ctx_len, h_s[pl.ds(n0, 8), :], 0.0)[0:1, :]
                    h_prev = jnp.where(row == ct - 1, edge, pltpu.roll(h, ct - 1, 0))
                da = g * h_prev
                iu = i * u
                diu = g * sf
                dlog_a = da * a - (g * iu) * (a * a) / sf
                dpre_r = (dlog_a * (-LRU_C * sp)) * (r * (1.0 - r))
                dpre_i = (diu * u) * (i * (1.0 - i))
                dpr_b, dpi_b = dpre_r.astype(BF16), dpre_i.astype(BF16)
                du = diu * i + _dot_nt(dpr_b, wa) + _dot_nt(dpi_b, wx)
                du_s[pl.ds(r0, ct), :] += du
                dwa_ref[0, dr, 0] += _dot_tn(ub, dpr_b)
                dwx_ref[0, dr, 0] += _dot_tn(ub, dpi_b)
                vacc = (vacc[0] + _colsum(dpre_r), vacc[1] + _colsum(dpre_i), vacc[2] + _colsum(dlog_a * (-LRU_C * r)))
                return gc_new, vacc

            zrow = jnp.zeros((1, ln), F32)
            _, vacc = _loop_chunks(n_all, bwd, (zrow, (zrow, zrow, zrow)))
            vec_ref[0, 5 + dr:6 + dr, :] = vacc[0]
            vec_ref[0, 7 + dr:8 + dr, :] = vacc[1]
            vec_ref[0, 9 + dr:10 + dr, :] = vacc[2]

        def conv_bwd(c, acc):
            r0 = pl.multiple_of(c * ct, ct)
            du = du_s[pl.ds(r0, ct), :]
            dua = (cw_ref[0:1, :] * _shifted_rows(du_s, c, 2, ctx_len, t)
                   + cw_ref[1:2, :] * _shifted_rows(du_s, c, 1, ctx_len, t)
                   + cw_ref[2:3, :] * du
                   + cw_ref[3:4, :] * _shifted_rows(du_s, c, -1, ctx_len, t))
            dua_ref[0, pl.ds(r0, ct), :] = dua.astype(BF16)
            return (acc[0] + _colsum(du * _shifted_rows(ua, c, -2, ctx_len, t)),
                    acc[1] + _colsum(du * _shifted_rows(ua, c, -1, ctx_len, t)),
                    acc[2] + _colsum(du * ua[pl.ds(r0, ct), :]),
                    acc[3] + _colsum(du * _shifted_rows(ua, c, 1, ctx_len, t)),
                    acc[4] + _colsum(du))

        zrow = jnp.zeros((1, ln), F32)
        acc = lax.fori_loop(0, n_all, conv_bwd, (zrow,) * 5)
        for k in range(5):
            vec_ref[0, k:k + 1, :] = acc[k]

    ng = d // ln
    return pl.pallas_call(
        body, name="lru_bwd", grid=(b, ng),
        in_specs=[pl.BlockSpec(memory_space=pl.ANY),
                  pl.BlockSpec((1, t, ln), lambda i, j: (i, 0, cb0 + j)),
                  pl.BlockSpec((1, t, ln), lambda i, j: (i, 0, j))] + _lru_specs(t, cb0),
        out_specs=(pl.BlockSpec((1, t, ln), lambda i, j: (i, 0, cb0 + j)),
                   pl.BlockSpec((1, 16, ln), lambda i, j: (i, 0, j)),
                   pl.BlockSpec((1, 2, 1, ln, ln), lambda i, j: (i, 0, j, 0, 0)),
                   pl.BlockSpec((1, 2, 1, ln, ln), lambda i, j: (i, 0, j, 0, 0))),
        out_shape=(jax.ShapeDtypeStruct(dproj3.shape, dproj3.dtype),
                   jax.ShapeDtypeStruct((b, 16, d), F32),
                   jax.ShapeDtypeStruct((b, 2, ng, ln, ln), F32),
                   jax.ShapeDtypeStruct((b, 2, ng, ln, ln), F32)),
        scratch_shapes=[pltpu.VMEM((t, ln), F32)] * 7,
        input_output_aliases={0: 0},
        compiler_params=_cparams(("parallel", "parallel"), VMEM_LIMIT),
    )(dproj3, proj3, dy3, conv_w, conv_b, wa_bd, ba, wx_bd, bx, lam)


def _rope_tables(ctx_len, seq):
    p = HEAD_DIM // 4
    inv = ROPE_THETA ** (-jnp.arange(p, dtype=F32) / p)
    tok = jnp.arange(seq)
    ang_r = (tok // GRID_W)[:, None] * inv
    ang_c = (tok % GRID_W)[:, None] * inv
    cos = jnp.concatenate([jnp.cos(ang_r)] * 2 + [jnp.cos(ang_c)] * 2, axis=1)
    sin = jnp.concatenate([-jnp.sin(ang_r), jnp.sin(ang_r), -jnp.sin(ang_c), jnp.sin(ang_c)], axis=1)
    cos = jnp.concatenate([jnp.ones((ctx_len, HEAD_DIM), F32), cos], axis=0)
    sin = jnp.concatenate([jnp.zeros((ctx_len, HEAD_DIM), F32), sin], axis=0)
    return cos, sin


def _swap_halves(v):
    lane = lax.broadcasted_iota(jnp.int32, v.shape, 1)
    return jnp.where((lane & 63) < 32, pltpu.roll(v, 96, 1), pltpu.roll(v, 32, 1))


def _head_rstd(v):
    return lax.rsqrt(jnp.mean(v * v, axis=-1, keepdims=True) + EPS)


QKV_BLOCK = GROUP * HEAD_DIM
PREP_ROWS = (2176, 256)


def _prep_fwd(proj3, qcol, kvcol, d, cos, sin, gq, gk, use_norm):
    b, t, _ = proj3.shape
    bt, wb = _pick(t, PREP_ROWS), QKV_BLOCK
    nqb = d // wb
    assert qcol % wb == 0 and kvcol % wb == 0 and d // HEAD_DIM // GROUP == 2
    qb0, kvb = qcol // wb, kvcol // wb

    def body(p_ref, cos_ref, sin_ref, gq_ref, gk_ref, o_ref):
        s = pl.program_id(2)
        c, sn = cos_ref[...], sin_ref[...]

        def rope(v):
            return v * c + _swap_halves(v) * sn

        @pl.when(s < nqb)
        def _():
            for hh in range(GROUP):
                v = p_ref[0, :, hh * HEAD_DIM:(hh + 1) * HEAD_DIM]
                if use_norm:
                    v = v * _head_rstd(v) * gq_ref[...]
                o_ref[0, :, hh * HEAD_DIM:(hh + 1) * HEAD_DIM] = rope(v).astype(BF16)

        @pl.when(s == nqb)
        def _():
            for hh in range(2):
                v = p_ref[0, :, hh * HEAD_DIM:(hh + 1) * HEAD_DIM]
                if use_norm:
                    v = v * _head_rstd(v) * gk_ref[...]
                o_ref[0, :, hh * HEAD_DIM:(hh + 1) * HEAD_DIM] = rope(v).astype(BF16)
            o_ref[0, :, 2 * HEAD_DIM:] = p_ref[0, :, 2 * HEAD_DIM:].astype(BF16)

    return pl.pallas_call(
        body, name="prep_fwd_norm" if use_norm else "prep_fwd", grid=(b, t // bt, nqb + 1),
        in_specs=[pl.BlockSpec((1, bt, wb), lambda i, j, s: (i, j, jnp.where(s < nqb, qb0 + s, kvb))),
                  pl.BlockSpec((bt, HEAD_DIM), lambda i, j, s: (j, 0)),
                  pl.BlockSpec((bt, HEAD_DIM), lambda i, j, s: (j, 0)),
                  pl.BlockSpec((1, HEAD_DIM), lambda i, j, s: (0, 0)),
                  pl.BlockSpec((1, HEAD_DIM), lambda i, j, s: (0, 0))],
        out_specs=pl.BlockSpec((1, bt, wb), lambda i, j, s: (i, j, s)),
        out_shape=jax.ShapeDtypeStruct((b, t, d + wb), BF16),
        compiler_params=_cparams(("parallel", "parallel", "arbitrary"), VMEM_LIMIT),
    )(proj3, cos, sin, gq, gk)


def _prep_bwd(dproj3, dq3, dkt, dvt, proj3, qcol, kvcol, d, cos, sin, gq, gk, use_norm):
    b, t, _ = proj3.shape
    bt, wb = _pick(t, PREP_ROWS), QKV_BLOCK
    nqb = d // wb
    qb0, kvb = qcol // wb, kvcol // wb
    kvh = dkt.shape[1]

    def body(dproj_hbm, dq_ref, dkt_ref, dvt_ref, p_ref, cos_ref, sin_ref, gq_ref, gk_ref, o_ref, gacc_ref):
        del dproj_hbm
        j, s = pl.program_id(1), pl.program_id(2)
        c, sn = cos_ref[...], sin_ref[...]

        @pl.when((j == 0) & (s == 0))
        def _():
            gacc_ref[...] = jnp.zeros_like(gacc_ref)

        def unrope(dv):
            return dv * c + _swap_halves(dv * sn)

        def head_bwd(dyv, xv, g_ref, acc_row):
            dyv = unrope(dyv)
            if not use_norm:
                return dyv
            rstd = _head_rstd(xv)
            xhat = xv * rstd
            gacc_ref[0, acc_row:acc_row + 1, :] += _colsum(dyv * xhat)
            dxhat = dyv * g_ref[...]
            return rstd * (dxhat - xhat * jnp.mean(dxhat * xhat, axis=-1, keepdims=True))

        @pl.when(s < nqb)
        def _():
            for hh in range(GROUP):
                sl = slice(hh * HEAD_DIM, (hh + 1) * HEAD_DIM)
                o_ref[0, :, sl] = head_bwd(dq_ref[0, :, sl], p_ref[0, :, sl], gq_ref, 0).astype(BF16)

        @pl.when(s == nqb)
        def _():
            for hh in range(kvh):
                sl = slice(hh * HEAD_DIM, (hh + 1) * HEAD_DIM)
                o_ref[0, :, sl] = head_bwd(dkt_ref[0, hh].T, p_ref[0, :, sl], gk_ref, 1).astype(BF16)
                sv = slice((kvh + hh) * HEAD_DIM, (kvh + hh + 1) * HEAD_DIM)
                o_ref[0, :, sv] = dvt_ref[0, hh].T.astype(BF16)

    col = lambda i, j, s: (i, j, jnp.where(s < nqb, qb0 + s, kvb))
    return pl.pallas_call(
        body, name="prep_bwd_norm" if use_norm else "prep_bwd", grid=(b, t // bt, nqb + 1),
        in_specs=[pl.BlockSpec(memory_space=pl.ANY),
                  pl.BlockSpec((1, bt, wb), lambda i, j, s: (i, j, jnp.minimum(s, nqb - 1))),
                  pl.BlockSpec((1, kvh, HEAD_DIM, bt), lambda i, j, s: (i, 0, 0, j)),
                  pl.BlockSpec((1, kvh, HEAD_DIM, bt), lambda i, j, s: (i, 0, 0, j)),
                  pl.BlockSpec((1, bt, wb), col),
                  pl.BlockSpec((bt, HEAD_DIM), lambda i, j, s: (j, 0)),
                  pl.BlockSpec((bt, HEAD_DIM), lambda i, j, s: (j, 0)),
                  pl.BlockSpec((1, HEAD_DIM), lambda i, j, s: (0, 0)),
                  pl.BlockSpec((1, HEAD_DIM), lambda i, j, s: (0, 0))],
        out_specs=(pl.BlockSpec((1, bt, wb), col), pl.BlockSpec((1, 8, HEAD_DIM), lambda i, j, s: (i, 0, 0))),
        out_shape=(jax.ShapeDtypeStruct(dproj3.shape, dproj3.dtype), jax.ShapeDtypeStruct((b, 8, HEAD_DIM), F32)),
        input_output_aliases={0: 0},
        compiler_params=_cparams(("parallel", "arbitrary", "arbitrary"), VMEM_LIMIT),
    )(dproj3, dq3, dkt, dvt, proj3, cos, sin, gq, gk)


def _stack_heads(ref, dtype=None):
    parts = [ref[0, :, g * HEAD_DIM:(g + 1) * HEAD_DIM] for g in range(GROUP)]
    v = jnp.concatenate(parts, axis=0)
    return v if dtype is None else v.astype(dtype)


def _unstack_heads(ref, v, bq):
    for g in range(GROUP):
        ref[0, :, g * HEAD_DIM:(g + 1) * HEAD_DIM] = v[g * bq:(g + 1) * bq, :]


def _attn_specs(t, d, bq):
    kvh = d // HEAD_DIM // GROUP
    kc0 = d // HEAD_DIM
    q_spec = pl.BlockSpec((1, bq, QKV_BLOCK), lambda i, h, j: (i, j, h))
    k_spec = pl.BlockSpec((1, t, HEAD_DIM), lambda i, h, j: (i, 0, kc0 + h))
    v_spec = pl.BlockSpec((1, t, HEAD_DIM), lambda i, h, j: (i, 0, kc0 + kvh + h))
    lse_spec = pl.BlockSpec((1, GROUP, bq, HEAD_DIM), lambda i, h, j: (i, h, j, 0))
    kt_spec = pl.BlockSpec((1, 1, HEAD_DIM, t), lambda i, h, j: (i, h, 0, 0))
    return kvh, q_spec, k_spec, v_spec, lse_spec, kt_spec


SCALE = HEAD_DIM ** -0.5


def _attn_dense_fwd(qkv, d, ctx_len):
    b, t, _ = qkv.shape
    bq = DENSE_FWD_BQ
    lq = ctx_len // bq
    kvh, q_spec, k_spec, v_spec, lse_spec, _ = _attn_specs(t, d, bq)

    def body(q_ref, k_ref, v_ref, o_ref, lse_ref):
        i = pl.program_id(2)

        def attend(k, v):
            for g in range(GROUP):
                sl = slice(g * HEAD_DIM, (g + 1) * HEAD_DIM)
                s = _dot_nt(q_ref[0, :, sl], k)
                m = jnp.max(s, axis=1, keepdims=True)
                p = jnp.exp2((s - m) * (SCALE * LOG2E))
                l = jnp.sum(p, axis=1, keepdims=True)
                o_ref[0, :, sl] = _dot(p.astype(BF16), v) / l
                lse_ref[0, g] = jnp.broadcast_to(m * SCALE + jnp.log(l), (bq, HEAD_DIM))

        @pl.when(i < lq)
        def _():
            attend(k_ref[0, 0:ctx_len, :], v_ref[0, 0:ctx_len, :])

        @pl.when(i >= lq)
        def _():
            attend(k_ref[0], v_ref[0])

    return pl.pallas_call(
        body, name="attn_dense_fwd", grid=(b, kvh, t // bq),
        in_specs=[q_spec, k_spec, v_spec], out_specs=(q_spec, lse_spec),
        out_shape=(jax.ShapeDtypeStruct((b, t, d), F32), jax.ShapeDtypeStruct((b, kvh * GROUP, t, HEAD_DIM), F32)),
        compiler_params=_cparams(("parallel", "parallel", "arbitrary"), VMEM_LIMIT),
    )(qkv, qkv, qkv)


def _attn_dense_bwd(qkv, o3, do3, lse, d, ctx_len):
    b, t, _ = qkv.shape
    bq = ATT_BQ
    lq = ctx_len // bq
    kvh, q_spec, k_spec, v_spec, lse_spec, kt_spec = _attn_specs(t, d, bq)

    def body(q_ref, k_ref, v_ref, o_ref, do_ref, lse_ref, dq_ref, dkt_ref, dvt_ref):
        i = pl.program_id(2)

        @pl.when(i == 0)
        def _():
            dkt_ref[...] = jnp.zeros_like(dkt_ref)
            dvt_ref[...] = jnp.zeros_like(dvt_ref)

        def run(k, v, width):
            dk_acc = dv_acc = None
            for g in range(GROUP):
                sl = slice(g * HEAD_DIM, (g + 1) * HEAD_DIM)
                q = q_ref[0, :, sl]
                do = do_ref[0, :, sl]
                dd = jnp.sum(do * o_ref[0, :, sl], axis=1, keepdims=True)
                dob = do.astype(BF16)
                p = jnp.exp2(_dot_nt(q, k) * (SCALE * LOG2E) - lse_ref[0, g][:, 0:1] * LOG2E)
                ds = (p * (_dot_nt(dob, v) - dd) * SCALE).astype(BF16)
                dq_ref[0, :, sl] = _dot(ds, k)
                dk_g = _dot(q.astype(F32).T.astype(BF16), ds)
                dv_g = _dot(do.T.astype(BF16), p.astype(BF16))
                dk_acc = dk_g if dk_acc is None else dk_acc + dk_g
                dv_acc = dv_g if dv_acc is None else dv_acc + dv_g
            dkt_ref[0, 0, :, 0:width] += dk_acc
            dvt_ref[0, 0, :, 0:width] += dv_acc

        @pl.when(i < lq)
        def _():
            run(k_ref[0, 0:ctx_len, :], v_ref[0, 0:ctx_len, :], ctx_len)

        @pl.when(i >= lq)
        def _():
            run(k_ref[0], v_ref[0], t)

    return pl.pallas_call(
        body, name="attn_dense_bwd", grid=(b, kvh, t // bq),
        in_specs=[q_spec, k_spec, v_spec, q_spec, q_spec, lse_spec], out_specs=(q_spec, kt_spec, kt_spec),
        out_shape=(jax.ShapeDtypeStruct((b, t, d), F32), jax.ShapeDtypeStruct((b, kvh, HEAD_DIM, t), F32),
                   jax.ShapeDtypeStruct((b, kvh, HEAD_DIM, t), F32)),
        compiler_params=_cparams(("parallel", "parallel", "arbitrary"), VMEM_LIMIT),
    )(qkv, qkv, qkv, o3, do3, lse)


def _sink_column(sink_ref, h, bq):
    rowi = lax.broadcasted_iota(jnp.int32, (GROUP * bq, 1), 0)
    col = jnp.zeros((GROUP * bq, 1), F32)
    for g in range(GROUP):
        col = jnp.where((rowi >= g * bq) & (rowi < (g + 1) * bq), sink_ref[h * GROUP + g], col)
    return col


def _band(i, lq, ctx_len, t, bq):
    n = i - lq
    start = pl.multiple_of(jnp.clip(ctx_len + n * bq - WINDOW, ctx_len, t - WIN_SPAN), WINDOW)
    shape = (GROUP * bq, WIN_SPAN)
    kpos = start - ctx_len + lax.broadcasted_iota(jnp.int32, shape, 1)
    qpos = n * bq + (lax.broadcasted_iota(jnp.int32, shape, 0) & (bq - 1))
    return start, jnp.abs(kpos - qpos) <= WINDOW


def _attn_win_fwd(qkv, sink, d, ctx_len):
    b, t, _ = qkv.shape
    bq = WIN_BQ
    rows = GROUP * bq
    lq = ctx_len // bq
    kvh, q_spec, k_spec, v_spec, lse_spec, _ = _attn_specs(t, d, bq)

    def body(sink_ref, q_ref, k_ref, v_ref, o_ref, lse_ref):
        h, i = pl.program_id(1), pl.program_id(2)
        q4 = _stack_heads(q_ref)
        sink_col = _sink_column(sink_ref, h, bq)
        sc = _dot_nt(q4, k_ref[0, 0:ctx_len, :]) * SCALE
        mc = jnp.maximum(jnp.max(sc, axis=1, keepdims=True), sink_col)

        def finish(m, l, acc):
            _unstack_heads(o_ref, acc / l, bq)
            lse_ref[0] = jnp.broadcast_to(m + jnp.log(l), (rows, HEAD_DIM)).reshape(GROUP, bq, HEAD_DIM)

        @pl.when(i < lq)
        def _():
            pc = jnp.exp(sc - mc)
            l = jnp.sum(pc, axis=1, keepdims=True) + jnp.exp(sink_col - mc)
            finish(mc, l, _dot(pc.astype(BF16), v_ref[0, 0:ctx_len, :]))

        @pl.when(i >= lq)
        def _():
            start, ok = _band(i, lq, ctx_len, t, bq)
            sb = jnp.where(ok, _dot_nt(q4, k_ref[0, pl.ds(start, WIN_SPAN), :]) * SCALE, NEG_INF)
            m = jnp.maximum(mc, jnp.max(sb, axis=1, keepdims=True))
            pc, pb = jnp.exp(sc - m), jnp.exp(sb - m)
            l = jnp.sum(pc, axis=1, keepdims=True) + jnp.sum(pb, axis=1, keepdims=True) + jnp.exp(sink_col - m)
            acc = _dot(pc.astype(BF16), v_ref[0, 0:ctx_len, :]) + _dot(pb.astype(BF16), v_ref[0, pl.ds(start, WIN_SPAN), :])
            finish(m, l, acc)

    return pl.pallas_call(
        body, name="attn_win_fwd", grid=(b, kvh, t // bq),
        in_specs=[pl.BlockSpec(memory_space=pltpu.SMEM), q_spec, k_spec, v_spec], out_specs=(q_spec, lse_spec),
        out_shape=(jax.ShapeDtypeStruct((b, t, d), F32), jax.ShapeDtypeStruct((b, kvh * GROUP, t, HEAD_DIM), F32)),
        compiler_params=_cparams(("parallel", "parallel", "arbitrary"), VMEM_LIMIT),
    )(sink, qkv, qkv, qkv)


def _attn_win_bwd(qkv, sink, o3, do3, lse, d, ctx_len):
    b, t, _ = qkv.shape
    bq = WIN_BQ
    rows = GROUP * bq
    lq = ctx_len // bq
    kvh, q_spec, k_spec, v_spec, lse_spec, kt_spec = _attn_specs(t, d, bq)

    def body(sink_ref, q_ref, k_ref, v_ref, o_ref, do_ref, lse_ref, dq_ref, dkt_ref, dvt_ref, dsk_ref):
        h, i = pl.program_id(1), pl.program_id(2)

        @pl.when(i == 0)
        def _():
            dkt_ref[...] = jnp.zeros_like(dkt_ref)
            dvt_ref[...] = jnp.zeros_like(dvt_ref)
            dsk_ref[...] = jnp.zeros_like(dsk_ref)

        q4 = _stack_heads(q_ref)
        do4 = _stack_heads(do_ref)
        dd = jnp.sum(do4 * _stack_heads(o_ref), axis=1, keepdims=True)
        lse_col = lse_ref[0].reshape(rows, HEAD_DIM)[:, 0:1]
        do4b = do4.astype(BF16)
        qt = q4.astype(F32).T.astype(BF16)
        dot = do4.T.astype(BF16)

        def part(k, v):
            return _dot_nt(q4, k) * SCALE, _dot_nt(do4b, v)

        def grads(p, dp, k):
            ds = (p * (dp - dd) * SCALE).astype(BF16)
            return _dot(ds, k), _dot(qt, ds), _dot(dot, p.astype(BF16))

        kc = k_ref[0, 0:ctx_len, :]
        sc, dpc = part(kc, v_ref[0, 0:ctx_len, :])
        dq_c, dk_c, dv_c = grads(jnp.exp(sc - lse_col), dpc, kc)
        dkt_ref[0, 0, :, 0:ctx_len] += dk_c
        dvt_ref[0, 0, :, 0:ctx_len] += dv_c
        _unstack_heads(dq_ref, dq_c, bq)

        @pl.when(i >= lq)
        def _():
            start, ok = _band(i, lq, ctx_len, t, bq)
            kb = k_ref[0, pl.ds(start, WIN_SPAN), :]
            sb, dpb = part(kb, v_ref[0, pl.ds(start, WIN_SPAN), :])
            pb = jnp.where(ok, jnp.exp(sb - lse_col), 0.0)
            dq_b, dk_b, dv_b = grads(pb, dpb, kb)
            dkt_ref[0, 0, :, pl.ds(start, WIN_SPAN)] += dk_b
            dvt_ref[0, 0, :, pl.ds(start, WIN_SPAN)] += dv_b
            for g in range(GROUP):
                dq_ref[0, :, g * HEAD_DIM:(g + 1) * HEAD_DIM] += dq_b[g * bq:(g + 1) * bq, :]

        ps = jnp.exp(_sink_column(sink_ref, h, bq) - lse_col) * dd
        for g in range(GROUP):
            val = jnp.sum(ps[g * bq:(g + 1) * bq, :], axis=0, keepdims=True)
            dsk_ref[0, 0, g:g + 1, :] -= jnp.broadcast_to(val, (1, HEAD_DIM))

    return pl.pallas_call(
        body, name="attn_win_bwd", grid=(b, kvh, t // bq),
        in_specs=[pl.BlockSpec(memory_space=pltpu.SMEM), q_spec, k_spec, v_spec, q_spec, q_spec, lse_spec],
        out_specs=(q_spec, kt_spec, kt_spec, pl.BlockSpec((1, 1, 8, HEAD_DIM), lambda i, h, j: (i, h, 0, 0))),
        out_shape=(jax.ShapeDtypeStruct((b, t, d), F32), jax.ShapeDtypeStruct((b, kvh, HEAD_DIM, t), F32),
                   jax.ShapeDtypeStruct((b, kvh, HEAD_DIM, t), F32), jax.ShapeDtypeStruct((b, kvh, 8, HEAD_DIM), F32)),
        compiler_params=_cparams(("parallel", "parallel", "arbitrary"), VMEM_LIMIT),
    )(sink, qkv, qkv, qkv, o3, do3, lse)


MERGE_BWD_ROWS = 256
MERGE_BWD_VMEM = 60 * 1024 * 1024


def _resident(shape):
    return pl.BlockSpec(shape, lambda *_: (0,) * len(shape), pipeline_mode=pl.Buffered(1))


def _merge_fwd(x3, ya, yb, yc, proj3, w_br, w_out, modsel, ctx_len):
    b, t, d = x3.shape
    bt = ROW_BLOCK
    lb = ctx_len // bt

    def body(x_ref, ya_ref, yb_ref, yc_ref, gm_ref, wbr_ref, wo_ref, m_ref, xn_ref, out_ref):
        mix = jnp.zeros((bt, d), F32)
        for n, y_ref in enumerate((ya_ref, yb_ref, yc_ref)):
            z = (y_ref[0] * _silu(gm_ref[0, :, n * d:(n + 1) * d])).astype(BF16)
            mix = mix + _sigmoid(gm_ref[0, :, (3 + n) * d:(4 + n) * d]) * _dot(z, wbr_ref[n])
        o = _dot(mix.astype(BF16), wo_ref[...])
        out_ref[0] = o
        xn_ref[0] = x_ref[0] + m_ref[0, 0, 2:3, :] * o

    blk = pl.BlockSpec((1, bt, d), lambda i, j: (i, j, 0))
    return pl.pallas_call(
        body, name="merge_fwd", grid=(b, t // bt),
        in_specs=[blk, blk, blk, blk, pl.BlockSpec((1, bt, 6 * d), lambda i, j: (i, j, 0)),
                  _resident((3, d, d)), _resident((d, d)),
                  pl.BlockSpec((1, 1, 8, d), lambda i, j: (i, _row_kind(j, lb), 0, 0))],
        out_specs=(blk, blk),
        out_shape=(jax.ShapeDtypeStruct((b, t, d), F32), jax.ShapeDtypeStruct((b, t, d), F32)),
        compiler_params=_cparams(("parallel", "arbitrary"), VMEM_LIMIT),
    )(x3, ya, yb, yc, proj3, w_br, w_out, modsel)


def _merge_bwd(dxn3, out3, ya, yb, yc, proj3, w_br, w_out, modsel, ctx_len):
    b, t, d = dxn3.shape
    n_cols = proj3.shape[2]
    bt = MERGE_BWD_ROWS
    lb = ctx_len // bt

    def body(dxn_ref, out_ref, ya_ref, yb_ref, yc_ref, gm_ref, wbr_ref, wo_ref, m_ref,
             dgm_ref, dya_ref, dyb_ref, dyc_ref, z_ref, dt_ref, mix_ref, dout_ref, gacc_ref):
        j = pl.program_id(1)
        dxn = dxn_ref[0]
        doutb = (m_ref[0, 0, 2:3, :] * dxn).astype(BF16)
        dout_ref[0] = doutb

        @pl.when((j == 0) | (j == lb))
        def _():
            gacc_ref[...] = jnp.zeros_like(gacc_ref)

        gacc_ref[0, 0, 0:1, :] += _colsum(dxn * out_ref[0])
        dmix = _dot_nt(doutb, wo_ref[...])
        mix = jnp.zeros((bt, d), F32)
        for n, (y_ref, dy_ref) in enumerate(((ya_ref, dya_ref), (yb_ref, dyb_ref), (yc_ref, dyc_ref))):
            g = gm_ref[0, :, n * d:(n + 1) * d]
            y = y_ref[0]
            sig_g = _sigmoid(g)
            silu_g = g * sig_g
            z = (y * silu_g).astype(BF16)
            z_ref[n, 0] = z
            tn = _dot(z, wbr_ref[n])
            s = _sigmoid(gm_ref[0, :, (3 + n) * d:(4 + n) * d])
            mix = mix + s * tn
            dgm_ref[0, :, (3 + n) * d:(4 + n) * d] = (dmix * tn * (s * (1.0 - s))).astype(BF16)
            dtb = (dmix * s).astype(BF16)
            dt_ref[n, 0] = dtb
            dz = _dot_nt(dtb, wbr_ref[n])
            dy_ref[0] = dz * silu_g
            dgm_ref[0, :, n * d:(n + 1) * d] = (dz * y * (sig_g * (1.0 + g * (1.0 - sig_g)))).astype(BF16)
        mix_ref[0] = mix.astype(BF16)

    blk = pl.BlockSpec((1, bt, d), lambda i, j: (i, j, 0))
    blk4 = pl.BlockSpec((3, 1, bt, d), lambda i, j: (0, i, j, 0))
    wide = pl.BlockSpec((1, bt, 6 * d), lambda i, j: (i, j, 0))
    return pl.pallas_call(
        body, name="merge_bwd", grid=(b, t // bt),
        in_specs=[blk, blk, blk, blk, blk, wide, _resident((3, d, d)), _resident((d, d)),
                  pl.BlockSpec((1, 1, 8, d), lambda i, j: (i, _row_kind(j, lb), 0, 0))],
        out_specs=(wide, blk, blk, blk, blk4, blk4, blk, blk,
                   pl.BlockSpec((1, 1, 8, d), lambda i, j: (i, _row_kind(j, lb), 0, 0))),
        out_shape=(jax.ShapeDtypeStruct((b, t, n_cols), BF16),
                   jax.ShapeDtypeStruct((b, t, d), F32), jax.ShapeDtypeStruct((b, t, d), F32),
                   jax.ShapeDtypeStruct((b, t, d), F32),
                   jax.ShapeDtypeStruct((3, b, t, d), BF16), jax.ShapeDtypeStruct((3, b, t, d), BF16),
                   jax.ShapeDtypeStruct((b, t, d), BF16), jax.ShapeDtypeStruct((b, t, d), BF16),
                   jax.ShapeDtypeStruct((b, 2, 8, d), F32)),
        compiler_params=_cparams(("parallel", "arbitrary"), MERGE_BWD_VMEM),
    )(dxn3, out3, ya, yb, yc, proj3, w_br, w_out, modsel)


def _final(x3, g, target, ctx_len):
    b, t, d = x3.shape
    bt = ROW_BLOCK
    lb = ctx_len // bt

    def body(x_ref, g_ref, t_ref, dx_ref, loss_ref, dg_ref):
        j = pl.program_id(1)

        @pl.when(j == 0)
        def _():
            loss_ref[...] = jnp.zeros_like(loss_ref)
            dg_ref[...] = jnp.zeros_like(dg_ref)

        @pl.when(j < lb)
        def _():
            dx_ref[...] = jnp.zeros_like(dx_ref)

        @pl.when(j >= lb)
        def _():
            x = x_ref[0]
            g_row = g_ref[...]
            rstd = lax.rsqrt(jnp.mean(x * x, axis=-1, keepdims=True) + EPS)
            xhat = x * rstd
            err = xhat * g_row - t_ref[0]
            loss_ref[...] += (0.5 / d) * jnp.sum(err * err)
            dy = err * (1.0 / d)
            dg_ref[0, 0:1, :] += _colsum(dy * xhat)
            dxhat = dy * g_row
            dx_ref[0] = rstd * (dxhat - xhat * jnp.mean(dxhat * xhat, axis=-1, keepdims=True))

    blk = pl.BlockSpec((1, bt, d), lambda i, j: (i, j, 0))
    return pl.pallas_call(
        body, name="final_loss", grid=(b, t // bt),
        in_specs=[blk, pl.BlockSpec((1, d), lambda i, j: (0, 0)),
                  pl.BlockSpec((1, bt, d), lambda i, j: (i, jnp.maximum(j - lb, 0), 0))],
        out_specs=(blk, pl.BlockSpec((1, 8, HEAD_DIM), lambda i, j: (i, 0, 0)), pl.BlockSpec((1, 8, d), lambda i, j: (i, 0, 0))),
        out_shape=(jax.ShapeDtypeStruct((b, t, d), F32), jax.ShapeDtypeStruct((b, 8, HEAD_DIM), F32),
                   jax.ShapeDtypeStruct((b, 8, d), F32)),
        compiler_params=_cparams(("parallel", "arbitrary")),
    )(x3, g, target)


TOKEN_BLOCKS = (1088, 512, 256, 128)


def _pick(n, options):
    for o in options:
        if n % o == 0:
            return o
    raise ValueError((n, options))


def _block_diag(w):
    per = LRU_LANES // LRU_BLOCK_W
    nd, nb, bw, _ = w.shape
    wr = w.reshape(nd, nb // per, per, bw, bw)
    eye = jnp.eye(per, dtype=w.dtype)
    bd = wr[:, :, :, :, None, :] * eye[None, None, :, None, :, None]
    return bd.reshape(nd, nb // per, per * bw, per * bw).astype(BF16)


def _block_diag_grad(g):
    per = LRU_LANES // LRU_BLOCK_W
    nd, ng, _, _ = g.shape
    gr = g.reshape(nd, ng, per, LRU_BLOCK_W, per, LRU_BLOCK_W)
    diag = jnp.stack([gr[:, :, k, :, k, :] for k in range(per)], axis=2)
    return diag.reshape(nd, ng * per, LRU_BLOCK_W, LRU_BLOCK_W)


def _mod_select(mod16, b, d):
    m3 = mod16.reshape(MOD_ROWS, 3, d)
    lat = m3[:b]
    ctx = jnp.broadcast_to(m3[b][None], (b, 3, d))
    sel = jnp.stack([ctx, lat], axis=1)
    return jnp.pad(sel, ((0, 0), (0, 0), (0, 5), (0, 0)))


def _layer_fwd(x3, c16, p, cos, sin, ctx_len):
    b, t, d = x3.shape
    off, n_cols = _layout(d)
    mod16 = _mod_fwd(c16, p["w_mod"], p["b_mod"])
    modsel = _mod_select(mod16, b, d)
    h = _norm_mod_fwd(x3, p["norm_g"], modsel, ctx_len)
    proj = _matmul(h.reshape(b * t, d), p["w_in"], bm=_pick(b * t, TOKEN_BLOCKS), bn=1024, bk=d, name="proj_fwd",
                   n_outer=True, b_lead=p["li"])
    proj3 = proj.reshape(b, t, n_cols)
    ya = _lru_fwd(proj3, off["uA"], p["conv_w"], p["conv_b"], p["wa_bd"], p["ba"], p["wx_bd"], p["bx"], p["lam"], ctx_len)
    qkv_b = _prep_fwd(proj3, off["qB"], off["kB"], d, cos, sin, p["gq"], p["gk"], use_norm=False)
    yb, lse_b = _attn_win_fwd(qkv_b, p["sink"], d, ctx_len)
    qkv_c = _prep_fwd(proj3, off["qC"], off["kC"], d, cos, sin, p["gq"], p["gk"], use_norm=True)
    yc, lse_c = _attn_dense_fwd(qkv_c, d, ctx_len)
    x_new, out3 = _merge_fwd(x3, ya, yb, yc, proj3, p["w_br"], p["w_out"], modsel, ctx_len)
    return x_new, (x3, modsel, h, proj3, ya, yb, yc, qkv_b, lse_b, qkv_c, lse_c, out3)


def _layer_bwd(dxn3, saved, c16, p, cos, sin, ctx_len):
    x3, modsel, h, proj3, ya, yb, yc, qkv_b, lse_b, qkv_c, lse_c, out3 = saved
    b, t, d = x3.shape
    off, n_cols = _layout(d)
    rows = b * t
    bk = _pick(rows, TOKEN_BLOCKS)
    dproj3, dya, dyb, dyc, z4, dt4, mixb, doutb, gacc = _merge_bwd(dxn3, out3, ya, yb, yc, proj3, p["w_br"], p["w_out"],
                                                                   modsel, ctx_len)
    dw_br = jnp.stack([_matmul(z4.reshape(3, rows, d), dt4.reshape(3, rows, d), ta=True, bm=d, bn=d, bk=bk,
                               name="dw_branch", a_lead=n, b_lead=n) for n in range(3)])
    dw_out = _matmul(mixb.reshape(rows, d), doutb.reshape(rows, d), ta=True, bm=d, bn=d, bk=bk, name="dw_out")
    dproj3, vec, dwa, dwx = _lru_bwd(dproj3, proj3, off["uA"], dya, p["conv_w"], p["conv_b"], p["wa_bd"], p["ba"],
                                     p["wx_bd"], p["bx"], p["lam"], ctx_len)
    dq_b, dkt_b, dvt_b, dsk = _attn_win_bwd(qkv_b, p["sink"], yb, dyb, lse_b, d, ctx_len)
    dproj3, _ = _prep_bwd(dproj3, dq_b, dkt_b, dvt_b, proj3, off["qB"], off["kB"], d, cos, sin, p["gq"], p["gk"], False)
    dq_c, dkt_c, dvt_c = _attn_dense_bwd(qkv_c, yc, dyc, lse_c, d, ctx_len)
    dproj3, gqk = _prep_bwd(dproj3, dq_c, dkt_c, dvt_c, proj3, off["qC"], off["kC"], d, cos, sin, p["gq"], p["gk"], True)
    dproj2 = dproj3.reshape(rows, n_cols)
    dw_in = _matmul(h.reshape(rows, d), dproj2, ta=True, bm=d, bn=1024, bk=bk, name="dw_in")
    dh = _matmul(dproj2, p["w_in"], tb=True, bm=_pick(rows, TOKEN_BLOCKS), bn=d, bk=1024, name="dh", b_lead=p["li"])
    dx3, nacc = _norm_mod_bwd(dh.reshape(b, t, d), x3, p["norm_g"], modsel, dxn3, ctx_len)
    per = jnp.stack([nacc[:, :, 0], nacc[:, :, 1], gacc[:, :, 0]], axis=2)
    dmod = jnp.concatenate([per[:, 1].reshape(b, 3 * d), jnp.sum(per[:, 0], axis=0).reshape(1, 3 * d)], axis=0)
    dmod16 = jnp.pad(dmod, ((0, MOD_ROWS - b - 1), (0, 0)))
    dw_mod, db_mod, dc16 = _mod_bwd(c16, dmod16, p["w_mod"])
    vsum = jnp.sum(vec, axis=0)
    grads = {
        "norm_g": jnp.sum(nacc[:, :, 2], axis=(0, 1)),
        "w_mod": dw_mod, "b_mod": db_mod[0], "w_in": dw_in,
        "conv_w": vsum[0:4], "conv_b": vsum[4],
        "lru_wa": _block_diag_grad(jnp.sum(dwa, axis=0)), "lru_ba": vsum[5:7],
        "lru_wx": _block_diag_grad(jnp.sum(dwx, axis=0)), "lru_bx": vsum[7:9],
        "lru_lambda": vsum[9:11] * (-jax.nn.sigmoid(-p["lam"])),
        "attn_sink": jnp.sum(dsk[:, :, 0:GROUP, 0], axis=0).reshape(-1),
        "q_norm_g": jnp.sum(gqk[:, 0], axis=0), "k_norm_g": jnp.sum(gqk[:, 1], axis=0),
        "w_branch": dw_br, "w_out": dw_out,
    }
    return dx3, dc16, grads


def _reorder_in_cols(w, d, inverse=False):
    off_new, _ = _layout(d)
    segs = _orig_segments(d)
    if inverse:
        return jnp.concatenate([w[..., off_new[n]:off_new[n] + wd] for n, _, wd in segs], axis=-1)
    by_name = {n: (o, wd) for n, o, wd in segs}
    order = sorted(off_new, key=off_new.get)
    return jnp.concatenate([w[..., by_name[n][0]:by_name[n][0] + by_name[n][1]] for n in order], axis=-1)


def _layer_params(li, w):
    return {
        "li": li, "norm_g": w["norm_g"][li][None], "w_mod": w["w_mod"][li], "b_mod": w["b_mod"][li][None],
        "w_in": w["w_in_r"],
        "conv_w": w["conv_w"][li], "conv_b": w["conv_b"][li][None],
        "wa_bd": _block_diag(w["lru_wa"][li]), "ba": w["lru_ba"][li],
        "wx_bd": _block_diag(w["lru_wx"][li]), "bx": w["lru_bx"][li], "lam": w["lru_lambda"][li],
        "sink": w["attn_sink"][li], "gq": w["q_norm_g"][li][None], "gk": w["k_norm_g"][li][None],
        "w_br": w["w_branch"][li], "w_out": w["w_out"][li],
    }


def _local_step(x, c, ctx, target, c_ctx, final_g, layers):
    b, s, d = x.shape
    ctx_len = ctx.shape[1]
    cos, sin = _rope_tables(ctx_len, s)
    x3 = jnp.concatenate([ctx, x], axis=1)
    c16 = jnp.concatenate([c, c_ctx[None], jnp.zeros((MOD_ROWS - b - 1, d), F32)], axis=0)
    saved = []
    for p in layers:
        x3, sv = _layer_fwd(x3, c16, p, cos, sin, ctx_len)
        saved.append(sv)
    dx3, loss_acc, dgf = _final(x3, final_g[None], target, ctx_len)
    grads = [None] * len(layers)
    dc_ctx = jnp.zeros((d,), F32)
    for li in reversed(range(len(layers))):
        dx3, dc16, grads[li] = _layer_bwd(dx3, saved[li], c16, layers[li], cos, sin, ctx_len)
        dc_ctx = dc_ctx + dc16[b]
    return jnp.sum(loss_acc[:, 0, 0]), dx3[:, ctx_len:], dc_ctx, jnp.sum(dgf[:, 0], axis=0), grads


N_CHIPS = 4
ANY = pl.BlockSpec(memory_space=pl.ANY)


def _place():
    x, y, c = lax.axis_index("x"), lax.axis_index("y"), lax.axis_index("c")
    return x, y, c, [(1 - x, y), (x, 1 - y), (1 - x, 1 - y)]


def _axis_part(ref, axis, start, size):
    idx = [slice(None)] * len(ref.shape)
    idx[axis] = pl.ds(start, size)
    return ref.at[tuple(idx)]


def _remote(src, dst, send, recv, dev):
    return pltpu.make_async_remote_copy(src_ref=src, dst_ref=dst, send_sem=send, recv_sem=recv, device_id=dev,
                                        device_id_type=MESH)


COPY_PIECES = 8


def _pieces(src, dst):
    shape = src.shape
    for ax in range(len(shape) - 1):
        if shape[ax] % COPY_PIECES == 0 and shape[ax] // COPY_PIECES >= 8:
            sz = shape[ax] // COPY_PIECES
            return [(_axis_part(src, ax, j * sz, sz), _axis_part(dst, ax, j * sz, sz)) for j in range(COPY_PIECES)]
    return [(src, dst)]


def _gather_chips(wholes, axes, name):
    n = len(wholes)

    def body(*refs):
        bufs = refs[n:2 * n]
        send, recv, fsend, frecv = refs[2 * n:]
        x, y, c, chips = _place()
        me = 2 * x + y
        sib = (x, y, 1 - c)

        def block(i, chip_index, half):
            sz = wholes[i].shape[axes[i]] // N_CHIPS
            hl = wholes[i].shape[0] // 2
            return _axis_part(bufs[i], axes[i], chip_index * sz, sz).at[pl.ds(half * hl, hl)]

        for i in range(n):
            for k, (px, py) in enumerate(chips):
                _remote(block(i, me, c), block(i, me, c), send.at[i, k], recv.at[i, k], (px, py, c)).start()
        for i in range(n):
            for k, (px, py) in enumerate(chips):
                landed = block(i, 2 * px + py, c)
                _remote(landed, landed, send.at[i, k], recv.at[i, k], (px, py, c)).wait_recv()
                for s_, d_ in _pieces(landed, landed):
                    _remote(s_, d_, fsend.at[i, k], frecv.at[i, k], sib).start()
        for i in range(n):
            for k, (px, py) in enumerate(chips):
                passed = _remote(block(i, 2 * px + py, c), block(i, 2 * px + py, 1 - c), fsend.at[i, k], frecv.at[i, k], sib)
                passed.wait_recv()
                passed.wait_send()
                _remote(block(i, me, c), block(i, me, c), send.at[i, k], recv.at[i, k], (px, py, c)).wait_send()

    sems = pltpu.SemaphoreType.DMA((n, 3))
    return pl.pallas_call(
        body, name=name, in_specs=[ANY] * n, out_specs=tuple([ANY] * n),
        out_shape=tuple(jax.ShapeDtypeStruct(a.shape, a.dtype) for a in wholes),
        input_output_aliases={i: i for i in range(n)},
        scratch_shapes=[sems, sems, sems, sems],
    )(*wholes)


def _own_block_placed(shard, axis, chip):
    shape = list(shard.shape)
    shape[axis] *= N_CHIPS
    return lax.dynamic_update_slice_in_dim(lax.empty(tuple(shape), shard.dtype), shard, chip * shard.shape[axis], axis)


def _split_cores(gs, name):
    n = len(gs)

    def body(*refs):
        ins, got = refs[:n], refs[n:2 * n]
        send, recv = refs[2 * n:]
        x, y, c, _ = _place()
        sib = (x, y, 1 - c)

        def theirs(i):
            hl = gs[i].shape[0] // 2
            return ins[i].at[pl.ds((1 - c) * hl, hl)]

        for i in range(n):
            for s_, d_ in _pieces(theirs(i), got[i]):
                _remote(s_, d_, send.at[i], recv.at[i], sib).start()
        for i in range(n):
            _remote(theirs(i), got[i], send.at[i], recv.at[i], sib).wait()

    return pl.pallas_call(
        body, name=name, in_specs=[ANY] * n, out_specs=tuple([ANY] * n),
        out_shape=tuple(jax.ShapeDtypeStruct((g.shape[0] // 2,) + g.shape[1:], g.dtype) for g in gs),
        scratch_shapes=[pltpu.SemaphoreType.DMA((n,)), pltpu.SemaphoreType.DMA((n,))],
    )(*gs)


def _scatter_chips(pbs, axes, name):
    n = len(pbs)

    def block(p, ax):
        shape = list(p.shape)
        shape[ax] //= N_CHIPS
        return tuple(shape)

    def body(*refs):
        inb, got = refs[:n], refs[n:2 * n]
        send, recv = refs[2 * n:]
        x, y, c, chips = _place()

        def part(i, chip_index):
            sz = pbs[i].shape[axes[i]] // N_CHIPS
            return _axis_part(inb[i], axes[i], chip_index * sz, sz)

        for i in range(n):
            for k, (px, py) in enumerate(chips):
                _remote(part(i, 2 * px + py), got[i].at[k], send.at[i, k], recv.at[i, k], (px, py, c)).start()
        for i in range(n):
            for k, (px, py) in enumerate(chips):
                _remote(part(i, 2 * px + py), got[i].at[k], send.at[i, k], recv.at[i, k], (px, py, c)).wait()

    return pl.pallas_call(
        body, name=name, in_specs=[ANY] * n, out_specs=tuple([ANY] * n),
        out_shape=tuple(jax.ShapeDtypeStruct((3,) + block(p, ax), p.dtype) for p, ax in zip(pbs, axes)),
        scratch_shapes=[pltpu.SemaphoreType.DMA((n, 3)), pltpu.SemaphoreType.DMA((n, 3))],
    )(*pbs)


def _join_cores(bufs, name):
    n = len(bufs)

    def body(*refs):
        outs = refs[n:2 * n]
        send, recv = refs[2 * n:]
        x, y, c, _ = _place()
        sib = (x, y, 1 - c)

        def half(i, which):
            hl = bufs[i].shape[0] // 2
            return outs[i].at[pl.ds(which * hl, hl)]

        for i in range(n):
            for s_, d_ in _pieces(half(i, c), half(i, c)):
                _remote(s_, d_, send.at[i], recv.at[i], sib).start()
        for i in range(n):
            cp = _remote(half(i, c), half(i, 1 - c), send.at[i], recv.at[i], sib)
            cp.wait_recv()
            cp.wait_send()

    return pl.pallas_call(
        body, name=name, in_specs=[ANY] * n, out_specs=tuple([ANY] * n),
        out_shape=tuple(jax.ShapeDtypeStruct(a.shape, a.dtype) for a in bufs),
        input_output_aliases={i: i for i in range(n)},
        scratch_shapes=[pltpu.SemaphoreType.DMA((n,)), pltpu.SemaphoreType.DMA((n,))],
    )(*bufs)


def _all_reduce_small(buf):
    r = buf.shape[0]

    def body(in_ref, out_ref, sib_buf, chip_sum, got, send, recv):
        x, y, c, chips = _place()
        cp = _remote(in_ref, sib_buf, send.at[0], recv.at[0], (x, y, 1 - c))
        cp.start()
        cp.wait()
        chip_sum[...] = in_ref[...] + sib_buf[...]
        cps = [_remote(chip_sum, got.at[k], send.at[1 + k], recv.at[1 + k], (px, py, c)) for k, (px, py) in enumerate(chips)]
        for cp in cps:
            cp.start()
        for cp in cps:
            cp.wait()
        out_ref[...] = (chip_sum[...] + got[0]) + (got[1] + got[2])

    return pl.pallas_call(
        body, name="all_reduce_small", out_shape=jax.ShapeDtypeStruct(buf.shape, F32),
        in_specs=[pl.BlockSpec(memory_space=pltpu.VMEM)], out_specs=pl.BlockSpec(memory_space=pltpu.VMEM),
        scratch_shapes=[pltpu.VMEM((r, 128), F32), pltpu.VMEM((r, 128), F32), pltpu.VMEM((3, r, 128), F32),
                        pltpu.SemaphoreType.DMA((4,)), pltpu.SemaphoreType.DMA((4,))],
        compiler_params=_cparams(None, VMEM_LIMIT),
    )(buf)


ELEMENTWISE_BLOCK_BYTES = 1 << 20


def _view2d(a):
    cols = a.shape[-1] if a.ndim > 1 else 128
    return a.reshape(-1, cols)


def _row_block(rows, cols):
    want = max(8, ELEMENTWISE_BLOCK_BYTES // (4 * cols))
    br = rows
    while br > want and br % 2 == 0 and (br // 2) % 16 == 0:
        br //= 2
    return br


def _core():
    return lax.axis_index("c")


def _chip():
    return 2 * lax.axis_index("x") + lax.axis_index("y")


def _sum_half(g, got, name):
    h = got.shape[0]
    gv = g.reshape(2 * h, -1, g.shape[-1])
    tv = got.reshape(h, -1, g.shape[-1])
    _, rows, cols = tv.shape
    br = _row_block(rows, cols)

    def body(g_ref, t_ref, p_ref, pb_ref):
        p = g_ref[...] + t_ref[...]
        p_ref[...] = p
        pb_ref[...] = p.astype(BF16)

    blk = pl.BlockSpec((1, br, cols), lambda l, i: (l, i, 0))
    p, pb = pl.pallas_call(
        body, name=name, grid=(h, rows // br),
        in_specs=[pl.BlockSpec((1, br, cols), lambda l, i: (_core() * h + l, i, 0)), blk], out_specs=(blk, blk),
        out_shape=(jax.ShapeDtypeStruct(tv.shape, F32), jax.ShapeDtypeStruct(tv.shape, BF16)),
        compiler_params=_cparams(("parallel", "parallel")))(gv, tv)
    return p.reshape(got.shape), pb.reshape(got.shape)


def _sum_blocks(p, got3, axis, name):
    h = p.shape[0]
    blk_shape = got3.shape[1:]
    cols_mode = axis == p.ndim - 1
    pv = p.reshape(-1, p.shape[-2], p.shape[-1])
    tv = got3.reshape(3, -1, blk_shape[-2], blk_shape[-1])
    la, rb, cb = tv.shape[1:]
    assert cols_mode or axis == p.ndim - 2
    br = _row_block(rb, cb)
    per = rb // br

    def body(p_ref, a_ref, b_ref, c_ref, out_ref):
        out_ref[...] = ((p_ref[...] + a_ref[0].astype(F32)) + b_ref[0].astype(F32)) + c_ref[0].astype(F32)

    if cols_mode:
        p_spec = pl.BlockSpec((1, br, cb), lambda l, i: (l, i, _chip()))
    else:
        p_spec = pl.BlockSpec((1, br, cb), lambda l, i: (l, _chip() * per + i, 0))
    out = pl.pallas_call(
        body, name=name, grid=(la, per),
        in_specs=[p_spec] + [pl.BlockSpec((1, 1, br, cb), lambda l, i, k=k: (k, l, i, 0)) for k in range(3)],
        out_specs=pl.BlockSpec((1, br, cb), lambda l, i: (_core() * la + l, i, 0)),
        out_shape=jax.ShapeDtypeStruct((2 * la, rb, cb), F32),
        compiler_params=_cparams(("parallel", "parallel")))(pv, tv, tv, tv)
    return out.reshape((2 * h,) + blk_shape[1:])


def _adamw(w, g, m, v, name):
    shape = w.shape
    ops = [_view2d(a) for a in (w, g, m, v)]
    rows, cols = ops[0].shape
    br = _row_block(rows, cols)
    c1 = 1.0 - ADAM_B1 ** ADAM_STEP
    c2 = 1.0 - ADAM_B2 ** ADAM_STEP

    def body(w_ref, g_ref, m_ref, v_ref, d_ref, nm_ref, nv_ref):
        g_ = g_ref[...]
        nm = ADAM_B1 * m_ref[...] + (1.0 - ADAM_B1) * g_
        nv = ADAM_B2 * v_ref[...] + (1.0 - ADAM_B2) * (g_ * g_)
        d_ref[...] = -ADAM_LR * ((nm / c1) / (jnp.sqrt(nv / c2) + ADAM_EPS) + ADAM_WD * w_ref[...])
        nm_ref[...] = nm
        nv_ref[...] = nv

    blk = pl.BlockSpec((br, cols), lambda i: (i, 0))
    outs = pl.pallas_call(body, name=name, grid=(rows // br,), in_specs=[blk] * 4, out_specs=(blk, blk, blk),
                          out_shape=tuple(jax.ShapeDtypeStruct((rows, cols), F32) for _ in range(3)),
                          compiler_params=_cparams(("parallel",)))(*ops)
    return tuple(o.reshape(shape) for o in outs)


def _pack(arrays):
    flat = jnp.concatenate([a.reshape(-1) for a in arrays])
    pad = (-flat.shape[0]) % (8 * 128)
    return jnp.pad(flat, (0, pad)).reshape(-1, 128)


def _unpack(buf, shapes):
    flat = buf.reshape(-1)
    out, o = [], 0
    for s in shapes:
        n = int(np.prod(s))
        out.append(flat[o:o + n].reshape(s))
        o += n
    return out


WEIGHTS = ["c_ctx", "norm_g", "w_mod", "b_mod", "w_in", "conv_w", "conv_b", "lru_wa", "lru_ba", "lru_wx", "lru_bx",
           "lru_lambda", "attn_sink", "q_norm_g", "k_norm_g", "w_branch", "w_out", "final_g"]
BIG = {"w_mod": 2, "w_in": 2, "w_branch": 2, "w_out": 1}
SMALL_SHARDED = ["conv_w", "lru_ba", "lru_bx", "lru_lambda"]
REPLICATED = [n for n in WEIGHTS if n not in BIG and n not in SMALL_SHARDED]


def kernel(x, c, ctx, c_ctx, norm_g, w_mod, b_mod, w_in, conv_w, conv_b, lru_wa, lru_ba, lru_wx, lru_bx, lru_lambda, attn_sink, q_norm_g, k_norm_g, w_branch, w_out, final_g, loss_target, m_c_ctx, m_norm_g, m_w_mod, m_b_mod, m_w_in, m_conv_w, m_conv_b, m_lru_wa, m_lru_ba, m_lru_wx, m_lru_bx, m_lru_lambda, m_attn_sink, m_q_norm_g, m_k_norm_g, m_w_branch, m_w_out, m_final_g, v_c_ctx, v_norm_g, v_w_mod, v_b_mod, v_w_in, v_conv_w, v_conv_b, v_lru_wa, v_lru_ba, v_lru_wx, v_lru_bx, v_lru_lambda, v_attn_sink, v_q_norm_g, v_k_norm_g, v_w_branch, v_w_out, v_final_g):
    args = dict(locals())
    w = {n: args[n] for n in WEIGHTS}
    mom = {n: args["m_" + n] for n in WEIGHTS}
    var = {n: args["v_" + n] for n in WEIGHTS}
    depth, d = norm_g.shape
    chip = 2 * lax.axis_index("x") + lax.axis_index("y")

    big_names = list(BIG)
    small_shard = jnp.concatenate([w[n] for n in SMALL_SHARDED], axis=1)
    gather_axes = [BIG[n] for n in big_names] + [2]
    placed = [_own_block_placed(a, ax, chip)
              for a, ax in zip([w[n].astype(BF16) for n in big_names] + [small_shard], gather_axes)]
    gathered = _gather_chips(placed, gather_axes, "gather_weights")
    whole = dict(w)
    whole.update(dict(zip(big_names, gathered[:-1])))
    o = 0
    for n in SMALL_SHARDED:
        rows = w[n].shape[1]
        whole[n] = gathered[-1][:, o:o + rows]
        o += rows
    whole["w_in_r"] = _reorder_in_cols(whole["w_in"], d)
    layers = [_layer_params(li, whole) for li in range(depth)]

    loss_local, grad_x, g_c_ctx, g_final, lgrads = _local_step(x, c, ctx, loss_target, c_ctx, final_g, layers)
    loss = lax.psum(loss_local, ("x", "y", "c"))
    full = {n: jnp.stack([lg[n] for lg in lgrads]) for n in lgrads[0]}
    full["w_in"] = _reorder_in_cols(full["w_in"], d, inverse=True)
    full["c_ctx"], full["final_g"] = g_c_ctx, g_final

    bigs = [full[n] for n in big_names]
    got = _split_cores(bigs, "grad_split_cores")
    parts = [_sum_half(g, t_, "grad_chip_sum") for g, t_ in zip(bigs, got)]
    recv = _scatter_chips([pb for _, pb in parts], [BIG[n] for n in big_names], "grad_scatter_chips")
    totals = [_sum_blocks(p_, r, BIG[n], "grad_total") for (p_, _), r, n in zip(parts, recv, big_names)]
    grad = dict(zip(big_names, _join_cores(totals, "grad_join_cores")))

    small_names = REPLICATED + SMALL_SHARDED
    reduced = _unpack(_all_reduce_small(_pack([full[n] for n in small_names])), [full[n].shape for n in small_names])
    for n, g in zip(small_names, reduced):
        if n in SMALL_SHARDED:
            sz = w[n].shape[-1]
            g = lax.dynamic_slice_in_dim(g, chip * sz, sz, axis=g.ndim - 1)
        grad[n] = g

    delta, new_m, new_v = {}, {}, {}
    for n in big_names:
        delta[n], new_m[n], new_v[n] = _adamw(w[n], grad[n], mom[n], var[n], "adamw_" + n)
    shapes = [w[n].shape for n in small_names]
    packed = _adamw(_pack([w[n] for n in small_names]), _pack([grad[n] for n in small_names]),
                    _pack([mom[n] for n in small_names]), _pack([var[n] for n in small_names]), "adamw_small")
    for res, p in zip((delta, new_m, new_v), packed):
        res.update(dict(zip(small_names, _unpack(p, shapes))))

    return (loss, grad_x, *[grad[n] for n in WEIGHTS], *[delta[n] for n in WEIGHTS],
            *[new_m[n] for n in WEIGHTS], *[new_v[n] for n in WEIGHTS])
```

```python
import functools

import jax
import jax.numpy as jnp
import numpy as np
from jax import lax
from jax.experimental import pallas as pl
from jax.experimental.pallas import tpu as pltpu

F32 = jnp.float32
BF16 = jnp.bfloat16

HEAD_DIM = 128
GROUP = 4
LRU_BLOCK_W = 64
LRU_C = 8.0
WINDOW = 128
GRID_W = 64
ROPE_THETA = 10000.0
EPS = 1e-6
NEG_INF = -1e30
ADAM_LR, ADAM_B1, ADAM_B2, ADAM_EPS, ADAM_WD, ADAM_STEP = 0.001, 0.9, 0.999, 1e-08, 0.01, 10

ROW_BLOCK = 256
LRU_LANES = 128
LRU_CHUNK = 128
LRU_UNROLL = 2
DENSE_FWD_BQ = 256
LOG2E = 1.4426950408889634
ATT_BQ = 256
WIN_BQ = 256
WIN_SPAN = WIN_BQ + 2 * WINDOW
MOD_ROWS = 16
VMEM_LIMIT = 56 * 1024 * 1024

MESH = pl.DeviceIdType.MESH


def _cparams(sem=None, vmem=None):
    kw = {}
    if sem is not None:
        kw["dimension_semantics"] = sem
    if vmem is not None:
        kw["vmem_limit_bytes"] = vmem
    return pltpu.CompilerParams(**kw)


def _sigmoid(v):
    return 1.0 / (1.0 + jnp.exp(-v))


def _silu(v):
    return v * _sigmoid(v)


def _dsilu(v):
    s = _sigmoid(v)
    return s * (1.0 + v * (1.0 - s))


def _one_minus_square(log_a, a):
    z2 = log_a * log_a
    series = (-2.0 * a * log_a) * (1.0 + z2 * (1.0 / 6 + z2 * (1.0 / 120 + z2 * (1.0 / 5040))))
    return jnp.where(z2 < 0.25, series, 1.0 - a * a)


def _log1p(y):
    u = 1.0 + y
    d = u - 1.0
    return jnp.where(d == 0.0, y, jnp.log(u) * (y / jnp.where(d == 0.0, 1.0, d)))


def _softplus(x):
    return jnp.maximum(x, 0.0) + _log1p(jnp.exp(-jnp.abs(x)))


def _dot(a, b):
    return jnp.dot(a, b, preferred_element_type=F32)


def _dot_nt(a, b):
    return lax.dot_general(a, b, (((1,), (1,)), ((), ())), preferred_element_type=F32)


def _dot_tn(a, b):
    return lax.dot_general(a, b, (((0,), (0,)), ((), ())), preferred_element_type=F32)


def _colsum(v):
    return jnp.sum(v, axis=0, keepdims=True)


def _layout(d_model):
    kvw = (d_model // HEAD_DIM // GROUP) * HEAD_DIM
    names = ["gA", "gB", "gC", "mA", "mB", "mC", "uA", "qB", "qC", "kB", "vB", "kC", "vC"]
    widths = [d_model] * 9 + [kvw] * 4
    off, o = {}, 0
    for n, w in zip(names, widths):
        off[n] = o
        o += w
    return off, o


def _orig_segments(d_model):
    kvw = (d_model // HEAD_DIM // GROUP) * HEAD_DIM
    names = ["uA", "gA", "qB", "kB", "vB", "gB", "qC", "kC", "vC", "gC", "mA", "mB", "mC"]
    widths = [d_model, d_model, d_model, kvw, kvw, d_model, d_model, kvw, kvw, d_model, d_model, d_model, d_model]
    out, o = [], 0
    for n, w in zip(names, widths):
        out.append((n, o, w))
        o += w
    return out


def _matmul(a, b, *, ta=False, tb=False, out_dtype=F32, bm, bn, bk, name, n_outer=False, a_lead=None, b_lead=None):
    a_shape = a.shape if a_lead is None else a.shape[1:]
    b_shape = b.shape if b_lead is None else b.shape[1:]
    (kdim, m) = a_shape if ta else a_shape[::-1]
    (n, kdim2) = b_shape if tb else b_shape[::-1]
    assert kdim == kdim2 and m % bm == 0 and n % bn == 0 and kdim % bk == 0, (a.shape, b.shape, bm, bn, bk)
    nk = kdim // bk
    dims = (((0 if ta else 1,), (1 if tb else 0,)), ((), ()))

    def ij(f):
        return (lambda j, i, k: f(i, j, k)) if n_outer else f

    def body(a_ref, b_ref, o_ref, *scratch):
        r = lax.dot_general(a_ref[...].astype(BF16), b_ref[...].astype(BF16), dims, preferred_element_type=F32)
        if nk == 1:
            o_ref[...] = r.astype(out_dtype)
        else:
            acc = scratch[0]
            k = pl.program_id(2)

            @pl.when(k == 0)
            def _():
                acc[...] = r

            @pl.when(k > 0)
            def _():
                acc[...] += r

            @pl.when(k == nk - 1)
            def _():
                o_ref[...] = acc[...].astype(out_dtype)

    def spec(shape, f, lead):
        f = ij(f)
        if lead is None:
            return pl.BlockSpec(shape, f)
        return pl.BlockSpec((None,) + shape, lambda *g: (lead,) + f(*g))

    a_spec = spec((bk, bm), lambda i, j, k: (k, i), a_lead) if ta else spec((bm, bk), lambda i, j, k: (i, k), a_lead)
    b_spec = spec((bn, bk), lambda i, j, k: (j, k), b_lead) if tb else spec((bk, bn), lambda i, j, k: (k, j), b_lead)
    return pl.pallas_call(
        body, name=name, grid=(n // bn, m // bm, nk) if n_outer else (m // bm, n // bn, nk),
        in_specs=[a_spec, b_spec], out_specs=pl.BlockSpec((bm, bn), ij(lambda i, j, k: (i, j))),
        out_shape=jax.ShapeDtypeStruct((m, n), out_dtype),
        scratch_shapes=[pltpu.VMEM((bm, bn), F32)] if nk > 1 else [],
        compiler_params=_cparams(("parallel", "parallel", "arbitrary"), VMEM_LIMIT),
    )(a, b)


def _mod_fwd(c16, w_mod, b_mod):
    d3 = w_mod.shape[1]

    def body(c_ref, w_ref, b_ref, o_ref):
        o_ref[...] = _dot(_silu(c_ref[...]).astype(BF16), w_ref[...]) + b_ref[...]

    return pl.pallas_call(body, name="mod_fwd", out_shape=jax.ShapeDtypeStruct((MOD_ROWS, d3), F32),
                          compiler_params=_cparams(None, VMEM_LIMIT))(c16, w_mod, b_mod)


def _mod_bwd(c16, dmod16, w_mod):
    d, d3 = w_mod.shape

    def body(c_ref, g_ref, w_ref, dw_ref, db_ref, dc_ref):
        c = c_ref[...]
        g = g_ref[...]
        gb = g.astype(BF16)
        dw_ref[...] = _dot_tn(_silu(c).astype(BF16), gb)
        db_ref[...] = _colsum(g)
        dc_ref[...] = _dot_nt(gb, w_ref[...]) * _dsilu(c)

    return pl.pallas_call(
        body, name="mod_bwd",
        out_shape=(jax.ShapeDtypeStruct((d, d3), F32), jax.ShapeDtypeStruct((1, d3), F32),
                   jax.ShapeDtypeStruct((MOD_ROWS, d), F32)),
        compiler_params=_cparams(None, VMEM_LIMIT))(c16, dmod16, w_mod)


def _row_kind(t, lb):
    return jnp.where(t >= lb, 1, 0)


def _norm_mod_fwd(x3, g, modsel, ctx_len):
    b, t, d = x3.shape
    bt = ROW_BLOCK
    lb = ctx_len // bt

    def body(x_ref, g_ref, m_ref, h_ref):
        x = x_ref[0]
        rstd = lax.rsqrt(jnp.mean(x * x, axis=-1, keepdims=True) + EPS)
        y = x * rstd * g_ref[...]
        h_ref[0] = (y * (1.0 + m_ref[0, 0, 1:2, :]) + m_ref[0, 0, 0:1, :]).astype(BF16)

    return pl.pallas_call(
        body, name="norm_mod_fwd", grid=(b, t // bt),
        in_specs=[pl.BlockSpec((1, bt, d), lambda i, j: (i, j, 0)),
                  pl.BlockSpec((1, d), lambda i, j: (0, 0)),
                  pl.BlockSpec((1, 1, 8, d), lambda i, j: (i, _row_kind(j, lb), 0, 0))],
        out_specs=pl.BlockSpec((1, bt, d), lambda i, j: (i, j, 0)),
        out_shape=jax.ShapeDtypeStruct((b, t, d), BF16),
        compiler_params=_cparams(("parallel", "arbitrary")),
    )(x3, g, modsel)


def _norm_mod_bwd(dh3, x3, g, modsel, dres3, ctx_len):
    b, t, d = x3.shape
    bt = ROW_BLOCK
    lb = ctx_len // bt

    def body(dh_ref, x_ref, g_ref, m_ref, dres_ref, dx_ref, acc_ref):
        j = pl.program_id(1)
        x = x_ref[0]
        dh = dh_ref[0]
        g_row = g_ref[...]
        rstd = lax.rsqrt(jnp.mean(x * x, axis=-1, keepdims=True) + EPS)
        xhat = x * rstd
        dhpre = dh * (1.0 + m_ref[0, 0, 1:2, :])
        dxhat = dhpre * g_row
        dx = rstd * (dxhat - xhat * jnp.mean(dxhat * xhat, axis=-1, keepdims=True))
        dx_ref[0] = dx + dres_ref[0]

        @pl.when((j == 0) | (j == lb))
        def _():
            acc_ref[...] = jnp.zeros_like(acc_ref)

        acc_ref[0, 0, 0:1, :] += _colsum(dh)
        acc_ref[0, 0, 1:2, :] += _colsum(dh * (xhat * g_row))
        acc_ref[0, 0, 2:3, :] += _colsum(dhpre * xhat)

    blk = pl.BlockSpec((1, bt, d), lambda i, j: (i, j, 0))
    return pl.pallas_call(
        body, name="norm_mod_bwd", grid=(b, t // bt),
        in_specs=[blk, blk, pl.BlockSpec((1, d), lambda i, j: (0, 0)),
                  pl.BlockSpec((1, 1, 8, d), lambda i, j: (i, _row_kind(j, lb), 0, 0)), blk],
        out_specs=(blk, pl.BlockSpec((1, 1, 8, d), lambda i, j: (i, _row_kind(j, lb), 0, 0))),
        out_shape=(jax.ShapeDtypeStruct((b, t, d), F32), jax.ShapeDtypeStruct((b, 2, 8, d), F32)),
        compiler_params=_cparams(("parallel", "arbitrary")),
    )(dh3, x3, g, modsel, dres3)


def _shifted_rows(ref, c, off, ctx_len, total):
    ct = LRU_CHUNK
    r0 = pl.multiple_of(c * ct, ct)
    x0 = ref[pl.ds(r0, ct), :]
    row = lax.broadcasted_iota(jnp.int32, x0.shape, 0)
    if off < 0:
        k = -off
        has = jnp.logical_and(r0 != 0, r0 != ctx_len)
        p0 = pl.multiple_of(jnp.maximum(r0 - 8, 0), 8)
        edge = jnp.where(has, ref[pl.ds(p0, 8), :], 0.0)
        out = pltpu.roll(x0, k, 0)
        for j in range(k):
            out = jnp.where(row == j, edge[8 - k + j:8 - k + j + 1, :], out)
    else:
        k = off
        has = jnp.logical_and(r0 + ct != ctx_len, r0 + ct != total)
        n0 = pl.multiple_of(jnp.minimum(r0 + ct, total - 8), 8)
        edge = jnp.where(has, ref[pl.ds(n0, 8), :], 0.0)
        out = pltpu.roll(x0, ct - k, 0)
        for j in range(k):
            out = jnp.where(row == ct - k + j, edge[j:j + 1, :], out)
    return out


def _chunk_scan(a, b, reverse):
    n = a.shape[0]
    row = lax.broadcasted_iota(jnp.int32, a.shape, 0)
    s = 1
    while s < n:
        if reverse:
            a_s, b_s, ok = pltpu.roll(a, n - s, 0), pltpu.roll(b, n - s, 0), row < n - s
        else:
            a_s, b_s, ok = pltpu.roll(a, s, 0), pltpu.roll(b, s, 0), row >= s
        b = jnp.where(ok, a * b_s + b, b)
        a = jnp.where(ok, a * a_s, a)
        s *= 2
    return a, b


def _loop_chunks(n, body, init):
    assert n % LRU_UNROLL == 0

    def group(s2, carry):
        for u in range(LRU_UNROLL):
            carry = body(LRU_UNROLL * s2 + u, carry)
        return carry

    return lax.fori_loop(0, n // LRU_UNROLL, group, init)


def _lru_order(d, s, n_ctx, n_all):
    if d == 0:
        return s
    return jnp.where(s < n_ctx, n_ctx - 1 - s, n_all - 1 - (s - n_ctx))


def _lru_gates(u, wa, ba, wx, bx, sp):
    ub = u.astype(BF16)
    r = _sigmoid(_dot(ub, wa) + ba)
    i = _sigmoid(_dot(ub, wx) + bx)
    log_a = (-LRU_C * sp) * r
    a = jnp.exp(log_a)
    sf = jnp.sqrt(_one_minus_square(log_a, a))
    return ub, r, i, a, sf


def _lru_specs(t, n_lane_blocks_offset):
    ln = LRU_LANES
    return [
        pl.BlockSpec((4, ln), lambda i, j: (0, j)),
        pl.BlockSpec((1, ln), lambda i, j: (0, j)),
        pl.BlockSpec((2, 1, ln, ln), lambda i, j: (0, j, 0, 0)),
        pl.BlockSpec((2, ln), lambda i, j: (0, j)),
        pl.BlockSpec((2, 1, ln, ln), lambda i, j: (0, j, 0, 0)),
        pl.BlockSpec((2, ln), lambda i, j: (0, j)),
        pl.BlockSpec((2, ln), lambda i, j: (0, j)),
    ]


def _lru_conv(ua_ref, cw_ref, cb_ref, u_s, ctx_len, total):
    ct = LRU_CHUNK

    def conv(c, _):
        r0 = pl.multiple_of(c * ct, ct)
        u = (cw_ref[0:1, :] * _shifted_rows(ua_ref, c, -2, ctx_len, total)
             + cw_ref[1:2, :] * _shifted_rows(ua_ref, c, -1, ctx_len, total)
             + cw_ref[2:3, :] * ua_ref[pl.ds(r0, ct), :]
             + cw_ref[3:4, :] * _shifted_rows(ua_ref, c, 1, ctx_len, total) + cb_ref[...])
        u_s[pl.ds(r0, ct), :] = u
        return 0

    lax.fori_loop(0, total // ct, conv, 0)


def _lru_fwd(proj3, col0, conv_w, conv_b, wa_bd, ba, wx_bd, bx, lam, ctx_len):
    b, t, _ = proj3.shape
    d = conv_w.shape[1]
    ln, ct = LRU_LANES, LRU_CHUNK
    n_all, n_ctx = t // ct, ctx_len // ct
    cb0 = col0 // ln

    def body(ua_ref, cw_ref, cb_ref, wa_ref, ba_ref, wx_ref, bx_ref, lam_ref, y_ref, u_s):
        ua = ua_ref.at[0]
        _lru_conv(ua, cw_ref, cb_ref, u_s, ctx_len, t)
        for dr in (0, 1):
            sp = _softplus(-lam_ref[dr:dr + 1, :])
            wa, wx = wa_ref[dr, 0], wx_ref[dr, 0]
            ba_row, bx_row = ba_ref[dr:dr + 1, :], bx_ref[dr:dr + 1, :]

            def step(s, carry, dr=dr, sp=sp, wa=wa, wx=wx, ba_row=ba_row, bx_row=bx_row):
                c = _lru_order(dr, s, n_ctx, n_all)
                r0 = pl.multiple_of(c * ct, ct)
                u = u_s[pl.ds(r0, ct), :]
                _, _, i, a, sf = _lru_gates(u, wa, ba_row, wx, bx_row, sp)
                aa, h0 = _chunk_scan(a, sf * (i * u), reverse=(dr == 1))
                h = h0 + aa * carry
                if dr == 0:
                    y_ref[0, pl.ds(r0, ct), :] = h
                    return h[ct - 1:ct, :]
                y_ref[0, pl.ds(r0, ct), :] += h
                return h[0:1, :]

            _loop_chunks(n_all, step, jnp.zeros((1, ln), F32))

    return pl.pallas_call(
        body, name="lru_fwd", grid=(b, d // ln),
        in_specs=[pl.BlockSpec((1, t, ln), lambda i, j: (i, 0, cb0 + j))] + _lru_specs(t, cb0),
        out_specs=pl.BlockSpec((1, t, ln), lambda i, j: (i, 0, j)),
        out_shape=jax.ShapeDtypeStruct((b, t, d), F32),
        scratch_shapes=[pltpu.VMEM((t, ln), F32)],
        compiler_params=_cparams(("parallel", "parallel"), VMEM_LIMIT),
    )(proj3, conv_w, conv_b, wa_bd, ba, wx_bd, bx, lam)


def _lru_bwd(dproj3, proj3, col0, dy3, conv_w, conv_b, wa_bd, ba, wx_bd, bx, lam, ctx_len):
    b, t, _ = proj3.shape
    d = conv_w.shape[1]
    ln, ct = LRU_LANES, LRU_CHUNK
    n_all, n_ctx = t // ct, ctx_len // ct
    cb0 = col0 // ln

    def body(dproj_hbm, ua_ref, dy_ref, cw_ref, cb_ref, wa_ref, ba_ref, wx_ref, bx_ref, lam_ref,
             dua_ref, vec_ref, dwa_ref, dwx_ref, u_s, h_s, du_s, a_s, sf_s, i_s, r_s):
        del dproj_hbm
        ua = ua_ref.at[0]
        _lru_conv(ua, cw_ref, cb_ref, u_s, ctx_len, t)
        du_s[...] = jnp.zeros_like(du_s)
        vec_ref[...] = jnp.zeros_like(vec_ref)
        for dr in (0, 1):
            sp = _softplus(-lam_ref[dr:dr + 1, :])
            wa, wx = wa_ref[dr, 0], wx_ref[dr, 0]
            ba_row, bx_row = ba_ref[dr:dr + 1, :], bx_ref[dr:dr + 1, :]

            def fwd(s, carry, dr=dr, sp=sp, wa=wa, wx=wx, ba_row=ba_row, bx_row=bx_row):
                c = _lru_order(dr, s, n_ctx, n_all)
                r0 = pl.multiple_of(c * ct, ct)
                u = u_s[pl.ds(r0, ct), :]
                _, r, i, a, sf = _lru_gates(u, wa, ba_row, wx, bx_row, sp)
                aa, h0 = _chunk_scan(a, sf * (i * u), reverse=(dr == 1))
                h = h0 + aa * carry
                h_s[pl.ds(r0, ct), :] = h
                a_s[pl.ds(r0, ct), :] = a
                sf_s[pl.ds(r0, ct), :] = sf
                i_s[pl.ds(r0, ct), :] = i
                r_s[pl.ds(r0, ct), :] = r
                return h[ct - 1:ct, :] if dr == 0 else h[0:1, :]

            _loop_chunks(n_all, fwd, jnp.zeros((1, ln), F32))
            dwa_ref[0, dr, 0] = jnp.zeros((ln, ln), F32)
            dwx_ref[0, dr, 0] = jnp.zeros((ln, ln), F32)

            def bwd(sr, carry, dr=dr, sp=sp, wa=wa, wx=wx):
                gc, vacc = carry
                c = _lru_order(dr, n_all - 1 - sr, n_ctx, n_all)
                r0 = pl.multiple_of(c * ct, ct)
                u = u_s[pl.ds(r0, ct), :]
                h = h_s[pl.ds(r0, ct), :]
                dy = dy_ref[0, pl.ds(r0, ct), :]
                ub = u.astype(BF16)
                r, i, a, sf = r_s[pl.ds(r0, ct), :], i_s[pl.ds(r0, ct), :], a_s[pl.ds(r0, ct), :], sf_s[pl.ds(r0, ct), :]
                row = lax.broadcasted_iota(jnp.int32, a.shape, 0)
                if dr == 0:
                    alpha = jnp.where(row == ct - 1, 1.0, pltpu.roll(a, ct - 1, 0))
                    aa, g0 = _chunk_scan(alpha, dy, reverse=True)
                    g = g0 + aa * gc
                    gc_new = a[0:1, :] * g[0:1, :]
                    p0 = pl.multiple_of(jnp.maximum(r0 - 8, 0), 8)
                    edge = jnp.where(r0 != 0, h_s[pl.ds(p0, 8), :], 0.0)[7:8, :]
                    h_prev = jnp.where(row == 0, edge, pltpu.roll(h, 1, 0))
                else:
                    alpha = jnp.where(row == 0, 1.0, pltpu.roll(a, 1, 0))
                    aa, g0 = _chunk_scan(alpha, dy, reverse=False)
                    g = g0 + aa * gc
                    gc_new = a[ct - 1:ct, :] * g[ct - 1:ct, :]
                    r_end = r0 + ct
                    n0 = pl.multiple_of(jnp.where(r_end == t, 0, jnp.minimum(r_end, t - 8)), 8)
                    edge = jnp.where(r_end != ctx_len, h_s[pl.ds(n0, 8), :], 0.0)[0:1, :]
                    h_prev = jnp.where(row == ct - 1, edge, pltpu.roll(h, ct - 1, 0))
                da = g * h_prev
                iu = i * u
                diu = g * sf
                dlog_a = da * a - (g * iu) * (a * a) / sf
                dpre_r = (dlog_a * (-LRU_C * sp)) * (r * (1.0 - r))
                dpre_i = (diu * u) * (i * (1.0 - i))
                dpr_b, dpi_b = dpre_r.astype(BF16), dpre_i.astype(BF16)
                du = diu * i + _dot_nt(dpr_b, wa) + _dot_nt(dpi_b, wx)
                du_s[pl.ds(r0, ct), :] += du
                dwa_ref[0, dr, 0] += _dot_tn(ub, dpr_b)
                dwx_ref[0, dr, 0] += _dot_tn(ub, dpi_b)
                vacc = (vacc[0] + _colsum(dpre_r), vacc[1] + _colsum(dpre_i), vacc[2] + _colsum(dlog_a * (-LRU_C * r)))
                return gc_new, vacc

            zrow = jnp.zeros((1, ln), F32)
            _, vacc = _loop_chunks(n_all, bwd, (zrow, (zrow, zrow, zrow)))
            vec_ref[0, 5 + dr:6 + dr, :] = vacc[0]
            vec_ref[0, 7 + dr:8 + dr, :] = vacc[1]
            vec_ref[0, 9 + dr:10 + dr, :] = vacc[2]

        def conv_bwd(c, acc):
            r0 = pl.multiple_of(c * ct, ct)
            du = du_s[pl.ds(r0, ct), :]
            dua = (cw_ref[0:1, :] * _shifted_rows(du_s, c, 2, ctx_len, t)
                   + cw_ref[1:2, :] * _shifted_rows(du_s, c, 1, ctx_len, t)
                   + cw_ref[2:3, :] * du
                   + cw_ref[3:4, :] * _shifted_rows(du_s, c, -1, ctx_len, t))
            dua_ref[0, pl.ds(r0, ct), :] = dua.astype(BF16)
            return (acc[0] + _colsum(du * _shifted_rows(ua, c, -2, ctx_len, t)),
                    acc[1] + _colsum(du * _shifted_rows(ua, c, -1, ctx_len, t)),
                    acc[2] + _colsum(du * ua[pl.ds(r0, ct), :]),
                    acc[3] + _colsum(du * _shifted_rows(ua, c, 1, ctx_len, t)),
                    acc[4] + _colsum(du))

        zrow = jnp.zeros((1, ln), F32)
        acc = lax.fori_loop(0, n_all, conv_bwd, (zrow,) * 5)
        for k in range(5):
            vec_ref[0, k:k + 1, :] = acc[k]

    ng = d // ln
    return pl.pallas_call(
        body, name="lru_bwd", grid=(b, ng),
        in_specs=[pl.BlockSpec(memory_space=pl.ANY),
                  pl.BlockSpec((1, t, ln), lambda i, j: (i, 0, cb0 + j)),
                  pl.BlockSpec((1, t, ln), lambda i, j: (i, 0, j))] + _lru_specs(t, cb0),
        out_specs=(pl.BlockSpec((1, t, ln), lambda i, j: (i, 0, cb0 + j)),
                   pl.BlockSpec((1, 16, ln), lambda i, j: (i, 0, j)),
                   pl.BlockSpec((1, 2, 1, ln, ln), lambda i, j: (i, 0, j, 0, 0)),
                   pl.BlockSpec((1, 2, 1, ln, ln), lambda i, j: (i, 0, j, 0, 0))),
        out_shape=(jax.ShapeDtypeStruct(dproj3.shape, dproj3.dtype),
                   jax.ShapeDtypeStruct((b, 16, d), F32),
                   jax.ShapeDtypeStruct((b, 2, ng, ln, ln), F32),
                   jax.ShapeDtypeStruct((b, 2, ng, ln, ln), F32)),
        scratch_shapes=[pltpu.VMEM((t, ln), F32)] * 7,
        input_output_aliases={0: 0},
        compiler_params=_cparams(("parallel", "parallel"), VMEM_LIMIT),
    )(dproj3, proj3, dy3, conv_w, conv_b, wa_bd, ba, wx_bd, bx, lam)


def _rope_tables(ctx_len, seq):
    p = HEAD_DIM // 4
    inv = ROPE_THETA ** (-jnp.arange(p, dtype=F32) / p)
    tok = jnp.arange(seq)
    ang_r = (tok // GRID_W)[:, None] * inv
    ang_c = (tok % GRID_W)[:, None] * inv
    cos = jnp.concatenate([jnp.cos(ang_r)] * 2 + [jnp.cos(ang_c)] * 2, axis=1)
    sin = jnp.concatenate([-jnp.sin(ang_r), jnp.sin(ang_r), -jnp.sin(ang_c), jnp.sin(ang_c)], axis=1)
    cos = jnp.concatenate([jnp.ones((ctx_len, HEAD_DIM), F32), cos], axis=0)
    sin = jnp.concatenate([jnp.zeros((ctx_len, HEAD_DIM), F32), sin], axis=0)
    return cos, sin


def _swap_halves(v):
    lane = lax.broadcasted_iota(jnp.int32, v.shape, 1)
    return jnp.where((lane & 63) < 32, pltpu.roll(v, 96, 1), pltpu.roll(v, 32, 1))


def _head_rstd(v):
    return lax.rsqrt(jnp.mean(v * v, axis=-1, keepdims=True) + EPS)


QKV_BLOCK = GROUP * HEAD_DIM
PREP_ROWS = (2176, 256)


def _prep_fwd(proj3, qcol, kvcol, d, cos, sin, gq, gk, use_norm):
    b, t, _ = proj3.shape
    bt, wb = _pick(t, PREP_ROWS), QKV_BLOCK
    nqb = d // wb
    assert qcol % wb == 0 and kvcol % wb == 0 and d // HEAD_DIM // GROUP == 2
    qb0, kvb = qcol // wb, kvcol // wb

    def body(p_ref, cos_ref, sin_ref, gq_ref, gk_ref, o_ref):
        s = pl.program_id(2)
        c, sn = cos_ref[...], sin_ref[...]

        def rope(v):
            return v * c + _swap_halves(v) * sn

        @pl.when(s < nqb)
        def _():
            for hh in range(GROUP):
                v = p_ref[0, :, hh * HEAD_DIM:(hh + 1) * HEAD_DIM]
                if use_norm:
                    v = v * _head_rstd(v) * gq_ref[...]
                o_ref[0, :, hh * HEAD_DIM:(hh + 1) * HEAD_DIM] = rope(v).astype(BF16)

        @pl.when(s == nqb)
        def _():
            for hh in range(2):
                v = p_ref[0, :, hh * HEAD_DIM:(hh + 1) * HEAD_DIM]
                if use_norm:
                    v = v * _head_rstd(v) * gk_ref[...]
                o_ref[0, :, hh * HEAD_DIM:(hh + 1) * HEAD_DIM] = rope(v).astype(BF16)
            o_ref[0, :, 2 * HEAD_DIM:] = p_ref[0, :, 2 * HEAD_DIM:].astype(BF16)

    return pl.pallas_call(
        body, name="prep_fwd_norm" if use_norm else "prep_fwd", grid=(b, t // bt, nqb + 1),
        in_specs=[pl.BlockSpec((1, bt, wb), lambda i, j, s: (i, j, jnp.where(s < nqb, qb0 + s, kvb))),
                  pl.BlockSpec((bt, HEAD_DIM), lambda i, j, s: (j, 0)),
                  pl.BlockSpec((bt, HEAD_DIM), lambda i, j, s: (j, 0)),
                  pl.BlockSpec((1, HEAD_DIM), lambda i, j, s: (0, 0)),
                  pl.BlockSpec((1, HEAD_DIM), lambda i, j, s: (0, 0))],
        out_specs=pl.BlockSpec((1, bt, wb), lambda i, j, s: (i, j, s)),
        out_shape=jax.ShapeDtypeStruct((b, t, d + wb), BF16),
        compiler_params=_cparams(("parallel", "parallel", "arbitrary"), VMEM_LIMIT),
    )(proj3, cos, sin, gq, gk)


def _prep_bwd(dproj3, dq3, dkt, dvt, proj3, qcol, kvcol, d, cos, sin, gq, gk, use_norm):
    b, t, _ = proj3.shape
    bt, wb = _pick(t, PREP_ROWS), QKV_BLOCK
    nqb = d // wb
    qb0, kvb = qcol // wb, kvcol // wb
    kvh = dkt.shape[1]

    def body(dproj_hbm, dq_ref, dkt_ref, dvt_ref, p_ref, cos_ref, sin_ref, gq_ref, gk_ref, o_ref, gacc_ref):
        del dproj_hbm
        j, s = pl.program_id(1), pl.program_id(2)
        c, sn = cos_ref[...], sin_ref[...]

        @pl.when((j == 0) & (s == 0))
        def _():
            gacc_ref[...] = jnp.zeros_like(gacc_ref)

        def unrope(dv):
            return dv * c + _swap_halves(dv * sn)

        def head_bwd(dyv, xv, g_ref, acc_row):
            dyv = unrope(dyv)
            if not use_norm:
                return dyv
            rstd = _head_rstd(xv)
            xhat = xv * rstd
            gacc_ref[0, acc_row:acc_row + 1, :] += _colsum(dyv * xhat)
            dxhat = dyv * g_ref[...]
            return rstd * (dxhat - xhat * jnp.mean(dxhat * xhat, axis=-1, keepdims=True))

        @pl.when(s < nqb)
        def _():
            for hh in range(GROUP):
                sl = slice(hh * HEAD_DIM, (hh + 1) * HEAD_DIM)
                o_ref[0, :, sl] = head_bwd(dq_ref[0, :, sl], p_ref[0, :, sl], gq_ref, 0).astype(BF16)

        @pl.when(s == nqb)
        def _():
            for hh in range(kvh):
                sl = slice(hh * HEAD_DIM, (hh + 1) * HEAD_DIM)
                o_ref[0, :, sl] = head_bwd(dkt_ref[0, hh].T, p_ref[0, :, sl], gk_ref, 1).astype(BF16)
                sv = slice((kvh + hh) * HEAD_DIM, (kvh + hh + 1) * HEAD_DIM)
                o_ref[0, :, sv] = dvt_ref[0, hh].T.astype(BF16)

    col = lambda i, j, s: (i, j, jnp.where(s < nqb, qb0 + s, kvb))
    return pl.pallas_call(
        body, name="prep_bwd_norm" if use_norm else "prep_bwd", grid=(b, t // bt, nqb + 1),
        in_specs=[pl.BlockSpec(memory_space=pl.ANY),
                  pl.BlockSpec((1, bt, wb), lambda i, j, s: (i, j, jnp.minimum(s, nqb - 1))),
                  pl.BlockSpec((1, kvh, HEAD_DIM, bt), lambda i, j, s: (i, 0, 0, j)),
                  pl.BlockSpec((1, kvh, HEAD_DIM, bt), lambda i, j, s: (i, 0, 0, j)),
                  pl.BlockSpec((1, bt, wb), col),
                  pl.BlockSpec((bt, HEAD_DIM), lambda i, j, s: (j, 0)),
                  pl.BlockSpec((bt, HEAD_DIM), lambda i, j, s: (j, 0)),
                  pl.BlockSpec((1, HEAD_DIM), lambda i, j, s: (0, 0)),
                  pl.BlockSpec((1, HEAD_DIM), lambda i, j, s: (0, 0))],
        out_specs=(pl.BlockSpec((1, bt, wb), col), pl.BlockSpec((1, 8, HEAD_DIM), lambda i, j, s: (i, 0, 0))),
        out_shape=(jax.ShapeDtypeStruct(dproj3.shape, dproj3.dtype), jax.ShapeDtypeStruct((b, 8, HEAD_DIM), F32)),
        input_output_aliases={0: 0},
        compiler_params=_cparams(("parallel", "arbitrary", "arbitrary"), VMEM_LIMIT),
    )(dproj3, dq3, dkt, dvt, proj3, cos, sin, gq, gk)


def _stack_heads(ref, dtype=None):
    parts = [ref[0, :, g * HEAD_DIM:(g + 1) * HEAD_DIM] for g in range(GROUP)]
    v = jnp.concatenate(parts, axis=0)
    return v if dtype is None else v.astype(dtype)


def _unstack_heads(ref, v, bq):
    for g in range(GROUP):
        ref[0, :, g * HEAD_DIM:(g + 1) * HEAD_DIM] = v[g * bq:(g + 1) * bq, :]


def _attn_specs(t, d, bq):
    kvh = d // HEAD_DIM // GROUP
    kc0 = d // HEAD_DIM
    q_spec = pl.BlockSpec((1, bq, QKV_BLOCK), lambda i, h, j: (i, j, h))
    k_spec = pl.BlockSpec((1, t, HEAD_DIM), lambda i, h, j: (i, 0, kc0 + h))
    v_spec = pl.BlockSpec((1, t, HEAD_DIM), lambda i, h, j: (i, 0, kc0 + kvh + h))
    lse_spec = pl.BlockSpec((1, GROUP, bq, HEAD_DIM), lambda i, h, j: (i, h, j, 0))
    kt_spec = pl.BlockSpec((1, 1, HEAD_DIM, t), lambda i, h, j: (i, h, 0, 0))
    return kvh, q_spec, k_spec, v_spec, lse_spec, kt_spec


SCALE = HEAD_DIM ** -0.5


def _attn_dense_fwd(qkv, d, ctx_len):
    b, t, _ = qkv.shape
    bq = DENSE_FWD_BQ
    lq = ctx_len // bq
    kvh, q_spec, k_spec, v_spec, lse_spec, _ = _attn_specs(t, d, bq)

    def body(q_ref, k_ref, v_ref, o_ref, lse_ref):
        i = pl.program_id(2)

        def attend(k, v):
            for g in range(GROUP):
                sl = slice(g * HEAD_DIM, (g + 1) * HEAD_DIM)
                s = _dot_nt(q_ref[0, :, sl], k)
                m = jnp.max(s, axis=1, keepdims=True)
                p = jnp.exp2((s - m) * (SCALE * LOG2E))
                l = jnp.sum(p, axis=1, keepdims=True)
                o_ref[0, :, sl] = _dot(p.astype(BF16), v) / l
                lse_ref[0, g] = jnp.broadcast_to(m * SCALE + jnp.log(l), (bq, HEAD_DIM))

        @pl.when(i < lq)
        def _():
            attend(k_ref[0, 0:ctx_len, :], v_ref[0, 0:ctx_len, :])

        @pl.when(i >= lq)
        def _():
            attend(k_ref[0], v_ref[0])

    return pl.pallas_call(
        body, name="attn_dense_fwd", grid=(b, kvh, t // bq),
        in_specs=[q_spec, k_spec, v_spec], out_specs=(q_spec, lse_spec),
        out_shape=(jax.ShapeDtypeStruct((b, t, d), F32), jax.ShapeDtypeStruct((b, kvh * GROUP, t, HEAD_DIM), F32)),
        compiler_params=_cparams(("parallel", "parallel", "arbitrary"), VMEM_LIMIT),
    )(qkv, qkv, qkv)


def _attn_dense_bwd(qkv, o3, do3, lse, d, ctx_len):
    b, t, _ = qkv.shape
    bq = ATT_BQ
    lq = ctx_len // bq
    kvh, q_spec, k_spec, v_spec, lse_spec, kt_spec = _attn_specs(t, d, bq)

    def body(q_ref, k_ref, v_ref, o_ref, do_ref, lse_ref, dq_ref, dkt_ref, dvt_ref):
        i = pl.program_id(2)

        @pl.when(i == 0)
        def _():
            dkt_ref[...] = jnp.zeros_like(dkt_ref)
            dvt_ref[...] = jnp.zeros_like(dvt_ref)

        def run(k, v, width):
            dk_acc = dv_acc = None
            for g in range(GROUP):
                sl = slice(g * HEAD_DIM, (g + 1) * HEAD_DIM)
                q = q_ref[0, :, sl]
                do = do_ref[0, :, sl]
                dd = jnp.sum(do * o_ref[0, :, sl], axis=1, keepdims=True)
                dob = do.astype(BF16)
                p = jnp.exp2(_dot_nt(q, k) * (SCALE * LOG2E) - lse_ref[0, g][:, 0:1] * LOG2E)
                ds = (p * (_dot_nt(dob, v) - dd) * SCALE).astype(BF16)
                dq_ref[0, :, sl] = _dot(ds, k)
                dk_g = _dot(q.astype(F32).T.astype(BF16), ds)
                dv_g = _dot(do.T.astype(BF16), p.astype(BF16))
                dk_acc = dk_g if dk_acc is None else dk_acc + dk_g
                dv_acc = dv_g if dv_acc is None else dv_acc + dv_g
            dkt_ref[0, 0, :, 0:width] += dk_acc
            dvt_ref[0, 0, :, 0:width] += dv_acc

        @pl.when(i < lq)
        def _():
            run(k_ref[0, 0:ctx_len, :], v_ref[0, 0:ctx_len, :], ctx_len)

        @pl.when(i >= lq)
        def _():
            run(k_ref[0], v_ref[0], t)

    return pl.pallas_call(
        body, name="attn_dense_bwd", grid=(b, kvh, t // bq),
        in_specs=[q_spec, k_spec, v_spec, q_spec, q_spec, lse_spec], out_specs=(q_spec, kt_spec, kt_spec),
        out_shape=(jax.ShapeDtypeStruct((b, t, d), F32), jax.ShapeDtypeStruct((b, kvh, HEAD_DIM, t), F32),
                   jax.ShapeDtypeStruct((b, kvh, HEAD_DIM, t), F32)),
        compiler_params=_cparams(("parallel", "parallel", "arbitrary"), VMEM_LIMIT),
    )(qkv, qkv, qkv, o3, do3, lse)


def _sink_column(sink_ref, h, bq):
    rowi = lax.broadcasted_iota(jnp.int32, (GROUP * bq, 1), 0)
    col = jnp.zeros((GROUP * bq, 1), F32)
    for g in range(GROUP):
        col = jnp.where((rowi >= g * bq) & (rowi < (g + 1) * bq), sink_ref[h * GROUP + g], col)
    return col


def _band(i, lq, ctx_len, t, bq):
    n = i - lq
    start = pl.multiple_of(jnp.clip(ctx_len + n * bq - WINDOW, ctx_len, t - WIN_SPAN), WINDOW)
    shape = (GROUP * bq, WIN_SPAN)
    kpos = start - ctx_len + lax.broadcasted_iota(jnp.int32, shape, 1)
    qpos = n * bq + (lax.broadcasted_iota(jnp.int32, shape, 0) & (bq - 1))
    return start, jnp.abs(kpos - qpos) <= WINDOW


def _attn_win_fwd(qkv, sink, d, ctx_len):
    b, t, _ = qkv.shape
    bq = WIN_BQ
    rows = GROUP * bq
    lq = ctx_len // bq
    kvh, q_spec, k_spec, v_spec, lse_spec, _ = _attn_specs(t, d, bq)

    def body(sink_ref, q_ref, k_ref, v_ref, o_ref, lse_ref):
        h, i = pl.program_id(1), pl.program_id(2)
        q4 = _stack_heads(q_ref)
        sink_col = _sink_column(sink_ref, h, bq)
        sc = _dot_nt(q4, k_ref[0, 0:ctx_len, :]) * SCALE
        mc = jnp.maximum(jnp.max(sc, axis=1, keepdims=True), sink_col)

        def finish(m, l, acc):
            _unstack_heads(o_ref, acc / l, bq)
            lse_ref[0] = jnp.broadcast_to(m + jnp.log(l), (rows, HEAD_DIM)).reshape(GROUP, bq, HEAD_DIM)

        @pl.when(i < lq)
        def _():
            pc = jnp.exp(sc - mc)
            l = jnp.sum(pc, axis=1, keepdims=True) + jnp.exp(sink_col - mc)
            finish(mc, l, _dot(pc.astype(BF16), v_ref[0, 0:ctx_len, :]))

        @pl.when(i >= lq)
        def _():
            start, ok = _band(i, lq, ctx_len, t, bq)
            sb = jnp.where(ok, _dot_nt(q4, k_ref[0, pl.ds(start, WIN_SPAN), :]) * SCALE, NEG_INF)
            m = jnp.maximum(mc, jnp.max(sb, axis=1, keepdims=True))
            pc, pb = jnp.exp(sc - m), jnp.exp(sb - m)
            l = jnp.sum(pc, axis=1, keepdims=True) + jnp.sum(pb, axis=1, keepdims=True) + jnp.exp(sink_col - m)
            acc = _dot(pc.astype(BF16), v_ref[0, 0:ctx_len, :]) + _dot(pb.astype(BF16), v_ref[0, pl.ds(start, WIN_SPAN), :])
            finish(m, l, acc)

    return pl.pallas_call(
        body, name="attn_win_fwd", grid=(b, kvh, t // bq),
        in_specs=[pl.BlockSpec(memory_space=pltpu.SMEM), q_spec, k_spec, v_spec], out_specs=(q_spec, lse_spec),
        out_shape=(jax.ShapeDtypeStruct((b, t, d), F32), jax.ShapeDtypeStruct((b, kvh * GROUP, t, HEAD_DIM), F32)),
        compiler_params=_cparams(("parallel", "parallel", "arbitrary"), VMEM_LIMIT),
    )(sink, qkv, qkv, qkv)


def _attn_win_bwd(qkv, sink, o3, do3, lse, d, ctx_len):
    b, t, _ = qkv.shape
    bq = WIN_BQ
    rows = GROUP * bq
    lq = ctx_len // bq
    kvh, q_spec, k_spec, v_spec, lse_spec, kt_spec = _attn_specs(t, d, bq)

    def body(sink_ref, q_ref, k_ref, v_ref, o_ref, do_ref, lse_ref, dq_ref, dkt_ref, dvt_ref, dsk_ref):
        h, i = pl.program_id(1), pl.program_id(2)

        @pl.when(i == 0)
        def _():
            dkt_ref[...] = jnp.zeros_like(dkt_ref)
            dvt_ref[...] = jnp.zeros_like(dvt_ref)
            dsk_ref[...] = jnp.zeros_like(dsk_ref)

        q4 = _stack_heads(q_ref)
        do4 = _stack_heads(do_ref)
        dd = jnp.sum(do4 * _stack_heads(o_ref), axis=1, keepdims=True)
        lse_col = lse_ref[0].reshape(rows, HEAD_DIM)[:, 0:1]
        do4b = do4.astype(BF16)
        qt = q4.astype(F32).T.astype(BF16)
        dot = do4.T.astype(BF16)

        def part(k, v):
            return _dot_nt(q4, k) * SCALE, _dot_nt(do4b, v)

        def grads(p, dp, k):
            ds = (p * (dp - dd) * SCALE).astype(BF16)
            return _dot(ds, k), _dot(qt, ds), _dot(dot, p.astype(BF16))

        kc = k_ref[0, 0:ctx_len, :]
        sc, dpc = part(kc, v_ref[0, 0:ctx_len, :])
        dq_c, dk_c, dv_c = grads(jnp.exp(sc - lse_col), dpc, kc)
        dkt_ref[0, 0, :, 0:ctx_len] += dk_c
        dvt_ref[0, 0, :, 0:ctx_len] += dv_c
        _unstack_heads(dq_ref, dq_c, bq)

        @pl.when(i >= lq)
        def _():
            start, ok = _band(i, lq, ctx_len, t, bq)
            kb = k_ref[0, pl.ds(start, WIN_SPAN), :]
            sb, dpb = part(kb, v_ref[0, pl.ds(start, WIN_SPAN), :])
            pb = jnp.where(ok, jnp.exp(sb - lse_col), 0.0)
            dq_b, dk_b, dv_b = grads(pb, dpb, kb)
            dkt_ref[0, 0, :, pl.ds(start, WIN_SPAN)] += dk_b
            dvt_ref[0, 0, :, pl.ds(start, WIN_SPAN)] += dv_b
            for g in range(GROUP):
                dq_ref[0, :, g * HEAD_DIM:(g + 1) * HEAD_DIM] += dq_b[g * bq:(g + 1) * bq, :]

        ps = jnp.exp(_sink_column(sink_ref, h, bq) - lse_col) * dd
        for g in range(GROUP):
            val = jnp.sum(ps[g * bq:(g + 1) * bq, :], axis=0, keepdims=True)
            dsk_ref[0, 0, g:g + 1, :] -= jnp.broadcast_to(val, (1, HEAD_DIM))

    return pl.pallas_call(
        body, name="attn_win_bwd", grid=(b, kvh, t // bq),
        in_specs=[pl.BlockSpec(memory_space=pltpu.SMEM), q_spec, k_spec, v_spec, q_spec, q_spec, lse_spec],
        out_specs=(q_spec, kt_spec, kt_spec, pl.BlockSpec((1, 1, 8, HEAD_DIM), lambda i, h, j: (i, h, 0, 0))),
        out_shape=(jax.ShapeDtypeStruct((b, t, d), F32), jax.ShapeDtypeStruct((b, kvh, HEAD_DIM, t), F32),
                   jax.ShapeDtypeStruct((b, kvh, HEAD_DIM, t), F32), jax.ShapeDtypeStruct((b, kvh, 8, HEAD_DIM), F32)),
        compiler_params=_cparams(("parallel", "parallel", "arbitrary"), VMEM_LIMIT),
    )(sink, qkv, qkv, qkv, o3, do3, lse)


MERGE_BWD_ROWS = 256
MERGE_BWD_VMEM = 60 * 1024 * 1024


def _resident(shape):
    return pl.BlockSpec(shape, lambda *_: (0,) * len(shape), pipeline_mode=pl.Buffered(1))


def _merge_fwd(x3, ya, yb, yc, proj3, w_br, w_out, modsel, ctx_len):
    b, t, d = x3.shape
    bt = ROW_BLOCK
    lb = ctx_len // bt

    def body(x_ref, ya_ref, yb_ref, yc_ref, gm_ref, wbr_ref, wo_ref, m_ref, xn_ref, out_ref):
        mix = jnp.zeros((bt, d), F32)
        for n, y_ref in enumerate((ya_ref, yb_ref, yc_ref)):
            z = (y_ref[0] * _silu(gm_ref[0, :, n * d:(n + 1) * d])).astype(BF16)
            mix = mix + _sigmoid(gm_ref[0, :, (3 + n) * d:(4 + n) * d]) * _dot(z, wbr_ref[n])
        o = _dot(mix.astype(BF16), wo_ref[...])
        out_ref[0] = o
        xn_ref[0] = x_ref[0] + m_ref[0, 0, 2:3, :] * o

    blk = pl.BlockSpec((1, bt, d), lambda i, j: (i, j, 0))
    return pl.pallas_call(
        body, name="merge_fwd", grid=(b, t // bt),
        in_specs=[blk, blk, blk, blk, pl.BlockSpec((1, bt, 6 * d), lambda i, j: (i, j, 0)),
                  _resident((3, d, d)), _resident((d, d)),
                  pl.BlockSpec((1, 1, 8, d), lambda i, j: (i, _row_kind(j, lb), 0, 0))],
        out_specs=(blk, blk),
        out_shape=(jax.ShapeDtypeStruct((b, t, d), F32), jax.ShapeDtypeStruct((b, t, d), F32)),
        compiler_params=_cparams(("parallel", "arbitrary"), VMEM_LIMIT),
    )(x3, ya, yb, yc, proj3, w_br, w_out, modsel)


def _merge_bwd(dxn3, out3, ya, yb, yc, proj3, w_br, w_out, modsel, ctx_len):
    b, t, d = dxn3.shape
    n_cols = proj3.shape[2]
    bt = MERGE_BWD_ROWS
    lb = ctx_len // bt

    def body(dxn_ref, out_ref, ya_ref, yb_ref, yc_ref, gm_ref, wbr_ref, wo_ref, m_ref,
             dgm_ref, dya_ref, dyb_ref, dyc_ref, z_ref, dt_ref, mix_ref, dout_ref, gacc_ref):
        j = pl.program_id(1)
        dxn = dxn_ref[0]
        doutb = (m_ref[0, 0, 2:3, :] * dxn).astype(BF16)
        dout_ref[0] = doutb

        @pl.when((j == 0) | (j == lb))
        def _():
            gacc_ref[...] = jnp.zeros_like(gacc_ref)

        gacc_ref[0, 0, 0:1, :] += _colsum(dxn * out_ref[0])
        dmix = _dot_nt(doutb, wo_ref[...])
        mix = jnp.zeros((bt, d), F32)
        for n, (y_ref, dy_ref) in enumerate(((ya_ref, dya_ref), (yb_ref, dyb_ref), (yc_ref, dyc_ref))):
            g = gm_ref[0, :, n * d:(n + 1) * d]
            y = y_ref[0]
            sig_g = _sigmoid(g)
            silu_g = g * sig_g
            z = (y * silu_g).astype(BF16)
            z_ref[n, 0] = z
            tn = _dot(z, wbr_ref[n])
            s = _sigmoid(gm_ref[0, :, (3 + n) * d:(4 + n) * d])
            mix = mix + s * tn
            dgm_ref[0, :, (3 + n) * d:(4 + n) * d] = (dmix * tn * (s * (1.0 - s))).astype(BF16)
            dtb = (dmix * s).astype(BF16)
            dt_ref[n, 0] = dtb
            dz = _dot_nt(dtb, wbr_ref[n])
            dy_ref[0] = dz * silu_g
            dgm_ref[0, :, n * d:(n + 1) * d] = (dz * y * (sig_g * (1.0 + g * (1.0 - sig_g)))).astype(BF16)
        mix_ref[0] = mix.astype(BF16)

    blk = pl.BlockSpec((1, bt, d), lambda i, j: (i, j, 0))
    blk4 = pl.BlockSpec((3, 1, bt, d), lambda i, j: (0, i, j, 0))
    wide = pl.BlockSpec((1, bt, 6 * d), lambda i, j: (i, j, 0))
    return pl.pallas_call(
        body, name="merge_bwd", grid=(b, t // bt),
        in_specs=[blk, blk, blk, blk, blk, wide, _resident((3, d, d)), _resident((d, d)),
                  pl.BlockSpec((1, 1, 8, d), lambda i, j: (i, _row_kind(j, lb), 0, 0))],
        out_specs=(wide, blk, blk, blk, blk4, blk4, blk, blk,
                   pl.BlockSpec((1, 1, 8, d), lambda i, j: (i, _row_kind(j, lb), 0, 0))),
        out_shape=(jax.ShapeDtypeStruct((b, t, n_cols), BF16),
                   jax.ShapeDtypeStruct((b, t, d), F32), jax.ShapeDtypeStruct((b, t, d), F32),
                   jax.ShapeDtypeStruct((b, t, d), F32),
                   jax.ShapeDtypeStruct((3, b, t, d), BF16), jax.ShapeDtypeStruct((3, b, t, d), BF16),
                   jax.ShapeDtypeStruct((b, t, d), BF16), jax.ShapeDtypeStruct((b, t, d), BF16),
                   jax.ShapeDtypeStruct((b, 2, 8, d), F32)),
        compiler_params=_cparams(("parallel", "arbitrary"), MERGE_BWD_VMEM),
    )(dxn3, out3, ya, yb, yc, proj3, w_br, w_out, modsel)


def _final(x3, g, target, ctx_len):
    b, t, d = x3.shape
    bt = ROW_BLOCK
    lb = ctx_len // bt

    def body(x_ref, g_ref, t_ref, dx_ref, loss_ref, dg_ref):
        j = pl.program_id(1)

        @pl.when(j == 0)
        def _():
            loss_ref[...] = jnp.zeros_like(loss_ref)
            dg_ref[...] = jnp.zeros_like(dg_ref)

        @pl.when(j < lb)
        def _():
            dx_ref[...] = jnp.zeros_like(dx_ref)

        @pl.when(j >= lb)
        def _():
            x = x_ref[0]
            g_row = g_ref[...]
            rstd = lax.rsqrt(jnp.mean(x * x, axis=-1, keepdims=True) + EPS)
            xhat = x * rstd
            err = xhat * g_row - t_ref[0]
            loss_ref[...] += (0.5 / d) * jnp.sum(err * err)
            dy = err * (1.0 / d)
            dg_ref[0, 0:1, :] += _colsum(dy * xhat)
            dxhat = dy * g_row
            dx_ref[0] = rstd * (dxhat - xhat * jnp.mean(dxhat * xhat, axis=-1, keepdims=True))

    blk = pl.BlockSpec((1, bt, d), lambda i, j: (i, j, 0))
    return pl.pallas_call(
        body, name="final_loss", grid=(b, t // bt),
        in_specs=[blk, pl.BlockSpec((1, d), lambda i, j: (0, 0)),
                  pl.BlockSpec((1, bt, d), lambda i, j: (i, jnp.maximum(j - lb, 0), 0))],
        out_specs=(blk, pl.BlockSpec((1, 8, HEAD_DIM), lambda i, j: (i, 0, 0)), pl.BlockSpec((1, 8, d), lambda i, j: (i, 0, 0))),
        out_shape=(jax.ShapeDtypeStruct((b, t, d), F32), jax.ShapeDtypeStruct((b, 8, HEAD_DIM), F32),
                   jax.ShapeDtypeStruct((b, 8, d), F32)),
        compiler_params=_cparams(("parallel", "arbitrary")),
    )(x3, g, target)


TOKEN_BLOCKS = (1088, 512, 256, 128)


def _pick(n, options):
    for o in options:
        if n % o == 0:
            return o
    raise ValueError((n, options))


def _block_diag(w):
    per = LRU_LANES // LRU_BLOCK_W
    nd, nb, bw, _ = w.shape
    wr = w.reshape(nd, nb // per, per, bw, bw)
    eye = jnp.eye(per, dtype=w.dtype)
    bd = wr[:, :, :, :, None, :] * eye[None, None, :, None, :, None]
    return bd.reshape(nd, nb // per, per * bw, per * bw).astype(BF16)


def _block_diag_grad(g):
    per = LRU_LANES // LRU_BLOCK_W
    nd, ng, _, _ = g.shape
    gr = g.reshape(nd, ng, per, LRU_BLOCK_W, per, LRU_BLOCK_W)
    diag = jnp.stack([gr[:, :, k, :, k, :] for k in range(per)], axis=2)
    return diag.reshape(nd, ng * per, LRU_BLOCK_W, LRU_BLOCK_W)


def _mod_select(mod16, b, d):
    m3 = mod16.reshape(MOD_ROWS, 3, d)
    lat = m3[:b]
    ctx = jnp.broadcast_to(m3[b][None], (b, 3, d))
    sel = jnp.stack([ctx, lat], axis=1)
    return jnp.pad(sel, ((0, 0), (0, 0), (0, 5), (0, 0)))


def _layer_fwd(x3, c16, p, cos, sin, ctx_len):
    b, t, d = x3.shape
    off, n_cols = _layout(d)
    mod16 = _mod_fwd(c16, p["w_mod"], p["b_mod"])
    modsel = _mod_select(mod16, b, d)
    h = _norm_mod_fwd(x3, p["norm_g"], modsel, ctx_len)
    proj = _matmul(h.reshape(b * t, d), p["w_in"], bm=_pick(b * t, TOKEN_BLOCKS), bn=1024, bk=d, name="proj_fwd",
                   n_outer=True, b_lead=p["li"])
    proj3 = proj.reshape(b, t, n_cols)
    ya = _lru_fwd(proj3, off["uA"], p["conv_w"], p["conv_b"], p["wa_bd"], p["ba"], p["wx_bd"], p["bx"], p["lam"], ctx_len)
    qkv_b = _prep_fwd(proj3, off["qB"], off["kB"], d, cos, sin, p["gq"], p["gk"], use_norm=False)
    yb, lse_b = _attn_win_fwd(qkv_b, p["sink"], d, ctx_len)
    qkv_c = _prep_fwd(proj3, off["qC"], off["kC"], d, cos, sin, p["gq"], p["gk"], use_norm=True)
    yc, lse_c = _attn_dense_fwd(qkv_c, d, ctx_len)
    x_new, out3 = _merge_fwd(x3, ya, yb, yc, proj3, p["w_br"], p["w_out"], modsel, ctx_len)
    return x_new, (x3, modsel, h, proj3, ya, yb, yc, qkv_b, lse_b, qkv_c, lse_c, out3)


def _layer_bwd(dxn3, saved, c16, p, cos, sin, ctx_len):
    x3, modsel, h, proj3, ya, yb, yc, qkv_b, lse_b, qkv_c, lse_c, out3 = saved
    b, t, d = x3.shape
    off, n_cols = _layout(d)
    rows = b * t
    bk = _pick(rows, TOKEN_BLOCKS)
    dproj3, dya, dyb, dyc, z4, dt4, mixb, doutb, gacc = _merge_bwd(dxn3, out3, ya, yb, yc, proj3, p["w_br"], p["w_out"],
                                                                   modsel, ctx_len)
    dw_br = jnp.stack([_matmul(z4.reshape(3, rows, d), dt4.reshape(3, rows, d), ta=True, bm=d, bn=d, bk=bk,
                               name="dw_branch", a_lead=n, b_lead=n) for n in range(3)])
    dw_out = _matmul(mixb.reshape(rows, d), doutb.reshape(rows, d), ta=True, bm=d, bn=d, bk=bk, name="dw_out")
    dproj3, vec, dwa, dwx = _lru_bwd(dproj3, proj3, off["uA"], dya, p["conv_w"], p["conv_b"], p["wa_bd"], p["ba"],
                                     p["wx_bd"], p["bx"], p["lam"], ctx_len)
    dq_b, dkt_b, dvt_b, dsk = _attn_win_bwd(qkv_b, p["sink"], yb, dyb, lse_b, d, ctx_len)
    dproj3, _ = _prep_bwd(dproj3, dq_b, dkt_b, dvt_b, proj3, off["qB"], off["kB"], d, cos, sin, p["gq"], p["gk"], False)
    dq_c, dkt_c, dvt_c = _attn_dense_bwd(qkv_c, yc, dyc, lse_c, d, ctx_len)
    dproj3, gqk = _prep_bwd(dproj3, dq_c, dkt_c, dvt_c, proj3, off["qC"], off["kC"], d, cos, sin, p["gq"], p["gk"], True)
    dproj2 = dproj3.reshape(rows, n_cols)
    dw_in = _matmul(h.reshape(rows, d), dproj2, ta=True, bm=d, bn=1024, bk=bk, name="dw_in")
    dh = _matmul(dproj2, p["w_in"], tb=True, bm=_pick(rows, TOKEN_BLOCKS), bn=d, bk=1024, name="dh", b_lead=p["li"])
    dx3, nacc = _norm_mod_bwd(dh.reshape(b, t, d), x3, p["norm_g"], modsel, dxn3, ctx_len)
    per = jnp.stack([nacc[:, :, 0], nacc[:, :, 1], gacc[:, :, 0]], axis=2)
    dmod = jnp.concatenate([per[:, 1].reshape(b, 3 * d), jnp.sum(per[:, 0], axis=0).reshape(1, 3 * d)], axis=0)
    dmod16 = jnp.pad(dmod, ((0, MOD_ROWS - b - 1), (0, 0)))
    dw_mod, db_mod, dc16 = _mod_bwd(c16, dmod16, p["w_mod"])
    vsum = jnp.sum(vec, axis=0)
    grads = {
        "norm_g": jnp.sum(nacc[:, :, 2], axis=(0, 1)),
        "w_mod": dw_mod, "b_mod": db_mod[0], "w_in": dw_in,
        "conv_w": vsum[0:4], "conv_b": vsum[4],
        "lru_wa": _block_diag_grad(jnp.sum(dwa, axis=0)), "lru_ba": vsum[5:7],
        "lru_wx": _block_diag_grad(jnp.sum(dwx, axis=0)), "lru_bx": vsum[7:9],
        "lru_lambda": vsum[9:11] * (-jax.nn.sigmoid(-p["lam"])),
        "attn_sink": jnp.sum(dsk[:, :, 0:GROUP, 0], axis=0).reshape(-1),
        "q_norm_g": jnp.sum(gqk[:, 0], axis=0), "k_norm_g": jnp.sum(gqk[:, 1], axis=0),
        "w_branch": dw_br, "w_out": dw_out,
    }
    return dx3, dc16, grads


def _reorder_in_cols(w, d, inverse=False):
    off_new, _ = _layout(d)
    segs = _orig_segments(d)
    if inverse:
        return jnp.concatenate([w[..., off_new[n]:off_new[n] + wd] for n, _, wd in segs], axis=-1)
    by_name = {n: (o, wd) for n, o, wd in segs}
    order = sorted(off_new, key=off_new.get)
    return jnp.concatenate([w[..., by_name[n][0]:by_name[n][0] + by_name[n][1]] for n in order], axis=-1)


def _layer_params(li, w):
    return {
        "li": li, "norm_g": w["norm_g"][li][None], "w_mod": w["w_mod"][li], "b_mod": w["b_mod"][li][None],
        "w_in": w["w_in_r"],
        "conv_w": w["conv_w"][li], "conv_b": w["conv_b"][li][None],
        "wa_bd": _block_diag(w["lru_wa"][li]), "ba": w["lru_ba"][li],
        "wx_bd": _block_diag(w["lru_wx"][li]), "bx": w["lru_bx"][li], "lam": w["lru_lambda"][li],
        "sink": w["attn_sink"][li], "gq": w["q_norm_g"][li][None], "gk": w["k_norm_g"][li][None],
        "w_br": w["w_branch"][li], "w_out": w["w_out"][li],
    }


def _local_step(x, c, ctx, target, c_ctx, final_g, layers):
    b, s, d = x.shape
    ctx_len = ctx.shape[1]
    cos, sin = _rope_tables(ctx_len, s)
    x3 = jnp.concatenate([ctx, x], axis=1)
    c16 = jnp.concatenate([c, c_ctx[None], jnp.zeros((MOD_ROWS - b - 1, d), F32)], axis=0)
    saved = []
    for p in layers:
        x3, sv = _layer_fwd(x3, c16, p, cos, sin, ctx_len)
        saved.append(sv)
    dx3, loss_acc, dgf = _final(x3, final_g[None], target, ctx_len)
    grads = [None] * len(layers)
    dc_ctx = jnp.zeros((d,), F32)
    for li in reversed(range(len(layers))):
        dx3, dc16, grads[li] = _layer_bwd(dx3, saved[li], c16, layers[li], cos, sin, ctx_len)
        dc_ctx = dc_ctx + dc16[b]
    return jnp.sum(loss_acc[:, 0, 0]), dx3[:, ctx_len:], dc_ctx, jnp.sum(dgf[:, 0], axis=0), grads


N_CHIPS = 4
ANY = pl.BlockSpec(memory_space=pl.ANY)


def _place():
    x, y, c = lax.axis_index("x"), lax.axis_index("y"), lax.axis_index("c")
    return x, y, c, [(1 - x, y), (x, 1 - y), (1 - x, 1 - y)]


def _axis_part(ref, axis, start, size):
    idx = [slice(None)] * len(ref.shape)
    idx[axis] = pl.ds(start, size)
    return ref.at[tuple(idx)]


def _remote(src, dst, send, recv, dev):
    return pltpu.make_async_remote_copy(src_ref=src, dst_ref=dst, send_sem=send, recv_sem=recv, device_id=dev,
                                        device_id_type=MESH)


COPY_PIECES = 8


def _pieces(src, dst):
    shape = src.shape
    for ax in range(len(shape) - 1):
        if shape[ax] % COPY_PIECES == 0 and shape[ax] // COPY_PIECES >= 8:
            sz = shape[ax] // COPY_PIECES
            return [(_axis_part(src, ax, j * sz, sz), _axis_part(dst, ax, j * sz, sz)) for j in range(COPY_PIECES)]
    return [(src, dst)]


def _gather_chips(wholes, axes, name):
    n = len(wholes)

    def body(*refs):
        bufs = refs[n:2 * n]
        send, recv, fsend, frecv = refs[2 * n:]
        x, y, c, chips = _place()
        me = 2 * x + y
        sib = (x, y, 1 - c)

        def block(i, chip_index, half):
            sz = wholes[i].shape[axes[i]] // N_CHIPS
            hl = wholes[i].shape[0] // 2
            return _axis_part(bufs[i], axes[i], chip_index * sz, sz).at[pl.ds(half * hl, hl)]

        for i in range(n):
            for k, (px, py) in enumerate(chips):
                _remote(block(i, me, c), block(i, me, c), send.at[i, k], recv.at[i, k], (px, py, c)).start()
        for i in range(n):
            for k, (px, py) in enumerate(chips):
                landed = block(i, 2 * px + py, c)
                _remote(landed, landed, send.at[i, k], recv.at[i, k], (px, py, c)).wait_recv()
                for s_, d_ in _pieces(landed, landed):
                    _remote(s_, d_, fsend.at[i, k], frecv.at[i, k], sib).start()
        for i in range(n):
            for k, (px, py) in enumerate(chips):
                passed = _remote(block(i, 2 * px + py, c), block(i, 2 * px + py, 1 - c), fsend.at[i, k], frecv.at[i, k], sib)
                passed.wait_recv()
                passed.wait_send()
                _remote(block(i, me, c), block(i, me, c), send.at[i, k], recv.at[i, k], (px, py, c)).wait_send()

    sems = pltpu.SemaphoreType.DMA((n, 3))
    return pl.pallas_call(
        body, name=name, in_specs=[ANY] * n, out_specs=tuple([ANY] * n),
        out_shape=tuple(jax.ShapeDtypeStruct(a.shape, a.dtype) for a in wholes),
        input_output_aliases={i: i for i in range(n)},
        scratch_shapes=[sems, sems, sems, sems],
    )(*wholes)


def _place_shard(shard, axis, dtype, name):
    cols_mode = axis == shard.ndim - 1
    assert cols_mode or axis == shard.ndim - 2
    sv = shard.reshape(-1, shard.shape[-2], shard.shape[-1])
    la, r, c = sv.shape
    br = _row_block(r, c)
    per = r // br

    def body(x_ref, o_ref):
        o_ref[...] = x_ref[...].astype(dtype)

    if cols_mode:
        out_shape, out_map = (la, r, N_CHIPS * c), lambda l, i: (l, i, _chip())
    else:
        out_shape, out_map = (la, N_CHIPS * r, c), lambda l, i: (l, _chip() * per + i, 0)
    out = pl.pallas_call(
        body, name=name, grid=(la, per),
        in_specs=[pl.BlockSpec((1, br, c), lambda l, i: (l, i, 0))], out_specs=pl.BlockSpec((1, br, c), out_map),
        out_shape=jax.ShapeDtypeStruct(out_shape, dtype), compiler_params=_cparams(("parallel", "parallel")))(sv)
    shape = list(shard.shape)
    shape[axis] *= N_CHIPS
    return out.reshape(tuple(shape))


def _split_cores(gs, name):
    n = len(gs)

    def body(*refs):
        ins, got = refs[:n], refs[n:2 * n]
        send, recv = refs[2 * n:]
        x, y, c, _ = _place()
        sib = (x, y, 1 - c)

        def theirs(i):
            hl = gs[i].shape[0] // 2
            return ins[i].at[pl.ds((1 - c) * hl, hl)]

        for i in range(n):
            for s_, d_ in _pieces(theirs(i), got[i]):
                _remote(s_, d_, send.at[i], recv.at[i], sib).start()
        for i in range(n):
            _remote(theirs(i), got[i], send.at[i], recv.at[i], sib).wait()

    return pl.pallas_call(
        body, name=name, in_specs=[ANY] * n, out_specs=tuple([ANY] * n),
        out_shape=tuple(jax.ShapeDtypeStruct((g.shape[0] // 2,) + g.shape[1:], g.dtype) for g in gs),
        scratch_shapes=[pltpu.SemaphoreType.DMA((n,)), pltpu.SemaphoreType.DMA((n,))],
    )(*gs)


def _scatter_chips(pbs, axes, name):
    n = len(pbs)

    def block(p, ax):
        shape = list(p.shape)
        shape[ax] //= N_CHIPS
        return tuple(shape)

    def body(*refs):
        inb, got = refs[:n], refs[n:2 * n]
        send, recv = refs[2 * n:]
        x, y, c, chips = _place()

        def part(i, chip_index):
            sz = pbs[i].shape[axes[i]] // N_CHIPS
            return _axis_part(inb[i], axes[i], chip_index * sz, sz)

        for i in range(n):
            for k, (px, py) in enumerate(chips):
                _remote(part(i, 2 * px + py), got[i].at[k], send.at[i, k], recv.at[i, k], (px, py, c)).start()
        for i in range(n):
            for k, (px, py) in enumerate(chips):
                _remote(part(i, 2 * px + py), got[i].at[k], send.at[i, k], recv.at[i, k], (px, py, c)).wait()

    return pl.pallas_call(
        body, name=name, in_specs=[ANY] * n, out_specs=tuple([ANY] * n),
        out_shape=tuple(jax.ShapeDtypeStruct((3,) + block(p, ax), p.dtype) for p, ax in zip(pbs, axes)),
        scratch_shapes=[pltpu.SemaphoreType.DMA((n, 3)), pltpu.SemaphoreType.DMA((n, 3))],
    )(*pbs)


def _join_cores(bufs, name):
    n = len(bufs)

    def body(*refs):
        outs = refs[n:2 * n]
        send, recv = refs[2 * n:]
        x, y, c, _ = _place()
        sib = (x, y, 1 - c)

        def half(i, which):
            hl = bufs[i].shape[0] // 2
            return outs[i].at[pl.ds(which * hl, hl)]

        for i in range(n):
            for s_, d_ in _pieces(half(i, c), half(i, c)):
                _remote(s_, d_, send.at[i], recv.at[i], sib).start()
        for i in range(n):
            cp = _remote(half(i, c), half(i, 1 - c), send.at[i], recv.at[i], sib)
            cp.wait_recv()
            cp.wait_send()

    return pl.pallas_call(
        body, name=name, in_specs=[ANY] * n, out_specs=tuple([ANY] * n),
        out_shape=tuple(jax.ShapeDtypeStruct(a.shape, a.dtype) for a in bufs),
        input_output_aliases={i: i for i in range(n)},
        scratch_shapes=[pltpu.SemaphoreType.DMA((n,)), pltpu.SemaphoreType.DMA((n,))],
    )(*bufs)


def _all_reduce_small(buf):
    r = buf.shape[0]

    def body(in_ref, out_ref, sib_buf, chip_sum, got, send, recv):
        x, y, c, chips = _place()
        cp = _remote(in_ref, sib_buf, send.at[0], recv.at[0], (x, y, 1 - c))
        cp.start()
        cp.wait()
        chip_sum[...] = in_ref[...] + sib_buf[...]
        cps = [_remote(chip_sum, got.at[k], send.at[1 + k], recv.at[1 + k], (px, py, c)) for k, (px, py) in enumerate(chips)]
        for cp in cps:
            cp.start()
        for cp in cps:
            cp.wait()
        out_ref[...] = (chip_sum[...] + got[0]) + (got[1] + got[2])

    return pl.pallas_call(
        body, name="all_reduce_small", out_shape=jax.ShapeDtypeStruct(buf.shape, F32),
        in_specs=[pl.BlockSpec(memory_space=pltpu.VMEM)], out_specs=pl.BlockSpec(memory_space=pltpu.VMEM),
        scratch_shapes=[pltpu.VMEM((r, 128), F32), pltpu.VMEM((r, 128), F32), pltpu.VMEM((3, r, 128), F32),
                        pltpu.SemaphoreType.DMA((4,)), pltpu.SemaphoreType.DMA((4,))],
        compiler_params=_cparams(None, VMEM_LIMIT),
    )(buf)


ELEMENTWISE_BLOCK_BYTES = 1 << 20


def _view2d(a):
    cols = a.shape[-1] if a.ndim > 1 else 128
    return a.reshape(-1, cols)


def _row_block(rows, cols):
    want = max(8, ELEMENTWISE_BLOCK_BYTES // (4 * cols))
    br = rows
    while br > want and br % 2 == 0 and (br // 2) % 16 == 0:
        br //= 2
    return br


def _core():
    return lax.axis_index("c")


def _chip():
    return 2 * lax.axis_index("x") + lax.axis_index("y")


def _sum_half(g, got, name):
    h = got.shape[0]
    gv = g.reshape(2 * h, -1, g.shape[-1])
    tv = got.reshape(h, -1, g.shape[-1])
    _, rows, cols = tv.shape
    br = _row_block(rows, cols)

    def body(g_ref, t_ref, p_ref, pb_ref):
        p = g_ref[...] + t_ref[...]
        p_ref[...] = p
        pb_ref[...] = p.astype(BF16)

    blk = pl.BlockSpec((1, br, cols), lambda l, i: (l, i, 0))
    p, pb = pl.pallas_call(
        body, name=name, grid=(h, rows // br),
        in_specs=[pl.BlockSpec((1, br, cols), lambda l, i: (_core() * h + l, i, 0)), blk], out_specs=(blk, blk),
        out_shape=(jax.ShapeDtypeStruct(tv.shape, F32), jax.ShapeDtypeStruct(tv.shape, BF16)),
        compiler_params=_cparams(("parallel", "parallel")))(gv, tv)
    return p.reshape(got.shape), pb.reshape(got.shape)


def _sum_blocks(p, got3, axis, name):
    h = p.shape[0]
    blk_shape = got3.shape[1:]
    cols_mode = axis == p.ndim - 1
    pv = p.reshape(-1, p.shape[-2], p.shape[-1])
    tv = got3.reshape(3, -1, blk_shape[-2], blk_shape[-1])
    la, rb, cb = tv.shape[1:]
    assert cols_mode or axis == p.ndim - 2
    br = _row_block(rb, cb)
    per = rb // br

    def body(p_ref, a_ref, b_ref, c_ref, out_ref):
        out_ref[...] = ((p_ref[...] + a_ref[0].astype(F32)) + b_ref[0].astype(F32)) + c_ref[0].astype(F32)

    if cols_mode:
        p_spec = pl.BlockSpec((1, br, cb), lambda l, i: (l, i, _chip()))
    else:
        p_spec = pl.BlockSpec((1, br, cb), lambda l, i: (l, _chip() * per + i, 0))
    out = pl.pallas_call(
        body, name=name, grid=(la, per),
        in_specs=[p_spec] + [pl.BlockSpec((1, 1, br, cb), lambda l, i, k=k: (k, l, i, 0)) for k in range(3)],
        out_specs=pl.BlockSpec((1, br, cb), lambda l, i: (_core() * la + l, i, 0)),
        out_shape=jax.ShapeDtypeStruct((2 * la, rb, cb), F32),
        compiler_params=_cparams(("parallel", "parallel")))(pv, tv, tv, tv)
    return out.reshape((2 * h,) + blk_shape[1:])


def _adamw(w, g, m, v, name):
    shape = w.shape
    ops = [_view2d(a) for a in (w, g, m, v)]
    rows, cols = ops[0].shape
    br = _row_block(rows, cols)
    c1 = 1.0 - ADAM_B1 ** ADAM_STEP
    c2 = 1.0 - ADAM_B2 ** ADAM_STEP

    def body(w_ref, g_ref, m_ref, v_ref, d_ref, nm_ref, nv_ref):
        g_ = g_ref[...]
        nm = ADAM_B1 * m_ref[...] + (1.0 - ADAM_B1) * g_
        nv = ADAM_B2 * v_ref[...] + (1.0 - ADAM_B2) * (g_ * g_)
        d_ref[...] = -ADAM_LR * ((nm / c1) / (jnp.sqrt(nv / c2) + ADAM_EPS) + ADAM_WD * w_ref[...])
        nm_ref[...] = nm
        nv_ref[...] = nv

    blk = pl.BlockSpec((br, cols), lambda i: (i, 0))
    outs = pl.pallas_call(body, name=name, grid=(rows // br,), in_specs=[blk] * 4, out_specs=(blk, blk, blk),
                          out_shape=tuple(jax.ShapeDtypeStruct((rows, cols), F32) for _ in range(3)),
                          compiler_params=_cparams(("parallel",)))(*ops)
    return tuple(o.reshape(shape) for o in outs)


PACK_ROWS = 2048


def _packed_rows(shape):
    return -(-int(np.prod(shape)) // (8 * 128)) * 8


def _pack(arrays):
    pieces = []
    for a in arrays:
        flat = a.reshape(-1)
        pieces.append(jnp.pad(flat, (0, _packed_rows(a.shape) * 128 - flat.shape[0])).reshape(-1, 128))
    rows = sum(p.shape[0] for p in pieces)
    pieces.append(jnp.zeros((-rows % PACK_ROWS, 128), F32))
    return jnp.concatenate(pieces, axis=0)


def _unpack(buf, shapes):
    out, o = [], 0
    for s in shapes:
        n, rows = int(np.prod(s)), _packed_rows(s)
        out.append(buf[o:o + rows].reshape(-1)[:n].reshape(s))
        o += rows
    return out


WEIGHTS = ["c_ctx", "norm_g", "w_mod", "b_mod", "w_in", "conv_w", "conv_b", "lru_wa", "lru_ba", "lru_wx", "lru_bx",
           "lru_lambda", "attn_sink", "q_norm_g", "k_norm_g", "w_branch", "w_out", "final_g"]
BIG = {"w_mod": 2, "w_in": 2, "w_branch": 2, "w_out": 1}
SMALL_SHARDED = ["conv_w", "lru_ba", "lru_bx", "lru_lambda"]
REPLICATED = [n for n in WEIGHTS if n not in BIG and n not in SMALL_SHARDED]


def kernel(x, c, ctx, c_ctx, norm_g, w_mod, b_mod, w_in, conv_w, conv_b, lru_wa, lru_ba, lru_wx, lru_bx, lru_lambda, attn_sink, q_norm_g, k_norm_g, w_branch, w_out, final_g, loss_target, m_c_ctx, m_norm_g, m_w_mod, m_b_mod, m_w_in, m_conv_w, m_conv_b, m_lru_wa, m_lru_ba, m_lru_wx, m_lru_bx, m_lru_lambda, m_attn_sink, m_q_norm_g, m_k_norm_g, m_w_branch, m_w_out, m_final_g, v_c_ctx, v_norm_g, v_w_mod, v_b_mod, v_w_in, v_conv_w, v_conv_b, v_lru_wa, v_lru_ba, v_lru_wx, v_lru_bx, v_lru_lambda, v_attn_sink, v_q_norm_g, v_k_norm_g, v_w_branch, v_w_out, v_final_g):
    args = dict(locals())
    w = {n: args[n] for n in WEIGHTS}
    mom = {n: args["m_" + n] for n in WEIGHTS}
    var = {n: args["v_" + n] for n in WEIGHTS}
    depth, d = norm_g.shape
    chip = 2 * lax.axis_index("x") + lax.axis_index("y")

    big_names = list(BIG)
    small_shard = jnp.concatenate([w[n] for n in SMALL_SHARDED], axis=1)
    gather_axes = [BIG[n] for n in big_names] + [2]
    placed = [_place_shard(w[n], BIG[n], BF16, "place_" + n) for n in big_names]
    placed.append(_place_shard(small_shard, 2, F32, "place_small"))
    gathered = _gather_chips(placed, gather_axes, "gather_weights")
    whole = dict(w)
    whole.update(dict(zip(big_names, gathered[:-1])))
    o = 0
    for n in SMALL_SHARDED:
        rows = w[n].shape[1]
        whole[n] = gathered[-1][:, o:o + rows]
        o += rows
    whole["w_in_r"] = _reorder_in_cols(whole["w_in"], d)
    layers = [_layer_params(li, whole) for li in range(depth)]

    loss_local, grad_x, g_c_ctx, g_final, lgrads = _local_step(x, c, ctx, loss_target, c_ctx, final_g, layers)
    loss = lax.psum(loss_local, ("x", "y", "c"))
    full = {n: jnp.stack([lg[n] for lg in lgrads]) for n in lgrads[0]}
    full["w_in"] = _reorder_in_cols(full["w_in"], d, inverse=True)
    full["c_ctx"], full["final_g"] = g_c_ctx, g_final

    bigs = [full[n] for n in big_names]
    got = _split_cores(bigs, "grad_split_cores")
    parts = [_sum_half(g, t_, "grad_chip_sum") for g, t_ in zip(bigs, got)]
    recv = _scatter_chips([pb for _, pb in parts], [BIG[n] for n in big_names], "grad_scatter_chips")
    totals = [_sum_blocks(p_, r, BIG[n], "grad_total") for (p_, _), r, n in zip(parts, recv, big_names)]
    grad = dict(zip(big_names, _join_cores(totals, "grad_join_cores")))

    small_names = REPLICATED + SMALL_SHARDED
    reduced = _unpack(_all_reduce_small(_pack([full[n] for n in small_names])), [full[n].shape for n in small_names])
    for n, g in zip(small_names, reduced):
        if n in SMALL_SHARDED:
            sz = w[n].shape[-1]
            g = lax.dynamic_slice_in_dim(g, chip * sz, sz, axis=g.ndim - 1)
        grad[n] = g

    delta, new_m, new_v = {}, {}, {}
    for n in big_names:
        delta[n], new_m[n], new_v[n] = _adamw(w[n], grad[n], mom[n], var[n], "adamw_" + n)
    shapes = [w[n].shape for n in small_names]
    packed = _adamw(_pack([w[n] for n in small_names]), _pack([grad[n] for n in small_names]),
                    _pack([mom[n] for n in small_names]), _pack([var[n] for n in small_names]), "adamw_small")
    for res, p in zip((delta, new_m, new_v), packed):
        res.update(dict(zip(small_names, _unpack(p, shapes))))

    return (loss, grad_x, *[grad[n] for n in WEIGHTS], *[delta[n] for n in WEIGHTS],
            *[new_m[n] for n in WEIGHTS], *[new_v[n] for n in WEIGHTS])
```

```python
import functools

import jax
import jax.numpy as jnp
import numpy as np
from jax import lax
from jax.experimental import pallas as pl
from jax.experimental.pallas import tpu as pltpu

F32 = jnp.float32
BF16 = jnp.bfloat16

HEAD_DIM = 128
GROUP = 4
LRU_BLOCK_W = 64
LRU_C = 8.0
WINDOW = 128
GRID_W = 64
ROPE_THETA = 10000.0
EPS = 1e-6
NEG_INF = -1e30
ADAM_LR, ADAM_B1, ADAM_B2, ADAM_EPS, ADAM_WD, ADAM_STEP = 0.001, 0.9, 0.999, 1e-08, 0.01, 10

ROW_BLOCK = 256
LRU_LANES = 128
LRU_CHUNK = 128
LRU_UNROLL = 2
DENSE_FWD_BQ = 256
LOG2E = 1.4426950408889634
ATT_BQ = 256
WIN_BQ = 256
WIN_SPAN = WIN_BQ + 2 * WINDOW
MOD_ROWS = 16
VMEM_LIMIT = 56 * 1024 * 1024

MESH = pl.DeviceIdType.MESH


def _cparams(sem=None, vmem=None):
    kw = {}
    if sem is not None:
        kw["dimension_semantics"] = sem
    if vmem is not None:
        kw["vmem_limit_bytes"] = vmem
    return pltpu.CompilerParams(**kw)


def _sigmoid(v):
    return 1.0 / (1.0 + jnp.exp(-v))


def _silu(v):
    return v * _sigmoid(v)


def _dsilu(v):
    s = _sigmoid(v)
    return s * (1.0 + v * (1.0 - s))


def _one_minus_square(log_a, a):
    z2 = log_a * log_a
    series = (-2.0 * a * log_a) * (1.0 + z2 * (1.0 / 6 + z2 * (1.0 / 120 + z2 * (1.0 / 5040))))
    return jnp.where(z2 < 0.25, series, 1.0 - a * a)


def _log1p(y):
    u = 1.0 + y
    d = u - 1.0
    return jnp.where(d == 0.0, y, jnp.log(u) * (y / jnp.where(d == 0.0, 1.0, d)))


def _softplus(x):
    return jnp.maximum(x, 0.0) + _log1p(jnp.exp(-jnp.abs(x)))


def _dot(a, b):
    return jnp.dot(a, b, preferred_element_type=F32)


def _dot_nt(a, b):
    return lax.dot_general(a, b, (((1,), (1,)), ((), ())), preferred_element_type=F32)


def _dot_tn(a, b):
    return lax.dot_general(a, b, (((0,), (0,)), ((), ())), preferred_element_type=F32)


def _colsum(v):
    return jnp.sum(v, axis=0, keepdims=True)


def _layout(d_model):
    kvw = (d_model // HEAD_DIM // GROUP) * HEAD_DIM
    names = ["gA", "gB", "gC", "mA", "mB", "mC", "uA", "qB", "qC", "kB", "vB", "kC", "vC"]
    widths = [d_model] * 9 + [kvw] * 4
    off, o = {}, 0
    for n, w in zip(names, widths):
        off[n] = o
        o += w
    return off, o


def _orig_segments(d_model):
    kvw = (d_model // HEAD_DIM // GROUP) * HEAD_DIM
    names = ["uA", "gA", "qB", "kB", "vB", "gB", "qC", "kC", "vC", "gC", "mA", "mB", "mC"]
    widths = [d_model, d_model, d_model, kvw, kvw, d_model, d_model, kvw, kvw, d_model, d_model, d_model, d_model]
    out, o = [], 0
    for n, w in zip(names, widths):
        out.append((n, o, w))
        o += w
    return out


def _matmul(a, b, *, ta=False, tb=False, out_dtype=F32, bm, bn, bk, name, n_outer=False, a_lead=None, b_lead=None):
    a_shape = a.shape if a_lead is None else a.shape[1:]
    b_shape = b.shape if b_lead is None else b.shape[1:]
    (kdim, m) = a_shape if ta else a_shape[::-1]
    (n, kdim2) = b_shape if tb else b_shape[::-1]
    assert kdim == kdim2 and m % bm == 0 and n % bn == 0 and kdim % bk == 0, (a.shape, b.shape, bm, bn, bk)
    nk = kdim // bk
    dims = (((0 if ta else 1,), (1 if tb else 0,)), ((), ()))

    def ij(f):
        return (lambda j, i, k: f(i, j, k)) if n_outer else f

    def body(a_ref, b_ref, o_ref, *scratch):
        r = lax.dot_general(a_ref[...].astype(BF16), b_ref[...].astype(BF16), dims, preferred_element_type=F32)
        if nk == 1:
            o_ref[...] = r.astype(out_dtype)
        else:
            acc = scratch[0]
            k = pl.program_id(2)

            @pl.when(k == 0)
            def _():
                acc[...] = r

            @pl.when(k > 0)
            def _():
                acc[...] += r

            @pl.when(k == nk - 1)
            def _():
                o_ref[...] = acc[...].astype(out_dtype)

    def spec(shape, f, lead):
        f = ij(f)
        if lead is None:
            return pl.BlockSpec(shape, f)
        return pl.BlockSpec((None,) + shape, lambda *g: (lead,) + f(*g))

    a_spec = spec((bk, bm), lambda i, j, k: (k, i), a_lead) if ta else spec((bm, bk), lambda i, j, k: (i, k), a_lead)
    b_spec = spec((bn, bk), lambda i, j, k: (j, k), b_lead) if tb else spec((bk, bn), lambda i, j, k: (k, j), b_lead)
    return pl.pallas_call(
        body, name=name, grid=(n // bn, m // bm, nk) if n_outer else (m // bm, n // bn, nk),
        in_specs=[a_spec, b_spec], out_specs=pl.BlockSpec((bm, bn), ij(lambda i, j, k: (i, j))),
        out_shape=jax.ShapeDtypeStruct((m, n), out_dtype),
        scratch_shapes=[pltpu.VMEM((bm, bn), F32)] if nk > 1 else [],
        compiler_params=_cparams(("parallel", "parallel", "arbitrary"), VMEM_LIMIT),
    )(a, b)


def _mod_fwd(c16, w_mod, b_mod):
    d3 = w_mod.shape[1]

    def body(c_ref, w_ref, b_ref, o_ref):
        o_ref[...] = _dot(_silu(c_ref[...]).astype(BF16), w_ref[...]) + b_ref[...]

    return pl.pallas_call(body, name="mod_fwd", out_shape=jax.ShapeDtypeStruct((MOD_ROWS, d3), F32),
                          compiler_params=_cparams(None, VMEM_LIMIT))(c16, w_mod, b_mod)


def _mod_bwd(c16, dmod16, w_mod):
    d, d3 = w_mod.shape

    def body(c_ref, g_ref, w_ref, dw_ref, db_ref, dc_ref):
        c = c_ref[...]
        g = g_ref[...]
        gb = g.astype(BF16)
        dw_ref[...] = _dot_tn(_silu(c).astype(BF16), gb)
        db_ref[...] = _colsum(g)
        dc_ref[...] = _dot_nt(gb, w_ref[...]) * _dsilu(c)

    return pl.pallas_call(
        body, name="mod_bwd",
        out_shape=(jax.ShapeDtypeStruct((d, d3), F32), jax.ShapeDtypeStruct((1, d3), F32),
                   jax.ShapeDtypeStruct((MOD_ROWS, d), F32)),
        compiler_params=_cparams(None, VMEM_LIMIT))(c16, dmod16, w_mod)


def _row_kind(t, lb):
    return jnp.where(t >= lb, 1, 0)


def _norm_mod_fwd(x3, g, modsel, ctx_len):
    b, t, d = x3.shape
    bt = ROW_BLOCK
    lb = ctx_len // bt

    def body(x_ref, g_ref, m_ref, h_ref):
        x = x_ref[0]
        rstd = lax.rsqrt(jnp.mean(x * x, axis=-1, keepdims=True) + EPS)
        y = x * rstd * g_ref[...]
        h_ref[0] = (y * (1.0 + m_ref[0, 0, 1:2, :]) + m_ref[0, 0, 0:1, :]).astype(BF16)

    return pl.pallas_call(
        body, name="norm_mod_fwd", grid=(b, t // bt),
        in_specs=[pl.BlockSpec((1, bt, d), lambda i, j: (i, j, 0)),
                  pl.BlockSpec((1, d), lambda i, j: (0, 0)),
                  pl.BlockSpec((1, 1, 8, d), lambda i, j: (i, _row_kind(j, lb), 0, 0))],
        out_specs=pl.BlockSpec((1, bt, d), lambda i, j: (i, j, 0)),
        out_shape=jax.ShapeDtypeStruct((b, t, d), BF16),
        compiler_params=_cparams(("parallel", "arbitrary")),
    )(x3, g, modsel)


def _norm_mod_bwd(dh3, x3, g, modsel, dres3, ctx_len):
    b, t, d = x3.shape
    bt = ROW_BLOCK
    lb = ctx_len // bt

    def body(dh_ref, x_ref, g_ref, m_ref, dres_ref, dx_ref, acc_ref):
        j = pl.program_id(1)
        x = x_ref[0]
        dh = dh_ref[0]
        g_row = g_ref[...]
        rstd = lax.rsqrt(jnp.mean(x * x, axis=-1, keepdims=True) + EPS)
        xhat = x * rstd
        dhpre = dh * (1.0 + m_ref[0, 0, 1:2, :])
        dxhat = dhpre * g_row
        dx = rstd * (dxhat - xhat * jnp.mean(dxhat * xhat, axis=-1, keepdims=True))
        dx_ref[0] = dx + dres_ref[0]

        @pl.when((j == 0) | (j == lb))
        def _():
            acc_ref[...] = jnp.zeros_like(acc_ref)

        acc_ref[0, 0, 0:1, :] += _colsum(dh)
        acc_ref[0, 0, 1:2, :] += _colsum(dh * (xhat * g_row))
        acc_ref[0, 0, 2:3, :] += _colsum(dhpre * xhat)

    blk = pl.BlockSpec((1, bt, d), lambda i, j: (i, j, 0))
    return pl.pallas_call(
        body, name="norm_mod_bwd", grid=(b, t // bt),
        in_specs=[blk, blk, pl.BlockSpec((1, d), lambda i, j: (0, 0)),
                  pl.BlockSpec((1, 1, 8, d), lambda i, j: (i, _row_kind(j, lb), 0, 0)), blk],
        out_specs=(blk, pl.BlockSpec((1, 1, 8, d), lambda i, j: (i, _row_kind(j, lb), 0, 0))),
        out_shape=(jax.ShapeDtypeStruct((b, t, d), F32), jax.ShapeDtypeStruct((b, 2, 8, d), F32)),
        compiler_params=_cparams(("parallel", "arbitrary")),
    )(dh3, x3, g, modsel, dres3)


def _shifted_rows(ref, c, off, ctx_len, total):
    ct = LRU_CHUNK
    r0 = pl.multiple_of(c * ct, ct)
    x0 = ref[pl.ds(r0, ct), :]
    row = lax.broadcasted_iota(jnp.int32, x0.shape, 0)
    if off < 0:
        k = -off
        has = jnp.logical_and(r0 != 0, r0 != ctx_len)
        p0 = pl.multiple_of(jnp.maximum(r0 - 8, 0), 8)
        edge = jnp.where(has, ref[pl.ds(p0, 8), :], 0.0)
        out = pltpu.roll(x0, k, 0)
        for j in range(k):
            out = jnp.where(row == j, edge[8 - k + j:8 - k + j + 1, :], out)
    else:
        k = off
        has = jnp.logical_and(r0 + ct != ctx_len, r0 + ct != total)
        n0 = pl.multiple_of(jnp.minimum(r0 + ct, total - 8), 8)
        edge = jnp.where(has, ref[pl.ds(n0, 8), :], 0.0)
        out = pltpu.roll(x0, ct - k, 0)
        for j in range(k):
            out = jnp.where(row == ct - k + j, edge[j:j + 1, :], out)
    return out


def _chunk_scan(a, b, reverse):
    n = a.shape[0]
    row = lax.broadcasted_iota(jnp.int32, a.shape, 0)
    s = 1
    while s < n:
        if reverse:
            a_s, b_s, ok = pltpu.roll(a, n - s, 0), pltpu.roll(b, n - s, 0), row < n - s
        else:
            a_s, b_s, ok = pltpu.roll(a, s, 0), pltpu.roll(b, s, 0), row >= s
        b = jnp.where(ok, a * b_s + b, b)
        a = jnp.where(ok, a * a_s, a)
        s *= 2
    return a, b


def _loop_chunks(n, body, init):
    assert n % LRU_UNROLL == 0

    def group(s2, carry):
        for u in range(LRU_UNROLL):
            carry = body(LRU_UNROLL * s2 + u, carry)
        return carry

    return lax.fori_loop(0, n // LRU_UNROLL, group, init)


def _lru_order(d, s, n_ctx, n_all):
    if d == 0:
        return s
    return jnp.where(s < n_ctx, n_ctx - 1 - s, n_all - 1 - (s - n_ctx))


def _lru_gates(u, wa, ba, wx, bx, sp):
    ub = u.astype(BF16)
    r = _sigmoid(_dot(ub, wa) + ba)
    i = _sigmoid(_dot(ub, wx) + bx)
    log_a = (-LRU_C * sp) * r
    a = jnp.exp(log_a)
    sf = jnp.sqrt(_one_minus_square(log_a, a))
    return ub, r, i, a, sf


def _lru_specs(t, n_lane_blocks_offset):
    ln = LRU_LANES
    return [
        pl.BlockSpec((4, ln), lambda i, j: (0, j)),
        pl.BlockSpec((1, ln), lambda i, j: (0, j)),
        pl.BlockSpec((2, 1, ln, ln), lambda i, j: (0, j, 0, 0)),
        pl.BlockSpec((2, ln), lambda i, j: (0, j)),
        pl.BlockSpec((2, 1, ln, ln), lambda i, j: (0, j, 0, 0)),
        pl.BlockSpec((2, ln), lambda i, j: (0, j)),
        pl.BlockSpec((2, ln), lambda i, j: (0, j)),
    ]


def _lru_conv(ua_ref, cw_ref, cb_ref, u_s, ctx_len, total):
    ct = LRU_CHUNK

    def conv(c, _):
        r0 = pl.multiple_of(c * ct, ct)
        u = (cw_ref[0:1, :] * _shifted_rows(ua_ref, c, -2, ctx_len, total)
             + cw_ref[1:2, :] * _shifted_rows(ua_ref, c, -1, ctx_len, total)
             + cw_ref[2:3, :] * ua_ref[pl.ds(r0, ct), :]
             + cw_ref[3:4, :] * _shifted_rows(ua_ref, c, 1, ctx_len, total) + cb_ref[...])
        u_s[pl.ds(r0, ct), :] = u
        return 0

    lax.fori_loop(0, total // ct, conv, 0)


def _lru_fwd(proj3, col0, conv_w, conv_b, wa_bd, ba, wx_bd, bx, lam, ctx_len):
    b, t, _ = proj3.shape
    d = conv_w.shape[1]
    ln, ct = LRU_LANES, LRU_CHUNK
    n_all, n_ctx = t // ct, ctx_len // ct
    cb0 = col0 // ln

    def body(ua_ref, cw_ref, cb_ref, wa_ref, ba_ref, wx_ref, bx_ref, lam_ref, y_ref, u_s, h1_s):
        ua = ua_ref.at[0]
        _lru_conv(ua, cw_ref, cb_ref, u_s, ctx_len, t)
        par = [(_softplus(-lam_ref[dr:dr + 1, :]), wa_ref[dr, 0], wx_ref[dr, 0], ba_ref[dr:dr + 1, :], bx_ref[dr:dr + 1, :])
               for dr in (0, 1)]

        def step(s, carry):
            out = []
            for dr in (0, 1):
                sp, wa, wx, ba_row, bx_row = par[dr]
                c = _lru_order(dr, s, n_ctx, n_all)
                r0 = pl.multiple_of(c * ct, ct)
                u = u_s[pl.ds(r0, ct), :]
                _, _, i, a, sf = _lru_gates(u, wa, ba_row, wx, bx_row, sp)
                aa, h0 = _chunk_scan(a, sf * (i * u), reverse=(dr == 1))
                h = h0 + aa * carry[dr]
                if dr == 0:
                    y_ref[0, pl.ds(r0, ct), :] = h
                    out.append(h[ct - 1:ct, :])
                else:
                    h1_s[pl.ds(r0, ct), :] = h
                    out.append(h[0:1, :])
            return tuple(out)

        zrow = jnp.zeros((1, ln), F32)
        _loop_chunks(n_all, step, (zrow, zrow))

        def add(c, _):
            r0 = pl.multiple_of(c * ct, ct)
            y_ref[0, pl.ds(r0, ct), :] += h1_s[pl.ds(r0, ct), :]
            return 0

        lax.fori_loop(0, n_all, add, 0)

    return pl.pallas_call(
        body, name="lru_fwd", grid=(b, d // ln),
        in_specs=[pl.BlockSpec((1, t, ln), lambda i, j: (i, 0, cb0 + j))] + _lru_specs(t, cb0),
        out_specs=pl.BlockSpec((1, t, ln), lambda i, j: (i, 0, j)),
        out_shape=jax.ShapeDtypeStruct((b, t, d), F32),
        scratch_shapes=[pltpu.VMEM((t, ln), F32)] * 2,
        compiler_params=_cparams(("parallel", "parallel"), VMEM_LIMIT),
    )(proj3, conv_w, conv_b, wa_bd, ba, wx_bd, bx, lam)


def _lru_bwd(dproj3, proj3, col0, dy3, conv_w, conv_b, wa_bd, ba, wx_bd, bx, lam, ctx_len):
    b, t, _ = proj3.shape
    d = conv_w.shape[1]
    ln, ct = LRU_LANES, LRU_CHUNK
    n_all, n_ctx = t // ct, ctx_len // ct
    cb0 = col0 // ln

    def body(dproj_hbm, ua_ref, dy_ref, cw_ref, cb_ref, wa_ref, ba_ref, wx_ref, bx_ref, lam_ref,
             dua_ref, vec_ref, dwa_ref, dwx_ref, u_s, h_s, du_s, a_s, sf_s, i_s, r_s):
        del dproj_hbm
        ua = ua_ref.at[0]
        _lru_conv(ua, cw_ref, cb_ref, u_s, ctx_len, t)
        du_s[...] = jnp.zeros_like(du_s)
        vec_ref[...] = jnp.zeros_like(vec_ref)
        for dr in (0, 1):
            sp = _softplus(-lam_ref[dr:dr + 1, :])
            wa, wx = wa_ref[dr, 0], wx_ref[dr, 0]
            ba_row, bx_row = ba_ref[dr:dr + 1, :], bx_ref[dr:dr + 1, :]

            def fwd(s, carry, dr=dr, sp=sp, wa=wa, wx=wx, ba_row=ba_row, bx_row=bx_row):
                c = _lru_order(dr, s, n_ctx, n_all)
                r0 = pl.multiple_of(c * ct, ct)
                u = u_s[pl.ds(r0, ct), :]
                _, r, i, a, sf = _lru_gates(u, wa, ba_row, wx, bx_row, sp)
                aa, h0 = _chunk_scan(a, sf * (i * u), reverse=(dr == 1))
                h = h0 + aa * carry
                h_s[pl.ds(r0, ct), :] = h
                a_s[pl.ds(r0, ct), :] = a
                sf_s[pl.ds(r0, ct), :] = sf
                i_s[pl.ds(r0, ct), :] = i
                r_s[pl.ds(r0, ct), :] = r
                return h[ct - 1:ct, :] if dr == 0 else h[0:1, :]

            _loop_chunks(n_all, fwd, jnp.zeros((1, ln), F32))
            dwa_ref[0, dr, 0] = jnp.zeros((ln, ln), F32)
            dwx_ref[0, dr, 0] = jnp.zeros((ln, ln), F32)

            def bwd(sr, carry, dr=dr, sp=sp, wa=wa, wx=wx):
                gc, vacc = carry
                c = _lru_order(dr, n_all - 1 - sr, n_ctx, n_all)
                r0 = pl.multiple_of(c * ct, ct)
                u = u_s[pl.ds(r0, ct), :]
                h = h_s[pl.ds(r0, ct), :]
                dy = dy_ref[0, pl.ds(r0, ct), :]
                ub = u.astype(BF16)
                r, i, a, sf = r_s[pl.ds(r0, ct), :], i_s[pl.ds(r0, ct), :], a_s[pl.ds(r0, ct), :], sf_s[pl.ds(r0, ct), :]
                row = lax.broadcasted_iota(jnp.int32, a.shape, 0)
                if dr == 0:
                    alpha = jnp.where(row == ct - 1, 1.0, pltpu.roll(a, ct - 1, 0))
                    aa, g0 = _chunk_scan(alpha, dy, reverse=True)
                    g = g0 + aa * gc
                    gc_new = a[0:1, :] * g[0:1, :]
                    p0 = pl.multiple_of(jnp.maximum(r0 - 8, 0), 8)
                    edge = jnp.where(r0 != 0, h_s[pl.ds(p0, 8), :], 0.0)[7:8, :]
                    h_prev = jnp.where(row == 0, edge, pltpu.roll(h, 1, 0))
                else:
                    alpha = jnp.where(row == 0, 1.0, pltpu.roll(a, 1, 0))
                    aa, g0 = _chunk_scan(alpha, dy, reverse=False)
                    g = g0 + aa * gc
                    gc_new = a[ct - 1:ct, :] * g[ct - 1:ct, :]
                    r_end = r0 + ct
                    n0 = pl.multiple_of(jnp.where(r_end == t, 0, jnp.minimum(r_end, t - 8)), 8)
                    edge = jnp.where(r_end != ctx_len, h_s[pl.ds(n0, 8), :], 0.0)[0:1, :]
                    h_prev = jnp.where(row == ct - 1, edge, pltpu.roll(h, ct - 1, 0))
                da = g * h_prev
                iu = i * u
                diu = g * sf
                dlog_a = da * a - (g * iu) * (a * a) / sf
                dpre_r = (dlog_a * (-LRU_C * sp)) * (r * (1.0 - r))
                dpre_i = (diu * u) * (i * (1.0 - i))
                dpr_b, dpi_b = dpre_r.astype(BF16), dpre_i.astype(BF16)
                du = diu * i + _dot_nt(dpr_b, wa) + _dot_nt(dpi_b, wx)
                du_s[pl.ds(r0, ct), :] += du
                dwa_ref[0, dr, 0] += _dot_tn(ub, dpr_b)
                dwx_ref[0, dr, 0] += _dot_tn(ub, dpi_b)
                vacc = (vacc[0] + _colsum(dpre_r), vacc[1] + _colsum(dpre_i), vacc[2] + _colsum(dlog_a * (-LRU_C * r)))
                return gc_new, vacc

            zrow = jnp.zeros((1, ln), F32)
            _, vacc = _loop_chunks(n_all, bwd, (zrow, (zrow, zrow, zrow)))
            vec_ref[0, 5 + dr:6 + dr, :] = vacc[0]
            vec_ref[0, 7 + dr:8 + dr, :] = vacc[1]
            vec_ref[0, 9 + dr:10 + dr, :] = vacc[2]

        def conv_bwd(c, acc):
            r0 = pl.multiple_of(c * ct, ct)
            du = du_s[pl.ds(r0, ct), :]
            dua = (cw_ref[0:1, :] * _shifted_rows(du_s, c, 2, ctx_len, t)
                   + cw_ref[1:2, :] * _shifted_rows(du_s, c, 1, ctx_len, t)
                   + cw_ref[2:3, :] * du
                   + cw_ref[3:4, :] * _shifted_rows(du_s, c, -1, ctx_len, t))
            dua_ref[0, pl.ds(r0, ct), :] = dua.astype(BF16)
            return (acc[0] + _colsum(du * _shifted_rows(ua, c, -2, ctx_len, t)),
                    acc[1] + _colsum(du * _shifted_rows(ua, c, -1, ctx_len, t)),
                    acc[2] + _colsum(du * ua[pl.ds(r0, ct), :]),
                    acc[3] + _colsum(du * _shifted_rows(ua, c, 1, ctx_len, t)),
                    acc[4] + _colsum(du))

        zrow = jnp.zeros((1, ln), F32)
        acc = lax.fori_loop(0, n_all, conv_bwd, (zrow,) * 5)
        for k in range(5):
            vec_ref[0, k:k + 1, :] = acc[k]

    ng = d // ln
    return pl.pallas_call(
        body, name="lru_bwd", grid=(b, ng),
        in_specs=[pl.BlockSpec(memory_space=pl.ANY),
                  pl.BlockSpec((1, t, ln), lambda i, j: (i, 0, cb0 + j)),
                  pl.BlockSpec((1, t, ln), lambda i, j: (i, 0, j))] + _lru_specs(t, cb0),
        out_specs=(pl.BlockSpec((1, t, ln), lambda i, j: (i, 0, cb0 + j)),
                   pl.BlockSpec((1, 16, ln), lambda i, j: (i, 0, j)),
                   pl.BlockSpec((1, 2, 1, ln, ln), lambda i, j: (i, 0, j, 0, 0)),
                   pl.BlockSpec((1, 2, 1, ln, ln), lambda i, j: (i, 0, j, 0, 0))),
        out_shape=(jax.ShapeDtypeStruct(dproj3.shape, dproj3.dtype),
                   jax.ShapeDtypeStruct((b, 16, d), F32),
                   jax.ShapeDtypeStruct((b, 2, ng, ln, ln), F32),
                   jax.ShapeDtypeStruct((b, 2, ng, ln, ln), F32)),
        scratch_shapes=[pltpu.VMEM((t, ln), F32)] * 7,
        input_output_aliases={0: 0},
        compiler_params=_cparams(("parallel", "parallel"), VMEM_LIMIT),
    )(dproj3, proj3, dy3, conv_w, conv_b, wa_bd, ba, wx_bd, bx, lam)


def _rope_tables(ctx_len, seq):
    p = HEAD_DIM // 4
    inv = ROPE_THETA ** (-jnp.arange(p, dtype=F32) / p)
    tok = jnp.arange(seq)
    ang_r = (tok // GRID_W)[:, None] * inv
    ang_c = (tok % GRID_W)[:, None] * inv
    cos = jnp.concatenate([jnp.cos(ang_r)] * 2 + [jnp.cos(ang_c)] * 2, axis=1)
    sin = jnp.concatenate([-jnp.sin(ang_r), jnp.sin(ang_r), -jnp.sin(ang_c), jnp.sin(ang_c)], axis=1)
    cos = jnp.concatenate([jnp.ones((ctx_len, HEAD_DIM), F32), cos], axis=0)
    sin = jnp.concatenate([jnp.zeros((ctx_len, HEAD_DIM), F32), sin], axis=0)
    return cos, sin


def _swap_halves(v):
    lane = lax.broadcasted_iota(jnp.int32, v.shape, 1)
    return jnp.where((lane & 63) < 32, pltpu.roll(v, 96, 1), pltpu.roll(v, 32, 1))


def _head_rstd(v):
    return lax.rsqrt(jnp.mean(v * v, axis=-1, keepdims=True) + EPS)


QKV_BLOCK = GROUP * HEAD_DIM
PREP_ROWS = (2176, 256)


def _prep_fwd(proj3, qcol, kvcol, d, cos, sin, gq, gk, use_norm):
    b, t, _ = proj3.shape
    bt, wb = _pick(t, PREP_ROWS), QKV_BLOCK
    nqb = d // wb
    assert qcol % wb == 0 and kvcol % wb == 0 and d // HEAD_DIM // GROUP == 2
    qb0, kvb = qcol // wb, kvcol // wb

    def body(p_ref, cos_ref, sin_ref, gq_ref, gk_ref, o_ref):
        s = pl.program_id(2)
        c, sn = cos_ref[...], sin_ref[...]

        def rope(v):
            return v * c + _swap_halves(v) * sn

        @pl.when(s < nqb)
        def _():
            for hh in range(GROUP):
                v = p_ref[0, :, hh * HEAD_DIM:(hh + 1) * HEAD_DIM]
                if use_norm:
                    v = v * _head_rstd(v) * gq_ref[...]
                o_ref[0, :, hh * HEAD_DIM:(hh + 1) * HEAD_DIM] = rope(v).astype(BF16)

        @pl.when(s == nqb)
        def _():
            for hh in range(2):
                v = p_ref[0, :, hh * HEAD_DIM:(hh + 1) * HEAD_DIM]
                if use_norm:
                    v = v * _head_rstd(v) * gk_ref[...]
                o_ref[0, :, hh * HEAD_DIM:(hh + 1) * HEAD_DIM] = rope(v).astype(BF16)
            o_ref[0, :, 2 * HEAD_DIM:] = p_ref[0, :, 2 * HEAD_DIM:].astype(BF16)

    return pl.pallas_call(
        body, name="prep_fwd_norm" if use_norm else "prep_fwd", grid=(b, t // bt, nqb + 1),
        in_specs=[pl.BlockSpec((1, bt, wb), lambda i, j, s: (i, j, jnp.where(s < nqb, qb0 + s, kvb))),
                  pl.BlockSpec((bt, HEAD_DIM), lambda i, j, s: (j, 0)),
                  pl.BlockSpec((bt, HEAD_DIM), lambda i, j, s: (j, 0)),
                  pl.BlockSpec((1, HEAD_DIM), lambda i, j, s: (0, 0)),
                  pl.BlockSpec((1, HEAD_DIM), lambda i, j, s: (0, 0))],
        out_specs=pl.BlockSpec((1, bt, wb), lambda i, j, s: (i, j, s)),
        out_shape=jax.ShapeDtypeStruct((b, t, d + wb), BF16),
        compiler_params=_cparams(("parallel", "parallel", "arbitrary"), VMEM_LIMIT),
    )(proj3, cos, sin, gq, gk)


def _prep_bwd(dproj3, dq3, dkt, dvt, proj3, qcol, kvcol, d, cos, sin, gq, gk, use_norm):
    b, t, _ = proj3.shape
    bt, wb = _pick(t, PREP_ROWS), QKV_BLOCK
    nqb = d // wb
    qb0, kvb = qcol // wb, kvcol // wb
    kvh = dkt.shape[1]

    def body(dproj_hbm, dq_ref, dkt_ref, dvt_ref, p_ref, cos_ref, sin_ref, gq_ref, gk_ref, o_ref, gacc_ref):
        del dproj_hbm
        j, s = pl.program_id(1), pl.program_id(2)
        c, sn = cos_ref[...], sin_ref[...]

        @pl.when((j == 0) & (s == 0))
        def _():
            gacc_ref[...] = jnp.zeros_like(gacc_ref)

        def unrope(dv):
            return dv * c + _swap_halves(dv * sn)

        def head_bwd(dyv, xv, g_ref, acc_row):
            dyv = unrope(dyv)
            if not use_norm:
                return dyv
            rstd = _head_rstd(xv)
            xhat = xv * rstd
            gacc_ref[0, acc_row:acc_row + 1, :] += _colsum(dyv * xhat)
            dxhat = dyv * g_ref[...]
            return rstd * (dxhat - xhat * jnp.mean(dxhat * xhat, axis=-1, keepdims=True))

        @pl.when(s < nqb)
        def _():
            for hh in range(GROUP):
                sl = slice(hh * HEAD_DIM, (hh + 1) * HEAD_DIM)
                o_ref[0, :, sl] = head_bwd(dq_ref[0, :, sl], p_ref[0, :, sl], gq_ref, 0).astype(BF16)

        @pl.when(s == nqb)
        def _():
            for hh in range(kvh):
                sl = slice(hh * HEAD_DIM, (hh + 1) * HEAD_DIM)
                o_ref[0, :, sl] = head_bwd(dkt_ref[0, hh].T, p_ref[0, :, sl], gk_ref, 1).astype(BF16)
                sv = slice((kvh + hh) * HEAD_DIM, (kvh + hh + 1) * HEAD_DIM)
                o_ref[0, :, sv] = dvt_ref[0, hh].T.astype(BF16)

    col = lambda i, j, s: (i, j, jnp.where(s < nqb, qb0 + s, kvb))
    return pl.pallas_call(
        body, name="prep_bwd_norm" if use_norm else "prep_bwd", grid=(b, t // bt, nqb + 1),
        in_specs=[pl.BlockSpec(memory_space=pl.ANY),
                  pl.BlockSpec((1, bt, wb), lambda i, j, s: (i, j, jnp.minimum(s, nqb - 1))),
                  pl.BlockSpec((1, kvh, HEAD_DIM, bt), lambda i, j, s: (i, 0, 0, j)),
                  pl.BlockSpec((1, kvh, HEAD_DIM, bt), lambda i, j, s: (i, 0, 0, j)),
                  pl.BlockSpec((1, bt, wb), col),
                  pl.BlockSpec((bt, HEAD_DIM), lambda i, j, s: (j, 0)),
                  pl.BlockSpec((bt, HEAD_DIM), lambda i, j, s: (j, 0)),
                  pl.BlockSpec((1, HEAD_DIM), lambda i, j, s: (0, 0)),
                  pl.BlockSpec((1, HEAD_DIM), lambda i, j, s: (0, 0))],
        out_specs=(pl.BlockSpec((1, bt, wb), col), pl.BlockSpec((1, 8, HEAD_DIM), lambda i, j, s: (i, 0, 0))),
        out_shape=(jax.ShapeDtypeStruct(dproj3.shape, dproj3.dtype), jax.ShapeDtypeStruct((b, 8, HEAD_DIM), F32)),
        input_output_aliases={0: 0},
        compiler_params=_cparams(("parallel", "arbitrary", "arbitrary"), VMEM_LIMIT),
    )(dproj3, dq3, dkt, dvt, proj3, cos, sin, gq, gk)


def _stack_heads(ref, dtype=None):
    parts = [ref[0, :, g * HEAD_DIM:(g + 1) * HEAD_DIM] for g in range(GROUP)]
    v = jnp.concatenate(parts, axis=0)
    return v if dtype is None else v.astype(dtype)


def _unstack_heads(ref, v, bq):
    for g in range(GROUP):
        ref[0, :, g * HEAD_DIM:(g + 1) * HEAD_DIM] = v[g * bq:(g + 1) * bq, :]


def _attn_specs(t, d, bq):
    kvh = d // HEAD_DIM // GROUP
    kc0 = d // HEAD_DIM
    q_spec = pl.BlockSpec((1, bq, QKV_BLOCK), lambda i, h, j: (i, j, h))
    k_spec = pl.BlockSpec((1, t, HEAD_DIM), lambda i, h, j: (i, 0, kc0 + h))
    v_spec = pl.BlockSpec((1, t, HEAD_DIM), lambda i, h, j: (i, 0, kc0 + kvh + h))
    lse_spec = pl.BlockSpec((1, GROUP, bq, HEAD_DIM), lambda i, h, j: (i, h, j, 0))
    kt_spec = pl.BlockSpec((1, 1, HEAD_DIM, t), lambda i, h, j: (i, h, 0, 0))
    return kvh, q_spec, k_spec, v_spec, lse_spec, kt_spec


SCALE = HEAD_DIM ** -0.5


def _attn_dense_fwd(qkv, d, ctx_len):
    b, t, _ = qkv.shape
    bq = DENSE_FWD_BQ
    lq = ctx_len // bq
    kvh, q_spec, k_spec, v_spec, lse_spec, _ = _attn_specs(t, d, bq)

    def body(q_ref, k_ref, v_ref, o_ref, lse_ref):
        i = pl.program_id(2)

        def attend(k, v):
            for g in range(GROUP):
                sl = slice(g * HEAD_DIM, (g + 1) * HEAD_DIM)
                s = _dot_nt(q_ref[0, :, sl], k)
                m = jnp.max(s, axis=1, keepdims=True)
                p = jnp.exp2((s - m) * (SCALE * LOG2E))
                l = jnp.sum(p, axis=1, keepdims=True)
                o_ref[0, :, sl] = _dot(p.astype(BF16), v) / l
                lse_ref[0, g] = jnp.broadcast_to(m * SCALE + jnp.log(l), (bq, HEAD_DIM))

        @pl.when(i < lq)
        def _():
            attend(k_ref[0, 0:ctx_len, :], v_ref[0, 0:ctx_len, :])

        @pl.when(i >= lq)
        def _():
            attend(k_ref[0], v_ref[0])

    return pl.pallas_call(
        body, name="attn_dense_fwd", grid=(b, kvh, t // bq),
        in_specs=[q_spec, k_spec, v_spec], out_specs=(q_spec, lse_spec),
        out_shape=(jax.ShapeDtypeStruct((b, t, d), F32), jax.ShapeDtypeStruct((b, kvh * GROUP, t, HEAD_DIM), F32)),
        compiler_params=_cparams(("parallel", "parallel", "arbitrary"), VMEM_LIMIT),
    )(qkv, qkv, qkv)


def _attn_dense_bwd(qkv, o3, do3, lse, d, ctx_len):
    b, t, _ = qkv.shape
    bq = ATT_BQ
    lq = ctx_len // bq
    kvh, q_spec, k_spec, v_spec, lse_spec, kt_spec = _attn_specs(t, d, bq)

    def body(q_ref, k_ref, v_ref, o_ref, do_ref, lse_ref, dq_ref, dkt_ref, dvt_ref):
        i = pl.program_id(2)

        @pl.when(i == 0)
        def _():
            dkt_ref[...] = jnp.zeros_like(dkt_ref)
            dvt_ref[...] = jnp.zeros_like(dvt_ref)

        def run(k, v, width):
            dk_acc = dv_acc = None
            for g in range(GROUP):
                sl = slice(g * HEAD_DIM, (g + 1) * HEAD_DIM)
                q = q_ref[0, :, sl]
                do = do_ref[0, :, sl]
                dd = jnp.sum(do * o_ref[0, :, sl], axis=1, keepdims=True)
                dob = do.astype(BF16)
                p = jnp.exp2(_dot_nt(q, k) * (SCALE * LOG2E) - lse_ref[0, g][:, 0:1] * LOG2E)
                ds = (p * (_dot_nt(dob, v) - dd) * SCALE).astype(BF16)
                dq_ref[0, :, sl] = _dot(ds, k)
                dk_g = _dot(q.astype(F32).T.astype(BF16), ds)
                dv_g = _dot(do.T.astype(BF16), p.astype(BF16))
                dk_acc = dk_g if dk_acc is None else dk_acc + dk_g
                dv_acc = dv_g if dv_acc is None else dv_acc + dv_g
            dkt_ref[0, 0, :, 0:width] += dk_acc
            dvt_ref[0, 0, :, 0:width] += dv_acc

        @pl.when(i < lq)
        def _():
            run(k_ref[0, 0:ctx_len, :], v_ref[0, 0:ctx_len, :], ctx_len)

        @pl.when(i >= lq)
        def _():
            run(k_ref[0], v_ref[0], t)

    return pl.pallas_call(
        body, name="attn_dense_bwd", grid=(b, kvh, t // bq),
        in_specs=[q_spec, k_spec, v_spec, q_spec, q_spec, lse_spec], out_specs=(q_spec, kt_spec, kt_spec),
        out_shape=(jax.ShapeDtypeStruct((b, t, d), F32), jax.ShapeDtypeStruct((b, kvh, HEAD_DIM, t), F32),
                   jax.ShapeDtypeStruct((b, kvh, HEAD_DIM, t), F32)),
        compiler_params=_cparams(("parallel", "parallel", "arbitrary"), VMEM_LIMIT),
    )(qkv, qkv, qkv, o3, do3, lse)


def _sink_column(sink_ref, h, bq):
    rowi = lax.broadcasted_iota(jnp.int32, (GROUP * bq, 1), 0)
    col = jnp.zeros((GROUP * bq, 1), F32)
    for g in range(GROUP):
        col = jnp.where((rowi >= g * bq) & (rowi < (g + 1) * bq), sink_ref[h * GROUP + g], col)
    return col


def _band(i, lq, ctx_len, t, bq):
    n = i - lq
    start = pl.multiple_of(jnp.clip(ctx_len + n * bq - WINDOW, ctx_len, t - WIN_SPAN), WINDOW)
    shape = (GROUP * bq, WIN_SPAN)
    kpos = start - ctx_len + lax.broadcasted_iota(jnp.int32, shape, 1)
    qpos = n * bq + (lax.broadcasted_iota(jnp.int32, shape, 0) & (bq - 1))
    return start, jnp.abs(kpos - qpos) <= WINDOW


def _attn_win_fwd(qkv, sink, d, ctx_len):
    b, t, _ = qkv.shape
    bq = WIN_BQ
    rows = GROUP * bq
    lq = ctx_len // bq
    kvh, q_spec, k_spec, v_spec, lse_spec, _ = _attn_specs(t, d, bq)

    def body(sink_ref, q_ref, k_ref, v_ref, o_ref, lse_ref):
        h, i = pl.program_id(1), pl.program_id(2)
        q4 = _stack_heads(q_ref)
        sink_col = _sink_column(sink_ref, h, bq)
        sc = _dot_nt(q4, k_ref[0, 0:ctx_len, :]) * SCALE
        mc = jnp.maximum(jnp.max(sc, axis=1, keepdims=True), sink_col)

        def finish(m, l, acc):
            _unstack_heads(o_ref, acc / l, bq)
            lse_ref[0] = jnp.broadcast_to(m + jnp.log(l), (rows, HEAD_DIM)).reshape(GROUP, bq, HEAD_DIM)

        @pl.when(i < lq)
        def _():
            pc = jnp.exp(sc - mc)
            l = jnp.sum(pc, axis=1, keepdims=True) + jnp.exp(sink_col - mc)
            finish(mc, l, _dot(pc.astype(BF16), v_ref[0, 0:ctx_len, :]))

        @pl.when(i >= lq)
        def _():
            start, ok = _band(i, lq, ctx_len, t, bq)
            sb = jnp.where(ok, _dot_nt(q4, k_ref[0, pl.ds(start, WIN_SPAN), :]) * SCALE, NEG_INF)
            m = jnp.maximum(mc, jnp.max(sb, axis=1, keepdims=True))
            pc, pb = jnp.exp(sc - m), jnp.exp(sb - m)
            l = jnp.sum(pc, axis=1, keepdims=True) + jnp.sum(pb, axis=1, keepdims=True) + jnp.exp(sink_col - m)
            acc = _dot(pc.astype(BF16), v_ref[0, 0:ctx_len, :]) + _dot(pb.astype(BF16), v_ref[0, pl.ds(start, WIN_SPAN), :])
            finish(m, l, acc)

    return pl.pallas_call(
        body, name="attn_win_fwd", grid=(b, kvh, t // bq),
        in_specs=[pl.BlockSpec(memory_space=pltpu.SMEM), q_spec, k_spec, v_spec], out_specs=(q_spec, lse_spec),
        out_shape=(jax.ShapeDtypeStruct((b, t, d), F32), jax.ShapeDtypeStruct((b, kvh * GROUP, t, HEAD_DIM), F32)),
        compiler_params=_cparams(("parallel", "parallel", "arbitrary"), VMEM_LIMIT),
    )(sink, qkv, qkv, qkv)


def _attn_win_bwd(qkv, sink, o3, do3, lse, d, ctx_len):
    b, t, _ = qkv.shape
    bq = WIN_BQ
    rows = GROUP * bq
    lq = ctx_len // bq
    kvh, q_spec, k_spec, v_spec, lse_spec, kt_spec = _attn_specs(t, d, bq)

    def body(sink_ref, q_ref, k_ref, v_ref, o_ref, do_ref, lse_ref, dq_ref, dkt_ref, dvt_ref, dsk_ref):
        h, i = pl.program_id(1), pl.program_id(2)

        @pl.when(i == 0)
        def _():
            dkt_ref[...] = jnp.zeros_like(dkt_ref)
            dvt_ref[...] = jnp.zeros_like(dvt_ref)
            dsk_ref[...] = jnp.zeros_like(dsk_ref)

        q4 = _stack_heads(q_ref)
        do4 = _stack_heads(do_ref)
        dd = jnp.sum(do4 * _stack_heads(o_ref), axis=1, keepdims=True)
        lse_col = lse_ref[0].reshape(rows, HEAD_DIM)[:, 0:1]
        do4b = do4.astype(BF16)
        qt = q4.astype(F32).T.astype(BF16)
        dot = do4.T.astype(BF16)

        def part(k, v):
            return _dot_nt(q4, k) * SCALE, _dot_nt(do4b, v)

        def grads(p, dp, k):
            ds = (p * (dp - dd) * SCALE).astype(BF16)
            return _dot(ds, k), _dot(qt, ds), _dot(dot, p.astype(BF16))

        kc = k_ref[0, 0:ctx_len, :]
        sc, dpc = part(kc, v_ref[0, 0:ctx_len, :])
        dq_c, dk_c, dv_c = grads(jnp.exp(sc - lse_col), dpc, kc)
        dkt_ref[0, 0, :, 0:ctx_len] += dk_c
        dvt_ref[0, 0, :, 0:ctx_len] += dv_c
        _unstack_heads(dq_ref, dq_c, bq)

        @pl.when(i >= lq)
        def _():
            start, ok = _band(i, lq, ctx_len, t, bq)
            kb = k_ref[0, pl.ds(start, WIN_SPAN), :]
            sb, dpb = part(kb, v_ref[0, pl.ds(start, WIN_SPAN), :])
            pb = jnp.where(ok, jnp.exp(sb - lse_col), 0.0)
            dq_b, dk_b, dv_b = grads(pb, dpb, kb)
            dkt_ref[0, 0, :, pl.ds(start, WIN_SPAN)] += dk_b
            dvt_ref[0, 0, :, pl.ds(start, WIN_SPAN)] += dv_b
            for g in range(GROUP):
                dq_ref[0, :, g * HEAD_DIM:(g + 1) * HEAD_DIM] += dq_b[g * bq:(g + 1) * bq, :]

        ps = jnp.exp(_sink_column(sink_ref, h, bq) - lse_col) * dd
        for g in range(GROUP):
            val = jnp.sum(ps[g * bq:(g + 1) * bq, :], axis=0, keepdims=True)
            dsk_ref[0, 0, g:g + 1, :] -= jnp.broadcast_to(val, (1, HEAD_DIM))

    return pl.pallas_call(
        body, name="attn_win_bwd", grid=(b, kvh, t // bq),
        in_specs=[pl.BlockSpec(memory_space=pltpu.SMEM), q_spec, k_spec, v_spec, q_spec, q_spec, lse_spec],
        out_specs=(q_spec, kt_spec, kt_spec, pl.BlockSpec((1, 1, 8, HEAD_DIM), lambda i, h, j: (i, h, 0, 0))),
        out_shape=(jax.ShapeDtypeStruct((b, t, d), F32), jax.ShapeDtypeStruct((b, kvh, HEAD_DIM, t), F32),
                   jax.ShapeDtypeStruct((b, kvh, HEAD_DIM, t), F32), jax.ShapeDtypeStruct((b, kvh, 8, HEAD_DIM), F32)),
        compiler_params=_cparams(("parallel", "parallel", "arbitrary"), VMEM_LIMIT),
    )(sink, qkv, qkv, qkv, o3, do3, lse)


MERGE_BWD_ROWS = 256
MERGE_BWD_VMEM = 60 * 1024 * 1024


def _resident(shape):
    return pl.BlockSpec(shape, lambda *_: (0,) * len(shape), pipeline_mode=pl.Buffered(1))


def _merge_fwd(x3, ya, yb, yc, proj3, w_br, w_out, modsel, ctx_len):
    b, t, d = x3.shape
    bt = ROW_BLOCK
    lb = ctx_len // bt

    def body(x_ref, ya_ref, yb_ref, yc_ref, gm_ref, wbr_ref, wo_ref, m_ref, xn_ref, out_ref):
        mix = jnp.zeros((bt, d), F32)
        for n, y_ref in enumerate((ya_ref, yb_ref, yc_ref)):
            z = (y_ref[0] * _silu(gm_ref[0, :, n * d:(n + 1) * d])).astype(BF16)
            mix = mix + _sigmoid(gm_ref[0, :, (3 + n) * d:(4 + n) * d]) * _dot(z, wbr_ref[n])
        o = _dot(mix.astype(BF16), wo_ref[...])
        out_ref[0] = o
        xn_ref[0] = x_ref[0] + m_ref[0, 0, 2:3, :] * o

    blk = pl.BlockSpec((1, bt, d), lambda i, j: (i, j, 0))
    return pl.pallas_call(
        body, name="merge_fwd", grid=(b, t // bt),
        in_specs=[blk, blk, blk, blk, pl.BlockSpec((1, bt, 6 * d), lambda i, j: (i, j, 0)),
                  _resident((3, d, d)), _resident((d, d)),
                  pl.BlockSpec((1, 1, 8, d), lambda i, j: (i, _row_kind(j, lb), 0, 0))],
        out_specs=(blk, blk),
        out_shape=(jax.ShapeDtypeStruct((b, t, d), F32), jax.ShapeDtypeStruct((b, t, d), F32)),
        compiler_params=_cparams(("parallel", "arbitrary"), VMEM_LIMIT),
    )(x3, ya, yb, yc, proj3, w_br, w_out, modsel)


def _merge_bwd(dxn3, out3, ya, yb, yc, proj3, w_br, w_out, modsel, ctx_len):
    b, t, d = dxn3.shape
    n_cols = proj3.shape[2]
    bt = MERGE_BWD_ROWS
    lb = ctx_len // bt

    def body(dxn_ref, out_ref, ya_ref, yb_ref, yc_ref, gm_ref, wbr_ref, wo_ref, m_ref,
             dgm_ref, dya_ref, dyb_ref, dyc_ref, z_ref, dt_ref, mix_ref, dout_ref, gacc_ref):
        j = pl.program_id(1)
        dxn = dxn_ref[0]
        doutb = (m_ref[0, 0, 2:3, :] * dxn).astype(BF16)
        dout_ref[0] = doutb

        @pl.when((j == 0) | (j == lb))
        def _():
            gacc_ref[...] = jnp.zeros_like(gacc_ref)

        gacc_ref[0, 0, 0:1, :] += _colsum(dxn * out_ref[0])
        dmix = _dot_nt(doutb, wo_ref[...])
        mix = jnp.zeros((bt, d), F32)
        for n, (y_ref, dy_ref) in enumerate(((ya_ref, dya_ref), (yb_ref, dyb_ref), (yc_ref, dyc_ref))):
            g = gm_ref[0, :, n * d:(n + 1) * d]
            y = y_ref[0]
            sig_g = _sigmoid(g)
            silu_g = g * sig_g
            z = (y * silu_g).astype(BF16)
            z_ref[n, 0] = z
            tn = _dot(z, wbr_ref[n])
            s = _sigmoid(gm_ref[0, :, (3 + n) * d:(4 + n) * d])
            mix = mix + s * tn
            dgm_ref[0, :, (3 + n) * d:(4 + n) * d] = (dmix * tn * (s * (1.0 - s))).astype(BF16)
            dtb = (dmix * s).astype(BF16)
            dt_ref[n, 0] = dtb
            dz = _dot_nt(dtb, wbr_ref[n])
            dy_ref[0] = dz * silu_g
            dgm_ref[0, :, n * d:(n + 1) * d] = (dz * y * (sig_g * (1.0 + g * (1.0 - sig_g)))).astype(BF16)
        mix_ref[0] = mix.astype(BF16)

    blk = pl.BlockSpec((1, bt, d), lambda i, j: (i, j, 0))
    blk4 = pl.BlockSpec((3, 1, bt, d), lambda i, j: (0, i, j, 0))
    wide = pl.BlockSpec((1, bt, 6 * d), lambda i, j: (i, j, 0))
    return pl.pallas_call(
        body, name="merge_bwd", grid=(b, t // bt),
        in_specs=[blk, blk, blk, blk, blk, wide, _resident((3, d, d)), _resident((d, d)),
                  pl.BlockSpec((1, 1, 8, d), lambda i, j: (i, _row_kind(j, lb), 0, 0))],
        out_specs=(wide, blk, blk, blk, blk4, blk4, blk, blk,
                   pl.BlockSpec((1, 1, 8, d), lambda i, j: (i, _row_kind(j, lb), 0, 0))),
        out_shape=(jax.ShapeDtypeStruct((b, t, n_cols), BF16),
                   jax.ShapeDtypeStruct((b, t, d), F32), jax.ShapeDtypeStruct((b, t, d), F32),
                   jax.ShapeDtypeStruct((b, t, d), F32),
                   jax.ShapeDtypeStruct((3, b, t, d), BF16), jax.ShapeDtypeStruct((3, b, t, d), BF16),
                   jax.ShapeDtypeStruct((b, t, d), BF16), jax.ShapeDtypeStruct((b, t, d), BF16),
                   jax.ShapeDtypeStruct((b, 2, 8, d), F32)),
        compiler_params=_cparams(("parallel", "arbitrary"), MERGE_BWD_VMEM),
    )(dxn3, out3, ya, yb, yc, proj3, w_br, w_out, modsel)


def _final(x3, g, target, ctx_len):
    b, t, d = x3.shape
    bt = ROW_BLOCK
    lb = ctx_len // bt

    def body(x_ref, g_ref, t_ref, dx_ref, loss_ref, dg_ref):
        j = pl.program_id(1)

        @pl.when(j == 0)
        def _():
            loss_ref[...] = jnp.zeros_like(loss_ref)
            dg_ref[...] = jnp.zeros_like(dg_ref)

        @pl.when(j < lb)
        def _():
            dx_ref[...] = jnp.zeros_like(dx_ref)

        @pl.when(j >= lb)
        def _():
            x = x_ref[0]
            g_row = g_ref[...]
            rstd = lax.rsqrt(jnp.mean(x * x, axis=-1, keepdims=True) + EPS)
            xhat = x * rstd
            err = xhat * g_row - t_ref[0]
            loss_ref[...] += (0.5 / d) * jnp.sum(err * err)
            dy = err * (1.0 / d)
            dg_ref[0, 0:1, :] += _colsum(dy * xhat)
            dxhat = dy * g_row
            dx_ref[0] = rstd * (dxhat - xhat * jnp.mean(dxhat * xhat, axis=-1, keepdims=True))

    blk = pl.BlockSpec((1, bt, d), lambda i, j: (i, j, 0))
    return pl.pallas_call(
        body, name="final_loss", grid=(b, t // bt),
        in_specs=[blk, pl.BlockSpec((1, d), lambda i, j: (0, 0)),
                  pl.BlockSpec((1, bt, d), lambda i, j: (i, jnp.maximum(j - lb, 0), 0))],
        out_specs=(blk, pl.BlockSpec((1, 8, HEAD_DIM), lambda i, j: (i, 0, 0)), pl.BlockSpec((1, 8, d), lambda i, j: (i, 0, 0))),
        out_shape=(jax.ShapeDtypeStruct((b, t, d), F32), jax.ShapeDtypeStruct((b, 8, HEAD_DIM), F32),
                   jax.ShapeDtypeStruct((b, 8, d), F32)),
        compiler_params=_cparams(("parallel", "arbitrary")),
    )(x3, g, target)


TOKEN_BLOCKS = (1088, 512, 256, 128)


def _pick(n, options):
    for o in options:
        if n % o == 0:
            return o
    raise ValueError((n, options))


def _block_diag(w):
    per = LRU_LANES // LRU_BLOCK_W
    nd, nb, bw, _ = w.shape
    wr = w.reshape(nd, nb // per, per, bw, bw)
    eye = jnp.eye(per, dtype=w.dtype)
    bd = wr[:, :, :, :, None, :] * eye[None, None, :, None, :, None]
    return bd.reshape(nd, nb // per, per * bw, per * bw).astype(BF16)


def _block_diag_grad(g):
    per = LRU_LANES // LRU_BLOCK_W
    nd, ng, _, _ = g.shape
    gr = g.reshape(nd, ng, per, LRU_BLOCK_W, per, LRU_BLOCK_W)
    diag = jnp.stack([gr[:, :, k, :, k, :] for k in range(per)], axis=2)
    return diag.reshape(nd, ng * per, LRU_BLOCK_W, LRU_BLOCK_W)


def _mod_select(mod16, b, d):
    m3 = mod16.reshape(MOD_ROWS, 3, d)
    lat = m3[:b]
    ctx = jnp.broadcast_to(m3[b][None], (b, 3, d))
    sel = jnp.stack([ctx, lat], axis=1)
    return jnp.pad(sel, ((0, 0), (0, 0), (0, 5), (0, 0)))


def _layer_fwd(x3, c16, p, cos, sin, ctx_len):
    b, t, d = x3.shape
    off, n_cols = _layout(d)
    mod16 = _mod_fwd(c16, p["w_mod"], p["b_mod"])
    modsel = _mod_select(mod16, b, d)
    h = _norm_mod_fwd(x3, p["norm_g"], modsel, ctx_len)
    proj = _matmul(h.reshape(b * t, d), p["w_in"], bm=_pick(b * t, TOKEN_BLOCKS), bn=1024, bk=d, name="proj_fwd",
                   n_outer=True, b_lead=p["li"])
    proj3 = proj.reshape(b, t, n_cols)
    ya = _lru_fwd(proj3, off["uA"], p["conv_w"], p["conv_b"], p["wa_bd"], p["ba"], p["wx_bd"], p["bx"], p["lam"], ctx_len)
    qkv_b = _prep_fwd(proj3, off["qB"], off["kB"], d, cos, sin, p["gq"], p["gk"], use_norm=False)
    yb, lse_b = _attn_win_fwd(qkv_b, p["sink"], d, ctx_len)
    qkv_c = _prep_fwd(proj3, off["qC"], off["kC"], d, cos, sin, p["gq"], p["gk"], use_norm=True)
    yc, lse_c = _attn_dense_fwd(qkv_c, d, ctx_len)
    x_new, out3 = _merge_fwd(x3, ya, yb, yc, proj3, p["w_br"], p["w_out"], modsel, ctx_len)
    return x_new, (x3, modsel, h, proj3, ya, yb, yc, qkv_b, lse_b, qkv_c, lse_c, out3)


def _layer_bwd(dxn3, saved, c16, p, cos, sin, ctx_len):
    x3, modsel, h, proj3, ya, yb, yc, qkv_b, lse_b, qkv_c, lse_c, out3 = saved
    b, t, d = x3.shape
    off, n_cols = _layout(d)
    rows = b * t
    bk = _pick(rows, (2 * TOKEN_BLOCKS[0],) + TOKEN_BLOCKS)
    dproj3, dya, dyb, dyc, z4, dt4, mixb, doutb, gacc = _merge_bwd(dxn3, out3, ya, yb, yc, proj3, p["w_br"], p["w_out"],
                                                                   modsel, ctx_len)
    dw_br = jnp.stack([_matmul(z4.reshape(3, rows, d), dt4.reshape(3, rows, d), ta=True, bm=d, bn=d, bk=bk,
                               name="dw_branch", a_lead=n, b_lead=n) for n in range(3)])
    dw_out = _matmul(mixb.reshape(rows, d), doutb.reshape(rows, d), ta=True, bm=d, bn=d, bk=bk, name="dw_out")
    dproj3, vec, dwa, dwx = _lru_bwd(dproj3, proj3, off["uA"], dya, p["conv_w"], p["conv_b"], p["wa_bd"], p["ba"],
                                     p["wx_bd"], p["bx"], p["lam"], ctx_len)
    dq_b, dkt_b, dvt_b, dsk = _attn_win_bwd(qkv_b, p["sink"], yb, dyb, lse_b, d, ctx_len)
    dproj3, _ = _prep_bwd(dproj3, dq_b, dkt_b, dvt_b, proj3, off["qB"], off["kB"], d, cos, sin, p["gq"], p["gk"], False)
    dq_c, dkt_c, dvt_c = _attn_dense_bwd(qkv_c, yc, dyc, lse_c, d, ctx_len)
    dproj3, gqk = _prep_bwd(dproj3, dq_c, dkt_c, dvt_c, proj3, off["qC"], off["kC"], d, cos, sin, p["gq"], p["gk"], True)
    dproj2 = dproj3.reshape(rows, n_cols)
    dw_in = _matmul(h.reshape(rows, d), dproj2, ta=True, bm=d, bn=1024, bk=bk, name="dw_in")
    dh = _matmul(dproj2, p["w_in"], tb=True, bm=_pick(rows, TOKEN_BLOCKS), bn=d, bk=_pick(n_cols, (2560, 1024)),
                 name="dh", b_lead=p["li"])
    dx3, nacc = _norm_mod_bwd(dh.reshape(b, t, d), x3, p["norm_g"], modsel, dxn3, ctx_len)
    per = jnp.stack([nacc[:, :, 0], nacc[:, :, 1], gacc[:, :, 0]], axis=2)
    dmod = jnp.concatenate([per[:, 1].reshape(b, 3 * d), jnp.sum(per[:, 0], axis=0).reshape(1, 3 * d)], axis=0)
    dmod16 = jnp.pad(dmod, ((0, MOD_ROWS - b - 1), (0, 0)))
    dw_mod, db_mod, dc16 = _mod_bwd(c16, dmod16, p["w_mod"])
    vsum = jnp.sum(vec, axis=0)
    grads = {
        "norm_g": jnp.sum(nacc[:, :, 2], axis=(0, 1)),
        "w_mod": dw_mod, "b_mod": db_mod[0], "w_in": dw_in,
        "conv_w": vsum[0:4], "conv_b": vsum[4],
        "lru_wa": _block_diag_grad(jnp.sum(dwa, axis=0)), "lru_ba": vsum[5:7],
        "lru_wx": _block_diag_grad(jnp.sum(dwx, axis=0)), "lru_bx": vsum[7:9],
        "lru_lambda": vsum[9:11] * (-jax.nn.sigmoid(-p["lam"])),
        "attn_sink": jnp.sum(dsk[:, :, 0:GROUP, 0], axis=0).reshape(-1),
        "q_norm_g": jnp.sum(gqk[:, 0], axis=0), "k_norm_g": jnp.sum(gqk[:, 1], axis=0),
        "w_branch": dw_br, "w_out": dw_out,
    }
    return dx3, dc16, grads


def _reorder_in_cols(w, d, inverse=False):
    off_new, _ = _layout(d)
    segs = _orig_segments(d)
    if inverse:
        return jnp.concatenate([w[..., off_new[n]:off_new[n] + wd] for n, _, wd in segs], axis=-1)
    by_name = {n: (o, wd) for n, o, wd in segs}
    order = sorted(off_new, key=off_new.get)
    return jnp.concatenate([w[..., by_name[n][0]:by_name[n][0] + by_name[n][1]] for n in order], axis=-1)


def _layer_params(li, w):
    return {
        "li": li, "norm_g": w["norm_g"][li][None], "w_mod": w["w_mod"][li], "b_mod": w["b_mod"][li][None],
        "w_in": w["w_in_r"],
        "conv_w": w["conv_w"][li], "conv_b": w["conv_b"][li][None],
        "wa_bd": _block_diag(w["lru_wa"][li]), "ba": w["lru_ba"][li],
        "wx_bd": _block_diag(w["lru_wx"][li]), "bx": w["lru_bx"][li], "lam": w["lru_lambda"][li],
        "sink": w["attn_sink"][li], "gq": w["q_norm_g"][li][None], "gk": w["k_norm_g"][li][None],
        "w_br": w["w_branch"][li], "w_out": w["w_out"][li],
    }


def _local_step(x, c, ctx, target, c_ctx, final_g, layers):
    b, s, d = x.shape
    ctx_len = ctx.shape[1]
    cos, sin = _rope_tables(ctx_len, s)
    x3 = jnp.concatenate([ctx, x], axis=1)
    c16 = jnp.concatenate([c, c_ctx[None], jnp.zeros((MOD_ROWS - b - 1, d), F32)], axis=0)
    saved = []
    for p in layers:
        x3, sv = _layer_fwd(x3, c16, p, cos, sin, ctx_len)
        saved.append(sv)
    dx3, loss_acc, dgf = _final(x3, final_g[None], target, ctx_len)
    grads = [None] * len(layers)
    dc_ctx = jnp.zeros((d,), F32)
    for li in reversed(range(len(layers))):
        dx3, dc16, grads[li] = _layer_bwd(dx3, saved[li], c16, layers[li], cos, sin, ctx_len)
        dc_ctx = dc_ctx + dc16[b]
    return jnp.sum(loss_acc[:, 0, 0]), dx3[:, ctx_len:], dc_ctx, jnp.sum(dgf[:, 0], axis=0), grads


N_CHIPS = 4
ANY = pl.BlockSpec(memory_space=pl.ANY)


def _place():
    x, y, c = lax.axis_index("x"), lax.axis_index("y"), lax.axis_index("c")
    return x, y, c, [(1 - x, y), (x, 1 - y), (1 - x, 1 - y)]


def _axis_part(ref, axis, start, size):
    idx = [slice(None)] * len(ref.shape)
    idx[axis] = pl.ds(start, size)
    return ref.at[tuple(idx)]


def _remote(src, dst, send, recv, dev):
    return pltpu.make_async_remote_copy(src_ref=src, dst_ref=dst, send_sem=send, recv_sem=recv, device_id=dev,
                                        device_id_type=MESH)


COPY_PIECES = 8


def _pieces(src, dst):
    shape = src.shape
    for ax in range(len(shape) - 1):
        if shape[ax] % COPY_PIECES == 0 and shape[ax] // COPY_PIECES >= 8:
            sz = shape[ax] // COPY_PIECES
            return [(_axis_part(src, ax, j * sz, sz), _axis_part(dst, ax, j * sz, sz)) for j in range(COPY_PIECES)]
    return [(src, dst)]


def _gather_chips(wholes, axes, name):
    n = len(wholes)

    def body(*refs):
        bufs = refs[n:2 * n]
        send, recv, fsend, frecv = refs[2 * n:]
        x, y, c, chips = _place()
        me = 2 * x + y
        sib = (x, y, 1 - c)

        def block(i, chip_index, half):
            sz = wholes[i].shape[axes[i]] // N_CHIPS
            hl = wholes[i].shape[0] // 2
            return _axis_part(bufs[i], axes[i], chip_index * sz, sz).at[pl.ds(half * hl, hl)]

        for i in range(n):
            for k, (px, py) in enumerate(chips):
                _remote(block(i, me, c), block(i, me, c), send.at[i, k], recv.at[i, k], (px, py, c)).start()
        for i in range(n):
            for k, (px, py) in enumerate(chips):
                landed = block(i, 2 * px + py, c)
                _remote(landed, landed, send.at[i, k], recv.at[i, k], (px, py, c)).wait_recv()
                for s_, d_ in _pieces(landed, landed):
                    _remote(s_, d_, fsend.at[i, k], frecv.at[i, k], sib).start()
        for i in range(n):
            for k, (px, py) in enumerate(chips):
                passed = _remote(block(i, 2 * px + py, c), block(i, 2 * px + py, 1 - c), fsend.at[i, k], frecv.at[i, k], sib)
                passed.wait_recv()
                passed.wait_send()
                _remote(block(i, me, c), block(i, me, c), send.at[i, k], recv.at[i, k], (px, py, c)).wait_send()

    sems = pltpu.SemaphoreType.DMA((n, 3))
    return pl.pallas_call(
        body, name=name, in_specs=[ANY] * n, out_specs=tuple([ANY] * n),
        out_shape=tuple(jax.ShapeDtypeStruct(a.shape, a.dtype) for a in wholes),
        input_output_aliases={i: i for i in range(n)},
        scratch_shapes=[sems, sems, sems, sems],
    )(*wholes)


def _place_shard(shard, axis, dtype, name):
    cols_mode = axis == shard.ndim - 1
    assert cols_mode or axis == shard.ndim - 2
    sv = shard.reshape(-1, shard.shape[-2], shard.shape[-1])
    la, r, c = sv.shape
    br = _row_block(r, c)
    per = r // br

    def body(x_ref, o_ref):
        o_ref[...] = x_ref[...].astype(dtype)

    if cols_mode:
        out_shape, out_map = (la, r, N_CHIPS * c), lambda l, i: (l, i, _chip())
    else:
        out_shape, out_map = (la, N_CHIPS * r, c), lambda l, i: (l, _chip() * per + i, 0)
    out = pl.pallas_call(
        body, name=name, grid=(la, per),
        in_specs=[pl.BlockSpec((1, br, c), lambda l, i: (l, i, 0))], out_specs=pl.BlockSpec((1, br, c), out_map),
        out_shape=jax.ShapeDtypeStruct(out_shape, dtype), compiler_params=_cparams(("parallel", "parallel")))(sv)
    shape = list(shard.shape)
    shape[axis] *= N_CHIPS
    return out.reshape(tuple(shape))


def _split_cores(gs, name):
    n = len(gs)

    def body(*refs):
        ins, got = refs[:n], refs[n:2 * n]
        send, recv = refs[2 * n:]
        x, y, c, _ = _place()
        sib = (x, y, 1 - c)

        def theirs(i):
            hl = gs[i].shape[0] // 2
            return ins[i].at[pl.ds((1 - c) * hl, hl)]

        for i in range(n):
            for s_, d_ in _pieces(theirs(i), got[i]):
                _remote(s_, d_, send.at[i], recv.at[i], sib).start()
        for i in range(n):
            _remote(theirs(i), got[i], send.at[i], recv.at[i], sib).wait()

    return pl.pallas_call(
        body, name=name, in_specs=[ANY] * n, out_specs=tuple([ANY] * n),
        out_shape=tuple(jax.ShapeDtypeStruct((g.shape[0] // 2,) + g.shape[1:], g.dtype) for g in gs),
        scratch_shapes=[pltpu.SemaphoreType.DMA((n,)), pltpu.SemaphoreType.DMA((n,))],
    )(*gs)


def _scatter_chips(pbs, axes, name):
    n = len(pbs)

    def block(p, ax):
        shape = list(p.shape)
        shape[ax] //= N_CHIPS
        return tuple(shape)

    def body(*refs):
        inb, got = refs[:n], refs[n:2 * n]
        send, recv = refs[2 * n:]
        x, y, c, chips = _place()

        def part(i, chip_index):
            sz = pbs[i].shape[axes[i]] // N_CHIPS
            return _axis_part(inb[i], axes[i], chip_index * sz, sz)

        for i in range(n):
            for k, (px, py) in enumerate(chips):
                _remote(part(i, 2 * px + py), got[i].at[k], send.at[i, k], recv.at[i, k], (px, py, c)).start()
        for i in range(n):
            for k, (px, py) in enumerate(chips):
                _remote(part(i, 2 * px + py), got[i].at[k], send.at[i, k], recv.at[i, k], (px, py, c)).wait()

    return pl.pallas_call(
        body, name=name, in_specs=[ANY] * n, out_specs=tuple([ANY] * n),
        out_shape=tuple(jax.ShapeDtypeStruct((3,) + block(p, ax), p.dtype) for p, ax in zip(pbs, axes)),
        scratch_shapes=[pltpu.SemaphoreType.DMA((n, 3)), pltpu.SemaphoreType.DMA((n, 3))],
    )(*pbs)


def _join_cores(bufs, name):
    n = len(bufs)

    def body(*refs):
        outs = refs[n:2 * n]
        send, recv = refs[2 * n:]
        x, y, c, _ = _place()
        sib = (x, y, 1 - c)

        def half(i, which):
            hl = bufs[i].shape[0] // 2
            return outs[i].at[pl.ds(which * hl, hl)]

        for i in range(n):
            for s_, d_ in _pieces(half(i, c), half(i, c)):
                _remote(s_, d_, send.at[i], recv.at[i], sib).start()
        for i in range(n):
            cp = _remote(half(i, c), half(i, 1 - c), send.at[i], recv.at[i], sib)
            cp.wait_recv()
            cp.wait_send()

    return pl.pallas_call(
        body, name=name, in_specs=[ANY] * n, out_specs=tuple([ANY] * n),
        out_shape=tuple(jax.ShapeDtypeStruct(a.shape, a.dtype) for a in bufs),
        input_output_aliases={i: i for i in range(n)},
        scratch_shapes=[pltpu.SemaphoreType.DMA((n,)), pltpu.SemaphoreType.DMA((n,))],
    )(*bufs)


def _all_reduce_small(buf):
    r = buf.shape[0]

    def body(in_ref, out_ref, sib_buf, chip_sum, got, send, recv):
        x, y, c, chips = _place()
        cp = _remote(in_ref, sib_buf, send.at[0], recv.at[0], (x, y, 1 - c))
        cp.start()
        cp.wait()
        chip_sum[...] = in_ref[...] + sib_buf[...]
        cps = [_remote(chip_sum, got.at[k], send.at[1 + k], recv.at[1 + k], (px, py, c)) for k, (px, py) in enumerate(chips)]
        for cp in cps:
            cp.start()
        for cp in cps:
            cp.wait()
        out_ref[...] = (chip_sum[...] + got[0]) + (got[1] + got[2])

    return pl.pallas_call(
        body, name="all_reduce_small", out_shape=jax.ShapeDtypeStruct(buf.shape, F32),
        in_specs=[pl.BlockSpec(memory_space=pltpu.VMEM)], out_specs=pl.BlockSpec(memory_space=pltpu.VMEM),
        scratch_shapes=[pltpu.VMEM((r, 128), F32), pltpu.VMEM((r, 128), F32), pltpu.VMEM((3, r, 128), F32),
                        pltpu.SemaphoreType.DMA((4,)), pltpu.SemaphoreType.DMA((4,))],
        compiler_params=_cparams(None, VMEM_LIMIT),
    )(buf)


ELEMENTWISE_BLOCK_BYTES = 1 << 20


def _view2d(a):
    cols = a.shape[-1] if a.ndim > 1 else 128
    return a.reshape(-1, cols)


def _row_block(rows, cols):
    want = max(8, ELEMENTWISE_BLOCK_BYTES // (4 * cols))
    br = rows
    while br > want and br % 2 == 0 and (br // 2) % 16 == 0:
        br //= 2
    return br


def _core():
    return lax.axis_index("c")


def _chip():
    return 2 * lax.axis_index("x") + lax.axis_index("y")


def _sum_half(g, got, name):
    h = got.shape[0]
    gv = g.reshape(2 * h, -1, g.shape[-1])
    tv = got.reshape(h, -1, g.shape[-1])
    _, rows, cols = tv.shape
    br = _row_block(rows, cols)

    def body(g_ref, t_ref, p_ref, pb_ref):
        p = g_ref[...] + t_ref[...]
        p_ref[...] = p
        pb_ref[...] = p.astype(BF16)

    blk = pl.BlockSpec((1, br, cols), lambda l, i: (l, i, 0))
    p, pb = pl.pallas_call(
        body, name=name, grid=(h, rows // br),
        in_specs=[pl.BlockSpec((1, br, cols), lambda l, i: (_core() * h + l, i, 0)), blk], out_specs=(blk, blk),
        out_shape=(jax.ShapeDtypeStruct(tv.shape, F32), jax.ShapeDtypeStruct(tv.shape, BF16)),
        compiler_params=_cparams(("parallel", "parallel")))(gv, tv)
    return p.reshape(got.shape), pb.reshape(got.shape)


def _sum_blocks(p, got3, axis, name):
    h = p.shape[0]
    blk_shape = got3.shape[1:]
    cols_mode = axis == p.ndim - 1
    pv = p.reshape(-1, p.shape[-2], p.shape[-1])
    tv = got3.reshape(3, -1, blk_shape[-2], blk_shape[-1])
    la, rb, cb = tv.shape[1:]
    assert cols_mode or axis == p.ndim - 2
    br = _row_block(rb, cb)
    per = rb // br

    def body(p_ref, a_ref, b_ref, c_ref, out_ref):
        out_ref[...] = ((p_ref[...] + a_ref[0].astype(F32)) + b_ref[0].astype(F32)) + c_ref[0].astype(F32)

    if cols_mode:
        p_spec = pl.BlockSpec((1, br, cb), lambda l, i: (l, i, _chip()))
    else:
        p_spec = pl.BlockSpec((1, br, cb), lambda l, i: (l, _chip() * per + i, 0))
    out = pl.pallas_call(
        body, name=name, grid=(la, per),
        in_specs=[p_spec] + [pl.BlockSpec((1, 1, br, cb), lambda l, i, k=k: (k, l, i, 0)) for k in range(3)],
        out_specs=pl.BlockSpec((1, br, cb), lambda l, i: (_core() * la + l, i, 0)),
        out_shape=jax.ShapeDtypeStruct((2 * la, rb, cb), F32),
        compiler_params=_cparams(("parallel", "parallel")))(pv, tv, tv, tv)
    return out.reshape((2 * h,) + blk_shape[1:])


def _adamw(w, g, m, v, name):
    shape = w.shape
    ops = [_view2d(a) for a in (w, g, m, v)]
    rows, cols = ops[0].shape
    br = _row_block(rows, cols)
    c1 = 1.0 - ADAM_B1 ** ADAM_STEP
    c2 = 1.0 - ADAM_B2 ** ADAM_STEP

    def body(w_ref, g_ref, m_ref, v_ref, d_ref, nm_ref, nv_ref):
        g_ = g_ref[...]
        nm = ADAM_B1 * m_ref[...] + (1.0 - ADAM_B1) * g_
        nv = ADAM_B2 * v_ref[...] + (1.0 - ADAM_B2) * (g_ * g_)
        d_ref[...] = -ADAM_LR * ((nm / c1) / (jnp.sqrt(nv / c2) + ADAM_EPS) + ADAM_WD * w_ref[...])
        nm_ref[...] = nm
        nv_ref[...] = nv

    blk = pl.BlockSpec((br, cols), lambda i: (i, 0))
    outs = pl.pallas_call(body, name=name, grid=(rows // br,), in_specs=[blk] * 4, out_specs=(blk, blk, blk),
                          out_shape=tuple(jax.ShapeDtypeStruct((rows, cols), F32) for _ in range(3)),
                          compiler_params=_cparams(("parallel",)))(*ops)
    return tuple(o.reshape(shape) for o in outs)


PACK_ROWS = 2048


def _packed_rows(shape):
    return -(-int(np.prod(shape)) // (8 * 128)) * 8


def _pack(arrays):
    pieces = []
    for a in arrays:
        flat = a.reshape(-1)
        pieces.append(jnp.pad(flat, (0, _packed_rows(a.shape) * 128 - flat.shape[0])).reshape(-1, 128))
    rows = sum(p.shape[0] for p in pieces)
    pieces.append(jnp.zeros((-rows % PACK_ROWS, 128), F32))
    return jnp.concatenate(pieces, axis=0)


def _unpack(buf, shapes):
    out, o = [], 0
    for s in shapes:
        n, rows = int(np.prod(s)), _packed_rows(s)
        out.append(buf[o:o + rows].reshape(-1)[:n].reshape(s))
        o += rows
    return out


WEIGHTS = ["c_ctx", "norm_g", "w_mod", "b_mod", "w_in", "conv_w", "conv_b", "lru_wa", "lru_ba", "lru_wx", "lru_bx",
           "lru_lambda", "attn_sink", "q_norm_g", "k_norm_g", "w_branch", "w_out", "final_g"]
BIG = {"w_mod": 2, "w_in": 2, "w_branch": 2, "w_out": 1}
SMALL_SHARDED = ["conv_w", "lru_ba", "lru_bx", "lru_lambda"]
REPLICATED = [n for n in WEIGHTS if n not in BIG and n not in SMALL_SHARDED]


def kernel(x, c, ctx, c_ctx, norm_g, w_mod, b_mod, w_in, conv_w, conv_b, lru_wa, lru_ba, lru_wx, lru_bx, lru_lambda, attn_sink, q_norm_g, k_norm_g, w_branch, w_out, final_g, loss_target, m_c_ctx, m_norm_g, m_w_mod, m_b_mod, m_w_in, m_conv_w, m_conv_b, m_lru_wa, m_lru_ba, m_lru_wx, m_lru_bx, m_lru_lambda, m_attn_sink, m_q_norm_g, m_k_norm_g, m_w_branch, m_w_out, m_final_g, v_c_ctx, v_norm_g, v_w_mod, v_b_mod, v_w_in, v_conv_w, v_conv_b, v_lru_wa, v_lru_ba, v_lru_wx, v_lru_bx, v_lru_lambda, v_attn_sink, v_q_norm_g, v_k_norm_g, v_w_branch, v_w_out, v_final_g):
    args = dict(locals())
    w = {n: args[n] for n in WEIGHTS}
    mom = {n: args["m_" + n] for n in WEIGHTS}
    var = {n: args["v_" + n] for n in WEIGHTS}
    depth, d = norm_g.shape
    chip = 2 * lax.axis_index("x") + lax.axis_index("y")

    big_names = list(BIG)
    small_shard = jnp.concatenate([w[n] for n in SMALL_SHARDED], axis=1)
    gather_axes = [BIG[n] for n in big_names] + [2]
    placed = [_place_shard(w[n], BIG[n], BF16, "place_" + n) for n in big_names]
    placed.append(_place_shard(small_shard, 2, F32, "place_small"))
    gathered = _gather_chips(placed, gather_axes, "gather_weights")
    whole = dict(w)
    whole.update(dict(zip(big_names, gathered[:-1])))
    o = 0
    for n in SMALL_SHARDED:
        rows = w[n].shape[1]
        whole[n] = gathered[-1][:, o:o + rows]
        o += rows
    whole["w_in_r"] = _reorder_in_cols(whole["w_in"], d)
    layers = [_layer_params(li, whole) for li in range(depth)]

    loss_local, grad_x, g_c_ctx, g_final, lgrads = _local_step(x, c, ctx, loss_target, c_ctx, final_g, layers)
    loss = lax.psum(loss_local, ("x", "y", "c"))
    full = {n: jnp.stack([lg[n] for lg in lgrads]) for n in lgrads[0]}
    full["w_in"] = _reorder_in_cols(full["w_in"], d, inverse=True)
    full["c_ctx"], full["final_g"] = g_c_ctx, g_final

    bigs = [full[n] for n in big_names]
    got = _split_cores(bigs, "grad_split_cores")
    parts = [_sum_half(g, t_, "grad_chip_sum") for g, t_ in zip(bigs, got)]
    recv = _scatter_chips([pb for _, pb in parts], [BIG[n] for n in big_names], "grad_scatter_chips")
    totals = [_sum_blocks(p_, r, BIG[n], "grad_total") for (p_, _), r, n in zip(parts, recv, big_names)]
    grad = dict(zip(big_names, _join_cores(totals, "grad_join_cores")))

    small_names = REPLICATED + SMALL_SHARDED
    reduced = _unpack(_all_reduce_small(_pack([full[n] for n in small_names])), [full[n].shape for n in small_names])
    for n, g in zip(small_names, reduced):
        if n in SMALL_SHARDED:
            sz = w[n].shape[-1]
            g = lax.dynamic_slice_in_dim(g, chip * sz, sz, axis=g.ndim - 1)
        grad[n] = g

    delta, new_m, new_v = {}, {}, {}
    for n in big_names:
        delta[n], new_m[n], new_v[n] = _adamw(w[n], grad[n], mom[n], var[n], "adamw_" + n)
    shapes = [w[n].shape for n in small_names]
    packed = _adamw(_pack([w[n] for n in small_names]), _pack([grad[n] for n in small_names]),
                    _pack([mom[n] for n in small_names]), _pack([var[n] for n in small_names]), "adamw_small")
    for res, p in zip((delta, new_m, new_v), packed):
        res.update(dict(zip(small_names, _unpack(p, shapes))))

    return (loss, grad_x, *[grad[n] for n in WEIGHTS], *[delta[n] for n in WEIGHTS],
            *[new_m[n] for n in WEIGHTS], *[new_v[n] for n in WEIGHTS])
```

```python
import functools

import jax
import jax.numpy as jnp
import numpy as np
from jax import lax
from jax.experimental import pallas as pl
from jax.experimental.pallas import tpu as pltpu

F32 = jnp.float32
BF16 = jnp.bfloat16

HEAD_DIM = 128
GROUP = 4
LRU_BLOCK_W = 64
LRU_C = 8.0
WINDOW = 128
GRID_W = 64
ROPE_THETA = 10000.0
EPS = 1e-6
NEG_INF = -1e30
ADAM_LR, ADAM_B1, ADAM_B2, ADAM_EPS, ADAM_WD, ADAM_STEP = 0.001, 0.9, 0.999, 1e-08, 0.01, 10

ROW_BLOCK = 256
LRU_LANES = 128
LRU_CHUNK = 128
LRU_UNROLL = 2
DENSE_FWD_BQ = 256
LOG2E = 1.4426950408889634
ATT_BQ = 256
WIN_BQ = 256
WIN_SPAN = WIN_BQ + 2 * WINDOW
MOD_ROWS = 16
VMEM_LIMIT = 56 * 1024 * 1024

MESH = pl.DeviceIdType.MESH


def _cparams(sem=None, vmem=None):
    kw = {}
    if sem is not None:
        kw["dimension_semantics"] = sem
    if vmem is not None:
        kw["vmem_limit_bytes"] = vmem
    return pltpu.CompilerParams(**kw)


def _sigmoid(v):
    return 1.0 / (1.0 + jnp.exp(-v))


def _silu(v):
    return v * _sigmoid(v)


def _dsilu(v):
    s = _sigmoid(v)
    return s * (1.0 + v * (1.0 - s))


def _one_minus_square(log_a, a):
    z2 = log_a * log_a
    series = (-2.0 * a * log_a) * (1.0 + z2 * (1.0 / 6 + z2 * (1.0 / 120 + z2 * (1.0 / 5040))))
    return jnp.where(z2 < 0.25, series, 1.0 - a * a)


def _log1p(y):
    u = 1.0 + y
    d = u - 1.0
    return jnp.where(d == 0.0, y, jnp.log(u) * (y / jnp.where(d == 0.0, 1.0, d)))


def _softplus(x):
    return jnp.maximum(x, 0.0) + _log1p(jnp.exp(-jnp.abs(x)))


def _dot(a, b):
    return jnp.dot(a, b, preferred_element_type=F32)


def _dot_nt(a, b):
    return lax.dot_general(a, b, (((1,), (1,)), ((), ())), preferred_element_type=F32)


def _dot_tn(a, b):
    return lax.dot_general(a, b, (((0,), (0,)), ((), ())), preferred_element_type=F32)


def _colsum(v):
    return jnp.sum(v, axis=0, keepdims=True)


def _layout(d_model):
    kvw = (d_model // HEAD_DIM // GROUP) * HEAD_DIM
    names = ["gA", "gB", "gC", "mA", "mB", "mC", "uA", "qB", "qC", "kB", "vB", "kC", "vC"]
    widths = [d_model] * 9 + [kvw] * 4
    off, o = {}, 0
    for n, w in zip(names, widths):
        off[n] = o
        o += w
    return off, o


def _orig_segments(d_model):
    kvw = (d_model // HEAD_DIM // GROUP) * HEAD_DIM
    names = ["uA", "gA", "qB", "kB", "vB", "gB", "qC", "kC", "vC", "gC", "mA", "mB", "mC"]
    widths = [d_model, d_model, d_model, kvw, kvw, d_model, d_model, kvw, kvw, d_model, d_model, d_model, d_model]
    out, o = [], 0
    for n, w in zip(names, widths):
        out.append((n, o, w))
        o += w
    return out


def _matmul(a, b, *, ta=False, tb=False, out_dtype=F32, bm, bn, bk, name, n_outer=False, a_lead=None, b_lead=None):
    a_shape = a.shape if a_lead is None else a.shape[1:]
    b_shape = b.shape if b_lead is None else b.shape[1:]
    (kdim, m) = a_shape if ta else a_shape[::-1]
    (n, kdim2) = b_shape if tb else b_shape[::-1]
    assert kdim == kdim2 and m % bm == 0 and n % bn == 0 and kdim % bk == 0, (a.shape, b.shape, bm, bn, bk)
    nk = kdim // bk
    dims = (((0 if ta else 1,), (1 if tb else 0,)), ((), ()))

    def ij(f):
        return (lambda j, i, k: f(i, j, k)) if n_outer else f

    def body(a_ref, b_ref, o_ref, *scratch):
        r = lax.dot_general(a_ref[...].astype(BF16), b_ref[...].astype(BF16), dims, preferred_element_type=F32)
        if nk == 1:
            o_ref[...] = r.astype(out_dtype)
        else:
            acc = scratch[0]
            k = pl.program_id(2)

            @pl.when(k == 0)
            def _():
                acc[...] = r

            @pl.when(k > 0)
            def _():
                acc[...] += r

            @pl.when(k == nk - 1)
            def _():
                o_ref[...] = acc[...].astype(out_dtype)

    def spec(shape, f, lead):
        f = ij(f)
        if lead is None:
            return pl.BlockSpec(shape, f)
        return pl.BlockSpec((None,) + shape, lambda *g: (lead,) + f(*g))

    a_spec = spec((bk, bm), lambda i, j, k: (k, i), a_lead) if ta else spec((bm, bk), lambda i, j, k: (i, k), a_lead)
    b_spec = spec((bn, bk), lambda i, j, k: (j, k), b_lead) if tb else spec((bk, bn), lambda i, j, k: (k, j), b_lead)
    return pl.pallas_call(
        body, name=name, grid=(n // bn, m // bm, nk) if n_outer else (m // bm, n // bn, nk),
        in_specs=[a_spec, b_spec], out_specs=pl.BlockSpec((bm, bn), ij(lambda i, j, k: (i, j))),
        out_shape=jax.ShapeDtypeStruct((m, n), out_dtype),
        scratch_shapes=[pltpu.VMEM((bm, bn), F32)] if nk > 1 else [],
        compiler_params=_cparams(("parallel", "parallel", "arbitrary"), VMEM_LIMIT),
    )(a, b)


def _mod_fwd(c16, w_mod, b_mod):
    d3 = w_mod.shape[1]

    def body(c_ref, w_ref, b_ref, o_ref):
        o_ref[...] = _dot(_silu(c_ref[...]).astype(BF16), w_ref[...]) + b_ref[...]

    return pl.pallas_call(body, name="mod_fwd", out_shape=jax.ShapeDtypeStruct((MOD_ROWS, d3), F32),
                          compiler_params=_cparams(None, VMEM_LIMIT))(c16, w_mod, b_mod)


def _mod_bwd(c16, dmod16, w_mod):
    d, d3 = w_mod.shape

    def body(c_ref, g_ref, w_ref, dw_ref, db_ref, dc_ref):
        c = c_ref[...]
        g = g_ref[...]
        gb = g.astype(BF16)
        dw_ref[...] = _dot_tn(_silu(c).astype(BF16), gb)
        db_ref[...] = _colsum(g)
        dc_ref[...] = _dot_nt(gb, w_ref[...]) * _dsilu(c)

    return pl.pallas_call(
        body, name="mod_bwd",
        out_shape=(jax.ShapeDtypeStruct((d, d3), F32), jax.ShapeDtypeStruct((1, d3), F32),
                   jax.ShapeDtypeStruct((MOD_ROWS, d), F32)),
        compiler_params=_cparams(None, VMEM_LIMIT))(c16, dmod16, w_mod)


def _row_kind(t, lb):
    return jnp.where(t >= lb, 1, 0)


def _norm_mod_fwd(x3, g, modsel, ctx_len):
    b, t, d = x3.shape
    bt = ROW_BLOCK
    lb = ctx_len // bt

    def body(x_ref, g_ref, m_ref, h_ref):
        x = x_ref[0]
        rstd = lax.rsqrt(jnp.mean(x * x, axis=-1, keepdims=True) + EPS)
        y = x * rstd * g_ref[...]
        h_ref[0] = (y * (1.0 + m_ref[0, 0, 1:2, :]) + m_ref[0, 0, 0:1, :]).astype(BF16)

    return pl.pallas_call(
        body, name="norm_mod_fwd", grid=(b, t // bt),
        in_specs=[pl.BlockSpec((1, bt, d), lambda i, j: (i, j, 0)),
                  pl.BlockSpec((1, d), lambda i, j: (0, 0)),
                  pl.BlockSpec((1, 1, 8, d), lambda i, j: (i, _row_kind(j, lb), 0, 0))],
        out_specs=pl.BlockSpec((1, bt, d), lambda i, j: (i, j, 0)),
        out_shape=jax.ShapeDtypeStruct((b, t, d), BF16),
        compiler_params=_cparams(("parallel", "arbitrary")),
    )(x3, g, modsel)


def _norm_mod_bwd(dh3, x3, g, modsel, dres3, ctx_len):
    b, t, d = x3.shape
    bt = ROW_BLOCK
    lb = ctx_len // bt

    def body(dh_ref, x_ref, g_ref, m_ref, dres_ref, dx_ref, acc_ref):
        j = pl.program_id(1)
        x = x_ref[0]
        dh = dh_ref[0]
        g_row = g_ref[...]
        rstd = lax.rsqrt(jnp.mean(x * x, axis=-1, keepdims=True) + EPS)
        xhat = x * rstd
        dhpre = dh * (1.0 + m_ref[0, 0, 1:2, :])
        dxhat = dhpre * g_row
        dx = rstd * (dxhat - xhat * jnp.mean(dxhat * xhat, axis=-1, keepdims=True))
        dx_ref[0] = dx + dres_ref[0]

        @pl.when((j == 0) | (j == lb))
        def _():
            acc_ref[...] = jnp.zeros_like(acc_ref)

        acc_ref[0, 0, 0:1, :] += _colsum(dh)
        acc_ref[0, 0, 1:2, :] += _colsum(dh * (xhat * g_row))
        acc_ref[0, 0, 2:3, :] += _colsum(dhpre * xhat)

    blk = pl.BlockSpec((1, bt, d), lambda i, j: (i, j, 0))
    return pl.pallas_call(
        body, name="norm_mod_bwd", grid=(b, t // bt),
        in_specs=[blk, blk, pl.BlockSpec((1, d), lambda i, j: (0, 0)),
                  pl.BlockSpec((1, 1, 8, d), lambda i, j: (i, _row_kind(j, lb), 0, 0)), blk],
        out_specs=(blk, pl.BlockSpec((1, 1, 8, d), lambda i, j: (i, _row_kind(j, lb), 0, 0))),
        out_shape=(jax.ShapeDtypeStruct((b, t, d), F32), jax.ShapeDtypeStruct((b, 2, 8, d), F32)),
        compiler_params=_cparams(("parallel", "arbitrary")),
    )(dh3, x3, g, modsel, dres3)


def _shifted_rows(ref, c, off, ctx_len, total):
    ct = LRU_CHUNK
    r0 = pl.multiple_of(c * ct, ct)
    x0 = ref[pl.ds(r0, ct), :]
    row = lax.broadcasted_iota(jnp.int32, x0.shape, 0)
    if off < 0:
        k = -off
        has = jnp.logical_and(r0 != 0, r0 != ctx_len)
        p0 = pl.multiple_of(jnp.maximum(r0 - 8, 0), 8)
        edge = jnp.where(has, ref[pl.ds(p0, 8), :], 0.0)
        out = pltpu.roll(x0, k, 0)
        for j in range(k):
            out = jnp.where(row == j, edge[8 - k + j:8 - k + j + 1, :], out)
    else:
        k = off
        has = jnp.logical_and(r0 + ct != ctx_len, r0 + ct != total)
        n0 = pl.multiple_of(jnp.minimum(r0 + ct, total - 8), 8)
        edge = jnp.where(has, ref[pl.ds(n0, 8), :], 0.0)
        out = pltpu.roll(x0, ct - k, 0)
        for j in range(k):
            out = jnp.where(row == ct - k + j, edge[j:j + 1, :], out)
    return out


def _chunk_scan(a, b, reverse):
    n = a.shape[0]
    row = lax.broadcasted_iota(jnp.int32, a.shape, 0)
    s = 1
    while s < n:
        if reverse:
            a_s, b_s, ok = pltpu.roll(a, n - s, 0), pltpu.roll(b, n - s, 0), row < n - s
        else:
            a_s, b_s, ok = pltpu.roll(a, s, 0), pltpu.roll(b, s, 0), row >= s
        b = jnp.where(ok, a * b_s + b, b)
        a = jnp.where(ok, a * a_s, a)
        s *= 2
    return a, b


def _loop_chunks(n, body, init):
    assert n % LRU_UNROLL == 0

    def group(s2, carry):
        for u in range(LRU_UNROLL):
            carry = body(LRU_UNROLL * s2 + u, carry)
        return carry

    return lax.fori_loop(0, n // LRU_UNROLL, group, init)


def _lru_order(d, s, n_ctx, n_all):
    if d == 0:
        return s
    return jnp.where(s < n_ctx, n_ctx - 1 - s, n_all - 1 - (s - n_ctx))


def _lru_gates(u, wa, ba, wx, bx, sp):
    ub = u.astype(BF16)
    r = _sigmoid(_dot(ub, wa) + ba)
    i = _sigmoid(_dot(ub, wx) + bx)
    log_a = (-LRU_C * sp) * r
    a = jnp.exp(log_a)
    sf = jnp.sqrt(_one_minus_square(log_a, a))
    return ub, r, i, a, sf


def _lru_specs(t, n_lane_blocks_offset):
    ln = LRU_LANES
    return [
        pl.BlockSpec((4, ln), lambda i, j: (0, j)),
        pl.BlockSpec((1, ln), lambda i, j: (0, j)),
        pl.BlockSpec((2, 1, ln, ln), lambda i, j: (0, j, 0, 0)),
        pl.BlockSpec((2, ln), lambda i, j: (0, j)),
        pl.BlockSpec((2, 1, ln, ln), lambda i, j: (0, j, 0, 0)),
        pl.BlockSpec((2, ln), lambda i, j: (0, j)),
        pl.BlockSpec((2, ln), lambda i, j: (0, j)),
    ]


def _widen(src_ref, dst, n_chunks):
    ct = LRU_CHUNK

    def copy(c, _):
        r0 = pl.multiple_of(c * ct, ct)
        dst[pl.ds(r0, ct), :] = src_ref[0, pl.ds(r0, ct), :].astype(F32)
        return 0

    lax.fori_loop(0, n_chunks, copy, 0)


def _lru_conv(ua_ref, cw_ref, cb_ref, u_s, ctx_len, total):
    ct = LRU_CHUNK

    def conv(c, _):
        r0 = pl.multiple_of(c * ct, ct)
        u = (cw_ref[0:1, :] * _shifted_rows(ua_ref, c, -2, ctx_len, total)
             + cw_ref[1:2, :] * _shifted_rows(ua_ref, c, -1, ctx_len, total)
             + cw_ref[2:3, :] * ua_ref[pl.ds(r0, ct), :]
             + cw_ref[3:4, :] * _shifted_rows(ua_ref, c, 1, ctx_len, total) + cb_ref[...])
        u_s[pl.ds(r0, ct), :] = u
        return 0

    lax.fori_loop(0, total // ct, conv, 0)


def _lru_fwd(proj3, col0, conv_w, conv_b, wa_bd, ba, wx_bd, bx, lam, ctx_len):
    b, t, _ = proj3.shape
    d = conv_w.shape[1]
    ln, ct = LRU_LANES, LRU_CHUNK
    n_all, n_ctx = t // ct, ctx_len // ct
    cb0 = col0 // ln

    def body(ua_ref, cw_ref, cb_ref, wa_ref, ba_ref, wx_ref, bx_ref, lam_ref, y_ref, u_s, h1_s, ua):
        _widen(ua_ref, ua, n_all)
        _lru_conv(ua, cw_ref, cb_ref, u_s, ctx_len, t)
        par = [(_softplus(-lam_ref[dr:dr + 1, :]), wa_ref[dr, 0], wx_ref[dr, 0], ba_ref[dr:dr + 1, :], bx_ref[dr:dr + 1, :])
               for dr in (0, 1)]

        def step(s, carry):
            out = []
            for dr in (0, 1):
                sp, wa, wx, ba_row, bx_row = par[dr]
                c = _lru_order(dr, s, n_ctx, n_all)
                r0 = pl.multiple_of(c * ct, ct)
                u = u_s[pl.ds(r0, ct), :]
                _, _, i, a, sf = _lru_gates(u, wa, ba_row, wx, bx_row, sp)
                aa, h0 = _chunk_scan(a, sf * (i * u), reverse=(dr == 1))
                h = h0 + aa * carry[dr]
                if dr == 0:
                    y_ref[0, pl.ds(r0, ct), :] = h
                    out.append(h[ct - 1:ct, :])
                else:
                    h1_s[pl.ds(r0, ct), :] = h
                    out.append(h[0:1, :])
            return tuple(out)

        zrow = jnp.zeros((1, ln), F32)
        _loop_chunks(n_all, step, (zrow, zrow))

        def add(c, _):
            r0 = pl.multiple_of(c * ct, ct)
            y_ref[0, pl.ds(r0, ct), :] += h1_s[pl.ds(r0, ct), :]
            return 0

        lax.fori_loop(0, n_all, add, 0)

    return pl.pallas_call(
        body, name="lru_fwd", grid=(b, d // ln),
        in_specs=[pl.BlockSpec((1, t, ln), lambda i, j: (i, 0, cb0 + j))] + _lru_specs(t, cb0),
        out_specs=pl.BlockSpec((1, t, ln), lambda i, j: (i, 0, j)),
        out_shape=jax.ShapeDtypeStruct((b, t, d), F32),
        scratch_shapes=[pltpu.VMEM((t, ln), F32)] * 3,
        compiler_params=_cparams(("parallel", "parallel"), VMEM_LIMIT),
    )(proj3, conv_w, conv_b, wa_bd, ba, wx_bd, bx, lam)


def _lru_bwd(dproj3, proj3, col0, dy3, conv_w, conv_b, wa_bd, ba, wx_bd, bx, lam, ctx_len):
    b, t, _ = proj3.shape
    d = conv_w.shape[1]
    ln, ct = LRU_LANES, LRU_CHUNK
    n_all, n_ctx = t // ct, ctx_len // ct
    cb0 = col0 // ln

    def body(dproj_hbm, ua_ref, dy_ref, cw_ref, cb_ref, wa_ref, ba_ref, wx_ref, bx_ref, lam_ref,
             dua_ref, vec_ref, dwa_ref, dwx_ref, u_s, du_s, ua, h_s, a_s, sf_s, i_s, r_s):
        del dproj_hbm
        _widen(ua_ref, ua, n_all)
        _lru_conv(ua, cw_ref, cb_ref, u_s, ctx_len, t)
        du_s[...] = jnp.zeros_like(du_s)
        vec_ref[...] = jnp.zeros_like(vec_ref)
        par = [(_softplus(-lam_ref[dr:dr + 1, :]), wa_ref[dr, 0], wx_ref[dr, 0], ba_ref[dr:dr + 1, :], bx_ref[dr:dr + 1, :])
               for dr in (0, 1)]

        def fwd_one(dr, s, carry):
            sp, wa, wx, ba_row, bx_row = par[dr]
            c = _lru_order(dr, s, n_ctx, n_all)
            r0 = pl.multiple_of(c * ct, ct)
            u = u_s[pl.ds(r0, ct), :]
            _, r, i, a, sf = _lru_gates(u, wa, ba_row, wx, bx_row, sp)
            aa, h0 = _chunk_scan(a, sf * (i * u), reverse=(dr == 1))
            h = h0 + aa * carry
            h_s[dr, pl.ds(r0, ct), :] = h
            a_s[dr, pl.ds(r0, ct), :] = a
            sf_s[dr, pl.ds(r0, ct), :] = sf
            i_s[dr, pl.ds(r0, ct), :] = i
            r_s[dr, pl.ds(r0, ct), :] = r
            return h[ct - 1:ct, :] if dr == 0 else h[0:1, :]

        def bwd_one(dr, sr, carry):
            sp, wa, wx, _, _ = par[dr]
            gc, vacc = carry
            hs = h_s.at[dr]
            c = _lru_order(dr, n_all - 1 - sr, n_ctx, n_all)
            r0 = pl.multiple_of(c * ct, ct)
            u = u_s[pl.ds(r0, ct), :]
            h = hs[pl.ds(r0, ct), :]
            dy = dy_ref[0, pl.ds(r0, ct), :]
            ub = u.astype(BF16)
            r, i = r_s[dr, pl.ds(r0, ct), :], i_s[dr, pl.ds(r0, ct), :]
            a, sf = a_s[dr, pl.ds(r0, ct), :], sf_s[dr, pl.ds(r0, ct), :]
            row = lax.broadcasted_iota(jnp.int32, a.shape, 0)
            if dr == 0:
                alpha = jnp.where(row == ct - 1, 1.0, pltpu.roll(a, ct - 1, 0))
                aa, g0 = _chunk_scan(alpha, dy, reverse=True)
                g = g0 + aa * gc
                gc_new = a[0:1, :] * g[0:1, :]
                p0 = pl.multiple_of(jnp.maximum(r0 - 8, 0), 8)
                edge = jnp.where(r0 != 0, hs[pl.ds(p0, 8), :], 0.0)[7:8, :]
                h_prev = jnp.where(row == 0, edge, pltpu.roll(h, 1, 0))
            else:
                alpha = jnp.where(row == 0, 1.0, pltpu.roll(a, 1, 0))
                aa, g0 = _chunk_scan(alpha, dy, reverse=False)
                g = g0 + aa * gc
                gc_new = a[ct - 1:ct, :] * g[ct - 1:ct, :]
                r_end = r0 + ct
                n0 = pl.multiple_of(jnp.where(r_end == t, 0, jnp.minimum(r_end, t - 8)), 8)
                edge = jnp.where(r_end != ctx_len, hs[pl.ds(n0, 8), :], 0.0)[0:1, :]
                h_prev = jnp.where(row == ct - 1, edge, pltpu.roll(h, ct - 1, 0))
            da = g * h_prev
            iu = i * u
            diu = g * sf
            dlog_a = da * a - (g * iu) * (a * a) / sf
            dpre_r = (dlog_a * (-LRU_C * sp)) * (r * (1.0 - r))
            dpre_i = (diu * u) * (i * (1.0 - i))
            dpr_b, dpi_b = dpre_r.astype(BF16), dpre_i.astype(BF16)
            du = diu * i + _dot_nt(dpr_b, wa) + _dot_nt(dpi_b, wx)
            du_s[pl.ds(r0, ct), :] += du
            dwa_ref[0, dr, 0] += _dot_tn(ub, dpr_b)
            dwx_ref[0, dr, 0] += _dot_tn(ub, dpi_b)
            vacc = (vacc[0] + _colsum(dpre_r), vacc[1] + _colsum(dpre_i), vacc[2] + _colsum(dlog_a * (-LRU_C * r)))
            return gc_new, vacc

        zrow = jnp.zeros((1, ln), F32)
        _loop_chunks(n_all, lambda s, cr: tuple(fwd_one(dr, s, cr[dr]) for dr in (0, 1)), (zrow, zrow))
        dwa_ref[...] = jnp.zeros_like(dwa_ref)
        dwx_ref[...] = jnp.zeros_like(dwx_ref)
        init = (zrow, (zrow, zrow, zrow))
        res = _loop_chunks(n_all, lambda sr, cr: tuple(bwd_one(dr, sr, cr[dr]) for dr in (0, 1)), (init, init))
        for dr in (0, 1):
            vacc = res[dr][1]
            vec_ref[0, 5 + dr:6 + dr, :] = vacc[0]
            vec_ref[0, 7 + dr:8 + dr, :] = vacc[1]
            vec_ref[0, 9 + dr:10 + dr, :] = vacc[2]

        def conv_bwd(c, acc):
            r0 = pl.multiple_of(c * ct, ct)
            du = du_s[pl.ds(r0, ct), :]
            dua = (cw_ref[0:1, :] * _shifted_rows(du_s, c, 2, ctx_len, t)
                   + cw_ref[1:2, :] * _shifted_rows(du_s, c, 1, ctx_len, t)
                   + cw_ref[2:3, :] * du
                   + cw_ref[3:4, :] * _shifted_rows(du_s, c, -1, ctx_len, t))
            dua_ref[0, pl.ds(r0, ct), :] = dua.astype(BF16)
            return (acc[0] + _colsum(du * _shifted_rows(ua, c, -2, ctx_len, t)),
                    acc[1] + _colsum(du * _shifted_rows(ua, c, -1, ctx_len, t)),
                    acc[2] + _colsum(du * ua[pl.ds(r0, ct), :]),
                    acc[3] + _colsum(du * _shifted_rows(ua, c, 1, ctx_len, t)),
                    acc[4] + _colsum(du))

        zrow = jnp.zeros((1, ln), F32)
        acc = lax.fori_loop(0, n_all, conv_bwd, (zrow,) * 5)
        for k in range(5):
            vec_ref[0, k:k + 1, :] = acc[k]

    ng = d // ln
    return pl.pallas_call(
        body, name="lru_bwd", grid=(b, ng),
        in_specs=[pl.BlockSpec(memory_space=pl.ANY),
                  pl.BlockSpec((1, t, ln), lambda i, j: (i, 0, cb0 + j)),
                  pl.BlockSpec((1, t, ln), lambda i, j: (i, 0, j))] + _lru_specs(t, cb0),
        out_specs=(pl.BlockSpec((1, t, ln), lambda i, j: (i, 0, cb0 + j)),
                   pl.BlockSpec((1, 16, ln), lambda i, j: (i, 0, j)),
                   pl.BlockSpec((1, 2, 1, ln, ln), lambda i, j: (i, 0, j, 0, 0)),
                   pl.BlockSpec((1, 2, 1, ln, ln), lambda i, j: (i, 0, j, 0, 0))),
        out_shape=(jax.ShapeDtypeStruct(dproj3.shape, dproj3.dtype),
                   jax.ShapeDtypeStruct((b, 16, d), F32),
                   jax.ShapeDtypeStruct((b, 2, ng, ln, ln), F32),
                   jax.ShapeDtypeStruct((b, 2, ng, ln, ln), F32)),
        scratch_shapes=[pltpu.VMEM((t, ln), F32)] * 3 + [pltpu.VMEM((2, t, ln), F32)] * 5,
        input_output_aliases={0: 0},
        compiler_params=_cparams(("parallel", "parallel"), VMEM_LIMIT),
    )(dproj3, proj3, dy3, conv_w, conv_b, wa_bd, ba, wx_bd, bx, lam)


def _rope_tables(ctx_len, seq):
    p = HEAD_DIM // 4
    inv = ROPE_THETA ** (-jnp.arange(p, dtype=F32) / p)
    tok = jnp.arange(seq)
    ang_r = (tok // GRID_W)[:, None] * inv
    ang_c = (tok % GRID_W)[:, None] * inv
    cos = jnp.concatenate([jnp.cos(ang_r)] * 2 + [jnp.cos(ang_c)] * 2, axis=1)
    sin = jnp.concatenate([-jnp.sin(ang_r), jnp.sin(ang_r), -jnp.sin(ang_c), jnp.sin(ang_c)], axis=1)
    cos = jnp.concatenate([jnp.ones((ctx_len, HEAD_DIM), F32), cos], axis=0)
    sin = jnp.concatenate([jnp.zeros((ctx_len, HEAD_DIM), F32), sin], axis=0)
    return cos, sin


def _swap_halves(v):
    lane = lax.broadcasted_iota(jnp.int32, v.shape, 1)
    return jnp.where((lane & 63) < 32, pltpu.roll(v, 96, 1), pltpu.roll(v, 32, 1))


def _head_rstd(v):
    return lax.rsqrt(jnp.mean(v * v, axis=-1, keepdims=True) + EPS)


QKV_BLOCK = GROUP * HEAD_DIM
PREP_ROWS = (2176, 256)


def _prep_fwd(proj3, qcol, kvcol, d, cos, sin, gq, gk, use_norm):
    b, t, _ = proj3.shape
    bt, wb = _pick(t, PREP_ROWS), QKV_BLOCK
    nqb = d // wb
    assert qcol % wb == 0 and kvcol % wb == 0 and d // HEAD_DIM // GROUP == 2
    qb0, kvb = qcol // wb, kvcol // wb

    def body(p_ref, cos_ref, sin_ref, gq_ref, gk_ref, o_ref):
        s = pl.program_id(2)
        c, sn = cos_ref[...], sin_ref[...]

        def rope(v):
            return v * c + _swap_halves(v) * sn

        @pl.when(s < nqb)
        def _():
            for hh in range(GROUP):
                v = p_ref[0, :, hh * HEAD_DIM:(hh + 1) * HEAD_DIM].astype(F32)
                if use_norm:
                    v = v * _head_rstd(v) * gq_ref[...]
                o_ref[0, :, hh * HEAD_DIM:(hh + 1) * HEAD_DIM] = rope(v).astype(BF16)

        @pl.when(s == nqb)
        def _():
            for hh in range(2):
                v = p_ref[0, :, hh * HEAD_DIM:(hh + 1) * HEAD_DIM].astype(F32)
                if use_norm:
                    v = v * _head_rstd(v) * gk_ref[...]
                o_ref[0, :, hh * HEAD_DIM:(hh + 1) * HEAD_DIM] = rope(v).astype(BF16)
            o_ref[0, :, 2 * HEAD_DIM:] = p_ref[0, :, 2 * HEAD_DIM:]

    return pl.pallas_call(
        body, name="prep_fwd_norm" if use_norm else "prep_fwd", grid=(b, t // bt, nqb + 1),
        in_specs=[pl.BlockSpec((1, bt, wb), lambda i, j, s: (i, j, jnp.where(s < nqb, qb0 + s, kvb))),
                  pl.BlockSpec((bt, HEAD_DIM), lambda i, j, s: (j, 0)),
                  pl.BlockSpec((bt, HEAD_DIM), lambda i, j, s: (j, 0)),
                  pl.BlockSpec((1, HEAD_DIM), lambda i, j, s: (0, 0)),
                  pl.BlockSpec((1, HEAD_DIM), lambda i, j, s: (0, 0))],
        out_specs=pl.BlockSpec((1, bt, wb), lambda i, j, s: (i, j, s)),
        out_shape=jax.ShapeDtypeStruct((b, t, d + wb), BF16),
        compiler_params=_cparams(("parallel", "parallel", "arbitrary"), VMEM_LIMIT),
    )(proj3, cos, sin, gq, gk)


def _prep_bwd(dproj3, dq3, dkt, dvt, proj3, qcol, kvcol, d, cos, sin, gq, gk, use_norm):
    b, t, _ = proj3.shape
    bt, wb = _pick(t, PREP_ROWS), QKV_BLOCK
    nqb = d // wb
    qb0, kvb = qcol // wb, kvcol // wb
    kvh = dkt.shape[1]

    def body(dproj_hbm, dq_ref, dkt_ref, dvt_ref, p_ref, cos_ref, sin_ref, gq_ref, gk_ref, o_ref, gacc_ref):
        del dproj_hbm
        j, s = pl.program_id(1), pl.program_id(2)
        c, sn = cos_ref[...], sin_ref[...]

        @pl.when((j == 0) & (s == 0))
        def _():
            gacc_ref[...] = jnp.zeros_like(gacc_ref)

        def unrope(dv):
            return dv * c + _swap_halves(dv * sn)

        def head_bwd(dyv, xv, g_ref, acc_row):
            dyv = unrope(dyv)
            if not use_norm:
                return dyv
            rstd = _head_rstd(xv)
            xhat = xv * rstd
            gacc_ref[0, acc_row:acc_row + 1, :] += _colsum(dyv * xhat)
            dxhat = dyv * g_ref[...]
            return rstd * (dxhat - xhat * jnp.mean(dxhat * xhat, axis=-1, keepdims=True))

        @pl.when(s < nqb)
        def _():
            for hh in range(GROUP):
                sl = slice(hh * HEAD_DIM, (hh + 1) * HEAD_DIM)
                o_ref[0, :, sl] = head_bwd(dq_ref[0, :, sl], p_ref[0, :, sl].astype(F32), gq_ref, 0).astype(BF16)

        @pl.when(s == nqb)
        def _():
            for hh in range(kvh):
                sl = slice(hh * HEAD_DIM, (hh + 1) * HEAD_DIM)
                o_ref[0, :, sl] = head_bwd(dkt_ref[0, hh].T, p_ref[0, :, sl].astype(F32), gk_ref, 1).astype(BF16)
                sv = slice((kvh + hh) * HEAD_DIM, (kvh + hh + 1) * HEAD_DIM)
                o_ref[0, :, sv] = dvt_ref[0, hh].T.astype(BF16)

    col = lambda i, j, s: (i, j, jnp.where(s < nqb, qb0 + s, kvb))
    return pl.pallas_call(
        body, name="prep_bwd_norm" if use_norm else "prep_bwd", grid=(b, t // bt, nqb + 1),
        in_specs=[pl.BlockSpec(memory_space=pl.ANY),
                  pl.BlockSpec((1, bt, wb), lambda i, j, s: (i, j, jnp.minimum(s, nqb - 1))),
                  pl.BlockSpec((1, kvh, HEAD_DIM, bt), lambda i, j, s: (i, 0, 0, j)),
                  pl.BlockSpec((1, kvh, HEAD_DIM, bt), lambda i, j, s: (i, 0, 0, j)),
                  pl.BlockSpec((1, bt, wb), col),
                  pl.BlockSpec((bt, HEAD_DIM), lambda i, j, s: (j, 0)),
                  pl.BlockSpec((bt, HEAD_DIM), lambda i, j, s: (j, 0)),
                  pl.BlockSpec((1, HEAD_DIM), lambda i, j, s: (0, 0)),
                  pl.BlockSpec((1, HEAD_DIM), lambda i, j, s: (0, 0))],
        out_specs=(pl.BlockSpec((1, bt, wb), col), pl.BlockSpec((1, 8, HEAD_DIM), lambda i, j, s: (i, 0, 0))),
        out_shape=(jax.ShapeDtypeStruct(dproj3.shape, dproj3.dtype), jax.ShapeDtypeStruct((b, 8, HEAD_DIM), F32)),
        input_output_aliases={0: 0},
        compiler_params=_cparams(("parallel", "arbitrary", "arbitrary"), VMEM_LIMIT),
    )(dproj3, dq3, dkt, dvt, proj3, cos, sin, gq, gk)


def _stack_heads(ref, dtype=None):
    parts = [ref[0, :, g * HEAD_DIM:(g + 1) * HEAD_DIM] for g in range(GROUP)]
    v = jnp.concatenate(parts, axis=0)
    return v if dtype is None else v.astype(dtype)


def _unstack_heads(ref, v, bq):
    for g in range(GROUP):
        ref[0, :, g * HEAD_DIM:(g + 1) * HEAD_DIM] = v[g * bq:(g + 1) * bq, :]


def _attn_specs(t, d, bq):
    kvh = d // HEAD_DIM // GROUP
    kc0 = d // HEAD_DIM
    q_spec = pl.BlockSpec((1, bq, QKV_BLOCK), lambda i, h, j: (i, j, h))
    k_spec = pl.BlockSpec((1, t, HEAD_DIM), lambda i, h, j: (i, 0, kc0 + h))
    v_spec = pl.BlockSpec((1, t, HEAD_DIM), lambda i, h, j: (i, 0, kc0 + kvh + h))
    lse_spec = pl.BlockSpec((1, GROUP, bq, HEAD_DIM), lambda i, h, j: (i, h, j, 0))
    kt_spec = pl.BlockSpec((1, 1, HEAD_DIM, t), lambda i, h, j: (i, h, 0, 0))
    return kvh, q_spec, k_spec, v_spec, lse_spec, kt_spec


SCALE = HEAD_DIM ** -0.5


def _attn_dense_fwd(qkv, d, ctx_len):
    b, t, _ = qkv.shape
    bq = DENSE_FWD_BQ
    lq = ctx_len // bq
    kvh, q_spec, k_spec, v_spec, lse_spec, _ = _attn_specs(t, d, bq)

    def body(q_ref, k_ref, v_ref, o_ref, lse_ref):
        i = pl.program_id(2)

        def attend(k, v):
            for g in range(GROUP):
                sl = slice(g * HEAD_DIM, (g + 1) * HEAD_DIM)
                s = _dot_nt(q_ref[0, :, sl], k)
                m = jnp.max(s, axis=1, keepdims=True)
                p = jnp.exp2((s - m) * (SCALE * LOG2E))
                l = jnp.sum(p, axis=1, keepdims=True)
                o_ref[0, :, sl] = _dot(p.astype(BF16), v) / l
                lse_ref[0, g] = jnp.broadcast_to(m * SCALE + jnp.log(l), (bq, HEAD_DIM))

        @pl.when(i < lq)
        def _():
            attend(k_ref[0, 0:ctx_len, :], v_ref[0, 0:ctx_len, :])

        @pl.when(i >= lq)
        def _():
            attend(k_ref[0], v_ref[0])

    return pl.pallas_call(
        body, name="attn_dense_fwd", grid=(b, kvh, t // bq),
        in_specs=[q_spec, k_spec, v_spec], out_specs=(q_spec, lse_spec),
        out_shape=(jax.ShapeDtypeStruct((b, t, d), F32), jax.ShapeDtypeStruct((b, kvh * GROUP, t, HEAD_DIM), F32)),
        compiler_params=_cparams(("parallel", "parallel", "arbitrary"), VMEM_LIMIT),
    )(qkv, qkv, qkv)


def _attn_dense_bwd(qkv, o3, do3, lse, d, ctx_len):
    b, t, _ = qkv.shape
    bq = ATT_BQ
    lq = ctx_len // bq
    kvh, q_spec, k_spec, v_spec, lse_spec, kt_spec = _attn_specs(t, d, bq)

    def body(q_ref, k_ref, v_ref, o_ref, do_ref, lse_ref, dq_ref, dkt_ref, dvt_ref):
        i = pl.program_id(2)

        @pl.when(i == 0)
        def _():
            dkt_ref[...] = jnp.zeros_like(dkt_ref)
            dvt_ref[...] = jnp.zeros_like(dvt_ref)

        def run(k, v, width):
            dk_acc = dv_acc = None
            for g in range(GROUP):
                sl = slice(g * HEAD_DIM, (g + 1) * HEAD_DIM)
                q = q_ref[0, :, sl]
                do = do_ref[0, :, sl]
                dd = jnp.sum(do * o_ref[0, :, sl], axis=1, keepdims=True)
                dob = do.astype(BF16)
                p = jnp.exp2(_dot_nt(q, k) * (SCALE * LOG2E) - lse_ref[0, g][:, 0:1] * LOG2E)
                ds = (p * (_dot_nt(dob, v) - dd) * SCALE).astype(BF16)
                dq_ref[0, :, sl] = _dot(ds, k)
                dk_g = _dot(q.astype(F32).T.astype(BF16), ds)
                dv_g = _dot(do.T.astype(BF16), p.astype(BF16))
                dk_acc = dk_g if dk_acc is None else dk_acc + dk_g
                dv_acc = dv_g if dv_acc is None else dv_acc + dv_g
            dkt_ref[0, 0, :, 0:width] += dk_acc
            dvt_ref[0, 0, :, 0:width] += dv_acc

        @pl.when(i < lq)
        def _():
            run(k_ref[0, 0:ctx_len, :], v_ref[0, 0:ctx_len, :], ctx_len)

        @pl.when(i >= lq)
        def _():
            run(k_ref[0], v_ref[0], t)

    return pl.pallas_call(
        body, name="attn_dense_bwd", grid=(b, kvh, t // bq),
        in_specs=[q_spec, k_spec, v_spec, q_spec, q_spec, lse_spec], out_specs=(q_spec, kt_spec, kt_spec),
        out_shape=(jax.ShapeDtypeStruct((b, t, d), F32), jax.ShapeDtypeStruct((b, kvh, HEAD_DIM, t), F32),
                   jax.ShapeDtypeStruct((b, kvh, HEAD_DIM, t), F32)),
        compiler_params=_cparams(("parallel", "parallel", "arbitrary"), VMEM_LIMIT),
    )(qkv, qkv, qkv, o3, do3, lse)


def _sink_column(sink_ref, h, bq):
    rowi = lax.broadcasted_iota(jnp.int32, (GROUP * bq, 1), 0)
    col = jnp.zeros((GROUP * bq, 1), F32)
    for g in range(GROUP):
        col = jnp.where((rowi >= g * bq) & (rowi < (g + 1) * bq), sink_ref[h * GROUP + g], col)
    return col


def _band(i, lq, ctx_len, t, bq):
    n = i - lq
    start = pl.multiple_of(jnp.clip(ctx_len + n * bq - WINDOW, ctx_len, t - WIN_SPAN), WINDOW)
    shape = (GROUP * bq, WIN_SPAN)
    kpos = start - ctx_len + lax.broadcasted_iota(jnp.int32, shape, 1)
    qpos = n * bq + (lax.broadcasted_iota(jnp.int32, shape, 0) & (bq - 1))
    return start, jnp.abs(kpos - qpos) <= WINDOW


def _attn_win_fwd(qkv, sink, d, ctx_len):
    b, t, _ = qkv.shape
    bq = WIN_BQ
    rows = GROUP * bq
    lq = ctx_len // bq
    kvh, q_spec, k_spec, v_spec, lse_spec, _ = _attn_specs(t, d, bq)

    def body(sink_ref, q_ref, k_ref, v_ref, o_ref, lse_ref):
        h, i = pl.program_id(1), pl.program_id(2)
        q4 = _stack_heads(q_ref)
        sink_col = _sink_column(sink_ref, h, bq)
        sc = _dot_nt(q4, k_ref[0, 0:ctx_len, :]) * SCALE
        mc = jnp.maximum(jnp.max(sc, axis=1, keepdims=True), sink_col)

        def finish(m, l, acc):
            _unstack_heads(o_ref, acc / l, bq)
            lse_ref[0] = jnp.broadcast_to(m + jnp.log(l), (rows, HEAD_DIM)).reshape(GROUP, bq, HEAD_DIM)

        @pl.when(i < lq)
        def _():
            pc = jnp.exp(sc - mc)
            l = jnp.sum(pc, axis=1, keepdims=True) + jnp.exp(sink_col - mc)
            finish(mc, l, _dot(pc.astype(BF16), v_ref[0, 0:ctx_len, :]))

        @pl.when(i >= lq)
        def _():
            start, ok = _band(i, lq, ctx_len, t, bq)
            sb = jnp.where(ok, _dot_nt(q4, k_ref[0, pl.ds(start, WIN_SPAN), :]) * SCALE, NEG_INF)
            m = jnp.maximum(mc, jnp.max(sb, axis=1, keepdims=True))
            pc, pb = jnp.exp(sc - m), jnp.exp(sb - m)
            l = jnp.sum(pc, axis=1, keepdims=True) + jnp.sum(pb, axis=1, keepdims=True) + jnp.exp(sink_col - m)
            acc = _dot(pc.astype(BF16), v_ref[0, 0:ctx_len, :]) + _dot(pb.astype(BF16), v_ref[0, pl.ds(start, WIN_SPAN), :])
            finish(m, l, acc)

    return pl.pallas_call(
        body, name="attn_win_fwd", grid=(b, kvh, t // bq),
        in_specs=[pl.BlockSpec(memory_space=pltpu.SMEM), q_spec, k_spec, v_spec], out_specs=(q_spec, lse_spec),
        out_shape=(jax.ShapeDtypeStruct((b, t, d), F32), jax.ShapeDtypeStruct((b, kvh * GROUP, t, HEAD_DIM), F32)),
        compiler_params=_cparams(("parallel", "parallel", "arbitrary"), VMEM_LIMIT),
    )(sink, qkv, qkv, qkv)


def _attn_win_bwd(qkv, sink, o3, do3, lse, d, ctx_len):
    b, t, _ = qkv.shape
    bq = WIN_BQ
    rows = GROUP * bq
    lq = ctx_len // bq
    kvh, q_spec, k_spec, v_spec, lse_spec, kt_spec = _attn_specs(t, d, bq)

    def body(sink_ref, q_ref, k_ref, v_ref, o_ref, do_ref, lse_ref, dq_ref, dkt_ref, dvt_ref, dsk_ref):
        h, i = pl.program_id(1), pl.program_id(2)

        @pl.when(i == 0)
        def _():
            dkt_ref[...] = jnp.zeros_like(dkt_ref)
            dvt_ref[...] = jnp.zeros_like(dvt_ref)
            dsk_ref[...] = jnp.zeros_like(dsk_ref)

        q4 = _stack_heads(q_ref)
        do4 = _stack_heads(do_ref)
        dd = jnp.sum(do4 * _stack_heads(o_ref), axis=1, keepdims=True)
        lse_col = lse_ref[0].reshape(rows, HEAD_DIM)[:, 0:1]
        do4b = do4.astype(BF16)
        qt = q4.astype(F32).T.astype(BF16)
        dot = do4.T.astype(BF16)

        def part(k, v):
            return _dot_nt(q4, k) * SCALE, _dot_nt(do4b, v)

        def grads(p, dp, k):
            ds = (p * (dp - dd) * SCALE).astype(BF16)
            return _dot(ds, k), _dot(qt, ds), _dot(dot, p.astype(BF16))

        kc = k_ref[0, 0:ctx_len, :]
        sc, dpc = part(kc, v_ref[0, 0:ctx_len, :])
        dq_c, dk_c, dv_c = grads(jnp.exp(sc - lse_col), dpc, kc)
        dkt_ref[0, 0, :, 0:ctx_len] += dk_c
        dvt_ref[0, 0, :, 0:ctx_len] += dv_c
        _unstack_heads(dq_ref, dq_c, bq)

        @pl.when(i >= lq)
        def _():
            start, ok = _band(i, lq, ctx_len, t, bq)
            kb = k_ref[0, pl.ds(start, WIN_SPAN), :]
            sb, dpb = part(kb, v_ref[0, pl.ds(start, WIN_SPAN), :])
            pb = jnp.where(ok, jnp.exp(sb - lse_col), 0.0)
            dq_b, dk_b, dv_b = grads(pb, dpb, kb)
            dkt_ref[0, 0, :, pl.ds(start, WIN_SPAN)] += dk_b
            dvt_ref[0, 0, :, pl.ds(start, WIN_SPAN)] += dv_b
            for g in range(GROUP):
                dq_ref[0, :, g * HEAD_DIM:(g + 1) * HEAD_DIM] += dq_b[g * bq:(g + 1) * bq, :]

        ps = jnp.exp(_sink_column(sink_ref, h, bq) - lse_col) * dd
        for g in range(GROUP):
            val = jnp.sum(ps[g * bq:(g + 1) * bq, :], axis=0, keepdims=True)
            dsk_ref[0, 0, g:g + 1, :] -= jnp.broadcast_to(val, (1, HEAD_DIM))

    return pl.pallas_call(
        body, name="attn_win_bwd", grid=(b, kvh, t // bq),
        in_specs=[pl.BlockSpec(memory_space=pltpu.SMEM), q_spec, k_spec, v_spec, q_spec, q_spec, lse_spec],
        out_specs=(q_spec, kt_spec, kt_spec, pl.BlockSpec((1, 1, 8, HEAD_DIM), lambda i, h, j: (i, h, 0, 0))),
        out_shape=(jax.ShapeDtypeStruct((b, t, d), F32), jax.ShapeDtypeStruct((b, kvh, HEAD_DIM, t), F32),
                   jax.ShapeDtypeStruct((b, kvh, HEAD_DIM, t), F32), jax.ShapeDtypeStruct((b, kvh, 8, HEAD_DIM), F32)),
        compiler_params=_cparams(("parallel", "parallel", "arbitrary"), VMEM_LIMIT),
    )(sink, qkv, qkv, qkv, o3, do3, lse)


MERGE_BWD_ROWS = 256
MERGE_BWD_VMEM = 60 * 1024 * 1024


def _resident(shape):
    return pl.BlockSpec(shape, lambda *_: (0,) * len(shape), pipeline_mode=pl.Buffered(1))


def _merge_fwd(x3, ya, yb, yc, proj3, w_br, w_out, modsel, ctx_len):
    b, t, d = x3.shape
    bt = ROW_BLOCK
    lb = ctx_len // bt

    def body(x_ref, ya_ref, yb_ref, yc_ref, gm_ref, wbr_ref, wo_ref, m_ref, xn_ref, out_ref):
        mix = jnp.zeros((bt, d), F32)
        for n, y_ref in enumerate((ya_ref, yb_ref, yc_ref)):
            z = (y_ref[0] * _silu(gm_ref[0, :, n * d:(n + 1) * d].astype(F32))).astype(BF16)
            mix = mix + _sigmoid(gm_ref[0, :, (3 + n) * d:(4 + n) * d].astype(F32)) * _dot(z, wbr_ref[n])
        o = _dot(mix.astype(BF16), wo_ref[...])
        out_ref[0] = o
        xn_ref[0] = x_ref[0] + m_ref[0, 0, 2:3, :] * o

    blk = pl.BlockSpec((1, bt, d), lambda i, j: (i, j, 0))
    return pl.pallas_call(
        body, name="merge_fwd", grid=(b, t // bt),
        in_specs=[blk, blk, blk, blk, pl.BlockSpec((1, bt, 6 * d), lambda i, j: (i, j, 0)),
                  _resident((3, d, d)), _resident((d, d)),
                  pl.BlockSpec((1, 1, 8, d), lambda i, j: (i, _row_kind(j, lb), 0, 0))],
        out_specs=(blk, blk),
        out_shape=(jax.ShapeDtypeStruct((b, t, d), F32), jax.ShapeDtypeStruct((b, t, d), F32)),
        compiler_params=_cparams(("parallel", "arbitrary"), VMEM_LIMIT),
    )(x3, ya, yb, yc, proj3, w_br, w_out, modsel)


def _merge_bwd(dxn3, out3, ya, yb, yc, proj3, w_br, w_out, modsel, ctx_len):
    b, t, d = dxn3.shape
    n_cols = proj3.shape[2]
    bt = MERGE_BWD_ROWS
    lb = ctx_len // bt

    def body(dxn_ref, out_ref, ya_ref, yb_ref, yc_ref, gm_ref, wbr_ref, wo_ref, m_ref,
             dgm_ref, dya_ref, dyb_ref, dyc_ref, z_ref, dt_ref, mix_ref, dout_ref, gacc_ref):
        j = pl.program_id(1)
        dxn = dxn_ref[0]
        doutb = (m_ref[0, 0, 2:3, :] * dxn).astype(BF16)
        dout_ref[0] = doutb

        @pl.when((j == 0) | (j == lb))
        def _():
            gacc_ref[...] = jnp.zeros_like(gacc_ref)

        gacc_ref[0, 0, 0:1, :] += _colsum(dxn * out_ref[0])
        dmix = _dot_nt(doutb, wo_ref[...])
        mix = jnp.zeros((bt, d), F32)
        for n, (y_ref, dy_ref) in enumerate(((ya_ref, dya_ref), (yb_ref, dyb_ref), (yc_ref, dyc_ref))):
            g = gm_ref[0, :, n * d:(n + 1) * d].astype(F32)
            y = y_ref[0]
            sig_g = _sigmoid(g)
            silu_g = g * sig_g
            z = (y * silu_g).astype(BF16)
            z_ref[n, 0] = z
            tn = _dot(z, wbr_ref[n])
            s = _sigmoid(gm_ref[0, :, (3 + n) * d:(4 + n) * d].astype(F32))
            mix = mix + s * tn
            dgm_ref[0, :, (3 + n) * d:(4 + n) * d] = (dmix * tn * (s * (1.0 - s))).astype(BF16)
            dtb = (dmix * s).astype(BF16)
            dt_ref[n, 0] = dtb
            dz = _dot_nt(dtb, wbr_ref[n])
            dy_ref[0] = dz * silu_g
            dgm_ref[0, :, n * d:(n + 1) * d] = (dz * y * (sig_g * (1.0 + g * (1.0 - sig_g)))).astype(BF16)
        mix_ref[0] = mix.astype(BF16)

    blk = pl.BlockSpec((1, bt, d), lambda i, j: (i, j, 0))
    blk4 = pl.BlockSpec((3, 1, bt, d), lambda i, j: (0, i, j, 0))
    wide = pl.BlockSpec((1, bt, 6 * d), lambda i, j: (i, j, 0))
    return pl.pallas_call(
        body, name="merge_bwd", grid=(b, t // bt),
        in_specs=[blk, blk, blk, blk, blk, wide, _resident((3, d, d)), _resident((d, d)),
                  pl.BlockSpec((1, 1, 8, d), lambda i, j: (i, _row_kind(j, lb), 0, 0))],
        out_specs=(wide, blk, blk, blk, blk4, blk4, blk, blk,
                   pl.BlockSpec((1, 1, 8, d), lambda i, j: (i, _row_kind(j, lb), 0, 0))),
        out_shape=(jax.ShapeDtypeStruct((b, t, n_cols), BF16),
                   jax.ShapeDtypeStruct((b, t, d), F32), jax.ShapeDtypeStruct((b, t, d), F32),
                   jax.ShapeDtypeStruct((b, t, d), F32),
                   jax.ShapeDtypeStruct((3, b, t, d), BF16), jax.ShapeDtypeStruct((3, b, t, d), BF16),
                   jax.ShapeDtypeStruct((b, t, d), BF16), jax.ShapeDtypeStruct((b, t, d), BF16),
                   jax.ShapeDtypeStruct((b, 2, 8, d), F32)),
        compiler_params=_cparams(("parallel", "arbitrary"), MERGE_BWD_VMEM),
    )(dxn3, out3, ya, yb, yc, proj3, w_br, w_out, modsel)


def _final(x3, g, target, ctx_len):
    b, t, d = x3.shape
    bt = ROW_BLOCK
    lb = ctx_len // bt

    def body(x_ref, g_ref, t_ref, dx_ref, loss_ref, dg_ref):
        j = pl.program_id(1)

        @pl.when(j == 0)
        def _():
            loss_ref[...] = jnp.zeros_like(loss_ref)
            dg_ref[...] = jnp.zeros_like(dg_ref)

        @pl.when(j < lb)
        def _():
            dx_ref[...] = jnp.zeros_like(dx_ref)

        @pl.when(j >= lb)
        def _():
            x = x_ref[0]
            g_row = g_ref[...]
            rstd = lax.rsqrt(jnp.mean(x * x, axis=-1, keepdims=True) + EPS)
            xhat = x * rstd
            err = xhat * g_row - t_ref[0]
            loss_ref[...] += (0.5 / d) * jnp.sum(err * err)
            dy = err * (1.0 / d)
            dg_ref[0, 0:1, :] += _colsum(dy * xhat)
            dxhat = dy * g_row
            dx_ref[0] = rstd * (dxhat - xhat * jnp.mean(dxhat * xhat, axis=-1, keepdims=True))

    blk = pl.BlockSpec((1, bt, d), lambda i, j: (i, j, 0))
    return pl.pallas_call(
        body, name="final_loss", grid=(b, t // bt),
        in_specs=[blk, pl.BlockSpec((1, d), lambda i, j: (0, 0)),
                  pl.BlockSpec((1, bt, d), lambda i, j: (i, jnp.maximum(j - lb, 0), 0))],
        out_specs=(blk, pl.BlockSpec((1, 8, HEAD_DIM), lambda i, j: (i, 0, 0)), pl.BlockSpec((1, 8, d), lambda i, j: (i, 0, 0))),
        out_shape=(jax.ShapeDtypeStruct((b, t, d), F32), jax.ShapeDtypeStruct((b, 8, HEAD_DIM), F32),
                   jax.ShapeDtypeStruct((b, 8, d), F32)),
        compiler_params=_cparams(("parallel", "arbitrary")),
    )(x3, g, target)


TOKEN_BLOCKS = (1088, 512, 256, 128)


def _pick(n, options):
    for o in options:
        if n % o == 0:
            return o
    raise ValueError((n, options))


def _block_diag(w):
    per = LRU_LANES // LRU_BLOCK_W
    nd, nb, bw, _ = w.shape
    wr = w.reshape(nd, nb // per, per, bw, bw)
    eye = jnp.eye(per, dtype=w.dtype)
    bd = wr[:, :, :, :, None, :] * eye[None, None, :, None, :, None]
    return bd.reshape(nd, nb // per, per * bw, per * bw).astype(BF16)


def _block_diag_grad(g):
    per = LRU_LANES // LRU_BLOCK_W
    nd, ng, _, _ = g.shape
    gr = g.reshape(nd, ng, per, LRU_BLOCK_W, per, LRU_BLOCK_W)
    diag = jnp.stack([gr[:, :, k, :, k, :] for k in range(per)], axis=2)
    return diag.reshape(nd, ng * per, LRU_BLOCK_W, LRU_BLOCK_W)


def _mod_select(mod16, b, d):
    m3 = mod16.reshape(MOD_ROWS, 3, d)
    lat = m3[:b]
    ctx = jnp.broadcast_to(m3[b][None], (b, 3, d))
    sel = jnp.stack([ctx, lat], axis=1)
    return jnp.pad(sel, ((0, 0), (0, 0), (0, 5), (0, 0)))


def _layer_fwd(x3, c16, p, cos, sin, ctx_len):
    b, t, d = x3.shape
    off, n_cols = _layout(d)
    mod16 = _mod_fwd(c16, p["w_mod"], p["b_mod"])
    modsel = _mod_select(mod16, b, d)
    h = _norm_mod_fwd(x3, p["norm_g"], modsel, ctx_len)
    proj = _matmul(h.reshape(b * t, d), p["w_in"], bm=_pick(b * t, TOKEN_BLOCKS), bn=1024, bk=d, name="proj_fwd",
                   n_outer=True, b_lead=p["li"], out_dtype=BF16)
    proj3 = proj.reshape(b, t, n_cols)
    ya = _lru_fwd(proj3, off["uA"], p["conv_w"], p["conv_b"], p["wa_bd"], p["ba"], p["wx_bd"], p["bx"], p["lam"], ctx_len)
    qkv_b = _prep_fwd(proj3, off["qB"], off["kB"], d, cos, sin, p["gq"], p["gk"], use_norm=False)
    yb, lse_b = _attn_win_fwd(qkv_b, p["sink"], d, ctx_len)
    qkv_c = _prep_fwd(proj3, off["qC"], off["kC"], d, cos, sin, p["gq"], p["gk"], use_norm=True)
    yc, lse_c = _attn_dense_fwd(qkv_c, d, ctx_len)
    x_new, out3 = _merge_fwd(x3, ya, yb, yc, proj3, p["w_br"], p["w_out"], modsel, ctx_len)
    return x_new, (x3, modsel, h, proj3, ya, yb, yc, qkv_b, lse_b, qkv_c, lse_c, out3)


def _layer_bwd(dxn3, saved, c16, p, cos, sin, ctx_len):
    x3, modsel, h, proj3, ya, yb, yc, qkv_b, lse_b, qkv_c, lse_c, out3 = saved
    b, t, d = x3.shape
    off, n_cols = _layout(d)
    rows = b * t
    bk = _pick(rows, (2 * TOKEN_BLOCKS[0],) + TOKEN_BLOCKS)
    dproj3, dya, dyb, dyc, z4, dt4, mixb, doutb, gacc = _merge_bwd(dxn3, out3, ya, yb, yc, proj3, p["w_br"], p["w_out"],
                                                                   modsel, ctx_len)
    dw_br = jnp.stack([_matmul(z4.reshape(3, rows, d), dt4.reshape(3, rows, d), ta=True, bm=d, bn=d, bk=bk,
                               name="dw_branch", a_lead=n, b_lead=n) for n in range(3)])
    dw_out = _matmul(mixb.reshape(rows, d), doutb.reshape(rows, d), ta=True, bm=d, bn=d, bk=bk, name="dw_out")
    dproj3, vec, dwa, dwx = _lru_bwd(dproj3, proj3, off["uA"], dya, p["conv_w"], p["conv_b"], p["wa_bd"], p["ba"],
                                     p["wx_bd"], p["bx"], p["lam"], ctx_len)
    dq_b, dkt_b, dvt_b, dsk = _attn_win_bwd(qkv_b, p["sink"], yb, dyb, lse_b, d, ctx_len)
    dproj3, _ = _prep_bwd(dproj3, dq_b, dkt_b, dvt_b, proj3, off["qB"], off["kB"], d, cos, sin, p["gq"], p["gk"], False)
    dq_c, dkt_c, dvt_c = _attn_dense_bwd(qkv_c, yc, dyc, lse_c, d, ctx_len)
    dproj3, gqk = _prep_bwd(dproj3, dq_c, dkt_c, dvt_c, proj3, off["qC"], off["kC"], d, cos, sin, p["gq"], p["gk"], True)
    dproj2 = dproj3.reshape(rows, n_cols)
    dw_in = _matmul(h.reshape(rows, d), dproj2, ta=True, bm=d, bn=1024, bk=bk, name="dw_in")
    dh = _matmul(dproj2, p["w_in"], tb=True, bm=_pick(rows, TOKEN_BLOCKS), bn=d, bk=_pick(n_cols, (2560, 1024)),
                 name="dh", b_lead=p["li"])
    dx3, nacc = _norm_mod_bwd(dh.reshape(b, t, d), x3, p["norm_g"], modsel, dxn3, ctx_len)
    per = jnp.stack([nacc[:, :, 0], nacc[:, :, 1], gacc[:, :, 0]], axis=2)
    dmod = jnp.concatenate([per[:, 1].reshape(b, 3 * d), jnp.sum(per[:, 0], axis=0).reshape(1, 3 * d)], axis=0)
    dmod16 = jnp.pad(dmod, ((0, MOD_ROWS - b - 1), (0, 0)))
    dw_mod, db_mod, dc16 = _mod_bwd(c16, dmod16, p["w_mod"])
    vsum = jnp.sum(vec, axis=0)
    grads = {
        "norm_g": jnp.sum(nacc[:, :, 2], axis=(0, 1)),
        "w_mod": dw_mod, "b_mod": db_mod[0], "w_in": dw_in,
        "conv_w": vsum[0:4], "conv_b": vsum[4],
        "lru_wa": _block_diag_grad(jnp.sum(dwa, axis=0)), "lru_ba": vsum[5:7],
        "lru_wx": _block_diag_grad(jnp.sum(dwx, axis=0)), "lru_bx": vsum[7:9],
        "lru_lambda": vsum[9:11] * (-jax.nn.sigmoid(-p["lam"])),
        "attn_sink": jnp.sum(dsk[:, :, 0:GROUP, 0], axis=0).reshape(-1),
        "q_norm_g": jnp.sum(gqk[:, 0], axis=0), "k_norm_g": jnp.sum(gqk[:, 1], axis=0),
        "w_branch": dw_br, "w_out": dw_out,
    }
    return dx3, dc16, grads


def _reorder_in_cols(w, d, inverse=False):
    off_new, _ = _layout(d)
    segs = _orig_segments(d)
    if inverse:
        return jnp.concatenate([w[..., off_new[n]:off_new[n] + wd] for n, _, wd in segs], axis=-1)
    by_name = {n: (o, wd) for n, o, wd in segs}
    order = sorted(off_new, key=off_new.get)
    return jnp.concatenate([w[..., by_name[n][0]:by_name[n][0] + by_name[n][1]] for n in order], axis=-1)


def _layer_params(li, w):
    return {
        "li": li, "norm_g": w["norm_g"][li][None], "w_mod": w["w_mod"][li], "b_mod": w["b_mod"][li][None],
        "w_in": w["w_in_r"],
        "conv_w": w["conv_w"][li], "conv_b": w["conv_b"][li][None],
        "wa_bd": _block_diag(w["lru_wa"][li]), "ba": w["lru_ba"][li],
        "wx_bd": _block_diag(w["lru_wx"][li]), "bx": w["lru_bx"][li], "lam": w["lru_lambda"][li],
        "sink": w["attn_sink"][li], "gq": w["q_norm_g"][li][None], "gk": w["k_norm_g"][li][None],
        "w_br": w["w_branch"][li], "w_out": w["w_out"][li],
    }


def _local_step(x, c, ctx, target, c_ctx, final_g, layers):
    b, s, d = x.shape
    ctx_len = ctx.shape[1]
    cos, sin = _rope_tables(ctx_len, s)
    x3 = jnp.concatenate([ctx, x], axis=1)
    c16 = jnp.concatenate([c, c_ctx[None], jnp.zeros((MOD_ROWS - b - 1, d), F32)], axis=0)
    saved = []
    for p in layers:
        x3, sv = _layer_fwd(x3, c16, p, cos, sin, ctx_len)
        saved.append(sv)
    dx3, loss_acc, dgf = _final(x3, final_g[None], target, ctx_len)
    grads = [None] * len(layers)
    dc_ctx = jnp.zeros((d,), F32)
    for li in reversed(range(len(layers))):
        dx3, dc16, grads[li] = _layer_bwd(dx3, saved[li], c16, layers[li], cos, sin, ctx_len)
        dc_ctx = dc_ctx + dc16[b]
    return jnp.sum(loss_acc[:, 0, 0]), dx3[:, ctx_len:], dc_ctx, jnp.sum(dgf[:, 0], axis=0), grads


N_CHIPS = 4
ANY = pl.BlockSpec(memory_space=pl.ANY)


def _place():
    x, y, c = lax.axis_index("x"), lax.axis_index("y"), lax.axis_index("c")
    return x, y, c, [(1 - x, y), (x, 1 - y), (1 - x, 1 - y)]


def _axis_part(ref, axis, start, size):
    idx = [slice(None)] * len(ref.shape)
    idx[axis] = pl.ds(start, size)
    return ref.at[tuple(idx)]


def _remote(src, dst, send, recv, dev):
    return pltpu.make_async_remote_copy(src_ref=src, dst_ref=dst, send_sem=send, recv_sem=recv, device_id=dev,
                                        device_id_type=MESH)


COPY_PIECES = 8


def _pieces(src, dst):
    shape = src.shape
    for ax in range(len(shape) - 1):
        if shape[ax] % COPY_PIECES == 0 and shape[ax] // COPY_PIECES >= 8:
            sz = shape[ax] // COPY_PIECES
            return [(_axis_part(src, ax, j * sz, sz), _axis_part(dst, ax, j * sz, sz)) for j in range(COPY_PIECES)]
    return [(src, dst)]


def _gather_chips(wholes, axes, name):
    n = len(wholes)

    def body(*refs):
        bufs = refs[n:2 * n]
        send, recv, fsend, frecv = refs[2 * n:]
        x, y, c, chips = _place()
        me = 2 * x + y
        sib = (x, y, 1 - c)

        def block(i, chip_index, half):
            sz = wholes[i].shape[axes[i]] // N_CHIPS
            hl = wholes[i].shape[0] // 2
            return _axis_part(bufs[i], axes[i], chip_index * sz, sz).at[pl.ds(half * hl, hl)]

        for i in range(n):
            for k, (px, py) in enumerate(chips):
                _remote(block(i, me, c), block(i, me, c), send.at[i, k], recv.at[i, k], (px, py, c)).start()
        for i in range(n):
            for k, (px, py) in enumerate(chips):
                landed = block(i, 2 * px + py, c)
                _remote(landed, landed, send.at[i, k], recv.at[i, k], (px, py, c)).wait_recv()
                for s_, d_ in _pieces(landed, landed):
                    _remote(s_, d_, fsend.at[i, k], frecv.at[i, k], sib).start()
        for i in range(n):
            for k, (px, py) in enumerate(chips):
                passed = _remote(block(i, 2 * px + py, c), block(i, 2 * px + py, 1 - c), fsend.at[i, k], frecv.at[i, k], sib)
                passed.wait_recv()
                passed.wait_send()
                _remote(block(i, me, c), block(i, me, c), send.at[i, k], recv.at[i, k], (px, py, c)).wait_send()

    sems = pltpu.SemaphoreType.DMA((n, 3))
    return pl.pallas_call(
        body, name=name, in_specs=[ANY] * n, out_specs=tuple([ANY] * n),
        out_shape=tuple(jax.ShapeDtypeStruct(a.shape, a.dtype) for a in wholes),
        input_output_aliases={i: i for i in range(n)},
        scratch_shapes=[sems, sems, sems, sems],
    )(*wholes)


def _place_shard(shard, axis, dtype, name):
    cols_mode = axis == shard.ndim - 1
    assert cols_mode or axis == shard.ndim - 2
    sv = shard.reshape(-1, shard.shape[-2], shard.shape[-1])
    la, r, c = sv.shape
    br = _row_block(r, c)
    per = r // br

    def body(x_ref, o_ref):
        o_ref[...] = x_ref[...].astype(dtype)

    if cols_mode:
        out_shape, out_map = (la, r, N_CHIPS * c), lambda l, i: (l, i, _chip())
    else:
        out_shape, out_map = (la, N_CHIPS * r, c), lambda l, i: (l, _chip() * per + i, 0)
    out = pl.pallas_call(
        body, name=name, grid=(la, per),
        in_specs=[pl.BlockSpec((1, br, c), lambda l, i: (l, i, 0))], out_specs=pl.BlockSpec((1, br, c), out_map),
        out_shape=jax.ShapeDtypeStruct(out_shape, dtype), compiler_params=_cparams(("parallel", "parallel")))(sv)
    shape = list(shard.shape)
    shape[axis] *= N_CHIPS
    return out.reshape(tuple(shape))


def _split_cores(gs, name):
    n = len(gs)

    def body(*refs):
        ins, got = refs[:n], refs[n:2 * n]
        send, recv = refs[2 * n:]
        x, y, c, _ = _place()
        sib = (x, y, 1 - c)

        def theirs(i):
            hl = gs[i].shape[0] // 2
            return ins[i].at[pl.ds((1 - c) * hl, hl)]

        for i in range(n):
            for s_, d_ in _pieces(theirs(i), got[i]):
                _remote(s_, d_, send.at[i], recv.at[i], sib).start()
        for i in range(n):
            _remote(theirs(i), got[i], send.at[i], recv.at[i], sib).wait()

    return pl.pallas_call(
        body, name=name, in_specs=[ANY] * n, out_specs=tuple([ANY] * n),
        out_shape=tuple(jax.ShapeDtypeStruct((g.shape[0] // 2,) + g.shape[1:], g.dtype) for g in gs),
        scratch_shapes=[pltpu.SemaphoreType.DMA((n,)), pltpu.SemaphoreType.DMA((n,))],
    )(*gs)


def _scatter_chips(pbs, axes, name):
    n = len(pbs)

    def block(p, ax):
        shape = list(p.shape)
        shape[ax] //= N_CHIPS
        return tuple(shape)

    def body(*refs):
        inb, got = refs[:n], refs[n:2 * n]
        send, recv = refs[2 * n:]
        x, y, c, chips = _place()

        def part(i, chip_index):
            sz = pbs[i].shape[axes[i]] // N_CHIPS
            return _axis_part(inb[i], axes[i], chip_index * sz, sz)

        for i in range(n):
            for k, (px, py) in enumerate(chips):
                _remote(part(i, 2 * px + py), got[i].at[k], send.at[i, k], recv.at[i, k], (px, py, c)).start()
        for i in range(n):
            for k, (px, py) in enumerate(chips):
                _remote(part(i, 2 * px + py), got[i].at[k], send.at[i, k], recv.at[i, k], (px, py, c)).wait()

    return pl.pallas_call(
        body, name=name, in_specs=[ANY] * n, out_specs=tuple([ANY] * n),
        out_shape=tuple(jax.ShapeDtypeStruct((3,) + block(p, ax), p.dtype) for p, ax in zip(pbs, axes)),
        scratch_shapes=[pltpu.SemaphoreType.DMA((n, 3)), pltpu.SemaphoreType.DMA((n, 3))],
    )(*pbs)


def _join_cores(bufs, name):
    n = len(bufs)

    def body(*refs):
        outs = refs[n:2 * n]
        send, recv = refs[2 * n:]
        x, y, c, _ = _place()
        sib = (x, y, 1 - c)

        def half(i, which):
            hl = bufs[i].shape[0] // 2
            return outs[i].at[pl.ds(which * hl, hl)]

        for i in range(n):
            for s_, d_ in _pieces(half(i, c), half(i, c)):
                _remote(s_, d_, send.at[i], recv.at[i], sib).start()
        for i in range(n):
            cp = _remote(half(i, c), half(i, 1 - c), send.at[i], recv.at[i], sib)
            cp.wait_recv()
            cp.wait_send()

    return pl.pallas_call(
        body, name=name, in_specs=[ANY] * n, out_specs=tuple([ANY] * n),
        out_shape=tuple(jax.ShapeDtypeStruct(a.shape, a.dtype) for a in bufs),
        input_output_aliases={i: i for i in range(n)},
        scratch_shapes=[pltpu.SemaphoreType.DMA((n,)), pltpu.SemaphoreType.DMA((n,))],
    )(*bufs)


def _all_reduce_small(buf):
    r = buf.shape[0]

    def body(in_ref, out_ref, sib_buf, chip_sum, got, send, recv):
        x, y, c, chips = _place()
        cp = _remote(in_ref, sib_buf, send.at[0], recv.at[0], (x, y, 1 - c))
        cp.start()
        cp.wait()
        chip_sum[...] = in_ref[...] + sib_buf[...]
        cps = [_remote(chip_sum, got.at[k], send.at[1 + k], recv.at[1 + k], (px, py, c)) for k, (px, py) in enumerate(chips)]
        for cp in cps:
            cp.start()
        for cp in cps:
            cp.wait()
        out_ref[...] = (chip_sum[...] + got[0]) + (got[1] + got[2])

    return pl.pallas_call(
        body, name="all_reduce_small", out_shape=jax.ShapeDtypeStruct(buf.shape, F32),
        in_specs=[pl.BlockSpec(memory_space=pltpu.VMEM)], out_specs=pl.BlockSpec(memory_space=pltpu.VMEM),
        scratch_shapes=[pltpu.VMEM((r, 128), F32), pltpu.VMEM((r, 128), F32), pltpu.VMEM((3, r, 128), F32),
                        pltpu.SemaphoreType.DMA((4,)), pltpu.SemaphoreType.DMA((4,))],
        compiler_params=_cparams(None, VMEM_LIMIT),
    )(buf)


ELEMENTWISE_BLOCK_BYTES = 1 << 20


def _view2d(a):
    cols = a.shape[-1] if a.ndim > 1 else 128
    return a.reshape(-1, cols)


def _row_block(rows, cols):
    want = max(8, ELEMENTWISE_BLOCK_BYTES // (4 * cols))
    br = rows
    while br > want and br % 2 == 0 and (br // 2) % 16 == 0:
        br //= 2
    return br


def _core():
    return lax.axis_index("c")


def _chip():
    return 2 * lax.axis_index("x") + lax.axis_index("y")


def _sum_half(g, got, name):
    h = got.shape[0]
    gv = g.reshape(2 * h, -1, g.shape[-1])
    tv = got.reshape(h, -1, g.shape[-1])
    _, rows, cols = tv.shape
    br = _row_block(rows, cols)

    def body(g_ref, t_ref, p_ref, pb_ref):
        p = g_ref[...] + t_ref[...]
        p_ref[...] = p
        pb_ref[...] = p.astype(BF16)

    blk = pl.BlockSpec((1, br, cols), lambda l, i: (l, i, 0))
    p, pb = pl.pallas_call(
        body, name=name, grid=(h, rows // br),
        in_specs=[pl.BlockSpec((1, br, cols), lambda l, i: (_core() * h + l, i, 0)), blk], out_specs=(blk, blk),
        out_shape=(jax.ShapeDtypeStruct(tv.shape, F32), jax.ShapeDtypeStruct(tv.shape, BF16)),
        compiler_params=_cparams(("parallel", "parallel")))(gv, tv)
    return p.reshape(got.shape), pb.reshape(got.shape)


def _sum_blocks(p, got3, axis, name):
    h = p.shape[0]
    blk_shape = got3.shape[1:]
    cols_mode = axis == p.ndim - 1
    pv = p.reshape(-1, p.shape[-2], p.shape[-1])
    tv = got3.reshape(3, -1, blk_shape[-2], blk_shape[-1])
    la, rb, cb = tv.shape[1:]
    assert cols_mode or axis == p.ndim - 2
    br = _row_block(rb, cb)
    per = rb // br

    def body(p_ref, a_ref, b_ref, c_ref, out_ref):
        out_ref[...] = ((p_ref[...] + a_ref[0].astype(F32)) + b_ref[0].astype(F32)) + c_ref[0].astype(F32)

    if cols_mode:
        p_spec = pl.BlockSpec((1, br, cb), lambda l, i: (l, i, _chip()))
    else:
        p_spec = pl.BlockSpec((1, br, cb), lambda l, i: (l, _chip() * per + i, 0))
    out = pl.pallas_call(
        body, name=name, grid=(la, per),
        in_specs=[p_spec] + [pl.BlockSpec((1, 1, br, cb), lambda l, i, k=k: (k, l, i, 0)) for k in range(3)],
        out_specs=pl.BlockSpec((1, br, cb), lambda l, i: (_core() * la + l, i, 0)),
        out_shape=jax.ShapeDtypeStruct((2 * la, rb, cb), F32),
        compiler_params=_cparams(("parallel", "parallel")))(pv, tv, tv, tv)
    return out.reshape((2 * h,) + blk_shape[1:])


def _adamw(w, g, m, v, name):
    shape = w.shape
    ops = [_view2d(a) for a in (w, g, m, v)]
    rows, cols = ops[0].shape
    br = _row_block(rows, cols)
    c1 = 1.0 - ADAM_B1 ** ADAM_STEP
    c2 = 1.0 - ADAM_B2 ** ADAM_STEP

    def body(w_ref, g_ref, m_ref, v_ref, d_ref, nm_ref, nv_ref):
        g_ = g_ref[...]
        nm = ADAM_B1 * m_ref[...] + (1.0 - ADAM_B1) * g_
        nv = ADAM_B2 * v_ref[...] + (1.0 - ADAM_B2) * (g_ * g_)
        d_ref[...] = -ADAM_LR * ((nm / c1) / (jnp.sqrt(nv / c2) + ADAM_EPS) + ADAM_WD * w_ref[...])
        nm_ref[...] = nm
        nv_ref[...] = nv

    blk = pl.BlockSpec((br, cols), lambda i: (i, 0))
    outs = pl.pallas_call(body, name=name, grid=(rows // br,), in_specs=[blk] * 4, out_specs=(blk, blk, blk),
                          out_shape=tuple(jax.ShapeDtypeStruct((rows, cols), F32) for _ in range(3)),
                          compiler_params=_cparams(("parallel",)))(*ops)
    return tuple(o.reshape(shape) for o in outs)


PACK_ROWS = 2048


def _packed_rows(shape):
    return -(-int(np.prod(shape)) // (8 * 128)) * 8


def _pack(arrays):
    pieces = []
    for a in arrays:
        flat = a.reshape(-1)
        pieces.append(jnp.pad(flat, (0, _packed_rows(a.shape) * 128 - flat.shape[0])).reshape(-1, 128))
    rows = sum(p.shape[0] for p in pieces)
    pieces.append(jnp.zeros((-rows % PACK_ROWS, 128), F32))
    return jnp.concatenate(pieces, axis=0)


def _unpack(buf, shapes):
    out, o = [], 0
    for s in shapes:
        n, rows = int(np.prod(s)), _packed_rows(s)
        out.append(buf[o:o + rows].reshape(-1)[:n].reshape(s))
        o += rows
    return out


WEIGHTS = ["c_ctx", "norm_g", "w_mod", "b_mod", "w_in", "conv_w", "conv_b", "lru_wa", "lru_ba", "lru_wx", "lru_bx",
           "lru_lambda", "attn_sink", "q_norm_g", "k_norm_g", "w_branch", "w_out", "final_g"]
BIG = {"w_mod": 2, "w_in": 2, "w_branch": 2, "w_out": 1}
SMALL_SHARDED = ["conv_w", "lru_ba", "lru_bx", "lru_lambda"]
REPLICATED = [n for n in WEIGHTS if n not in BIG and n not in SMALL_SHARDED]


def kernel(x, c, ctx, c_ctx, norm_g, w_mod, b_mod, w_in, conv_w, conv_b, lru_wa, lru_ba, lru_wx, lru_bx, lru_lambda, attn_sink, q_norm_g, k_norm_g, w_branch, w_out, final_g, loss_target, m_c_ctx, m_norm_g, m_w_mod, m_b_mod, m_w_in, m_conv_w, m_conv_b, m_lru_wa, m_lru_ba, m_lru_wx, m_lru_bx, m_lru_lambda, m_attn_sink, m_q_norm_g, m_k_norm_g, m_w_branch, m_w_out, m_final_g, v_c_ctx, v_norm_g, v_w_mod, v_b_mod, v_w_in, v_conv_w, v_conv_b, v_lru_wa, v_lru_ba, v_lru_wx, v_lru_bx, v_lru_lambda, v_attn_sink, v_q_norm_g, v_k_norm_g, v_w_branch, v_w_out, v_final_g):
    args = dict(locals())
    w = {n: args[n] for n in WEIGHTS}
    mom = {n: args["m_" + n] for n in WEIGHTS}
    var = {n: args["v_" + n] for n in WEIGHTS}
    depth, d = norm_g.shape
    chip = 2 * lax.axis_index("x") + lax.axis_index("y")

    big_names = list(BIG)
    small_shard = jnp.concatenate([w[n] for n in SMALL_SHARDED], axis=1)
    gather_axes = [BIG[n] for n in big_names] + [2]
    placed = [_place_shard(w[n], BIG[n], BF16, "place_" + n) for n in big_names]
    placed.append(_place_shard(small_shard, 2, F32, "place_small"))
    gathered = _gather_chips(placed, gather_axes, "gather_weights")
    whole = dict(w)
    whole.update(dict(zip(big_names, gathered[:-1])))
    o = 0
    for n in SMALL_SHARDED:
        rows = w[n].shape[1]
        whole[n] = gathered[-1][:, o:o + rows]
        o += rows
    whole["w_in_r"] = _reorder_in_cols(whole["w_in"], d)
    layers = [_layer_params(li, whole) for li in range(depth)]

    loss_local, grad_x, g_c_ctx, g_final, lgrads = _local_step(x, c, ctx, loss_target, c_ctx, final_g, layers)
    loss = lax.psum(loss_local, ("x", "y", "c"))
    full = {n: jnp.stack([lg[n] for lg in lgrads]) for n in lgrads[0]}
    full["w_in"] = _reorder_in_cols(full["w_in"], d, inverse=True)
    full["c_ctx"], full["final_g"] = g_c_ctx, g_final

    bigs = [full[n] for n in big_names]
    got = _split_cores(bigs, "grad_split_cores")
    parts = [_sum_half(g, t_, "grad_chip_sum") for g, t_ in zip(bigs, got)]
    recv = _scatter_chips([pb for _, pb in parts], [BIG[n] for n in big_names], "grad_scatter_chips")
    totals = [_sum_blocks(p_, r, BIG[n], "grad_total") for (p_, _), r, n in zip(parts, recv, big_names)]
    grad = dict(zip(big_names, _join_cores(totals, "grad_join_cores")))

    small_names = REPLICATED + SMALL_SHARDED
    reduced = _unpack(_all_reduce_small(_pack([full[n] for n in small_names])), [full[n].shape for n in small_names])
    for n, g in zip(small_names, reduced):
        if n in SMALL_SHARDED:
            sz = w[n].shape[-1]
            g = lax.dynamic_slice_in_dim(g, chip * sz, sz, axis=g.ndim - 1)
        grad[n] = g

    delta, new_m, new_v = {}, {}, {}
    for n in big_names:
        delta[n], new_m[n], new_v[n] = _adamw(w[n], grad[n], mom[n], var[n], "adamw_" + n)
    shapes = [w[n].shape for n in small_names]
    packed = _adamw(_pack([w[n] for n in small_names]), _pack([grad[n] for n in small_names]),
                    _pack([mom[n] for n in small_names]), _pack([var[n] for n in small_names]), "adamw_small")
    for res, p in zip((delta, new_m, new_v), packed):
        res.update(dict(zip(small_names, _unpack(p, shapes))))

    return (loss, grad_x, *[grad[n] for n in WEIGHTS], *[delta[n] for n in WEIGHTS],
            *[new_m[n] for n in WEIGHTS], *[new_v[n] for n in WEIGHTS])
```

```python
import functools

import jax
import jax.numpy as jnp
import numpy as np
from jax import lax
from jax.experimental import pallas as pl
from jax.experimental.pallas import tpu as pltpu

F32 = jnp.float32
BF16 = jnp.bfloat16

HEAD_DIM = 128
GROUP = 4
LRU_BLOCK_W = 64
LRU_C = 8.0
WINDOW = 128
GRID_W = 64
ROPE_THETA = 10000.0
EPS = 1e-6
NEG_INF = -1e30
ADAM_LR, ADAM_B1, ADAM_B2, ADAM_EPS, ADAM_WD, ADAM_STEP = 0.001, 0.9, 0.999, 1e-08, 0.01, 10

ROW_BLOCK = 256
LRU_LANES = 128
LRU_CHUNK = 128
LRU_UNROLL = 2
DENSE_FWD_BQ = 256
LOG2E = 1.4426950408889634
ATT_BQ = 256
WIN_BQ = 256
WIN_SPAN = WIN_BQ + 2 * WINDOW
MOD_ROWS = 16
VMEM_LIMIT = 56 * 1024 * 1024

MESH = pl.DeviceIdType.MESH


def _cparams(sem=None, vmem=None):
    kw = {}
    if sem is not None:
        kw["dimension_semantics"] = sem
    if vmem is not None:
        kw["vmem_limit_bytes"] = vmem
    return pltpu.CompilerParams(**kw)


def _sigmoid(v):
    return 1.0 / (1.0 + jnp.exp(-v))


def _silu(v):
    return v * _sigmoid(v)


def _dsilu(v):
    s = _sigmoid(v)
    return s * (1.0 + v * (1.0 - s))


def _one_minus_square(log_a, a):
    z2 = log_a * log_a
    series = (-2.0 * a * log_a) * (1.0 + z2 * (1.0 / 6 + z2 * (1.0 / 120 + z2 * (1.0 / 5040))))
    return jnp.where(z2 < 0.25, series, 1.0 - a * a)


def _log1p(y):
    u = 1.0 + y
    d = u - 1.0
    return jnp.where(d == 0.0, y, jnp.log(u) * (y / jnp.where(d == 0.0, 1.0, d)))


def _softplus(x):
    return jnp.maximum(x, 0.0) + _log1p(jnp.exp(-jnp.abs(x)))


def _dot(a, b):
    return jnp.dot(a, b, preferred_element_type=F32)


def _dot_nt(a, b):
    return lax.dot_general(a, b, (((1,), (1,)), ((), ())), preferred_element_type=F32)


def _dot_tn(a, b):
    return lax.dot_general(a, b, (((0,), (0,)), ((), ())), preferred_element_type=F32)


def _colsum(v):
    return jnp.sum(v, axis=0, keepdims=True)


def _layout(d_model):
    kvw = (d_model // HEAD_DIM // GROUP) * HEAD_DIM
    names = ["gA", "gB", "gC", "mA", "mB", "mC", "uA", "qB", "qC", "kB", "vB", "kC", "vC"]
    widths = [d_model] * 9 + [kvw] * 4
    off, o = {}, 0
    for n, w in zip(names, widths):
        off[n] = o
        o += w
    return off, o


def _orig_segments(d_model):
    kvw = (d_model // HEAD_DIM // GROUP) * HEAD_DIM
    names = ["uA", "gA", "qB", "kB", "vB", "gB", "qC", "kC", "vC", "gC", "mA", "mB", "mC"]
    widths = [d_model, d_model, d_model, kvw, kvw, d_model, d_model, kvw, kvw, d_model, d_model, d_model, d_model]
    out, o = [], 0
    for n, w in zip(names, widths):
        out.append((n, o, w))
        o += w
    return out


def _matmul(a, b, *, ta=False, tb=False, out_dtype=F32, bm, bn, bk, name, n_outer=False, a_lead=None, b_lead=None):
    a_shape = a.shape if a_lead is None else a.shape[1:]
    b_shape = b.shape if b_lead is None else b.shape[1:]
    (kdim, m) = a_shape if ta else a_shape[::-1]
    (n, kdim2) = b_shape if tb else b_shape[::-1]
    assert kdim == kdim2 and m % bm == 0 and n % bn == 0 and kdim % bk == 0, (a.shape, b.shape, bm, bn, bk)
    nk = kdim // bk
    dims = (((0 if ta else 1,), (1 if tb else 0,)), ((), ()))

    def ij(f):
        return (lambda j, i, k: f(i, j, k)) if n_outer else f

    def body(a_ref, b_ref, o_ref, *scratch):
        r = lax.dot_general(a_ref[...].astype(BF16), b_ref[...].astype(BF16), dims, preferred_element_type=F32)
        if nk == 1:
            o_ref[...] = r.astype(out_dtype)
        else:
            acc = scratch[0]
            k = pl.program_id(2)

            @pl.when(k == 0)
            def _():
                acc[...] = r

            @pl.when(k > 0)
            def _():
                acc[...] += r

            @pl.when(k == nk - 1)
            def _():
                o_ref[...] = acc[...].astype(out_dtype)

    def spec(shape, f, lead):
        f = ij(f)
        if lead is None:
            return pl.BlockSpec(shape, f)
        return pl.BlockSpec((None,) + shape, lambda *g: (lead,) + f(*g))

    a_spec = spec((bk, bm), lambda i, j, k: (k, i), a_lead) if ta else spec((bm, bk), lambda i, j, k: (i, k), a_lead)
    b_spec = spec((bn, bk), lambda i, j, k: (j, k), b_lead) if tb else spec((bk, bn), lambda i, j, k: (k, j), b_lead)
    return pl.pallas_call(
        body, name=name, grid=(n // bn, m // bm, nk) if n_outer else (m // bm, n // bn, nk),
        in_specs=[a_spec, b_spec], out_specs=pl.BlockSpec((bm, bn), ij(lambda i, j, k: (i, j))),
        out_shape=jax.ShapeDtypeStruct((m, n), out_dtype),
        scratch_shapes=[pltpu.VMEM((bm, bn), F32)] if nk > 1 else [],
        compiler_params=_cparams(("parallel", "parallel", "arbitrary"), VMEM_LIMIT),
    )(a, b)


def _mod_fwd(c16, w_mod, b_mod):
    d3 = w_mod.shape[1]

    def body(c_ref, w_ref, b_ref, o_ref):
        o_ref[...] = _dot(_silu(c_ref[...]).astype(BF16), w_ref[...]) + b_ref[...]

    return pl.pallas_call(body, name="mod_fwd", out_shape=jax.ShapeDtypeStruct((MOD_ROWS, d3), F32),
                          compiler_params=_cparams(None, VMEM_LIMIT))(c16, w_mod, b_mod)


def _mod_bwd(c16, dmod16, w_mod):
    d, d3 = w_mod.shape

    def body(c_ref, g_ref, w_ref, dw_ref, db_ref, dc_ref):
        c = c_ref[...]
        g = g_ref[...]
        gb = g.astype(BF16)
        dw_ref[...] = _dot_tn(_silu(c).astype(BF16), gb)
        db_ref[...] = _colsum(g)
        dc_ref[...] = _dot_nt(gb, w_ref[...]) * _dsilu(c)

    return pl.pallas_call(
        body, name="mod_bwd",
        out_shape=(jax.ShapeDtypeStruct((d, d3), F32), jax.ShapeDtypeStruct((1, d3), F32),
                   jax.ShapeDtypeStruct((MOD_ROWS, d), F32)),
        compiler_params=_cparams(None, VMEM_LIMIT))(c16, dmod16, w_mod)


def _row_kind(t, lb):
    return jnp.where(t >= lb, 1, 0)


def _norm_mod_fwd(x3, g, modsel, ctx_len):
    b, t, d = x3.shape
    bt = ROW_BLOCK
    lb = ctx_len // bt

    def body(x_ref, g_ref, m_ref, h_ref):
        x = x_ref[0]
        rstd = lax.rsqrt(jnp.mean(x * x, axis=-1, keepdims=True) + EPS)
        y = x * rstd * g_ref[...]
        h_ref[0] = (y * (1.0 + m_ref[0, 0, 1:2, :]) + m_ref[0, 0, 0:1, :]).astype(BF16)

    return pl.pallas_call(
        body, name="norm_mod_fwd", grid=(b, t // bt),
        in_specs=[pl.BlockSpec((1, bt, d), lambda i, j: (i, j, 0)),
                  pl.BlockSpec((1, d), lambda i, j: (0, 0)),
                  pl.BlockSpec((1, 1, 8, d), lambda i, j: (i, _row_kind(j, lb), 0, 0))],
        out_specs=pl.BlockSpec((1, bt, d), lambda i, j: (i, j, 0)),
        out_shape=jax.ShapeDtypeStruct((b, t, d), BF16),
        compiler_params=_cparams(("parallel", "arbitrary")),
    )(x3, g, modsel)


def _norm_mod_bwd(dh3, x3, g, modsel, dres3, ctx_len):
    b, t, d = x3.shape
    bt = ROW_BLOCK
    lb = ctx_len // bt

    def body(dh_ref, x_ref, g_ref, m_ref, dres_ref, dx_ref, acc_ref):
        j = pl.program_id(1)
        x = x_ref[0]
        dh = dh_ref[0]
        g_row = g_ref[...]
        rstd = lax.rsqrt(jnp.mean(x * x, axis=-1, keepdims=True) + EPS)
        xhat = x * rstd
        dhpre = dh * (1.0 + m_ref[0, 0, 1:2, :])
        dxhat = dhpre * g_row
        dx = rstd * (dxhat - xhat * jnp.mean(dxhat * xhat, axis=-1, keepdims=True))
        dx_ref[0] = dx + dres_ref[0]

        @pl.when((j == 0) | (j == lb))
        def _():
            acc_ref[...] = jnp.zeros_like(acc_ref)

        acc_ref[0, 0, 0:1, :] += _colsum(dh)
        acc_ref[0, 0, 1:2, :] += _colsum(dh * (xhat * g_row))
        acc_ref[0, 0, 2:3, :] += _colsum(dhpre * xhat)

    blk = pl.BlockSpec((1, bt, d), lambda i, j: (i, j, 0))
    return pl.pallas_call(
        body, name="norm_mod_bwd", grid=(b, t // bt),
        in_specs=[blk, blk, pl.BlockSpec((1, d), lambda i, j: (0, 0)),
                  pl.BlockSpec((1, 1, 8, d), lambda i, j: (i, _row_kind(j, lb), 0, 0)), blk],
        out_specs=(blk, pl.BlockSpec((1, 1, 8, d), lambda i, j: (i, _row_kind(j, lb), 0, 0))),
        out_shape=(jax.ShapeDtypeStruct((b, t, d), F32), jax.ShapeDtypeStruct((b, 2, 8, d), F32)),
        compiler_params=_cparams(("parallel", "arbitrary")),
    )(dh3, x3, g, modsel, dres3)


def _shifted_rows(ref, c, off, ctx_len, total):
    ct = LRU_CHUNK
    r0 = pl.multiple_of(c * ct, ct)
    x0 = ref[pl.ds(r0, ct), :]
    row = lax.broadcasted_iota(jnp.int32, x0.shape, 0)
    if off < 0:
        k = -off
        has = jnp.logical_and(r0 != 0, r0 != ctx_len)
        p0 = pl.multiple_of(jnp.maximum(r0 - 8, 0), 8)
        edge = jnp.where(has, ref[pl.ds(p0, 8), :], 0.0)
        out = pltpu.roll(x0, k, 0)
        for j in range(k):
            out = jnp.where(row == j, edge[8 - k + j:8 - k + j + 1, :], out)
    else:
        k = off
        has = jnp.logical_and(r0 + ct != ctx_len, r0 + ct != total)
        n0 = pl.multiple_of(jnp.minimum(r0 + ct, total - 8), 8)
        edge = jnp.where(has, ref[pl.ds(n0, 8), :], 0.0)
        out = pltpu.roll(x0, ct - k, 0)
        for j in range(k):
            out = jnp.where(row == ct - k + j, edge[j:j + 1, :], out)
    return out


def _chunk_scan(a, b, reverse):
    n = a.shape[0]
    row = lax.broadcasted_iota(jnp.int32, a.shape, 0)
    s = 1
    while s < n:
        if reverse:
            a_s, b_s, ok = pltpu.roll(a, n - s, 0), pltpu.roll(b, n - s, 0), row < n - s
        else:
            a_s, b_s, ok = pltpu.roll(a, s, 0), pltpu.roll(b, s, 0), row >= s
        b = jnp.where(ok, a * b_s + b, b)
        a = jnp.where(ok, a * a_s, a)
        s *= 2
    return a, b


def _loop_chunks(n, body, init):
    assert n % LRU_UNROLL == 0

    def group(s2, carry):
        for u in range(LRU_UNROLL):
            carry = body(LRU_UNROLL * s2 + u, carry)
        return carry

    return lax.fori_loop(0, n // LRU_UNROLL, group, init)


def _lru_order(d, s, n_ctx, n_all):
    if d == 0:
        return s
    return jnp.where(s < n_ctx, n_ctx - 1 - s, n_all - 1 - (s - n_ctx))


def _lru_gates(u, wa, ba, wx, bx, sp):
    ub = u.astype(BF16)
    r = _sigmoid(_dot(ub, wa) + ba)
    i = _sigmoid(_dot(ub, wx) + bx)
    log_a = (-LRU_C * sp) * r
    a = jnp.exp(log_a)
    sf = jnp.sqrt(_one_minus_square(log_a, a))
    return ub, r, i, a, sf


def _lru_specs(t, n_lane_blocks_offset):
    ln = LRU_LANES
    return [
        pl.BlockSpec((4, ln), lambda i, j: (0, j)),
        pl.BlockSpec((1, ln), lambda i, j: (0, j)),
        pl.BlockSpec((2, 1, ln, ln), lambda i, j: (0, j, 0, 0)),
        pl.BlockSpec((2, ln), lambda i, j: (0, j)),
        pl.BlockSpec((2, 1, ln, ln), lambda i, j: (0, j, 0, 0)),
        pl.BlockSpec((2, ln), lambda i, j: (0, j)),
        pl.BlockSpec((2, ln), lambda i, j: (0, j)),
    ]


def _widen(src_ref, dst, n_chunks):
    ct = LRU_CHUNK

    def copy(c, _):
        r0 = pl.multiple_of(c * ct, ct)
        dst[pl.ds(r0, ct), :] = src_ref[0, pl.ds(r0, ct), :].astype(F32)
        return 0

    lax.fori_loop(0, n_chunks, copy, 0)


def _lru_conv(ua_ref, cw_ref, cb_ref, u_s, ctx_len, total):
    ct = LRU_CHUNK

    def conv(c, _):
        r0 = pl.multiple_of(c * ct, ct)
        u = (cw_ref[0:1, :] * _shifted_rows(ua_ref, c, -2, ctx_len, total)
             + cw_ref[1:2, :] * _shifted_rows(ua_ref, c, -1, ctx_len, total)
             + cw_ref[2:3, :] * ua_ref[pl.ds(r0, ct), :]
             + cw_ref[3:4, :] * _shifted_rows(ua_ref, c, 1, ctx_len, total) + cb_ref[...])
        u_s[pl.ds(r0, ct), :] = u
        return 0

    lax.fori_loop(0, total // ct, conv, 0)


def _lru_fwd(proj3, col0, conv_w, conv_b, wa_bd, ba, wx_bd, bx, lam, ctx_len):
    b, t, _ = proj3.shape
    d = conv_w.shape[1]
    ln, ct = LRU_LANES, LRU_CHUNK
    n_all, n_ctx = t // ct, ctx_len // ct
    cb0 = col0 // ln

    def body(ua_ref, cw_ref, cb_ref, wa_ref, ba_ref, wx_ref, bx_ref, lam_ref, y_ref, u_s, h1_s, ua):
        _widen(ua_ref, ua, n_all)
        _lru_conv(ua, cw_ref, cb_ref, u_s, ctx_len, t)
        par = [(_softplus(-lam_ref[dr:dr + 1, :]), wa_ref[dr, 0], wx_ref[dr, 0], ba_ref[dr:dr + 1, :], bx_ref[dr:dr + 1, :])
               for dr in (0, 1)]

        def step(s, carry):
            out = []
            for dr in (0, 1):
                sp, wa, wx, ba_row, bx_row = par[dr]
                c = _lru_order(dr, s, n_ctx, n_all)
                r0 = pl.multiple_of(c * ct, ct)
                u = u_s[pl.ds(r0, ct), :]
                _, _, i, a, sf = _lru_gates(u, wa, ba_row, wx, bx_row, sp)
                aa, h0 = _chunk_scan(a, sf * (i * u), reverse=(dr == 1))
                h = h0 + aa * carry[dr]
                if dr == 0:
                    y_ref[0, pl.ds(r0, ct), :] = h
                    out.append(h[ct - 1:ct, :])
                else:
                    h1_s[pl.ds(r0, ct), :] = h
                    out.append(h[0:1, :])
            return tuple(out)

        zrow = jnp.zeros((1, ln), F32)
        _loop_chunks(n_all, step, (zrow, zrow))

        def add(c, _):
            r0 = pl.multiple_of(c * ct, ct)
            y_ref[0, pl.ds(r0, ct), :] += h1_s[pl.ds(r0, ct), :]
            return 0

        lax.fori_loop(0, n_all, add, 0)

    return pl.pallas_call(
        body, name="lru_fwd", grid=(b, d // ln),
        in_specs=[pl.BlockSpec((1, t, ln), lambda i, j: (i, 0, cb0 + j))] + _lru_specs(t, cb0),
        out_specs=pl.BlockSpec((1, t, ln), lambda i, j: (i, 0, j)),
        out_shape=jax.ShapeDtypeStruct((b, t, d), F32),
        scratch_shapes=[pltpu.VMEM((t, ln), F32)] * 3,
        compiler_params=_cparams(("parallel", "parallel"), VMEM_LIMIT),
    )(proj3, conv_w, conv_b, wa_bd, ba, wx_bd, bx, lam)


def _lru_bwd(dproj3, proj3, col0, dy3, conv_w, conv_b, wa_bd, ba, wx_bd, bx, lam, ctx_len):
    b, t, _ = proj3.shape
    d = conv_w.shape[1]
    ln, ct = LRU_LANES, LRU_CHUNK
    n_all, n_ctx = t // ct, ctx_len // ct
    cb0 = col0 // ln

    def body(dproj_hbm, ua_ref, dy_ref, cw_ref, cb_ref, wa_ref, ba_ref, wx_ref, bx_ref, lam_ref,
             dua_ref, vec_ref, dwa_ref, dwx_ref, u_s, du_s, ua, h_s, a_s, sf_s, i_s, r_s):
        del dproj_hbm
        _widen(ua_ref, ua, n_all)
        _lru_conv(ua, cw_ref, cb_ref, u_s, ctx_len, t)
        du_s[...] = jnp.zeros_like(du_s)
        vec_ref[...] = jnp.zeros_like(vec_ref)
        par = [(_softplus(-lam_ref[dr:dr + 1, :]), wa_ref[dr, 0], wx_ref[dr, 0], ba_ref[dr:dr + 1, :], bx_ref[dr:dr + 1, :])
               for dr in (0, 1)]

        def fwd_one(dr, s, carry):
            sp, wa, wx, ba_row, bx_row = par[dr]
            c = _lru_order(dr, s, n_ctx, n_all)
            r0 = pl.multiple_of(c * ct, ct)
            u = u_s[pl.ds(r0, ct), :]
            _, r, i, a, sf = _lru_gates(u, wa, ba_row, wx, bx_row, sp)
            aa, h0 = _chunk_scan(a, sf * (i * u), reverse=(dr == 1))
            h = h0 + aa * carry
            h_s[dr, pl.ds(r0, ct), :] = h
            a_s[dr, pl.ds(r0, ct), :] = a
            sf_s[dr, pl.ds(r0, ct), :] = sf
            i_s[dr, pl.ds(r0, ct), :] = i
            r_s[dr, pl.ds(r0, ct), :] = r
            return h[ct - 1:ct, :] if dr == 0 else h[0:1, :]

        def bwd_one(dr, sr, carry):
            sp, wa, wx, _, _ = par[dr]
            gc, vacc = carry
            hs = h_s.at[dr]
            c = _lru_order(dr, n_all - 1 - sr, n_ctx, n_all)
            r0 = pl.multiple_of(c * ct, ct)
            u = u_s[pl.ds(r0, ct), :]
            h = hs[pl.ds(r0, ct), :]
            dy = dy_ref[0, pl.ds(r0, ct), :]
            ub = u.astype(BF16)
            r, i = r_s[dr, pl.ds(r0, ct), :], i_s[dr, pl.ds(r0, ct), :]
            a, sf = a_s[dr, pl.ds(r0, ct), :], sf_s[dr, pl.ds(r0, ct), :]
            row = lax.broadcasted_iota(jnp.int32, a.shape, 0)
            if dr == 0:
                alpha = jnp.where(row == ct - 1, 1.0, pltpu.roll(a, ct - 1, 0))
                aa, g0 = _chunk_scan(alpha, dy, reverse=True)
                g = g0 + aa * gc
                gc_new = a[0:1, :] * g[0:1, :]
                p0 = pl.multiple_of(jnp.maximum(r0 - 8, 0), 8)
                edge = jnp.where(r0 != 0, hs[pl.ds(p0, 8), :], 0.0)[7:8, :]
                h_prev = jnp.where(row == 0, edge, pltpu.roll(h, 1, 0))
            else:
                alpha = jnp.where(row == 0, 1.0, pltpu.roll(a, 1, 0))
                aa, g0 = _chunk_scan(alpha, dy, reverse=False)
                g = g0 + aa * gc
                gc_new = a[ct - 1:ct, :] * g[ct - 1:ct, :]
                r_end = r0 + ct
                n0 = pl.multiple_of(jnp.where(r_end == t, 0, jnp.minimum(r_end, t - 8)), 8)
                edge = jnp.where(r_end != ctx_len, hs[pl.ds(n0, 8), :], 0.0)[0:1, :]
                h_prev = jnp.where(row == ct - 1, edge, pltpu.roll(h, ct - 1, 0))
            da = g * h_prev
            iu = i * u
            diu = g * sf
            dlog_a = da * a - (g * iu) * (a * a) / sf
            dpre_r = (dlog_a * (-LRU_C * sp)) * (r * (1.0 - r))
            dpre_i = (diu * u) * (i * (1.0 - i))
            dpr_b, dpi_b = dpre_r.astype(BF16), dpre_i.astype(BF16)
            du = diu * i + _dot_nt(dpr_b, wa) + _dot_nt(dpi_b, wx)
            du_s[pl.ds(r0, ct), :] += du
            dwa_ref[0, dr, 0] += _dot_tn(ub, dpr_b)
            dwx_ref[0, dr, 0] += _dot_tn(ub, dpi_b)
            vacc = (vacc[0] + _colsum(dpre_r), vacc[1] + _colsum(dpre_i), vacc[2] + _colsum(dlog_a * (-LRU_C * r)))
            return gc_new, vacc

        zrow = jnp.zeros((1, ln), F32)
        _loop_chunks(n_all, lambda s, cr: tuple(fwd_one(dr, s, cr[dr]) for dr in (0, 1)), (zrow, zrow))
        dwa_ref[...] = jnp.zeros_like(dwa_ref)
        dwx_ref[...] = jnp.zeros_like(dwx_ref)
        init = (zrow, (zrow, zrow, zrow))
        res = _loop_chunks(n_all, lambda sr, cr: tuple(bwd_one(dr, sr, cr[dr]) for dr in (0, 1)), (init, init))
        for dr in (0, 1):
            vacc = res[dr][1]
            vec_ref[0, 5 + dr:6 + dr, :] = vacc[0]
            vec_ref[0, 7 + dr:8 + dr, :] = vacc[1]
            vec_ref[0, 9 + dr:10 + dr, :] = vacc[2]

        def conv_bwd(c, acc):
            r0 = pl.multiple_of(c * ct, ct)
            du = du_s[pl.ds(r0, ct), :]
            dua = (cw_ref[0:1, :] * _shifted_rows(du_s, c, 2, ctx_len, t)
                   + cw_ref[1:2, :] * _shifted_rows(du_s, c, 1, ctx_len, t)
                   + cw_ref[2:3, :] * du
                   + cw_ref[3:4, :] * _shifted_rows(du_s, c, -1, ctx_len, t))
            dua_ref[0, pl.ds(r0, ct), :] = dua.astype(BF16)
            return (acc[0] + _colsum(du * _shifted_rows(ua, c, -2, ctx_len, t)),
                    acc[1] + _colsum(du * _shifted_rows(ua, c, -1, ctx_len, t)),
                    acc[2] + _colsum(du * ua[pl.ds(r0, ct), :]),
                    acc[3] + _colsum(du * _shifted_rows(ua, c, 1, ctx_len, t)),
                    acc[4] + _colsum(du))

        zrow = jnp.zeros((1, ln), F32)
        acc = lax.fori_loop(0, n_all, conv_bwd, (zrow,) * 5)
        for k in range(5):
            vec_ref[0, k:k + 1, :] = acc[k]

    ng = d // ln
    return pl.pallas_call(
        body, name="lru_bwd", grid=(b, ng),
        in_specs=[pl.BlockSpec(memory_space=pl.ANY),
                  pl.BlockSpec((1, t, ln), lambda i, j: (i, 0, cb0 + j)),
                  pl.BlockSpec((1, t, ln), lambda i, j: (i, 0, j))] + _lru_specs(t, cb0),
        out_specs=(pl.BlockSpec((1, t, ln), lambda i, j: (i, 0, cb0 + j)),
                   pl.BlockSpec((1, 16, ln), lambda i, j: (i, 0, j)),
                   pl.BlockSpec((1, 2, 1, ln, ln), lambda i, j: (i, 0, j, 0, 0)),
                   pl.BlockSpec((1, 2, 1, ln, ln), lambda i, j: (i, 0, j, 0, 0))),
        out_shape=(jax.ShapeDtypeStruct(dproj3.shape, dproj3.dtype),
                   jax.ShapeDtypeStruct((b, 16, d), F32),
                   jax.ShapeDtypeStruct((b, 2, ng, ln, ln), F32),
                   jax.ShapeDtypeStruct((b, 2, ng, ln, ln), F32)),
        scratch_shapes=[pltpu.VMEM((t, ln), F32)] * 3 + [pltpu.VMEM((2, t, ln), F32)] * 5,
        input_output_aliases={0: 0},
        compiler_params=_cparams(("parallel", "parallel"), VMEM_LIMIT),
    )(dproj3, proj3, dy3, conv_w, conv_b, wa_bd, ba, wx_bd, bx, lam)


def _rope_tables(ctx_len, seq):
    p = HEAD_DIM // 4
    inv = ROPE_THETA ** (-jnp.arange(p, dtype=F32) / p)
    tok = jnp.arange(seq)
    ang_r = (tok // GRID_W)[:, None] * inv
    ang_c = (tok % GRID_W)[:, None] * inv
    cos = jnp.concatenate([jnp.cos(ang_r)] * 2 + [jnp.cos(ang_c)] * 2, axis=1)
    sin = jnp.concatenate([-jnp.sin(ang_r), jnp.sin(ang_r), -jnp.sin(ang_c), jnp.sin(ang_c)], axis=1)
    cos = jnp.concatenate([jnp.ones((ctx_len, HEAD_DIM), F32), cos], axis=0)
    sin = jnp.concatenate([jnp.zeros((ctx_len, HEAD_DIM), F32), sin], axis=0)
    return cos, sin


def _swap_halves(v):
    lane = lax.broadcasted_iota(jnp.int32, v.shape, 1)
    return jnp.where((lane & 63) < 32, pltpu.roll(v, 96, 1), pltpu.roll(v, 32, 1))


def _head_rstd(v):
    return lax.rsqrt(jnp.mean(v * v, axis=-1, keepdims=True) + EPS)


QKV_BLOCK = GROUP * HEAD_DIM
PREP_ROWS = (2176, 256)


def _prep_fwd(proj3, qcol, kvcol, d, cos, sin, gq, gk, use_norm):
    b, t, _ = proj3.shape
    bt, wb = _pick(t, PREP_ROWS), QKV_BLOCK
    nqb = d // wb
    assert qcol % wb == 0 and kvcol % wb == 0 and d // HEAD_DIM // GROUP == 2
    qb0, kvb = qcol // wb, kvcol // wb

    def body(p_ref, cos_ref, sin_ref, gq_ref, gk_ref, o_ref):
        s = pl.program_id(2)
        c, sn = cos_ref[...], sin_ref[...]

        def rope(v):
            return v * c + _swap_halves(v) * sn

        @pl.when(s < nqb)
        def _():
            for hh in range(GROUP):
                v = p_ref[0, :, hh * HEAD_DIM:(hh + 1) * HEAD_DIM].astype(F32)
                if use_norm:
                    v = v * _head_rstd(v) * gq_ref[...]
                o_ref[0, :, hh * HEAD_DIM:(hh + 1) * HEAD_DIM] = rope(v).astype(BF16)

        @pl.when(s == nqb)
        def _():
            for hh in range(2):
                v = p_ref[0, :, hh * HEAD_DIM:(hh + 1) * HEAD_DIM].astype(F32)
                if use_norm:
                    v = v * _head_rstd(v) * gk_ref[...]
                o_ref[0, :, hh * HEAD_DIM:(hh + 1) * HEAD_DIM] = rope(v).astype(BF16)
            o_ref[0, :, 2 * HEAD_DIM:] = p_ref[0, :, 2 * HEAD_DIM:]

    return pl.pallas_call(
        body, name="prep_fwd_norm" if use_norm else "prep_fwd", grid=(b, t // bt, nqb + 1),
        in_specs=[pl.BlockSpec((1, bt, wb), lambda i, j, s: (i, j, jnp.where(s < nqb, qb0 + s, kvb))),
                  pl.BlockSpec((bt, HEAD_DIM), lambda i, j, s: (j, 0)),
                  pl.BlockSpec((bt, HEAD_DIM), lambda i, j, s: (j, 0)),
                  pl.BlockSpec((1, HEAD_DIM), lambda i, j, s: (0, 0)),
                  pl.BlockSpec((1, HEAD_DIM), lambda i, j, s: (0, 0))],
        out_specs=pl.BlockSpec((1, bt, wb), lambda i, j, s: (i, j, s)),
        out_shape=jax.ShapeDtypeStruct((b, t, d + wb), BF16),
        compiler_params=_cparams(("parallel", "parallel", "arbitrary"), VMEM_LIMIT),
    )(proj3, cos, sin, gq, gk)


def _prep_bwd(dproj3, dq3, dkt, dvt, proj3, qcol, kvcol, d, cos, sin, gq, gk, use_norm):
    b, t, _ = proj3.shape
    bt, wb = _pick(t, PREP_ROWS), QKV_BLOCK
    nqb = d // wb
    qb0, kvb = qcol // wb, kvcol // wb
    kvh = dkt.shape[1]

    def body(dproj_hbm, dq_ref, dkt_ref, dvt_ref, p_ref, cos_ref, sin_ref, gq_ref, gk_ref, o_ref, gacc_ref):
        del dproj_hbm
        j, s = pl.program_id(1), pl.program_id(2)
        c, sn = cos_ref[...], sin_ref[...]

        @pl.when((j == 0) & (s == 0))
        def _():
            gacc_ref[...] = jnp.zeros_like(gacc_ref)

        def unrope(dv):
            return dv * c + _swap_halves(dv * sn)

        def head_bwd(dyv, xv, g_ref, acc_row):
            dyv = unrope(dyv)
            if not use_norm:
                return dyv
            rstd = _head_rstd(xv)
            xhat = xv * rstd
            gacc_ref[0, acc_row:acc_row + 1, :] += _colsum(dyv * xhat)
            dxhat = dyv * g_ref[...]
            return rstd * (dxhat - xhat * jnp.mean(dxhat * xhat, axis=-1, keepdims=True))

        @pl.when(s < nqb)
        def _():
            for hh in range(GROUP):
                sl = slice(hh * HEAD_DIM, (hh + 1) * HEAD_DIM)
                o_ref[0, :, sl] = head_bwd(dq_ref[0, :, sl], p_ref[0, :, sl].astype(F32), gq_ref, 0).astype(BF16)

        @pl.when(s == nqb)
        def _():
            for hh in range(kvh):
                sl = slice(hh * HEAD_DIM, (hh + 1) * HEAD_DIM)
                o_ref[0, :, sl] = head_bwd(dkt_ref[0, hh].T, p_ref[0, :, sl].astype(F32), gk_ref, 1).astype(BF16)
                sv = slice((kvh + hh) * HEAD_DIM, (kvh + hh + 1) * HEAD_DIM)
                o_ref[0, :, sv] = dvt_ref[0, hh].T.astype(BF16)

    col = lambda i, j, s: (i, j, jnp.where(s < nqb, qb0 + s, kvb))
    return pl.pallas_call(
        body, name="prep_bwd_norm" if use_norm else "prep_bwd", grid=(b, t // bt, nqb + 1),
        in_specs=[pl.BlockSpec(memory_space=pl.ANY),
                  pl.BlockSpec((1, bt, wb), lambda i, j, s: (i, j, jnp.minimum(s, nqb - 1))),
                  pl.BlockSpec((1, kvh, HEAD_DIM, bt), lambda i, j, s: (i, 0, 0, j)),
                  pl.BlockSpec((1, kvh, HEAD_DIM, bt), lambda i, j, s: (i, 0, 0, j)),
                  pl.BlockSpec((1, bt, wb), col),
                  pl.BlockSpec((bt, HEAD_DIM), lambda i, j, s: (j, 0)),
                  pl.BlockSpec((bt, HEAD_DIM), lambda i, j, s: (j, 0)),
                  pl.BlockSpec((1, HEAD_DIM), lambda i, j, s: (0, 0)),
                  pl.BlockSpec((1, HEAD_DIM), lambda i, j, s: (0, 0))],
        out_specs=(pl.BlockSpec((1, bt, wb), col), pl.BlockSpec((1, 8, HEAD_DIM), lambda i, j, s: (i, 0, 0))),
        out_shape=(jax.ShapeDtypeStruct(dproj3.shape, dproj3.dtype), jax.ShapeDtypeStruct((b, 8, HEAD_DIM), F32)),
        input_output_aliases={0: 0},
        compiler_params=_cparams(("parallel", "arbitrary", "arbitrary"), VMEM_LIMIT),
    )(dproj3, dq3, dkt, dvt, proj3, cos, sin, gq, gk)


def _stack_heads(ref, dtype=None):
    parts = [ref[0, :, g * HEAD_DIM:(g + 1) * HEAD_DIM] for g in range(GROUP)]
    v = jnp.concatenate(parts, axis=0)
    return v if dtype is None else v.astype(dtype)


def _unstack_heads(ref, v, bq):
    for g in range(GROUP):
        ref[0, :, g * HEAD_DIM:(g + 1) * HEAD_DIM] = v[g * bq:(g + 1) * bq, :]


def _attn_specs(t, d, bq):
    kvh = d // HEAD_DIM // GROUP
    kc0 = d // HEAD_DIM
    q_spec = pl.BlockSpec((1, bq, QKV_BLOCK), lambda i, h, j: (i, j, h))
    k_spec = pl.BlockSpec((1, t, HEAD_DIM), lambda i, h, j: (i, 0, kc0 + h))
    v_spec = pl.BlockSpec((1, t, HEAD_DIM), lambda i, h, j: (i, 0, kc0 + kvh + h))
    lse_spec = pl.BlockSpec((1, GROUP, bq, HEAD_DIM), lambda i, h, j: (i, h, j, 0))
    kt_spec = pl.BlockSpec((1, 1, HEAD_DIM, t), lambda i, h, j: (i, h, 0, 0))
    return kvh, q_spec, k_spec, v_spec, lse_spec, kt_spec


SCALE = HEAD_DIM ** -0.5


def _attn_dense_fwd(qkv, d, ctx_len):
    b, t, _ = qkv.shape
    bq = DENSE_FWD_BQ
    lq = ctx_len // bq
    kvh, q_spec, k_spec, v_spec, lse_spec, _ = _attn_specs(t, d, bq)

    def body(q_ref, k_ref, v_ref, o_ref, lse_ref):
        i = pl.program_id(2)

        def attend(k, v):
            for g in range(GROUP):
                sl = slice(g * HEAD_DIM, (g + 1) * HEAD_DIM)
                s = _dot_nt(q_ref[0, :, sl], k)
                m = jnp.max(s, axis=1, keepdims=True)
                p = jnp.exp2((s - m) * (SCALE * LOG2E))
                l = jnp.sum(p, axis=1, keepdims=True)
                o_ref[0, :, sl] = _dot(p.astype(BF16), v) / l
                lse_ref[0, g] = jnp.broadcast_to(m * SCALE + jnp.log(l), (bq, HEAD_DIM))

        @pl.when(i < lq)
        def _():
            attend(k_ref[0, 0:ctx_len, :], v_ref[0, 0:ctx_len, :])

        @pl.when(i >= lq)
        def _():
            attend(k_ref[0], v_ref[0])

    return pl.pallas_call(
        body, name="attn_dense_fwd", grid=(b, kvh, t // bq),
        in_specs=[q_spec, k_spec, v_spec], out_specs=(q_spec, lse_spec),
        out_shape=(jax.ShapeDtypeStruct((b, t, d), F32), jax.ShapeDtypeStruct((b, kvh * GROUP, t, HEAD_DIM), F32)),
        compiler_params=_cparams(("parallel", "parallel", "arbitrary"), VMEM_LIMIT),
    )(qkv, qkv, qkv)


def _attn_dense_bwd(qkv, o3, do3, lse, d, ctx_len):
    b, t, _ = qkv.shape
    bq = ATT_BQ
    lq = ctx_len // bq
    kvh, q_spec, k_spec, v_spec, lse_spec, kt_spec = _attn_specs(t, d, bq)

    def body(q_ref, k_ref, v_ref, o_ref, do_ref, lse_ref, dq_ref, dkt_ref, dvt_ref):
        i = pl.program_id(2)

        @pl.when(i == 0)
        def _():
            dkt_ref[...] = jnp.zeros_like(dkt_ref)
            dvt_ref[...] = jnp.zeros_like(dvt_ref)

        def run(k, v, width):
            dk_acc = dv_acc = None
            for g in range(GROUP):
                sl = slice(g * HEAD_DIM, (g + 1) * HEAD_DIM)
                q = q_ref[0, :, sl]
                do = do_ref[0, :, sl]
                dd = jnp.sum(do * o_ref[0, :, sl], axis=1, keepdims=True)
                dob = do.astype(BF16)
                p = jnp.exp2(_dot_nt(q, k) * (SCALE * LOG2E) - lse_ref[0, g][:, 0:1] * LOG2E)
                ds = (p * (_dot_nt(dob, v) - dd) * SCALE).astype(BF16)
                dq_ref[0, :, sl] = _dot(ds, k)
                dk_g = _dot(q.astype(F32).T.astype(BF16), ds)
                dv_g = _dot(do.T.astype(BF16), p.astype(BF16))
                dk_acc = dk_g if dk_acc is None else dk_acc + dk_g
                dv_acc = dv_g if dv_acc is None else dv_acc + dv_g
            dkt_ref[0, 0, :, 0:width] += dk_acc
            dvt_ref[0, 0, :, 0:width] += dv_acc

        @pl.when(i < lq)
        def _():
            run(k_ref[0, 0:ctx_len, :], v_ref[0, 0:ctx_len, :], ctx_len)

        @pl.when(i >= lq)
        def _():
            run(k_ref[0], v_ref[0], t)

    return pl.pallas_call(
        body, name="attn_dense_bwd", grid=(b, kvh, t // bq),
        in_specs=[q_spec, k_spec, v_spec, q_spec, q_spec, lse_spec], out_specs=(q_spec, kt_spec, kt_spec),
        out_shape=(jax.ShapeDtypeStruct((b, t, d), F32), jax.ShapeDtypeStruct((b, kvh, HEAD_DIM, t), F32),
                   jax.ShapeDtypeStruct((b, kvh, HEAD_DIM, t), F32)),
        compiler_params=_cparams(("parallel", "parallel", "arbitrary"), VMEM_LIMIT),
    )(qkv, qkv, qkv, o3, do3, lse)


def _sink_column(sink_ref, h, bq):
    rowi = lax.broadcasted_iota(jnp.int32, (GROUP * bq, 1), 0)
    col = jnp.zeros((GROUP * bq, 1), F32)
    for g in range(GROUP):
        col = jnp.where((rowi >= g * bq) & (rowi < (g + 1) * bq), sink_ref[h * GROUP + g], col)
    return col


def _band(i, lq, ctx_len, t, bq):
    n = i - lq
    start = pl.multiple_of(jnp.clip(ctx_len + n * bq - WINDOW, ctx_len, t - WIN_SPAN), WINDOW)
    shape = (GROUP * bq, WIN_SPAN)
    kpos = start - ctx_len + lax.broadcasted_iota(jnp.int32, shape, 1)
    qpos = n * bq + (lax.broadcasted_iota(jnp.int32, shape, 0) & (bq - 1))
    return start, jnp.abs(kpos - qpos) <= WINDOW


def _attn_win_fwd(qkv, sink, d, ctx_len):
    b, t, _ = qkv.shape
    bq = WIN_BQ
    rows = GROUP * bq
    lq = ctx_len // bq
    kvh, q_spec, k_spec, v_spec, lse_spec, _ = _attn_specs(t, d, bq)

    def body(sink_ref, q_ref, k_ref, v_ref, o_ref, lse_ref):
        h, i = pl.program_id(1), pl.program_id(2)
        q4 = _stack_heads(q_ref)
        sink_col = _sink_column(sink_ref, h, bq)
        sc = _dot_nt(q4, k_ref[0, 0:ctx_len, :]) * SCALE
        mc = jnp.maximum(jnp.max(sc, axis=1, keepdims=True), sink_col)

        def finish(m, l, acc):
            _unstack_heads(o_ref, acc / l, bq)
            lse_ref[0] = jnp.broadcast_to(m + jnp.log(l), (rows, HEAD_DIM)).reshape(GROUP, bq, HEAD_DIM)

        @pl.when(i < lq)
        def _():
            pc = jnp.exp(sc - mc)
            l = jnp.sum(pc, axis=1, keepdims=True) + jnp.exp(sink_col - mc)
            finish(mc, l, _dot(pc.astype(BF16), v_ref[0, 0:ctx_len, :]))

        @pl.when(i >= lq)
        def _():
            start, ok = _band(i, lq, ctx_len, t, bq)
            sb = jnp.where(ok, _dot_nt(q4, k_ref[0, pl.ds(start, WIN_SPAN), :]) * SCALE, NEG_INF)
            m = jnp.maximum(mc, jnp.max(sb, axis=1, keepdims=True))
            pc, pb = jnp.exp(sc - m), jnp.exp(sb - m)
            l = jnp.sum(pc, axis=1, keepdims=True) + jnp.sum(pb, axis=1, keepdims=True) + jnp.exp(sink_col - m)
            acc = _dot(pc.astype(BF16), v_ref[0, 0:ctx_len, :]) + _dot(pb.astype(BF16), v_ref[0, pl.ds(start, WIN_SPAN), :])
            finish(m, l, acc)

    return pl.pallas_call(
        body, name="attn_win_fwd", grid=(b, kvh, t // bq),
        in_specs=[pl.BlockSpec(memory_space=pltpu.SMEM), q_spec, k_spec, v_spec], out_specs=(q_spec, lse_spec),
        out_shape=(jax.ShapeDtypeStruct((b, t, d), F32), jax.ShapeDtypeStruct((b, kvh * GROUP, t, HEAD_DIM), F32)),
        compiler_params=_cparams(("parallel", "parallel", "arbitrary"), VMEM_LIMIT),
    )(sink, qkv, qkv, qkv)


def _attn_win_bwd(qkv, sink, o3, do3, lse, d, ctx_len):
    b, t, _ = qkv.shape
    bq = WIN_BQ
    rows = GROUP * bq
    lq = ctx_len // bq
    kvh, q_spec, k_spec, v_spec, lse_spec, kt_spec = _attn_specs(t, d, bq)

    def body(sink_ref, q_ref, k_ref, v_ref, o_ref, do_ref, lse_ref, dq_ref, dkt_ref, dvt_ref, dsk_ref):
        h, i = pl.program_id(1), pl.program_id(2)

        @pl.when(i == 0)
        def _():
            dkt_ref[...] = jnp.zeros_like(dkt_ref)
            dvt_ref[...] = jnp.zeros_like(dvt_ref)
            dsk_ref[...] = jnp.zeros_like(dsk_ref)

        q4 = _stack_heads(q_ref)
        do4 = _stack_heads(do_ref)
        dd = jnp.sum(do4 * _stack_heads(o_ref), axis=1, keepdims=True)
        lse_col = lse_ref[0].reshape(rows, HEAD_DIM)[:, 0:1]
        do4b = do4.astype(BF16)
        qt = q4.astype(F32).T.astype(BF16)
        dot = do4.T.astype(BF16)

        def part(k, v):
            return _dot_nt(q4, k) * SCALE, _dot_nt(do4b, v)

        def grads(p, dp, k):
            ds = (p * (dp - dd) * SCALE).astype(BF16)
            return _dot(ds, k), _dot(qt, ds), _dot(dot, p.astype(BF16))

        kc = k_ref[0, 0:ctx_len, :]
        sc, dpc = part(kc, v_ref[0, 0:ctx_len, :])
        dq_c, dk_c, dv_c = grads(jnp.exp(sc - lse_col), dpc, kc)
        dkt_ref[0, 0, :, 0:ctx_len] += dk_c
        dvt_ref[0, 0, :, 0:ctx_len] += dv_c
        _unstack_heads(dq_ref, dq_c, bq)

        @pl.when(i >= lq)
        def _():
            start, ok = _band(i, lq, ctx_len, t, bq)
            kb = k_ref[0, pl.ds(start, WIN_SPAN), :]
            sb, dpb = part(kb, v_ref[0, pl.ds(start, WIN_SPAN), :])
            pb = jnp.where(ok, jnp.exp(sb - lse_col), 0.0)
            dq_b, dk_b, dv_b = grads(pb, dpb, kb)
            dkt_ref[0, 0, :, pl.ds(start, WIN_SPAN)] += dk_b
            dvt_ref[0, 0, :, pl.ds(start, WIN_SPAN)] += dv_b
            for g in range(GROUP):
                dq_ref[0, :, g * HEAD_DIM:(g + 1) * HEAD_DIM] += dq_b[g * bq:(g + 1) * bq, :]

        ps = jnp.exp(_sink_column(sink_ref, h, bq) - lse_col) * dd
        for g in range(GROUP):
            val = jnp.sum(ps[g * bq:(g + 1) * bq, :], axis=0, keepdims=True)
            dsk_ref[0, 0, g:g + 1, :] -= jnp.broadcast_to(val, (1, HEAD_DIM))

    return pl.pallas_call(
        body, name="attn_win_bwd", grid=(b, kvh, t // bq),
        in_specs=[pl.BlockSpec(memory_space=pltpu.SMEM), q_spec, k_spec, v_spec, q_spec, q_spec, lse_spec],
        out_specs=(q_spec, kt_spec, kt_spec, pl.BlockSpec((1, 1, 8, HEAD_DIM), lambda i, h, j: (i, h, 0, 0))),
        out_shape=(jax.ShapeDtypeStruct((b, t, d), F32), jax.ShapeDtypeStruct((b, kvh, HEAD_DIM, t), F32),
                   jax.ShapeDtypeStruct((b, kvh, HEAD_DIM, t), F32), jax.ShapeDtypeStruct((b, kvh, 8, HEAD_DIM), F32)),
        compiler_params=_cparams(("parallel", "parallel", "arbitrary"), VMEM_LIMIT),
    )(sink, qkv, qkv, qkv, o3, do3, lse)


MERGE_BWD_ROWS = 256
MERGE_BWD_VMEM = 60 * 1024 * 1024


def _resident(shape):
    return pl.BlockSpec(shape, lambda *_: (0,) * len(shape), pipeline_mode=pl.Buffered(1))


def _merge_fwd(x3, ya, yb, yc, proj3, w_br, w_out, modsel, ctx_len):
    b, t, d = x3.shape
    bt = ROW_BLOCK
    lb = ctx_len // bt

    def body(x_ref, ya_ref, yb_ref, yc_ref, gm_ref, wbr_ref, wo_ref, m_ref, xn_ref, out_ref):
        mix = jnp.zeros((bt, d), F32)
        for n, y_ref in enumerate((ya_ref, yb_ref, yc_ref)):
            z = (y_ref[0] * _silu(gm_ref[0, :, n * d:(n + 1) * d].astype(F32))).astype(BF16)
            mix = mix + _sigmoid(gm_ref[0, :, (3 + n) * d:(4 + n) * d].astype(F32)) * _dot(z, wbr_ref[n])
        o = _dot(mix.astype(BF16), wo_ref[...])
        out_ref[0] = o
        xn_ref[0] = x_ref[0] + m_ref[0, 0, 2:3, :] * o

    blk = pl.BlockSpec((1, bt, d), lambda i, j: (i, j, 0))
    return pl.pallas_call(
        body, name="merge_fwd", grid=(b, t // bt),
        in_specs=[blk, blk, blk, blk, pl.BlockSpec((1, bt, 6 * d), lambda i, j: (i, j, 0)),
                  _resident((3, d, d)), _resident((d, d)),
                  pl.BlockSpec((1, 1, 8, d), lambda i, j: (i, _row_kind(j, lb), 0, 0))],
        out_specs=(blk, blk),
        out_shape=(jax.ShapeDtypeStruct((b, t, d), F32), jax.ShapeDtypeStruct((b, t, d), F32)),
        compiler_params=_cparams(("parallel", "arbitrary"), VMEM_LIMIT),
    )(x3, ya, yb, yc, proj3, w_br, w_out, modsel)


def _merge_bwd(dxn3, out3, ya, yb, yc, proj3, w_br, w_out, modsel, ctx_len):
    b, t, d = dxn3.shape
    n_cols = proj3.shape[2]
    bt = MERGE_BWD_ROWS
    lb = ctx_len // bt

    def body(dxn_ref, out_ref, ya_ref, yb_ref, yc_ref, gm_ref, wbr_ref, wo_ref, m_ref,
             dgm_ref, dya_ref, dyb_ref, dyc_ref, z_ref, dt_ref, mix_ref, dout_ref, gacc_ref):
        j = pl.program_id(1)
        dxn = dxn_ref[0]
        doutb = (m_ref[0, 0, 2:3, :] * dxn).astype(BF16)
        dout_ref[0] = doutb

        @pl.when((j == 0) | (j == lb))
        def _():
            gacc_ref[...] = jnp.zeros_like(gacc_ref)

        gacc_ref[0, 0, 0:1, :] += _colsum(dxn * out_ref[0])
        dmix = _dot_nt(doutb, wo_ref[...])
        mix = jnp.zeros((bt, d), F32)
        for n, (y_ref, dy_ref) in enumerate(((ya_ref, dya_ref), (yb_ref, dyb_ref), (yc_ref, dyc_ref))):
            g = gm_ref[0, :, n * d:(n + 1) * d].astype(F32)
            y = y_ref[0]
            sig_g = _sigmoid(g)
            silu_g = g * sig_g
            z = (y * silu_g).astype(BF16)
            z_ref[n, 0] = z
            tn = _dot(z, wbr_ref[n])
            s = _sigmoid(gm_ref[0, :, (3 + n) * d:(4 + n) * d].astype(F32))
            mix = mix + s * tn
            dgm_ref[0, :, (3 + n) * d:(4 + n) * d] = (dmix * tn * (s * (1.0 - s))).astype(BF16)
            dtb = (dmix * s).astype(BF16)
            dt_ref[n, 0] = dtb
            dz = _dot_nt(dtb, wbr_ref[n])
            dy_ref[0] = dz * silu_g
            dgm_ref[0, :, n * d:(n + 1) * d] = (dz * y * (sig_g * (1.0 + g * (1.0 - sig_g)))).astype(BF16)
        mix_ref[0] = mix.astype(BF16)

    blk = pl.BlockSpec((1, bt, d), lambda i, j: (i, j, 0))
    blk4 = pl.BlockSpec((3, 1, bt, d), lambda i, j: (0, i, j, 0))
    wide = pl.BlockSpec((1, bt, 6 * d), lambda i, j: (i, j, 0))
    return pl.pallas_call(
        body, name="merge_bwd", grid=(b, t // bt),
        in_specs=[blk, blk, blk, blk, blk, wide, _resident((3, d, d)), _resident((d, d)),
                  pl.BlockSpec((1, 1, 8, d), lambda i, j: (i, _row_kind(j, lb), 0, 0))],
        out_specs=(wide, blk, blk, blk, blk4, blk4, blk, blk,
                   pl.BlockSpec((1, 1, 8, d), lambda i, j: (i, _row_kind(j, lb), 0, 0))),
        out_shape=(jax.ShapeDtypeStruct((b, t, n_cols), BF16),
                   jax.ShapeDtypeStruct((b, t, d), F32), jax.ShapeDtypeStruct((b, t, d), F32),
                   jax.ShapeDtypeStruct((b, t, d), F32),
                   jax.ShapeDtypeStruct((3, b, t, d), BF16), jax.ShapeDtypeStruct((3, b, t, d), BF16),
                   jax.ShapeDtypeStruct((b, t, d), BF16), jax.ShapeDtypeStruct((b, t, d), BF16),
                   jax.ShapeDtypeStruct((b, 2, 8, d), F32)),
        compiler_params=_cparams(("parallel", "arbitrary"), MERGE_BWD_VMEM),
    )(dxn3, out3, ya, yb, yc, proj3, w_br, w_out, modsel)


def _final(x3, g, target, ctx_len):
    b, t, d = x3.shape
    bt = ROW_BLOCK
    lb = ctx_len // bt

    def body(x_ref, g_ref, t_ref, dx_ref, loss_ref, dg_ref):
        j = pl.program_id(1)

        @pl.when(j == 0)
        def _():
            loss_ref[...] = jnp.zeros_like(loss_ref)
            dg_ref[...] = jnp.zeros_like(dg_ref)

        @pl.when(j < lb)
        def _():
            dx_ref[...] = jnp.zeros_like(dx_ref)

        @pl.when(j >= lb)
        def _():
            x = x_ref[0]
            g_row = g_ref[...]
            rstd = lax.rsqrt(jnp.mean(x * x, axis=-1, keepdims=True) + EPS)
            xhat = x * rstd
            err = xhat * g_row - t_ref[0]
            loss_ref[...] += (0.5 / d) * jnp.sum(err * err)
            dy = err * (1.0 / d)
            dg_ref[0, 0:1, :] += _colsum(dy * xhat)
            dxhat = dy * g_row
            dx_ref[0] = rstd * (dxhat - xhat * jnp.mean(dxhat * xhat, axis=-1, keepdims=True))

    blk = pl.BlockSpec((1, bt, d), lambda i, j: (i, j, 0))
    return pl.pallas_call(
        body, name="final_loss", grid=(b, t // bt),
        in_specs=[blk, pl.BlockSpec((1, d), lambda i, j: (0, 0)),
                  pl.BlockSpec((1, bt, d), lambda i, j: (i, jnp.maximum(j - lb, 0), 0))],
        out_specs=(blk, pl.BlockSpec((1, 8, HEAD_DIM), lambda i, j: (i, 0, 0)), pl.BlockSpec((1, 8, d), lambda i, j: (i, 0, 0))),
        out_shape=(jax.ShapeDtypeStruct((b, t, d), F32), jax.ShapeDtypeStruct((b, 8, HEAD_DIM), F32),
                   jax.ShapeDtypeStruct((b, 8, d), F32)),
        compiler_params=_cparams(("parallel", "arbitrary")),
    )(x3, g, target)


TOKEN_BLOCKS = (1088, 512, 256, 128)


def _pick(n, options):
    for o in options:
        if n % o == 0:
            return o
    raise ValueError((n, options))


def _block_diag(w):
    per = LRU_LANES // LRU_BLOCK_W
    nd, nb, bw, _ = w.shape
    wr = w.reshape(nd, nb // per, per, bw, bw)
    eye = jnp.eye(per, dtype=w.dtype)
    bd = wr[:, :, :, :, None, :] * eye[None, None, :, None, :, None]
    return bd.reshape(nd, nb // per, per * bw, per * bw).astype(BF16)


def _block_diag_grad(g):
    per = LRU_LANES // LRU_BLOCK_W
    nd, ng, _, _ = g.shape
    gr = g.reshape(nd, ng, per, LRU_BLOCK_W, per, LRU_BLOCK_W)
    diag = jnp.stack([gr[:, :, k, :, k, :] for k in range(per)], axis=2)
    return diag.reshape(nd, ng * per, LRU_BLOCK_W, LRU_BLOCK_W)


def _mod_select(mod16, b, d):
    m3 = mod16.reshape(MOD_ROWS, 3, d)
    lat = m3[:b]
    ctx = jnp.broadcast_to(m3[b][None], (b, 3, d))
    sel = jnp.stack([ctx, lat], axis=1)
    return jnp.pad(sel, ((0, 0), (0, 0), (0, 5), (0, 0)))


def _layer_fwd(x3, c16, p, cos, sin, ctx_len):
    b, t, d = x3.shape
    off, n_cols = _layout(d)
    mod16 = _mod_fwd(c16, p["w_mod"], p["b_mod"])
    modsel = _mod_select(mod16, b, d)
    h = _norm_mod_fwd(x3, p["norm_g"], modsel, ctx_len)
    proj = _matmul(h.reshape(b * t, d), p["w_in"], bm=_pick(b * t, TOKEN_BLOCKS), bn=1024, bk=d, name="proj_fwd",
                   n_outer=True, b_lead=p["li"], out_dtype=BF16)
    proj3 = proj.reshape(b, t, n_cols)
    ya = _lru_fwd(proj3, off["uA"], p["conv_w"], p["conv_b"], p["wa_bd"], p["ba"], p["wx_bd"], p["bx"], p["lam"], ctx_len)
    qkv_b = _prep_fwd(proj3, off["qB"], off["kB"], d, cos, sin, p["gq"], p["gk"], use_norm=False)
    yb, lse_b = _attn_win_fwd(qkv_b, p["sink"], d, ctx_len)
    qkv_c = _prep_fwd(proj3, off["qC"], off["kC"], d, cos, sin, p["gq"], p["gk"], use_norm=True)
    yc, lse_c = _attn_dense_fwd(qkv_c, d, ctx_len)
    x_new, out3 = _merge_fwd(x3, ya, yb, yc, proj3, p["w_br"], p["w_out"], modsel, ctx_len)
    return x_new, (x3, modsel, h, proj3, ya, yb, yc, qkv_b, lse_b, qkv_c, lse_c, out3)


def _layer_bwd(dxn3, saved, c16, p, cos, sin, ctx_len):
    x3, modsel, h, proj3, ya, yb, yc, qkv_b, lse_b, qkv_c, lse_c, out3 = saved
    b, t, d = x3.shape
    off, n_cols = _layout(d)
    rows = b * t
    bk = _pick(rows, (2 * TOKEN_BLOCKS[0],) + TOKEN_BLOCKS)
    dproj3, dya, dyb, dyc, z4, dt4, mixb, doutb, gacc = _merge_bwd(dxn3, out3, ya, yb, yc, proj3, p["w_br"], p["w_out"],
                                                                   modsel, ctx_len)
    dw_br = jnp.stack([_matmul(z4.reshape(3, rows, d), dt4.reshape(3, rows, d), ta=True, bm=d, bn=d, bk=bk,
                               name="dw_branch", a_lead=n, b_lead=n) for n in range(3)])
    dw_out = _matmul(mixb.reshape(rows, d), doutb.reshape(rows, d), ta=True, bm=d, bn=d, bk=bk, name="dw_out")
    dproj3, vec, dwa, dwx = _lru_bwd(dproj3, proj3, off["uA"], dya, p["conv_w"], p["conv_b"], p["wa_bd"], p["ba"],
                                     p["wx_bd"], p["bx"], p["lam"], ctx_len)
    dq_b, dkt_b, dvt_b, dsk = _attn_win_bwd(qkv_b, p["sink"], yb, dyb, lse_b, d, ctx_len)
    dproj3, _ = _prep_bwd(dproj3, dq_b, dkt_b, dvt_b, proj3, off["qB"], off["kB"], d, cos, sin, p["gq"], p["gk"], False)
    dq_c, dkt_c, dvt_c = _attn_dense_bwd(qkv_c, yc, dyc, lse_c, d, ctx_len)
    dproj3, gqk = _prep_bwd(dproj3, dq_c, dkt_c, dvt_c, proj3, off["qC"], off["kC"], d, cos, sin, p["gq"], p["gk"], True)
    dproj2 = dproj3.reshape(rows, n_cols)
    dw_in = _matmul(h.reshape(rows, d), dproj2, ta=True, bm=d, bn=1024, bk=bk, name="dw_in")
    dh = _matmul(dproj2, p["w_in"], tb=True, bm=_pick(rows, TOKEN_BLOCKS), bn=d, bk=_pick(n_cols, (2560, 1024)),
                 name="dh", b_lead=p["li"])
    dx3, nacc = _norm_mod_bwd(dh.reshape(b, t, d), x3, p["norm_g"], modsel, dxn3, ctx_len)
    per = jnp.stack([nacc[:, :, 0], nacc[:, :, 1], gacc[:, :, 0]], axis=2)
    dmod = jnp.concatenate([per[:, 1].reshape(b, 3 * d), jnp.sum(per[:, 0], axis=0).reshape(1, 3 * d)], axis=0)
    dmod16 = jnp.pad(dmod, ((0, MOD_ROWS - b - 1), (0, 0)))
    dw_mod, db_mod, dc16 = _mod_bwd(c16, dmod16, p["w_mod"])
    vsum = jnp.sum(vec, axis=0)
    grads = {
        "norm_g": jnp.sum(nacc[:, :, 2], axis=(0, 1)),
        "w_mod": dw_mod, "b_mod": db_mod[0], "w_in": dw_in,
        "conv_w": vsum[0:4], "conv_b": vsum[4],
        "lru_wa": _block_diag_grad(jnp.sum(dwa, axis=0)), "lru_ba": vsum[5:7],
        "lru_wx": _block_diag_grad(jnp.sum(dwx, axis=0)), "lru_bx": vsum[7:9],
        "lru_lambda": vsum[9:11] * (-jax.nn.sigmoid(-p["lam"])),
        "attn_sink": jnp.sum(dsk[:, :, 0:GROUP, 0], axis=0).reshape(-1),
        "q_norm_g": jnp.sum(gqk[:, 0], axis=0), "k_norm_g": jnp.sum(gqk[:, 1], axis=0),
        "w_branch": dw_br, "w_out": dw_out,
    }
    return dx3, dc16, grads


def _reorder_in_cols(w, d, inverse=False):
    off_new, _ = _layout(d)
    segs = _orig_segments(d)
    if inverse:
        return jnp.concatenate([w[..., off_new[n]:off_new[n] + wd] for n, _, wd in segs], axis=-1)
    by_name = {n: (o, wd) for n, o, wd in segs}
    order = sorted(off_new, key=off_new.get)
    return jnp.concatenate([w[..., by_name[n][0]:by_name[n][0] + by_name[n][1]] for n in order], axis=-1)


def _layer_params(li, w, g=None, lj=None):
    g = w if g is None else g
    lj = li if lj is None else lj
    return {
        "li": lj, "norm_g": w["norm_g"][li][None], "w_mod": g["w_mod"][lj], "b_mod": w["b_mod"][li][None],
        "w_in": g["w_in_r"],
        "conv_w": g["conv_w"][lj], "conv_b": w["conv_b"][li][None],
        "wa_bd": _block_diag(w["lru_wa"][li]), "ba": g["lru_ba"][lj],
        "wx_bd": _block_diag(w["lru_wx"][li]), "bx": g["lru_bx"][lj], "lam": g["lru_lambda"][lj],
        "sink": w["attn_sink"][li], "gq": w["q_norm_g"][li][None], "gk": w["k_norm_g"][li][None],
        "w_br": g["w_branch"][lj], "w_out": g["w_out"][lj],
    }


def _local_step(x, c, ctx, target, c_ctx, final_g, layers):
    b, s, d = x.shape
    ctx_len = ctx.shape[1]
    cos, sin = _rope_tables(ctx_len, s)
    x3 = jnp.concatenate([ctx, x], axis=1)
    c16 = jnp.concatenate([c, c_ctx[None], jnp.zeros((MOD_ROWS - b - 1, d), F32)], axis=0)
    saved = []
    layers = list(layers)
    for li, p in enumerate(layers):
        if callable(p):
            p = layers[li] = p(x3)
        x3, sv = _layer_fwd(x3, c16, p, cos, sin, ctx_len)
        saved.append(sv)
    dx3, loss_acc, dgf = _final(x3, final_g[None], target, ctx_len)
    grads = [None] * len(layers)
    dc_ctx = jnp.zeros((d,), F32)
    for li in reversed(range(len(layers))):
        dx3, dc16, grads[li] = _layer_bwd(dx3, saved[li], c16, layers[li], cos, sin, ctx_len)
        dc_ctx = dc_ctx + dc16[b]
    return jnp.sum(loss_acc[:, 0, 0]), dx3[:, ctx_len:], dc_ctx, jnp.sum(dgf[:, 0], axis=0), grads


N_CHIPS = 4
ANY = pl.BlockSpec(memory_space=pl.ANY)


def _place():
    x, y, c = lax.axis_index("x"), lax.axis_index("y"), lax.axis_index("c")
    return x, y, c, [(1 - x, y), (x, 1 - y), (1 - x, 1 - y)]


def _axis_part(ref, axis, start, size):
    idx = [slice(None)] * len(ref.shape)
    idx[axis] = pl.ds(start, size)
    return ref.at[tuple(idx)]


def _remote(src, dst, send, recv, dev):
    return pltpu.make_async_remote_copy(src_ref=src, dst_ref=dst, send_sem=send, recv_sem=recv, device_id=dev,
                                        device_id_type=MESH)


COPY_PIECES = 8


def _pieces(src, dst):
    shape = src.shape
    for ax in range(len(shape) - 1):
        if shape[ax] % COPY_PIECES == 0 and shape[ax] // COPY_PIECES >= 8:
            sz = shape[ax] // COPY_PIECES
            return [(_axis_part(src, ax, j * sz, sz), _axis_part(dst, ax, j * sz, sz)) for j in range(COPY_PIECES)]
    return [(src, dst)]


def _gather_chips(wholes, axes, name):
    n = len(wholes)

    def body(*refs):
        bufs = refs[n:2 * n]
        send, recv, fsend, frecv = refs[2 * n:]
        x, y, c, chips = _place()
        me = 2 * x + y
        sib = (x, y, 1 - c)

        def block(i, chip_index, half):
            sz = wholes[i].shape[axes[i]] // N_CHIPS
            hl = wholes[i].shape[0] // 2
            return _axis_part(bufs[i], axes[i], chip_index * sz, sz).at[pl.ds(half * hl, hl)]

        for i in range(n):
            for k, (px, py) in enumerate(chips):
                _remote(block(i, me, c), block(i, me, c), send.at[i, k], recv.at[i, k], (px, py, c)).start()
        for i in range(n):
            for k, (px, py) in enumerate(chips):
                landed = block(i, 2 * px + py, c)
                _remote(landed, landed, send.at[i, k], recv.at[i, k], (px, py, c)).wait_recv()
                for s_, d_ in _pieces(landed, landed):
                    _remote(s_, d_, fsend.at[i, k], frecv.at[i, k], sib).start()
        for i in range(n):
            for k, (px, py) in enumerate(chips):
                passed = _remote(block(i, 2 * px + py, c), block(i, 2 * px + py, 1 - c), fsend.at[i, k], frecv.at[i, k], sib)
                passed.wait_recv()
                passed.wait_send()
                _remote(block(i, me, c), block(i, me, c), send.at[i, k], recv.at[i, k], (px, py, c)).wait_send()

    sems = pltpu.SemaphoreType.DMA((n, 3))
    return pl.pallas_call(
        body, name=name, in_specs=[ANY] * n, out_specs=tuple([ANY] * n),
        out_shape=tuple(jax.ShapeDtypeStruct(a.shape, a.dtype) for a in wholes),
        input_output_aliases={i: i for i in range(n)},
        scratch_shapes=[sems, sems, sems, sems],
    )(*wholes)


HBM = pl.BlockSpec(memory_space=pltpu.HBM)
SEM = pl.BlockSpec(memory_space=pltpu.SEMAPHORE)
DATAFLOW = pltpu.SideEffectType.DATAFLOW_SIDE_EFFECTING


def _gather_start(wholes, axes, name):
    n = len(wholes)

    def body(*refs):
        bufs = refs[:n]
        send, recv, token = refs[2 * n:]
        x, y, c, chips = _place()
        me = 2 * x + y
        for i in range(n):
            sz = wholes[i].shape[axes[i]] // N_CHIPS
            mine = _axis_part(bufs[i], axes[i], me * sz, sz)
            for k, (px, py) in enumerate(chips):
                _remote(mine, mine, send.at[3 * i + k], recv.at[3 * i + k], (px, py, c)).start()
        token[...] = jnp.zeros_like(token)

    outs = pl.pallas_call(
        body, name=name, in_specs=[HBM] * n,
        out_specs=tuple([HBM] * n + [SEM, SEM, pl.BlockSpec(memory_space=pltpu.VMEM)]),
        out_shape=tuple(pltpu.HBM(a.shape, a.dtype) for a in wholes)
        + (pltpu.SemaphoreType.DMA((3 * n,)), pltpu.SemaphoreType.DMA((3 * n,)), jax.ShapeDtypeStruct((8, 128), F32)),
        input_output_aliases={i: i for i in range(n)},
        compiler_params=pltpu.CompilerParams(has_side_effects=DATAFLOW),
    )(*[pltpu.with_memory_space_constraint(a, pltpu.HBM) for a in wholes])
    return outs[:n], outs[n], outs[n + 1], outs[n + 2]


def _gather_wait(flying, send, recv, after, axes, name):
    n = len(flying)

    def body(*refs):
        bufs = refs[:n]
        send_sems, recv_sems = refs[n], refs[n + 1]
        x, y, c, chips = _place()
        me = 2 * x + y
        for i in range(n):
            sz = flying[i].shape[axes[i]] // N_CHIPS
            mine = _axis_part(bufs[i], axes[i], me * sz, sz)
            for k, (px, py) in enumerate(chips):
                theirs = _axis_part(bufs[i], axes[i], (2 * px + py) * sz, sz)
                cp = _remote(mine, theirs, send_sems.at[3 * i + k], recv_sems.at[3 * i + k], (px, py, c))
                cp.wait_send()
                cp.wait_recv()

    return pl.pallas_call(
        body, name=name, in_specs=[HBM] * n + [SEM, SEM, pl.BlockSpec(memory_space=pl.ANY)],
        out_specs=tuple([HBM] * n), out_shape=tuple(pltpu.HBM(a.shape, a.dtype) for a in flying),
        input_output_aliases={i: i for i in range(n)},
        compiler_params=pltpu.CompilerParams(has_side_effects=DATAFLOW),
    )(*flying, send, recv, after)


def _place_shard(shard, axis, dtype, name):
    cols_mode = axis == shard.ndim - 1
    assert cols_mode or axis == shard.ndim - 2
    sv = shard.reshape(-1, shard.shape[-2], shard.shape[-1])
    la, r, c = sv.shape
    br = _row_block(r, c)
    per = r // br

    def body(x_ref, o_ref):
        o_ref[...] = x_ref[...].astype(dtype)

    if cols_mode:
        out_shape, out_map = (la, r, N_CHIPS * c), lambda l, i: (l, i, _chip())
    else:
        out_shape, out_map = (la, N_CHIPS * r, c), lambda l, i: (l, _chip() * per + i, 0)
    out = pl.pallas_call(
        body, name=name, grid=(la, per),
        in_specs=[pl.BlockSpec((1, br, c), lambda l, i: (l, i, 0))], out_specs=pl.BlockSpec((1, br, c), out_map),
        out_shape=jax.ShapeDtypeStruct(out_shape, dtype), compiler_params=_cparams(("parallel", "parallel")))(sv)
    shape = list(shard.shape)
    shape[axis] *= N_CHIPS
    return out.reshape(tuple(shape))


def _split_cores(gs, name):
    n = len(gs)

    def body(*refs):
        ins, got = refs[:n], refs[n:2 * n]
        send, recv = refs[2 * n:]
        x, y, c, _ = _place()
        sib = (x, y, 1 - c)

        def theirs(i):
            hl = gs[i].shape[0] // 2
            return ins[i].at[pl.ds((1 - c) * hl, hl)]

        for i in range(n):
            for s_, d_ in _pieces(theirs(i), got[i]):
                _remote(s_, d_, send.at[i], recv.at[i], sib).start()
        for i in range(n):
            _remote(theirs(i), got[i], send.at[i], recv.at[i], sib).wait()

    return pl.pallas_call(
        body, name=name, in_specs=[ANY] * n, out_specs=tuple([ANY] * n),
        out_shape=tuple(jax.ShapeDtypeStruct((g.shape[0] // 2,) + g.shape[1:], g.dtype) for g in gs),
        scratch_shapes=[pltpu.SemaphoreType.DMA((n,)), pltpu.SemaphoreType.DMA((n,))],
    )(*gs)


def _scatter_chips(pbs, axes, name):
    n = len(pbs)

    def block(p, ax):
        shape = list(p.shape)
        shape[ax] //= N_CHIPS
        return tuple(shape)

    def body(*refs):
        inb, got = refs[:n], refs[n:2 * n]
        send, recv = refs[2 * n:]
        x, y, c, chips = _place()

        def part(i, chip_index):
            sz = pbs[i].shape[axes[i]] // N_CHIPS
            return _axis_part(inb[i], axes[i], chip_index * sz, sz)

        for i in range(n):
            for k, (px, py) in enumerate(chips):
                _remote(part(i, 2 * px + py), got[i].at[k], send.at[i, k], recv.at[i, k], (px, py, c)).start()
        for i in range(n):
            for k, (px, py) in enumerate(chips):
                _remote(part(i, 2 * px + py), got[i].at[k], send.at[i, k], recv.at[i, k], (px, py, c)).wait()

    return pl.pallas_call(
        body, name=name, in_specs=[ANY] * n, out_specs=tuple([ANY] * n),
        out_shape=tuple(jax.ShapeDtypeStruct((3,) + block(p, ax), p.dtype) for p, ax in zip(pbs, axes)),
        scratch_shapes=[pltpu.SemaphoreType.DMA((n, 3)), pltpu.SemaphoreType.DMA((n, 3))],
    )(*pbs)


def _join_cores(bufs, name):
    n = len(bufs)

    def body(*refs):
        outs = refs[n:2 * n]
        send, recv = refs[2 * n:]
        x, y, c, _ = _place()
        sib = (x, y, 1 - c)

        def half(i, which):
            hl = bufs[i].shape[0] // 2
            return outs[i].at[pl.ds(which * hl, hl)]

        for i in range(n):
            for s_, d_ in _pieces(half(i, c), half(i, c)):
                _remote(s_, d_, send.at[i], recv.at[i], sib).start()
        for i in range(n):
            cp = _remote(half(i, c), half(i, 1 - c), send.at[i], recv.at[i], sib)
            cp.wait_recv()
            cp.wait_send()

    return pl.pallas_call(
        body, name=name, in_specs=[ANY] * n, out_specs=tuple([ANY] * n),
        out_shape=tuple(jax.ShapeDtypeStruct(a.shape, a.dtype) for a in bufs),
        input_output_aliases={i: i for i in range(n)},
        scratch_shapes=[pltpu.SemaphoreType.DMA((n,)), pltpu.SemaphoreType.DMA((n,))],
    )(*bufs)


def _all_reduce_small(buf):
    r = buf.shape[0]

    def body(in_ref, out_ref, sib_buf, chip_sum, got, send, recv):
        x, y, c, chips = _place()
        cp = _remote(in_ref, sib_buf, send.at[0], recv.at[0], (x, y, 1 - c))
        cp.start()
        cp.wait()
        chip_sum[...] = in_ref[...] + sib_buf[...]
        cps = [_remote(chip_sum, got.at[k], send.at[1 + k], recv.at[1 + k], (px, py, c)) for k, (px, py) in enumerate(chips)]
        for cp in cps:
            cp.start()
        for cp in cps:
            cp.wait()
        out_ref[...] = (chip_sum[...] + got[0]) + (got[1] + got[2])

    return pl.pallas_call(
        body, name="all_reduce_small", out_shape=jax.ShapeDtypeStruct(buf.shape, F32),
        in_specs=[pl.BlockSpec(memory_space=pltpu.VMEM)], out_specs=pl.BlockSpec(memory_space=pltpu.VMEM),
        scratch_shapes=[pltpu.VMEM((r, 128), F32), pltpu.VMEM((r, 128), F32), pltpu.VMEM((3, r, 128), F32),
                        pltpu.SemaphoreType.DMA((4,)), pltpu.SemaphoreType.DMA((4,))],
        compiler_params=_cparams(None, VMEM_LIMIT),
    )(buf)


ELEMENTWISE_BLOCK_BYTES = 1 << 20


def _view2d(a):
    cols = a.shape[-1] if a.ndim > 1 else 128
    return a.reshape(-1, cols)


def _row_block(rows, cols):
    want = max(8, ELEMENTWISE_BLOCK_BYTES // (4 * cols))
    br = rows
    while br > want and br % 2 == 0 and (br // 2) % 16 == 0:
        br //= 2
    return br


def _core():
    return lax.axis_index("c")


def _chip():
    return 2 * lax.axis_index("x") + lax.axis_index("y")


def _sum_half(g, got, name):
    h = got.shape[0]
    gv = g.reshape(2 * h, -1, g.shape[-1])
    tv = got.reshape(h, -1, g.shape[-1])
    _, rows, cols = tv.shape
    br = _row_block(rows, cols)

    def body(g_ref, t_ref, p_ref, pb_ref):
        p = g_ref[...] + t_ref[...]
        p_ref[...] = p
        pb_ref[...] = p.astype(BF16)

    blk = pl.BlockSpec((1, br, cols), lambda l, i: (l, i, 0))
    p, pb = pl.pallas_call(
        body, name=name, grid=(h, rows // br),
        in_specs=[pl.BlockSpec((1, br, cols), lambda l, i: (_core() * h + l, i, 0)), blk], out_specs=(blk, blk),
        out_shape=(jax.ShapeDtypeStruct(tv.shape, F32), jax.ShapeDtypeStruct(tv.shape, BF16)),
        compiler_params=_cparams(("parallel", "parallel")))(gv, tv)
    return p.reshape(got.shape), pb.reshape(got.shape)


def _sum_blocks(p, got3, axis, name):
    h = p.shape[0]
    blk_shape = got3.shape[1:]
    cols_mode = axis == p.ndim - 1
    pv = p.reshape(-1, p.shape[-2], p.shape[-1])
    tv = got3.reshape(3, -1, blk_shape[-2], blk_shape[-1])
    la, rb, cb = tv.shape[1:]
    assert cols_mode or axis == p.ndim - 2
    br = _row_block(rb, cb)
    per = rb // br

    def body(p_ref, a_ref, b_ref, c_ref, out_ref):
        out_ref[...] = ((p_ref[...] + a_ref[0].astype(F32)) + b_ref[0].astype(F32)) + c_ref[0].astype(F32)

    if cols_mode:
        p_spec = pl.BlockSpec((1, br, cb), lambda l, i: (l, i, _chip()))
    else:
        p_spec = pl.BlockSpec((1, br, cb), lambda l, i: (l, _chip() * per + i, 0))
    out = pl.pallas_call(
        body, name=name, grid=(la, per),
        in_specs=[p_spec] + [pl.BlockSpec((1, 1, br, cb), lambda l, i, k=k: (k, l, i, 0)) for k in range(3)],
        out_specs=pl.BlockSpec((1, br, cb), lambda l, i: (_core() * la + l, i, 0)),
        out_shape=jax.ShapeDtypeStruct((2 * la, rb, cb), F32),
        compiler_params=_cparams(("parallel", "parallel")))(pv, tv, tv, tv)
    return out.reshape((2 * h,) + blk_shape[1:])


def _adamw(w, g, m, v, name):
    shape = w.shape
    ops = [_view2d(a) for a in (w, g, m, v)]
    rows, cols = ops[0].shape
    br = _row_block(rows, cols)
    c1 = 1.0 - ADAM_B1 ** ADAM_STEP
    c2 = 1.0 - ADAM_B2 ** ADAM_STEP

    def body(w_ref, g_ref, m_ref, v_ref, d_ref, nm_ref, nv_ref):
        g_ = g_ref[...]
        nm = ADAM_B1 * m_ref[...] + (1.0 - ADAM_B1) * g_
        nv = ADAM_B2 * v_ref[...] + (1.0 - ADAM_B2) * (g_ * g_)
        d_ref[...] = -ADAM_LR * ((nm / c1) / (jnp.sqrt(nv / c2) + ADAM_EPS) + ADAM_WD * w_ref[...])
        nm_ref[...] = nm
        nv_ref[...] = nv

    blk = pl.BlockSpec((br, cols), lambda i: (i, 0))
    outs = pl.pallas_call(body, name=name, grid=(rows // br,), in_specs=[blk] * 4, out_specs=(blk, blk, blk),
                          out_shape=tuple(jax.ShapeDtypeStruct((rows, cols), F32) for _ in range(3)),
                          compiler_params=_cparams(("parallel",)))(*ops)
    return tuple(o.reshape(shape) for o in outs)


PACK_ROWS = 2048


def _packed_rows(shape):
    return -(-int(np.prod(shape)) // (8 * 128)) * 8


def _pack(arrays):
    pieces = []
    for a in arrays:
        flat = a.reshape(-1)
        pieces.append(jnp.pad(flat, (0, _packed_rows(a.shape) * 128 - flat.shape[0])).reshape(-1, 128))
    rows = sum(p.shape[0] for p in pieces)
    pieces.append(jnp.zeros((-rows % PACK_ROWS, 128), F32))
    return jnp.concatenate(pieces, axis=0)


def _unpack(buf, shapes):
    out, o = [], 0
    for s in shapes:
        n, rows = int(np.prod(s)), _packed_rows(s)
        out.append(buf[o:o + rows].reshape(-1)[:n].reshape(s))
        o += rows
    return out


WEIGHTS = ["c_ctx", "norm_g", "w_mod", "b_mod", "w_in", "conv_w", "conv_b", "lru_wa", "lru_ba", "lru_wx", "lru_bx",
           "lru_lambda", "attn_sink", "q_norm_g", "k_norm_g", "w_branch", "w_out", "final_g"]
BIG = {"w_mod": 2, "w_in": 2, "w_branch": 2, "w_out": 1}
SMALL_SHARDED = ["conv_w", "lru_ba", "lru_bx", "lru_lambda"]
REPLICATED = [n for n in WEIGHTS if n not in BIG and n not in SMALL_SHARDED]


def kernel(x, c, ctx, c_ctx, norm_g, w_mod, b_mod, w_in, conv_w, conv_b, lru_wa, lru_ba, lru_wx, lru_bx, lru_lambda, attn_sink, q_norm_g, k_norm_g, w_branch, w_out, final_g, loss_target, m_c_ctx, m_norm_g, m_w_mod, m_b_mod, m_w_in, m_conv_w, m_conv_b, m_lru_wa, m_lru_ba, m_lru_wx, m_lru_bx, m_lru_lambda, m_attn_sink, m_q_norm_g, m_k_norm_g, m_w_branch, m_w_out, m_final_g, v_c_ctx, v_norm_g, v_w_mod, v_b_mod, v_w_in, v_conv_w, v_conv_b, v_lru_wa, v_lru_ba, v_lru_wx, v_lru_bx, v_lru_lambda, v_attn_sink, v_q_norm_g, v_k_norm_g, v_w_branch, v_w_out, v_final_g):
    args = dict(locals())
    w = {n: args[n] for n in WEIGHTS}
    mom = {n: args["m_" + n] for n in WEIGHTS}
    var = {n: args["v_" + n] for n in WEIGHTS}
    depth, d = norm_g.shape
    chip = 2 * lax.axis_index("x") + lax.axis_index("y")

    big_names = list(BIG)
    small_shard = jnp.concatenate([w[n] for n in SMALL_SHARDED], axis=1)
    gather_axes = [BIG[n] for n in big_names] + [2]
    first = depth // 2

    def placed(lo, hi, tag):
        out = [_place_shard(w[n][lo:hi], BIG[n], BF16, "place_" + tag + n) for n in big_names]
        return out + [_place_shard(small_shard[lo:hi], 2, F32, "place_" + tag + "small")]

    def named(gathered):
        g = dict(zip(big_names, gathered[:-1]))
        o = 0
        for n in SMALL_SHARDED:
            rows = w[n].shape[1]
            g[n] = gathered[-1][:, o:o + rows]
            o += rows
        g["w_in_r"] = _reorder_in_cols(g["w_in"], d)
        return g

    early = named(_gather_chips(placed(0, first, "first_"), gather_axes, "gather_weights_first"))
    flying, send_sems, recv_sems, token = _gather_start(placed(first, depth, "rest_"), gather_axes, "gather_rest_start")
    late = {}

    def late_layer(li):
        def params(x3):
            if not late:
                late.update(named(_gather_wait(flying, send_sems, recv_sems, x3, gather_axes, "gather_rest_wait")))
            return _layer_params(li, w, late, li - first)
        return params

    layers = [_layer_params(li, w, early, li) for li in range(first)] + [late_layer(li) for li in range(first, depth)]
    loss_local, grad_x, g_c_ctx, g_final, lgrads = _local_step(x, c + token[0, 0], ctx, loss_target, c_ctx, final_g, layers)
    loss = lax.psum(loss_local, ("x", "y", "c"))
    full = {n: jnp.stack([lg[n] for lg in lgrads]) for n in lgrads[0]}
    full["w_in"] = _reorder_in_cols(full["w_in"], d, inverse=True)
    full["c_ctx"], full["final_g"] = g_c_ctx, g_final

    bigs = [full[n] for n in big_names]
    got = _split_cores(bigs, "grad_split_cores")
    parts = [_sum_half(g, t_, "grad_chip_sum") for g, t_ in zip(bigs, got)]
    recv = _scatter_chips([pb for _, pb in parts], [BIG[n] for n in big_names], "grad_scatter_chips")
    totals = [_sum_blocks(p_, r, BIG[n], "grad_total") for (p_, _), r, n in zip(parts, recv, big_names)]
    grad = dict(zip(big_names, _join_cores(totals, "grad_join_cores")))

    small_names = REPLICATED + SMALL_SHARDED
    reduced = _unpack(_all_reduce_small(_pack([full[n] for n in small_names])), [full[n].shape for n in small_names])
    for n, g in zip(small_names, reduced):
        if n in SMALL_SHARDED:
            sz = w[n].shape[-1]
            g = lax.dynamic_slice_in_dim(g, chip * sz, sz, axis=g.ndim - 1)
        grad[n] = g

    delta, new_m, new_v = {}, {}, {}
    for n in big_names:
        delta[n], new_m[n], new_v[n] = _adamw(w[n], grad[n], mom[n], var[n], "adamw_" + n)
    shapes = [w[n].shape for n in small_names]
    packed = _adamw(_pack([w[n] for n in small_names]), _pack([grad[n] for n in small_names]),
                    _pack([mom[n] for n in small_names]), _pack([var[n] for n in small_names]), "adamw_small")
    for res, p in zip((delta, new_m, new_v), packed):
        res.update(dict(zip(small_names, _unpack(p, shapes))))

    return (loss, grad_x, *[grad[n] for n in WEIGHTS], *[delta[n] for n in WEIGHTS],
            *[new_m[n] for n in WEIGHTS], *[new_v[n] for n in WEIGHTS])
```

```python
import functools

import jax
import jax.numpy as jnp
import numpy as np
from jax import lax
from jax.experimental import pallas as pl
from jax.experimental.pallas import tpu as pltpu

F32 = jnp.float32
BF16 = jnp.bfloat16

HEAD_DIM = 128
GROUP = 4
LRU_BLOCK_W = 64
LRU_C = 8.0
WINDOW = 128
GRID_W = 64
ROPE_THETA = 10000.0
EPS = 1e-6
NEG_INF = -1e30
ADAM_LR, ADAM_B1, ADAM_B2, ADAM_EPS, ADAM_WD, ADAM_STEP = 0.001, 0.9, 0.999, 1e-08, 0.01, 10

ROW_BLOCK = 256
LRU_LANES = 128
LRU_CHUNK = 128
LRU_UNROLL = 2
DENSE_FWD_BQ = 256
LOG2E = 1.4426950408889634
ATT_BQ = 256
WIN_BQ = 256
WIN_SPAN = WIN_BQ + 2 * WINDOW
MOD_ROWS = 16
VMEM_LIMIT = 56 * 1024 * 1024

MESH = pl.DeviceIdType.MESH


def _cparams(sem=None, vmem=None):
    kw = {}
    if sem is not None:
        kw["dimension_semantics"] = sem
    if vmem is not None:
        kw["vmem_limit_bytes"] = vmem
    return pltpu.CompilerParams(**kw)


def _sigmoid(v):
    return 1.0 / (1.0 + jnp.exp(-v))


def _silu(v):
    return v * _sigmoid(v)


def _dsilu(v):
    s = _sigmoid(v)
    return s * (1.0 + v * (1.0 - s))


def _one_minus_square(log_a, a):
    z2 = log_a * log_a
    series = (-2.0 * a * log_a) * (1.0 + z2 * (1.0 / 6 + z2 * (1.0 / 120 + z2 * (1.0 / 5040))))
    return jnp.where(z2 < 0.25, series, 1.0 - a * a)


def _log1p(y):
    u = 1.0 + y
    d = u - 1.0
    return jnp.where(d == 0.0, y, jnp.log(u) * (y / jnp.where(d == 0.0, 1.0, d)))


def _softplus(x):
    return jnp.maximum(x, 0.0) + _log1p(jnp.exp(-jnp.abs(x)))


def _dot(a, b):
    return jnp.dot(a, b, preferred_element_type=F32)


def _dot_nt(a, b):
    return lax.dot_general(a, b, (((1,), (1,)), ((), ())), preferred_element_type=F32)


def _dot_tn(a, b):
    return lax.dot_general(a, b, (((0,), (0,)), ((), ())), preferred_element_type=F32)


def _colsum(v):
    return jnp.sum(v, axis=0, keepdims=True)


def _layout(d_model):
    kvw = (d_model // HEAD_DIM // GROUP) * HEAD_DIM
    names = ["gA", "gB", "gC", "mA", "mB", "mC", "uA", "qB", "qC", "kB", "vB", "kC", "vC"]
    widths = [d_model] * 9 + [kvw] * 4
    off, o = {}, 0
    for n, w in zip(names, widths):
        off[n] = o
        o += w
    return off, o


def _orig_segments(d_model):
    kvw = (d_model // HEAD_DIM // GROUP) * HEAD_DIM
    names = ["uA", "gA", "qB", "kB", "vB", "gB", "qC", "kC", "vC", "gC", "mA", "mB", "mC"]
    widths = [d_model, d_model, d_model, kvw, kvw, d_model, d_model, kvw, kvw, d_model, d_model, d_model, d_model]
    out, o = [], 0
    for n, w in zip(names, widths):
        out.append((n, o, w))
        o += w
    return out


def _matmul(a, b, *, ta=False, tb=False, out_dtype=F32, bm, bn, bk, name, n_outer=False, a_lead=None, b_lead=None):
    a_shape = a.shape if a_lead is None else a.shape[1:]
    b_shape = b.shape if b_lead is None else b.shape[1:]
    (kdim, m) = a_shape if ta else a_shape[::-1]
    (n, kdim2) = b_shape if tb else b_shape[::-1]
    assert kdim == kdim2 and m % bm == 0 and n % bn == 0 and kdim % bk == 0, (a.shape, b.shape, bm, bn, bk)
    nk = kdim // bk
    dims = (((0 if ta else 1,), (1 if tb else 0,)), ((), ()))

    def ij(f):
        return (lambda j, i, k: f(i, j, k)) if n_outer else f

    def body(a_ref, b_ref, o_ref, *scratch):
        r = lax.dot_general(a_ref[...].astype(BF16), b_ref[...].astype(BF16), dims, preferred_element_type=F32)
        if nk == 1:
            o_ref[...] = r.astype(out_dtype)
        else:
            acc = scratch[0]
            k = pl.program_id(2)

            @pl.when(k == 0)
            def _():
                acc[...] = r

            @pl.when(k > 0)
            def _():
                acc[...] += r

            @pl.when(k == nk - 1)
            def _():
                o_ref[...] = acc[...].astype(out_dtype)

    def spec(shape, f, lead):
        f = ij(f)
        if lead is None:
            return pl.BlockSpec(shape, f)
        return pl.BlockSpec((None,) + shape, lambda *g: (lead,) + f(*g))

    a_spec = spec((bk, bm), lambda i, j, k: (k, i), a_lead) if ta else spec((bm, bk), lambda i, j, k: (i, k), a_lead)
    b_spec = spec((bn, bk), lambda i, j, k: (j, k), b_lead) if tb else spec((bk, bn), lambda i, j, k: (k, j), b_lead)
    return pl.pallas_call(
        body, name=name, grid=(n // bn, m // bm, nk) if n_outer else (m // bm, n // bn, nk),
        in_specs=[a_spec, b_spec], out_specs=pl.BlockSpec((bm, bn), ij(lambda i, j, k: (i, j))),
        out_shape=jax.ShapeDtypeStruct((m, n), out_dtype),
        scratch_shapes=[pltpu.VMEM((bm, bn), F32)] if nk > 1 else [],
        compiler_params=_cparams(("parallel", "parallel", "arbitrary"), VMEM_LIMIT),
    )(a, b)


def _mod_fwd(c16, w_mod, b_mod):
    d3 = w_mod.shape[1]

    def body(c_ref, w_ref, b_ref, o_ref):
        o_ref[...] = _dot(_silu(c_ref[...]).astype(BF16), w_ref[...]) + b_ref[...]

    return pl.pallas_call(body, name="mod_fwd", out_shape=jax.ShapeDtypeStruct((MOD_ROWS, d3), F32),
                          compiler_params=_cparams(None, VMEM_LIMIT))(c16, w_mod, b_mod)


def _mod_bwd(c16, dmod16, w_mod):
    d, d3 = w_mod.shape

    def body(c_ref, g_ref, w_ref, dw_ref, db_ref, dc_ref):
        c = c_ref[...]
        g = g_ref[...]
        gb = g.astype(BF16)
        dw_ref[...] = _dot_tn(_silu(c).astype(BF16), gb)
        db_ref[...] = _colsum(g)
        dc_ref[...] = _dot_nt(gb, w_ref[...]) * _dsilu(c)

    return pl.pallas_call(
        body, name="mod_bwd",
        out_shape=(jax.ShapeDtypeStruct((d, d3), F32), jax.ShapeDtypeStruct((1, d3), F32),
                   jax.ShapeDtypeStruct((MOD_ROWS, d), F32)),
        compiler_params=_cparams(None, VMEM_LIMIT))(c16, dmod16, w_mod)


def _row_kind(t, lb):
    return jnp.where(t >= lb, 1, 0)


def _norm_mod_fwd(x3, g, modsel, ctx_len):
    b, t, d = x3.shape
    bt = ROW_BLOCK
    lb = ctx_len // bt

    def body(x_ref, g_ref, m_ref, h_ref):
        x = x_ref[0]
        rstd = lax.rsqrt(jnp.mean(x * x, axis=-1, keepdims=True) + EPS)
        y = x * rstd * g_ref[...]
        h_ref[0] = (y * (1.0 + m_ref[0, 0, 1:2, :]) + m_ref[0, 0, 0:1, :]).astype(BF16)

    return pl.pallas_call(
        body, name="norm_mod_fwd", grid=(b, t // bt),
        in_specs=[pl.BlockSpec((1, bt, d), lambda i, j: (i, j, 0)),
                  pl.BlockSpec((1, d), lambda i, j: (0, 0)),
                  pl.BlockSpec((1, 1, 8, d), lambda i, j: (i, _row_kind(j, lb), 0, 0))],
        out_specs=pl.BlockSpec((1, bt, d), lambda i, j: (i, j, 0)),
        out_shape=jax.ShapeDtypeStruct((b, t, d), BF16),
        compiler_params=_cparams(("parallel", "arbitrary")),
    )(x3, g, modsel)


def _norm_mod_bwd(dh3, x3, g, modsel, dres3, ctx_len):
    b, t, d = x3.shape
    bt = ROW_BLOCK
    lb = ctx_len // bt

    def body(dh_ref, x_ref, g_ref, m_ref, dres_ref, dx_ref, acc_ref):
        j = pl.program_id(1)
        x = x_ref[0]
        dh = dh_ref[0]
        g_row = g_ref[...]
        rstd = lax.rsqrt(jnp.mean(x * x, axis=-1, keepdims=True) + EPS)
        xhat = x * rstd
        dhpre = dh * (1.0 + m_ref[0, 0, 1:2, :])
        dxhat = dhpre * g_row
        dx = rstd * (dxhat - xhat * jnp.mean(dxhat * xhat, axis=-1, keepdims=True))
        dx_ref[0] = dx + dres_ref[0]

        @pl.when((j == 0) | (j == lb))
        def _():
            acc_ref[...] = jnp.zeros_like(acc_ref)

        acc_ref[0, 0, 0:1, :] += _colsum(dh)
        acc_ref[0, 0, 1:2, :] += _colsum(dh * (xhat * g_row))
        acc_ref[0, 0, 2:3, :] += _colsum(dhpre * xhat)

    blk = pl.BlockSpec((1, bt, d), lambda i, j: (i, j, 0))
    return pl.pallas_call(
        body, name="norm_mod_bwd", grid=(b, t // bt),
        in_specs=[blk, blk, pl.BlockSpec((1, d), lambda i, j: (0, 0)),
                  pl.BlockSpec((1, 1, 8, d), lambda i, j: (i, _row_kind(j, lb), 0, 0)), blk],
        out_specs=(blk, pl.BlockSpec((1, 1, 8, d), lambda i, j: (i, _row_kind(j, lb), 0, 0))),
        out_shape=(jax.ShapeDtypeStruct((b, t, d), F32), jax.ShapeDtypeStruct((b, 2, 8, d), F32)),
        compiler_params=_cparams(("parallel", "arbitrary")),
    )(dh3, x3, g, modsel, dres3)


def _shifted_rows(ref, c, off, ctx_len, total):
    ct = LRU_CHUNK
    r0 = pl.multiple_of(c * ct, ct)
    x0 = ref[pl.ds(r0, ct), :]
    row = lax.broadcasted_iota(jnp.int32, x0.shape, 0)
    if off < 0:
        k = -off
        has = jnp.logical_and(r0 != 0, r0 != ctx_len)
        p0 = pl.multiple_of(jnp.maximum(r0 - 8, 0), 8)
        edge = jnp.where(has, ref[pl.ds(p0, 8), :], 0.0)
        out = pltpu.roll(x0, k, 0)
        for j in range(k):
            out = jnp.where(row == j, edge[8 - k + j:8 - k + j + 1, :], out)
    else:
        k = off
        has = jnp.logical_and(r0 + ct != ctx_len, r0 + ct != total)
        n0 = pl.multiple_of(jnp.minimum(r0 + ct, total - 8), 8)
        edge = jnp.where(has, ref[pl.ds(n0, 8), :], 0.0)
        out = pltpu.roll(x0, ct - k, 0)
        for j in range(k):
            out = jnp.where(row == ct - k + j, edge[j:j + 1, :], out)
    return out


def _chunk_scan(a, b, reverse):
    n = a.shape[0]
    row = lax.broadcasted_iota(jnp.int32, a.shape, 0)
    s = 1
    while s < n:
        if reverse:
            a_s, b_s, ok = pltpu.roll(a, n - s, 0), pltpu.roll(b, n - s, 0), row < n - s
        else:
            a_s, b_s, ok = pltpu.roll(a, s, 0), pltpu.roll(b, s, 0), row >= s
        b = jnp.where(ok, a * b_s + b, b)
        a = jnp.where(ok, a * a_s, a)
        s *= 2
    return a, b


def _loop_chunks(n, body, init):
    assert n % LRU_UNROLL == 0

    def group(s2, carry):
        for u in range(LRU_UNROLL):
            carry = body(LRU_UNROLL * s2 + u, carry)
        return carry

    return lax.fori_loop(0, n // LRU_UNROLL, group, init)


def _lru_order(d, s, n_ctx, n_all):
    if d == 0:
        return s
    return jnp.where(s < n_ctx, n_ctx - 1 - s, n_all - 1 - (s - n_ctx))


def _lru_gates(u, wa, ba, wx, bx, sp):
    ub = u.astype(BF16)
    r = _sigmoid(_dot(ub, wa) + ba)
    i = _sigmoid(_dot(ub, wx) + bx)
    log_a = (-LRU_C * sp) * r
    a = jnp.exp(log_a)
    sf = jnp.sqrt(_one_minus_square(log_a, a))
    return ub, r, i, a, sf


def _lru_specs(t, n_lane_blocks_offset):
    ln = LRU_LANES
    return [
        pl.BlockSpec((4, ln), lambda i, j: (0, j)),
        pl.BlockSpec((1, ln), lambda i, j: (0, j)),
        pl.BlockSpec((2, 1, ln, ln), lambda i, j: (0, j, 0, 0)),
        pl.BlockSpec((2, ln), lambda i, j: (0, j)),
        pl.BlockSpec((2, 1, ln, ln), lambda i, j: (0, j, 0, 0)),
        pl.BlockSpec((2, ln), lambda i, j: (0, j)),
        pl.BlockSpec((2, ln), lambda i, j: (0, j)),
    ]


def _widen(src_ref, dst, n_chunks):
    ct = LRU_CHUNK

    def copy(c, _):
        r0 = pl.multiple_of(c * ct, ct)
        dst[pl.ds(r0, ct), :] = src_ref[0, pl.ds(r0, ct), :].astype(F32)
        return 0

    lax.fori_loop(0, n_chunks, copy, 0)


def _lru_conv(ua_ref, cw_ref, cb_ref, u_s, ctx_len, total):
    ct = LRU_CHUNK

    def conv(c, _):
        r0 = pl.multiple_of(c * ct, ct)
        u = (cw_ref[0:1, :] * _shifted_rows(ua_ref, c, -2, ctx_len, total)
             + cw_ref[1:2, :] * _shifted_rows(ua_ref, c, -1, ctx_len, total)
             + cw_ref[2:3, :] * ua_ref[pl.ds(r0, ct), :]
             + cw_ref[3:4, :] * _shifted_rows(ua_ref, c, 1, ctx_len, total) + cb_ref[...])
        u_s[pl.ds(r0, ct), :] = u
        return 0

    lax.fori_loop(0, total // ct, conv, 0)


def _lru_fwd(proj3, col0, conv_w, conv_b, wa_bd, ba, wx_bd, bx, lam, ctx_len):
    b, t, _ = proj3.shape
    d = conv_w.shape[1]
    ln, ct = LRU_LANES, LRU_CHUNK
    n_all, n_ctx = t // ct, ctx_len // ct
    cb0 = col0 // ln

    def body(ua_ref, cw_ref, cb_ref, wa_ref, ba_ref, wx_ref, bx_ref, lam_ref, y_ref, u_s, h1_s, ua):
        _widen(ua_ref, ua, n_all)
        _lru_conv(ua, cw_ref, cb_ref, u_s, ctx_len, t)
        par = [(_softplus(-lam_ref[dr:dr + 1, :]), wa_ref[dr, 0], wx_ref[dr, 0], ba_ref[dr:dr + 1, :], bx_ref[dr:dr + 1, :])
               for dr in (0, 1)]

        def step(s, carry):
            out = []
            for dr in (0, 1):
                sp, wa, wx, ba_row, bx_row = par[dr]
                c = _lru_order(dr, s, n_ctx, n_all)
                r0 = pl.multiple_of(c * ct, ct)
                u = u_s[pl.ds(r0, ct), :]
                _, _, i, a, sf = _lru_gates(u, wa, ba_row, wx, bx_row, sp)
                aa, h0 = _chunk_scan(a, sf * (i * u), reverse=(dr == 1))
                h = h0 + aa * carry[dr]
                if dr == 0:
                    y_ref[0, pl.ds(r0, ct), :] = h
                    out.append(h[ct - 1:ct, :])
                else:
                    h1_s[pl.ds(r0, ct), :] = h
                    out.append(h[0:1, :])
            return tuple(out)

        zrow = jnp.zeros((1, ln), F32)
        _loop_chunks(n_all, step, (zrow, zrow))

        def add(c, _):
            r0 = pl.multiple_of(c * ct, ct)
            y_ref[0, pl.ds(r0, ct), :] += h1_s[pl.ds(r0, ct), :]
            return 0

        lax.fori_loop(0, n_all, add, 0)

    return pl.pallas_call(
        body, name="lru_fwd", grid=(b, d // ln),
        in_specs=[pl.BlockSpec((1, t, ln), lambda i, j: (i, 0, cb0 + j))] + _lru_specs(t, cb0),
        out_specs=pl.BlockSpec((1, t, ln), lambda i, j: (i, 0, j)),
        out_shape=jax.ShapeDtypeStruct((b, t, d), F32),
        scratch_shapes=[pltpu.VMEM((t, ln), F32)] * 3,
        compiler_params=_cparams(("parallel", "parallel"), VMEM_LIMIT),
    )(proj3, conv_w, conv_b, wa_bd, ba, wx_bd, bx, lam)


def _lru_bwd(dproj3, proj3, col0, dy3, conv_w, conv_b, wa_bd, ba, wx_bd, bx, lam, ctx_len):
    b, t, _ = proj3.shape
    d = conv_w.shape[1]
    ln, ct = LRU_LANES, LRU_CHUNK
    n_all, n_ctx = t // ct, ctx_len // ct
    cb0 = col0 // ln

    def body(dproj_hbm, ua_ref, dy_ref, cw_ref, cb_ref, wa_ref, ba_ref, wx_ref, bx_ref, lam_ref,
             dua_ref, vec_ref, dwa_ref, dwx_ref, u_s, du_s, ua, h_s, a_s, sf_s, i_s, r_s):
        del dproj_hbm
        _widen(ua_ref, ua, n_all)
        _lru_conv(ua, cw_ref, cb_ref, u_s, ctx_len, t)
        du_s[...] = jnp.zeros_like(du_s)
        vec_ref[...] = jnp.zeros_like(vec_ref)
        par = [(_softplus(-lam_ref[dr:dr + 1, :]), wa_ref[dr, 0], wx_ref[dr, 0], ba_ref[dr:dr + 1, :], bx_ref[dr:dr + 1, :])
               for dr in (0, 1)]

        def fwd_one(dr, s, carry):
            sp, wa, wx, ba_row, bx_row = par[dr]
            c = _lru_order(dr, s, n_ctx, n_all)
            r0 = pl.multiple_of(c * ct, ct)
            u = u_s[pl.ds(r0, ct), :]
            _, r, i, a, sf = _lru_gates(u, wa, ba_row, wx, bx_row, sp)
            aa, h0 = _chunk_scan(a, sf * (i * u), reverse=(dr == 1))
            h = h0 + aa * carry
            h_s[dr, pl.ds(r0, ct), :] = h
            a_s[dr, pl.ds(r0, ct), :] = a
            sf_s[dr, pl.ds(r0, ct), :] = sf
            i_s[dr, pl.ds(r0, ct), :] = i
            r_s[dr, pl.ds(r0, ct), :] = r
            return h[ct - 1:ct, :] if dr == 0 else h[0:1, :]

        def bwd_one(dr, sr, carry):
            sp, wa, wx, _, _ = par[dr]
            gc, vacc = carry
            hs = h_s.at[dr]
            c = _lru_order(dr, n_all - 1 - sr, n_ctx, n_all)
            r0 = pl.multiple_of(c * ct, ct)
            u = u_s[pl.ds(r0, ct), :]
            h = hs[pl.ds(r0, ct), :]
            dy = dy_ref[0, pl.ds(r0, ct), :]
            ub = u.astype(BF16)
            r, i = r_s[dr, pl.ds(r0, ct), :], i_s[dr, pl.ds(r0, ct), :]
            a, sf = a_s[dr, pl.ds(r0, ct), :], sf_s[dr, pl.ds(r0, ct), :]
            row = lax.broadcasted_iota(jnp.int32, a.shape, 0)
            if dr == 0:
                alpha = jnp.where(row == ct - 1, 1.0, pltpu.roll(a, ct - 1, 0))
                aa, g0 = _chunk_scan(alpha, dy, reverse=True)
                g = g0 + aa * gc
                gc_new = a[0:1, :] * g[0:1, :]
                p0 = pl.multiple_of(jnp.maximum(r0 - 8, 0), 8)
                edge = jnp.where(r0 != 0, hs[pl.ds(p0, 8), :], 0.0)[7:8, :]
                h_prev = jnp.where(row == 0, edge, pltpu.roll(h, 1, 0))
            else:
                alpha = jnp.where(row == 0, 1.0, pltpu.roll(a, 1, 0))
                aa, g0 = _chunk_scan(alpha, dy, reverse=False)
                g = g0 + aa * gc
                gc_new = a[ct - 1:ct, :] * g[ct - 1:ct, :]
                r_end = r0 + ct
                n0 = pl.multiple_of(jnp.where(r_end == t, 0, jnp.minimum(r_end, t - 8)), 8)
                edge = jnp.where(r_end != ctx_len, hs[pl.ds(n0, 8), :], 0.0)[0:1, :]
                h_prev = jnp.where(row == ct - 1, edge, pltpu.roll(h, ct - 1, 0))
            da = g * h_prev
            iu = i * u
            diu = g * sf
            dlog_a = da * a - (g * iu) * (a * a) / sf
            dpre_r = (dlog_a * (-LRU_C * sp)) * (r * (1.0 - r))
            dpre_i = (diu * u) * (i * (1.0 - i))
            dpr_b, dpi_b = dpre_r.astype(BF16), dpre_i.astype(BF16)
            du = diu * i + _dot_nt(dpr_b, wa) + _dot_nt(dpi_b, wx)
            du_s[pl.ds(r0, ct), :] += du
            dwa_ref[0, dr, 0] += _dot_tn(ub, dpr_b)
            dwx_ref[0, dr, 0] += _dot_tn(ub, dpi_b)
            vacc = (vacc[0] + _colsum(dpre_r), vacc[1] + _colsum(dpre_i), vacc[2] + _colsum(dlog_a * (-LRU_C * r)))
            return gc_new, vacc

        zrow = jnp.zeros((1, ln), F32)
        _loop_chunks(n_all, lambda s, cr: tuple(fwd_one(dr, s, cr[dr]) for dr in (0, 1)), (zrow, zrow))
        dwa_ref[...] = jnp.zeros_like(dwa_ref)
        dwx_ref[...] = jnp.zeros_like(dwx_ref)
        init = (zrow, (zrow, zrow, zrow))
        res = _loop_chunks(n_all, lambda sr, cr: tuple(bwd_one(dr, sr, cr[dr]) for dr in (0, 1)), (init, init))
        for dr in (0, 1):
            vacc = res[dr][1]
            vec_ref[0, 5 + dr:6 + dr, :] = vacc[0]
            vec_ref[0, 7 + dr:8 + dr, :] = vacc[1]
            vec_ref[0, 9 + dr:10 + dr, :] = vacc[2]

        def conv_bwd(c, acc):
            r0 = pl.multiple_of(c * ct, ct)
            du = du_s[pl.ds(r0, ct), :]
            dua = (cw_ref[0:1, :] * _shifted_rows(du_s, c, 2, ctx_len, t)
                   + cw_ref[1:2, :] * _shifted_rows(du_s, c, 1, ctx_len, t)
                   + cw_ref[2:3, :] * du
                   + cw_ref[3:4, :] * _shifted_rows(du_s, c, -1, ctx_len, t))
            dua_ref[0, pl.ds(r0, ct), :] = dua.astype(BF16)
            return (acc[0] + _colsum(du * _shifted_rows(ua, c, -2, ctx_len, t)),
                    acc[1] + _colsum(du * _shifted_rows(ua, c, -1, ctx_len, t)),
                    acc[2] + _colsum(du * ua[pl.ds(r0, ct), :]),
                    acc[3] + _colsum(du * _shifted_rows(ua, c, 1, ctx_len, t)),
                    acc[4] + _colsum(du))

        zrow = jnp.zeros((1, ln), F32)
        acc = lax.fori_loop(0, n_all, conv_bwd, (zrow,) * 5)
        for k in range(5):
            vec_ref[0, k:k + 1, :] = acc[k]

    ng = d // ln
    return pl.pallas_call(
        body, name="lru_bwd", grid=(b, ng),
        in_specs=[pl.BlockSpec(memory_space=pl.ANY),
                  pl.BlockSpec((1, t, ln), lambda i, j: (i, 0, cb0 + j)),
                  pl.BlockSpec((1, t, ln), lambda i, j: (i, 0, j))] + _lru_specs(t, cb0),
        out_specs=(pl.BlockSpec((1, t, ln), lambda i, j: (i, 0, cb0 + j)),
                   pl.BlockSpec((1, 16, ln), lambda i, j: (i, 0, j)),
                   pl.BlockSpec((1, 2, 1, ln, ln), lambda i, j: (i, 0, j, 0, 0)),
                   pl.BlockSpec((1, 2, 1, ln, ln), lambda i, j: (i, 0, j, 0, 0))),
        out_shape=(jax.ShapeDtypeStruct(dproj3.shape, dproj3.dtype),
                   jax.ShapeDtypeStruct((b, 16, d), F32),
                   jax.ShapeDtypeStruct((b, 2, ng, ln, ln), F32),
                   jax.ShapeDtypeStruct((b, 2, ng, ln, ln), F32)),
        scratch_shapes=[pltpu.VMEM((t, ln), F32)] * 3 + [pltpu.VMEM((2, t, ln), F32)] * 5,
        input_output_aliases={0: 0},
        compiler_params=_cparams(("parallel", "parallel"), VMEM_LIMIT),
    )(dproj3, proj3, dy3, conv_w, conv_b, wa_bd, ba, wx_bd, bx, lam)


def _rope_tables(ctx_len, seq):
    p = HEAD_DIM // 4
    inv = ROPE_THETA ** (-jnp.arange(p, dtype=F32) / p)
    tok = jnp.arange(seq)
    ang_r = (tok // GRID_W)[:, None] * inv
    ang_c = (tok % GRID_W)[:, None] * inv
    cos = jnp.concatenate([jnp.cos(ang_r)] * 2 + [jnp.cos(ang_c)] * 2, axis=1)
    sin = jnp.concatenate([-jnp.sin(ang_r), jnp.sin(ang_r), -jnp.sin(ang_c), jnp.sin(ang_c)], axis=1)
    cos = jnp.concatenate([jnp.ones((ctx_len, HEAD_DIM), F32), cos], axis=0)
    sin = jnp.concatenate([jnp.zeros((ctx_len, HEAD_DIM), F32), sin], axis=0)
    return cos, sin


def _swap_halves(v):
    lane = lax.broadcasted_iota(jnp.int32, v.shape, 1)
    return jnp.where((lane & 63) < 32, pltpu.roll(v, 96, 1), pltpu.roll(v, 32, 1))


def _head_rstd(v):
    return lax.rsqrt(jnp.mean(v * v, axis=-1, keepdims=True) + EPS)


QKV_BLOCK = GROUP * HEAD_DIM
PREP_ROWS = (2176, 256)


def _prep_fwd(proj3, qcol, kvcol, d, cos, sin, gq, gk, use_norm):
    b, t, _ = proj3.shape
    bt, wb = _pick(t, PREP_ROWS), QKV_BLOCK
    nqb = d // wb
    assert qcol % wb == 0 and kvcol % wb == 0 and d // HEAD_DIM // GROUP == 2
    qb0, kvb = qcol // wb, kvcol // wb

    def body(p_ref, cos_ref, sin_ref, gq_ref, gk_ref, o_ref):
        s = pl.program_id(2)
        c, sn = cos_ref[...], sin_ref[...]

        def rope(v):
            return v * c + _swap_halves(v) * sn

        @pl.when(s < nqb)
        def _():
            for hh in range(GROUP):
                v = p_ref[0, :, hh * HEAD_DIM:(hh + 1) * HEAD_DIM].astype(F32)
                if use_norm:
                    v = v * _head_rstd(v) * gq_ref[...]
                o_ref[0, :, hh * HEAD_DIM:(hh + 1) * HEAD_DIM] = rope(v).astype(BF16)

        @pl.when(s == nqb)
        def _():
            for hh in range(2):
                v = p_ref[0, :, hh * HEAD_DIM:(hh + 1) * HEAD_DIM].astype(F32)
                if use_norm:
                    v = v * _head_rstd(v) * gk_ref[...]
                o_ref[0, :, hh * HEAD_DIM:(hh + 1) * HEAD_DIM] = rope(v).astype(BF16)
            o_ref[0, :, 2 * HEAD_DIM:] = p_ref[0, :, 2 * HEAD_DIM:]

    return pl.pallas_call(
        body, name="prep_fwd_norm" if use_norm else "prep_fwd", grid=(b, t // bt, nqb + 1),
        in_specs=[pl.BlockSpec((1, bt, wb), lambda i, j, s: (i, j, jnp.where(s < nqb, qb0 + s, kvb))),
                  pl.BlockSpec((bt, HEAD_DIM), lambda i, j, s: (j, 0)),
                  pl.BlockSpec((bt, HEAD_DIM), lambda i, j, s: (j, 0)),
                  pl.BlockSpec((1, HEAD_DIM), lambda i, j, s: (0, 0)),
                  pl.BlockSpec((1, HEAD_DIM), lambda i, j, s: (0, 0))],
        out_specs=pl.BlockSpec((1, bt, wb), lambda i, j, s: (i, j, s)),
        out_shape=jax.ShapeDtypeStruct((b, t, d + wb), BF16),
        compiler_params=_cparams(("parallel", "parallel", "arbitrary"), VMEM_LIMIT),
    )(proj3, cos, sin, gq, gk)


def _prep_bwd(dproj3, dq3, dkt, dvt, proj3, qcol, kvcol, d, cos, sin, gq, gk, use_norm):
    b, t, _ = proj3.shape
    bt, wb = _pick(t, PREP_ROWS), QKV_BLOCK
    nqb = d // wb
    qb0, kvb = qcol // wb, kvcol // wb
    kvh = dkt.shape[1]

    def body(dproj_hbm, dq_ref, dkt_ref, dvt_ref, p_ref, cos_ref, sin_ref, gq_ref, gk_ref, o_ref, gacc_ref):
        del dproj_hbm
        j, s = pl.program_id(1), pl.program_id(2)
        c, sn = cos_ref[...], sin_ref[...]

        @pl.when((j == 0) & (s == 0))
        def _():
            gacc_ref[...] = jnp.zeros_like(gacc_ref)

        def unrope(dv):
            return dv * c + _swap_halves(dv * sn)

        def head_bwd(dyv, xv, g_ref, acc_row):
            dyv = unrope(dyv)
            if not use_norm:
                return dyv
            rstd = _head_rstd(xv)
            xhat = xv * rstd
            gacc_ref[0, acc_row:acc_row + 1, :] += _colsum(dyv * xhat)
            dxhat = dyv * g_ref[...]
            return rstd * (dxhat - xhat * jnp.mean(dxhat * xhat, axis=-1, keepdims=True))

        @pl.when(s < nqb)
        def _():
            for hh in range(GROUP):
                sl = slice(hh * HEAD_DIM, (hh + 1) * HEAD_DIM)
                o_ref[0, :, sl] = head_bwd(dq_ref[0, :, sl], p_ref[0, :, sl].astype(F32), gq_ref, 0).astype(BF16)

        @pl.when(s == nqb)
        def _():
            for hh in range(kvh):
                sl = slice(hh * HEAD_DIM, (hh + 1) * HEAD_DIM)
                o_ref[0, :, sl] = head_bwd(dkt_ref[0, hh].T, p_ref[0, :, sl].astype(F32), gk_ref, 1).astype(BF16)
                sv = slice((kvh + hh) * HEAD_DIM, (kvh + hh + 1) * HEAD_DIM)
                o_ref[0, :, sv] = dvt_ref[0, hh].T.astype(BF16)

    col = lambda i, j, s: (i, j, jnp.where(s < nqb, qb0 + s, kvb))
    return pl.pallas_call(
        body, name="prep_bwd_norm" if use_norm else "prep_bwd", grid=(b, t // bt, nqb + 1),
        in_specs=[pl.BlockSpec(memory_space=pl.ANY),
                  pl.BlockSpec((1, bt, wb), lambda i, j, s: (i, j, jnp.minimum(s, nqb - 1))),
                  pl.BlockSpec((1, kvh, HEAD_DIM, bt), lambda i, j, s: (i, 0, 0, j)),
                  pl.BlockSpec((1, kvh, HEAD_DIM, bt), lambda i, j, s: (i, 0, 0, j)),
                  pl.BlockSpec((1, bt, wb), col),
                  pl.BlockSpec((bt, HEAD_DIM), lambda i, j, s: (j, 0)),
                  pl.BlockSpec((bt, HEAD_DIM), lambda i, j, s: (j, 0)),
                  pl.BlockSpec((1, HEAD_DIM), lambda i, j, s: (0, 0)),
                  pl.BlockSpec((1, HEAD_DIM), lambda i, j, s: (0, 0))],
        out_specs=(pl.BlockSpec((1, bt, wb), col), pl.BlockSpec((1, 8, HEAD_DIM), lambda i, j, s: (i, 0, 0))),
        out_shape=(jax.ShapeDtypeStruct(dproj3.shape, dproj3.dtype), jax.ShapeDtypeStruct((b, 8, HEAD_DIM), F32)),
        input_output_aliases={0: 0},
        compiler_params=_cparams(("parallel", "arbitrary", "arbitrary"), VMEM_LIMIT),
    )(dproj3, dq3, dkt, dvt, proj3, cos, sin, gq, gk)


def _stack_heads(ref, dtype=None):
    parts = [ref[0, :, g * HEAD_DIM:(g + 1) * HEAD_DIM] for g in range(GROUP)]
    v = jnp.concatenate(parts, axis=0)
    return v if dtype is None else v.astype(dtype)


def _unstack_heads(ref, v, bq):
    for g in range(GROUP):
        ref[0, :, g * HEAD_DIM:(g + 1) * HEAD_DIM] = v[g * bq:(g + 1) * bq, :]


def _attn_specs(t, d, bq):
    kvh = d // HEAD_DIM // GROUP
    kc0 = d // HEAD_DIM
    q_spec = pl.BlockSpec((1, bq, QKV_BLOCK), lambda i, h, j: (i, j, h))
    k_spec = pl.BlockSpec((1, t, HEAD_DIM), lambda i, h, j: (i, 0, kc0 + h))
    v_spec = pl.BlockSpec((1, t, HEAD_DIM), lambda i, h, j: (i, 0, kc0 + kvh + h))
    lse_spec = pl.BlockSpec((1, GROUP, bq, HEAD_DIM), lambda i, h, j: (i, h, j, 0))
    kt_spec = pl.BlockSpec((1, 1, HEAD_DIM, t), lambda i, h, j: (i, h, 0, 0))
    return kvh, q_spec, k_spec, v_spec, lse_spec, kt_spec


SCALE = HEAD_DIM ** -0.5


def _attn_dense_fwd(qkv, d, ctx_len):
    b, t, _ = qkv.shape
    bq = DENSE_FWD_BQ
    lq = ctx_len // bq
    kvh, q_spec, k_spec, v_spec, lse_spec, _ = _attn_specs(t, d, bq)

    def body(q_ref, k_ref, v_ref, o_ref, lse_ref):
        i = pl.program_id(2)

        def attend(k, v):
            for g in range(GROUP):
                sl = slice(g * HEAD_DIM, (g + 1) * HEAD_DIM)
                s = _dot_nt(q_ref[0, :, sl], k)
                m = jnp.max(s, axis=1, keepdims=True)
                p = jnp.exp2((s - m) * (SCALE * LOG2E))
                l = jnp.sum(p, axis=1, keepdims=True)
                o_ref[0, :, sl] = _dot(p.astype(BF16), v) / l
                lse_ref[0, g] = jnp.broadcast_to(m * SCALE + jnp.log(l), (bq, HEAD_DIM))

        @pl.when(i < lq)
        def _():
            attend(k_ref[0, 0:ctx_len, :], v_ref[0, 0:ctx_len, :])

        @pl.when(i >= lq)
        def _():
            attend(k_ref[0], v_ref[0])

    return pl.pallas_call(
        body, name="attn_dense_fwd", grid=(b, kvh, t // bq),
        in_specs=[q_spec, k_spec, v_spec], out_specs=(q_spec, lse_spec),
        out_shape=(jax.ShapeDtypeStruct((b, t, d), F32), jax.ShapeDtypeStruct((b, kvh * GROUP, t, HEAD_DIM), F32)),
        compiler_params=_cparams(("parallel", "parallel", "arbitrary"), VMEM_LIMIT),
    )(qkv, qkv, qkv)


def _attn_dense_bwd(qkv, o3, do3, lse, d, ctx_len):
    b, t, _ = qkv.shape
    bq = ATT_BQ
    lq = ctx_len // bq
    kvh, q_spec, k_spec, v_spec, lse_spec, kt_spec = _attn_specs(t, d, bq)

    def body(q_ref, k_ref, v_ref, o_ref, do_ref, lse_ref, dq_ref, dkt_ref, dvt_ref):
        i = pl.program_id(2)

        @pl.when(i == 0)
        def _():
            dkt_ref[...] = jnp.zeros_like(dkt_ref)
            dvt_ref[...] = jnp.zeros_like(dvt_ref)

        def run(k, v, width):
            dk_acc = dv_acc = None
            for g in range(GROUP):
                sl = slice(g * HEAD_DIM, (g + 1) * HEAD_DIM)
                q = q_ref[0, :, sl]
                do = do_ref[0, :, sl]
                dd = jnp.sum(do * o_ref[0, :, sl], axis=1, keepdims=True)
                dob = do.astype(BF16)
                p = jnp.exp2(_dot_nt(q, k) * (SCALE * LOG2E) - lse_ref[0, g][:, 0:1] * LOG2E)
                ds = (p * (_dot_nt(dob, v) - dd) * SCALE).astype(BF16)
                dq_ref[0, :, sl] = _dot(ds, k)
                dk_g = _dot(q.astype(F32).T.astype(BF16), ds)
                dv_g = _dot(do.T.astype(BF16), p.astype(BF16))
                dk_acc = dk_g if dk_acc is None else dk_acc + dk_g
                dv_acc = dv_g if dv_acc is None else dv_acc + dv_g
            dkt_ref[0, 0, :, 0:width] += dk_acc
            dvt_ref[0, 0, :, 0:width] += dv_acc

        @pl.when(i < lq)
        def _():
            run(k_ref[0, 0:ctx_len, :], v_ref[0, 0:ctx_len, :], ctx_len)

        @pl.when(i >= lq)
        def _():
            run(k_ref[0], v_ref[0], t)

    return pl.pallas_call(
        body, name="attn_dense_bwd", grid=(b, kvh, t // bq),
        in_specs=[q_spec, k_spec, v_spec, q_spec, q_spec, lse_spec], out_specs=(q_spec, kt_spec, kt_spec),
        out_shape=(jax.ShapeDtypeStruct((b, t, d), F32), jax.ShapeDtypeStruct((b, kvh, HEAD_DIM, t), F32),
                   jax.ShapeDtypeStruct((b, kvh, HEAD_DIM, t), F32)),
        compiler_params=_cparams(("parallel", "parallel", "arbitrary"), VMEM_LIMIT),
    )(qkv, qkv, qkv, o3, do3, lse)


def _sink_column(sink_ref, h, bq):
    rowi = lax.broadcasted_iota(jnp.int32, (GROUP * bq, 1), 0)
    col = jnp.zeros((GROUP * bq, 1), F32)
    for g in range(GROUP):
        col = jnp.where((rowi >= g * bq) & (rowi < (g + 1) * bq), sink_ref[h * GROUP + g], col)
    return col


def _band(i, lq, ctx_len, t, bq):
    n = i - lq
    start = pl.multiple_of(jnp.clip(ctx_len + n * bq - WINDOW, ctx_len, t - WIN_SPAN), WINDOW)
    shape = (GROUP * bq, WIN_SPAN)
    kpos = start - ctx_len + lax.broadcasted_iota(jnp.int32, shape, 1)
    qpos = n * bq + (lax.broadcasted_iota(jnp.int32, shape, 0) & (bq - 1))
    return start, jnp.abs(kpos - qpos) <= WINDOW


def _attn_win_fwd(qkv, sink, d, ctx_len):
    b, t, _ = qkv.shape
    bq = WIN_BQ
    rows = GROUP * bq
    lq = ctx_len // bq
    kvh, q_spec, k_spec, v_spec, lse_spec, _ = _attn_specs(t, d, bq)

    def body(sink_ref, q_ref, k_ref, v_ref, o_ref, lse_ref):
        h, i = pl.program_id(1), pl.program_id(2)
        q4 = _stack_heads(q_ref)
        sink_col = _sink_column(sink_ref, h, bq)
        sc = _dot_nt(q4, k_ref[0, 0:ctx_len, :]) * SCALE
        mc = jnp.maximum(jnp.max(sc, axis=1, keepdims=True), sink_col)

        def finish(m, l, acc):
            _unstack_heads(o_ref, acc / l, bq)
            lse_ref[0] = jnp.broadcast_to(m + jnp.log(l), (rows, HEAD_DIM)).reshape(GROUP, bq, HEAD_DIM)

        @pl.when(i < lq)
        def _():
            pc = jnp.exp(sc - mc)
            l = jnp.sum(pc, axis=1, keepdims=True) + jnp.exp(sink_col - mc)
            finish(mc, l, _dot(pc.astype(BF16), v_ref[0, 0:ctx_len, :]))

        @pl.when(i >= lq)
        def _():
            start, ok = _band(i, lq, ctx_len, t, bq)
            sb = jnp.where(ok, _dot_nt(q4, k_ref[0, pl.ds(start, WIN_SPAN), :]) * SCALE, NEG_INF)
            m = jnp.maximum(mc, jnp.max(sb, axis=1, keepdims=True))
            pc, pb = jnp.exp(sc - m), jnp.exp(sb - m)
            l = jnp.sum(pc, axis=1, keepdims=True) + jnp.sum(pb, axis=1, keepdims=True) + jnp.exp(sink_col - m)
            acc = _dot(pc.astype(BF16), v_ref[0, 0:ctx_len, :]) + _dot(pb.astype(BF16), v_ref[0, pl.ds(start, WIN_SPAN), :])
            finish(m, l, acc)

    return pl.pallas_call(
        body, name="attn_win_fwd", grid=(b, kvh, t // bq),
        in_specs=[pl.BlockSpec(memory_space=pltpu.SMEM), q_spec, k_spec, v_spec], out_specs=(q_spec, lse_spec),
        out_shape=(jax.ShapeDtypeStruct((b, t, d), F32), jax.ShapeDtypeStruct((b, kvh * GROUP, t, HEAD_DIM), F32)),
        compiler_params=_cparams(("parallel", "parallel", "arbitrary"), VMEM_LIMIT),
    )(sink, qkv, qkv, qkv)


def _attn_win_bwd(qkv, sink, o3, do3, lse, d, ctx_len):
    b, t, _ = qkv.shape
    bq = WIN_BQ
    rows = GROUP * bq
    lq = ctx_len // bq
    kvh, q_spec, k_spec, v_spec, lse_spec, kt_spec = _attn_specs(t, d, bq)

    def body(sink_ref, q_ref, k_ref, v_ref, o_ref, do_ref, lse_ref, dq_ref, dkt_ref, dvt_ref, dsk_ref):
        h, i = pl.program_id(1), pl.program_id(2)

        @pl.when(i == 0)
        def _():
            dkt_ref[...] = jnp.zeros_like(dkt_ref)
            dvt_ref[...] = jnp.zeros_like(dvt_ref)
            dsk_ref[...] = jnp.zeros_like(dsk_ref)

        q4 = _stack_heads(q_ref)
        do4 = _stack_heads(do_ref)
        dd = jnp.sum(do4 * _stack_heads(o_ref), axis=1, keepdims=True)
        lse_col = lse_ref[0].reshape(rows, HEAD_DIM)[:, 0:1]
        do4b = do4.astype(BF16)
        qt = q4.astype(F32).T.astype(BF16)
        dot = do4.T.astype(BF16)

        def part(k, v):
            return _dot_nt(q4, k) * SCALE, _dot_nt(do4b, v)

        def grads(p, dp, k):
            ds = (p * (dp - dd) * SCALE).astype(BF16)
            return _dot(ds, k), _dot(qt, ds), _dot(dot, p.astype(BF16))

        kc = k_ref[0, 0:ctx_len, :]
        sc, dpc = part(kc, v_ref[0, 0:ctx_len, :])
        dq_c, dk_c, dv_c = grads(jnp.exp(sc - lse_col), dpc, kc)
        dkt_ref[0, 0, :, 0:ctx_len] += dk_c
        dvt_ref[0, 0, :, 0:ctx_len] += dv_c
        _unstack_heads(dq_ref, dq_c, bq)

        @pl.when(i >= lq)
        def _():
            start, ok = _band(i, lq, ctx_len, t, bq)
            kb = k_ref[0, pl.ds(start, WIN_SPAN), :]
            sb, dpb = part(kb, v_ref[0, pl.ds(start, WIN_SPAN), :])
            pb = jnp.where(ok, jnp.exp(sb - lse_col), 0.0)
            dq_b, dk_b, dv_b = grads(pb, dpb, kb)
            dkt_ref[0, 0, :, pl.ds(start, WIN_SPAN)] += dk_b
            dvt_ref[0, 0, :, pl.ds(start, WIN_SPAN)] += dv_b
            for g in range(GROUP):
                dq_ref[0, :, g * HEAD_DIM:(g + 1) * HEAD_DIM] += dq_b[g * bq:(g + 1) * bq, :]

        ps = jnp.exp(_sink_column(sink_ref, h, bq) - lse_col) * dd
        for g in range(GROUP):
            val = jnp.sum(ps[g * bq:(g + 1) * bq, :], axis=0, keepdims=True)
            dsk_ref[0, 0, g:g + 1, :] -= jnp.broadcast_to(val, (1, HEAD_DIM))

    return pl.pallas_call(
        body, name="attn_win_bwd", grid=(b, kvh, t // bq),
        in_specs=[pl.BlockSpec(memory_space=pltpu.SMEM), q_spec, k_spec, v_spec, q_spec, q_spec, lse_spec],
        out_specs=(q_spec, kt_spec, kt_spec, pl.BlockSpec((1, 1, 8, HEAD_DIM), lambda i, h, j: (i, h, 0, 0))),
        out_shape=(jax.ShapeDtypeStruct((b, t, d), F32), jax.ShapeDtypeStruct((b, kvh, HEAD_DIM, t), F32),
                   jax.ShapeDtypeStruct((b, kvh, HEAD_DIM, t), F32), jax.ShapeDtypeStruct((b, kvh, 8, HEAD_DIM), F32)),
        compiler_params=_cparams(("parallel", "parallel", "arbitrary"), VMEM_LIMIT),
    )(sink, qkv, qkv, qkv, o3, do3, lse)


MERGE_BWD_ROWS = 256
MERGE_BWD_VMEM = 60 * 1024 * 1024


def _resident(shape):
    return pl.BlockSpec(shape, lambda *_: (0,) * len(shape), pipeline_mode=pl.Buffered(1))


def _merge_fwd(x3, ya, yb, yc, proj3, w_br, w_out, modsel, ctx_len):
    b, t, d = x3.shape
    bt = ROW_BLOCK
    lb = ctx_len // bt

    def body(x_ref, ya_ref, yb_ref, yc_ref, gm_ref, wbr_ref, wo_ref, m_ref, xn_ref, out_ref):
        mix = jnp.zeros((bt, d), F32)
        for n, y_ref in enumerate((ya_ref, yb_ref, yc_ref)):
            z = (y_ref[0] * _silu(gm_ref[0, :, n * d:(n + 1) * d].astype(F32))).astype(BF16)
            mix = mix + _sigmoid(gm_ref[0, :, (3 + n) * d:(4 + n) * d].astype(F32)) * _dot(z, wbr_ref[n])
        o = _dot(mix.astype(BF16), wo_ref[...])
        out_ref[0] = o
        xn_ref[0] = x_ref[0] + m_ref[0, 0, 2:3, :] * o

    blk = pl.BlockSpec((1, bt, d), lambda i, j: (i, j, 0))
    return pl.pallas_call(
        body, name="merge_fwd", grid=(b, t // bt),
        in_specs=[blk, blk, blk, blk, pl.BlockSpec((1, bt, 6 * d), lambda i, j: (i, j, 0)),
                  _resident((3, d, d)), _resident((d, d)),
                  pl.BlockSpec((1, 1, 8, d), lambda i, j: (i, _row_kind(j, lb), 0, 0))],
        out_specs=(blk, blk),
        out_shape=(jax.ShapeDtypeStruct((b, t, d), F32), jax.ShapeDtypeStruct((b, t, d), F32)),
        compiler_params=_cparams(("parallel", "arbitrary"), VMEM_LIMIT),
    )(x3, ya, yb, yc, proj3, w_br, w_out, modsel)


def _merge_bwd(dxn3, out3, ya, yb, yc, proj3, w_br, w_out, modsel, ctx_len):
    b, t, d = dxn3.shape
    n_cols = proj3.shape[2]
    bt = MERGE_BWD_ROWS
    lb = ctx_len // bt

    def body(dxn_ref, out_ref, ya_ref, yb_ref, yc_ref, gm_ref, wbr_ref, wo_ref, m_ref,
             dgm_ref, dya_ref, dyb_ref, dyc_ref, z_ref, dt_ref, mix_ref, dout_ref, gacc_ref):
        j = pl.program_id(1)
        dxn = dxn_ref[0]
        doutb = (m_ref[0, 0, 2:3, :] * dxn).astype(BF16)
        dout_ref[0] = doutb

        @pl.when((j == 0) | (j == lb))
        def _():
            gacc_ref[...] = jnp.zeros_like(gacc_ref)

        gacc_ref[0, 0, 0:1, :] += _colsum(dxn * out_ref[0])
        dmix = _dot_nt(doutb, wo_ref[...])
        mix = jnp.zeros((bt, d), F32)
        for n, (y_ref, dy_ref) in enumerate(((ya_ref, dya_ref), (yb_ref, dyb_ref), (yc_ref, dyc_ref))):
            g = gm_ref[0, :, n * d:(n + 1) * d].astype(F32)
            y = y_ref[0]
            sig_g = _sigmoid(g)
            silu_g = g * sig_g
            z = (y * silu_g).astype(BF16)
            z_ref[n, 0] = z
            tn = _dot(z, wbr_ref[n])
            s = _sigmoid(gm_ref[0, :, (3 + n) * d:(4 + n) * d].astype(F32))
            mix = mix + s * tn
            dgm_ref[0, :, (3 + n) * d:(4 + n) * d] = (dmix * tn * (s * (1.0 - s))).astype(BF16)
            dtb = (dmix * s).astype(BF16)
            dt_ref[n, 0] = dtb
            dz = _dot_nt(dtb, wbr_ref[n])
            dy_ref[0] = dz * silu_g
            dgm_ref[0, :, n * d:(n + 1) * d] = (dz * y * (sig_g * (1.0 + g * (1.0 - sig_g)))).astype(BF16)
        mix_ref[0] = mix.astype(BF16)

    blk = pl.BlockSpec((1, bt, d), lambda i, j: (i, j, 0))
    blk4 = pl.BlockSpec((3, 1, bt, d), lambda i, j: (0, i, j, 0))
    wide = pl.BlockSpec((1, bt, 6 * d), lambda i, j: (i, j, 0))
    return pl.pallas_call(
        body, name="merge_bwd", grid=(b, t // bt),
        in_specs=[blk, blk, blk, blk, blk, wide, _resident((3, d, d)), _resident((d, d)),
                  pl.BlockSpec((1, 1, 8, d), lambda i, j: (i, _row_kind(j, lb), 0, 0))],
        out_specs=(wide, blk, blk, blk, blk4, blk4, blk, blk,
                   pl.BlockSpec((1, 1, 8, d), lambda i, j: (i, _row_kind(j, lb), 0, 0))),
        out_shape=(jax.ShapeDtypeStruct((b, t, n_cols), BF16),
                   jax.ShapeDtypeStruct((b, t, d), F32), jax.ShapeDtypeStruct((b, t, d), F32),
                   jax.ShapeDtypeStruct((b, t, d), F32),
                   jax.ShapeDtypeStruct((3, b, t, d), BF16), jax.ShapeDtypeStruct((3, b, t, d), BF16),
                   jax.ShapeDtypeStruct((b, t, d), BF16), jax.ShapeDtypeStruct((b, t, d), BF16),
                   jax.ShapeDtypeStruct((b, 2, 8, d), F32)),
        compiler_params=_cparams(("parallel", "arbitrary"), MERGE_BWD_VMEM),
    )(dxn3, out3, ya, yb, yc, proj3, w_br, w_out, modsel)


def _final(x3, g, target, ctx_len):
    b, t, d = x3.shape
    bt = ROW_BLOCK
    lb = ctx_len // bt

    def body(x_ref, g_ref, t_ref, dx_ref, loss_ref, dg_ref):
        j = pl.program_id(1)

        @pl.when(j == 0)
        def _():
            loss_ref[...] = jnp.zeros_like(loss_ref)
            dg_ref[...] = jnp.zeros_like(dg_ref)

        @pl.when(j < lb)
        def _():
            dx_ref[...] = jnp.zeros_like(dx_ref)

        @pl.when(j >= lb)
        def _():
            x = x_ref[0]
            g_row = g_ref[...]
            rstd = lax.rsqrt(jnp.mean(x * x, axis=-1, keepdims=True) + EPS)
            xhat = x * rstd
            err = xhat * g_row - t_ref[0]
            loss_ref[...] += (0.5 / d) * jnp.sum(err * err)
            dy = err * (1.0 / d)
            dg_ref[0, 0:1, :] += _colsum(dy * xhat)
            dxhat = dy * g_row
            dx_ref[0] = rstd * (dxhat - xhat * jnp.mean(dxhat * xhat, axis=-1, keepdims=True))

    blk = pl.BlockSpec((1, bt, d), lambda i, j: (i, j, 0))
    return pl.pallas_call(
        body, name="final_loss", grid=(b, t // bt),
        in_specs=[blk, pl.BlockSpec((1, d), lambda i, j: (0, 0)),
                  pl.BlockSpec((1, bt, d), lambda i, j: (i, jnp.maximum(j - lb, 0), 0))],
        out_specs=(blk, pl.BlockSpec((1, 8, HEAD_DIM), lambda i, j: (i, 0, 0)), pl.BlockSpec((1, 8, d), lambda i, j: (i, 0, 0))),
        out_shape=(jax.ShapeDtypeStruct((b, t, d), F32), jax.ShapeDtypeStruct((b, 8, HEAD_DIM), F32),
                   jax.ShapeDtypeStruct((b, 8, d), F32)),
        compiler_params=_cparams(("parallel", "arbitrary")),
    )(x3, g, target)


TOKEN_BLOCKS = (1088, 512, 256, 128)


def _pick(n, options):
    for o in options:
        if n % o == 0:
            return o
    raise ValueError((n, options))


def _block_diag(w):
    per = LRU_LANES // LRU_BLOCK_W
    nd, nb, bw, _ = w.shape
    wr = w.reshape(nd, nb // per, per, bw, bw)
    eye = jnp.eye(per, dtype=w.dtype)
    bd = wr[:, :, :, :, None, :] * eye[None, None, :, None, :, None]
    return bd.reshape(nd, nb // per, per * bw, per * bw).astype(BF16)


def _block_diag_grad(g):
    per = LRU_LANES // LRU_BLOCK_W
    nd, ng, _, _ = g.shape
    gr = g.reshape(nd, ng, per, LRU_BLOCK_W, per, LRU_BLOCK_W)
    diag = jnp.stack([gr[:, :, k, :, k, :] for k in range(per)], axis=2)
    return diag.reshape(nd, ng * per, LRU_BLOCK_W, LRU_BLOCK_W)


def _mod_select(mod16, b, d):
    m3 = mod16.reshape(MOD_ROWS, 3, d)
    lat = m3[:b]
    ctx = jnp.broadcast_to(m3[b][None], (b, 3, d))
    sel = jnp.stack([ctx, lat], axis=1)
    return jnp.pad(sel, ((0, 0), (0, 0), (0, 5), (0, 0)))


def _layer_fwd(x3, c16, p, cos, sin, ctx_len):
    b, t, d = x3.shape
    off, n_cols = _layout(d)
    mod16 = _mod_fwd(c16, p["w_mod"], p["b_mod"])
    modsel = _mod_select(mod16, b, d)
    h = _norm_mod_fwd(x3, p["norm_g"], modsel, ctx_len)
    proj = _matmul(h.reshape(b * t, d), p["w_in"], bm=_pick(b * t, TOKEN_BLOCKS), bn=1024, bk=d, name="proj_fwd",
                   n_outer=True, b_lead=p["li"], out_dtype=BF16)
    proj3 = proj.reshape(b, t, n_cols)
    ya = _lru_fwd(proj3, off["uA"], p["conv_w"], p["conv_b"], p["wa_bd"], p["ba"], p["wx_bd"], p["bx"], p["lam"], ctx_len)
    qkv_b = _prep_fwd(proj3, off["qB"], off["kB"], d, cos, sin, p["gq"], p["gk"], use_norm=False)
    yb, lse_b = _attn_win_fwd(qkv_b, p["sink"], d, ctx_len)
    qkv_c = _prep_fwd(proj3, off["qC"], off["kC"], d, cos, sin, p["gq"], p["gk"], use_norm=True)
    yc, lse_c = _attn_dense_fwd(qkv_c, d, ctx_len)
    x_new, out3 = _merge_fwd(x3, ya, yb, yc, proj3, p["w_br"], p["w_out"], modsel, ctx_len)
    return x_new, (x3, modsel, h, proj3, ya, yb, yc, qkv_b, lse_b, qkv_c, lse_c, out3)


def _layer_bwd(dxn3, saved, c16, p, cos, sin, ctx_len):
    x3, modsel, h, proj3, ya, yb, yc, qkv_b, lse_b, qkv_c, lse_c, out3 = saved
    b, t, d = x3.shape
    off, n_cols = _layout(d)
    rows = b * t
    bk = _pick(rows, (2 * TOKEN_BLOCKS[0],) + TOKEN_BLOCKS)
    dproj3, dya, dyb, dyc, z4, dt4, mixb, doutb, gacc = _merge_bwd(dxn3, out3, ya, yb, yc, proj3, p["w_br"], p["w_out"],
                                                                   modsel, ctx_len)
    dw_br = jnp.stack([_matmul(z4.reshape(3, rows, d), dt4.reshape(3, rows, d), ta=True, bm=d, bn=d, bk=bk,
                               name="dw_branch", a_lead=n, b_lead=n) for n in range(3)])
    dw_out = _matmul(mixb.reshape(rows, d), doutb.reshape(rows, d), ta=True, bm=d, bn=d, bk=bk, name="dw_out")
    dproj3, vec, dwa, dwx = _lru_bwd(dproj3, proj3, off["uA"], dya, p["conv_w"], p["conv_b"], p["wa_bd"], p["ba"],
                                     p["wx_bd"], p["bx"], p["lam"], ctx_len)
    dq_b, dkt_b, dvt_b, dsk = _attn_win_bwd(qkv_b, p["sink"], yb, dyb, lse_b, d, ctx_len)
    dproj3, _ = _prep_bwd(dproj3, dq_b, dkt_b, dvt_b, proj3, off["qB"], off["kB"], d, cos, sin, p["gq"], p["gk"], False)
    dq_c, dkt_c, dvt_c = _attn_dense_bwd(qkv_c, yc, dyc, lse_c, d, ctx_len)
    dproj3, gqk = _prep_bwd(dproj3, dq_c, dkt_c, dvt_c, proj3, off["qC"], off["kC"], d, cos, sin, p["gq"], p["gk"], True)
    dproj2 = dproj3.reshape(rows, n_cols)
    dw_in = _matmul(h.reshape(rows, d), dproj2, ta=True, bm=d, bn=1024, bk=bk, name="dw_in")
    dh = _matmul(dproj2, p["w_in"], tb=True, bm=_pick(rows, TOKEN_BLOCKS), bn=d, bk=_pick(n_cols, (2560, 1024)),
                 name="dh", b_lead=p["li"])
    dx3, nacc = _norm_mod_bwd(dh.reshape(b, t, d), x3, p["norm_g"], modsel, dxn3, ctx_len)
    per = jnp.stack([nacc[:, :, 0], nacc[:, :, 1], gacc[:, :, 0]], axis=2)
    dmod = jnp.concatenate([per[:, 1].reshape(b, 3 * d), jnp.sum(per[:, 0], axis=0).reshape(1, 3 * d)], axis=0)
    dmod16 = jnp.pad(dmod, ((0, MOD_ROWS - b - 1), (0, 0)))
    dw_mod, db_mod, dc16 = _mod_bwd(c16, dmod16, p["w_mod"])
    vsum = jnp.sum(vec, axis=0)
    grads = {
        "norm_g": jnp.sum(nacc[:, :, 2], axis=(0, 1)),
        "w_mod": dw_mod, "b_mod": db_mod[0], "w_in": dw_in,
        "conv_w": vsum[0:4], "conv_b": vsum[4],
        "lru_wa": _block_diag_grad(jnp.sum(dwa, axis=0)), "lru_ba": vsum[5:7],
        "lru_wx": _block_diag_grad(jnp.sum(dwx, axis=0)), "lru_bx": vsum[7:9],
        "lru_lambda": vsum[9:11] * (-jax.nn.sigmoid(-p["lam"])),
        "attn_sink": jnp.sum(dsk[:, :, 0:GROUP, 0], axis=0).reshape(-1),
        "q_norm_g": jnp.sum(gqk[:, 0], axis=0), "k_norm_g": jnp.sum(gqk[:, 1], axis=0),
        "w_branch": dw_br, "w_out": dw_out,
    }
    return dx3, dc16, grads


def _reorder_in_cols(w, d, inverse=False):
    off_new, _ = _layout(d)
    segs = _orig_segments(d)
    if inverse:
        return jnp.concatenate([w[..., off_new[n]:off_new[n] + wd] for n, _, wd in segs], axis=-1)
    by_name = {n: (o, wd) for n, o, wd in segs}
    order = sorted(off_new, key=off_new.get)
    return jnp.concatenate([w[..., by_name[n][0]:by_name[n][0] + by_name[n][1]] for n in order], axis=-1)


def _layer_params(li, w, g=None, lj=None):
    g = w if g is None else g
    lj = li if lj is None else lj
    return {
        "li": lj, "norm_g": w["norm_g"][li][None], "w_mod": g["w_mod"][lj], "b_mod": w["b_mod"][li][None],
        "w_in": g["w_in_r"],
        "conv_w": g["conv_w"][lj], "conv_b": w["conv_b"][li][None],
        "wa_bd": _block_diag(w["lru_wa"][li]), "ba": g["lru_ba"][lj],
        "wx_bd": _block_diag(w["lru_wx"][li]), "bx": g["lru_bx"][lj], "lam": g["lru_lambda"][lj],
        "sink": w["attn_sink"][li], "gq": w["q_norm_g"][li][None], "gk": w["k_norm_g"][li][None],
        "w_br": g["w_branch"][lj], "w_out": g["w_out"][lj],
    }


def _local_step(x, c, ctx, target, c_ctx, final_g, layers):
    b, s, d = x.shape
    ctx_len = ctx.shape[1]
    cos, sin = _rope_tables(ctx_len, s)
    x3 = jnp.concatenate([ctx, x], axis=1)
    c16 = jnp.concatenate([c, c_ctx[None], jnp.zeros((MOD_ROWS - b - 1, d), F32)], axis=0)
    saved = []
    layers = list(layers)
    for li, p in enumerate(layers):
        if callable(p):
            p = layers[li] = p(x3)
        x3, sv = _layer_fwd(x3, c16, p, cos, sin, ctx_len)
        saved.append(sv)
    dx3, loss_acc, dgf = _final(x3, final_g[None], target, ctx_len)
    grads = [None] * len(layers)
    dc_ctx = jnp.zeros((d,), F32)
    for li in reversed(range(len(layers))):
        dx3, dc16, grads[li] = _layer_bwd(dx3, saved[li], c16, layers[li], cos, sin, ctx_len)
        dc_ctx = dc_ctx + dc16[b]
    return jnp.sum(loss_acc[:, 0, 0]), dx3[:, ctx_len:], dc_ctx, jnp.sum(dgf[:, 0], axis=0), grads


N_CHIPS = 4
ANY = pl.BlockSpec(memory_space=pl.ANY)


def _place():
    x, y, c = lax.axis_index("x"), lax.axis_index("y"), lax.axis_index("c")
    return x, y, c, [(1 - x, y), (x, 1 - y), (1 - x, 1 - y)]


def _axis_part(ref, axis, start, size):
    idx = [slice(None)] * len(ref.shape)
    idx[axis] = pl.ds(start, size)
    return ref.at[tuple(idx)]


def _remote(src, dst, send, recv, dev):
    return pltpu.make_async_remote_copy(src_ref=src, dst_ref=dst, send_sem=send, recv_sem=recv, device_id=dev,
                                        device_id_type=MESH)


COPY_PIECES = 8


def _pieces(src, dst):
    shape = src.shape
    for ax in range(len(shape) - 1):
        if shape[ax] % COPY_PIECES == 0 and shape[ax] // COPY_PIECES >= 8:
            sz = shape[ax] // COPY_PIECES
            return [(_axis_part(src, ax, j * sz, sz), _axis_part(dst, ax, j * sz, sz)) for j in range(COPY_PIECES)]
    return [(src, dst)]


def _gather_chips(wholes, axes, name):
    n = len(wholes)

    def body(*refs):
        bufs = refs[n:2 * n]
        send, recv, fsend, frecv = refs[2 * n:]
        x, y, c, chips = _place()
        me = 2 * x + y
        sib = (x, y, 1 - c)

        def block(i, chip_index, half):
            sz = wholes[i].shape[axes[i]] // N_CHIPS
            hl = wholes[i].shape[0] // 2
            return _axis_part(bufs[i], axes[i], chip_index * sz, sz).at[pl.ds(half * hl, hl)]

        for i in range(n):
            for k, (px, py) in enumerate(chips):
                _remote(block(i, me, c), block(i, me, c), send.at[i, k], recv.at[i, k], (px, py, c)).start()
        for i in range(n):
            for k, (px, py) in enumerate(chips):
                landed = block(i, 2 * px + py, c)
                _remote(landed, landed, send.at[i, k], recv.at[i, k], (px, py, c)).wait_recv()
                for s_, d_ in _pieces(landed, landed):
                    _remote(s_, d_, fsend.at[i, k], frecv.at[i, k], sib).start()
        for i in range(n):
            for k, (px, py) in enumerate(chips):
                passed = _remote(block(i, 2 * px + py, c), block(i, 2 * px + py, 1 - c), fsend.at[i, k], frecv.at[i, k], sib)
                passed.wait_recv()
                passed.wait_send()
                _remote(block(i, me, c), block(i, me, c), send.at[i, k], recv.at[i, k], (px, py, c)).wait_send()

    sems = pltpu.SemaphoreType.DMA((n, 3))
    return pl.pallas_call(
        body, name=name, in_specs=[ANY] * n, out_specs=tuple([ANY] * n),
        out_shape=tuple(jax.ShapeDtypeStruct(a.shape, a.dtype) for a in wholes),
        input_output_aliases={i: i for i in range(n)},
        scratch_shapes=[sems, sems, sems, sems],
    )(*wholes)


HBM = pl.BlockSpec(memory_space=pltpu.HBM)
SEM = pl.BlockSpec(memory_space=pltpu.SEMAPHORE)
DATAFLOW = pltpu.SideEffectType.DATAFLOW_SIDE_EFFECTING


def _gather_start(wholes, axes, name):
    n = len(wholes)

    def body(*refs):
        bufs = refs[:n]
        send, recv, token = refs[2 * n:]
        x, y, c, chips = _place()
        me = 2 * x + y
        for i in range(n):
            sz = wholes[i].shape[axes[i]] // N_CHIPS
            mine = _axis_part(bufs[i], axes[i], me * sz, sz)
            for k, (px, py) in enumerate(chips):
                _remote(mine, mine, send.at[3 * i + k], recv.at[3 * i + k], (px, py, c)).start()
        token[...] = jnp.zeros_like(token)

    outs = pl.pallas_call(
        body, name=name, in_specs=[HBM] * n,
        out_specs=tuple([HBM] * n + [SEM, SEM, pl.BlockSpec(memory_space=pltpu.VMEM)]),
        out_shape=tuple(pltpu.HBM(a.shape, a.dtype) for a in wholes)
        + (pltpu.SemaphoreType.DMA((3 * n,)), pltpu.SemaphoreType.DMA((3 * n,)), jax.ShapeDtypeStruct((8, 128), F32)),
        input_output_aliases={i: i for i in range(n)},
        compiler_params=pltpu.CompilerParams(has_side_effects=DATAFLOW),
    )(*[pltpu.with_memory_space_constraint(a, pltpu.HBM) for a in wholes])
    return outs[:n], outs[n], outs[n + 1], outs[n + 2]


def _gather_wait(flying, send, recv, after, axes, name):
    n = len(flying)

    def body(*refs):
        bufs = refs[:n]
        send_sems, recv_sems = refs[n], refs[n + 1]
        x, y, c, chips = _place()
        me = 2 * x + y
        for i in range(n):
            sz = flying[i].shape[axes[i]] // N_CHIPS
            mine = _axis_part(bufs[i], axes[i], me * sz, sz)
            for k, (px, py) in enumerate(chips):
                theirs = _axis_part(bufs[i], axes[i], (2 * px + py) * sz, sz)
                cp = _remote(mine, theirs, send_sems.at[3 * i + k], recv_sems.at[3 * i + k], (px, py, c))
                cp.wait_send()
                cp.wait_recv()

    return pl.pallas_call(
        body, name=name, in_specs=[HBM] * n + [SEM, SEM, pl.BlockSpec(memory_space=pl.ANY)],
        out_specs=tuple([HBM] * n), out_shape=tuple(pltpu.HBM(a.shape, a.dtype) for a in flying),
        input_output_aliases={i: i for i in range(n)},
        compiler_params=pltpu.CompilerParams(has_side_effects=DATAFLOW),
    )(*flying, send, recv, after)


def _scatter_parts(ref, i_shape, axis, chip_index):
    sz = i_shape[axis] // N_CHIPS
    return _axis_part(ref, axis, chip_index * sz, sz)


def _scatter_start(pbs, axes, name):
    n = len(pbs)

    def block(p, ax):
        shape = list(p.shape)
        shape[ax] //= N_CHIPS
        return (3,) + tuple(shape)

    lands = [lax.empty(block(p, ax), p.dtype) for p, ax in zip(pbs, axes)]

    def body(*refs):
        src, land = refs[:n], refs[n:2 * n]
        send, recv, token = refs[4 * n:]
        x, y, c, chips = _place()
        for i in range(n):
            for k, (px, py) in enumerate(chips):
                _remote(_scatter_parts(src[i], pbs[i].shape, axes[i], 2 * px + py), land[i].at[k], send.at[3 * i + k],
                        recv.at[3 * i + k], (px, py, c)).start()
        token[...] = jnp.zeros_like(token)

    arrays = list(pbs) + lands
    outs = pl.pallas_call(
        body, name=name, in_specs=[HBM] * (2 * n),
        out_specs=tuple([HBM] * (2 * n) + [SEM, SEM, pl.BlockSpec(memory_space=pltpu.VMEM)]),
        out_shape=tuple(pltpu.HBM(a.shape, a.dtype) for a in arrays)
        + (pltpu.SemaphoreType.DMA((3 * n,)), pltpu.SemaphoreType.DMA((3 * n,)), jax.ShapeDtypeStruct((8, 128), F32)),
        input_output_aliases={i: i for i in range(2 * n)},
        compiler_params=pltpu.CompilerParams(has_side_effects=DATAFLOW),
    )(*[pltpu.with_memory_space_constraint(a, pltpu.HBM) for a in arrays])
    return outs[:n], outs[n:2 * n], outs[2 * n], outs[2 * n + 1], outs[2 * n + 2]


def _scatter_wait(sent, landing, send, recv, after, axes, name):
    n = len(sent)

    def body(*refs):
        src, land = refs[:n], refs[n:2 * n]
        send_sems, recv_sems = refs[2 * n], refs[2 * n + 1]
        x, y, c, chips = _place()
        for i in range(n):
            for k, (px, py) in enumerate(chips):
                cp = _remote(_scatter_parts(src[i], sent[i].shape, axes[i], 2 * px + py), land[i].at[k],
                             send_sems.at[3 * i + k], recv_sems.at[3 * i + k], (px, py, c))
                cp.wait_send()
                cp.wait_recv()

    arrays = list(sent) + list(landing)
    outs = pl.pallas_call(
        body, name=name, in_specs=[HBM] * (2 * n) + [SEM, SEM, pl.BlockSpec(memory_space=pl.ANY)],
        out_specs=tuple([HBM] * (2 * n)), out_shape=tuple(pltpu.HBM(a.shape, a.dtype) for a in arrays),
        input_output_aliases={i: i for i in range(2 * n)},
        compiler_params=pltpu.CompilerParams(has_side_effects=DATAFLOW),
    )(*arrays, send, recv, after)
    return outs[n:]


def _place_shard(shard, axis, dtype, name):
    cols_mode = axis == shard.ndim - 1
    assert cols_mode or axis == shard.ndim - 2
    sv = shard.reshape(-1, shard.shape[-2], shard.shape[-1])
    la, r, c = sv.shape
    br = _row_block(r, c)
    per = r // br

    def body(x_ref, o_ref):
        o_ref[...] = x_ref[...].astype(dtype)

    if cols_mode:
        out_shape, out_map = (la, r, N_CHIPS * c), lambda l, i: (l, i, _chip())
    else:
        out_shape, out_map = (la, N_CHIPS * r, c), lambda l, i: (l, _chip() * per + i, 0)
    out = pl.pallas_call(
        body, name=name, grid=(la, per),
        in_specs=[pl.BlockSpec((1, br, c), lambda l, i: (l, i, 0))], out_specs=pl.BlockSpec((1, br, c), out_map),
        out_shape=jax.ShapeDtypeStruct(out_shape, dtype), compiler_params=_cparams(("parallel", "parallel")))(sv)
    shape = list(shard.shape)
    shape[axis] *= N_CHIPS
    return out.reshape(tuple(shape))


def _split_cores(gs, name):
    n = len(gs)

    def body(*refs):
        ins, got = refs[:n], refs[n:2 * n]
        send, recv = refs[2 * n:]
        x, y, c, _ = _place()
        sib = (x, y, 1 - c)

        def theirs(i):
            hl = gs[i].shape[0] // 2
            return ins[i].at[pl.ds((1 - c) * hl, hl)]

        for i in range(n):
            for s_, d_ in _pieces(theirs(i), got[i]):
                _remote(s_, d_, send.at[i], recv.at[i], sib).start()
        for i in range(n):
            _remote(theirs(i), got[i], send.at[i], recv.at[i], sib).wait()

    return pl.pallas_call(
        body, name=name, in_specs=[ANY] * n, out_specs=tuple([ANY] * n),
        out_shape=tuple(jax.ShapeDtypeStruct((g.shape[0] // 2,) + g.shape[1:], g.dtype) for g in gs),
        scratch_shapes=[pltpu.SemaphoreType.DMA((n,)), pltpu.SemaphoreType.DMA((n,))],
    )(*gs)


def _scatter_chips(pbs, axes, name):
    n = len(pbs)

    def block(p, ax):
        shape = list(p.shape)
        shape[ax] //= N_CHIPS
        return tuple(shape)

    def body(*refs):
        inb, got = refs[:n], refs[n:2 * n]
        send, recv = refs[2 * n:]
        x, y, c, chips = _place()

        def part(i, chip_index):
            sz = pbs[i].shape[axes[i]] // N_CHIPS
            return _axis_part(inb[i], axes[i], chip_index * sz, sz)

        for i in range(n):
            for k, (px, py) in enumerate(chips):
                _remote(part(i, 2 * px + py), got[i].at[k], send.at[i, k], recv.at[i, k], (px, py, c)).start()
        for i in range(n):
            for k, (px, py) in enumerate(chips):
                _remote(part(i, 2 * px + py), got[i].at[k], send.at[i, k], recv.at[i, k], (px, py, c)).wait()

    return pl.pallas_call(
        body, name=name, in_specs=[ANY] * n, out_specs=tuple([ANY] * n),
        out_shape=tuple(jax.ShapeDtypeStruct((3,) + block(p, ax), p.dtype) for p, ax in zip(pbs, axes)),
        scratch_shapes=[pltpu.SemaphoreType.DMA((n, 3)), pltpu.SemaphoreType.DMA((n, 3))],
    )(*pbs)


def _join_cores(bufs, name):
    n = len(bufs)

    def body(*refs):
        outs = refs[n:2 * n]
        send, recv = refs[2 * n:]
        x, y, c, _ = _place()
        sib = (x, y, 1 - c)

        def half(i, which):
            hl = bufs[i].shape[0] // 2
            return outs[i].at[pl.ds(which * hl, hl)]

        for i in range(n):
            for s_, d_ in _pieces(half(i, c), half(i, c)):
                _remote(s_, d_, send.at[i], recv.at[i], sib).start()
        for i in range(n):
            cp = _remote(half(i, c), half(i, 1 - c), send.at[i], recv.at[i], sib)
            cp.wait_recv()
            cp.wait_send()

    return pl.pallas_call(
        body, name=name, in_specs=[ANY] * n, out_specs=tuple([ANY] * n),
        out_shape=tuple(jax.ShapeDtypeStruct(a.shape, a.dtype) for a in bufs),
        input_output_aliases={i: i for i in range(n)},
        scratch_shapes=[pltpu.SemaphoreType.DMA((n,)), pltpu.SemaphoreType.DMA((n,))],
    )(*bufs)


def _all_reduce_small(buf):
    r = buf.shape[0]

    def body(in_ref, out_ref, sib_buf, chip_sum, got, send, recv):
        x, y, c, chips = _place()
        cp = _remote(in_ref, sib_buf, send.at[0], recv.at[0], (x, y, 1 - c))
        cp.start()
        cp.wait()
        chip_sum[...] = in_ref[...] + sib_buf[...]
        cps = [_remote(chip_sum, got.at[k], send.at[1 + k], recv.at[1 + k], (px, py, c)) for k, (px, py) in enumerate(chips)]
        for cp in cps:
            cp.start()
        for cp in cps:
            cp.wait()
        out_ref[...] = (chip_sum[...] + got[0]) + (got[1] + got[2])

    return pl.pallas_call(
        body, name="all_reduce_small", out_shape=jax.ShapeDtypeStruct(buf.shape, F32),
        in_specs=[pl.BlockSpec(memory_space=pltpu.VMEM)], out_specs=pl.BlockSpec(memory_space=pltpu.VMEM),
        scratch_shapes=[pltpu.VMEM((r, 128), F32), pltpu.VMEM((r, 128), F32), pltpu.VMEM((3, r, 128), F32),
                        pltpu.SemaphoreType.DMA((4,)), pltpu.SemaphoreType.DMA((4,))],
        compiler_params=_cparams(None, VMEM_LIMIT),
    )(buf)


ELEMENTWISE_BLOCK_BYTES = 1 << 20


def _view2d(a):
    cols = a.shape[-1] if a.ndim > 1 else 128
    return a.reshape(-1, cols)


def _row_block(rows, cols):
    want = max(8, ELEMENTWISE_BLOCK_BYTES // (4 * cols))
    br = rows
    while br > want and br % 2 == 0 and (br // 2) % 16 == 0:
        br //= 2
    return br


def _core():
    return lax.axis_index("c")


def _chip():
    return 2 * lax.axis_index("x") + lax.axis_index("y")


def _sum_half(g, got, name):
    h = got.shape[0]
    gv = g.reshape(2 * h, -1, g.shape[-1])
    tv = got.reshape(h, -1, g.shape[-1])
    _, rows, cols = tv.shape
    br = _row_block(rows, cols)

    def body(g_ref, t_ref, p_ref, pb_ref):
        p = g_ref[...] + t_ref[...]
        p_ref[...] = p
        pb_ref[...] = p.astype(BF16)

    blk = pl.BlockSpec((1, br, cols), lambda l, i: (l, i, 0))
    p, pb = pl.pallas_call(
        body, name=name, grid=(h, rows // br),
        in_specs=[pl.BlockSpec((1, br, cols), lambda l, i: (_core() * h + l, i, 0)), blk], out_specs=(blk, blk),
        out_shape=(jax.ShapeDtypeStruct(tv.shape, F32), jax.ShapeDtypeStruct(tv.shape, BF16)),
        compiler_params=_cparams(("parallel", "parallel")))(gv, tv)
    return p.reshape(got.shape), pb.reshape(got.shape)


def _sum_blocks(p, got3, axis, name):
    h = p.shape[0]
    blk_shape = got3.shape[1:]
    cols_mode = axis == p.ndim - 1
    pv = p.reshape(-1, p.shape[-2], p.shape[-1])
    tv = got3.reshape(3, -1, blk_shape[-2], blk_shape[-1])
    la, rb, cb = tv.shape[1:]
    assert cols_mode or axis == p.ndim - 2
    br = _row_block(rb, cb)
    per = rb // br

    def body(p_ref, a_ref, b_ref, c_ref, out_ref):
        out_ref[...] = ((p_ref[...] + a_ref[0].astype(F32)) + b_ref[0].astype(F32)) + c_ref[0].astype(F32)

    if cols_mode:
        p_spec = pl.BlockSpec((1, br, cb), lambda l, i: (l, i, _chip()))
    else:
        p_spec = pl.BlockSpec((1, br, cb), lambda l, i: (l, _chip() * per + i, 0))
    out = pl.pallas_call(
        body, name=name, grid=(la, per),
        in_specs=[p_spec] + [pl.BlockSpec((1, 1, br, cb), lambda l, i, k=k: (k, l, i, 0)) for k in range(3)],
        out_specs=pl.BlockSpec((1, br, cb), lambda l, i: (_core() * la + l, i, 0)),
        out_shape=jax.ShapeDtypeStruct((2 * la, rb, cb), F32),
        compiler_params=_cparams(("parallel", "parallel")))(pv, tv, tv, tv)
    return out.reshape((2 * h,) + blk_shape[1:])


def _adamw(w, g, m, v, name):
    shape = w.shape
    ops = [_view2d(a) for a in (w, g, m, v)]
    rows, cols = ops[0].shape
    br = _row_block(rows, cols)
    c1 = 1.0 - ADAM_B1 ** ADAM_STEP
    c2 = 1.0 - ADAM_B2 ** ADAM_STEP

    def body(w_ref, g_ref, m_ref, v_ref, d_ref, nm_ref, nv_ref):
        g_ = g_ref[...]
        nm = ADAM_B1 * m_ref[...] + (1.0 - ADAM_B1) * g_
        nv = ADAM_B2 * v_ref[...] + (1.0 - ADAM_B2) * (g_ * g_)
        d_ref[...] = -ADAM_LR * ((nm / c1) / (jnp.sqrt(nv / c2) + ADAM_EPS) + ADAM_WD * w_ref[...])
        nm_ref[...] = nm
        nv_ref[...] = nv

    blk = pl.BlockSpec((br, cols), lambda i: (i, 0))
    outs = pl.pallas_call(body, name=name, grid=(rows // br,), in_specs=[blk] * 4, out_specs=(blk, blk, blk),
                          out_shape=tuple(jax.ShapeDtypeStruct((rows, cols), F32) for _ in range(3)),
                          compiler_params=_cparams(("parallel",)))(*ops)
    return tuple(o.reshape(shape) for o in outs)


PACK_ROWS = 2048


def _packed_rows(shape):
    return -(-int(np.prod(shape)) // (8 * 128)) * 8


def _pack(arrays):
    pieces = []
    for a in arrays:
        flat = a.reshape(-1)
        pieces.append(jnp.pad(flat, (0, _packed_rows(a.shape) * 128 - flat.shape[0])).reshape(-1, 128))
    rows = sum(p.shape[0] for p in pieces)
    pieces.append(jnp.zeros((-rows % PACK_ROWS, 128), F32))
    return jnp.concatenate(pieces, axis=0)


def _unpack(buf, shapes):
    out, o = [], 0
    for s in shapes:
        n, rows = int(np.prod(s)), _packed_rows(s)
        out.append(buf[o:o + rows].reshape(-1)[:n].reshape(s))
        o += rows
    return out


WEIGHTS = ["c_ctx", "norm_g", "w_mod", "b_mod", "w_in", "conv_w", "conv_b", "lru_wa", "lru_ba", "lru_wx", "lru_bx",
           "lru_lambda", "attn_sink", "q_norm_g", "k_norm_g", "w_branch", "w_out", "final_g"]
BIG = {"w_mod": 2, "w_in": 2, "w_branch": 2, "w_out": 1}
SMALL_SHARDED = ["conv_w", "lru_ba", "lru_bx", "lru_lambda"]
REPLICATED = [n for n in WEIGHTS if n not in BIG and n not in SMALL_SHARDED]


def kernel(x, c, ctx, c_ctx, norm_g, w_mod, b_mod, w_in, conv_w, conv_b, lru_wa, lru_ba, lru_wx, lru_bx, lru_lambda, attn_sink, q_norm_g, k_norm_g, w_branch, w_out, final_g, loss_target, m_c_ctx, m_norm_g, m_w_mod, m_b_mod, m_w_in, m_conv_w, m_conv_b, m_lru_wa, m_lru_ba, m_lru_wx, m_lru_bx, m_lru_lambda, m_attn_sink, m_q_norm_g, m_k_norm_g, m_w_branch, m_w_out, m_final_g, v_c_ctx, v_norm_g, v_w_mod, v_b_mod, v_w_in, v_conv_w, v_conv_b, v_lru_wa, v_lru_ba, v_lru_wx, v_lru_bx, v_lru_lambda, v_attn_sink, v_q_norm_g, v_k_norm_g, v_w_branch, v_w_out, v_final_g):
    args = dict(locals())
    w = {n: args[n] for n in WEIGHTS}
    mom = {n: args["m_" + n] for n in WEIGHTS}
    var = {n: args["v_" + n] for n in WEIGHTS}
    depth, d = norm_g.shape
    chip = 2 * lax.axis_index("x") + lax.axis_index("y")

    big_names = list(BIG)
    small_shard = jnp.concatenate([w[n] for n in SMALL_SHARDED], axis=1)
    gather_axes = [BIG[n] for n in big_names] + [2]
    first = depth // 2

    def placed(lo, hi, tag):
        out = [_place_shard(w[n][lo:hi], BIG[n], BF16, "place_" + tag + n) for n in big_names]
        return out + [_place_shard(small_shard[lo:hi], 2, F32, "place_" + tag + "small")]

    def named(gathered):
        g = dict(zip(big_names, gathered[:-1]))
        o = 0
        for n in SMALL_SHARDED:
            rows = w[n].shape[1]
            g[n] = gathered[-1][:, o:o + rows]
            o += rows
        g["w_in_r"] = _reorder_in_cols(g["w_in"], d)
        return g

    early = named(_gather_chips(placed(0, first, "first_"), gather_axes, "gather_weights_first"))
    flying, send_sems, recv_sems, token = _gather_start(placed(first, depth, "rest_"), gather_axes, "gather_rest_start")
    late = {}

    def late_layer(li):
        def params(x3):
            if not late:
                late.update(named(_gather_wait(flying, send_sems, recv_sems, x3, gather_axes, "gather_rest_wait")))
            return _layer_params(li, w, late, li - first)
        return params

    layers = [_layer_params(li, w, early, li) for li in range(first)] + [late_layer(li) for li in range(first, depth)]
    loss_local, grad_x, g_c_ctx, g_final, lgrads = _local_step(x, c + token[0, 0], ctx, loss_target, c_ctx, final_g, layers)
    loss = lax.psum(loss_local, ("x", "y", "c"))
    full = {n: jnp.stack([lg[n] for lg in lgrads]) for n in lgrads[0]}
    full["w_in"] = _reorder_in_cols(full["w_in"], d, inverse=True)
    full["c_ctx"], full["final_g"] = g_c_ctx, g_final

    bigs = [full[n] for n in big_names]
    got = _split_cores(bigs, "grad_split_cores")
    parts = [_sum_half(g, t_, "grad_chip_sum") for g, t_ in zip(bigs, got)]
    big_axes = [BIG[n] for n in big_names]
    sent, landing, send_sems, recv_sems, token = _scatter_start([pb for _, pb in parts], big_axes, "grad_scatter_start")

    small_names = REPLICATED + SMALL_SHARDED
    small_sum = _all_reduce_small(_pack([full[n] for n in small_names]) + token[0, 0])
    reduced = _unpack(small_sum, [full[n].shape for n in small_names])
    recv = _scatter_wait(sent, landing, send_sems, recv_sems, small_sum, big_axes, "grad_scatter_wait")
    totals = [_sum_blocks(p_, r, BIG[n], "grad_total") for (p_, _), r, n in zip(parts, recv, big_names)]
    grad = dict(zip(big_names, _join_cores(totals, "grad_join_cores")))
    for n, g in zip(small_names, reduced):
        if n in SMALL_SHARDED:
            sz = w[n].shape[-1]
            g = lax.dynamic_slice_in_dim(g, chip * sz, sz, axis=g.ndim - 1)
        grad[n] = g

    delta, new_m, new_v = {}, {}, {}
    for n in big_names:
        delta[n], new_m[n], new_v[n] = _adamw(w[n], grad[n], mom[n], var[n], "adamw_" + n)
    shapes = [w[n].shape for n in small_names]
    packed = _adamw(_pack([w[n] for n in small_names]), _pack([grad[n] for n in small_names]),
                    _pack([mom[n] for n in small_names]), _pack([var[n] for n in small_names]), "adamw_small")
    for res, p in zip((delta, new_m, new_v), packed):
        res.update(dict(zip(small_names, _unpack(p, shapes))))

    return (loss, grad_x, *[grad[n] for n in WEIGHTS], *[delta[n] for n in WEIGHTS],
            *[new_m[n] for n in WEIGHTS], *[new_v[n] for n in WEIGHTS])
```
